```python
import jax, jax.numpy as jnp
from jax import lax
import numpy as np

D_MODEL = 1024
BATCH = 32
SEQ = 2048
DEPTH = 1

N_META = 16
D_MIX = D_MODEL
D_CONV = D_MIX // 2
D_POOL = D_MIX - D_CONV
CONV_HEADS = 8
CONV_WIDTH = 3
POOL_WINDOWS = (2, 4, 8, 16)
N_POOL_GROUPS = len(POOL_WINDOWS)
POOL_GROUP = D_POOL // N_POOL_GROUPS
D_IN_PROJ = 3 * D_CONV + D_POOL
D_FF = ((int(np.ceil(8 * D_MODEL / 3)) + 255) // 256) * 256
RMS_EPS = 1e-6

kernel_name = "hymba_conv_pool_hybrid_block"


def rms_norm(x, g):
    xf = x.astype(jnp.float32)
    y = xf * lax.rsqrt(jnp.mean(xf * xf, axis=-1, keepdims=True) + RMS_EPS)
    return (y * g.astype(jnp.float32)).astype(x.dtype)


def causal_short_conv(u, w):
    k_width = w.shape[0]
    seq_len = u.shape[1]
    up = jnp.pad(u, ((0, 0), (k_width - 1, 0), (0, 0)))
    y = w[0] * up[:, 0:seq_len]
    for k in range(1, k_width):
        y = y + w[k] * up[:, k:k + seq_len]
    return y


def multiscale_pool(u, pool_w, pool_scale):
    bsz, seq_len, _ = u.shape
    ug = u.reshape(bsz, seq_len, N_POOL_GROUPS, POOL_GROUP)
    pos = jnp.arange(seq_len)
    outs = []
    for g, win in enumerate(POOL_WINDOWS):
        xg = ug[:, :, g].astype(jnp.float32)
        cs = jnp.cumsum(xg, axis=1)
        cs_prev = jnp.pad(cs, ((0, 0), (win, 0), (0, 0)))[:, :seq_len]
        cnt = jnp.minimum(pos + 1, win).astype(jnp.float32)[None, :, None]
        outs.append((cs - cs_prev) / cnt - xg)
    pooled = jnp.stack(outs, axis=2).astype(u.dtype)
    mixed = jnp.einsum('blgc,gcd->blgd', pooled, pool_w)
    return mixed.reshape(bsz, seq_len, D_POOL) * pool_scale


def _fwd_setup_inputs(seed: int = 0) -> dict:
    key = jax.random.key(seed)
    ks = jax.random.split(key, 16)
    f32 = jnp.float32

    def nrm(k, shape, scale):
        return jax.random.normal(k, shape, f32) * scale

    def gain(k):
        return 1.0 + 0.05 * jax.random.normal(k, (DEPTH, D_MODEL), f32)

    return {
        "x": jax.random.normal(ks[0], (BATCH, SEQ, D_MODEL), f32),
        "meta_tokens": nrm(ks[1], (N_META, D_MODEL), 1.0),
        "norm_mix_pre": gain(ks[2]),
        "w_in": nrm(ks[3], (DEPTH, D_MODEL, D_IN_PROJ), D_MODEL ** -0.5),
        "conv_w": nrm(ks[4], (DEPTH, CONV_WIDTH, D_CONV), CONV_WIDTH ** -0.5),
        "pool_w": nrm(ks[5], (DEPTH, N_POOL_GROUPS, POOL_GROUP, POOL_GROUP), POOL_GROUP ** -0.5),
        "pool_scale": 1.0 + 0.1 * jax.random.normal(ks[6], (DEPTH, D_POOL), f32),
        "w_out": nrm(ks[7], (DEPTH, D_MIX, D_MODEL), D_MIX ** -0.5),
        "norm_mix_post": gain(ks[8]),
        "norm_ffn_pre": gain(ks[9]),
        "w_gate": nrm(ks[10], (DEPTH, D_MODEL, D_FF), D_MODEL ** -0.5),
        "w_up": nrm(ks[11], (DEPTH, D_MODEL, D_FF), D_MODEL ** -0.5),
        "w_down": nrm(ks[12], (DEPTH, D_FF, D_MODEL), D_FF ** -0.5),
        "norm_ffn_post": gain(ks[13]),
    }


def _fwd_reference(x, meta_tokens, norm_mix_pre, w_in, conv_w, pool_w, pool_scale, w_out,
              norm_mix_post, norm_ffn_pre, w_gate, w_up, w_down, norm_ffn_post):
    bsz = x.shape[0]
    meta = jnp.broadcast_to(meta_tokens[None].astype(x.dtype), (bsz, N_META, D_MODEL))
    h = jnp.concatenate([meta, x], axis=1)

    for i in range(DEPTH):
        a = rms_norm(h, norm_mix_pre[i])
        z = a @ w_in[i]
        b_gate = z[..., 0:D_CONV]
        c_gate = z[..., D_CONV:2 * D_CONV]
        v = z[..., 2 * D_CONV:3 * D_CONV]
        p = z[..., 3 * D_CONV:]
        y_conv = b_gate * causal_short_conv(c_gate * v, conv_w[i])
        y_pool = multiscale_pool(p, pool_w[i], pool_scale[i])
        m = jnp.concatenate([y_conv, y_pool], axis=-1) @ w_out[i]
        h = h + rms_norm(m, norm_mix_post[i])

        f = rms_norm(h, norm_ffn_pre[i])
        g = jax.nn.silu(f @ w_gate[i]) * (f @ w_up[i])
        h = h + rms_norm(g @ w_down[i], norm_ffn_post[i])

    return h[:, N_META:]


import jax as _jax
import jax.numpy as _jnp

TWIN_FORMAT = 'train_step'
FWD_PARAMS = ['x', 'meta_tokens', 'norm_mix_pre', 'w_in', 'conv_w', 'pool_w', 'pool_scale', 'w_out', 'norm_mix_post', 'norm_ffn_pre', 'w_gate', 'w_up', 'w_down', 'norm_ffn_post']
TWIN_WEIGHTS = ['meta_tokens', 'norm_mix_pre', 'w_in', 'conv_w', 'pool_w', 'pool_scale', 'w_out', 'norm_mix_post', 'norm_ffn_pre', 'w_gate', 'w_up', 'w_down', 'norm_ffn_post']
TWIN_DIFF_INPUT = 'x'
TWIN_INPUTS = ['x', 'meta_tokens', 'norm_mix_pre', 'w_in', 'conv_w', 'pool_w', 'pool_scale', 'w_out', 'norm_mix_post', 'norm_ffn_pre', 'w_gate', 'w_up', 'w_down', 'norm_ffn_post', 'loss_target', 'm_meta_tokens', 'm_norm_mix_pre', 'm_w_in', 'm_conv_w', 'm_pool_w', 'm_pool_scale', 'm_w_out', 'm_norm_mix_post', 'm_norm_ffn_pre', 'm_w_gate', 'm_w_up', 'm_w_down', 'm_norm_ffn_post', 'v_meta_tokens', 'v_norm_mix_pre', 'v_w_in', 'v_conv_w', 'v_pool_w', 'v_pool_scale', 'v_w_out', 'v_norm_mix_post', 'v_norm_ffn_pre', 'v_w_gate', 'v_w_up', 'v_w_down', 'v_norm_ffn_post']
TWIN_OUTPUTS = ['loss', 'grad_x', 'grad_meta_tokens', 'grad_norm_mix_pre', 'grad_w_in', 'grad_conv_w', 'grad_pool_w', 'grad_pool_scale', 'grad_w_out', 'grad_norm_mix_post', 'grad_norm_ffn_pre', 'grad_w_gate', 'grad_w_up', 'grad_w_down', 'grad_norm_ffn_post', 'delta_meta_tokens', 'delta_norm_mix_pre', 'delta_w_in', 'delta_conv_w', 'delta_pool_w', 'delta_pool_scale', 'delta_w_out', 'delta_norm_mix_post', 'delta_norm_ffn_pre', 'delta_w_gate', 'delta_w_up', 'delta_w_down', 'delta_norm_ffn_post', 'new_m_meta_tokens', 'new_m_norm_mix_pre', 'new_m_w_in', 'new_m_conv_w', 'new_m_pool_w', 'new_m_pool_scale', 'new_m_w_out', 'new_m_norm_mix_post', 'new_m_norm_ffn_pre', 'new_m_w_gate', 'new_m_w_up', 'new_m_w_down', 'new_m_norm_ffn_post', 'new_v_meta_tokens', 'new_v_norm_mix_pre', 'new_v_w_in', 'new_v_conv_w', 'new_v_pool_w', 'new_v_pool_scale', 'new_v_w_out', 'new_v_norm_mix_post', 'new_v_norm_ffn_pre', 'new_v_w_gate', 'new_v_w_up', 'new_v_w_down', 'new_v_norm_ffn_post']
TWIN_LEAF_KINDS = {'loss': 'loss', 'grad_x': 'grad_x', 'grad_meta_tokens': 'grad_w', 'grad_norm_mix_pre': 'grad_w', 'grad_w_in': 'grad_w', 'grad_conv_w': 'grad_w', 'grad_pool_w': 'grad_w', 'grad_pool_scale': 'grad_w', 'grad_w_out': 'grad_w', 'grad_norm_mix_post': 'grad_w', 'grad_norm_ffn_pre': 'grad_w', 'grad_w_gate': 'grad_w', 'grad_w_up': 'grad_w', 'grad_w_down': 'grad_w', 'grad_norm_ffn_post': 'grad_w', 'delta_meta_tokens': 'delta_w', 'delta_norm_mix_pre': 'delta_w', 'delta_w_in': 'delta_w', 'delta_conv_w': 'delta_w', 'delta_pool_w': 'delta_w', 'delta_pool_scale': 'delta_w', 'delta_w_out': 'delta_w', 'delta_norm_mix_post': 'delta_w', 'delta_norm_ffn_pre': 'delta_w', 'delta_w_gate': 'delta_w', 'delta_w_up': 'delta_w', 'delta_w_down': 'delta_w', 'delta_norm_ffn_post': 'delta_w', 'new_m_meta_tokens': 'new_m', 'new_m_norm_mix_pre': 'new_m', 'new_m_w_in': 'new_m', 'new_m_conv_w': 'new_m', 'new_m_pool_w': 'new_m', 'new_m_pool_scale': 'new_m', 'new_m_w_out': 'new_m', 'new_m_norm_mix_post': 'new_m', 'new_m_norm_ffn_pre': 'new_m', 'new_m_w_gate': 'new_m', 'new_m_w_up': 'new_m', 'new_m_w_down': 'new_m', 'new_m_norm_ffn_post': 'new_m', 'new_v_meta_tokens': 'new_v', 'new_v_norm_mix_pre': 'new_v', 'new_v_w_in': 'new_v', 'new_v_conv_w': 'new_v', 'new_v_pool_w': 'new_v', 'new_v_pool_scale': 'new_v', 'new_v_w_out': 'new_v', 'new_v_norm_mix_post': 'new_v', 'new_v_norm_ffn_pre': 'new_v', 'new_v_w_gate': 'new_v', 'new_v_w_up': 'new_v', 'new_v_w_down': 'new_v', 'new_v_norm_ffn_post': 'new_v'}


def _forward(args):
    return _fwd_reference(*[args[k] for k in FWD_PARAMS])


def _output_shape():
    out = _jax.eval_shape(lambda: _forward(_fwd_setup_inputs(0)))
    return out.shape, out.dtype

N_MICROBATCH = 1
ADAM_LR = 0.001
ADAM_B1 = 0.9
ADAM_B2 = 0.999
ADAM_EPS = 1e-08
ADAM_WD = 0.01
ADAM_STEP = 10
PER_EXAMPLE_BATCH_AXIS = {'x': 0, 'loss_target': 0}
SHARED_INPUTS = []
_WEIGHT_DTYPES = {'meta_tokens': _jnp.float32, 'norm_mix_pre': _jnp.float32, 'w_in': _jnp.float32, 'conv_w': _jnp.float32, 'pool_w': _jnp.float32, 'pool_scale': _jnp.float32, 'w_out': _jnp.float32, 'norm_mix_post': _jnp.float32, 'norm_ffn_pre': _jnp.float32, 'w_gate': _jnp.float32, 'w_up': _jnp.float32, 'w_down': _jnp.float32, 'norm_ffn_post': _jnp.float32}
MOMENT_SCALE = {'meta_tokens': 1.077984e-02, 'norm_mix_pre': 1.155722e+00, 'w_in': 7.509167e-01, 'conv_w': 6.942598e-01, 'pool_w': 1.119133e+00, 'pool_scale': 1.259731e+00, 'w_out': 9.633250e-01, 'norm_mix_post': 6.369818e+01, 'norm_ffn_pre': 8.109324e-01, 'w_gate': 2.742134e-01, 'w_up': 4.417794e-01, 'w_down': 7.351291e-01, 'norm_ffn_post': 6.384628e+01}


def _to_microbatches(a, axis):
    t = _jnp.moveaxis(a, axis, 0)
    t = t.reshape((N_MICROBATCH, t.shape[0] // N_MICROBATCH) + t.shape[1:])
    return _jnp.moveaxis(t, 1, axis + 1)


def setup_inputs(seed: int = 0) -> dict:
    inp = _fwd_setup_inputs(seed)
    key = _jax.random.fold_in(_jax.random.key(seed), 7919)
    shape, _ = _output_shape()
    out = dict(inp)
    out["loss_target"] = _jax.random.normal(_jax.random.fold_in(key, 0), shape, _jnp.float32)
    for i, name in enumerate(TWIN_WEIGHTS):
        w = inp[name].astype(_jnp.float32)
        if MOMENT_SCALE is None:
            s = _jnp.sqrt(_jnp.mean(_jnp.square(w)) + 1e-30)
        else:
            s = MOMENT_SCALE[name]
        km, kv = _jax.random.split(_jax.random.fold_in(key, i + 1))
        out[name] = w
        out["m_" + name] = s * _jax.random.normal(km, w.shape, _jnp.float32)
        out["v_" + name] = (s * s) * _jax.random.uniform(kv, w.shape, _jnp.float32, 0.5, 1.5)
    if N_MICROBATCH > 1:
        for name, axis in PER_EXAMPLE_BATCH_AXIS.items():
            out[name] = _to_microbatches(out[name], axis)
    return {'x': out['x'], 'meta_tokens': out['meta_tokens'], 'norm_mix_pre': out['norm_mix_pre'], 'w_in': out['w_in'], 'conv_w': out['conv_w'], 'pool_w': out['pool_w'], 'pool_scale': out['pool_scale'], 'w_out': out['w_out'], 'norm_mix_post': out['norm_mix_post'], 'norm_ffn_pre': out['norm_ffn_pre'], 'w_gate': out['w_gate'], 'w_up': out['w_up'], 'w_down': out['w_down'], 'norm_ffn_post': out['norm_ffn_post'], 'loss_target': out['loss_target'], 'm_meta_tokens': out['m_meta_tokens'], 'm_norm_mix_pre': out['m_norm_mix_pre'], 'm_w_in': out['m_w_in'], 'm_conv_w': out['m_conv_w'], 'm_pool_w': out['m_pool_w'], 'm_pool_scale': out['m_pool_scale'], 'm_w_out': out['m_w_out'], 'm_norm_mix_post': out['m_norm_mix_post'], 'm_norm_ffn_pre': out['m_norm_ffn_pre'], 'm_w_gate': out['m_w_gate'], 'm_w_up': out['m_w_up'], 'm_w_down': out['m_w_down'], 'm_norm_ffn_post': out['m_norm_ffn_post'], 'v_meta_tokens': out['v_meta_tokens'], 'v_norm_mix_pre': out['v_norm_mix_pre'], 'v_w_in': out['v_w_in'], 'v_conv_w': out['v_conv_w'], 'v_pool_w': out['v_pool_w'], 'v_pool_scale': out['v_pool_scale'], 'v_w_out': out['v_w_out'], 'v_norm_mix_post': out['v_norm_mix_post'], 'v_norm_ffn_pre': out['v_norm_ffn_pre'], 'v_w_gate': out['v_w_gate'], 'v_w_up': out['v_w_up'], 'v_w_down': out['v_w_down'], 'v_norm_ffn_post': out['v_norm_ffn_post']}


def _loss(weights, diff, rest, loss_target):
    with _jax.named_scope("forward"):
        args = {**rest, TWIN_DIFF_INPUT: diff, **{k: w.astype(_WEIGHT_DTYPES[k]) for k, w in weights.items()}}
        y = _forward(args)
    with _jax.named_scope("loss_head"):
        err = _jnp.square(y.astype(_jnp.float32) - loss_target)
        return 0.5 * _jnp.sum(_jnp.mean(err, axis=-1)) if err.ndim else 0.5 * err


def _adamw(w, g, m, v):
    m = ADAM_B1 * m + (1.0 - ADAM_B1) * g
    v = ADAM_B2 * v + (1.0 - ADAM_B2) * _jnp.square(g)
    m_hat = m / (1.0 - ADAM_B1 ** ADAM_STEP)
    v_hat = v / (1.0 - ADAM_B2 ** ADAM_STEP)
    delta = -ADAM_LR * (m_hat / (_jnp.sqrt(v_hat) + ADAM_EPS) + ADAM_WD * w)
    return delta, m, v


def reference(x, meta_tokens, norm_mix_pre, w_in, conv_w, pool_w, pool_scale, w_out, norm_mix_post, norm_ffn_pre, w_gate, w_up, w_down, norm_ffn_post, loss_target, m_meta_tokens, m_norm_mix_pre, m_w_in, m_conv_w, m_pool_w, m_pool_scale, m_w_out, m_norm_mix_post, m_norm_ffn_pre, m_w_gate, m_w_up, m_w_down, m_norm_ffn_post, v_meta_tokens, v_norm_mix_pre, v_w_in, v_conv_w, v_pool_w, v_pool_scale, v_w_out, v_norm_mix_post, v_norm_ffn_pre, v_w_gate, v_w_up, v_w_down, v_norm_ffn_post):
    given = dict(x=x, meta_tokens=meta_tokens, norm_mix_pre=norm_mix_pre, w_in=w_in, conv_w=conv_w, pool_w=pool_w, pool_scale=pool_scale, w_out=w_out, norm_mix_post=norm_mix_post, norm_ffn_pre=norm_ffn_pre, w_gate=w_gate, w_up=w_up, w_down=w_down, norm_ffn_post=norm_ffn_post, loss_target=loss_target, m_meta_tokens=m_meta_tokens, m_norm_mix_pre=m_norm_mix_pre, m_w_in=m_w_in, m_conv_w=m_conv_w, m_pool_w=m_pool_w, m_pool_scale=m_pool_scale, m_w_out=m_w_out, m_norm_mix_post=m_norm_mix_post, m_norm_ffn_pre=m_norm_ffn_pre, m_w_gate=m_w_gate, m_w_up=m_w_up, m_w_down=m_w_down, m_norm_ffn_post=m_norm_ffn_post, v_meta_tokens=v_meta_tokens, v_norm_mix_pre=v_norm_mix_pre, v_w_in=v_w_in, v_conv_w=v_conv_w, v_pool_w=v_pool_w, v_pool_scale=v_pool_scale, v_w_out=v_w_out, v_norm_mix_post=v_norm_mix_post, v_norm_ffn_pre=v_norm_ffn_pre, v_w_gate=v_w_gate, v_w_up=v_w_up, v_w_down=v_w_down, v_norm_ffn_post=v_norm_ffn_post)
    weights = {n: given[n] for n in TWIN_WEIGHTS}
    shared = {n: given[n] for n in SHARED_INPUTS}
    per_example = {n: given[n] for n in ['x']}
    grad_fn = _jax.value_and_grad(_loss, argnums=(0, 1))

    def one_microbatch(ex, loss_target):
        ex = dict(ex)
        diff = ex.pop(TWIN_DIFF_INPUT)
        return grad_fn(weights, diff, {**shared, **ex}, loss_target)

    if N_MICROBATCH == 1:
        loss, (grad_w, grad_x) = one_microbatch(per_example, given["loss_target"])
    else:
        def body(carry, xs):
            loss_sum, grad_sum = carry
            l_k, (gw_k, gx_k) = one_microbatch(xs[0], xs[1])
            with _jax.named_scope("update"):
                return (loss_sum + l_k, _jax.tree.map(_jnp.add, grad_sum, gw_k)), gx_k

        init = (_jnp.zeros((), _jnp.float32), _jax.tree.map(_jnp.zeros_like, weights))
        (loss, grad_w), grad_x = _jax.lax.scan(body, init, (per_example, given["loss_target"]))
    with _jax.named_scope("update"):
        delta_w, new_m, new_v = {}, {}, {}
        for n in TWIN_WEIGHTS:
            delta_w[n], new_m[n], new_v[n] = _adamw(weights[n], grad_w[n], given["m_" + n], given["v_" + n])
    return (loss, grad_x, *[grad_w[n] for n in TWIN_WEIGHTS], *[delta_w[n] for n in TWIN_WEIGHTS],
            *[new_m[n] for n in TWIN_WEIGHTS], *[new_v[n] for n in TWIN_WEIGHTS])
```

```python
import functools

import jax
import jax.numpy as jnp
from jax import lax
from jax.experimental import pallas as pl
from jax.experimental.pallas import tpu as pltpu

F32, BF16 = jnp.float32, jnp.bfloat16
RMS_EPS = 1e-6
N_META = 16
CONV_WIDTH = 3
POOL_WINDOWS = (2, 4, 8, 16)
POOL_GROUP = 128
HALO = 16
N_DEV = 8
MESH_AXES = ("x", "y", "c")
MESH = pl.DeviceIdType.MESH
VMEM_LIMIT_BYTES = 56 * 1024 * 1024
ADAMW_BLOCK_ELEMS = 64 * 1024
TM_MIX = 512
TM_FFN = 256
FF_CHUNKS = 2

ADAM_LR, ADAM_B1, ADAM_B2, ADAM_EPS, ADAM_WD, ADAM_STEP = 0.001, 0.9, 0.999, 1e-08, 0.01, 10


def _dot(a, b):
    return jnp.dot(a, b, preferred_element_type=F32)


def _dot_nt(a, b):
    return lax.dot_general(a, b, (((1,), (1,)), ((), ())), preferred_element_type=F32)


def _dot_tn(a, b):
    return lax.dot_general(a, b, (((0,), (0,)), ((), ())), preferred_element_type=F32)


def _rms_stats(h):
    rstd = lax.rsqrt(jnp.mean(h * h, axis=-1, keepdims=True) + RMS_EPS)
    return h * rstd, rstd


def _rms_bwd(hat, rstd, g, dy):
    gdy = dy * g
    proj = jnp.mean(gdy * hat, axis=-1, keepdims=True)
    return rstd * (gdy - hat * proj), jnp.sum(dy * hat, axis=0, keepdims=True)


def _params(*semantics):
    return pltpu.CompilerParams(dimension_semantics=semantics or None, vmem_limit_bytes=VMEM_LIMIT_BYTES)


def _resident(shape):
    zeros = (0,) * len(shape)
    return pl.BlockSpec(shape, lambda *_: zeros, pipeline_mode=pl.Buffered(1))


def _const(shape):
    zeros = (0,) * len(shape)
    return pl.BlockSpec(shape, lambda *_: zeros)


ANY = pl.BlockSpec(memory_space=pl.ANY)


def _my_place():
    x, y, c = (lax.axis_index(a) for a in MESH_AXES)
    return x, y, c


def _all_gather(shards, out_dtypes, name):
    n = len(shards)

    def body(*refs):
        ins, outs, stages = refs[:n], refs[n:2 * n], refs[2 * n:3 * n]
        send_sems, recv_sems, local_sems = refs[3 * n:]
        x, y, c = _my_place()
        sibling = (x, y, 1 - c)
        chips = [(1 - x, y), (x, 1 - y), (1 - x, 1 - y)]

        def slab(px, py, pc):
            return 4 * px + 2 * py + pc

        def copy(a, k, block, to, src=None):
            dst = outs[a].at[slab(*block)]
            return pltpu.make_async_remote_copy(
                src_ref=dst if src is None else src, dst_ref=dst, send_sem=send_sems.at[a, k], recv_sem=recv_sems.at[a, k],
                device_id=to, device_id_type=MESH)

        me = (x, y, c)
        mine, first, passed = [], [], []
        for a in range(n):
            stages[a][...] = ins[a][...].astype(stages[a].dtype)
            mine.append(pltpu.make_async_copy(stages[a], outs[a].at[slab(*me)], local_sems.at[a]))
            mine[-1].start()
            first.append(copy(a, 0, me, sibling, src=stages[a]))
            first += [copy(a, 1 + j, me, (*chip, c), src=stages[a]) for j, chip in enumerate(chips)]
        for cp in first:
            cp.start()
        for j, chip in enumerate(chips):
            for a in range(n):
                copy(a, 1 + j, (*chip, c), me).wait_recv()
                passed.append(copy(a, 4 + j, (*chip, c), sibling))
                passed[-1].start()
        for a in range(n):
            copy(a, 0, sibling, me).wait_recv()
            for j, chip in enumerate(chips):
                copy(a, 4 + j, (*chip, 1 - c), me).wait_recv()
        for cp in first + passed:
            cp.wait_send()
        for cp in mine:
            cp.wait()

    return pl.pallas_call(
        body, name=name,
        out_shape=[jax.ShapeDtypeStruct((N_DEV, *s.shape), d) for s, d in zip(shards, out_dtypes)],
        in_specs=[pl.BlockSpec(memory_space=pltpu.VMEM)] * n,
        out_specs=[ANY] * n,
        scratch_shapes=[pltpu.VMEM(s.shape, d) for s, d in zip(shards, out_dtypes)]
        + [pltpu.SemaphoreType.DMA((n, 7)), pltpu.SemaphoreType.DMA((n, 7)), pltpu.SemaphoreType.DMA((n,))],
        compiler_params=pltpu.CompilerParams(vmem_limit_bytes=VMEM_LIMIT_BYTES),
    )(*shards)


def _all_to_all(arrays, name):
    n = len(arrays)

    def body(*refs):
        ins, outs = refs[:n], refs[n:2 * n]
        send_sems, recv_sems, local_sems = refs[2 * n:]
        x, y, c = _my_place()
        me = 4 * x + 2 * y + c
        sends, mine = [], []
        for a in range(n):
            mine.append(pltpu.make_async_copy(ins[a].at[me], outs[a].at[me], local_sems.at[a]))
            mine[-1].start()
            for k in range(1, N_DEV):
                to = (me + k) % N_DEV
                sends.append(pltpu.make_async_remote_copy(
                    src_ref=ins[a].at[to], dst_ref=outs[a].at[me], send_sem=send_sems.at[a, k - 1], recv_sem=recv_sems.at[a, k - 1],
                    device_id=(to // 4, (to // 2) % 2, to % 2), device_id_type=MESH))
                sends[-1].start()
        for a in range(n):
            for k in range(1, N_DEV):
                frm = (me + N_DEV - k) % N_DEV
                pltpu.make_async_remote_copy(
                    src_ref=ins[a].at[frm], dst_ref=outs[a].at[frm], send_sem=send_sems.at[a, k - 1], recv_sem=recv_sems.at[a, k - 1],
                    device_id=(x, y, c), device_id_type=MESH).wait_recv()
        for cp in sends:
            cp.wait_send()
        for cp in mine:
            cp.wait()

    return pl.pallas_call(
        body, name=name,
        out_shape=[jax.ShapeDtypeStruct(a.shape, a.dtype) for a in arrays],
        in_specs=[ANY] * n, out_specs=[ANY] * n,
        scratch_shapes=[pltpu.SemaphoreType.DMA((n, 7)), pltpu.SemaphoreType.DMA((n, 7)), pltpu.SemaphoreType.DMA((n,))],
    )(*arrays)


def _columns_from_slabs(slabs):
    def body(*refs):
        k = len(refs) // 2
        for src, dst in zip(refs[:k], refs[k:]):
            n = src.shape[2]
            for i in range(N_DEV):
                dst[:, pl.ds(n * i, n)] = src[i]

    return pl.pallas_call(
        body, name="columns_from_slabs",
        out_shape=[jax.ShapeDtypeStruct((s.shape[1], N_DEV * s.shape[2]), s.dtype) for s in slabs],
        compiler_params=pltpu.CompilerParams(vmem_limit_bytes=VMEM_LIMIT_BYTES),
    )(*slabs)


def _mixer_core(z, ext_u, ext_p, conv_ref, pool_w_ref, tm):
    c_w = z.shape[1] // 4
    b, c, v, p = z[:, :c_w], z[:, c_w:2 * c_w], z[:, 2 * c_w:3 * c_w], z[:, 3 * c_w:]
    u = c * v
    ext_u[pl.ds(HALO, tm), :] = u
    ext_p[pl.ds(HALO, tm), :] = p
    u1 = ext_u[pl.ds(HALO - 1, tm), :]
    u2 = ext_u[pl.ds(HALO - 2, tm), :]
    yc = conv_ref[pl.ds(2, 1), :] * u + conv_ref[pl.ds(1, 1), :] * u1 + conv_ref[pl.ds(0, 1), :] * u2
    pooled, mixed = [], []
    for g, win in enumerate(POOL_WINDOWS):
        lanes = pl.ds(POOL_GROUP * g, POOL_GROUP)
        pg = p[:, POOL_GROUP * g:POOL_GROUP * (g + 1)]
        s = pg
        for k in range(1, win):
            s = s + ext_p[pl.ds(HALO - k, tm), lanes]
        pooled.append((s * (1.0 / win) - pg).astype(BF16))
        mixed.append(_dot(pooled[-1], pool_w_ref[g].astype(BF16)))
    return b, c, v, u, u1, u2, yc, pooled, mixed


def _meta_forward(meta, g1, w_in):
    def body(meta_ref, g1_ref, w_ref, a_ref, z_ref):
        hat, _ = _rms_stats(meta_ref[...])
        a = (hat * g1_ref[...]).astype(BF16)
        a_ref[...] = a
        z_ref[...] = _dot(a, w_ref[...])

    return pl.pallas_call(
        body, name="meta_forward",
        out_shape=[jax.ShapeDtypeStruct(meta.shape, BF16), jax.ShapeDtypeStruct((N_META, w_in.shape[1]), F32)],
        compiler_params=pltpu.CompilerParams(vmem_limit_bytes=VMEM_LIMIT_BYTES),
    )(meta, g1, w_in)


def _mixer_forward(x2d, z_meta, g1, w_in, conv_w, pool_w, pool_scale, w_out, g2, n_seq):
    t, d = x2d.shape
    zw = w_in.shape[1]
    cw = zw // 4
    s = t // n_seq
    tm = min(TM_MIX, s)
    nj = s // tm

    def body(x_ref, zm_ref, g1_ref, win_ref, conv_ref, pw_ref, ps_ref, wout_ref, g2_ref, h1_ref, z_ref, m_ref, ext_u, ext_p):
        @pl.when(pl.program_id(1) == 0)
        def _():
            zm = zm_ref[...]
            ext_u[pl.ds(0, HALO), :] = zm[:, cw:2 * cw] * zm[:, 2 * cw:3 * cw]
            ext_p[pl.ds(0, HALO), :] = zm[:, 3 * cw:]

        h0 = x_ref[...]
        hat, _ = _rms_stats(h0)
        z = _dot((hat * g1_ref[...]).astype(BF16), win_ref[...])
        z_ref[...] = z.astype(BF16)
        b, _, _, _, _, _, yc, _, mixed = _mixer_core(z, ext_u, ext_p, conv_ref, pw_ref, tm)
        ps = ps_ref[...]
        y = [b * yc] + [mixed[g] * ps[:, POOL_GROUP * g:POOL_GROUP * (g + 1)] for g in range(len(POOL_WINDOWS))]
        m = _dot(jnp.concatenate(y, axis=1).astype(BF16), wout_ref[...])
        m_ref[...] = m
        m_hat, _ = _rms_stats(m)
        h1_ref[...] = h0 + m_hat * g2_ref[...]
        ext_u[pl.ds(0, HALO), :] = ext_u[pl.ds(tm, HALO), :]
        ext_p[pl.ds(0, HALO), :] = ext_p[pl.ds(tm, HALO), :]

    row = lambda b, j: (b * nj + j, 0)
    return pl.pallas_call(
        body, name="mixer_forward", grid=(n_seq, nj),
        in_specs=[pl.BlockSpec((tm, d), row), _const(z_meta.shape), _const(g1.shape), _resident(w_in.shape), _const(conv_w.shape),
                  _const(pool_w.shape), _const(pool_scale.shape), _resident(w_out.shape), _const(g2.shape)],
        out_specs=[pl.BlockSpec((tm, d), row), pl.BlockSpec((tm, zw), row), pl.BlockSpec((tm, d), row)],
        out_shape=[jax.ShapeDtypeStruct((t, d), F32), jax.ShapeDtypeStruct((t, zw), BF16), jax.ShapeDtypeStruct((t, d), F32)],
        scratch_shapes=[pltpu.VMEM((tm + HALO, cw), F32), pltpu.VMEM((tm + HALO, cw), F32)],
        compiler_params=_params("arbitrary", "arbitrary"),
    )(x2d, z_meta, g1, w_in, conv_w, pool_w, pool_scale, w_out, g2)


def _mixer_backward(x2d, dh1, m, z, meta, a_meta, z_meta, g1, w_in, conv_w, pool_w, pool_scale, w_out, g2, n_seq):
    t, d = x2d.shape
    zw = w_in.shape[1]
    cw = zw // 4
    s = t // n_seq
    tm = min(TM_MIX, s)
    nj = s // tm
    n_groups = len(POOL_WINDOWS)
    zs = zw // N_DEV

    def body(x_ref, dh1_ref, m_ref, z_ref, zprev_ref, meta_ref, am_ref, zm_ref, g1_ref, win_ref, conv_ref, pw_ref, ps_ref, wout_ref,
             g2_ref, gx_ref, dwin_ref, dwout_ref, dg1_ref, dg2_ref, dconv_ref, dpw_ref, dps_ref, dmeta_ref,
             ext_u, ext_p, ext_dyc, ext_dq, acc_win, acc_wout, sem):
        b_id, j = pl.program_id(0), pl.program_id(1)
        jr = nj - 1 - j

        @pl.when((b_id == 0) & (j == 0))
        def _():
            acc_win[...] = jnp.zeros_like(acc_win)
            acc_wout[...] = jnp.zeros_like(acc_wout)
            for r in (dg1_ref, dg2_ref, dconv_ref, dpw_ref, dps_ref, dmeta_ref):
                r[...] = jnp.zeros_like(r)

        @pl.when(j == 0)
        def _():
            ext_dyc[pl.ds(tm, HALO), :] = jnp.zeros((HALO, cw), F32)
            ext_dq[pl.ds(tm, HALO), :] = jnp.zeros((HALO, cw), F32)

        zm = zm_ref[...]
        halo = jnp.where(jr == 0, zm, zprev_ref[...].astype(F32))
        ext_u[pl.ds(0, HALO), :] = halo[:, cw:2 * cw] * halo[:, 2 * cw:3 * cw]
        ext_p[pl.ds(0, HALO), :] = halo[:, 3 * cw:]

        h0 = x_ref[...]
        hat0, rstd0 = _rms_stats(h0)
        g1 = g1_ref[...]
        a = (hat0 * g1).astype(BF16)
        b, c, v, u, u1, u2, yc, pooled, mixed = _mixer_core(z_ref[...].astype(F32), ext_u, ext_p, conv_ref, pw_ref, tm)
        ps = ps_ref[...]
        y = [b * yc] + [mixed[g] * ps[:, POOL_GROUP * g:POOL_GROUP * (g + 1)] for g in range(n_groups)]
        ycat = jnp.concatenate(y, axis=1).astype(BF16)

        dh1v = dh1_ref[...]
        m_hat, m_rstd = _rms_stats(m_ref[...])
        dm, dg2 = _rms_bwd(m_hat, m_rstd, g2_ref[...], dh1v)
        dg2_ref[...] += dg2
        dm = dm.astype(BF16)
        acc_wout[...] += _dot_tn(ycat, dm)
        dycat = _dot_nt(dm, wout_ref[...])

        dyconv = dycat[:, :cw]
        db = dyconv * yc
        dyc = dyconv * b
        ext_dyc[pl.ds(0, tm), :] = dyc
        du = (conv_ref[pl.ds(2, 1), :] * dyc + conv_ref[pl.ds(1, 1), :] * ext_dyc[pl.ds(1, tm), :]
              + conv_ref[pl.ds(0, 1), :] * ext_dyc[pl.ds(2, tm), :])
        dconv_ref[pl.ds(2, 1), :] += jnp.sum(dyc * u, axis=0, keepdims=True)
        dconv_ref[pl.ds(1, 1), :] += jnp.sum(dyc * u1, axis=0, keepdims=True)
        dconv_ref[pl.ds(0, 1), :] += jnp.sum(dyc * u2, axis=0, keepdims=True)

        dp = []
        for g, win in enumerate(POOL_WINDOWS):
            lanes = pl.ds(POOL_GROUP * g, POOL_GROUP)
            dypool = dycat[:, cw + POOL_GROUP * g:cw + POOL_GROUP * (g + 1)]
            dps_ref[:, lanes] += jnp.sum(dypool * mixed[g], axis=0, keepdims=True)
            dmixed = (dypool * ps[:, POOL_GROUP * g:POOL_GROUP * (g + 1)]).astype(BF16)
            dpw_ref[g] += _dot_tn(pooled[g], dmixed)
            dq = _dot_nt(dmixed, pw_ref[g].astype(BF16))
            ext_dq[pl.ds(0, tm), lanes] = dq
            acc = dq
            for k in range(1, win):
                acc = acc + ext_dq[pl.ds(k, tm), lanes]
            dp.append(acc * (1.0 / win) - dq)

        dz = jnp.concatenate([db, du * v, du * c] + dp, axis=1).astype(BF16)
        acc_win[...] += _dot_tn(a, dz)
        dh0, dg1 = _rms_bwd(hat0, rstd0, g1, _dot_nt(dz, win_ref[...]))
        dg1_ref[...] += dg1
        gx_ref[...] = dh1v + dh0

        ext_dyc[pl.ds(tm, HALO), :] = ext_dyc[pl.ds(0, HALO), :]
        ext_dq[pl.ds(tm, HALO), :] = ext_dq[pl.ds(0, HALO), :]

        @pl.when(jr == 0)
        def _():
            ext_dyc[pl.ds(tm - HALO, HALO), :] = jnp.zeros((HALO, cw), F32)
            ext_dq[pl.ds(tm - HALO, HALO), :] = jnp.zeros((HALO, cw), F32)
            du_m = (conv_ref[pl.ds(1, 1), :] * ext_dyc[pl.ds(tm - HALO + 1, HALO), :]
                    + conv_ref[pl.ds(0, 1), :] * ext_dyc[pl.ds(tm - HALO + 2, HALO), :])
            dp_m = []
            for g, win in enumerate(POOL_WINDOWS):
                lanes = pl.ds(POOL_GROUP * g, POOL_GROUP)
                acc = ext_dq[pl.ds(tm - HALO + 1, HALO), lanes]
                for k in range(2, win):
                    acc = acc + ext_dq[pl.ds(tm - HALO + k, HALO), lanes]
                dp_m.append(acc * (1.0 / win))
            dz_m = jnp.concatenate([jnp.zeros((HALO, cw), F32), du_m * zm[:, 2 * cw:3 * cw], du_m * zm[:, cw:2 * cw]] + dp_m,
                                   axis=1).astype(BF16)
            acc_win[...] += _dot_tn(am_ref[...], dz_m)
            hat_m, rstd_m = _rms_stats(meta_ref[...])
            dmeta, dg1_m = _rms_bwd(hat_m, rstd_m, g1, _dot_nt(dz_m, win_ref[...]))
            dg1_ref[...] += dg1_m
            dmeta_ref[...] += dmeta

        @pl.when((b_id == n_seq - 1) & (j == nj - 1))
        def _():
            copies = [pltpu.make_async_copy(acc_win.at[:, pl.ds(zs * i, zs)], dwin_ref.at[i], sem.at[i]) for i in range(N_DEV)]
            copies.append(pltpu.make_async_copy(acc_wout, dwout_ref, sem.at[N_DEV]))
            for cp in copies:
                cp.start()
            for cp in copies:
                cp.wait()

    row = lambda b, j: (b * nj + nj - 1 - j, 0)
    prev = lambda b, j: (jnp.maximum((b * s + (nj - 1 - j) * tm) // HALO - 1, 0), 0)
    small = [g1.shape, g2.shape, conv_w.shape, pool_w.shape, pool_scale.shape, meta.shape]
    return pl.pallas_call(
        body, name="mixer_backward", grid=(n_seq, nj),
        in_specs=[pl.BlockSpec((tm, d), row), pl.BlockSpec((tm, d), row), pl.BlockSpec((tm, d), row), pl.BlockSpec((tm, zw), row),
                  pl.BlockSpec((HALO, zw), prev), _const(meta.shape), _const(a_meta.shape), _const(z_meta.shape), _const(g1.shape),
                  _resident(w_in.shape), _const(conv_w.shape), _const(pool_w.shape), _const(pool_scale.shape), _resident(w_out.shape),
                  _const(g2.shape)],
        out_specs=[pl.BlockSpec((tm, d), row), ANY, ANY] + [_const(sh) for sh in small],
        out_shape=[jax.ShapeDtypeStruct((t, d), F32), jax.ShapeDtypeStruct((N_DEV, d, zs), F32),
                   jax.ShapeDtypeStruct(w_out.shape, F32)] + [jax.ShapeDtypeStruct(sh, F32) for sh in small],
        scratch_shapes=[pltpu.VMEM((tm + HALO, cw), F32)] * 4
        + [pltpu.VMEM(w_in.shape, F32), pltpu.VMEM(w_out.shape, F32), pltpu.SemaphoreType.DMA((N_DEV + 1,))],
        compiler_params=_params("arbitrary", "arbitrary"),
    )(x2d, dh1, m, z, z, meta, a_meta, z_meta, g1, w_in, conv_w, pool_w, pool_scale, w_out, g2)


def _ffn_forward(h1, target, g3, w_gate, w_up, w_down, g4):
    t, d = h1.shape
    ff = w_gate.shape[1]
    tm = min(TM_FFN, t)

    def body(h1_ref, tgt_ref, g3_ref, wg_ref, wu_ref, wd_ref, g4_ref, f_ref, gate_ref, up_ref, dd_ref, dh2_ref, loss_ref, dg4_ref):
        @pl.when(pl.program_id(0) == 0)
        def _():
            loss_ref[...] = jnp.zeros_like(loss_ref)
            dg4_ref[...] = jnp.zeros_like(dg4_ref)

        h1v = h1_ref[...]
        hat, _ = _rms_stats(h1v)
        f = (hat * g3_ref[...]).astype(BF16)
        f_ref[...] = f
        gate = _dot(f, wg_ref[...])
        up = _dot(f, wu_ref[...])
        gate_ref[...] = gate.astype(BF16)
        up_ref[...] = up.astype(BF16)
        act = (gate * jax.nn.sigmoid(gate) * up).astype(BF16)
        d_hat, d_rstd = _rms_stats(_dot(act, wd_ref[...]))
        g4 = g4_ref[...]
        err = h1v + d_hat * g4 - tgt_ref[...]
        loss_ref[...] += jnp.sum(err * err) * (0.5 / d)
        dh2 = err * (1.0 / d)
        dh2_ref[...] = dh2
        dd, dg4 = _rms_bwd(d_hat, d_rstd, g4, dh2)
        dg4_ref[...] += dg4
        dd_ref[...] = dd.astype(BF16)

    row = lambda i: (i, 0)
    return pl.pallas_call(
        body, name="ffn_forward", grid=(t // tm,),
        in_specs=[pl.BlockSpec((tm, d), row), pl.BlockSpec((tm, d), row), _const(g3.shape), _resident(w_gate.shape),
                  _resident(w_up.shape), _resident(w_down.shape), _const(g4.shape)],
        out_specs=[pl.BlockSpec((tm, d), row), pl.BlockSpec((tm, ff), row), pl.BlockSpec((tm, ff), row), pl.BlockSpec((tm, d), row),
                   pl.BlockSpec((tm, d), row), _const((8, 128)), _const(g4.shape)],
        out_shape=[jax.ShapeDtypeStruct((t, d), BF16), jax.ShapeDtypeStruct((t, ff), BF16), jax.ShapeDtypeStruct((t, ff), BF16),
                   jax.ShapeDtypeStruct((t, d), BF16), jax.ShapeDtypeStruct((t, d), F32), jax.ShapeDtypeStruct((8, 128), F32),
                   jax.ShapeDtypeStruct(g4.shape, F32)],
        compiler_params=_params("arbitrary"),
    )(h1, target, g3, w_gate, w_up, w_down, g4)


def _ffn_backward(h1, dh2, dd, gate, up, g3, w_gate, w_up, w_down):
    t, d = h1.shape
    ff = w_gate.shape[1]
    tm = min(TM_FFN, t)

    def body(h1_ref, dh2_ref, dd_ref, gate_ref, up_ref, g3_ref, wg_ref, wu_ref, wd_ref, dh1_ref, dgate_ref, dup_ref, act_ref, dg3_ref):
        @pl.when(pl.program_id(0) == 0)
        def _():
            dg3_ref[...] = jnp.zeros_like(dg3_ref)

        dact = _dot_nt(dd_ref[...], wd_ref[...])
        gate = gate_ref[...].astype(F32)
        up = up_ref[...].astype(F32)
        sig = jax.nn.sigmoid(gate)
        silu = gate * sig
        act_ref[...] = (silu * up).astype(BF16)
        dup = (dact * silu).astype(BF16)
        dgate = (dact * up * (sig * (1.0 + gate * (1.0 - sig)))).astype(BF16)
        dup_ref[...] = dup
        dgate_ref[...] = dgate
        df = _dot_nt(dgate, wg_ref[...]) + _dot_nt(dup, wu_ref[...])
        hat, rstd = _rms_stats(h1_ref[...])
        dh1, dg3 = _rms_bwd(hat, rstd, g3_ref[...], df)
        dg3_ref[...] += dg3
        dh1_ref[...] = dh2_ref[...] + dh1

    row = lambda i: (i, 0)
    return pl.pallas_call(
        body, name="ffn_backward", grid=(t // tm,),
        in_specs=[pl.BlockSpec((tm, d), row), pl.BlockSpec((tm, d), row), pl.BlockSpec((tm, d), row), pl.BlockSpec((tm, ff), row),
                  pl.BlockSpec((tm, ff), row), _const(g3.shape), _resident(w_gate.shape), _resident(w_up.shape), _resident(w_down.shape)],
        out_specs=[pl.BlockSpec((tm, d), row), pl.BlockSpec((tm, ff), row), pl.BlockSpec((tm, ff), row), pl.BlockSpec((tm, ff), row),
                   _const(g3.shape)],
        out_shape=[jax.ShapeDtypeStruct((t, d), F32), jax.ShapeDtypeStruct((t, ff), BF16), jax.ShapeDtypeStruct((t, ff), BF16),
                   jax.ShapeDtypeStruct((t, ff), BF16), jax.ShapeDtypeStruct(g3.shape, F32)],
        compiler_params=_params("arbitrary"),
    )(h1, dh2, dd, gate, up, g3, w_gate, w_up, w_down)


def _ffn_weight_grads(f, dd, dgate, dup, act):
    t, d = f.shape
    ff = dgate.shape[1]
    tm = min(TM_FFN, t)
    nt = t // tm
    fc = ff // FF_CHUNKS
    fs = ff // N_DEV
    per = N_DEV // FF_CHUNKS

    def body(f_ref, dd_ref, dgate_ref, dup_ref, act_ref, dwg_ref, dwu_ref, dwd_ref, acc_g, acc_u, acc_d, stage, sem):
        c, i = pl.program_id(0), pl.program_id(1)

        @pl.when(i == 0)
        def _():
            acc_g[...] = jnp.zeros_like(acc_g)
            acc_u[...] = jnp.zeros_like(acc_u)
            acc_d[...] = jnp.zeros_like(acc_d)

        fv = f_ref[...]
        acc_g[...] += _dot_tn(fv, dgate_ref[...])
        acc_u[...] += _dot_tn(fv, dup_ref[...])
        acc_d[...] += _dot_tn(act_ref[...], dd_ref[...])

        @pl.when(i == nt - 1)
        def _():
            down = pltpu.make_async_copy(acc_d, dwd_ref.at[pl.ds(pl.multiple_of(c * fc, 8), fc), :], sem.at[0])
            down.start()
            for acc, out in ((acc_g, dwg_ref), (acc_u, dwu_ref)):
                for k in range(per):
                    stage[...] = acc[:, pl.ds(fs * k, fs)]
                    cp = pltpu.make_async_copy(stage, out.at[c * per + k], sem.at[1])
                    cp.start()
                    cp.wait()
            down.wait()

    row = lambda c, i: (i, 0)
    col = lambda c, i: (i, c)
    return pl.pallas_call(
        body, name="ffn_weight_grads", grid=(FF_CHUNKS, nt),
        in_specs=[pl.BlockSpec((tm, d), row), pl.BlockSpec((tm, d), row), pl.BlockSpec((tm, fc), col), pl.BlockSpec((tm, fc), col),
                  pl.BlockSpec((tm, fc), col)],
        out_specs=[ANY, ANY, ANY],
        out_shape=[jax.ShapeDtypeStruct((N_DEV, d, fs), F32), jax.ShapeDtypeStruct((N_DEV, d, fs), F32),
                   jax.ShapeDtypeStruct((ff, d), F32)],
        scratch_shapes=[pltpu.VMEM((d, fc), F32), pltpu.VMEM((d, fc), F32), pltpu.VMEM((fc, d), F32), pltpu.VMEM((d, fs), F32),
                        pltpu.SemaphoreType.DMA((2,))],
        compiler_params=_params("arbitrary", "arbitrary"),
    )(f, dd, dgate, dup, act)


def _adamw(w, g, m, v):
    m = ADAM_B1 * m + (1.0 - ADAM_B1) * g
    v = ADAM_B2 * v + (1.0 - ADAM_B2) * (g * g)
    m_hat = m / (1.0 - ADAM_B1 ** ADAM_STEP)
    v_hat = v / (1.0 - ADAM_B2 ** ADAM_STEP)
    return -ADAM_LR * (m_hat / (jnp.sqrt(v_hat) + ADAM_EPS) + ADAM_WD * w), m, v


def _sum_slabs(ref):
    total = ref[0]
    for i in range(1, N_DEV):
        total = total + ref[i]
    return total


def _reduce_adamw(parts, w, m, v, name):
    r, c = w.shape
    tr = r
    for cand in range(8, r, 8):
        if r % cand == 0 and cand * c <= ADAMW_BLOCK_ELEMS:
            tr = cand
    if r * c <= ADAMW_BLOCK_ELEMS:
        tr = r

    def body(p_ref, w_ref, m_ref, v_ref, g_out, d_out, m_out, v_out):
        g = _sum_slabs(p_ref)
        g_out[...] = g
        d_out[...], m_out[...], v_out[...] = _adamw(w_ref[...], g, m_ref[...], v_ref[...])

    blk = pl.BlockSpec((tr, c), lambda i: (i, 0))
    return pl.pallas_call(
        body, name=name, grid=(r // tr,),
        in_specs=[pl.BlockSpec((N_DEV, tr, c), lambda i: (0, i, 0)), blk, blk, blk],
        out_specs=[blk] * 4, out_shape=[jax.ShapeDtypeStruct((r, c), F32)] * 4,
        compiler_params=_params("arbitrary"),
    )(parts, w, m, v)


def _reduce_adamw_small(parts, ws, ms, vs):
    n = len(parts)

    def body(*refs):
        p_refs, w_refs, m_refs, v_refs = (refs[k * n:(k + 1) * n] for k in range(4))
        outs = refs[4 * n:]
        for a in range(n):
            g = _sum_slabs(p_refs[a])
            outs[4 * a][...] = g
            outs[4 * a + 1][...], outs[4 * a + 2][...], outs[4 * a + 3][...] = _adamw(w_refs[a][...], g, m_refs[a][...], v_refs[a][...])

    out = pl.pallas_call(
        body, name="adamw_replicated",
        out_shape=[jax.ShapeDtypeStruct(w.shape, F32) for w in ws for _ in range(4)],
        compiler_params=pltpu.CompilerParams(vmem_limit_bytes=VMEM_LIMIT_BYTES),
    )(*parts, *ws, *ms, *vs)
    return [tuple(out[4 * a:4 * a + 4]) for a in range(n)]


def kernel(x, meta_tokens, norm_mix_pre, w_in, conv_w, pool_w, pool_scale, w_out, norm_mix_post, norm_ffn_pre, w_gate, w_up, w_down, norm_ffn_post, loss_target, m_meta_tokens, m_norm_mix_pre, m_w_in, m_conv_w, m_pool_w, m_pool_scale, m_w_out, m_norm_mix_post, m_norm_ffn_pre, m_w_gate, m_w_up, m_w_down, m_norm_ffn_post, v_meta_tokens, v_norm_mix_pre, v_w_in, v_conv_w, v_pool_w, v_pool_scale, v_w_out, v_norm_mix_post, v_norm_ffn_pre, v_w_gate, v_w_up, v_w_down, v_norm_ffn_post):
    n_seq, seq, d = x.shape
    x2d = x.reshape(n_seq * seq, d)
    target = loss_target.reshape(n_seq * seq, d)

    win_s, wout_s, wg_s, wu_s, wd_s, meta_s, conv_s = _all_gather(
        [w_in[0], w_out[0], w_gate[0], w_up[0], w_down[0], meta_tokens, conv_w[0]], [BF16] * 5 + [F32] * 2, "gather_weights")
    win_b, wg_b, wu_b = _columns_from_slabs([win_s, wg_s, wu_s])
    wout_b = wout_s.reshape(d, d)
    wd_b = wd_s.reshape(-1, d)
    meta = jnp.transpose(meta_s, (1, 0, 2)).reshape(N_META, d)
    conv = jnp.transpose(conv_s, (1, 0, 2)).reshape(CONV_WIDTH, -1)
    pw, ps = pool_w[0], pool_scale

    a_meta, z_meta = _meta_forward(meta, norm_mix_pre, win_b)
    h1, z, m = _mixer_forward(x2d, z_meta, norm_mix_pre, win_b, conv, pw, ps, wout_b, norm_mix_post, n_seq)
    f, gate, up, dd, dh2, loss_sum, dg4 = _ffn_forward(h1, target, norm_ffn_pre, wg_b, wu_b, wd_b, norm_ffn_post)
    dh1, dgate, dup, act, dg3 = _ffn_backward(h1, dh2, dd, gate, up, norm_ffn_pre, wg_b, wu_b, wd_b)
    dwg, dwu, dwd = _ffn_weight_grads(f, dd, dgate, dup, act)
    gx, dwin, dwout, dg1, dg2, dconv, dpw, dps, dmeta = _mixer_backward(
        x2d, dh1, m, z, meta, a_meta, z_meta, norm_mix_pre, win_b, conv, pw, ps, wout_b, norm_mix_post, n_seq)

    dmeta_s = jnp.transpose(dmeta.reshape(N_META, N_DEV, -1), (1, 0, 2))
    dconv_s = jnp.transpose(dconv.reshape(CONV_WIDTH, N_DEV, -1), (1, 0, 2))
    sharded = _all_to_all([dwin, dwout.reshape(N_DEV, -1, d), dwg, dwu, dwd.reshape(N_DEV, -1, d), dmeta_s, dconv_s], "exchange_grads")
    replicated = _all_gather([dg1, dg2, dg3, dg4, dpw, dps], [F32] * 6, "gather_small_grads")

    names = ["meta_tokens", "norm_mix_pre", "w_in", "conv_w", "pool_w", "pool_scale", "w_out", "norm_mix_post", "norm_ffn_pre", "w_gate",
             "w_up", "w_down", "norm_ffn_post"]
    res = {}
    for nm, parts, w, m_, v_ in (("w_in", sharded[0], w_in, m_w_in, v_w_in), ("w_out", sharded[1], w_out, m_w_out, v_w_out),
                                 ("w_gate", sharded[2], w_gate, m_w_gate, v_w_gate), ("w_up", sharded[3], w_up, m_w_up, v_w_up),
                                 ("w_down", sharded[4], w_down, m_w_down, v_w_down), ("conv_w", sharded[6], conv_w, m_conv_w, v_conv_w)):
        res[nm] = tuple(o[None] for o in _reduce_adamw(parts, w[0], m_[0], v_[0], "adamw_" + nm))
    res["meta_tokens"] = tuple(_reduce_adamw(sharded[5], meta_tokens, m_meta_tokens, v_meta_tokens, "adamw_meta_tokens"))
    small = _reduce_adamw_small(
        replicated, [norm_mix_pre, norm_mix_post, norm_ffn_pre, norm_ffn_post, pool_w[0], pool_scale],
        [m_norm_mix_pre, m_norm_mix_post, m_norm_ffn_pre, m_norm_ffn_post, m_pool_w[0], m_pool_scale],
        [v_norm_mix_pre, v_norm_mix_post, v_norm_ffn_pre, v_norm_ffn_post, v_pool_w[0], v_pool_scale])
    for nm, r in zip(["norm_mix_pre", "norm_mix_post", "norm_ffn_pre", "norm_ffn_post", "pool_w", "pool_scale"], small):
        res[nm] = tuple(o[None] for o in r) if nm == "pool_w" else r

    loss = lax.psum(loss_sum[0, 0], MESH_AXES)
    return (loss, gx.reshape(n_seq, seq, d), *[res[nm][0] for nm in names], *[res[nm][1] for nm in names],
            *[res[nm][2] for nm in names], *[res[nm][3] for nm in names])
```

```python
import functools

import jax
import jax.numpy as jnp
from jax import lax
from jax.experimental import pallas as pl
from jax.experimental.pallas import tpu as pltpu

F32, BF16 = jnp.float32, jnp.bfloat16
RMS_EPS = 1e-6
N_META = 16
CONV_WIDTH = 3
POOL_WINDOWS = (2, 4, 8, 16)
POOL_GROUP = 128
HALO = 16
N_DEV = 8
MESH_AXES = ("x", "y", "c")
MESH = pl.DeviceIdType.MESH
VMEM_LIMIT_BYTES = 56 * 1024 * 1024
ADAMW_BLOCK_ELEMS = 64 * 1024
TM_MIX = 512
TM_FFN = 256
FF_CHUNKS = 2

ADAM_LR, ADAM_B1, ADAM_B2, ADAM_EPS, ADAM_WD, ADAM_STEP = 0.001, 0.9, 0.999, 1e-08, 0.01, 10


def _dot(a, b):
    return jnp.dot(a, b, preferred_element_type=F32)


def _dot_nt(a, b):
    return lax.dot_general(a, b, (((1,), (1,)), ((), ())), preferred_element_type=F32)


def _dot_tn(a, b):
    return lax.dot_general(a, b, (((0,), (0,)), ((), ())), preferred_element_type=F32)


def _rms_stats(h):
    rstd = lax.rsqrt(jnp.mean(h * h, axis=-1, keepdims=True) + RMS_EPS)
    return h * rstd, rstd


def _rms_bwd(hat, rstd, g, dy):
    gdy = dy * g
    proj = jnp.mean(gdy * hat, axis=-1, keepdims=True)
    return rstd * (gdy - hat * proj), jnp.sum(dy * hat, axis=0, keepdims=True)


def _params(*semantics):
    return pltpu.CompilerParams(dimension_semantics=semantics or None, vmem_limit_bytes=VMEM_LIMIT_BYTES)


def _resident(shape):
    zeros = (0,) * len(shape)
    return pl.BlockSpec(shape, lambda *_: zeros, pipeline_mode=pl.Buffered(1))


def _const(shape):
    zeros = (0,) * len(shape)
    return pl.BlockSpec(shape, lambda *_: zeros)


ANY = pl.BlockSpec(memory_space=pl.ANY)


def _my_place():
    x, y, c = (lax.axis_index(a) for a in MESH_AXES)
    return x, y, c


def _exchange_sems(n):
    return [pltpu.SemaphoreType.DMA((n, N_DEV - 1)), pltpu.SemaphoreType.DMA((n, N_DEV - 1)), pltpu.SemaphoreType.DMA((n,))]


def _gather_ops(srcs, outs, send_sems, recv_sems, local_sems):
    n = len(srcs)
    x, y, c = _my_place()
    me, sibling = (x, y, c), (x, y, 1 - c)
    chips = [(1 - x, y), (x, 1 - y), (1 - x, 1 - y)]

    def slab(px, py, pc):
        return 4 * px + 2 * py + pc

    def copy(a, k, block, to, src=None):
        dst = outs[a].at[slab(*block)]
        return pltpu.make_async_remote_copy(
            src_ref=dst if src is None else src, dst_ref=dst, send_sem=send_sems.at[a, k], recv_sem=recv_sems.at[a, k],
            device_id=to, device_id_type=MESH)

    def mine(a):
        return pltpu.make_async_copy(srcs[a], outs[a].at[slab(*me)], local_sems.at[a])

    def first(a):
        return [copy(a, 0, me, sibling, src=srcs[a])] + [copy(a, 1 + j, me, (*chip, c), src=srcs[a]) for j, chip in enumerate(chips)]

    def passed(a, j):
        return copy(a, 4 + j, (*chips[j], c), sibling)

    def start():
        for a in range(n):
            mine(a).start()
            for cp in first(a):
                cp.start()

    def forward():
        for j, chip in enumerate(chips):
            for a in range(n):
                copy(a, 1 + j, (*chip, c), me).wait_recv()
                passed(a, j).start()

    def finish():
        for a in range(n):
            copy(a, 0, sibling, me).wait_recv()
            for j, chip in enumerate(chips):
                copy(a, 4 + j, (*chip, 1 - c), me).wait_recv()
        for a in range(n):
            for cp in first(a) + [passed(a, j) for j in range(len(chips))]:
                cp.wait_send()
            mine(a).wait()

    return start, forward, finish


def _exchange_ops(ins, outs, whole, send_sems, recv_sems, local_sems):
    n = len(ins)
    x, y, c = _my_place()
    me = 4 * x + 2 * y + c

    def src(a, i):
        return ins[a] if whole[a] else ins[a].at[i]

    def mine(a):
        return pltpu.make_async_copy(src(a, me), outs[a].at[me], local_sems.at[a])

    def send(a, k):
        to = (me + k) % N_DEV
        return pltpu.make_async_remote_copy(
            src_ref=src(a, to), dst_ref=outs[a].at[me], send_sem=send_sems.at[a, k - 1], recv_sem=recv_sems.at[a, k - 1],
            device_id=(to // 4, (to // 2) % 2, to % 2), device_id_type=MESH)

    def landed(a, k):
        frm = (me + N_DEV - k) % N_DEV
        return pltpu.make_async_remote_copy(
            src_ref=src(a, frm), dst_ref=outs[a].at[frm], send_sem=send_sems.at[a, k - 1], recv_sem=recv_sems.at[a, k - 1],
            device_id=(x, y, c), device_id_type=MESH)

    def start():
        for a in range(n):
            mine(a).start()
            for k in range(1, N_DEV):
                send(a, k).start()

    def finish():
        for a in range(n):
            for k in range(1, N_DEV):
                landed(a, k).wait_recv()
        for a in range(n):
            for k in range(1, N_DEV):
                send(a, k).wait_send()
            mine(a).wait()

    return start, finish


def _gather_first_weights(gathered, dtypes, cast_only):
    n, k = len(gathered), len(cast_only)

    def body(*refs):
        ins, casts_in = refs[:n], refs[n:n + k]
        outs, casts_out = refs[n + k:2 * n + k], refs[2 * n + k:2 * n + 2 * k]
        stages = refs[2 * n + 2 * k:3 * n + 2 * k]
        start, forward, finish = _gather_ops(stages, outs, *refs[3 * n + 2 * k:])
        for a in range(n):
            stages[a][...] = ins[a][...].astype(stages[a].dtype)
        start()
        for a in range(k):
            casts_out[a][...] = casts_in[a][...].astype(BF16)
        forward()
        finish()

    vmem = pl.BlockSpec(memory_space=pltpu.VMEM)
    out = pl.pallas_call(
        body, name="gather_first_weights",
        out_shape=[jax.ShapeDtypeStruct((N_DEV, *s.shape), d) for s, d in zip(gathered, dtypes)]
        + [jax.ShapeDtypeStruct(s.shape, BF16) for s in cast_only],
        in_specs=[vmem] * (n + k), out_specs=[ANY] * n + [vmem] * k,
        scratch_shapes=[pltpu.VMEM(s.shape, d) for s, d in zip(gathered, dtypes)] + _exchange_sems(n),
        compiler_params=pltpu.CompilerParams(vmem_limit_bytes=VMEM_LIMIT_BYTES),
    )(*gathered, *cast_only)
    return out[:n], out[n:]


def _exchange(arrays, whole, name):
    n = len(arrays)

    def body(*refs):
        start, finish = _exchange_ops(refs[:n], refs[n:2 * n], whole, *refs[2 * n:])
        start()
        finish()

    return pl.pallas_call(
        body, name=name,
        out_shape=[jax.ShapeDtypeStruct((N_DEV, *a.shape) if w else a.shape, a.dtype) for a, w in zip(arrays, whole)],
        in_specs=[ANY] * n, out_specs=[ANY] * n, scratch_shapes=_exchange_sems(n),
    )(*arrays)


def _columns_from_slabs(slabs):
    def body(*refs):
        k = len(refs) // 2
        for src, dst in zip(refs[:k], refs[k:]):
            n = src.shape[2]
            for i in range(N_DEV):
                dst[:, pl.ds(n * i, n)] = src[i]

    return pl.pallas_call(
        body, name="columns_from_slabs",
        out_shape=[jax.ShapeDtypeStruct((s.shape[1], N_DEV * s.shape[2]), s.dtype) for s in slabs],
        compiler_params=pltpu.CompilerParams(vmem_limit_bytes=VMEM_LIMIT_BYTES),
    )(*slabs)


def _mixer_core(z, ext_u, ext_p, conv_ref, pool_w_ref, tm):
    c_w = z.shape[1] // 4
    b, c, v, p = z[:, :c_w], z[:, c_w:2 * c_w], z[:, 2 * c_w:3 * c_w], z[:, 3 * c_w:]
    u = c * v
    ext_u[pl.ds(HALO, tm), :] = u
    ext_p[pl.ds(HALO, tm), :] = p
    u1 = ext_u[pl.ds(HALO - 1, tm), :]
    u2 = ext_u[pl.ds(HALO - 2, tm), :]
    yc = conv_ref[pl.ds(2, 1), :] * u + conv_ref[pl.ds(1, 1), :] * u1 + conv_ref[pl.ds(0, 1), :] * u2
    pooled, mixed = [], []
    for g, win in enumerate(POOL_WINDOWS):
        lanes = pl.ds(POOL_GROUP * g, POOL_GROUP)
        pg = p[:, POOL_GROUP * g:POOL_GROUP * (g + 1)]
        s = pg
        for k in range(1, win):
            s = s + ext_p[pl.ds(HALO - k, tm), lanes]
        pooled.append((s * (1.0 / win) - pg).astype(BF16))
        mixed.append(_dot(pooled[-1], pool_w_ref[g].astype(BF16)))
    return b, c, v, u, u1, u2, yc, pooled, mixed


def _meta_forward(meta, g1, w_in):
    def body(meta_ref, g1_ref, w_ref, a_ref, z_ref):
        hat, _ = _rms_stats(meta_ref[...])
        a = (hat * g1_ref[...]).astype(BF16)
        a_ref[...] = a
        z_ref[...] = _dot(a, w_ref[...])

    return pl.pallas_call(
        body, name="meta_forward",
        out_shape=[jax.ShapeDtypeStruct(meta.shape, BF16), jax.ShapeDtypeStruct((N_META, w_in.shape[1]), F32)],
        compiler_params=pltpu.CompilerParams(vmem_limit_bytes=VMEM_LIMIT_BYTES),
    )(meta, g1, w_in)


def _mixer_forward(x2d, z_meta, g1, w_in, conv_w, pool_w, pool_scale, w_out, g2, n_seq, to_gather):
    t, d = x2d.shape
    zw = w_in.shape[1]
    cw = zw // 4
    s = t // n_seq
    tm = min(TM_MIX, s)
    nj = s // tm
    ng = len(to_gather)

    def body(x_ref, zm_ref, g1_ref, win_ref, conv_ref, pw_ref, ps_ref, wout_ref, g2_ref, *rest):
        shards, (h1_ref, z_ref, m_ref), slabs = rest[:ng], rest[ng:ng + 3], rest[ng + 3:2 * ng + 3]
        ext_u, ext_p = rest[2 * ng + 3:2 * ng + 5]
        start, forward, finish = _gather_ops(shards, slabs, *rest[2 * ng + 5:])
        pl.when((pl.program_id(0) == 0) & (pl.program_id(1) == 0))(start)

        @pl.when(pl.program_id(1) == 0)
        def _():
            zm = zm_ref[...]
            ext_u[pl.ds(0, HALO), :] = zm[:, cw:2 * cw] * zm[:, 2 * cw:3 * cw]
            ext_p[pl.ds(0, HALO), :] = zm[:, 3 * cw:]

        h0 = x_ref[...]
        hat, _ = _rms_stats(h0)
        z = _dot((hat * g1_ref[...]).astype(BF16), win_ref[...])
        z_ref[...] = z.astype(BF16)
        b, _, _, _, _, _, yc, _, mixed = _mixer_core(z, ext_u, ext_p, conv_ref, pw_ref, tm)
        ps = ps_ref[...]
        y = [b * yc] + [mixed[g] * ps[:, POOL_GROUP * g:POOL_GROUP * (g + 1)] for g in range(len(POOL_WINDOWS))]
        m = _dot(jnp.concatenate(y, axis=1).astype(BF16), wout_ref[...])
        m_ref[...] = m
        m_hat, _ = _rms_stats(m)
        h1_ref[...] = h0 + m_hat * g2_ref[...]
        ext_u[pl.ds(0, HALO), :] = ext_u[pl.ds(tm, HALO), :]
        ext_p[pl.ds(0, HALO), :] = ext_p[pl.ds(tm, HALO), :]

        @pl.when((pl.program_id(0) == n_seq - 1) & (pl.program_id(1) == nj - 1))
        def _():
            forward()
            finish()

    row = lambda b, j: (b * nj + j, 0)
    out = pl.pallas_call(
        body, name="mixer_forward", grid=(n_seq, nj),
        in_specs=[pl.BlockSpec((tm, d), row), _const(z_meta.shape), _const(g1.shape), _resident(w_in.shape), _const(conv_w.shape),
                  _const(pool_w.shape), _const(pool_scale.shape), _resident(w_out.shape), _const(g2.shape)] + [ANY] * ng,
        out_specs=[pl.BlockSpec((tm, d), row), pl.BlockSpec((tm, zw), row), pl.BlockSpec((tm, d), row)] + [ANY] * ng,
        out_shape=[jax.ShapeDtypeStruct((t, d), F32), jax.ShapeDtypeStruct((t, zw), BF16), jax.ShapeDtypeStruct((t, d), F32)]
        + [jax.ShapeDtypeStruct((N_DEV, *a.shape), a.dtype) for a in to_gather],
        scratch_shapes=[pltpu.VMEM((tm + HALO, cw), F32), pltpu.VMEM((tm + HALO, cw), F32)] + _exchange_sems(ng),
        compiler_params=_params("arbitrary", "arbitrary"),
    )(x2d, z_meta, g1, w_in, conv_w, pool_w, pool_scale, w_out, g2, *to_gather)
    return out[:3], out[3:]


def _mixer_backward(x2d, dh1, m, z, meta, a_meta, z_meta, g1, w_in, conv_w, pool_w, pool_scale, w_out, g2, n_seq, to_exchange):
    t, d = x2d.shape
    zw = w_in.shape[1]
    cw = zw // 4
    s = t // n_seq
    tm = min(TM_MIX, s)
    nj = s // tm
    n_groups = len(POOL_WINDOWS)
    zs = zw // N_DEV
    nx = len(to_exchange)

    def body(x_ref, dh1_ref, m_ref, z_ref, zprev_ref, meta_ref, am_ref, zm_ref, g1_ref, win_ref, conv_ref, pw_ref, ps_ref, wout_ref,
             g2_ref, *rest):
        sent, rest = rest[:nx], rest[nx:]
        gx_ref, dwin_ref, dwout_ref, dg1_ref, dg2_ref, dconv_ref, dpw_ref, dps_ref, dmeta_ref = rest[:9]
        landed, rest = rest[9:9 + nx], rest[9 + nx:]
        ext_u, ext_p, ext_dyc, ext_dq, acc_win, acc_wout, stage16, sem = rest[:8]
        start, finish = _exchange_ops(sent, landed, [False] * nx, *rest[8:])
        b_id, j = pl.program_id(0), pl.program_id(1)
        jr = nj - 1 - j
        pl.when((b_id == 0) & (j == 0))(start)

        @pl.when((b_id == 0) & (j == 0))
        def _():
            acc_win[...] = jnp.zeros_like(acc_win)
            acc_wout[...] = jnp.zeros_like(acc_wout)
            for r in (dg1_ref, dg2_ref, dconv_ref, dpw_ref, dps_ref, dmeta_ref):
                r[...] = jnp.zeros_like(r)

        @pl.when(j == 0)
        def _():
            ext_dyc[pl.ds(tm, HALO), :] = jnp.zeros((HALO, cw), F32)
            ext_dq[pl.ds(tm, HALO), :] = jnp.zeros((HALO, cw), F32)

        zm = zm_ref[...]
        halo = jnp.where(jr == 0, zm, zprev_ref[...].astype(F32))
        ext_u[pl.ds(0, HALO), :] = halo[:, cw:2 * cw] * halo[:, 2 * cw:3 * cw]
        ext_p[pl.ds(0, HALO), :] = halo[:, 3 * cw:]

        h0 = x_ref[...]
        hat0, rstd0 = _rms_stats(h0)
        g1 = g1_ref[...]
        a = (hat0 * g1).astype(BF16)
        b, c, v, u, u1, u2, yc, pooled, mixed = _mixer_core(z_ref[...].astype(F32), ext_u, ext_p, conv_ref, pw_ref, tm)
        ps = ps_ref[...]
        y = [b * yc] + [mixed[g] * ps[:, POOL_GROUP * g:POOL_GROUP * (g + 1)] for g in range(n_groups)]
        ycat = jnp.concatenate(y, axis=1).astype(BF16)

        dh1v = dh1_ref[...]
        m_hat, m_rstd = _rms_stats(m_ref[...])
        dm, dg2 = _rms_bwd(m_hat, m_rstd, g2_ref[...], dh1v)
        dg2_ref[...] += dg2
        dm = dm.astype(BF16)
        acc_wout[...] += _dot_tn(ycat, dm)
        dycat = _dot_nt(dm, wout_ref[...])

        dyconv = dycat[:, :cw]
        db = dyconv * yc
        dyc = dyconv * b
        ext_dyc[pl.ds(0, tm), :] = dyc
        du = (conv_ref[pl.ds(2, 1), :] * dyc + conv_ref[pl.ds(1, 1), :] * ext_dyc[pl.ds(1, tm), :]
              + conv_ref[pl.ds(0, 1), :] * ext_dyc[pl.ds(2, tm), :])
        dconv_ref[pl.ds(2, 1), :] += jnp.sum(dyc * u, axis=0, keepdims=True)
        dconv_ref[pl.ds(1, 1), :] += jnp.sum(dyc * u1, axis=0, keepdims=True)
        dconv_ref[pl.ds(0, 1), :] += jnp.sum(dyc * u2, axis=0, keepdims=True)

        dp = []
        for g, win in enumerate(POOL_WINDOWS):
            lanes = pl.ds(POOL_GROUP * g, POOL_GROUP)
            dypool = dycat[:, cw + POOL_GROUP * g:cw + POOL_GROUP * (g + 1)]
            dps_ref[:, lanes] += jnp.sum(dypool * mixed[g], axis=0, keepdims=True)
            dmixed = (dypool * ps[:, POOL_GROUP * g:POOL_GROUP * (g + 1)]).astype(BF16)
            dpw_ref[g] += _dot_tn(pooled[g], dmixed)
            dq = _dot_nt(dmixed, pw_ref[g].astype(BF16))
            ext_dq[pl.ds(0, tm), lanes] = dq
            acc = dq
            for k in range(1, win):
                acc = acc + ext_dq[pl.ds(k, tm), lanes]
            dp.append(acc * (1.0 / win) - dq)

        dz = jnp.concatenate([db, du * v, du * c] + dp, axis=1).astype(BF16)
        acc_win[...] += _dot_tn(a, dz)
        dh0, dg1 = _rms_bwd(hat0, rstd0, g1, _dot_nt(dz, win_ref[...]))
        dg1_ref[...] += dg1
        gx_ref[...] = dh1v + dh0

        ext_dyc[pl.ds(tm, HALO), :] = ext_dyc[pl.ds(0, HALO), :]
        ext_dq[pl.ds(tm, HALO), :] = ext_dq[pl.ds(0, HALO), :]

        @pl.when(jr == 0)
        def _():
            ext_dyc[pl.ds(tm - HALO, HALO), :] = jnp.zeros((HALO, cw), F32)
            ext_dq[pl.ds(tm - HALO, HALO), :] = jnp.zeros((HALO, cw), F32)
            du_m = (conv_ref[pl.ds(1, 1), :] * ext_dyc[pl.ds(tm - HALO + 1, HALO), :]
                    + conv_ref[pl.ds(0, 1), :] * ext_dyc[pl.ds(tm - HALO + 2, HALO), :])
            dp_m = []
            for g, win in enumerate(POOL_WINDOWS):
                lanes = pl.ds(POOL_GROUP * g, POOL_GROUP)
                acc = ext_dq[pl.ds(tm - HALO + 1, HALO), lanes]
                for k in range(2, win):
                    acc = acc + ext_dq[pl.ds(tm - HALO + k, HALO), lanes]
                dp_m.append(acc * (1.0 / win))
            dz_m = jnp.concatenate([jnp.zeros((HALO, cw), F32), du_m * zm[:, 2 * cw:3 * cw], du_m * zm[:, cw:2 * cw]] + dp_m,
                                   axis=1).astype(BF16)
            acc_win[...] += _dot_tn(am_ref[...], dz_m)
            hat_m, rstd_m = _rms_stats(meta_ref[...])
            dmeta, dg1_m = _rms_bwd(hat_m, rstd_m, g1, _dot_nt(dz_m, win_ref[...]))
            dg1_ref[...] += dg1_m
            dmeta_ref[...] += dmeta

        @pl.when((b_id == n_seq - 1) & (j == nj - 1))
        def _():
            pieces = [(acc_win, zs * i, dwin_ref.at[i]) for i in range(N_DEV)]
            pieces += [(acc_wout, zs * i, dwout_ref.at[:, pl.ds(zs * i, zs)]) for i in range(d // zs)]
            copies = []
            for k, (acc, col, dst) in enumerate(pieces):
                if k >= 2:
                    copies[k - 2].wait()
                stage16[k % 2] = acc[:, pl.ds(col, zs)].astype(BF16)
                copies.append(pltpu.make_async_copy(stage16.at[k % 2], dst, sem.at[k % 2]))
                copies[k].start()
            copies[-2].wait()
            copies[-1].wait()
            finish()

    row = lambda b, j: (b * nj + nj - 1 - j, 0)
    prev = lambda b, j: (jnp.maximum((b * s + (nj - 1 - j) * tm) // HALO - 1, 0), 0)
    small = [g1.shape, g2.shape, conv_w.shape, pool_w.shape, pool_scale.shape, meta.shape]
    out = pl.pallas_call(
        body, name="mixer_backward", grid=(n_seq, nj),
        in_specs=[pl.BlockSpec((tm, d), row), pl.BlockSpec((tm, d), row), pl.BlockSpec((tm, d), row), pl.BlockSpec((tm, zw), row),
                  pl.BlockSpec((HALO, zw), prev), _const(meta.shape), _const(a_meta.shape), _const(z_meta.shape), _const(g1.shape),
                  _resident(w_in.shape), _const(conv_w.shape), _const(pool_w.shape), _const(pool_scale.shape), _resident(w_out.shape),
                  _const(g2.shape)] + [ANY] * nx,
        out_specs=[pl.BlockSpec((tm, d), row), ANY, ANY] + [_const(sh) for sh in small] + [ANY] * nx,
        out_shape=[jax.ShapeDtypeStruct((t, d), F32), jax.ShapeDtypeStruct((N_DEV, d, zs), BF16),
                   jax.ShapeDtypeStruct(w_out.shape, BF16)] + [jax.ShapeDtypeStruct(sh, F32) for sh in small]
        + [jax.ShapeDtypeStruct(a.shape, a.dtype) for a in to_exchange],
        scratch_shapes=[pltpu.VMEM((tm + HALO, cw), F32)] * 4
        + [pltpu.VMEM(w_in.shape, F32), pltpu.VMEM(w_out.shape, F32), pltpu.VMEM((2, d, zs), BF16),
           pltpu.SemaphoreType.DMA((2,))] + _exchange_sems(nx),
        compiler_params=_params("arbitrary", "arbitrary"),
    )(x2d, dh1, m, z, z, meta, a_meta, z_meta, g1, w_in, conv_w, pool_w, pool_scale, w_out, g2, *to_exchange)
    return out[:9], out[9:]


def _ffn_forward(h1, target, g3, w_gate, w_up, w_down, g4):
    t, d = h1.shape
    ff = w_gate.shape[1]
    tm = min(TM_FFN, t)

    def body(h1_ref, tgt_ref, g3_ref, wg_ref, wu_ref, wd_ref, g4_ref, f_ref, gate_ref, up_ref, dd_ref, dh2_ref, loss_ref, dg4_ref):
        @pl.when(pl.program_id(0) == 0)
        def _():
            loss_ref[...] = jnp.zeros_like(loss_ref)
            dg4_ref[...] = jnp.zeros_like(dg4_ref)

        h1v = h1_ref[...]
        hat, _ = _rms_stats(h1v)
        f = (hat * g3_ref[...]).astype(BF16)
        f_ref[...] = f
        gate = _dot(f, wg_ref[...])
        up = _dot(f, wu_ref[...])
        gate_ref[...] = gate.astype(BF16)
        up_ref[...] = up.astype(BF16)
        act = (gate * jax.nn.sigmoid(gate) * up).astype(BF16)
        d_hat, d_rstd = _rms_stats(_dot(act, wd_ref[...]))
        g4 = g4_ref[...]
        err = h1v + d_hat * g4 - tgt_ref[...]
        loss_ref[...] += jnp.sum(err * err) * (0.5 / d)
        dh2 = err * (1.0 / d)
        dh2_ref[...] = dh2
        dd, dg4 = _rms_bwd(d_hat, d_rstd, g4, dh2)
        dg4_ref[...] += dg4
        dd_ref[...] = dd.astype(BF16)

    row = lambda i: (i, 0)
    return pl.pallas_call(
        body, name="ffn_forward", grid=(t // tm,),
        in_specs=[pl.BlockSpec((tm, d), row), pl.BlockSpec((tm, d), row), _const(g3.shape), _resident(w_gate.shape),
                  _resident(w_up.shape), _resident(w_down.shape), _const(g4.shape)],
        out_specs=[pl.BlockSpec((tm, d), row), pl.BlockSpec((tm, ff), row), pl.BlockSpec((tm, ff), row), pl.BlockSpec((tm, d), row),
                   pl.BlockSpec((tm, d), row), _const((8, 128)), _const(g4.shape)],
        out_shape=[jax.ShapeDtypeStruct((t, d), BF16), jax.ShapeDtypeStruct((t, ff), BF16), jax.ShapeDtypeStruct((t, ff), BF16),
                   jax.ShapeDtypeStruct((t, d), BF16), jax.ShapeDtypeStruct((t, d), F32), jax.ShapeDtypeStruct((8, 128), F32),
                   jax.ShapeDtypeStruct(g4.shape, F32)],
        compiler_params=_params("arbitrary"),
    )(h1, target, g3, w_gate, w_up, w_down, g4)


def _ffn_backward(h1, dh2, dd, gate, up, g3, w_gate, w_up, w_down):
    t, d = h1.shape
    ff = w_gate.shape[1]
    tm = min(TM_FFN, t)

    def body(h1_ref, dh2_ref, dd_ref, gate_ref, up_ref, g3_ref, wg_ref, wu_ref, wd_ref, dh1_ref, dgate_ref, dup_ref, act_ref, dg3_ref):
        @pl.when(pl.program_id(0) == 0)
        def _():
            dg3_ref[...] = jnp.zeros_like(dg3_ref)

        dact = _dot_nt(dd_ref[...], wd_ref[...])
        gate = gate_ref[...].astype(F32)
        up = up_ref[...].astype(F32)
        sig = jax.nn.sigmoid(gate)
        silu = gate * sig
        act_ref[...] = (silu * up).astype(BF16)
        dup = (dact * silu).astype(BF16)
        dgate = (dact * up * (sig * (1.0 + gate * (1.0 - sig)))).astype(BF16)
        dup_ref[...] = dup
        dgate_ref[...] = dgate
        df = _dot_nt(dgate, wg_ref[...]) + _dot_nt(dup, wu_ref[...])
        hat, rstd = _rms_stats(h1_ref[...])
        dh1, dg3 = _rms_bwd(hat, rstd, g3_ref[...], df)
        dg3_ref[...] += dg3
        dh1_ref[...] = dh2_ref[...] + dh1

    row = lambda i: (i, 0)
    return pl.pallas_call(
        body, name="ffn_backward", grid=(t // tm,),
        in_specs=[pl.BlockSpec((tm, d), row), pl.BlockSpec((tm, d), row), pl.BlockSpec((tm, d), row), pl.BlockSpec((tm, ff), row),
                  pl.BlockSpec((tm, ff), row), _const(g3.shape), _resident(w_gate.shape), _resident(w_up.shape), _resident(w_down.shape)],
        out_specs=[pl.BlockSpec((tm, d), row), pl.BlockSpec((tm, ff), row), pl.BlockSpec((tm, ff), row), pl.BlockSpec((tm, ff), row),
                   _const(g3.shape)],
        out_shape=[jax.ShapeDtypeStruct((t, d), F32), jax.ShapeDtypeStruct((t, ff), BF16), jax.ShapeDtypeStruct((t, ff), BF16),
                   jax.ShapeDtypeStruct((t, ff), BF16), jax.ShapeDtypeStruct(g3.shape, F32)],
        compiler_params=_params("arbitrary"),
    )(h1, dh2, dd, gate, up, g3, w_gate, w_up, w_down)


def _ffn_weight_grads(f, dd, dgate, dup, act):
    t, d = f.shape
    ff = dgate.shape[1]
    tm = min(TM_FFN, t)
    nt = t // tm
    fc = ff // FF_CHUNKS
    fs = ff // N_DEV
    per = N_DEV // FF_CHUNKS

    def body(f_ref, dd_ref, dgate_ref, dup_ref, act_ref, dwg_ref, dwu_ref, dwd_ref, acc_g, acc_u, acc_d, stage, stage_d, sem):
        c, i = pl.program_id(0), pl.program_id(1)

        @pl.when(i == 0)
        def _():
            acc_g[...] = jnp.zeros_like(acc_g)
            acc_u[...] = jnp.zeros_like(acc_u)
            acc_d[...] = jnp.zeros_like(acc_d)

        fv = f_ref[...]
        acc_g[...] += _dot_tn(fv, dgate_ref[...])
        acc_u[...] += _dot_tn(fv, dup_ref[...])
        acc_d[...] += _dot_tn(act_ref[...], dd_ref[...])

        @pl.when(i == nt - 1)
        def _():
            stage_d[...] = acc_d[...].astype(BF16)
            down = pltpu.make_async_copy(stage_d, dwd_ref.at[pl.ds(pl.multiple_of(c * fc, 16), fc), :], sem.at[0])
            down.start()
            for acc, out in ((acc_g, dwg_ref), (acc_u, dwu_ref)):
                for k in range(per):
                    stage[...] = acc[:, pl.ds(fs * k, fs)].astype(BF16)
                    cp = pltpu.make_async_copy(stage, out.at[c * per + k], sem.at[1])
                    cp.start()
                    cp.wait()
            down.wait()

    row = lambda c, i: (i, 0)
    col = lambda c, i: (i, c)
    return pl.pallas_call(
        body, name="ffn_weight_grads", grid=(FF_CHUNKS, nt),
        in_specs=[pl.BlockSpec((tm, d), row), pl.BlockSpec((tm, d), row), pl.BlockSpec((tm, fc), col), pl.BlockSpec((tm, fc), col),
                  pl.BlockSpec((tm, fc), col)],
        out_specs=[ANY, ANY, ANY],
        out_shape=[jax.ShapeDtypeStruct((N_DEV, d, fs), BF16), jax.ShapeDtypeStruct((N_DEV, d, fs), BF16),
                   jax.ShapeDtypeStruct((ff, d), BF16)],
        scratch_shapes=[pltpu.VMEM((d, fc), F32), pltpu.VMEM((d, fc), F32), pltpu.VMEM((fc, d), F32), pltpu.VMEM((d, fs), BF16),
                        pltpu.VMEM((fc, d), BF16), pltpu.SemaphoreType.DMA((2,))],
        compiler_params=_params("arbitrary", "arbitrary"),
    )(f, dd, dgate, dup, act)


def _adamw(w, g, m, v):
    m = ADAM_B1 * m + (1.0 - ADAM_B1) * g
    v = ADAM_B2 * v + (1.0 - ADAM_B2) * (g * g)
    m_hat = m / (1.0 - ADAM_B1 ** ADAM_STEP)
    v_hat = v / (1.0 - ADAM_B2 ** ADAM_STEP)
    return -ADAM_LR * (m_hat / (jnp.sqrt(v_hat) + ADAM_EPS) + ADAM_WD * w), m, v


def _sum_slabs(ref):
    total = ref[0].astype(F32)
    for i in range(1, N_DEV):
        total = total + ref[i].astype(F32)
    return total


def _reduce_adamw(parts, w, m, v, name):
    r, c = w.shape
    tr = r
    for cand in range(8, r, 8):
        if r % cand == 0 and cand * c <= ADAMW_BLOCK_ELEMS:
            tr = cand
    if r * c <= ADAMW_BLOCK_ELEMS:
        tr = r

    def body(p_ref, w_ref, m_ref, v_ref, g_out, d_out, m_out, v_out):
        g = _sum_slabs(p_ref)
        g_out[...] = g
        d_out[...], m_out[...], v_out[...] = _adamw(w_ref[...], g, m_ref[...], v_ref[...])

    blk = pl.BlockSpec((tr, c), lambda i: (i, 0))
    return pl.pallas_call(
        body, name=name, grid=(r // tr,),
        in_specs=[pl.BlockSpec((N_DEV, tr, c), lambda i: (0, i, 0)), blk, blk, blk],
        out_specs=[blk] * 4, out_shape=[jax.ShapeDtypeStruct((r, c), F32)] * 4,
        compiler_params=_params("arbitrary"),
    )(parts, w, m, v)


def _reduce_adamw_small(parts, ws, ms, vs):
    n = len(parts)

    def body(*refs):
        p_refs, w_refs, m_refs, v_refs = (refs[k * n:(k + 1) * n] for k in range(4))
        outs = refs[4 * n:]
        for a in range(n):
            g = _sum_slabs(p_refs[a])
            outs[4 * a][...] = g
            outs[4 * a + 1][...], outs[4 * a + 2][...], outs[4 * a + 3][...] = _adamw(w_refs[a][...], g, m_refs[a][...], v_refs[a][...])

    out = pl.pallas_call(
        body, name="adamw_replicated",
        out_shape=[jax.ShapeDtypeStruct(w.shape, F32) for w in ws for _ in range(4)],
        compiler_params=pltpu.CompilerParams(vmem_limit_bytes=VMEM_LIMIT_BYTES),
    )(*parts, *ws, *ms, *vs)
    return [tuple(out[4 * a:4 * a + 4]) for a in range(n)]


def kernel(x, meta_tokens, norm_mix_pre, w_in, conv_w, pool_w, pool_scale, w_out, norm_mix_post, norm_ffn_pre, w_gate, w_up, w_down, norm_ffn_post, loss_target, m_meta_tokens, m_norm_mix_pre, m_w_in, m_conv_w, m_pool_w, m_pool_scale, m_w_out, m_norm_mix_post, m_norm_ffn_pre, m_w_gate, m_w_up, m_w_down, m_norm_ffn_post, v_meta_tokens, v_norm_mix_pre, v_w_in, v_conv_w, v_pool_w, v_pool_scale, v_w_out, v_norm_mix_post, v_norm_ffn_pre, v_w_gate, v_w_up, v_w_down, v_norm_ffn_post):
    n_seq, seq, d = x.shape
    x2d = x.reshape(n_seq * seq, d)
    target = loss_target.reshape(n_seq * seq, d)

    (win_s, wout_s, meta_s, conv_s), ffn_shards = _gather_first_weights(
        [w_in[0], w_out[0], meta_tokens, conv_w[0]], [BF16, BF16, F32, F32], [w_gate[0], w_up[0], w_down[0]])
    (win_b,) = _columns_from_slabs([win_s])
    wout_b = wout_s.reshape(d, d)
    meta = jnp.transpose(meta_s, (1, 0, 2)).reshape(N_META, d)
    conv = jnp.transpose(conv_s, (1, 0, 2)).reshape(CONV_WIDTH, -1)
    pw, ps = pool_w[0], pool_scale

    a_meta, z_meta = _meta_forward(meta, norm_mix_pre, win_b)
    (h1, z, m), (wg_s, wu_s, wd_s) = _mixer_forward(x2d, z_meta, norm_mix_pre, win_b, conv, pw, ps, wout_b, norm_mix_post, n_seq, ffn_shards)
    wg_b, wu_b = _columns_from_slabs([wg_s, wu_s])
    wd_b = wd_s.reshape(-1, d)
    f, gate, up, dd, dh2, loss_sum, dg4 = _ffn_forward(h1, target, norm_ffn_pre, wg_b, wu_b, wd_b, norm_ffn_post)
    dh1, dgate, dup, act, dg3 = _ffn_backward(h1, dh2, dd, gate, up, norm_ffn_pre, wg_b, wu_b, wd_b)
    dwg, dwu, dwd = _ffn_weight_grads(f, dd, dgate, dup, act)
    (gx, dwin, dwout, dg1, dg2, dconv, dpw, dps, dmeta), ffn_parts = _mixer_backward(
        x2d, dh1, m, z, meta, a_meta, z_meta, norm_mix_pre, win_b, conv, pw, ps, wout_b, norm_mix_post, n_seq,
        [dwg, dwu, dwd.reshape(N_DEV, -1, d)])

    dmeta_s = jnp.transpose(dmeta.reshape(N_META, N_DEV, -1), (1, 0, 2))
    dconv_s = jnp.transpose(dconv.reshape(CONV_WIDTH, N_DEV, -1), (1, 0, 2))
    last = _exchange([dwin, dwout.reshape(N_DEV, -1, d), dmeta_s, dconv_s, dg1, dg2, dg3, dg4, dpw, dps],
                     [False] * 4 + [True] * 6, "exchange_last_grads")
    sharded = [last[0], last[1], *ffn_parts, last[2], last[3]]
    replicated = last[4:]

    names = ["meta_tokens", "norm_mix_pre", "w_in", "conv_w", "pool_w", "pool_scale", "w_out", "norm_mix_post", "norm_ffn_pre", "w_gate",
             "w_up", "w_down", "norm_ffn_post"]
    res = {}
    for nm, parts, w, m_, v_ in (("w_in", sharded[0], w_in, m_w_in, v_w_in), ("w_out", sharded[1], w_out, m_w_out, v_w_out),
                                 ("w_gate", sharded[2], w_gate, m_w_gate, v_w_gate), ("w_up", sharded[3], w_up, m_w_up, v_w_up),
                                 ("w_down", sharded[4], w_down, m_w_down, v_w_down), ("conv_w", sharded[6], conv_w, m_conv_w, v_conv_w)):
        res[nm] = tuple(o[None] for o in _reduce_adamw(parts, w[0], m_[0], v_[0], "adamw_" + nm))
    res["meta_tokens"] = tuple(_reduce_adamw(sharded[5], meta_tokens, m_meta_tokens, v_meta_tokens, "adamw_meta_tokens"))
    small = _reduce_adamw_small(
        replicated, [norm_mix_pre, norm_mix_post, norm_ffn_pre, norm_ffn_post, pool_w[0], pool_scale],
        [m_norm_mix_pre, m_norm_mix_post, m_norm_ffn_pre, m_norm_ffn_post, m_pool_w[0], m_pool_scale],
        [v_norm_mix_pre, v_norm_mix_post, v_norm_ffn_pre, v_norm_ffn_post, v_pool_w[0], v_pool_scale])
    for nm, r in zip(["norm_mix_pre", "norm_mix_post", "norm_ffn_pre", "norm_ffn_post", "pool_w", "pool_scale"], small):
        res[nm] = tuple(o[None] for o in r) if nm == "pool_w" else r

    loss = lax.psum(loss_sum[0, 0], MESH_AXES)
    return (loss, gx.reshape(n_seq, seq, d), *[res[nm][0] for nm in names], *[res[nm][1] for nm in names],
            *[res[nm][2] for nm in names], *[res[nm][3] for nm in names])
```

```python
import functools

import jax
import jax.numpy as jnp
from jax import lax
from jax.experimental import pallas as pl
from jax.experimental.pallas import tpu as pltpu

F32, BF16 = jnp.float32, jnp.bfloat16
RMS_EPS = 1e-6
N_META = 16
CONV_WIDTH = 3
POOL_WINDOWS = (2, 4, 8, 16)
POOL_GROUP = 128
HALO = 16
N_DEV = 8
MESH_AXES = ("x", "y", "c")
MESH = pl.DeviceIdType.MESH
VMEM_LIMIT_BYTES = 56 * 1024 * 1024
ADAMW_BLOCK_ELEMS = 64 * 1024
TM_MIX = 512
TM_FFN = 256
FF_CHUNKS = 2

ADAM_LR, ADAM_B1, ADAM_B2, ADAM_EPS, ADAM_WD, ADAM_STEP = 0.001, 0.9, 0.999, 1e-08, 0.01, 10


def _dot(a, b):
    return jnp.dot(a, b, preferred_element_type=F32)


def _dot_nt(a, b):
    return lax.dot_general(a, b, (((1,), (1,)), ((), ())), preferred_element_type=F32)


def _dot_tn(a, b):
    return lax.dot_general(a, b, (((0,), (0,)), ((), ())), preferred_element_type=F32)


def _rms_stats(h):
    rstd = lax.rsqrt(jnp.mean(h * h, axis=-1, keepdims=True) + RMS_EPS)
    return h * rstd, rstd


def _rms_bwd(hat, rstd, g, dy):
    gdy = dy * g
    proj = jnp.mean(gdy * hat, axis=-1, keepdims=True)
    return rstd * (gdy - hat * proj), jnp.sum(dy * hat, axis=0, keepdims=True)


def _params(*semantics):
    return pltpu.CompilerParams(dimension_semantics=semantics or None, vmem_limit_bytes=VMEM_LIMIT_BYTES)


def _resident(shape):
    zeros = (0,) * len(shape)
    return pl.BlockSpec(shape, lambda *_: zeros, pipeline_mode=pl.Buffered(1))


def _const(shape):
    zeros = (0,) * len(shape)
    return pl.BlockSpec(shape, lambda *_: zeros)


ANY = pl.BlockSpec(memory_space=pl.ANY)


def _my_place():
    x, y, c = (lax.axis_index(a) for a in MESH_AXES)
    return x, y, c


def _exchange_sems(n):
    return [pltpu.SemaphoreType.DMA((n, N_DEV - 1)), pltpu.SemaphoreType.DMA((n, N_DEV - 1)), pltpu.SemaphoreType.DMA((n,))]


def _gather_ops(srcs, outs, send_sems, recv_sems, local_sems):
    n = len(srcs)
    x, y, c = _my_place()
    me, sibling = (x, y, c), (x, y, 1 - c)
    chips = [(1 - x, y), (x, 1 - y), (1 - x, 1 - y)]

    def slab(px, py, pc):
        return 4 * px + 2 * py + pc

    def copy(a, k, block, to, src=None):
        dst = outs[a].at[slab(*block)]
        return pltpu.make_async_remote_copy(
            src_ref=dst if src is None else src, dst_ref=dst, send_sem=send_sems.at[a, k], recv_sem=recv_sems.at[a, k],
            device_id=to, device_id_type=MESH)

    def mine(a):
        return pltpu.make_async_copy(srcs[a], outs[a].at[slab(*me)], local_sems.at[a])

    def first(a):
        return [copy(a, 0, me, sibling, src=srcs[a])] + [copy(a, 1 + j, me, (*chip, c), src=srcs[a]) for j, chip in enumerate(chips)]

    def passed(a, j):
        return copy(a, 4 + j, (*chips[j], c), sibling)

    def start():
        for a in range(n):
            mine(a).start()
            for cp in first(a):
                cp.start()

    def forward():
        for j, chip in enumerate(chips):
            for a in range(n):
                copy(a, 1 + j, (*chip, c), me).wait_recv()
                passed(a, j).start()

    def finish():
        for a in range(n):
            copy(a, 0, sibling, me).wait_recv()
            for j, chip in enumerate(chips):
                copy(a, 4 + j, (*chip, 1 - c), me).wait_recv()
        for a in range(n):
            for cp in first(a) + [passed(a, j) for j in range(len(chips))]:
                cp.wait_send()
            mine(a).wait()

    return start, forward, finish


def _exchange_ops(ins, outs, whole, send_sems, recv_sems, local_sems):
    n = len(ins)
    x, y, c = _my_place()
    me = 4 * x + 2 * y + c

    def src(a, i):
        return ins[a] if whole[a] else ins[a].at[i]

    def mine(a):
        return pltpu.make_async_copy(src(a, me), outs[a].at[me], local_sems.at[a])

    def send(a, k):
        to = (me + k) % N_DEV
        return pltpu.make_async_remote_copy(
            src_ref=src(a, to), dst_ref=outs[a].at[me], send_sem=send_sems.at[a, k - 1], recv_sem=recv_sems.at[a, k - 1],
            device_id=(to // 4, (to // 2) % 2, to % 2), device_id_type=MESH)

    def landed(a, k):
        frm = (me + N_DEV - k) % N_DEV
        return pltpu.make_async_remote_copy(
            src_ref=src(a, frm), dst_ref=outs[a].at[frm], send_sem=send_sems.at[a, k - 1], recv_sem=recv_sems.at[a, k - 1],
            device_id=(x, y, c), device_id_type=MESH)

    def start():
        for a in range(n):
            mine(a).start()
            for k in range(1, N_DEV):
                send(a, k).start()

    def finish():
        for a in range(n):
            for k in range(1, N_DEV):
                landed(a, k).wait_recv()
        for a in range(n):
            for k in range(1, N_DEV):
                send(a, k).wait_send()
            mine(a).wait()

    return start, finish


def _gather_first_weights(gathered, dtypes, cast_only):
    n, k = len(gathered), len(cast_only)

    def body(*refs):
        ins, casts_in = refs[:n], refs[n:n + k]
        outs, casts_out = refs[n + k:2 * n + k], refs[2 * n + k:2 * n + 2 * k]
        stages = refs[2 * n + 2 * k:3 * n + 2 * k]
        start, forward, finish = _gather_ops(stages, outs, *refs[3 * n + 2 * k:])
        for a in range(n):
            stages[a][...] = ins[a][...].astype(stages[a].dtype)
        start()
        for a in range(k):
            casts_out[a][...] = casts_in[a][...].astype(BF16)
        forward()
        finish()

    vmem = pl.BlockSpec(memory_space=pltpu.VMEM)
    out = pl.pallas_call(
        body, name="gather_first_weights",
        out_shape=[jax.ShapeDtypeStruct((N_DEV, *s.shape), d) for s, d in zip(gathered, dtypes)]
        + [jax.ShapeDtypeStruct(s.shape, BF16) for s in cast_only],
        in_specs=[vmem] * (n + k), out_specs=[ANY] * n + [vmem] * k,
        scratch_shapes=[pltpu.VMEM(s.shape, d) for s, d in zip(gathered, dtypes)] + _exchange_sems(n),
        compiler_params=pltpu.CompilerParams(vmem_limit_bytes=VMEM_LIMIT_BYTES),
    )(*gathered, *cast_only)
    return out[:n], out[n:]


def _exchange(arrays, whole, name):
    n = len(arrays)

    def body(*refs):
        start, finish = _exchange_ops(refs[:n], refs[n:2 * n], whole, *refs[2 * n:])
        start()
        finish()

    return pl.pallas_call(
        body, name=name,
        out_shape=[jax.ShapeDtypeStruct((N_DEV, *a.shape) if w else a.shape, a.dtype) for a, w in zip(arrays, whole)],
        in_specs=[ANY] * n, out_specs=[ANY] * n, scratch_shapes=_exchange_sems(n),
    )(*arrays)


def _columns_from_slabs(slabs):
    def body(*refs):
        k = len(refs) // 2
        for src, dst in zip(refs[:k], refs[k:]):
            n = src.shape[2]
            for i in range(N_DEV):
                dst[:, pl.ds(n * i, n)] = src[i]

    return pl.pallas_call(
        body, name="columns_from_slabs",
        out_shape=[jax.ShapeDtypeStruct((s.shape[1], N_DEV * s.shape[2]), s.dtype) for s in slabs],
        compiler_params=pltpu.CompilerParams(vmem_limit_bytes=VMEM_LIMIT_BYTES),
    )(*slabs)


def _mixer_core(z, ext_u, ext_p, conv_ref, pool_w_ref, tm):
    c_w = z.shape[1] // 4
    b, c, v, p = z[:, :c_w], z[:, c_w:2 * c_w], z[:, 2 * c_w:3 * c_w], z[:, 3 * c_w:]
    u = c * v
    ext_u[pl.ds(HALO, tm), :] = u
    ext_p[pl.ds(HALO, tm), :] = p
    u1 = ext_u[pl.ds(HALO - 1, tm), :]
    u2 = ext_u[pl.ds(HALO - 2, tm), :]
    yc = conv_ref[pl.ds(2, 1), :] * u + conv_ref[pl.ds(1, 1), :] * u1 + conv_ref[pl.ds(0, 1), :] * u2
    pooled, mixed = [], []
    for g, win in enumerate(POOL_WINDOWS):
        lanes = pl.ds(POOL_GROUP * g, POOL_GROUP)
        pg = p[:, POOL_GROUP * g:POOL_GROUP * (g + 1)]
        s = pg
        for k in range(1, win):
            s = s + ext_p[pl.ds(HALO - k, tm), lanes]
        pooled.append((s * (1.0 / win) - pg).astype(BF16))
        mixed.append(_dot(pooled[-1], pool_w_ref[g].astype(BF16)))
    return b, c, v, u, u1, u2, yc, pooled, mixed


def _meta_forward(meta, g1, w_in):
    def body(meta_ref, g1_ref, w_ref, a_ref, z_ref):
        hat, _ = _rms_stats(meta_ref[...])
        a = (hat * g1_ref[...]).astype(BF16)
        a_ref[...] = a
        z_ref[...] = _dot(a, w_ref[...])

    return pl.pallas_call(
        body, name="meta_forward",
        out_shape=[jax.ShapeDtypeStruct(meta.shape, BF16), jax.ShapeDtypeStruct((N_META, w_in.shape[1]), F32)],
        compiler_params=pltpu.CompilerParams(vmem_limit_bytes=VMEM_LIMIT_BYTES),
    )(meta, g1, w_in)


def _mixer_forward(x2d, z_meta, g1, w_in, conv_w, pool_w, pool_scale, w_out, g2, n_seq, to_gather):
    t, d = x2d.shape
    zw = w_in.shape[1]
    cw = zw // 4
    s = t // n_seq
    tm = min(TM_MIX, s)
    nj = s // tm
    ng = len(to_gather)

    def body(x_ref, zm_ref, g1_ref, win_ref, conv_ref, pw_ref, ps_ref, wout_ref, g2_ref, *rest):
        shards, (h1_ref, z_ref, m_ref), slabs = rest[:ng], rest[ng:ng + 3], rest[ng + 3:2 * ng + 3]
        ext_u, ext_p = rest[2 * ng + 3:2 * ng + 5]
        start, forward, finish = _gather_ops(shards, slabs, *rest[2 * ng + 5:])
        pl.when((pl.program_id(0) == 0) & (pl.program_id(1) == 0))(start)

        @pl.when(pl.program_id(1) == 0)
        def _():
            zm = zm_ref[...]
            ext_u[pl.ds(0, HALO), :] = zm[:, cw:2 * cw] * zm[:, 2 * cw:3 * cw]
            ext_p[pl.ds(0, HALO), :] = zm[:, 3 * cw:]

        h0 = x_ref[...]
        hat, _ = _rms_stats(h0)
        z = _dot((hat * g1_ref[...]).astype(BF16), win_ref[...])
        z_ref[...] = z.astype(BF16)
        b, _, _, _, _, _, yc, _, mixed = _mixer_core(z, ext_u, ext_p, conv_ref, pw_ref, tm)
        ps = ps_ref[...]
        y = [b * yc] + [mixed[g] * ps[:, POOL_GROUP * g:POOL_GROUP * (g + 1)] for g in range(len(POOL_WINDOWS))]
        m = _dot(jnp.concatenate(y, axis=1).astype(BF16), wout_ref[...])
        m_ref[...] = m
        m_hat, _ = _rms_stats(m)
        h1_ref[...] = h0 + m_hat * g2_ref[...]
        ext_u[pl.ds(0, HALO), :] = ext_u[pl.ds(tm, HALO), :]
        ext_p[pl.ds(0, HALO), :] = ext_p[pl.ds(tm, HALO), :]

        @pl.when((pl.program_id(0) == n_seq - 1) & (pl.program_id(1) == nj - 1))
        def _():
            forward()
            finish()

    row = lambda b, j: (b * nj + j, 0)
    out = pl.pallas_call(
        body, name="mixer_forward", grid=(n_seq, nj),
        in_specs=[pl.BlockSpec((tm, d), row), _const(z_meta.shape), _const(g1.shape), _resident(w_in.shape), _const(conv_w.shape),
                  _const(pool_w.shape), _const(pool_scale.shape), _resident(w_out.shape), _const(g2.shape)] + [ANY] * ng,
        out_specs=[pl.BlockSpec((tm, d), row), pl.BlockSpec((tm, zw), row), pl.BlockSpec((tm, d), row)] + [ANY] * ng,
        out_shape=[jax.ShapeDtypeStruct((t, d), F32), jax.ShapeDtypeStruct((t, zw), BF16), jax.ShapeDtypeStruct((t, d), F32)]
        + [jax.ShapeDtypeStruct((N_DEV, *a.shape), a.dtype) for a in to_gather],
        scratch_shapes=[pltpu.VMEM((tm + HALO, cw), F32), pltpu.VMEM((tm + HALO, cw), F32)] + _exchange_sems(ng),
        compiler_params=_params("arbitrary", "arbitrary"),
    )(x2d, z_meta, g1, w_in, conv_w, pool_w, pool_scale, w_out, g2, *to_gather)
    return out[:3], out[3:]


def _mixer_backward(x2d, dh1, m, z, meta, a_meta, z_meta, g1, w_in, conv_w, pool_w, pool_scale, w_out, g2, n_seq, to_exchange):
    t, d = x2d.shape
    zw = w_in.shape[1]
    cw = zw // 4
    s = t // n_seq
    tm = min(TM_MIX, s)
    nj = s // tm
    n_groups = len(POOL_WINDOWS)
    zs = zw // N_DEV
    nx = len(to_exchange)

    def body(x_ref, dh1_ref, m_ref, z_ref, zprev_ref, meta_ref, am_ref, zm_ref, g1_ref, win_ref, conv_ref, pw_ref, ps_ref, wout_ref,
             g2_ref, *rest):
        sent, rest = rest[:nx], rest[nx:]
        gx_ref, dwin_ref, dwout_ref, dg1_ref, dg2_ref, dconv_ref, dpw_ref, dps_ref, dmeta_ref = rest[:9]
        landed, rest = rest[9:9 + nx], rest[9 + nx:]
        ext_u, ext_p, ext_dyc, ext_dq, acc_win, acc_wout, stage16, sem = rest[:8]
        start, finish = _exchange_ops(sent, landed, [False] * nx, *rest[8:])
        b_id, j = pl.program_id(0), pl.program_id(1)
        jr = nj - 1 - j
        pl.when((b_id == 0) & (j == 0))(start)

        @pl.when((b_id == 0) & (j == 0))
        def _():
            acc_win[...] = jnp.zeros_like(acc_win)
            acc_wout[...] = jnp.zeros_like(acc_wout)
            for r in (dg1_ref, dg2_ref, dconv_ref, dpw_ref, dps_ref, dmeta_ref):
                r[...] = jnp.zeros_like(r)

        @pl.when(j == 0)
        def _():
            ext_dyc[pl.ds(tm, HALO), :] = jnp.zeros((HALO, cw), F32)
            ext_dq[pl.ds(tm, HALO), :] = jnp.zeros((HALO, cw), F32)

        zm = zm_ref[...]
        halo = jnp.where(jr == 0, zm, zprev_ref[...].astype(F32))
        ext_u[pl.ds(0, HALO), :] = halo[:, cw:2 * cw] * halo[:, 2 * cw:3 * cw]
        ext_p[pl.ds(0, HALO), :] = halo[:, 3 * cw:]

        h0 = x_ref[...]
        hat0, rstd0 = _rms_stats(h0)
        g1 = g1_ref[...]
        a = (hat0 * g1).astype(BF16)
        b, c, v, u, u1, u2, yc, pooled, mixed = _mixer_core(z_ref[...].astype(F32), ext_u, ext_p, conv_ref, pw_ref, tm)
        ps = ps_ref[...]
        y = [b * yc] + [mixed[g] * ps[:, POOL_GROUP * g:POOL_GROUP * (g + 1)] for g in range(n_groups)]
        ycat = jnp.concatenate(y, axis=1).astype(BF16)

        dh1v = dh1_ref[...]
        m_hat, m_rstd = _rms_stats(m_ref[...])
        dm, dg2 = _rms_bwd(m_hat, m_rstd, g2_ref[...], dh1v)
        dg2_ref[...] += dg2
        dm = dm.astype(BF16)
        acc_wout[...] += _dot_tn(ycat, dm)
        dycat = _dot_nt(dm, wout_ref[...])

        dyconv = dycat[:, :cw]
        db = dyconv * yc
        dyc = dyconv * b
        ext_dyc[pl.ds(0, tm), :] = dyc
        du = (conv_ref[pl.ds(2, 1), :] * dyc + conv_ref[pl.ds(1, 1), :] * ext_dyc[pl.ds(1, tm), :]
              + conv_ref[pl.ds(0, 1), :] * ext_dyc[pl.ds(2, tm), :])
        dconv_ref[pl.ds(2, 1), :] += jnp.sum(dyc * u, axis=0, keepdims=True)
        dconv_ref[pl.ds(1, 1), :] += jnp.sum(dyc * u1, axis=0, keepdims=True)
        dconv_ref[pl.ds(0, 1), :] += jnp.sum(dyc * u2, axis=0, keepdims=True)

        dp = []
        for g, win in enumerate(POOL_WINDOWS):
            lanes = pl.ds(POOL_GROUP * g, POOL_GROUP)
            dypool = dycat[:, cw + POOL_GROUP * g:cw + POOL_GROUP * (g + 1)]
            dps_ref[:, lanes] += jnp.sum(dypool * mixed[g], axis=0, keepdims=True)
            dmixed = (dypool * ps[:, POOL_GROUP * g:POOL_GROUP * (g + 1)]).astype(BF16)
            dpw_ref[g] += _dot_tn(pooled[g], dmixed)
            dq = _dot_nt(dmixed, pw_ref[g].astype(BF16))
            ext_dq[pl.ds(0, tm), lanes] = dq
            acc = dq
            for k in range(1, win):
                acc = acc + ext_dq[pl.ds(k, tm), lanes]
            dp.append(acc * (1.0 / win) - dq)

        dz = jnp.concatenate([db, du * v, du * c] + dp, axis=1).astype(BF16)
        acc_win[...] += _dot_tn(a, dz)
        dh0, dg1 = _rms_bwd(hat0, rstd0, g1, _dot_nt(dz, win_ref[...]))
        dg1_ref[...] += dg1
        gx_ref[...] = dh1v + dh0

        ext_dyc[pl.ds(tm, HALO), :] = ext_dyc[pl.ds(0, HALO), :]
        ext_dq[pl.ds(tm, HALO), :] = ext_dq[pl.ds(0, HALO), :]

        @pl.when(jr == 0)
        def _():
            ext_dyc[pl.ds(tm - HALO, HALO), :] = jnp.zeros((HALO, cw), F32)
            ext_dq[pl.ds(tm - HALO, HALO), :] = jnp.zeros((HALO, cw), F32)
            du_m = (conv_ref[pl.ds(1, 1), :] * ext_dyc[pl.ds(tm - HALO + 1, HALO), :]
                    + conv_ref[pl.ds(0, 1), :] * ext_dyc[pl.ds(tm - HALO + 2, HALO), :])
            dp_m = []
            for g, win in enumerate(POOL_WINDOWS):
                lanes = pl.ds(POOL_GROUP * g, POOL_GROUP)
                acc = ext_dq[pl.ds(tm - HALO + 1, HALO), lanes]
                for k in range(2, win):
                    acc = acc + ext_dq[pl.ds(tm - HALO + k, HALO), lanes]
                dp_m.append(acc * (1.0 / win))
            dz_m = jnp.concatenate([jnp.zeros((HALO, cw), F32), du_m * zm[:, 2 * cw:3 * cw], du_m * zm[:, cw:2 * cw]] + dp_m,
                                   axis=1).astype(BF16)
            acc_win[...] += _dot_tn(am_ref[...], dz_m)
            hat_m, rstd_m = _rms_stats(meta_ref[...])
            dmeta, dg1_m = _rms_bwd(hat_m, rstd_m, g1, _dot_nt(dz_m, win_ref[...]))
            dg1_ref[...] += dg1_m
            dmeta_ref[...] += dmeta

        @pl.when((b_id == n_seq - 1) & (j == nj - 1))
        def _():
            pieces = [(acc_win, zs * i, dwin_ref.at[i]) for i in range(N_DEV)]
            pieces += [(acc_wout, zs * i, dwout_ref.at[:, pl.ds(zs * i, zs)]) for i in range(d // zs)]
            copies = []
            for k, (acc, col, dst) in enumerate(pieces):
                if k >= 2:
                    copies[k - 2].wait()
                stage16[k % 2] = acc[:, pl.ds(col, zs)].astype(BF16)
                copies.append(pltpu.make_async_copy(stage16.at[k % 2], dst, sem.at[k % 2]))
                copies[k].start()
            copies[-2].wait()
            copies[-1].wait()
            finish()

    row = lambda b, j: (b * nj + nj - 1 - j, 0)
    prev = lambda b, j: (jnp.maximum((b * s + (nj - 1 - j) * tm) // HALO - 1, 0), 0)
    small = [g1.shape, g2.shape, conv_w.shape, pool_w.shape, pool_scale.shape, meta.shape]
    out = pl.pallas_call(
        body, name="mixer_backward", grid=(n_seq, nj),
        in_specs=[pl.BlockSpec((tm, d), row), pl.BlockSpec((tm, d), row), pl.BlockSpec((tm, d), row), pl.BlockSpec((tm, zw), row),
                  pl.BlockSpec((HALO, zw), prev), _const(meta.shape), _const(a_meta.shape), _const(z_meta.shape), _const(g1.shape),
                  _resident(w_in.shape), _const(conv_w.shape), _const(pool_w.shape), _const(pool_scale.shape), _resident(w_out.shape),
                  _const(g2.shape)] + [ANY] * nx,
        out_specs=[pl.BlockSpec((tm, d), row), ANY, ANY] + [_const(sh) for sh in small] + [ANY] * nx,
        out_shape=[jax.ShapeDtypeStruct((t, d), F32), jax.ShapeDtypeStruct((N_DEV, d, zs), BF16),
                   jax.ShapeDtypeStruct(w_out.shape, BF16)] + [jax.ShapeDtypeStruct(sh, F32) for sh in small]
        + [jax.ShapeDtypeStruct(a.shape, a.dtype) for a in to_exchange],
        scratch_shapes=[pltpu.VMEM((tm + HALO, cw), F32)] * 4
        + [pltpu.VMEM(w_in.shape, F32), pltpu.VMEM(w_out.shape, F32), pltpu.VMEM((2, d, zs), BF16),
           pltpu.SemaphoreType.DMA((2,))] + _exchange_sems(nx),
        compiler_params=_params("arbitrary", "arbitrary"),
    )(x2d, dh1, m, z, z, meta, a_meta, z_meta, g1, w_in, conv_w, pool_w, pool_scale, w_out, g2, *to_exchange)
    return out[:9], out[9:]


def _ffn_forward(h1, target, g3, w_gate, w_up, w_down, g4):
    t, d = h1.shape
    ff = w_gate.shape[0]
    tm = min(TM_FFN, t)

    def body(h1_ref, tgt_ref, g3_ref, wg_ref, wu_ref, wd_ref, g4_ref, f_ref, gate_ref, up_ref, dd_ref, dh2_ref, loss_ref, dg4_ref):
        @pl.when(pl.program_id(0) == 0)
        def _():
            loss_ref[...] = jnp.zeros_like(loss_ref)
            dg4_ref[...] = jnp.zeros_like(dg4_ref)

        h1v = h1_ref[...]
        hat, _ = _rms_stats(h1v)
        f = (hat * g3_ref[...]).astype(BF16)
        f_ref[...] = f
        gate = _dot_nt(f, wg_ref[...])
        up = _dot_nt(f, wu_ref[...])
        gate_ref[...] = gate.astype(BF16)
        up_ref[...] = up.astype(BF16)
        act = (gate * jax.nn.sigmoid(gate) * up).astype(BF16)
        d_hat, d_rstd = _rms_stats(_dot(act, wd_ref[...]))
        g4 = g4_ref[...]
        err = h1v + d_hat * g4 - tgt_ref[...]
        loss_ref[...] += jnp.sum(err * err) * (0.5 / d)
        dh2 = err * (1.0 / d)
        dh2_ref[...] = dh2
        dd, dg4 = _rms_bwd(d_hat, d_rstd, g4, dh2)
        dg4_ref[...] += dg4
        dd_ref[...] = dd.astype(BF16)

    row = lambda i: (i, 0)
    return pl.pallas_call(
        body, name="ffn_forward", grid=(t // tm,),
        in_specs=[pl.BlockSpec((tm, d), row), pl.BlockSpec((tm, d), row), _const(g3.shape), _resident(w_gate.shape),
                  _resident(w_up.shape), _resident(w_down.shape), _const(g4.shape)],
        out_specs=[pl.BlockSpec((tm, d), row), pl.BlockSpec((tm, ff), row), pl.BlockSpec((tm, ff), row), pl.BlockSpec((tm, d), row),
                   pl.BlockSpec((tm, d), row), _const((8, 128)), _const(g4.shape)],
        out_shape=[jax.ShapeDtypeStruct((t, d), BF16), jax.ShapeDtypeStruct((t, ff), BF16), jax.ShapeDtypeStruct((t, ff), BF16),
                   jax.ShapeDtypeStruct((t, d), BF16), jax.ShapeDtypeStruct((t, d), F32), jax.ShapeDtypeStruct((8, 128), F32),
                   jax.ShapeDtypeStruct(g4.shape, F32)],
        compiler_params=_params("arbitrary"),
    )(h1, target, g3, w_gate, w_up, w_down, g4)


def _ffn_backward(h1, dh2, dd, gate, up, g3, w_gate, w_up, w_down):
    t, d = h1.shape
    ff = w_gate.shape[0]
    tm = min(TM_FFN, t)

    def body(h1_ref, dh2_ref, dd_ref, gate_ref, up_ref, g3_ref, wg_ref, wu_ref, wd_ref, dh1_ref, dgate_ref, dup_ref, act_ref, dg3_ref):
        @pl.when(pl.program_id(0) == 0)
        def _():
            dg3_ref[...] = jnp.zeros_like(dg3_ref)

        dact = _dot_nt(dd_ref[...], wd_ref[...])
        gate = gate_ref[...].astype(F32)
        up = up_ref[...].astype(F32)
        sig = jax.nn.sigmoid(gate)
        silu = gate * sig
        act_ref[...] = (silu * up).astype(BF16)
        dup = (dact * silu).astype(BF16)
        dgate = (dact * up * (sig * (1.0 + gate * (1.0 - sig)))).astype(BF16)
        dup_ref[...] = dup
        dgate_ref[...] = dgate
        df = _dot(dgate, wg_ref[...]) + _dot(dup, wu_ref[...])
        hat, rstd = _rms_stats(h1_ref[...])
        dh1, dg3 = _rms_bwd(hat, rstd, g3_ref[...], df)
        dg3_ref[...] += dg3
        dh1_ref[...] = dh2_ref[...] + dh1

    row = lambda i: (i, 0)
    return pl.pallas_call(
        body, name="ffn_backward", grid=(t // tm,),
        in_specs=[pl.BlockSpec((tm, d), row), pl.BlockSpec((tm, d), row), pl.BlockSpec((tm, d), row), pl.BlockSpec((tm, ff), row),
                  pl.BlockSpec((tm, ff), row), _const(g3.shape), _resident(w_gate.shape), _resident(w_up.shape), _resident(w_down.shape)],
        out_specs=[pl.BlockSpec((tm, d), row), pl.BlockSpec((tm, ff), row), pl.BlockSpec((tm, ff), row), pl.BlockSpec((tm, ff), row),
                   _const(g3.shape)],
        out_shape=[jax.ShapeDtypeStruct((t, d), F32), jax.ShapeDtypeStruct((t, ff), BF16), jax.ShapeDtypeStruct((t, ff), BF16),
                   jax.ShapeDtypeStruct((t, ff), BF16), jax.ShapeDtypeStruct(g3.shape, F32)],
        compiler_params=_params("arbitrary"),
    )(h1, dh2, dd, gate, up, g3, w_gate, w_up, w_down)


def _ffn_weight_grads(f, dd, dgate, dup, act):
    t, d = f.shape
    ff = dgate.shape[1]
    tm = min(TM_FFN, t)
    nt = t // tm
    fc = ff // FF_CHUNKS

    def body(f_ref, dd_ref, dgate_ref, dup_ref, act_ref, dwg_ref, dwu_ref, dwd_ref, acc_g, acc_u, acc_d, stage, sem):
        c, i = pl.program_id(0), pl.program_id(1)

        @pl.when(i == 0)
        def _():
            acc_g[...] = jnp.zeros_like(acc_g)
            acc_u[...] = jnp.zeros_like(acc_u)
            acc_d[...] = jnp.zeros_like(acc_d)

        fv = f_ref[...]
        acc_g[...] += _dot_tn(fv, dgate_ref[...])
        acc_u[...] += _dot_tn(fv, dup_ref[...])
        acc_d[...] += _dot_tn(act_ref[...], dd_ref[...])

        @pl.when(i == nt - 1)
        def _():
            rows = pl.ds(pl.multiple_of(c * fc, 16), fc)
            copies = []
            for k, (acc, out, transposed) in enumerate(((acc_d, dwd_ref, False), (acc_g, dwg_ref, True), (acc_u, dwu_ref, True))):
                if k >= 2:
                    copies[k - 2].wait()
                stage[k % 2] = (acc[...].T if transposed else acc[...]).astype(BF16)
                copies.append(pltpu.make_async_copy(stage.at[k % 2], out.at[rows, :], sem.at[k % 2]))
                copies[k].start()
            copies[-2].wait()
            copies[-1].wait()

    row = lambda c, i: (i, 0)
    col = lambda c, i: (i, c)
    return pl.pallas_call(
        body, name="ffn_weight_grads", grid=(FF_CHUNKS, nt),
        in_specs=[pl.BlockSpec((tm, d), row), pl.BlockSpec((tm, d), row), pl.BlockSpec((tm, fc), col), pl.BlockSpec((tm, fc), col),
                  pl.BlockSpec((tm, fc), col)],
        out_specs=[ANY, ANY, ANY],
        out_shape=[jax.ShapeDtypeStruct((ff, d), BF16)] * 3,
        scratch_shapes=[pltpu.VMEM((d, fc), F32), pltpu.VMEM((d, fc), F32), pltpu.VMEM((fc, d), F32), pltpu.VMEM((2, fc, d), BF16),
                        pltpu.SemaphoreType.DMA((2,))],
        compiler_params=_params("arbitrary", "arbitrary"),
    )(f, dd, dgate, dup, act)


def _adamw(w, g, m, v):
    m = ADAM_B1 * m + (1.0 - ADAM_B1) * g
    v = ADAM_B2 * v + (1.0 - ADAM_B2) * (g * g)
    m_hat = m / (1.0 - ADAM_B1 ** ADAM_STEP)
    v_hat = v / (1.0 - ADAM_B2 ** ADAM_STEP)
    return -ADAM_LR * (m_hat / (jnp.sqrt(v_hat) + ADAM_EPS) + ADAM_WD * w), m, v


def _sum_slabs(ref):
    total = ref[0].astype(F32)
    for i in range(1, N_DEV):
        total = total + ref[i].astype(F32)
    return total


def _adamw_rows(r, c):
    tr = r
    for cand in range(8, r, 8):
        if r % cand == 0 and cand * c <= ADAMW_BLOCK_ELEMS:
            tr = cand
    return r if r * c <= ADAMW_BLOCK_ELEMS else tr


def _reduce_adamw_carrying(parts, ws, ms, vs, to_exchange, whole):
    k, nx = len(ws), len(to_exchange)
    r, c = ws[0].shape
    tr = _adamw_rows(r, c)
    steps = r // tr

    def body(*refs):
        p_refs, w_refs, m_refs, v_refs = (refs[a * k:(a + 1) * k] for a in range(4))
        sent, outs = refs[4 * k:4 * k + nx], refs[4 * k + nx:8 * k + nx]
        landed, sems = refs[8 * k + nx:8 * k + 2 * nx], refs[8 * k + 2 * nx:]
        start, finish = _exchange_ops(sent, landed, whole, *sems)
        pl.when(pl.program_id(0) == 0)(start)
        for a in range(k):
            g = _sum_slabs(p_refs[a])
            outs[4 * a][...] = g
            outs[4 * a + 1][...], outs[4 * a + 2][...], outs[4 * a + 3][...] = _adamw(w_refs[a][...], g, m_refs[a][...], v_refs[a][...])
        pl.when(pl.program_id(0) == steps - 1)(finish)

    blk = pl.BlockSpec((tr, c), lambda i: (i, 0))
    out = pl.pallas_call(
        body, name="adamw_ffn_exchange_rest", grid=(steps,),
        in_specs=[pl.BlockSpec((N_DEV, tr, c), lambda i: (0, i, 0))] * k + [blk] * (3 * k) + [ANY] * nx,
        out_specs=[blk] * (4 * k) + [ANY] * nx,
        out_shape=[jax.ShapeDtypeStruct((r, c), F32)] * (4 * k)
        + [jax.ShapeDtypeStruct((N_DEV, *a.shape) if w else a.shape, a.dtype) for a, w in zip(to_exchange, whole)],
        scratch_shapes=_exchange_sems(nx),
        compiler_params=_params("arbitrary"),
    )(*parts, *ws, *ms, *vs, *to_exchange)
    return [tuple(out[4 * a:4 * a + 4]) for a in range(k)], out[4 * k:]


def _reduce_adamw(parts, w, m, v, name):
    r, c = w.shape
    tr = _adamw_rows(r, c)

    def body(p_ref, w_ref, m_ref, v_ref, g_out, d_out, m_out, v_out):
        g = _sum_slabs(p_ref)
        g_out[...] = g
        d_out[...], m_out[...], v_out[...] = _adamw(w_ref[...], g, m_ref[...], v_ref[...])

    blk = pl.BlockSpec((tr, c), lambda i: (i, 0))
    return pl.pallas_call(
        body, name=name, grid=(r // tr,),
        in_specs=[pl.BlockSpec((N_DEV, tr, c), lambda i: (0, i, 0)), blk, blk, blk],
        out_specs=[blk] * 4, out_shape=[jax.ShapeDtypeStruct((r, c), F32)] * 4,
        compiler_params=_params("arbitrary"),
    )(parts, w, m, v)


def _reduce_adamw_small(parts, ws, ms, vs):
    n = len(parts)

    def body(*refs):
        p_refs, w_refs, m_refs, v_refs = (refs[k * n:(k + 1) * n] for k in range(4))
        outs = refs[4 * n:]
        for a in range(n):
            g = _sum_slabs(p_refs[a])
            outs[4 * a][...] = g
            outs[4 * a + 1][...], outs[4 * a + 2][...], outs[4 * a + 3][...] = _adamw(w_refs[a][...], g, m_refs[a][...], v_refs[a][...])

    out = pl.pallas_call(
        body, name="adamw_replicated",
        out_shape=[jax.ShapeDtypeStruct(w.shape, F32) for w in ws for _ in range(4)],
        compiler_params=pltpu.CompilerParams(vmem_limit_bytes=VMEM_LIMIT_BYTES),
    )(*parts, *ws, *ms, *vs)
    return [tuple(out[4 * a:4 * a + 4]) for a in range(n)]


def kernel(x, meta_tokens, norm_mix_pre, w_in, conv_w, pool_w, pool_scale, w_out, norm_mix_post, norm_ffn_pre, w_gate, w_up, w_down, norm_ffn_post, loss_target, m_meta_tokens, m_norm_mix_pre, m_w_in, m_conv_w, m_pool_w, m_pool_scale, m_w_out, m_norm_mix_post, m_norm_ffn_pre, m_w_gate, m_w_up, m_w_down, m_norm_ffn_post, v_meta_tokens, v_norm_mix_pre, v_w_in, v_conv_w, v_pool_w, v_pool_scale, v_w_out, v_norm_mix_post, v_norm_ffn_pre, v_w_gate, v_w_up, v_w_down, v_norm_ffn_post):
    n_seq, seq, d = x.shape
    x2d = x.reshape(n_seq * seq, d)
    target = loss_target.reshape(n_seq * seq, d)

    t_ = lambda a: jnp.swapaxes(a[0], 0, 1)
    (win_s, wout_s, meta_s, conv_s), ffn_shards = _gather_first_weights(
        [w_in[0], w_out[0], meta_tokens, conv_w[0]], [BF16, BF16, F32, F32], [t_(w_gate), t_(w_up), w_down[0]])
    (win_b,) = _columns_from_slabs([win_s])
    wout_b = wout_s.reshape(d, d)
    meta = jnp.transpose(meta_s, (1, 0, 2)).reshape(N_META, d)
    conv = jnp.transpose(conv_s, (1, 0, 2)).reshape(CONV_WIDTH, -1)
    pw, ps = pool_w[0], pool_scale

    a_meta, z_meta = _meta_forward(meta, norm_mix_pre, win_b)
    (h1, z, m), ffn_slabs = _mixer_forward(x2d, z_meta, norm_mix_pre, win_b, conv, pw, ps, wout_b, norm_mix_post, n_seq, ffn_shards)
    wg_b, wu_b, wd_b = (s.reshape(-1, d) for s in ffn_slabs)
    f, gate, up, dd, dh2, loss_sum, dg4 = _ffn_forward(h1, target, norm_ffn_pre, wg_b, wu_b, wd_b, norm_ffn_post)
    dh1, dgate, dup, act, dg3 = _ffn_backward(h1, dh2, dd, gate, up, norm_ffn_pre, wg_b, wu_b, wd_b)
    ffn_grads = _ffn_weight_grads(f, dd, dgate, dup, act)
    (gx, dwin, dwout, dg1, dg2, dconv, dpw, dps, dmeta), ffn_parts = _mixer_backward(
        x2d, dh1, m, z, meta, a_meta, z_meta, norm_mix_pre, win_b, conv, pw, ps, wout_b, norm_mix_post, n_seq,
        [g.reshape(N_DEV, -1, d) for g in ffn_grads])

    dmeta_s = jnp.transpose(dmeta.reshape(N_META, N_DEV, -1), (1, 0, 2))
    dconv_s = jnp.transpose(dconv.reshape(CONV_WIDTH, N_DEV, -1), (1, 0, 2))
    ffn_res, last = _reduce_adamw_carrying(
        ffn_parts, [t_(w_gate), t_(w_up), w_down[0]], [t_(m_w_gate), t_(m_w_up), m_w_down[0]], [t_(v_w_gate), t_(v_w_up), v_w_down[0]],
        [dwin, dwout.reshape(N_DEV, -1, d), dmeta_s, dconv_s, dg1, dg2, dg3, dg4, dpw, dps], [False] * 4 + [True] * 6)
    replicated = last[4:]

    names = ["meta_tokens", "norm_mix_pre", "w_in", "conv_w", "pool_w", "pool_scale", "w_out", "norm_mix_post", "norm_ffn_pre", "w_gate",
             "w_up", "w_down", "norm_ffn_post"]
    res = {"w_gate": tuple(jnp.swapaxes(o, 0, 1)[None] for o in ffn_res[0]),
           "w_up": tuple(jnp.swapaxes(o, 0, 1)[None] for o in ffn_res[1]), "w_down": tuple(o[None] for o in ffn_res[2])}
    for nm, parts, w, m_, v_ in (("w_in", last[0], w_in, m_w_in, v_w_in), ("w_out", last[1], w_out, m_w_out, v_w_out),
                                 ("conv_w", last[3], conv_w, m_conv_w, v_conv_w)):
        res[nm] = tuple(o[None] for o in _reduce_adamw(parts, w[0], m_[0], v_[0], "adamw_" + nm))
    res["meta_tokens"] = tuple(_reduce_adamw(last[2], meta_tokens, m_meta_tokens, v_meta_tokens, "adamw_meta_tokens"))
    small = _reduce_adamw_small(
        replicated, [norm_mix_pre, norm_mix_post, norm_ffn_pre, norm_ffn_post, pool_w[0], pool_scale],
        [m_norm_mix_pre, m_norm_mix_post, m_norm_ffn_pre, m_norm_ffn_post, m_pool_w[0], m_pool_scale],
        [v_norm_mix_pre, v_norm_mix_post, v_norm_ffn_pre, v_norm_ffn_post, v_pool_w[0], v_pool_scale])
    for nm, r in zip(["norm_mix_pre", "norm_mix_post", "norm_ffn_pre", "norm_ffn_post", "pool_w", "pool_scale"], small):
        res[nm] = tuple(o[None] for o in r) if nm == "pool_w" else r

    loss = lax.psum(loss_sum[0, 0], MESH_AXES)
    return (loss, gx.reshape(n_seq, seq, d), *[res[nm][0] for nm in names], *[res[nm][1] for nm in names],
            *[res[nm][2] for nm in names], *[res[nm][3] for nm in names])
```

```python
import functools

import jax
import jax.numpy as jnp
from jax import lax
from jax.experimental import pallas as pl
from jax.experimental.pallas import tpu as pltpu

F32, BF16 = jnp.float32, jnp.bfloat16
RMS_EPS = 1e-6
N_META = 16
CONV_WIDTH = 3
POOL_WINDOWS = (2, 4, 8, 16)
POOL_GROUP = 128
HALO = 16
N_DEV = 8
MESH_AXES = ("x", "y", "c")
MESH = pl.DeviceIdType.MESH
VMEM_LIMIT_BYTES = 56 * 1024 * 1024
ADAMW_BLOCK_ELEMS = 64 * 1024
TM_MIX = 512
TM_FFN = 256
TM_WGRAD = 512
FF_CHUNKS = 2

ADAM_LR, ADAM_B1, ADAM_B2, ADAM_EPS, ADAM_WD, ADAM_STEP = 0.001, 0.9, 0.999, 1e-08, 0.01, 10


def _dot(a, b):
    return jnp.dot(a, b, preferred_element_type=F32)


def _dot_nt(a, b):
    return lax.dot_general(a, b, (((1,), (1,)), ((), ())), preferred_element_type=F32)


def _dot_tn(a, b):
    return lax.dot_general(a, b, (((0,), (0,)), ((), ())), preferred_element_type=F32)


def _rms_stats(h):
    rstd = lax.rsqrt(jnp.mean(h * h, axis=-1, keepdims=True) + RMS_EPS)
    return h * rstd, rstd


def _rms_bwd(hat, rstd, g, dy):
    gdy = dy * g
    proj = jnp.mean(gdy * hat, axis=-1, keepdims=True)
    return rstd * (gdy - hat * proj), jnp.sum(dy * hat, axis=0, keepdims=True)


def _params(*semantics):
    return pltpu.CompilerParams(dimension_semantics=semantics or None, vmem_limit_bytes=VMEM_LIMIT_BYTES)


def _resident(shape):
    zeros = (0,) * len(shape)
    return pl.BlockSpec(shape, lambda *_: zeros, pipeline_mode=pl.Buffered(1))


def _const(shape):
    zeros = (0,) * len(shape)
    return pl.BlockSpec(shape, lambda *_: zeros)


ANY = pl.BlockSpec(memory_space=pl.ANY)


def _my_place():
    x, y, c = (lax.axis_index(a) for a in MESH_AXES)
    return x, y, c


def _exchange_sems(n):
    return [pltpu.SemaphoreType.DMA((n, N_DEV - 1)), pltpu.SemaphoreType.DMA((n, N_DEV - 1)), pltpu.SemaphoreType.DMA((n,))]


def _gather_ops(srcs, outs, send_sems, recv_sems, local_sems):
    n = len(srcs)
    x, y, c = _my_place()
    me, sibling = (x, y, c), (x, y, 1 - c)
    chips = [(1 - x, y), (x, 1 - y), (1 - x, 1 - y)]

    def slab(px, py, pc):
        return 4 * px + 2 * py + pc

    def copy(a, k, block, to, src=None):
        dst = outs[a].at[slab(*block)]
        return pltpu.make_async_remote_copy(
            src_ref=dst if src is None else src, dst_ref=dst, send_sem=send_sems.at[a, k], recv_sem=recv_sems.at[a, k],
            device_id=to, device_id_type=MESH)

    def mine(a):
        return pltpu.make_async_copy(srcs[a], outs[a].at[slab(*me)], local_sems.at[a])

    def first(a):
        return [copy(a, 0, me, sibling, src=srcs[a])] + [copy(a, 1 + j, me, (*chip, c), src=srcs[a]) for j, chip in enumerate(chips)]

    def passed(a, j):
        return copy(a, 4 + j, (*chips[j], c), sibling)

    def start():
        for a in range(n):
            mine(a).start()
            for cp in first(a):
                cp.start()

    def forward():
        for j, chip in enumerate(chips):
            for a in range(n):
                copy(a, 1 + j, (*chip, c), me).wait_recv()
                passed(a, j).start()

    def finish():
        for a in range(n):
            copy(a, 0, sibling, me).wait_recv()
            for j, chip in enumerate(chips):
                copy(a, 4 + j, (*chip, 1 - c), me).wait_recv()
        for a in range(n):
            for cp in first(a) + [passed(a, j) for j in range(len(chips))]:
                cp.wait_send()
            mine(a).wait()

    return start, forward, finish


def _exchange_ops(ins, outs, whole, send_sems, recv_sems, local_sems):
    n = len(ins)
    x, y, c = _my_place()
    me = 4 * x + 2 * y + c

    def src(a, i):
        return ins[a] if whole[a] else ins[a].at[i]

    def mine(a):
        return pltpu.make_async_copy(src(a, me), outs[a].at[me], local_sems.at[a])

    def send(a, k):
        to = (me + k) % N_DEV
        return pltpu.make_async_remote_copy(
            src_ref=src(a, to), dst_ref=outs[a].at[me], send_sem=send_sems.at[a, k - 1], recv_sem=recv_sems.at[a, k - 1],
            device_id=(to // 4, (to // 2) % 2, to % 2), device_id_type=MESH)

    def landed(a, k):
        frm = (me + N_DEV - k) % N_DEV
        return pltpu.make_async_remote_copy(
            src_ref=src(a, frm), dst_ref=outs[a].at[frm], send_sem=send_sems.at[a, k - 1], recv_sem=recv_sems.at[a, k - 1],
            device_id=(x, y, c), device_id_type=MESH)

    def start():
        for a in range(n):
            mine(a).start()
            for k in range(1, N_DEV):
                send(a, k).start()

    def finish():
        for a in range(n):
            for k in range(1, N_DEV):
                landed(a, k).wait_recv()
        for a in range(n):
            for k in range(1, N_DEV):
                send(a, k).wait_send()
            mine(a).wait()

    return start, finish


def _gather_first_weights(gathered, dtypes, cast_only):
    n, k = len(gathered), len(cast_only)

    def body(*refs):
        ins, casts_in = refs[:n], refs[n:n + k]
        outs, casts_out = refs[n + k:2 * n + k], refs[2 * n + k:2 * n + 2 * k]
        stages = refs[2 * n + 2 * k:3 * n + 2 * k]
        start, forward, finish = _gather_ops(stages, outs, *refs[3 * n + 2 * k:])
        for a in range(n):
            stages[a][...] = ins[a][...].astype(stages[a].dtype)
        start()
        for a in range(k):
            casts_out[a][...] = casts_in[a][...].astype(BF16)
        forward()
        finish()

    vmem = pl.BlockSpec(memory_space=pltpu.VMEM)
    out = pl.pallas_call(
        body, name="gather_first_weights",
        out_shape=[jax.ShapeDtypeStruct((N_DEV, *s.shape), d) for s, d in zip(gathered, dtypes)]
        + [jax.ShapeDtypeStruct(s.shape, BF16) for s in cast_only],
        in_specs=[vmem] * (n + k), out_specs=[ANY] * n + [vmem] * k,
        scratch_shapes=[pltpu.VMEM(s.shape, d) for s, d in zip(gathered, dtypes)] + _exchange_sems(n),
        compiler_params=pltpu.CompilerParams(vmem_limit_bytes=VMEM_LIMIT_BYTES),
    )(*gathered, *cast_only)
    return out[:n], out[n:]


def _exchange(arrays, whole, name):
    n = len(arrays)

    def body(*refs):
        start, finish = _exchange_ops(refs[:n], refs[n:2 * n], whole, *refs[2 * n:])
        start()
        finish()

    return pl.pallas_call(
        body, name=name,
        out_shape=[jax.ShapeDtypeStruct((N_DEV, *a.shape) if w else a.shape, a.dtype) for a, w in zip(arrays, whole)],
        in_specs=[ANY] * n, out_specs=[ANY] * n, scratch_shapes=_exchange_sems(n),
    )(*arrays)


def _columns_from_slabs(slabs):
    def body(*refs):
        k = len(refs) // 2
        for src, dst in zip(refs[:k], refs[k:]):
            n = src.shape[2]
            for i in range(N_DEV):
                dst[:, pl.ds(n * i, n)] = src[i]

    return pl.pallas_call(
        body, name="columns_from_slabs",
        out_shape=[jax.ShapeDtypeStruct((s.shape[1], N_DEV * s.shape[2]), s.dtype) for s in slabs],
        compiler_params=pltpu.CompilerParams(vmem_limit_bytes=VMEM_LIMIT_BYTES),
    )(*slabs)


def _mixer_core(z, ext_u, ext_p, conv_ref, pool_w_ref, tm):
    c_w = z.shape[1] // 4
    b, c, v, p = z[:, :c_w], z[:, c_w:2 * c_w], z[:, 2 * c_w:3 * c_w], z[:, 3 * c_w:]
    u = c * v
    ext_u[pl.ds(HALO, tm), :] = u
    ext_p[pl.ds(HALO, tm), :] = p
    u1 = ext_u[pl.ds(HALO - 1, tm), :]
    u2 = ext_u[pl.ds(HALO - 2, tm), :]
    yc = conv_ref[pl.ds(2, 1), :] * u + conv_ref[pl.ds(1, 1), :] * u1 + conv_ref[pl.ds(0, 1), :] * u2
    pooled, mixed = [], []
    for g, win in enumerate(POOL_WINDOWS):
        lanes = pl.ds(POOL_GROUP * g, POOL_GROUP)
        pg = p[:, POOL_GROUP * g:POOL_GROUP * (g + 1)]
        s = pg
        for k in range(1, win):
            s = s + ext_p[pl.ds(HALO - k, tm), lanes]
        pooled.append((s * (1.0 / win) - pg).astype(BF16))
        mixed.append(_dot(pooled[-1], pool_w_ref[g].astype(BF16)))
    return b, c, v, u, u1, u2, yc, pooled, mixed


def _meta_forward(meta, g1, w_in):
    def body(meta_ref, g1_ref, w_ref, a_ref, z_ref):
        hat, _ = _rms_stats(meta_ref[...])
        a = (hat * g1_ref[...]).astype(BF16)
        a_ref[...] = a
        z_ref[...] = _dot(a, w_ref[...])

    return pl.pallas_call(
        body, name="meta_forward",
        out_shape=[jax.ShapeDtypeStruct(meta.shape, BF16), jax.ShapeDtypeStruct((N_META, w_in.shape[1]), F32)],
        compiler_params=pltpu.CompilerParams(vmem_limit_bytes=VMEM_LIMIT_BYTES),
    )(meta, g1, w_in)


def _mixer_forward(x2d, z_meta, g1, w_in, conv_w, pool_w, pool_scale, w_out, g2, n_seq, to_gather):
    t, d = x2d.shape
    zw = w_in.shape[1]
    cw = zw // 4
    s = t // n_seq
    tm = min(TM_MIX, s)
    nj = s // tm
    ng = len(to_gather)

    def body(x_ref, zm_ref, g1_ref, win_ref, conv_ref, pw_ref, ps_ref, wout_ref, g2_ref, *rest):
        shards, (h1_ref, z_ref, m_ref), slabs = rest[:ng], rest[ng:ng + 3], rest[ng + 3:2 * ng + 3]
        ext_u, ext_p = rest[2 * ng + 3:2 * ng + 5]
        start, forward, finish = _gather_ops(shards, slabs, *rest[2 * ng + 5:])
        pl.when((pl.program_id(0) == 0) & (pl.program_id(1) == 0))(start)

        @pl.when(pl.program_id(1) == 0)
        def _():
            zm = zm_ref[...]
            ext_u[pl.ds(0, HALO), :] = zm[:, cw:2 * cw] * zm[:, 2 * cw:3 * cw]
            ext_p[pl.ds(0, HALO), :] = zm[:, 3 * cw:]

        h0 = x_ref[...]
        hat, _ = _rms_stats(h0)
        z = _dot((hat * g1_ref[...]).astype(BF16), win_ref[...])
        z_ref[...] = z.astype(BF16)
        b, _, _, _, _, _, yc, _, mixed = _mixer_core(z, ext_u, ext_p, conv_ref, pw_ref, tm)
        ps = ps_ref[...]
        y = [b * yc] + [mixed[g] * ps[:, POOL_GROUP * g:POOL_GROUP * (g + 1)] for g in range(len(POOL_WINDOWS))]
        m = _dot(jnp.concatenate(y, axis=1).astype(BF16), wout_ref[...])
        m_ref[...] = m
        m_hat, _ = _rms_stats(m)
        h1_ref[...] = h0 + m_hat * g2_ref[...]
        ext_u[pl.ds(0, HALO), :] = ext_u[pl.ds(tm, HALO), :]
        ext_p[pl.ds(0, HALO), :] = ext_p[pl.ds(tm, HALO), :]

        @pl.when((pl.program_id(0) == n_seq - 1) & (pl.program_id(1) == nj - 1))
        def _():
            forward()
            finish()

    row = lambda b, j: (b * nj + j, 0)
    out = pl.pallas_call(
        body, name="mixer_forward", grid=(n_seq, nj),
        in_specs=[pl.BlockSpec((tm, d), row), _const(z_meta.shape), _const(g1.shape), _resident(w_in.shape), _const(conv_w.shape),
                  _const(pool_w.shape), _const(pool_scale.shape), _resident(w_out.shape), _const(g2.shape)] + [ANY] * ng,
        out_specs=[pl.BlockSpec((tm, d), row), pl.BlockSpec((tm, zw), row), pl.BlockSpec((tm, d), row)] + [ANY] * ng,
        out_shape=[jax.ShapeDtypeStruct((t, d), F32), jax.ShapeDtypeStruct((t, zw), BF16), jax.ShapeDtypeStruct((t, d), F32)]
        + [jax.ShapeDtypeStruct((N_DEV, *a.shape), a.dtype) for a in to_gather],
        scratch_shapes=[pltpu.VMEM((tm + HALO, cw), F32), pltpu.VMEM((tm + HALO, cw), F32)] + _exchange_sems(ng),
        compiler_params=_params("arbitrary", "arbitrary"),
    )(x2d, z_meta, g1, w_in, conv_w, pool_w, pool_scale, w_out, g2, *to_gather)
    return out[:3], out[3:]


def _mixer_backward(x2d, dh1, m, z, meta, a_meta, z_meta, g1, w_in, conv_w, pool_w, pool_scale, w_out, g2, n_seq, to_exchange):
    t, d = x2d.shape
    zw = w_in.shape[1]
    cw = zw // 4
    s = t // n_seq
    tm = min(TM_MIX, s)
    nj = s // tm
    n_groups = len(POOL_WINDOWS)
    zs = zw // N_DEV
    nx = len(to_exchange)

    def body(x_ref, dh1_ref, m_ref, z_ref, zprev_ref, meta_ref, am_ref, zm_ref, g1_ref, win_ref, conv_ref, pw_ref, ps_ref, wout_ref,
             g2_ref, *rest):
        sent, rest = rest[:nx], rest[nx:]
        gx_ref, dwin_ref, dwout_ref, dg1_ref, dg2_ref, dconv_ref, dpw_ref, dps_ref, dmeta_ref = rest[:9]
        landed, rest = rest[9:9 + nx], rest[9 + nx:]
        ext_u, ext_p, ext_dyc, ext_dq, acc_win, acc_wout, stage16, sem = rest[:8]
        start, finish = _exchange_ops(sent, landed, [False] * nx, *rest[8:])
        b_id, j = pl.program_id(0), pl.program_id(1)
        jr = nj - 1 - j
        pl.when((b_id == 0) & (j == 0))(start)

        @pl.when((b_id == 0) & (j == 0))
        def _():
            acc_win[...] = jnp.zeros_like(acc_win)
            acc_wout[...] = jnp.zeros_like(acc_wout)
            for r in (dg1_ref, dg2_ref, dconv_ref, dpw_ref, dps_ref, dmeta_ref):
                r[...] = jnp.zeros_like(r)

        @pl.when(j == 0)
        def _():
            ext_dyc[pl.ds(tm, HALO), :] = jnp.zeros((HALO, cw), F32)
            ext_dq[pl.ds(tm, HALO), :] = jnp.zeros((HALO, cw), F32)

        zm = zm_ref[...]
        halo = jnp.where(jr == 0, zm, zprev_ref[...].astype(F32))
        ext_u[pl.ds(0, HALO), :] = halo[:, cw:2 * cw] * halo[:, 2 * cw:3 * cw]
        ext_p[pl.ds(0, HALO), :] = halo[:, 3 * cw:]

        h0 = x_ref[...]
        hat0, rstd0 = _rms_stats(h0)
        g1 = g1_ref[...]
        a = (hat0 * g1).astype(BF16)
        b, c, v, u, u1, u2, yc, pooled, mixed = _mixer_core(z_ref[...].astype(F32), ext_u, ext_p, conv_ref, pw_ref, tm)
        ps = ps_ref[...]
        y = [b * yc] + [mixed[g] * ps[:, POOL_GROUP * g:POOL_GROUP * (g + 1)] for g in range(n_groups)]
        ycat = jnp.concatenate(y, axis=1).astype(BF16)

        dh1v = dh1_ref[...]
        m_hat, m_rstd = _rms_stats(m_ref[...])
        dm, dg2 = _rms_bwd(m_hat, m_rstd, g2_ref[...], dh1v)
        dg2_ref[...] += dg2
        dm = dm.astype(BF16)
        acc_wout[...] += _dot_tn(ycat, dm)
        dycat = _dot_nt(dm, wout_ref[...])

        dyconv = dycat[:, :cw]
        db = dyconv * yc
        dyc = dyconv * b
        ext_dyc[pl.ds(0, tm), :] = dyc
        du = (conv_ref[pl.ds(2, 1), :] * dyc + conv_ref[pl.ds(1, 1), :] * ext_dyc[pl.ds(1, tm), :]
              + conv_ref[pl.ds(0, 1), :] * ext_dyc[pl.ds(2, tm), :])
        dconv_ref[pl.ds(2, 1), :] += jnp.sum(dyc * u, axis=0, keepdims=True)
        dconv_ref[pl.ds(1, 1), :] += jnp.sum(dyc * u1, axis=0, keepdims=True)
        dconv_ref[pl.ds(0, 1), :] += jnp.sum(dyc * u2, axis=0, keepdims=True)

        dp = []
        for g, win in enumerate(POOL_WINDOWS):
            lanes = pl.ds(POOL_GROUP * g, POOL_GROUP)
            dypool = dycat[:, cw + POOL_GROUP * g:cw + POOL_GROUP * (g + 1)]
            dps_ref[:, lanes] += jnp.sum(dypool * mixed[g], axis=0, keepdims=True)
            dmixed = (dypool * ps[:, POOL_GROUP * g:POOL_GROUP * (g + 1)]).astype(BF16)
            dpw_ref[g] += _dot_tn(pooled[g], dmixed)
            dq = _dot_nt(dmixed, pw_ref[g].astype(BF16))
            ext_dq[pl.ds(0, tm), lanes] = dq
            acc = dq
            for k in range(1, win):
                acc = acc + ext_dq[pl.ds(k, tm), lanes]
            dp.append(acc * (1.0 / win) - dq)

        dz = jnp.concatenate([db, du * v, du * c] + dp, axis=1).astype(BF16)
        acc_win[...] += _dot_tn(a, dz)
        dh0, dg1 = _rms_bwd(hat0, rstd0, g1, _dot_nt(dz, win_ref[...]))
        dg1_ref[...] += dg1
        gx_ref[...] = dh1v + dh0

        ext_dyc[pl.ds(tm, HALO), :] = ext_dyc[pl.ds(0, HALO), :]
        ext_dq[pl.ds(tm, HALO), :] = ext_dq[pl.ds(0, HALO), :]

        @pl.when(jr == 0)
        def _():
            ext_dyc[pl.ds(tm - HALO, HALO), :] = jnp.zeros((HALO, cw), F32)
            ext_dq[pl.ds(tm - HALO, HALO), :] = jnp.zeros((HALO, cw), F32)
            du_m = (conv_ref[pl.ds(1, 1), :] * ext_dyc[pl.ds(tm - HALO + 1, HALO), :]
                    + conv_ref[pl.ds(0, 1), :] * ext_dyc[pl.ds(tm - HALO + 2, HALO), :])
            dp_m = []
            for g, win in enumerate(POOL_WINDOWS):
                lanes = pl.ds(POOL_GROUP * g, POOL_GROUP)
                acc = ext_dq[pl.ds(tm - HALO + 1, HALO), lanes]
                for k in range(2, win):
                    acc = acc + ext_dq[pl.ds(tm - HALO + k, HALO), lanes]
                dp_m.append(acc * (1.0 / win))
            dz_m = jnp.concatenate([jnp.zeros((HALO, cw), F32), du_m * zm[:, 2 * cw:3 * cw], du_m * zm[:, cw:2 * cw]] + dp_m,
                                   axis=1).astype(BF16)
            acc_win[...] += _dot_tn(am_ref[...], dz_m)
            hat_m, rstd_m = _rms_stats(meta_ref[...])
            dmeta, dg1_m = _rms_bwd(hat_m, rstd_m, g1, _dot_nt(dz_m, win_ref[...]))
            dg1_ref[...] += dg1_m
            dmeta_ref[...] += dmeta

        @pl.when((b_id == n_seq - 1) & (j == nj - 1))
        def _():
            pieces = [(acc_win, zs * i, dwin_ref.at[i]) for i in range(N_DEV)]
            pieces += [(acc_wout, zs * i, dwout_ref.at[:, pl.ds(zs * i, zs)]) for i in range(d // zs)]
            copies = []
            for k, (acc, col, dst) in enumerate(pieces):
                if k >= 2:
                    copies[k - 2].wait()
                stage16[k % 2] = acc[:, pl.ds(col, zs)].astype(BF16)
                copies.append(pltpu.make_async_copy(stage16.at[k % 2], dst, sem.at[k % 2]))
                copies[k].start()
            copies[-2].wait()
            copies[-1].wait()
            finish()

    row = lambda b, j: (b * nj + nj - 1 - j, 0)
    prev = lambda b, j: (jnp.maximum((b * s + (nj - 1 - j) * tm) // HALO - 1, 0), 0)
    small = [g1.shape, g2.shape, conv_w.shape, pool_w.shape, pool_scale.shape, meta.shape]
    out = pl.pallas_call(
        body, name="mixer_backward", grid=(n_seq, nj),
        in_specs=[pl.BlockSpec((tm, d), row), pl.BlockSpec((tm, d), row), pl.BlockSpec((tm, d), row), pl.BlockSpec((tm, zw), row),
                  pl.BlockSpec((HALO, zw), prev), _const(meta.shape), _const(a_meta.shape), _const(z_meta.shape), _const(g1.shape),
                  _resident(w_in.shape), _const(conv_w.shape), _const(pool_w.shape), _const(pool_scale.shape), _resident(w_out.shape),
                  _const(g2.shape)] + [ANY] * nx,
        out_specs=[pl.BlockSpec((tm, d), row), ANY, ANY] + [_const(sh) for sh in small] + [ANY] * nx,
        out_shape=[jax.ShapeDtypeStruct((t, d), F32), jax.ShapeDtypeStruct((N_DEV, d, zs), BF16),
                   jax.ShapeDtypeStruct(w_out.shape, BF16)] + [jax.ShapeDtypeStruct(sh, F32) for sh in small]
        + [jax.ShapeDtypeStruct(a.shape, a.dtype) for a in to_exchange],
        scratch_shapes=[pltpu.VMEM((tm + HALO, cw), F32)] * 4
        + [pltpu.VMEM(w_in.shape, F32), pltpu.VMEM(w_out.shape, F32), pltpu.VMEM((2, d, zs), BF16),
           pltpu.SemaphoreType.DMA((2,))] + _exchange_sems(nx),
        compiler_params=_params("arbitrary", "arbitrary"),
    )(x2d, dh1, m, z, z, meta, a_meta, z_meta, g1, w_in, conv_w, pool_w, pool_scale, w_out, g2, *to_exchange)
    return out[:9], out[9:]


def _ffn_forward(h1, target, g3, w_gate, w_up, w_down, g4):
    t, d = h1.shape
    ff = w_gate.shape[0]
    tm = min(TM_FFN, t)

    def body(h1_ref, tgt_ref, g3_ref, wg_ref, wu_ref, wd_ref, g4_ref, f_ref, gate_ref, up_ref, dd_ref, dh2_ref, loss_ref, dg4_ref):
        @pl.when(pl.program_id(0) == 0)
        def _():
            loss_ref[...] = jnp.zeros_like(loss_ref)
            dg4_ref[...] = jnp.zeros_like(dg4_ref)

        h1v = h1_ref[...]
        hat, _ = _rms_stats(h1v)
        f = (hat * g3_ref[...]).astype(BF16)
        f_ref[...] = f
        gate = _dot_nt(f, wg_ref[...])
        up = _dot_nt(f, wu_ref[...])
        gate_ref[...] = gate.astype(BF16)
        up_ref[...] = up.astype(BF16)
        act = (gate * jax.nn.sigmoid(gate) * up).astype(BF16)
        d_hat, d_rstd = _rms_stats(_dot(act, wd_ref[...]))
        g4 = g4_ref[...]
        err = h1v + d_hat * g4 - tgt_ref[...]
        loss_ref[...] += jnp.sum(err * err) * (0.5 / d)
        dh2 = err * (1.0 / d)
        dh2_ref[...] = dh2
        dd, dg4 = _rms_bwd(d_hat, d_rstd, g4, dh2)
        dg4_ref[...] += dg4
        dd_ref[...] = dd.astype(BF16)

    row = lambda i: (i, 0)
    return pl.pallas_call(
        body, name="ffn_forward", grid=(t // tm,),
        in_specs=[pl.BlockSpec((tm, d), row), pl.BlockSpec((tm, d), row), _const(g3.shape), _resident(w_gate.shape),
                  _resident(w_up.shape), _resident(w_down.shape), _const(g4.shape)],
        out_specs=[pl.BlockSpec((tm, d), row), pl.BlockSpec((tm, ff), row), pl.BlockSpec((tm, ff), row), pl.BlockSpec((tm, d), row),
                   pl.BlockSpec((tm, d), row), _const((8, 128)), _const(g4.shape)],
        out_shape=[jax.ShapeDtypeStruct((t, d), BF16), jax.ShapeDtypeStruct((t, ff), BF16), jax.ShapeDtypeStruct((t, ff), BF16),
                   jax.ShapeDtypeStruct((t, d), BF16), jax.ShapeDtypeStruct((t, d), F32), jax.ShapeDtypeStruct((8, 128), F32),
                   jax.ShapeDtypeStruct(g4.shape, F32)],
        compiler_params=_params("arbitrary"),
    )(h1, target, g3, w_gate, w_up, w_down, g4)


def _ffn_backward(h1, dh2, dd, gate, up, g3, w_gate, w_up, w_down):
    t, d = h1.shape
    ff = w_gate.shape[0]
    tm = min(TM_FFN, t)

    def body(h1_ref, dh2_ref, dd_ref, gate_ref, up_ref, g3_ref, wg_ref, wu_ref, wd_ref, dh1_ref, dgate_ref, dup_ref, act_ref, dg3_ref):
        @pl.when(pl.program_id(0) == 0)
        def _():
            dg3_ref[...] = jnp.zeros_like(dg3_ref)

        dact = _dot_nt(dd_ref[...], wd_ref[...])
        gate = gate_ref[...].astype(F32)
        up = up_ref[...].astype(F32)
        sig = jax.nn.sigmoid(gate)
        silu = gate * sig
        act_ref[...] = (silu * up).astype(BF16)
        dup = (dact * silu).astype(BF16)
        dgate = (dact * up * (sig * (1.0 + gate * (1.0 - sig)))).astype(BF16)
        dup_ref[...] = dup
        dgate_ref[...] = dgate
        df = _dot(dgate, wg_ref[...]) + _dot(dup, wu_ref[...])
        hat, rstd = _rms_stats(h1_ref[...])
        dh1, dg3 = _rms_bwd(hat, rstd, g3_ref[...], df)
        dg3_ref[...] += dg3
        dh1_ref[...] = dh2_ref[...] + dh1

    row = lambda i: (i, 0)
    return pl.pallas_call(
        body, name="ffn_backward", grid=(t // tm,),
        in_specs=[pl.BlockSpec((tm, d), row), pl.BlockSpec((tm, d), row), pl.BlockSpec((tm, d), row), pl.BlockSpec((tm, ff), row),
                  pl.BlockSpec((tm, ff), row), _const(g3.shape), _resident(w_gate.shape), _resident(w_up.shape), _resident(w_down.shape)],
        out_specs=[pl.BlockSpec((tm, d), row), pl.BlockSpec((tm, ff), row), pl.BlockSpec((tm, ff), row), pl.BlockSpec((tm, ff), row),
                   _const(g3.shape)],
        out_shape=[jax.ShapeDtypeStruct((t, d), F32), jax.ShapeDtypeStruct((t, ff), BF16), jax.ShapeDtypeStruct((t, ff), BF16),
                   jax.ShapeDtypeStruct((t, ff), BF16), jax.ShapeDtypeStruct(g3.shape, F32)],
        compiler_params=_params("arbitrary"),
    )(h1, dh2, dd, gate, up, g3, w_gate, w_up, w_down)


def _ffn_weight_grads(f, dd, dgate, dup, act):
    t, d = f.shape
    ff = dgate.shape[1]
    tm = min(TM_WGRAD, t)
    nt = t // tm
    fc = ff // FF_CHUNKS

    def body(f_ref, dd_ref, dgate_ref, dup_ref, act_ref, dwg_ref, dwu_ref, dwd_ref, acc_g, acc_u, acc_d, stage, sem):
        c, i = pl.program_id(0), pl.program_id(1)

        @pl.when(i == 0)
        def _():
            acc_g[...] = jnp.zeros_like(acc_g)
            acc_u[...] = jnp.zeros_like(acc_u)
            acc_d[...] = jnp.zeros_like(acc_d)

        fv = f_ref[...]
        acc_g[...] += _dot_tn(fv, dgate_ref[...])
        acc_u[...] += _dot_tn(fv, dup_ref[...])
        acc_d[...] += _dot_tn(act_ref[...], dd_ref[...])

        @pl.when(i == nt - 1)
        def _():
            rows = pl.ds(pl.multiple_of(c * fc, 16), fc)
            copies = []
            for k, (acc, out, transposed) in enumerate(((acc_d, dwd_ref, False), (acc_g, dwg_ref, True), (acc_u, dwu_ref, True))):
                if k >= 2:
                    copies[k - 2].wait()
                stage[k % 2] = (acc[...].T if transposed else acc[...]).astype(BF16)
                copies.append(pltpu.make_async_copy(stage.at[k % 2], out.at[rows, :], sem.at[k % 2]))
                copies[k].start()
            copies[-2].wait()
            copies[-1].wait()

    row = lambda c, i: (i, 0)
    col = lambda c, i: (i, c)
    return pl.pallas_call(
        body, name="ffn_weight_grads", grid=(FF_CHUNKS, nt),
        in_specs=[pl.BlockSpec((tm, d), row), pl.BlockSpec((tm, d), row), pl.BlockSpec((tm, fc), col), pl.BlockSpec((tm, fc), col),
                  pl.BlockSpec((tm, fc), col)],
        out_specs=[ANY, ANY, ANY],
        out_shape=[jax.ShapeDtypeStruct((ff, d), BF16)] * 3,
        scratch_shapes=[pltpu.VMEM((d, fc), F32), pltpu.VMEM((d, fc), F32), pltpu.VMEM((fc, d), F32), pltpu.VMEM((2, fc, d), BF16),
                        pltpu.SemaphoreType.DMA((2,))],
        compiler_params=_params("arbitrary", "arbitrary"),
    )(f, dd, dgate, dup, act)


def _adamw(w, g, m, v):
    m = ADAM_B1 * m + (1.0 - ADAM_B1) * g
    v = ADAM_B2 * v + (1.0 - ADAM_B2) * (g * g)
    m_hat = m / (1.0 - ADAM_B1 ** ADAM_STEP)
    v_hat = v / (1.0 - ADAM_B2 ** ADAM_STEP)
    return -ADAM_LR * (m_hat / (jnp.sqrt(v_hat) + ADAM_EPS) + ADAM_WD * w), m, v


def _sum_slabs(ref):
    total = ref[0].astype(F32)
    for i in range(1, N_DEV):
        total = total + ref[i].astype(F32)
    return total


def _adamw_rows(r, c):
    tr = r
    for cand in range(8, r, 8):
        if r % cand == 0 and cand * c <= ADAMW_BLOCK_ELEMS:
            tr = cand
    return r if r * c <= ADAMW_BLOCK_ELEMS else tr


def _reduce_adamw_carrying(parts, ws, ms, vs, to_exchange, whole):
    k, nx = len(ws), len(to_exchange)
    r, c = ws[0].shape
    tr = _adamw_rows(r, c)
    steps = r // tr

    def body(*refs):
        p_refs, w_refs, m_refs, v_refs = (refs[a * k:(a + 1) * k] for a in range(4))
        sent, outs = refs[4 * k:4 * k + nx], refs[4 * k + nx:8 * k + nx]
        landed, sems = refs[8 * k + nx:8 * k + 2 * nx], refs[8 * k + 2 * nx:]
        start, finish = _exchange_ops(sent, landed, whole, *sems)
        pl.when(pl.program_id(0) == 0)(start)
        for a in range(k):
            g = _sum_slabs(p_refs[a])
            outs[4 * a][...] = g
            outs[4 * a + 1][...], outs[4 * a + 2][...], outs[4 * a + 3][...] = _adamw(w_refs[a][...], g, m_refs[a][...], v_refs[a][...])
        pl.when(pl.program_id(0) == steps - 1)(finish)

    blk = pl.BlockSpec((tr, c), lambda i: (i, 0))
    out = pl.pallas_call(
        body, name="adamw_ffn_exchange_rest", grid=(steps,),
        in_specs=[pl.BlockSpec((N_DEV, tr, c), lambda i: (0, i, 0))] * k + [blk] * (3 * k) + [ANY] * nx,
        out_specs=[blk] * (4 * k) + [ANY] * nx,
        out_shape=[jax.ShapeDtypeStruct((r, c), F32)] * (4 * k)
        + [jax.ShapeDtypeStruct((N_DEV, *a.shape) if w else a.shape, a.dtype) for a, w in zip(to_exchange, whole)],
        scratch_shapes=_exchange_sems(nx),
        compiler_params=_params("arbitrary"),
    )(*parts, *ws, *ms, *vs, *to_exchange)
    return [tuple(out[4 * a:4 * a + 4]) for a in range(k)], out[4 * k:]


def _reduce_adamw(parts, w, m, v, name):
    r, c = w.shape
    tr = _adamw_rows(r, c)

    def body(p_ref, w_ref, m_ref, v_ref, g_out, d_out, m_out, v_out):
        g = _sum_slabs(p_ref)
        g_out[...] = g
        d_out[...], m_out[...], v_out[...] = _adamw(w_ref[...], g, m_ref[...], v_ref[...])

    blk = pl.BlockSpec((tr, c), lambda i: (i, 0))
    return pl.pallas_call(
        body, name=name, grid=(r // tr,),
        in_specs=[pl.BlockSpec((N_DEV, tr, c), lambda i: (0, i, 0)), blk, blk, blk],
        out_specs=[blk] * 4, out_shape=[jax.ShapeDtypeStruct((r, c), F32)] * 4,
        compiler_params=_params("arbitrary"),
    )(parts, w, m, v)


def _reduce_adamw_small(parts, ws, ms, vs, loss_parts):
    n = len(parts)

    def body(*refs):
        p_refs, w_refs, m_refs, v_refs = (refs[k * n:(k + 1) * n] for k in range(4))
        outs = refs[4 * n + 1:]
        outs[4 * n][...] = _sum_slabs(refs[4 * n])
        for a in range(n):
            g = _sum_slabs(p_refs[a])
            outs[4 * a][...] = g
            outs[4 * a + 1][...], outs[4 * a + 2][...], outs[4 * a + 3][...] = _adamw(w_refs[a][...], g, m_refs[a][...], v_refs[a][...])

    out = pl.pallas_call(
        body, name="adamw_replicated",
        out_shape=[jax.ShapeDtypeStruct(w.shape, F32) for w in ws for _ in range(4)] + [jax.ShapeDtypeStruct(loss_parts.shape[1:], F32)],
        compiler_params=pltpu.CompilerParams(vmem_limit_bytes=VMEM_LIMIT_BYTES),
    )(*parts, *ws, *ms, *vs, loss_parts)
    return [tuple(out[4 * a:4 * a + 4]) for a in range(n)], out[4 * n]


def kernel(x, meta_tokens, norm_mix_pre, w_in, conv_w, pool_w, pool_scale, w_out, norm_mix_post, norm_ffn_pre, w_gate, w_up, w_down, norm_ffn_post, loss_target, m_meta_tokens, m_norm_mix_pre, m_w_in, m_conv_w, m_pool_w, m_pool_scale, m_w_out, m_norm_mix_post, m_norm_ffn_pre, m_w_gate, m_w_up, m_w_down, m_norm_ffn_post, v_meta_tokens, v_norm_mix_pre, v_w_in, v_conv_w, v_pool_w, v_pool_scale, v_w_out, v_norm_mix_post, v_norm_ffn_pre, v_w_gate, v_w_up, v_w_down, v_norm_ffn_post):
    n_seq, seq, d = x.shape
    x2d = x.reshape(n_seq * seq, d)
    target = loss_target.reshape(n_seq * seq, d)

    t_ = lambda a: jnp.swapaxes(a[0], 0, 1)
    (win_s, wout_s, meta_s, conv_s), ffn_shards = _gather_first_weights(
        [w_in[0], w_out[0], meta_tokens, conv_w[0]], [BF16, BF16, F32, F32], [t_(w_gate), t_(w_up), w_down[0]])
    (win_b,) = _columns_from_slabs([win_s])
    wout_b = wout_s.reshape(d, d)
    meta = jnp.transpose(meta_s, (1, 0, 2)).reshape(N_META, d)
    conv = jnp.transpose(conv_s, (1, 0, 2)).reshape(CONV_WIDTH, -1)
    pw, ps = pool_w[0], pool_scale

    a_meta, z_meta = _meta_forward(meta, norm_mix_pre, win_b)
    (h1, z, m), ffn_slabs = _mixer_forward(x2d, z_meta, norm_mix_pre, win_b, conv, pw, ps, wout_b, norm_mix_post, n_seq, ffn_shards)
    wg_b, wu_b, wd_b = (s.reshape(-1, d) for s in ffn_slabs)
    f, gate, up, dd, dh2, loss_sum, dg4 = _ffn_forward(h1, target, norm_ffn_pre, wg_b, wu_b, wd_b, norm_ffn_post)
    dh1, dgate, dup, act, dg3 = _ffn_backward(h1, dh2, dd, gate, up, norm_ffn_pre, wg_b, wu_b, wd_b)
    ffn_grads = _ffn_weight_grads(f, dd, dgate, dup, act)
    (gx, dwin, dwout, dg1, dg2, dconv, dpw, dps, dmeta), ffn_parts = _mixer_backward(
        x2d, dh1, m, z, meta, a_meta, z_meta, norm_mix_pre, win_b, conv, pw, ps, wout_b, norm_mix_post, n_seq,
        [g.reshape(N_DEV, -1, d) for g in ffn_grads])

    dmeta_s = jnp.transpose(dmeta.reshape(N_META, N_DEV, -1), (1, 0, 2))
    dconv_s = jnp.transpose(dconv.reshape(CONV_WIDTH, N_DEV, -1), (1, 0, 2))
    ffn_res, last = _reduce_adamw_carrying(
        ffn_parts, [t_(w_gate), t_(w_up), w_down[0]], [t_(m_w_gate), t_(m_w_up), m_w_down[0]], [t_(v_w_gate), t_(v_w_up), v_w_down[0]],
        [dwin, dwout.reshape(N_DEV, -1, d), dmeta_s, dconv_s, dg1, dg2, dg3, dg4, dpw, dps, loss_sum], [False] * 4 + [True] * 7)
    replicated = last[4:10]

    names = ["meta_tokens", "norm_mix_pre", "w_in", "conv_w", "pool_w", "pool_scale", "w_out", "norm_mix_post", "norm_ffn_pre", "w_gate",
             "w_up", "w_down", "norm_ffn_post"]
    res = {"w_gate": tuple(jnp.swapaxes(o, 0, 1)[None] for o in ffn_res[0]),
           "w_up": tuple(jnp.swapaxes(o, 0, 1)[None] for o in ffn_res[1]), "w_down": tuple(o[None] for o in ffn_res[2])}
    for nm, parts, w, m_, v_ in (("w_in", last[0], w_in, m_w_in, v_w_in), ("w_out", last[1], w_out, m_w_out, v_w_out),
                                 ("conv_w", last[3], conv_w, m_conv_w, v_conv_w)):
        res[nm] = tuple(o[None] for o in _reduce_adamw(parts, w[0], m_[0], v_[0], "adamw_" + nm))
    res["meta_tokens"] = tuple(_reduce_adamw(last[2], meta_tokens, m_meta_tokens, v_meta_tokens, "adamw_meta_tokens"))
    small, loss = _reduce_adamw_small(
        replicated, [norm_mix_pre, norm_mix_post, norm_ffn_pre, norm_ffn_post, pool_w[0], pool_scale],
        [m_norm_mix_pre, m_norm_mix_post, m_norm_ffn_pre, m_norm_ffn_post, m_pool_w[0], m_pool_scale],
        [v_norm_mix_pre, v_norm_mix_post, v_norm_ffn_pre, v_norm_ffn_post, v_pool_w[0], v_pool_scale], last[10])
    for nm, r in zip(["norm_mix_pre", "norm_mix_post", "norm_ffn_pre", "norm_ffn_post", "pool_w", "pool_scale"], small):
        res[nm] = tuple(o[None] for o in r) if nm == "pool_w" else r

    return (loss[0, 0], gx.reshape(n_seq, seq, d), *[res[nm][0] for nm in names], *[res[nm][1] for nm in names],
            *[res[nm][2] for nm in names], *[res[nm][3] for nm in names])
```

```python
import functools

import jax
import jax.numpy as jnp
from jax import lax
from jax.experimental import pallas as pl
from jax.experimental.pallas import tpu as pltpu

F32, BF16 = jnp.float32, jnp.bfloat16
RMS_EPS = 1e-6
N_META = 16
CONV_WIDTH = 3
POOL_WINDOWS = (2, 4, 8, 16)
POOL_GROUP = 128
HALO = 16
N_DEV = 8
MESH_AXES = ("x", "y", "c")
MESH = pl.DeviceIdType.MESH
VMEM_LIMIT_BYTES = 56 * 1024 * 1024
ADAMW_BLOCK_ELEMS = 64 * 1024
TM_MIX = 512
TM_FFN = 256
FFN_CHUNK = 512
FFN_BACKWARD_LAG = 2
TM_WGRAD = 512
FF_CHUNKS = 2

ADAM_LR, ADAM_B1, ADAM_B2, ADAM_EPS, ADAM_WD, ADAM_STEP = 0.001, 0.9, 0.999, 1e-08, 0.01, 10


def _dot(a, b):
    return jnp.dot(a, b, preferred_element_type=F32)


def _dot_nt(a, b):
    return lax.dot_general(a, b, (((1,), (1,)), ((), ())), preferred_element_type=F32)


def _dot_tn(a, b):
    return lax.dot_general(a, b, (((0,), (0,)), ((), ())), preferred_element_type=F32)


def _rms_stats(h):
    rstd = lax.rsqrt(jnp.mean(h * h, axis=-1, keepdims=True) + RMS_EPS)
    return h * rstd, rstd


def _rms_bwd(hat, rstd, g, dy):
    gdy = dy * g
    proj = jnp.mean(gdy * hat, axis=-1, keepdims=True)
    return rstd * (gdy - hat * proj), jnp.sum(dy * hat, axis=0, keepdims=True)


def _params(*semantics):
    return pltpu.CompilerParams(dimension_semantics=semantics or None, vmem_limit_bytes=VMEM_LIMIT_BYTES)


def _resident(shape):
    zeros = (0,) * len(shape)
    return pl.BlockSpec(shape, lambda *_: zeros, pipeline_mode=pl.Buffered(1))


def _const(shape):
    zeros = (0,) * len(shape)
    return pl.BlockSpec(shape, lambda *_: zeros)


ANY = pl.BlockSpec(memory_space=pl.ANY)


def _my_place():
    x, y, c = (lax.axis_index(a) for a in MESH_AXES)
    return x, y, c


def _exchange_sems(n):
    return [pltpu.SemaphoreType.DMA((n, N_DEV - 1)), pltpu.SemaphoreType.DMA((n, N_DEV - 1)), pltpu.SemaphoreType.DMA((n,))]


def _gather_ops(srcs, outs, send_sems, recv_sems, local_sems):
    n = len(srcs)
    x, y, c = _my_place()
    me, sibling = (x, y, c), (x, y, 1 - c)
    chips = [(1 - x, y), (x, 1 - y), (1 - x, 1 - y)]

    def slab(px, py, pc):
        return 4 * px + 2 * py + pc

    def copy(a, k, block, to, src=None):
        dst = outs[a].at[slab(*block)]
        return pltpu.make_async_remote_copy(
            src_ref=dst if src is None else src, dst_ref=dst, send_sem=send_sems.at[a, k], recv_sem=recv_sems.at[a, k],
            device_id=to, device_id_type=MESH)

    def mine(a):
        return pltpu.make_async_copy(srcs[a], outs[a].at[slab(*me)], local_sems.at[a])

    def first(a):
        return [copy(a, 0, me, sibling, src=srcs[a])] + [copy(a, 1 + j, me, (*chip, c), src=srcs[a]) for j, chip in enumerate(chips)]

    def passed(a, j):
        return copy(a, 4 + j, (*chips[j], c), sibling)

    def start():
        for a in range(n):
            mine(a).start()
            for cp in first(a):
                cp.start()

    def forward():
        for j, chip in enumerate(chips):
            for a in range(n):
                copy(a, 1 + j, (*chip, c), me).wait_recv()
                passed(a, j).start()

    def finish():
        for a in range(n):
            copy(a, 0, sibling, me).wait_recv()
            for j, chip in enumerate(chips):
                copy(a, 4 + j, (*chip, 1 - c), me).wait_recv()
        for a in range(n):
            for cp in first(a) + [passed(a, j) for j in range(len(chips))]:
                cp.wait_send()
            mine(a).wait()

    return start, forward, finish


def _exchange_ops(ins, outs, whole, send_sems, recv_sems, local_sems):
    n = len(ins)
    x, y, c = _my_place()
    me = 4 * x + 2 * y + c

    def src(a, i):
        return ins[a] if whole[a] else ins[a].at[i]

    def mine(a):
        return pltpu.make_async_copy(src(a, me), outs[a].at[me], local_sems.at[a])

    def send(a, k):
        to = (me + k) % N_DEV
        return pltpu.make_async_remote_copy(
            src_ref=src(a, to), dst_ref=outs[a].at[me], send_sem=send_sems.at[a, k - 1], recv_sem=recv_sems.at[a, k - 1],
            device_id=(to // 4, (to // 2) % 2, to % 2), device_id_type=MESH)

    def landed(a, k):
        frm = (me + N_DEV - k) % N_DEV
        return pltpu.make_async_remote_copy(
            src_ref=src(a, frm), dst_ref=outs[a].at[frm], send_sem=send_sems.at[a, k - 1], recv_sem=recv_sems.at[a, k - 1],
            device_id=(x, y, c), device_id_type=MESH)

    def start():
        for a in range(n):
            mine(a).start()
            for k in range(1, N_DEV):
                send(a, k).start()

    def finish():
        for a in range(n):
            for k in range(1, N_DEV):
                landed(a, k).wait_recv()
        for a in range(n):
            for k in range(1, N_DEV):
                send(a, k).wait_send()
            mine(a).wait()

    return start, finish


def _gather_first_weights(gathered, dtypes, cast_only):
    n, k = len(gathered), len(cast_only)

    def body(*refs):
        ins, casts_in = refs[:n], refs[n:n + k]
        outs, casts_out = refs[n + k:2 * n + k], refs[2 * n + k:2 * n + 2 * k]
        stages = refs[2 * n + 2 * k:3 * n + 2 * k]
        start, forward, finish = _gather_ops(stages, outs, *refs[3 * n + 2 * k:])
        for a in range(n):
            stages[a][...] = ins[a][...].astype(stages[a].dtype)
        start()
        for a in range(k):
            casts_out[a][...] = casts_in[a][...].astype(BF16)
        forward()
        finish()

    vmem = pl.BlockSpec(memory_space=pltpu.VMEM)
    out = pl.pallas_call(
        body, name="gather_first_weights",
        out_shape=[jax.ShapeDtypeStruct((N_DEV, *s.shape), d) for s, d in zip(gathered, dtypes)]
        + [jax.ShapeDtypeStruct(s.shape, BF16) for s in cast_only],
        in_specs=[vmem] * (n + k), out_specs=[ANY] * n + [vmem] * k,
        scratch_shapes=[pltpu.VMEM(s.shape, d) for s, d in zip(gathered, dtypes)] + _exchange_sems(n),
        compiler_params=pltpu.CompilerParams(vmem_limit_bytes=VMEM_LIMIT_BYTES),
    )(*gathered, *cast_only)
    return out[:n], out[n:]


def _exchange(arrays, whole, name):
    n = len(arrays)

    def body(*refs):
        start, finish = _exchange_ops(refs[:n], refs[n:2 * n], whole, *refs[2 * n:])
        start()
        finish()

    return pl.pallas_call(
        body, name=name,
        out_shape=[jax.ShapeDtypeStruct((N_DEV, *a.shape) if w else a.shape, a.dtype) for a, w in zip(arrays, whole)],
        in_specs=[ANY] * n, out_specs=[ANY] * n, scratch_shapes=_exchange_sems(n),
    )(*arrays)


def _columns_from_slabs(slabs):
    def body(*refs):
        k = len(refs) // 2
        for src, dst in zip(refs[:k], refs[k:]):
            n = src.shape[2]
            for i in range(N_DEV):
                dst[:, pl.ds(n * i, n)] = src[i]

    return pl.pallas_call(
        body, name="columns_from_slabs",
        out_shape=[jax.ShapeDtypeStruct((s.shape[1], N_DEV * s.shape[2]), s.dtype) for s in slabs],
        compiler_params=pltpu.CompilerParams(vmem_limit_bytes=VMEM_LIMIT_BYTES),
    )(*slabs)


def _mixer_core(z, ext_u, ext_p, conv_ref, pool_w_ref, tm):
    c_w = z.shape[1] // 4
    b, c, v, p = z[:, :c_w], z[:, c_w:2 * c_w], z[:, 2 * c_w:3 * c_w], z[:, 3 * c_w:]
    u = c * v
    ext_u[pl.ds(HALO, tm), :] = u
    ext_p[pl.ds(HALO, tm), :] = p
    u1 = ext_u[pl.ds(HALO - 1, tm), :]
    u2 = ext_u[pl.ds(HALO - 2, tm), :]
    yc = conv_ref[pl.ds(2, 1), :] * u + conv_ref[pl.ds(1, 1), :] * u1 + conv_ref[pl.ds(0, 1), :] * u2
    pooled, mixed = [], []
    for g, win in enumerate(POOL_WINDOWS):
        lanes = pl.ds(POOL_GROUP * g, POOL_GROUP)
        pg = p[:, POOL_GROUP * g:POOL_GROUP * (g + 1)]
        s = pg
        for k in range(1, win):
            s = s + ext_p[pl.ds(HALO - k, tm), lanes]
        pooled.append((s * (1.0 / win) - pg).astype(BF16))
        mixed.append(_dot(pooled[-1], pool_w_ref[g].astype(BF16)))
    return b, c, v, u, u1, u2, yc, pooled, mixed


def _meta_forward(meta, g1, w_in):
    def body(meta_ref, g1_ref, w_ref, a_ref, z_ref):
        hat, _ = _rms_stats(meta_ref[...])
        a = (hat * g1_ref[...]).astype(BF16)
        a_ref[...] = a
        z_ref[...] = _dot(a, w_ref[...])

    return pl.pallas_call(
        body, name="meta_forward",
        out_shape=[jax.ShapeDtypeStruct(meta.shape, BF16), jax.ShapeDtypeStruct((N_META, w_in.shape[1]), F32)],
        compiler_params=pltpu.CompilerParams(vmem_limit_bytes=VMEM_LIMIT_BYTES),
    )(meta, g1, w_in)


def _mixer_forward(x2d, z_meta, g1, w_in, conv_w, pool_w, pool_scale, w_out, g2, n_seq, to_gather):
    t, d = x2d.shape
    zw = w_in.shape[1]
    cw = zw // 4
    s = t // n_seq
    tm = min(TM_MIX, s)
    nj = s // tm
    ng = len(to_gather)

    def body(x_ref, zm_ref, g1_ref, win_ref, conv_ref, pw_ref, ps_ref, wout_ref, g2_ref, *rest):
        shards, (h1_ref, z_ref, m_ref), slabs = rest[:ng], rest[ng:ng + 3], rest[ng + 3:2 * ng + 3]
        ext_u, ext_p = rest[2 * ng + 3:2 * ng + 5]
        start, forward, finish = _gather_ops(shards, slabs, *rest[2 * ng + 5:])
        pl.when((pl.program_id(0) == 0) & (pl.program_id(1) == 0))(start)

        @pl.when(pl.program_id(1) == 0)
        def _():
            zm = zm_ref[...]
            ext_u[pl.ds(0, HALO), :] = zm[:, cw:2 * cw] * zm[:, 2 * cw:3 * cw]
            ext_p[pl.ds(0, HALO), :] = zm[:, 3 * cw:]

        h0 = x_ref[...]
        hat, _ = _rms_stats(h0)
        z = _dot((hat * g1_ref[...]).astype(BF16), win_ref[...])
        z_ref[...] = z.astype(BF16)
        b, _, _, _, _, _, yc, _, mixed = _mixer_core(z, ext_u, ext_p, conv_ref, pw_ref, tm)
        ps = ps_ref[...]
        y = [b * yc] + [mixed[g] * ps[:, POOL_GROUP * g:POOL_GROUP * (g + 1)] for g in range(len(POOL_WINDOWS))]
        m = _dot(jnp.concatenate(y, axis=1).astype(BF16), wout_ref[...])
        m_ref[...] = m
        m_hat, _ = _rms_stats(m)
        h1_ref[...] = h0 + m_hat * g2_ref[...]
        ext_u[pl.ds(0, HALO), :] = ext_u[pl.ds(tm, HALO), :]
        ext_p[pl.ds(0, HALO), :] = ext_p[pl.ds(tm, HALO), :]

        @pl.when((pl.program_id(0) == n_seq - 1) & (pl.program_id(1) == nj - 1))
        def _():
            forward()
            finish()

    row = lambda b, j: (b * nj + j, 0)
    out = pl.pallas_call(
        body, name="mixer_forward", grid=(n_seq, nj),
        in_specs=[pl.BlockSpec((tm, d), row), _const(z_meta.shape), _const(g1.shape), _resident(w_in.shape), _const(conv_w.shape),
                  _const(pool_w.shape), _const(pool_scale.shape), _resident(w_out.shape), _const(g2.shape)] + [ANY] * ng,
        out_specs=[pl.BlockSpec((tm, d), row), pl.BlockSpec((tm, zw), row), pl.BlockSpec((tm, d), row)] + [ANY] * ng,
        out_shape=[jax.ShapeDtypeStruct((t, d), F32), jax.ShapeDtypeStruct((t, zw), BF16), jax.ShapeDtypeStruct((t, d), F32)]
        + [jax.ShapeDtypeStruct((N_DEV, *a.shape), a.dtype) for a in to_gather],
        scratch_shapes=[pltpu.VMEM((tm + HALO, cw), F32), pltpu.VMEM((tm + HALO, cw), F32)] + _exchange_sems(ng),
        compiler_params=_params("arbitrary", "arbitrary"),
    )(x2d, z_meta, g1, w_in, conv_w, pool_w, pool_scale, w_out, g2, *to_gather)
    return out[:3], out[3:]


def _mixer_backward(x2d, dh1, m, z, meta, a_meta, z_meta, g1, w_in, conv_w, pool_w, pool_scale, w_out, g2, n_seq, to_exchange):
    t, d = x2d.shape
    zw = w_in.shape[1]
    cw = zw // 4
    s = t // n_seq
    tm = min(TM_MIX, s)
    nj = s // tm
    n_groups = len(POOL_WINDOWS)
    zs = zw // N_DEV
    nx = len(to_exchange)

    def body(x_ref, dh1_ref, m_ref, z_ref, zprev_ref, meta_ref, am_ref, zm_ref, g1_ref, win_ref, conv_ref, pw_ref, ps_ref, wout_ref,
             g2_ref, *rest):
        sent, rest = rest[:nx], rest[nx:]
        gx_ref, dwin_ref, dwout_ref, dg1_ref, dg2_ref, dconv_ref, dpw_ref, dps_ref, dmeta_ref = rest[:9]
        landed, rest = rest[9:9 + nx], rest[9 + nx:]
        ext_u, ext_p, ext_dyc, ext_dq, acc_win, acc_wout, stage16, sem = rest[:8]
        start, finish = _exchange_ops(sent, landed, [False] * nx, *rest[8:])
        b_id, j = pl.program_id(0), pl.program_id(1)
        jr = nj - 1 - j
        pl.when((b_id == 0) & (j == 0))(start)

        @pl.when((b_id == 0) & (j == 0))
        def _():
            acc_win[...] = jnp.zeros_like(acc_win)
            acc_wout[...] = jnp.zeros_like(acc_wout)
            for r in (dg1_ref, dg2_ref, dconv_ref, dpw_ref, dps_ref, dmeta_ref):
                r[...] = jnp.zeros_like(r)

        @pl.when(j == 0)
        def _():
            ext_dyc[pl.ds(tm, HALO), :] = jnp.zeros((HALO, cw), F32)
            ext_dq[pl.ds(tm, HALO), :] = jnp.zeros((HALO, cw), F32)

        zm = zm_ref[...]
        halo = jnp.where(jr == 0, zm, zprev_ref[...].astype(F32))
        ext_u[pl.ds(0, HALO), :] = halo[:, cw:2 * cw] * halo[:, 2 * cw:3 * cw]
        ext_p[pl.ds(0, HALO), :] = halo[:, 3 * cw:]

        h0 = x_ref[...]
        hat0, rstd0 = _rms_stats(h0)
        g1 = g1_ref[...]
        a = (hat0 * g1).astype(BF16)
        b, c, v, u, u1, u2, yc, pooled, mixed = _mixer_core(z_ref[...].astype(F32), ext_u, ext_p, conv_ref, pw_ref, tm)
        ps = ps_ref[...]
        y = [b * yc] + [mixed[g] * ps[:, POOL_GROUP * g:POOL_GROUP * (g + 1)] for g in range(n_groups)]
        ycat = jnp.concatenate(y, axis=1).astype(BF16)

        dh1v = dh1_ref[...]
        m_hat, m_rstd = _rms_stats(m_ref[...])
        dm, dg2 = _rms_bwd(m_hat, m_rstd, g2_ref[...], dh1v)
        dg2_ref[...] += dg2
        dm = dm.astype(BF16)
        acc_wout[...] += _dot_tn(ycat, dm)
        dycat = _dot_nt(dm, wout_ref[...])

        dyconv = dycat[:, :cw]
        db = dyconv * yc
        dyc = dyconv * b
        ext_dyc[pl.ds(0, tm), :] = dyc
        du = (conv_ref[pl.ds(2, 1), :] * dyc + conv_ref[pl.ds(1, 1), :] * ext_dyc[pl.ds(1, tm), :]
              + conv_ref[pl.ds(0, 1), :] * ext_dyc[pl.ds(2, tm), :])
        dconv_ref[pl.ds(2, 1), :] += jnp.sum(dyc * u, axis=0, keepdims=True)
        dconv_ref[pl.ds(1, 1), :] += jnp.sum(dyc * u1, axis=0, keepdims=True)
        dconv_ref[pl.ds(0, 1), :] += jnp.sum(dyc * u2, axis=0, keepdims=True)

        dp = []
        for g, win in enumerate(POOL_WINDOWS):
            lanes = pl.ds(POOL_GROUP * g, POOL_GROUP)
            dypool = dycat[:, cw + POOL_GROUP * g:cw + POOL_GROUP * (g + 1)]
            dps_ref[:, lanes] += jnp.sum(dypool * mixed[g], axis=0, keepdims=True)
            dmixed = (dypool * ps[:, POOL_GROUP * g:POOL_GROUP * (g + 1)]).astype(BF16)
            dpw_ref[g] += _dot_tn(pooled[g], dmixed)
            dq = _dot_nt(dmixed, pw_ref[g].astype(BF16))
            ext_dq[pl.ds(0, tm), lanes] = dq
            acc = dq
            for k in range(1, win):
                acc = acc + ext_dq[pl.ds(k, tm), lanes]
            dp.append(acc * (1.0 / win) - dq)

        dz = jnp.concatenate([db, du * v, du * c] + dp, axis=1).astype(BF16)
        acc_win[...] += _dot_tn(a, dz)
        dh0, dg1 = _rms_bwd(hat0, rstd0, g1, _dot_nt(dz, win_ref[...]))
        dg1_ref[...] += dg1
        gx_ref[...] = dh1v + dh0

        ext_dyc[pl.ds(tm, HALO), :] = ext_dyc[pl.ds(0, HALO), :]
        ext_dq[pl.ds(tm, HALO), :] = ext_dq[pl.ds(0, HALO), :]

        @pl.when(jr == 0)
        def _():
            ext_dyc[pl.ds(tm - HALO, HALO), :] = jnp.zeros((HALO, cw), F32)
            ext_dq[pl.ds(tm - HALO, HALO), :] = jnp.zeros((HALO, cw), F32)
            du_m = (conv_ref[pl.ds(1, 1), :] * ext_dyc[pl.ds(tm - HALO + 1, HALO), :]
                    + conv_ref[pl.ds(0, 1), :] * ext_dyc[pl.ds(tm - HALO + 2, HALO), :])
            dp_m = []
            for g, win in enumerate(POOL_WINDOWS):
                lanes = pl.ds(POOL_GROUP * g, POOL_GROUP)
                acc = ext_dq[pl.ds(tm - HALO + 1, HALO), lanes]
                for k in range(2, win):
                    acc = acc + ext_dq[pl.ds(tm - HALO + k, HALO), lanes]
                dp_m.append(acc * (1.0 / win))
            dz_m = jnp.concatenate([jnp.zeros((HALO, cw), F32), du_m * zm[:, 2 * cw:3 * cw], du_m * zm[:, cw:2 * cw]] + dp_m,
                                   axis=1).astype(BF16)
            acc_win[...] += _dot_tn(am_ref[...], dz_m)
            hat_m, rstd_m = _rms_stats(meta_ref[...])
            dmeta, dg1_m = _rms_bwd(hat_m, rstd_m, g1, _dot_nt(dz_m, win_ref[...]))
            dg1_ref[...] += dg1_m
            dmeta_ref[...] += dmeta

        @pl.when((b_id == n_seq - 1) & (j == nj - 1))
        def _():
            pieces = [(acc_win, zs * i, dwin_ref.at[i]) for i in range(N_DEV)]
            pieces += [(acc_wout, zs * i, dwout_ref.at[:, pl.ds(zs * i, zs)]) for i in range(d // zs)]
            copies = []
            for k, (acc, col, dst) in enumerate(pieces):
                if k >= 2:
                    copies[k - 2].wait()
                stage16[k % 2] = acc[:, pl.ds(col, zs)].astype(BF16)
                copies.append(pltpu.make_async_copy(stage16.at[k % 2], dst, sem.at[k % 2]))
                copies[k].start()
            copies[-2].wait()
            copies[-1].wait()
            finish()

    row = lambda b, j: (b * nj + nj - 1 - j, 0)
    prev = lambda b, j: (jnp.maximum((b * s + (nj - 1 - j) * tm) // HALO - 1, 0), 0)
    small = [g1.shape, g2.shape, conv_w.shape, pool_w.shape, pool_scale.shape, meta.shape]
    out = pl.pallas_call(
        body, name="mixer_backward", grid=(n_seq, nj),
        in_specs=[pl.BlockSpec((tm, d), row), pl.BlockSpec((tm, d), row), pl.BlockSpec((tm, d), row), pl.BlockSpec((tm, zw), row),
                  pl.BlockSpec((HALO, zw), prev), _const(meta.shape), _const(a_meta.shape), _const(z_meta.shape), _const(g1.shape),
                  _resident(w_in.shape), _const(conv_w.shape), _const(pool_w.shape), _const(pool_scale.shape), _resident(w_out.shape),
                  _const(g2.shape)] + [ANY] * nx,
        out_specs=[pl.BlockSpec((tm, d), row), ANY, ANY] + [_const(sh) for sh in small] + [ANY] * nx,
        out_shape=[jax.ShapeDtypeStruct((t, d), F32), jax.ShapeDtypeStruct((N_DEV, d, zs), BF16),
                   jax.ShapeDtypeStruct(w_out.shape, BF16)] + [jax.ShapeDtypeStruct(sh, F32) for sh in small]
        + [jax.ShapeDtypeStruct(a.shape, a.dtype) for a in to_exchange],
        scratch_shapes=[pltpu.VMEM((tm + HALO, cw), F32)] * 4
        + [pltpu.VMEM(w_in.shape, F32), pltpu.VMEM(w_out.shape, F32), pltpu.VMEM((2, d, zs), BF16),
           pltpu.SemaphoreType.DMA((2,))] + _exchange_sems(nx),
        compiler_params=_params("arbitrary", "arbitrary"),
    )(x2d, dh1, m, z, z, meta, a_meta, z_meta, g1, w_in, conv_w, pool_w, pool_scale, w_out, g2, *to_exchange)
    return out[:9], out[9:]


def _ffn_forward_backward(h1, target, g3, w_gate, w_up, w_down, g4):
    t, d = h1.shape
    ff = w_gate.shape[0]
    tm = min(TM_FFN, t)
    nt = t // tm
    chunks = [(s, min(FFN_CHUNK, ff - s)) for s in range(0, ff, FFN_CHUNK)]

    def body(h1_ref, h1pp_ref, tgt_ref, g3_ref, wg_ref, wu_ref, wd_ref, g4_ref,
             f_ref, act_ref, dd_ref, dgate_ref, dup_ref, dh1_ref, loss_ref, dg3_ref, dg4_ref, *slots):
        gate_s, up_s, dd_s, dh2_s, df_s = (slots[2 * k:2 * k + 2] for k in range(5))
        i = pl.program_id(0)

        def forward(slot):
            h1v = h1_ref[...]
            hat, _ = _rms_stats(h1v)
            f = (hat * g3_ref[...]).astype(BF16)
            f_ref[...] = f
            s, n = chunks[0]
            gate, up = _dot_nt(f_ref[...], wg_ref[pl.ds(s, n), :]), _dot_nt(f_ref[...], wu_ref[pl.ds(s, n), :])
            yield
            down = None
            for k, (s, n) in enumerate(chunks):
                gate_s[slot][:, pl.ds(s, n)] = gate.astype(BF16)
                up_s[slot][:, pl.ds(s, n)] = up.astype(BF16)
                act = (gate * jax.nn.sigmoid(gate) * up).astype(BF16)
                act_ref[:, pl.ds(s, n)] = act
                if k + 1 < len(chunks):
                    s1, n1 = chunks[k + 1]
                    gate, up = _dot_nt(f_ref[...], wg_ref[pl.ds(s1, n1), :]), _dot_nt(f_ref[...], wu_ref[pl.ds(s1, n1), :])
                yield
                part = _dot(act_ref[:, pl.ds(s, n)], wd_ref[pl.ds(s, n), :])
                down = part if down is None else down + part
                yield
            d_hat, d_rstd = _rms_stats(down)
            g4 = g4_ref[...]
            err = h1v + d_hat * g4 - tgt_ref[...]
            loss_ref[...] += jnp.sum(err * err) * (0.5 / d)
            dh2 = err * (1.0 / d)
            dh2_s[slot][...] = dh2
            dd, dg4 = _rms_bwd(d_hat, d_rstd, g4, dh2)
            dg4_ref[...] += dg4
            dd = dd.astype(BF16)
            dd_ref[...] = dd
            dd_s[slot][...] = dd

        def backward(slot):
            s, n = chunks[0]
            dact = _dot_nt(dd_s[slot][...], wd_ref[pl.ds(s, n), :])
            yield
            df = None
            for k, (s, n) in enumerate(chunks):
                gate = gate_s[slot][:, pl.ds(s, n)].astype(F32)
                up = up_s[slot][:, pl.ds(s, n)].astype(F32)
                sig = jax.nn.sigmoid(gate)
                dup = (dact * (gate * sig)).astype(BF16)
                dgate = (dact * up * (sig * (1.0 + gate * (1.0 - sig)))).astype(BF16)
                dup_ref[:, pl.ds(s, n)] = dup
                dgate_ref[:, pl.ds(s, n)] = dgate
                if k + 1 < len(chunks):
                    s1, n1 = chunks[k + 1]
                    dact = _dot_nt(dd_s[slot][...], wd_ref[pl.ds(s1, n1), :])
                yield
                part = _dot(dgate_ref[:, pl.ds(s, n)], wg_ref[pl.ds(s, n), :]) + _dot(dup_ref[:, pl.ds(s, n)], wu_ref[pl.ds(s, n), :])
                df = part if df is None else df + part
                yield
            df_s[slot][...] = df

        def last(slot):
            hat, rstd = _rms_stats(h1pp_ref[...])
            dh1, dg3 = _rms_bwd(hat, rstd, g3_ref[...], df_s[slot][...])
            dg3_ref[...] += dg3
            dh1_ref[...] = dh2_s[slot][...] + dh1

        def emit(parity, with_forward, with_backward, with_last):
            fwd = forward(parity) if with_forward else iter(())
            bwd = backward(1 - parity) if with_backward else iter(())
            next(fwd, None)
            if with_last:
                last(parity)
            for _ in range(FFN_BACKWARD_LAG):
                next(fwd, None)
            alive = True
            while alive:
                alive = next(bwd, True) is None
                alive = (next(fwd, True) is None) or alive

        @pl.when(i == 0)
        def _():
            for r in (loss_ref, dg3_ref, dg4_ref):
                r[...] = jnp.zeros_like(r)
            emit(0, True, False, False)

        @pl.when(i == 1)
        def _():
            emit(1, True, True, False)

        for parity in (0, 1):
            @pl.when((i >= 2) & (i < nt) & (i % 2 == parity))
            def _():
                emit(parity, True, True, True)

        @pl.when(i == nt)
        def _():
            emit(nt % 2, False, True, True)

        @pl.when(i == nt + 1)
        def _():
            emit((nt + 1) % 2, False, False, True)

    cur = lambda i: (jnp.minimum(i, nt - 1), 0)
    prev = lambda i: (jnp.clip(i - 1, 0, nt - 1), 0)
    prev2 = lambda i: (jnp.clip(i - 2, 0, nt - 1), 0)
    return pl.pallas_call(
        body, name="ffn_forward_backward", grid=(nt + 2,),
        in_specs=[pl.BlockSpec((tm, d), cur), pl.BlockSpec((tm, d), prev2), pl.BlockSpec((tm, d), cur), _const(g3.shape),
                  _resident(w_gate.shape), _resident(w_up.shape), _resident(w_down.shape), _const(g4.shape)],
        out_specs=[pl.BlockSpec((tm, d), cur), pl.BlockSpec((tm, ff), cur), pl.BlockSpec((tm, d), cur), pl.BlockSpec((tm, ff), prev),
                   pl.BlockSpec((tm, ff), prev), pl.BlockSpec((tm, d), prev2), _const((8, 128)), _const(g3.shape), _const(g4.shape)],
        out_shape=[jax.ShapeDtypeStruct((t, d), BF16), jax.ShapeDtypeStruct((t, ff), BF16), jax.ShapeDtypeStruct((t, d), BF16),
                   jax.ShapeDtypeStruct((t, ff), BF16), jax.ShapeDtypeStruct((t, ff), BF16), jax.ShapeDtypeStruct((t, d), F32),
                   jax.ShapeDtypeStruct((8, 128), F32), jax.ShapeDtypeStruct(g3.shape, F32), jax.ShapeDtypeStruct(g4.shape, F32)],
        scratch_shapes=[pltpu.VMEM((tm, ff), BF16)] * 4 + [pltpu.VMEM((tm, d), BF16)] * 2 + [pltpu.VMEM((tm, d), F32)] * 4,
        compiler_params=_params("arbitrary"),
    )(h1, h1, target, g3, w_gate, w_up, w_down, g4)


def _ffn_weight_grads(f, dd, dgate, dup, act):
    t, d = f.shape
    ff = dgate.shape[1]
    tm = min(TM_WGRAD, t)
    nt = t // tm
    fc = ff // FF_CHUNKS

    def body(f_ref, dd_ref, dgate_ref, dup_ref, act_ref, dwg_ref, dwu_ref, dwd_ref, acc_g, acc_u, acc_d, stage, sem):
        c, i = pl.program_id(0), pl.program_id(1)

        @pl.when(i == 0)
        def _():
            acc_g[...] = jnp.zeros_like(acc_g)
            acc_u[...] = jnp.zeros_like(acc_u)
            acc_d[...] = jnp.zeros_like(acc_d)

        fv = f_ref[...]
        acc_g[...] += _dot_tn(fv, dgate_ref[...])
        acc_u[...] += _dot_tn(fv, dup_ref[...])
        acc_d[...] += _dot_tn(act_ref[...], dd_ref[...])

        @pl.when(i == nt - 1)
        def _():
            rows = pl.ds(pl.multiple_of(c * fc, 16), fc)
            copies = []
            for k, (acc, out, transposed) in enumerate(((acc_d, dwd_ref, False), (acc_g, dwg_ref, True), (acc_u, dwu_ref, True))):
                if k >= 2:
                    copies[k - 2].wait()
                stage[k % 2] = (acc[...].T if transposed else acc[...]).astype(BF16)
                copies.append(pltpu.make_async_copy(stage.at[k % 2], out.at[rows, :], sem.at[k % 2]))
                copies[k].start()
            copies[-2].wait()
            copies[-1].wait()

    row = lambda c, i: (i, 0)
    col = lambda c, i: (i, c)
    return pl.pallas_call(
        body, name="ffn_weight_grads", grid=(FF_CHUNKS, nt),
        in_specs=[pl.BlockSpec((tm, d), row), pl.BlockSpec((tm, d), row), pl.BlockSpec((tm, fc), col), pl.BlockSpec((tm, fc), col),
                  pl.BlockSpec((tm, fc), col)],
        out_specs=[ANY, ANY, ANY],
        out_shape=[jax.ShapeDtypeStruct((ff, d), BF16)] * 3,
        scratch_shapes=[pltpu.VMEM((d, fc), F32), pltpu.VMEM((d, fc), F32), pltpu.VMEM((fc, d), F32), pltpu.VMEM((2, fc, d), BF16),
                        pltpu.SemaphoreType.DMA((2,))],
        compiler_params=_params("arbitrary", "arbitrary"),
    )(f, dd, dgate, dup, act)


def _adamw(w, g, m, v):
    m = ADAM_B1 * m + (1.0 - ADAM_B1) * g
    v = ADAM_B2 * v + (1.0 - ADAM_B2) * (g * g)
    m_hat = m / (1.0 - ADAM_B1 ** ADAM_STEP)
    v_hat = v / (1.0 - ADAM_B2 ** ADAM_STEP)
    return -ADAM_LR * (m_hat / (jnp.sqrt(v_hat) + ADAM_EPS) + ADAM_WD * w), m, v


def _sum_slabs(ref):
    total = ref[0].astype(F32)
    for i in range(1, N_DEV):
        total = total + ref[i].astype(F32)
    return total


def _adamw_rows(r, c):
    tr = r
    for cand in range(8, r, 8):
        if r % cand == 0 and cand * c <= ADAMW_BLOCK_ELEMS:
            tr = cand
    return r if r * c <= ADAMW_BLOCK_ELEMS else tr


def _reduce_adamw_carrying(parts, ws, ms, vs, to_exchange, whole):
    k, nx = len(ws), len(to_exchange)
    r, c = ws[0].shape
    tr = _adamw_rows(r, c)
    steps = r // tr

    def body(*refs):
        p_refs, w_refs, m_refs, v_refs = (refs[a * k:(a + 1) * k] for a in range(4))
        sent, outs = refs[4 * k:4 * k + nx], refs[4 * k + nx:8 * k + nx]
        landed, sems = refs[8 * k + nx:8 * k + 2 * nx], refs[8 * k + 2 * nx:]
        start, finish = _exchange_ops(sent, landed, whole, *sems)
        pl.when(pl.program_id(0) == 0)(start)
        for a in range(k):
            g = _sum_slabs(p_refs[a])
            outs[4 * a][...] = g
            outs[4 * a + 1][...], outs[4 * a + 2][...], outs[4 * a + 3][...] = _adamw(w_refs[a][...], g, m_refs[a][...], v_refs[a][...])
        pl.when(pl.program_id(0) == steps - 1)(finish)

    blk = pl.BlockSpec((tr, c), lambda i: (i, 0))
    out = pl.pallas_call(
        body, name="adamw_ffn_exchange_rest", grid=(steps,),
        in_specs=[pl.BlockSpec((N_DEV, tr, c), lambda i: (0, i, 0))] * k + [blk] * (3 * k) + [ANY] * nx,
        out_specs=[blk] * (4 * k) + [ANY] * nx,
        out_shape=[jax.ShapeDtypeStruct((r, c), F32)] * (4 * k)
        + [jax.ShapeDtypeStruct((N_DEV, *a.shape) if w else a.shape, a.dtype) for a, w in zip(to_exchange, whole)],
        scratch_shapes=_exchange_sems(nx),
        compiler_params=_params("arbitrary"),
    )(*parts, *ws, *ms, *vs, *to_exchange)
    return [tuple(out[4 * a:4 * a + 4]) for a in range(k)], out[4 * k:]


def _reduce_adamw(parts, w, m, v, name):
    r, c = w.shape
    tr = _adamw_rows(r, c)

    def body(p_ref, w_ref, m_ref, v_ref, g_out, d_out, m_out, v_out):
        g = _sum_slabs(p_ref)
        g_out[...] = g
        d_out[...], m_out[...], v_out[...] = _adamw(w_ref[...], g, m_ref[...], v_ref[...])

    blk = pl.BlockSpec((tr, c), lambda i: (i, 0))
    return pl.pallas_call(
        body, name=name, grid=(r // tr,),
        in_specs=[pl.BlockSpec((N_DEV, tr, c), lambda i: (0, i, 0)), blk, blk, blk],
        out_specs=[blk] * 4, out_shape=[jax.ShapeDtypeStruct((r, c), F32)] * 4,
        compiler_params=_params("arbitrary"),
    )(parts, w, m, v)


def _reduce_adamw_small(parts, ws, ms, vs, loss_parts):
    n = len(parts)

    def body(*refs):
        p_refs, w_refs, m_refs, v_refs = (refs[k * n:(k + 1) * n] for k in range(4))
        outs = refs[4 * n + 1:]
        outs[4 * n][...] = _sum_slabs(refs[4 * n])
        for a in range(n):
            g = _sum_slabs(p_refs[a])
            outs[4 * a][...] = g
            outs[4 * a + 1][...], outs[4 * a + 2][...], outs[4 * a + 3][...] = _adamw(w_refs[a][...], g, m_refs[a][...], v_refs[a][...])

    out = pl.pallas_call(
        body, name="adamw_replicated",
        out_shape=[jax.ShapeDtypeStruct(w.shape, F32) for w in ws for _ in range(4)] + [jax.ShapeDtypeStruct(loss_parts.shape[1:], F32)],
        compiler_params=pltpu.CompilerParams(vmem_limit_bytes=VMEM_LIMIT_BYTES),
    )(*parts, *ws, *ms, *vs, loss_parts)
    return [tuple(out[4 * a:4 * a + 4]) for a in range(n)], out[4 * n]


def kernel(x, meta_tokens, norm_mix_pre, w_in, conv_w, pool_w, pool_scale, w_out, norm_mix_post, norm_ffn_pre, w_gate, w_up, w_down, norm_ffn_post, loss_target, m_meta_tokens, m_norm_mix_pre, m_w_in, m_conv_w, m_pool_w, m_pool_scale, m_w_out, m_norm_mix_post, m_norm_ffn_pre, m_w_gate, m_w_up, m_w_down, m_norm_ffn_post, v_meta_tokens, v_norm_mix_pre, v_w_in, v_conv_w, v_pool_w, v_pool_scale, v_w_out, v_norm_mix_post, v_norm_ffn_pre, v_w_gate, v_w_up, v_w_down, v_norm_ffn_post):
    n_seq, seq, d = x.shape
    x2d = x.reshape(n_seq * seq, d)
    target = loss_target.reshape(n_seq * seq, d)

    t_ = lambda a: jnp.swapaxes(a[0], 0, 1)
    (win_s, wout_s, meta_s, conv_s), ffn_shards = _gather_first_weights(
        [w_in[0], w_out[0], meta_tokens, conv_w[0]], [BF16, BF16, F32, F32], [t_(w_gate), t_(w_up), w_down[0]])
    (win_b,) = _columns_from_slabs([win_s])
    wout_b = wout_s.reshape(d, d)
    meta = jnp.transpose(meta_s, (1, 0, 2)).reshape(N_META, d)
    conv = jnp.transpose(conv_s, (1, 0, 2)).reshape(CONV_WIDTH, -1)
    pw, ps = pool_w[0], pool_scale

    a_meta, z_meta = _meta_forward(meta, norm_mix_pre, win_b)
    (h1, z, m), ffn_slabs = _mixer_forward(x2d, z_meta, norm_mix_pre, win_b, conv, pw, ps, wout_b, norm_mix_post, n_seq, ffn_shards)
    wg_b, wu_b, wd_b = (s.reshape(-1, d) for s in ffn_slabs)
    f, act, dd, dgate, dup, dh1, loss_sum, dg3, dg4 = _ffn_forward_backward(h1, target, norm_ffn_pre, wg_b, wu_b, wd_b, norm_ffn_post)
    ffn_grads = _ffn_weight_grads(f, dd, dgate, dup, act)
    (gx, dwin, dwout, dg1, dg2, dconv, dpw, dps, dmeta), ffn_parts = _mixer_backward(
        x2d, dh1, m, z, meta, a_meta, z_meta, norm_mix_pre, win_b, conv, pw, ps, wout_b, norm_mix_post, n_seq,
        [g.reshape(N_DEV, -1, d) for g in ffn_grads])

    dmeta_s = jnp.transpose(dmeta.reshape(N_META, N_DEV, -1), (1, 0, 2))
    dconv_s = jnp.transpose(dconv.reshape(CONV_WIDTH, N_DEV, -1), (1, 0, 2))
    ffn_res, last = _reduce_adamw_carrying(
        ffn_parts, [t_(w_gate), t_(w_up), w_down[0]], [t_(m_w_gate), t_(m_w_up), m_w_down[0]], [t_(v_w_gate), t_(v_w_up), v_w_down[0]],
        [dwin, dwout.reshape(N_DEV, -1, d), dmeta_s, dconv_s, dg1, dg2, dg3, dg4, dpw, dps, loss_sum], [False] * 4 + [True] * 7)
    replicated = last[4:10]

    names = ["meta_tokens", "norm_mix_pre", "w_in", "conv_w", "pool_w", "pool_scale", "w_out", "norm_mix_post", "norm_ffn_pre", "w_gate",
             "w_up", "w_down", "norm_ffn_post"]
    res = {"w_gate": tuple(jnp.swapaxes(o, 0, 1)[None] for o in ffn_res[0]),
           "w_up": tuple(jnp.swapaxes(o, 0, 1)[None] for o in ffn_res[1]), "w_down": tuple(o[None] for o in ffn_res[2])}
    for nm, parts, w, m_, v_ in (("w_in", last[0], w_in, m_w_in, v_w_in), ("w_out", last[1], w_out, m_w_out, v_w_out),
                                 ("conv_w", last[3], conv_w, m_conv_w, v_conv_w)):
        res[nm] = tuple(o[None] for o in _reduce_adamw(parts, w[0], m_[0], v_[0], "adamw_" + nm))
    res["meta_tokens"] = tuple(_reduce_adamw(last[2], meta_tokens, m_meta_tokens, v_meta_tokens, "adamw_meta_tokens"))
    small, loss = _reduce_adamw_small(
        replicated, [norm_mix_pre, norm_mix_post, norm_ffn_pre, norm_ffn_post, pool_w[0], pool_scale],
        [m_norm_mix_pre, m_norm_mix_post, m_norm_ffn_pre, m_norm_ffn_post, m_pool_w[0], m_pool_scale],
        [v_norm_mix_pre, v_norm_mix_post, v_norm_ffn_pre, v_norm_ffn_post, v_pool_w[0], v_pool_scale], last[10])
    for nm, r in zip(["norm_mix_pre", "norm_mix_post", "norm_ffn_pre", "norm_ffn_post", "pool_w", "pool_scale"], small):
        res[nm] = tuple(o[None] for o in r) if nm == "pool_w" else r

    return (loss[0, 0], gx.reshape(n_seq, seq, d), *[res[nm][0] for nm in names], *[res[nm][1] for nm in names],
            *[res[nm][2] for nm in names], *[res[nm][3] for nm in names])
```

```python
import functools

import jax
import jax.numpy as jnp
from jax import lax
from jax.experimental import pallas as pl
from jax.experimental.pallas import tpu as pltpu

F32, BF16 = jnp.float32, jnp.bfloat16
RMS_EPS = 1e-6
N_META = 16
CONV_WIDTH = 3
POOL_WINDOWS = (2, 4, 8, 16)
POOL_GROUP = 128
HALO = 16
N_DEV = 8
MESH_AXES = ("x", "y", "c")
MESH = pl.DeviceIdType.MESH
VMEM_LIMIT_BYTES = 56 * 1024 * 1024
ADAMW_BLOCK_ELEMS = 64 * 1024
TM_MIX = 512
TM_FFN = 256
FFN_CHUNK = 512
FFN_BACKWARD_LAG = 2
TM_WGRAD = 512
FF_CHUNKS = 2

ADAM_LR, ADAM_B1, ADAM_B2, ADAM_EPS, ADAM_WD, ADAM_STEP = 0.001, 0.9, 0.999, 1e-08, 0.01, 10


def _dot(a, b):
    return jnp.dot(a, b, preferred_element_type=F32)


def _dot_nt(a, b):
    return lax.dot_general(a, b, (((1,), (1,)), ((), ())), preferred_element_type=F32)


def _dot_tn(a, b):
    return lax.dot_general(a, b, (((0,), (0,)), ((), ())), preferred_element_type=F32)


def _rms_stats(h):
    rstd = lax.rsqrt(jnp.mean(h * h, axis=-1, keepdims=True) + RMS_EPS)
    return h * rstd, rstd


def _rms_bwd(hat, rstd, g, dy):
    gdy = dy * g
    proj = jnp.mean(gdy * hat, axis=-1, keepdims=True)
    return rstd * (gdy - hat * proj), jnp.sum(dy * hat, axis=0, keepdims=True)


def _params(*semantics):
    return pltpu.CompilerParams(dimension_semantics=semantics or None, vmem_limit_bytes=VMEM_LIMIT_BYTES)


def _resident(shape):
    zeros = (0,) * len(shape)
    return pl.BlockSpec(shape, lambda *_: zeros, pipeline_mode=pl.Buffered(1))


def _const(shape):
    zeros = (0,) * len(shape)
    return pl.BlockSpec(shape, lambda *_: zeros)


ANY = pl.BlockSpec(memory_space=pl.ANY)


def _my_place():
    x, y, c = (lax.axis_index(a) for a in MESH_AXES)
    return x, y, c


def _exchange_sems(n):
    return [pltpu.SemaphoreType.DMA((n, N_DEV - 1)), pltpu.SemaphoreType.DMA((n, N_DEV - 1)), pltpu.SemaphoreType.DMA((n,))]


def _gather_ops(srcs, outs, send_sems, recv_sems, local_sems):
    n = len(srcs)
    x, y, c = _my_place()
    me, sibling = (x, y, c), (x, y, 1 - c)
    chips = [(1 - x, y), (x, 1 - y), (1 - x, 1 - y)]

    def slab(px, py, pc):
        return 4 * px + 2 * py + pc

    def copy(a, k, block, to, src=None):
        dst = outs[a].at[slab(*block)]
        return pltpu.make_async_remote_copy(
            src_ref=dst if src is None else src, dst_ref=dst, send_sem=send_sems.at[a, k], recv_sem=recv_sems.at[a, k],
            device_id=to, device_id_type=MESH)

    def mine(a):
        return pltpu.make_async_copy(srcs[a], outs[a].at[slab(*me)], local_sems.at[a])

    def first(a):
        return [copy(a, 0, me, sibling, src=srcs[a])] + [copy(a, 1 + j, me, (*chip, c), src=srcs[a]) for j, chip in enumerate(chips)]

    def passed(a, j):
        return copy(a, 4 + j, (*chips[j], c), sibling)

    def start():
        for a in range(n):
            mine(a).start()
            for cp in first(a):
                cp.start()

    def forward():
        for j, chip in enumerate(chips):
            for a in range(n):
                copy(a, 1 + j, (*chip, c), me).wait_recv()
                passed(a, j).start()

    def finish():
        for a in range(n):
            copy(a, 0, sibling, me).wait_recv()
            for j, chip in enumerate(chips):
                copy(a, 4 + j, (*chip, 1 - c), me).wait_recv()
        for a in range(n):
            for cp in first(a) + [passed(a, j) for j in range(len(chips))]:
                cp.wait_send()
            mine(a).wait()

    return start, forward, finish


def _exchange_ops(ins, outs, whole, send_sems, recv_sems, local_sems):
    n = len(ins)
    x, y, c = _my_place()
    me = 4 * x + 2 * y + c

    def src(a, i):
        return ins[a] if whole[a] else ins[a].at[i]

    def mine(a):
        return pltpu.make_async_copy(src(a, me), outs[a].at[me], local_sems.at[a])

    def send(a, k):
        to = (me + k) % N_DEV
        return pltpu.make_async_remote_copy(
            src_ref=src(a, to), dst_ref=outs[a].at[me], send_sem=send_sems.at[a, k - 1], recv_sem=recv_sems.at[a, k - 1],
            device_id=(to // 4, (to // 2) % 2, to % 2), device_id_type=MESH)

    def landed(a, k):
        frm = (me + N_DEV - k) % N_DEV
        return pltpu.make_async_remote_copy(
            src_ref=src(a, frm), dst_ref=outs[a].at[frm], send_sem=send_sems.at[a, k - 1], recv_sem=recv_sems.at[a, k - 1],
            device_id=(x, y, c), device_id_type=MESH)

    def start():
        for a in range(n):
            mine(a).start()
            for k in range(1, N_DEV):
                send(a, k).start()

    def finish():
        for a in range(n):
            for k in range(1, N_DEV):
                landed(a, k).wait_recv()
        for a in range(n):
            for k in range(1, N_DEV):
                send(a, k).wait_send()
            mine(a).wait()

    return start, finish


def _gather_first_weights(gathered, dtypes, cast_only):
    n, k = len(gathered), len(cast_only)

    def body(*refs):
        ins, casts_in = refs[:n], refs[n:n + k]
        outs, casts_out = refs[n + k:2 * n + k], refs[2 * n + k:2 * n + 2 * k]
        stages = refs[2 * n + 2 * k:3 * n + 2 * k]
        start, forward, finish = _gather_ops(stages, outs, *refs[3 * n + 2 * k:])
        for a in range(n):
            stages[a][...] = ins[a][...].astype(stages[a].dtype)
        start()
        for a in range(k):
            casts_out[a][...] = casts_in[a][...].astype(BF16)
        forward()
        finish()

    vmem = pl.BlockSpec(memory_space=pltpu.VMEM)
    out = pl.pallas_call(
        body, name="gather_first_weights",
        out_shape=[jax.ShapeDtypeStruct((N_DEV, *s.shape), d) for s, d in zip(gathered, dtypes)]
        + [jax.ShapeDtypeStruct(s.shape, BF16) for s in cast_only],
        in_specs=[vmem] * (n + k), out_specs=[ANY] * n + [vmem] * k,
        scratch_shapes=[pltpu.VMEM(s.shape, d) for s, d in zip(gathered, dtypes)] + _exchange_sems(n),
        compiler_params=pltpu.CompilerParams(vmem_limit_bytes=VMEM_LIMIT_BYTES),
    )(*gathered, *cast_only)
    return out[:n], out[n:]


def _exchange(arrays, whole, name):
    n = len(arrays)

    def body(*refs):
        start, finish = _exchange_ops(refs[:n], refs[n:2 * n], whole, *refs[2 * n:])
        start()
        finish()

    return pl.pallas_call(
        body, name=name,
        out_shape=[jax.ShapeDtypeStruct((N_DEV, *a.shape) if w else a.shape, a.dtype) for a, w in zip(arrays, whole)],
        in_specs=[ANY] * n, out_specs=[ANY] * n, scratch_shapes=_exchange_sems(n),
    )(*arrays)


def _columns_from_slabs(slabs):
    def body(*refs):
        k = len(refs) // 2
        for src, dst in zip(refs[:k], refs[k:]):
            n = src.shape[2]
            for i in range(N_DEV):
                dst[:, pl.ds(n * i, n)] = src[i]

    return pl.pallas_call(
        body, name="columns_from_slabs",
        out_shape=[jax.ShapeDtypeStruct((s.shape[1], N_DEV * s.shape[2]), s.dtype) for s in slabs],
        compiler_params=pltpu.CompilerParams(vmem_limit_bytes=VMEM_LIMIT_BYTES),
    )(*slabs)


def _mixer_core(z, ext_u, ext_p, conv_ref, pool_w_ref, tm):
    c_w = z.shape[1] // 4
    b, c, v, p = z[:, :c_w], z[:, c_w:2 * c_w], z[:, 2 * c_w:3 * c_w], z[:, 3 * c_w:]
    u = c * v
    ext_u[pl.ds(HALO, tm), :] = u
    ext_p[pl.ds(HALO, tm), :] = p
    u1 = ext_u[pl.ds(HALO - 1, tm), :]
    u2 = ext_u[pl.ds(HALO - 2, tm), :]
    yc = conv_ref[pl.ds(2, 1), :] * u + conv_ref[pl.ds(1, 1), :] * u1 + conv_ref[pl.ds(0, 1), :] * u2
    pooled, mixed = [], []
    for g, win in enumerate(POOL_WINDOWS):
        lanes = pl.ds(POOL_GROUP * g, POOL_GROUP)
        pg = p[:, POOL_GROUP * g:POOL_GROUP * (g + 1)]
        s = pg
        for k in range(1, win):
            s = s + ext_p[pl.ds(HALO - k, tm), lanes]
        pooled.append((s * (1.0 / win) - pg).astype(BF16))
        mixed.append(_dot(pooled[-1], pool_w_ref[g].astype(BF16)))
    return b, c, v, u, u1, u2, yc, pooled, mixed


def _meta_forward(meta, g1, w_in):
    def body(meta_ref, g1_ref, w_ref, a_ref, z_ref):
        hat, _ = _rms_stats(meta_ref[...])
        a = (hat * g1_ref[...]).astype(BF16)
        a_ref[...] = a
        z_ref[...] = _dot(a, w_ref[...])

    return pl.pallas_call(
        body, name="meta_forward",
        out_shape=[jax.ShapeDtypeStruct(meta.shape, BF16), jax.ShapeDtypeStruct((N_META, w_in.shape[1]), F32)],
        compiler_params=pltpu.CompilerParams(vmem_limit_bytes=VMEM_LIMIT_BYTES),
    )(meta, g1, w_in)


def _mixer_forward(x2d, z_meta, g1, w_in, conv_w, pool_w, pool_scale, w_out, g2, n_seq, to_gather):
    t, d = x2d.shape
    zw = w_in.shape[1]
    cw = zw // 4
    s = t // n_seq
    tm = min(TM_MIX, s)
    nj = s // tm
    ng = len(to_gather)

    def body(x_ref, zm_ref, g1_ref, win_ref, conv_ref, pw_ref, ps_ref, wout_ref, g2_ref, *rest):
        shards, (h1_ref, z_ref, m_ref), slabs = rest[:ng], rest[ng:ng + 3], rest[ng + 3:2 * ng + 3]
        ext_u, ext_p = rest[2 * ng + 3:2 * ng + 5]
        start, forward, finish = _gather_ops(shards, slabs, *rest[2 * ng + 5:])
        pl.when((pl.program_id(0) == 0) & (pl.program_id(1) == 0))(start)

        @pl.when(pl.program_id(1) == 0)
        def _():
            zm = zm_ref[...]
            ext_u[pl.ds(0, HALO), :] = zm[:, cw:2 * cw] * zm[:, 2 * cw:3 * cw]
            ext_p[pl.ds(0, HALO), :] = zm[:, 3 * cw:]

        h0 = x_ref[...]
        hat, _ = _rms_stats(h0)
        z = _dot((hat * g1_ref[...]).astype(BF16), win_ref[...])
        z_ref[...] = z.astype(BF16)
        b, _, _, _, _, _, yc, _, mixed = _mixer_core(z, ext_u, ext_p, conv_ref, pw_ref, tm)
        ps = ps_ref[...]
        y = [b * yc] + [mixed[g] * ps[:, POOL_GROUP * g:POOL_GROUP * (g + 1)] for g in range(len(POOL_WINDOWS))]
        m = _dot(jnp.concatenate(y, axis=1).astype(BF16), wout_ref[...])
        m_ref[...] = m
        m_hat, _ = _rms_stats(m)
        h1_ref[...] = h0 + m_hat * g2_ref[...]
        ext_u[pl.ds(0, HALO), :] = ext_u[pl.ds(tm, HALO), :]
        ext_p[pl.ds(0, HALO), :] = ext_p[pl.ds(tm, HALO), :]

        @pl.when((pl.program_id(0) == n_seq - 1) & (pl.program_id(1) == nj - 1))
        def _():
            forward()
            finish()

    row = lambda b, j: (b * nj + j, 0)
    out = pl.pallas_call(
        body, name="mixer_forward", grid=(n_seq, nj),
        in_specs=[pl.BlockSpec((tm, d), row), _const(z_meta.shape), _const(g1.shape), _resident(w_in.shape), _const(conv_w.shape),
                  _const(pool_w.shape), _const(pool_scale.shape), _resident(w_out.shape), _const(g2.shape)] + [ANY] * ng,
        out_specs=[pl.BlockSpec((tm, d), row), pl.BlockSpec((tm, zw), row), pl.BlockSpec((tm, d), row)] + [ANY] * ng,
        out_shape=[jax.ShapeDtypeStruct((t, d), F32), jax.ShapeDtypeStruct((t, zw), BF16), jax.ShapeDtypeStruct((t, d), F32)]
        + [jax.ShapeDtypeStruct((N_DEV, *a.shape), a.dtype) for a in to_gather],
        scratch_shapes=[pltpu.VMEM((tm + HALO, cw), F32), pltpu.VMEM((tm + HALO, cw), F32)] + _exchange_sems(ng),
        compiler_params=_params("arbitrary", "arbitrary"),
    )(x2d, z_meta, g1, w_in, conv_w, pool_w, pool_scale, w_out, g2, *to_gather)
    return out[:3], out[3:]


def _mixer_backward(x2d, dh1, m, z, meta, a_meta, z_meta, g1, w_in, conv_w, pool_w, pool_scale, w_out, g2, n_seq, to_exchange):
    t, d = x2d.shape
    zw = w_in.shape[1]
    cw = zw // 4
    s = t // n_seq
    tm = min(TM_MIX, s)
    nj = s // tm
    n_groups = len(POOL_WINDOWS)
    zs = zw // N_DEV
    nx = len(to_exchange)

    def body(x_ref, dh1_ref, m_ref, z_ref, zprev_ref, meta_ref, am_ref, zm_ref, g1_ref, win_ref, conv_ref, pw_ref, ps_ref, wout_ref,
             g2_ref, *rest):
        sent, rest = rest[:nx], rest[nx:]
        gx_ref, dwin_ref, dwout_ref, dg1_ref, dg2_ref, dconv_ref, dpw_ref, dps_ref, dmeta_ref = rest[:9]
        landed, rest = rest[9:9 + nx], rest[9 + nx:]
        ext_u, ext_p, ext_dyc, ext_dq, acc_win, acc_wout, stage16, sem = rest[:8]
        start, finish = _exchange_ops(sent, landed, [False] * nx, *rest[8:])
        b_id, j = pl.program_id(0), pl.program_id(1)
        jr = nj - 1 - j
        pl.when((b_id == 0) & (j == 0))(start)

        @pl.when((b_id == 0) & (j == 0))
        def _():
            acc_win[...] = jnp.zeros_like(acc_win)
            acc_wout[...] = jnp.zeros_like(acc_wout)
            for r in (dg1_ref, dg2_ref, dconv_ref, dpw_ref, dps_ref, dmeta_ref):
                r[...] = jnp.zeros_like(r)

        @pl.when(j == 0)
        def _():
            ext_dyc[pl.ds(tm, HALO), :] = jnp.zeros((HALO, cw), F32)
            ext_dq[pl.ds(tm, HALO), :] = jnp.zeros((HALO, cw), F32)

        zm = zm_ref[...]
        halo = jnp.where(jr == 0, zm, zprev_ref[...].astype(F32))
        ext_u[pl.ds(0, HALO), :] = halo[:, cw:2 * cw] * halo[:, 2 * cw:3 * cw]
        ext_p[pl.ds(0, HALO), :] = halo[:, 3 * cw:]

        h0 = x_ref[...]
        hat0, rstd0 = _rms_stats(h0)
        g1 = g1_ref[...]
        a = (hat0 * g1).astype(BF16)
        b, c, v, u, u1, u2, yc, pooled, mixed = _mixer_core(z_ref[...].astype(F32), ext_u, ext_p, conv_ref, pw_ref, tm)
        ps = ps_ref[...]
        y = [b * yc] + [mixed[g] * ps[:, POOL_GROUP * g:POOL_GROUP * (g + 1)] for g in range(n_groups)]
        ycat = jnp.concatenate(y, axis=1).astype(BF16)

        dh1v = dh1_ref[...]
        m_hat, m_rstd = _rms_stats(m_ref[...])
        dm, dg2 = _rms_bwd(m_hat, m_rstd, g2_ref[...], dh1v)
        dg2_ref[...] += dg2
        dm = dm.astype(BF16)
        acc_wout[...] += _dot_tn(ycat, dm)
        dycat = _dot_nt(dm, wout_ref[...])

        dyconv = dycat[:, :cw]
        db = dyconv * yc
        dyc = dyconv * b
        ext_dyc[pl.ds(0, tm), :] = dyc
        du = (conv_ref[pl.ds(2, 1), :] * dyc + conv_ref[pl.ds(1, 1), :] * ext_dyc[pl.ds(1, tm), :]
              + conv_ref[pl.ds(0, 1), :] * ext_dyc[pl.ds(2, tm), :])
        dconv_ref[pl.ds(2, 1), :] += jnp.sum(dyc * u, axis=0, keepdims=True)
        dconv_ref[pl.ds(1, 1), :] += jnp.sum(dyc * u1, axis=0, keepdims=True)
        dconv_ref[pl.ds(0, 1), :] += jnp.sum(dyc * u2, axis=0, keepdims=True)

        dp = []
        for g, win in enumerate(POOL_WINDOWS):
            lanes = pl.ds(POOL_GROUP * g, POOL_GROUP)
            dypool = dycat[:, cw + POOL_GROUP * g:cw + POOL_GROUP * (g + 1)]
            dps_ref[:, lanes] += jnp.sum(dypool * mixed[g], axis=0, keepdims=True)
            dmixed = (dypool * ps[:, POOL_GROUP * g:POOL_GROUP * (g + 1)]).astype(BF16)
            dpw_ref[g] += _dot_tn(pooled[g], dmixed)
            dq = _dot_nt(dmixed, pw_ref[g].astype(BF16))
            ext_dq[pl.ds(0, tm), lanes] = dq
            acc = dq
            for k in range(1, win):
                acc = acc + ext_dq[pl.ds(k, tm), lanes]
            dp.append(acc * (1.0 / win) - dq)

        dz = jnp.concatenate([db, du * v, du * c] + dp, axis=1).astype(BF16)
        acc_win[...] += _dot_tn(a, dz)
        dh0, dg1 = _rms_bwd(hat0, rstd0, g1, _dot_nt(dz, win_ref[...]))
        dg1_ref[...] += dg1
        gx_ref[...] = dh1v + dh0

        ext_dyc[pl.ds(tm, HALO), :] = ext_dyc[pl.ds(0, HALO), :]
        ext_dq[pl.ds(tm, HALO), :] = ext_dq[pl.ds(0, HALO), :]

        @pl.when(jr == 0)
        def _():
            ext_dyc[pl.ds(tm - HALO, HALO), :] = jnp.zeros((HALO, cw), F32)
            ext_dq[pl.ds(tm - HALO, HALO), :] = jnp.zeros((HALO, cw), F32)
            du_m = (conv_ref[pl.ds(1, 1), :] * ext_dyc[pl.ds(tm - HALO + 1, HALO), :]
                    + conv_ref[pl.ds(0, 1), :] * ext_dyc[pl.ds(tm - HALO + 2, HALO), :])
            dp_m = []
            for g, win in enumerate(POOL_WINDOWS):
                lanes = pl.ds(POOL_GROUP * g, POOL_GROUP)
                acc = ext_dq[pl.ds(tm - HALO + 1, HALO), lanes]
                for k in range(2, win):
                    acc = acc + ext_dq[pl.ds(tm - HALO + k, HALO), lanes]
                dp_m.append(acc * (1.0 / win))
            dz_m = jnp.concatenate([jnp.zeros((HALO, cw), F32), du_m * zm[:, 2 * cw:3 * cw], du_m * zm[:, cw:2 * cw]] + dp_m,
                                   axis=1).astype(BF16)
            acc_win[...] += _dot_tn(am_ref[...], dz_m)
            hat_m, rstd_m = _rms_stats(meta_ref[...])
            dmeta, dg1_m = _rms_bwd(hat_m, rstd_m, g1, _dot_nt(dz_m, win_ref[...]))
            dg1_ref[...] += dg1_m
            dmeta_ref[...] += dmeta

        @pl.when((b_id == n_seq - 1) & (j == nj - 1))
        def _():
            pieces = [(acc_win, zs * i, dwin_ref.at[i]) for i in range(N_DEV)]
            pieces += [(acc_wout, zs * i, dwout_ref.at[:, pl.ds(zs * i, zs)]) for i in range(d // zs)]
            copies = []
            for k, (acc, col, dst) in enumerate(pieces):
                if k >= 2:
                    copies[k - 2].wait()
                stage16[k % 2] = acc[:, pl.ds(col, zs)].astype(BF16)
                copies.append(pltpu.make_async_copy(stage16.at[k % 2], dst, sem.at[k % 2]))
                copies[k].start()
            copies[-2].wait()
            copies[-1].wait()
            finish()

    row = lambda b, j: (b * nj + nj - 1 - j, 0)
    prev = lambda b, j: (jnp.maximum((b * s + (nj - 1 - j) * tm) // HALO - 1, 0), 0)
    small = [g1.shape, g2.shape, conv_w.shape, pool_w.shape, pool_scale.shape, meta.shape]
    out = pl.pallas_call(
        body, name="mixer_backward", grid=(n_seq, nj),
        in_specs=[pl.BlockSpec((tm, d), row), pl.BlockSpec((tm, d), row), pl.BlockSpec((tm, d), row), pl.BlockSpec((tm, zw), row),
                  pl.BlockSpec((HALO, zw), prev), _const(meta.shape), _const(a_meta.shape), _const(z_meta.shape), _const(g1.shape),
                  _resident(w_in.shape), _const(conv_w.shape), _const(pool_w.shape), _const(pool_scale.shape), _resident(w_out.shape),
                  _const(g2.shape)] + [ANY] * nx,
        out_specs=[pl.BlockSpec((tm, d), row), ANY, ANY] + [_const(sh) for sh in small] + [ANY] * nx,
        out_shape=[jax.ShapeDtypeStruct((t, d), F32), jax.ShapeDtypeStruct((N_DEV, d, zs), BF16),
                   jax.ShapeDtypeStruct(w_out.shape, BF16)] + [jax.ShapeDtypeStruct(sh, F32) for sh in small]
        + [jax.ShapeDtypeStruct(a.shape, a.dtype) for a in to_exchange],
        scratch_shapes=[pltpu.VMEM((tm + HALO, cw), F32)] * 4
        + [pltpu.VMEM(w_in.shape, F32), pltpu.VMEM(w_out.shape, F32), pltpu.VMEM((2, d, zs), BF16),
           pltpu.SemaphoreType.DMA((2,))] + _exchange_sems(nx),
        compiler_params=_params("arbitrary", "arbitrary"),
    )(x2d, dh1, m, z, z, meta, a_meta, z_meta, g1, w_in, conv_w, pool_w, pool_scale, w_out, g2, *to_exchange)
    return out[:9], out[9:]


def _ffn_forward_backward(h1, target, g3, w_gate, w_up, w_down, g4):
    t, d = h1.shape
    ff = w_gate.shape[0]
    tm = min(TM_FFN, t)
    nt = t // tm
    chunks = [(s, min(FFN_CHUNK, ff - s)) for s in range(0, ff, FFN_CHUNK)]

    def body(h1_ref, h1pp_ref, tgt_ref, g3_ref, wg_ref, wu_ref, wd_ref, g4_ref,
             f_ref, act_ref, dd_ref, dgate_ref, dup_ref, dh1_ref, loss_ref, dg3_ref, dg4_ref, *slots):
        gate_s, up_s, dd_s, dh2_s, df_s = slots
        i = pl.program_id(0)

        def forward(slot):
            h1v = h1_ref[...]
            hat, _ = _rms_stats(h1v)
            f = (hat * g3_ref[...]).astype(BF16)
            f_ref[...] = f
            s, n = chunks[0]
            gate, up = _dot_nt(f_ref[...], wg_ref[pl.ds(s, n), :]), _dot_nt(f_ref[...], wu_ref[pl.ds(s, n), :])
            yield
            down = None
            for k, (s, n) in enumerate(chunks):
                gate_s.at[slot][:, pl.ds(s, n)] = gate.astype(BF16)
                up_s.at[slot][:, pl.ds(s, n)] = up.astype(BF16)
                act = (gate * jax.nn.sigmoid(gate) * up).astype(BF16)
                act_ref[:, pl.ds(s, n)] = act
                if k + 1 < len(chunks):
                    s1, n1 = chunks[k + 1]
                    gate, up = _dot_nt(f_ref[...], wg_ref[pl.ds(s1, n1), :]), _dot_nt(f_ref[...], wu_ref[pl.ds(s1, n1), :])
                yield
                part = _dot(act_ref[:, pl.ds(s, n)], wd_ref[pl.ds(s, n), :])
                down = part if down is None else down + part
                yield
            d_hat, d_rstd = _rms_stats(down)
            g4 = g4_ref[...]
            err = h1v + d_hat * g4 - tgt_ref[...]
            loss_ref[...] += jnp.sum(err * err) * (0.5 / d)
            dh2 = err * (1.0 / d)
            dh2_s.at[slot][...] = dh2
            dd, dg4 = _rms_bwd(d_hat, d_rstd, g4, dh2)
            dg4_ref[...] += dg4
            dd = dd.astype(BF16)
            dd_ref[...] = dd
            dd_s.at[slot][...] = dd

        def backward(slot):
            s, n = chunks[0]
            dact = _dot_nt(dd_s.at[slot][...], wd_ref[pl.ds(s, n), :])
            yield
            df = None
            for k, (s, n) in enumerate(chunks):
                gate = gate_s.at[slot][:, pl.ds(s, n)].astype(F32)
                up = up_s.at[slot][:, pl.ds(s, n)].astype(F32)
                sig = jax.nn.sigmoid(gate)
                dup = (dact * (gate * sig)).astype(BF16)
                dgate = (dact * up * (sig * (1.0 + gate * (1.0 - sig)))).astype(BF16)
                dup_ref[:, pl.ds(s, n)] = dup
                dgate_ref[:, pl.ds(s, n)] = dgate
                if k + 1 < len(chunks):
                    s1, n1 = chunks[k + 1]
                    dact = _dot_nt(dd_s.at[slot][...], wd_ref[pl.ds(s1, n1), :])
                yield
                part = _dot(dgate_ref[:, pl.ds(s, n)], wg_ref[pl.ds(s, n), :]) + _dot(dup_ref[:, pl.ds(s, n)], wu_ref[pl.ds(s, n), :])
                df = part if df is None else df + part
                yield
            df_s.at[slot][...] = df

        def last(slot):
            hat, rstd = _rms_stats(h1pp_ref[...])
            dh1, dg3 = _rms_bwd(hat, rstd, g3_ref[...], df_s.at[slot][...])
            dg3_ref[...] += dg3
            dh1_ref[...] = dh2_s.at[slot][...] + dh1

        def emit(parity, with_forward, with_backward, with_last):
            fwd = forward(parity) if with_forward else iter(())
            bwd = backward(1 - parity) if with_backward else iter(())
            next(fwd, None)
            if with_last:
                last(parity)
            for _ in range(FFN_BACKWARD_LAG):
                next(fwd, None)
            alive = True
            while alive:
                alive = next(bwd, True) is None
                alive = (next(fwd, True) is None) or alive

        @pl.when(i == 0)
        def _():
            for r in (loss_ref, dg3_ref, dg4_ref, *slots):
                r[...] = jnp.zeros_like(r)

        @pl.when(i < nt)
        def _():
            emit(i % 2, True, True, True)

        @pl.when(i == nt)
        def _():
            emit(nt % 2, False, True, True)

        @pl.when(i == nt + 1)
        def _():
            emit((nt + 1) % 2, False, False, True)

    cur = lambda i: (jnp.minimum(i, nt - 1), 0)
    prev = lambda i: (jnp.clip(i - 1, 0, nt - 1), 0)
    prev2 = lambda i: (jnp.clip(i - 2, 0, nt - 1), 0)
    return pl.pallas_call(
        body, name="ffn_forward_backward", grid=(nt + 2,),
        in_specs=[pl.BlockSpec((tm, d), cur), pl.BlockSpec((tm, d), prev2), pl.BlockSpec((tm, d), cur), _const(g3.shape),
                  _resident(w_gate.shape), _resident(w_up.shape), _resident(w_down.shape), _const(g4.shape)],
        out_specs=[pl.BlockSpec((tm, d), cur), pl.BlockSpec((tm, ff), cur), pl.BlockSpec((tm, d), cur), pl.BlockSpec((tm, ff), prev),
                   pl.BlockSpec((tm, ff), prev), pl.BlockSpec((tm, d), prev2), _const((8, 128)), _const(g3.shape), _const(g4.shape)],
        out_shape=[jax.ShapeDtypeStruct((t, d), BF16), jax.ShapeDtypeStruct((t, ff), BF16), jax.ShapeDtypeStruct((t, d), BF16),
                   jax.ShapeDtypeStruct((t, ff), BF16), jax.ShapeDtypeStruct((t, ff), BF16), jax.ShapeDtypeStruct((t, d), F32),
                   jax.ShapeDtypeStruct((8, 128), F32), jax.ShapeDtypeStruct(g3.shape, F32), jax.ShapeDtypeStruct(g4.shape, F32)],
        scratch_shapes=[pltpu.VMEM((2, tm, ff), BF16)] * 2 + [pltpu.VMEM((2, tm, d), BF16)] + [pltpu.VMEM((2, tm, d), F32)] * 2,
        compiler_params=_params("arbitrary"),
    )(h1, h1, target, g3, w_gate, w_up, w_down, g4)


def _ffn_weight_grads(f, dd, dgate, dup, act):
    t, d = f.shape
    ff = dgate.shape[1]
    tm = min(TM_WGRAD, t)
    nt = t // tm
    fc = ff // FF_CHUNKS

    def body(f_ref, dd_ref, dgate_ref, dup_ref, act_ref, dwg_ref, dwu_ref, dwd_ref, acc_g, acc_u, acc_d, stage, sem):
        c, i = pl.program_id(0), pl.program_id(1)

        @pl.when(i == 0)
        def _():
            acc_g[...] = jnp.zeros_like(acc_g)
            acc_u[...] = jnp.zeros_like(acc_u)
            acc_d[...] = jnp.zeros_like(acc_d)

        fv = f_ref[...]
        acc_g[...] += _dot_tn(fv, dgate_ref[...])
        acc_u[...] += _dot_tn(fv, dup_ref[...])
        acc_d[...] += _dot_tn(act_ref[...], dd_ref[...])

        @pl.when(i == nt - 1)
        def _():
            rows = pl.ds(pl.multiple_of(c * fc, 16), fc)
            copies = []
            for k, (acc, out, transposed) in enumerate(((acc_d, dwd_ref, False), (acc_g, dwg_ref, True), (acc_u, dwu_ref, True))):
                if k >= 2:
                    copies[k - 2].wait()
                stage[k % 2] = (acc[...].T if transposed else acc[...]).astype(BF16)
                copies.append(pltpu.make_async_copy(stage.at[k % 2], out.at[rows, :], sem.at[k % 2]))
                copies[k].start()
            copies[-2].wait()
            copies[-1].wait()

    row = lambda c, i: (i, 0)
    col = lambda c, i: (i, c)
    return pl.pallas_call(
        body, name="ffn_weight_grads", grid=(FF_CHUNKS, nt),
        in_specs=[pl.BlockSpec((tm, d), row), pl.BlockSpec((tm, d), row), pl.BlockSpec((tm, fc), col), pl.BlockSpec((tm, fc), col),
                  pl.BlockSpec((tm, fc), col)],
        out_specs=[ANY, ANY, ANY],
        out_shape=[jax.ShapeDtypeStruct((ff, d), BF16)] * 3,
        scratch_shapes=[pltpu.VMEM((d, fc), F32), pltpu.VMEM((d, fc), F32), pltpu.VMEM((fc, d), F32), pltpu.VMEM((2, fc, d), BF16),
                        pltpu.SemaphoreType.DMA((2,))],
        compiler_params=_params("arbitrary", "arbitrary"),
    )(f, dd, dgate, dup, act)


def _adamw(w, g, m, v):
    m = ADAM_B1 * m + (1.0 - ADAM_B1) * g
    v = ADAM_B2 * v + (1.0 - ADAM_B2) * (g * g)
    m_hat = m / (1.0 - ADAM_B1 ** ADAM_STEP)
    v_hat = v / (1.0 - ADAM_B2 ** ADAM_STEP)
    return -ADAM_LR * (m_hat / (jnp.sqrt(v_hat) + ADAM_EPS) + ADAM_WD * w), m, v


def _sum_slabs(ref):
    total = ref[0].astype(F32)
    for i in range(1, N_DEV):
        total = total + ref[i].astype(F32)
    return total


def _adamw_rows(r, c):
    tr = r
    for cand in range(8, r, 8):
        if r % cand == 0 and cand * c <= ADAMW_BLOCK_ELEMS:
            tr = cand
    return r if r * c <= ADAMW_BLOCK_ELEMS else tr


def _reduce_adamw_carrying(parts, ws, ms, vs, to_exchange, whole):
    k, nx = len(ws), len(to_exchange)
    r, c = ws[0].shape
    tr = _adamw_rows(r, c)
    steps = r // tr

    def body(*refs):
        p_refs, w_refs, m_refs, v_refs = (refs[a * k:(a + 1) * k] for a in range(4))
        sent, outs = refs[4 * k:4 * k + nx], refs[4 * k + nx:8 * k + nx]
        landed, sems = refs[8 * k + nx:8 * k + 2 * nx], refs[8 * k + 2 * nx:]
        start, finish = _exchange_ops(sent, landed, whole, *sems)
        pl.when(pl.program_id(0) == 0)(start)
        for a in range(k):
            g = _sum_slabs(p_refs[a])
            outs[4 * a][...] = g
            outs[4 * a + 1][...], outs[4 * a + 2][...], outs[4 * a + 3][...] = _adamw(w_refs[a][...], g, m_refs[a][...], v_refs[a][...])
        pl.when(pl.program_id(0) == steps - 1)(finish)

    blk = pl.BlockSpec((tr, c), lambda i: (i, 0))
    out = pl.pallas_call(
        body, name="adamw_ffn_exchange_rest", grid=(steps,),
        in_specs=[pl.BlockSpec((N_DEV, tr, c), lambda i: (0, i, 0))] * k + [blk] * (3 * k) + [ANY] * nx,
        out_specs=[blk] * (4 * k) + [ANY] * nx,
        out_shape=[jax.ShapeDtypeStruct((r, c), F32)] * (4 * k)
        + [jax.ShapeDtypeStruct((N_DEV, *a.shape) if w else a.shape, a.dtype) for a, w in zip(to_exchange, whole)],
        scratch_shapes=_exchange_sems(nx),
        compiler_params=_params("arbitrary"),
    )(*parts, *ws, *ms, *vs, *to_exchange)
    return [tuple(out[4 * a:4 * a + 4]) for a in range(k)], out[4 * k:]


def _reduce_adamw(parts, w, m, v, name):
    r, c = w.shape
    tr = _adamw_rows(r, c)

    def body(p_ref, w_ref, m_ref, v_ref, g_out, d_out, m_out, v_out):
        g = _sum_slabs(p_ref)
        g_out[...] = g
        d_out[...], m_out[...], v_out[...] = _adamw(w_ref[...], g, m_ref[...], v_ref[...])

    blk = pl.BlockSpec((tr, c), lambda i: (i, 0))
    return pl.pallas_call(
        body, name=name, grid=(r // tr,),
        in_specs=[pl.BlockSpec((N_DEV, tr, c), lambda i: (0, i, 0)), blk, blk, blk],
        out_specs=[blk] * 4, out_shape=[jax.ShapeDtypeStruct((r, c), F32)] * 4,
        compiler_params=_params("arbitrary"),
    )(parts, w, m, v)


def _reduce_adamw_small(parts, ws, ms, vs, loss_parts):
    n = len(parts)

    def body(*refs):
        p_refs, w_refs, m_refs, v_refs = (refs[k * n:(k + 1) * n] for k in range(4))
        outs = refs[4 * n + 1:]
        outs[4 * n][...] = _sum_slabs(refs[4 * n])
        for a in range(n):
            g = _sum_slabs(p_refs[a])
            outs[4 * a][...] = g
            outs[4 * a + 1][...], outs[4 * a + 2][...], outs[4 * a + 3][...] = _adamw(w_refs[a][...], g, m_refs[a][...], v_refs[a][...])

    out = pl.pallas_call(
        body, name="adamw_replicated",
        out_shape=[jax.ShapeDtypeStruct(w.shape, F32) for w in ws for _ in range(4)] + [jax.ShapeDtypeStruct(loss_parts.shape[1:], F32)],
        compiler_params=pltpu.CompilerParams(vmem_limit_bytes=VMEM_LIMIT_BYTES),
    )(*parts, *ws, *ms, *vs, loss_parts)
    return [tuple(out[4 * a:4 * a + 4]) for a in range(n)], out[4 * n]


def kernel(x, meta_tokens, norm_mix_pre, w_in, conv_w, pool_w, pool_scale, w_out, norm_mix_post, norm_ffn_pre, w_gate, w_up, w_down, norm_ffn_post, loss_target, m_meta_tokens, m_norm_mix_pre, m_w_in, m_conv_w, m_pool_w, m_pool_scale, m_w_out, m_norm_mix_post, m_norm_ffn_pre, m_w_gate, m_w_up, m_w_down, m_norm_ffn_post, v_meta_tokens, v_norm_mix_pre, v_w_in, v_conv_w, v_pool_w, v_pool_scale, v_w_out, v_norm_mix_post, v_norm_ffn_pre, v_w_gate, v_w_up, v_w_down, v_norm_ffn_post):
    n_seq, seq, d = x.shape
    x2d = x.reshape(n_seq * seq, d)
    target = loss_target.reshape(n_seq * seq, d)

    t_ = lambda a: jnp.swapaxes(a[0], 0, 1)
    (win_s, wout_s, meta_s, conv_s), ffn_shards = _gather_first_weights(
        [w_in[0], w_out[0], meta_tokens, conv_w[0]], [BF16, BF16, F32, F32], [t_(w_gate), t_(w_up), w_down[0]])
    (win_b,) = _columns_from_slabs([win_s])
    wout_b = wout_s.reshape(d, d)
    meta = jnp.transpose(meta_s, (1, 0, 2)).reshape(N_META, d)
    conv = jnp.transpose(conv_s, (1, 0, 2)).reshape(CONV_WIDTH, -1)
    pw, ps = pool_w[0], pool_scale

    a_meta, z_meta = _meta_forward(meta, norm_mix_pre, win_b)
    (h1, z, m), ffn_slabs = _mixer_forward(x2d, z_meta, norm_mix_pre, win_b, conv, pw, ps, wout_b, norm_mix_post, n_seq, ffn_shards)
    wg_b, wu_b, wd_b = (s.reshape(-1, d) for s in ffn_slabs)
    f, act, dd, dgate, dup, dh1, loss_sum, dg3, dg4 = _ffn_forward_backward(h1, target, norm_ffn_pre, wg_b, wu_b, wd_b, norm_ffn_post)
    ffn_grads = _ffn_weight_grads(f, dd, dgate, dup, act)
    (gx, dwin, dwout, dg1, dg2, dconv, dpw, dps, dmeta), ffn_parts = _mixer_backward(
        x2d, dh1, m, z, meta, a_meta, z_meta, norm_mix_pre, win_b, conv, pw, ps, wout_b, norm_mix_post, n_seq,
        [g.reshape(N_DEV, -1, d) for g in ffn_grads])

    dmeta_s = jnp.transpose(dmeta.reshape(N_META, N_DEV, -1), (1, 0, 2))
    dconv_s = jnp.transpose(dconv.reshape(CONV_WIDTH, N_DEV, -1), (1, 0, 2))
    ffn_res, last = _reduce_adamw_carrying(
        ffn_parts, [t_(w_gate), t_(w_up), w_down[0]], [t_(m_w_gate), t_(m_w_up), m_w_down[0]], [t_(v_w_gate), t_(v_w_up), v_w_down[0]],
        [dwin, dwout.reshape(N_DEV, -1, d), dmeta_s, dconv_s, dg1, dg2, dg3, dg4, dpw, dps, loss_sum], [False] * 4 + [True] * 7)
    replicated = last[4:10]

    names = ["meta_tokens", "norm_mix_pre", "w_in", "conv_w", "pool_w", "pool_scale", "w_out", "norm_mix_post", "norm_ffn_pre", "w_gate",
             "w_up", "w_down", "norm_ffn_post"]
    res = {"w_gate": tuple(jnp.swapaxes(o, 0, 1)[None] for o in ffn_res[0]),
           "w_up": tuple(jnp.swapaxes(o, 0, 1)[None] for o in ffn_res[1]), "w_down": tuple(o[None] for o in ffn_res[2])}
    for nm, parts, w, m_, v_ in (("w_in", last[0], w_in, m_w_in, v_w_in), ("w_out", last[1], w_out, m_w_out, v_w_out),
                                 ("conv_w", last[3], conv_w, m_conv_w, v_conv_w)):
        res[nm] = tuple(o[None] for o in _reduce_adamw(parts, w[0], m_[0], v_[0], "adamw_" + nm))
    res["meta_tokens"] = tuple(_reduce_adamw(last[2], meta_tokens, m_meta_tokens, v_meta_tokens, "adamw_meta_tokens"))
    small, loss = _reduce_adamw_small(
        replicated, [norm_mix_pre, norm_mix_post, norm_ffn_pre, norm_ffn_post, pool_w[0], pool_scale],
        [m_norm_mix_pre, m_norm_mix_post, m_norm_ffn_pre, m_norm_ffn_post, m_pool_w[0], m_pool_scale],
        [v_norm_mix_pre, v_norm_mix_post, v_norm_ffn_pre, v_norm_ffn_post, v_pool_w[0], v_pool_scale], last[10])
    for nm, r in zip(["norm_mix_pre", "norm_mix_post", "norm_ffn_pre", "norm_ffn_post", "pool_w", "pool_scale"], small):
        res[nm] = tuple(o[None] for o in r) if nm == "pool_w" else r

    return (loss[0, 0], gx.reshape(n_seq, seq, d), *[res[nm][0] for nm in names], *[res[nm][1] for nm in names],
            *[res[nm][2] for nm in names], *[res[nm][3] for nm in names])
```

```python
import functools

import jax
import jax.numpy as jnp
from jax import lax
from jax.experimental import pallas as pl
from jax.experimental.pallas import tpu as pltpu

F32, BF16 = jnp.float32, jnp.bfloat16
RMS_EPS = 1e-6
N_META = 16
CONV_WIDTH = 3
POOL_WINDOWS = (2, 4, 8, 16)
POOL_GROUP = 128
HALO = 16
N_DEV = 8
MESH_AXES = ("x", "y", "c")
MESH = pl.DeviceIdType.MESH
VMEM_LIMIT_BYTES = 56 * 1024 * 1024
ADAMW_BLOCK_ELEMS = 64 * 1024
TM_MIX = 512
TM_FFN = 256
FFN_CHUNK = 512
FFN_BACKWARD_LAG = 2
TM_WGRAD = 512
FF_CHUNKS = 2

ADAM_LR, ADAM_B1, ADAM_B2, ADAM_EPS, ADAM_WD, ADAM_STEP = 0.001, 0.9, 0.999, 1e-08, 0.01, 10


def _dot(a, b):
    return jnp.dot(a, b, preferred_element_type=F32)


def _dot_nt(a, b):
    return lax.dot_general(a, b, (((1,), (1,)), ((), ())), preferred_element_type=F32)


def _dot_tn(a, b):
    return lax.dot_general(a, b, (((0,), (0,)), ((), ())), preferred_element_type=F32)


def _rms_stats(h):
    rstd = lax.rsqrt(jnp.mean(h * h, axis=-1, keepdims=True) + RMS_EPS)
    return h * rstd, rstd


def _rms_bwd(hat, rstd, g, dy):
    gdy = dy * g
    proj = jnp.mean(gdy * hat, axis=-1, keepdims=True)
    return rstd * (gdy - hat * proj), jnp.sum(dy * hat, axis=0, keepdims=True)


def _params(*semantics):
    return pltpu.CompilerParams(dimension_semantics=semantics or None, vmem_limit_bytes=VMEM_LIMIT_BYTES)


def _resident(shape):
    zeros = (0,) * len(shape)
    return pl.BlockSpec(shape, lambda *_: zeros, pipeline_mode=pl.Buffered(1))


def _const(shape):
    zeros = (0,) * len(shape)
    return pl.BlockSpec(shape, lambda *_: zeros)


ANY = pl.BlockSpec(memory_space=pl.ANY)


def _my_place():
    x, y, c = (lax.axis_index(a) for a in MESH_AXES)
    return x, y, c


def _exchange_sems(n):
    return [pltpu.SemaphoreType.DMA((n, N_DEV - 1)), pltpu.SemaphoreType.DMA((n, N_DEV - 1)), pltpu.SemaphoreType.DMA((n,))]


def _gather_ops(srcs, outs, send_sems, recv_sems, local_sems):
    n = len(srcs)
    x, y, c = _my_place()
    me, sibling = (x, y, c), (x, y, 1 - c)
    chips = [(1 - x, y), (x, 1 - y), (1 - x, 1 - y)]

    def slab(px, py, pc):
        return 4 * px + 2 * py + pc

    def copy(a, k, block, to, src=None):
        dst = outs[a].at[slab(*block)]
        return pltpu.make_async_remote_copy(
            src_ref=dst if src is None else src, dst_ref=dst, send_sem=send_sems.at[a, k], recv_sem=recv_sems.at[a, k],
            device_id=to, device_id_type=MESH)

    def mine(a):
        return pltpu.make_async_copy(srcs[a], outs[a].at[slab(*me)], local_sems.at[a])

    def first(a):
        return [copy(a, 0, me, sibling, src=srcs[a])] + [copy(a, 1 + j, me, (*chip, c), src=srcs[a]) for j, chip in enumerate(chips)]

    def passed(a, j):
        return copy(a, 4 + j, (*chips[j], c), sibling)

    def start():
        for a in range(n):
            mine(a).start()
            for cp in first(a):
                cp.start()

    def forward():
        for j, chip in enumerate(chips):
            for a in range(n):
                copy(a, 1 + j, (*chip, c), me).wait_recv()
                passed(a, j).start()

    def finish():
        for a in range(n):
            copy(a, 0, sibling, me).wait_recv()
            for j, chip in enumerate(chips):
                copy(a, 4 + j, (*chip, 1 - c), me).wait_recv()
        for a in range(n):
            for cp in first(a) + [passed(a, j) for j in range(len(chips))]:
                cp.wait_send()
            mine(a).wait()

    return start, forward, finish


def _exchange_ops(ins, outs, whole, send_sems, recv_sems, local_sems):
    n = len(ins)
    x, y, c = _my_place()
    me = 4 * x + 2 * y + c

    def src(a, i):
        return ins[a] if whole[a] else ins[a].at[i]

    def mine(a):
        return pltpu.make_async_copy(src(a, me), outs[a].at[me], local_sems.at[a])

    def send(a, k):
        to = (me + k) % N_DEV
        return pltpu.make_async_remote_copy(
            src_ref=src(a, to), dst_ref=outs[a].at[me], send_sem=send_sems.at[a, k - 1], recv_sem=recv_sems.at[a, k - 1],
            device_id=(to // 4, (to // 2) % 2, to % 2), device_id_type=MESH)

    def landed(a, k):
        frm = (me + N_DEV - k) % N_DEV
        return pltpu.make_async_remote_copy(
            src_ref=src(a, frm), dst_ref=outs[a].at[frm], send_sem=send_sems.at[a, k - 1], recv_sem=recv_sems.at[a, k - 1],
            device_id=(x, y, c), device_id_type=MESH)

    def start():
        for a in range(n):
            mine(a).start()
            for k in range(1, N_DEV):
                send(a, k).start()

    def finish():
        for a in range(n):
            for k in range(1, N_DEV):
                landed(a, k).wait_recv()
        for a in range(n):
            for k in range(1, N_DEV):
                send(a, k).wait_send()
            mine(a).wait()

    return start, finish


def _gather_first_weights(gathered, dtypes, cast_only):
    n, k = len(gathered), len(cast_only)

    def body(*refs):
        ins, casts_in = refs[:n], refs[n:n + k]
        outs, casts_out = refs[n + k:2 * n + k], refs[2 * n + k:2 * n + 2 * k]
        stages = refs[2 * n + 2 * k:3 * n + 2 * k]
        start, forward, finish = _gather_ops(stages, outs, *refs[3 * n + 2 * k:])
        for a in range(n):
            stages[a][...] = ins[a][...].astype(stages[a].dtype)
        start()
        for a in range(k):
            casts_out[a][...] = casts_in[a][...].astype(BF16)
        forward()
        finish()

    vmem = pl.BlockSpec(memory_space=pltpu.VMEM)
    out = pl.pallas_call(
        body, name="gather_first_weights",
        out_shape=[jax.ShapeDtypeStruct((N_DEV, *s.shape), d) for s, d in zip(gathered, dtypes)]
        + [jax.ShapeDtypeStruct(s.shape, BF16) for s in cast_only],
        in_specs=[vmem] * (n + k), out_specs=[ANY] * n + [vmem] * k,
        scratch_shapes=[pltpu.VMEM(s.shape, d) for s, d in zip(gathered, dtypes)] + _exchange_sems(n),
        compiler_params=pltpu.CompilerParams(vmem_limit_bytes=VMEM_LIMIT_BYTES),
    )(*gathered, *cast_only)
    return out[:n], out[n:]


def _exchange(arrays, whole, name):
    n = len(arrays)

    def body(*refs):
        start, finish = _exchange_ops(refs[:n], refs[n:2 * n], whole, *refs[2 * n:])
        start()
        finish()

    return pl.pallas_call(
        body, name=name,
        out_shape=[jax.ShapeDtypeStruct((N_DEV, *a.shape) if w else a.shape, a.dtype) for a, w in zip(arrays, whole)],
        in_specs=[ANY] * n, out_specs=[ANY] * n, scratch_shapes=_exchange_sems(n),
    )(*arrays)


def _columns_from_slabs(slabs):
    def body(*refs):
        k = len(refs) // 2
        for src, dst in zip(refs[:k], refs[k:]):
            n = src.shape[2]
            for i in range(N_DEV):
                dst[:, pl.ds(n * i, n)] = src[i]

    return pl.pallas_call(
        body, name="columns_from_slabs",
        out_shape=[jax.ShapeDtypeStruct((s.shape[1], N_DEV * s.shape[2]), s.dtype) for s in slabs],
        compiler_params=pltpu.CompilerParams(vmem_limit_bytes=VMEM_LIMIT_BYTES),
    )(*slabs)


def _window_sum(x, win, ahead):
    n = x.shape[0]
    span = 1
    while span < win:
        x = x + pltpu.roll(x, n - span if ahead else span, 0)
        span *= 2
    return x


def _conv_branch(z, ext_u, conv_ref, tm):
    c_w = z.shape[1] // 4
    b, c, v = z[:, :c_w], z[:, c_w:2 * c_w], z[:, 2 * c_w:3 * c_w]
    u = c * v
    ext_u[pl.ds(HALO, tm), :] = u
    u1 = ext_u[pl.ds(HALO - 1, tm), :]
    u2 = ext_u[pl.ds(HALO - 2, tm), :]
    yc = conv_ref[pl.ds(2, 1), :] * u + conv_ref[pl.ds(1, 1), :] * u1 + conv_ref[pl.ds(0, 1), :] * u2
    return b, c, v, u, u1, u2, yc


def _pool_branch(p, ext_p, pool_w_ref, tm):
    ext_p[pl.ds(HALO, tm), :] = p
    pooled, mixed = [], []
    for g, win in enumerate(POOL_WINDOWS):
        s = _window_sum(ext_p[:, pl.ds(POOL_GROUP * g, POOL_GROUP)], win, ahead=False)[HALO:HALO + tm, :]
        pooled.append((s * (1.0 / win) - p[:, POOL_GROUP * g:POOL_GROUP * (g + 1)]).astype(BF16))
        mixed.append(_dot(pooled[-1], pool_w_ref[g].astype(BF16)))
    return pooled, mixed


def _meta_forward(meta, g1, w_in):
    def body(meta_ref, g1_ref, w_ref, a_ref, z_ref):
        hat, _ = _rms_stats(meta_ref[...])
        a = (hat * g1_ref[...]).astype(BF16)
        a_ref[...] = a
        z_ref[...] = _dot(a, w_ref[...])

    return pl.pallas_call(
        body, name="meta_forward",
        out_shape=[jax.ShapeDtypeStruct(meta.shape, BF16), jax.ShapeDtypeStruct((N_META, w_in.shape[1]), F32)],
        compiler_params=pltpu.CompilerParams(vmem_limit_bytes=VMEM_LIMIT_BYTES),
    )(meta, g1, w_in)


def _mixer_forward(x2d, z_meta, g1, w_in, conv_w, pool_w, pool_scale, w_out, g2, n_seq, to_gather):
    t, d = x2d.shape
    zw = w_in.shape[1]
    cw = zw // 4
    s = t // n_seq
    tm = min(TM_MIX, s)
    nj = s // tm
    ng = len(to_gather)

    def body(x_ref, zm_ref, g1_ref, win_ref, conv_ref, pw_ref, ps_ref, wout_ref, g2_ref, *rest):
        shards, (h1_ref, z_ref, m_ref, pooled_ref, mixed_ref), slabs = rest[:ng], rest[ng:ng + 5], rest[ng + 5:2 * ng + 5]
        ext_u, ext_p = rest[2 * ng + 5:2 * ng + 7]
        start, forward, finish = _gather_ops(shards, slabs, *rest[2 * ng + 7:])
        pl.when((pl.program_id(0) == 0) & (pl.program_id(1) == 0))(start)

        @pl.when(pl.program_id(1) == 0)
        def _():
            zm = zm_ref[...]
            ext_u[pl.ds(0, HALO), :] = zm[:, cw:2 * cw] * zm[:, 2 * cw:3 * cw]
            ext_p[pl.ds(0, HALO), :] = zm[:, 3 * cw:]

        h0 = x_ref[...]
        hat, _ = _rms_stats(h0)
        z = _dot((hat * g1_ref[...]).astype(BF16), win_ref[...])
        z_ref[...] = z.astype(BF16)
        b, _, _, _, _, _, yc = _conv_branch(z, ext_u, conv_ref, tm)
        pooled, mixed = _pool_branch(z[:, 3 * cw:], ext_p, pw_ref, tm)
        pooled_ref[...] = jnp.concatenate(pooled, axis=1)
        mixed_ref[...] = jnp.concatenate(mixed, axis=1).astype(BF16)
        ps = ps_ref[...]
        y = [b * yc] + [mixed[g] * ps[:, POOL_GROUP * g:POOL_GROUP * (g + 1)] for g in range(len(POOL_WINDOWS))]
        m = _dot(jnp.concatenate(y, axis=1).astype(BF16), wout_ref[...])
        m_ref[...] = m
        m_hat, _ = _rms_stats(m)
        h1_ref[...] = h0 + m_hat * g2_ref[...]
        ext_u[pl.ds(0, HALO), :] = ext_u[pl.ds(tm, HALO), :]
        ext_p[pl.ds(0, HALO), :] = ext_p[pl.ds(tm, HALO), :]

        @pl.when((pl.program_id(0) == n_seq - 1) & (pl.program_id(1) == nj - 1))
        def _():
            forward()
            finish()

    row = lambda b, j: (b * nj + j, 0)
    out = pl.pallas_call(
        body, name="mixer_forward", grid=(n_seq, nj),
        in_specs=[pl.BlockSpec((tm, d), row), _const(z_meta.shape), _const(g1.shape), _resident(w_in.shape), _const(conv_w.shape),
                  _const(pool_w.shape), _const(pool_scale.shape), _resident(w_out.shape), _const(g2.shape)] + [ANY] * ng,
        out_specs=[pl.BlockSpec((tm, d), row), pl.BlockSpec((tm, zw), row), pl.BlockSpec((tm, d), row), pl.BlockSpec((tm, cw), row),
                   pl.BlockSpec((tm, cw), row)] + [ANY] * ng,
        out_shape=[jax.ShapeDtypeStruct((t, d), F32), jax.ShapeDtypeStruct((t, zw), BF16), jax.ShapeDtypeStruct((t, d), F32),
                   jax.ShapeDtypeStruct((t, cw), BF16), jax.ShapeDtypeStruct((t, cw), BF16)]
        + [jax.ShapeDtypeStruct((N_DEV, *a.shape), a.dtype) for a in to_gather],
        scratch_shapes=[pltpu.VMEM((tm + HALO, cw), F32), pltpu.VMEM((tm + HALO, cw), F32)] + _exchange_sems(ng),
        compiler_params=_params("arbitrary", "arbitrary"),
    )(x2d, z_meta, g1, w_in, conv_w, pool_w, pool_scale, w_out, g2, *to_gather)
    return out[:5], out[5:]


def _mixer_backward(x2d, dh1, m, z, pooled, mixed, meta, a_meta, z_meta, g1, w_in, conv_w, pool_w, pool_scale, w_out, g2, n_seq, to_exchange):
    t, d = x2d.shape
    zw = w_in.shape[1]
    cw = zw // 4
    s = t // n_seq
    tm = min(TM_MIX, s)
    nj = s // tm
    n_groups = len(POOL_WINDOWS)
    zs = zw // N_DEV
    nx = len(to_exchange)

    def body(x_ref, dh1_ref, m_ref, z_ref, zprev_ref, pooled_ref, mixed_ref, meta_ref, am_ref, zm_ref, g1_ref, win_ref, conv_ref, pw_ref, ps_ref, wout_ref,
             g2_ref, *rest):
        sent, rest = rest[:nx], rest[nx:]
        gx_ref, dwin_ref, dwout_ref, dg1_ref, dg2_ref, dconv_ref, dpw_ref, dps_ref, dmeta_ref = rest[:9]
        landed, rest = rest[9:9 + nx], rest[9 + nx:]
        ext_u, ext_dyc, ext_dq, acc_win, acc_wout, stage16, sem = rest[:7]
        start, finish = _exchange_ops(sent, landed, [False] * nx, *rest[7:])
        b_id, j = pl.program_id(0), pl.program_id(1)
        jr = nj - 1 - j
        pl.when((b_id == 0) & (j == 0))(start)

        @pl.when((b_id == 0) & (j == 0))
        def _():
            acc_win[...] = jnp.zeros_like(acc_win)
            acc_wout[...] = jnp.zeros_like(acc_wout)
            for r in (dg1_ref, dg2_ref, dconv_ref, dpw_ref, dps_ref, dmeta_ref):
                r[...] = jnp.zeros_like(r)

        @pl.when(j == 0)
        def _():
            ext_dyc[pl.ds(tm, HALO), :] = jnp.zeros((HALO, cw), F32)
            ext_dq[pl.ds(tm, HALO), :] = jnp.zeros((HALO, cw), F32)

        zm = zm_ref[...]
        halo = jnp.where(jr == 0, zm, zprev_ref[...].astype(F32))
        ext_u[pl.ds(0, HALO), :] = halo[:, cw:2 * cw] * halo[:, 2 * cw:3 * cw]

        h0 = x_ref[...]
        hat0, rstd0 = _rms_stats(h0)
        g1 = g1_ref[...]
        a = (hat0 * g1).astype(BF16)
        b, c, v, u, u1, u2, yc = _conv_branch(z_ref[...].astype(F32), ext_u, conv_ref, tm)
        mixed = [mixed_ref[:, pl.ds(POOL_GROUP * g, POOL_GROUP)].astype(F32) for g in range(n_groups)]
        ps = ps_ref[...]
        y = [b * yc] + [mixed[g] * ps[:, POOL_GROUP * g:POOL_GROUP * (g + 1)] for g in range(n_groups)]
        ycat = jnp.concatenate(y, axis=1).astype(BF16)

        dh1v = dh1_ref[...]
        m_hat, m_rstd = _rms_stats(m_ref[...])
        dm, dg2 = _rms_bwd(m_hat, m_rstd, g2_ref[...], dh1v)
        dg2_ref[...] += dg2
        dm = dm.astype(BF16)
        acc_wout[...] += _dot_tn(ycat, dm)
        dycat = _dot_nt(dm, wout_ref[...])

        dyconv = dycat[:, :cw]
        db = dyconv * yc
        dyc = dyconv * b
        ext_dyc[pl.ds(0, tm), :] = dyc
        du = (conv_ref[pl.ds(2, 1), :] * dyc + conv_ref[pl.ds(1, 1), :] * ext_dyc[pl.ds(1, tm), :]
              + conv_ref[pl.ds(0, 1), :] * ext_dyc[pl.ds(2, tm), :])
        dconv_ref[pl.ds(2, 1), :] += jnp.sum(dyc * u, axis=0, keepdims=True)
        dconv_ref[pl.ds(1, 1), :] += jnp.sum(dyc * u1, axis=0, keepdims=True)
        dconv_ref[pl.ds(0, 1), :] += jnp.sum(dyc * u2, axis=0, keepdims=True)

        dp = []
        for g, win in enumerate(POOL_WINDOWS):
            lanes = pl.ds(POOL_GROUP * g, POOL_GROUP)
            dypool = dycat[:, cw + POOL_GROUP * g:cw + POOL_GROUP * (g + 1)]
            dps_ref[:, lanes] += jnp.sum(dypool * mixed[g], axis=0, keepdims=True)
            dmixed = (dypool * ps[:, POOL_GROUP * g:POOL_GROUP * (g + 1)]).astype(BF16)
            dpw_ref[g] += _dot_tn(pooled_ref[:, lanes], dmixed)
            dq = _dot_nt(dmixed, pw_ref[g].astype(BF16))
            ext_dq[pl.ds(0, tm), lanes] = dq
            acc = _window_sum(ext_dq[:, lanes], win, ahead=True)[0:tm, :]
            dp.append(acc * (1.0 / win) - dq)

        dz = jnp.concatenate([db, du * v, du * c] + dp, axis=1).astype(BF16)
        acc_win[...] += _dot_tn(a, dz)
        dh0, dg1 = _rms_bwd(hat0, rstd0, g1, _dot_nt(dz, win_ref[...]))
        dg1_ref[...] += dg1
        gx_ref[...] = dh1v + dh0

        ext_dyc[pl.ds(tm, HALO), :] = ext_dyc[pl.ds(0, HALO), :]
        ext_dq[pl.ds(tm, HALO), :] = ext_dq[pl.ds(0, HALO), :]

        @pl.when(jr == 0)
        def _():
            ext_dyc[pl.ds(tm - HALO, HALO), :] = jnp.zeros((HALO, cw), F32)
            ext_dq[pl.ds(tm - HALO, HALO), :] = jnp.zeros((HALO, cw), F32)
            du_m = (conv_ref[pl.ds(1, 1), :] * ext_dyc[pl.ds(tm - HALO + 1, HALO), :]
                    + conv_ref[pl.ds(0, 1), :] * ext_dyc[pl.ds(tm - HALO + 2, HALO), :])
            dp_m = []
            for g, win in enumerate(POOL_WINDOWS):
                lanes = pl.ds(POOL_GROUP * g, POOL_GROUP)
                acc = ext_dq[pl.ds(tm - HALO + 1, HALO), lanes]
                for k in range(2, win):
                    acc = acc + ext_dq[pl.ds(tm - HALO + k, HALO), lanes]
                dp_m.append(acc * (1.0 / win))
            dz_m = jnp.concatenate([jnp.zeros((HALO, cw), F32), du_m * zm[:, 2 * cw:3 * cw], du_m * zm[:, cw:2 * cw]] + dp_m,
                                   axis=1).astype(BF16)
            acc_win[...] += _dot_tn(am_ref[...], dz_m)
            hat_m, rstd_m = _rms_stats(meta_ref[...])
            dmeta, dg1_m = _rms_bwd(hat_m, rstd_m, g1, _dot_nt(dz_m, win_ref[...]))
            dg1_ref[...] += dg1_m
            dmeta_ref[...] += dmeta

        @pl.when((b_id == n_seq - 1) & (j == nj - 1))
        def _():
            pieces = [(acc_win, zs * i, dwin_ref.at[i]) for i in range(N_DEV)]
            pieces += [(acc_wout, zs * i, dwout_ref.at[:, pl.ds(zs * i, zs)]) for i in range(d // zs)]
            copies = []
            for k, (acc, col, dst) in enumerate(pieces):
                if k >= 2:
                    copies[k - 2].wait()
                stage16[k % 2] = acc[:, pl.ds(col, zs)].astype(BF16)
                copies.append(pltpu.make_async_copy(stage16.at[k % 2], dst, sem.at[k % 2]))
                copies[k].start()
            copies[-2].wait()
            copies[-1].wait()
            finish()

    row = lambda b, j: (b * nj + nj - 1 - j, 0)
    prev = lambda b, j: (jnp.maximum((b * s + (nj - 1 - j) * tm) // HALO - 1, 0), 0)
    small = [g1.shape, g2.shape, conv_w.shape, pool_w.shape, pool_scale.shape, meta.shape]
    out = pl.pallas_call(
        body, name="mixer_backward", grid=(n_seq, nj),
        in_specs=[pl.BlockSpec((tm, d), row), pl.BlockSpec((tm, d), row), pl.BlockSpec((tm, d), row), pl.BlockSpec((tm, zw), row),
                  pl.BlockSpec((HALO, zw), prev), pl.BlockSpec((tm, cw), row), pl.BlockSpec((tm, cw), row), _const(meta.shape), _const(a_meta.shape), _const(z_meta.shape), _const(g1.shape),
                  _resident(w_in.shape), _const(conv_w.shape), _const(pool_w.shape), _const(pool_scale.shape), _resident(w_out.shape),
                  _const(g2.shape)] + [ANY] * nx,
        out_specs=[pl.BlockSpec((tm, d), row), ANY, ANY] + [_const(sh) for sh in small] + [ANY] * nx,
        out_shape=[jax.ShapeDtypeStruct((t, d), F32), jax.ShapeDtypeStruct((N_DEV, d, zs), BF16),
                   jax.ShapeDtypeStruct(w_out.shape, BF16)] + [jax.ShapeDtypeStruct(sh, F32) for sh in small]
        + [jax.ShapeDtypeStruct(a.shape, a.dtype) for a in to_exchange],
        scratch_shapes=[pltpu.VMEM((tm + HALO, cw), F32)] * 3
        + [pltpu.VMEM(w_in.shape, F32), pltpu.VMEM(w_out.shape, F32), pltpu.VMEM((2, d, zs), BF16),
           pltpu.SemaphoreType.DMA((2,))] + _exchange_sems(nx),
        compiler_params=_params("arbitrary", "arbitrary"),
    )(x2d, dh1, m, z, z, pooled, mixed, meta, a_meta, z_meta, g1, w_in, conv_w, pool_w, pool_scale, w_out, g2, *to_exchange)
    return out[:9], out[9:]


def _ffn_forward_backward(h1, target, g3, w_gate, w_up, w_down, g4):
    t, d = h1.shape
    ff = w_gate.shape[0]
    tm = min(TM_FFN, t)
    nt = t // tm
    chunks = [(s, min(FFN_CHUNK, ff - s)) for s in range(0, ff, FFN_CHUNK)]

    def body(h1_ref, h1pp_ref, tgt_ref, g3_ref, wg_ref, wu_ref, wd_ref, g4_ref,
             f_ref, act_ref, dd_ref, dgate_ref, dup_ref, dh1_ref, loss_ref, dg3_ref, dg4_ref, *slots):
        gate_s, up_s, dd_s, dh2_s, df_s = slots
        i = pl.program_id(0)

        def forward(slot):
            h1v = h1_ref[...]
            hat, _ = _rms_stats(h1v)
            f = (hat * g3_ref[...]).astype(BF16)
            f_ref[...] = f
            s, n = chunks[0]
            gate, up = _dot_nt(f_ref[...], wg_ref[pl.ds(s, n), :]), _dot_nt(f_ref[...], wu_ref[pl.ds(s, n), :])
            yield
            down = None
            for k, (s, n) in enumerate(chunks):
                gate_s.at[slot][:, pl.ds(s, n)] = gate.astype(BF16)
                up_s.at[slot][:, pl.ds(s, n)] = up.astype(BF16)
                act = (gate * jax.nn.sigmoid(gate) * up).astype(BF16)
                act_ref[:, pl.ds(s, n)] = act
                if k + 1 < len(chunks):
                    s1, n1 = chunks[k + 1]
                    gate, up = _dot_nt(f_ref[...], wg_ref[pl.ds(s1, n1), :]), _dot_nt(f_ref[...], wu_ref[pl.ds(s1, n1), :])
                yield
                part = _dot(act_ref[:, pl.ds(s, n)], wd_ref[pl.ds(s, n), :])
                down = part if down is None else down + part
                yield
            d_hat, d_rstd = _rms_stats(down)
            g4 = g4_ref[...]
            err = h1v + d_hat * g4 - tgt_ref[...]
            loss_ref[...] += jnp.sum(err * err) * (0.5 / d)
            dh2 = err * (1.0 / d)
            dh2_s.at[slot][...] = dh2
            dd, dg4 = _rms_bwd(d_hat, d_rstd, g4, dh2)
            dg4_ref[...] += dg4
            dd = dd.astype(BF16)
            dd_ref[...] = dd
            dd_s.at[slot][...] = dd

        def backward(slot):
            s, n = chunks[0]
            dact = _dot_nt(dd_s.at[slot][...], wd_ref[pl.ds(s, n), :])
            yield
            df = None
            for k, (s, n) in enumerate(chunks):
                gate = gate_s.at[slot][:, pl.ds(s, n)].astype(F32)
                up = up_s.at[slot][:, pl.ds(s, n)].astype(F32)
                sig = jax.nn.sigmoid(gate)
                dup = (dact * (gate * sig)).astype(BF16)
                dgate = (dact * up * (sig * (1.0 + gate * (1.0 - sig)))).astype(BF16)
                dup_ref[:, pl.ds(s, n)] = dup
                dgate_ref[:, pl.ds(s, n)] = dgate
                if k + 1 < len(chunks):
                    s1, n1 = chunks[k + 1]
                    dact = _dot_nt(dd_s.at[slot][...], wd_ref[pl.ds(s1, n1), :])
                yield
                part = _dot(dgate_ref[:, pl.ds(s, n)], wg_ref[pl.ds(s, n), :]) + _dot(dup_ref[:, pl.ds(s, n)], wu_ref[pl.ds(s, n), :])
                df = part if df is None else df + part
                yield
            df_s.at[slot][...] = df

        def last(slot):
            hat, rstd = _rms_stats(h1pp_ref[...])
            dh1, dg3 = _rms_bwd(hat, rstd, g3_ref[...], df_s.at[slot][...])
            dg3_ref[...] += dg3
            dh1_ref[...] = dh2_s.at[slot][...] + dh1

        def emit(parity, with_forward, with_backward, with_last):
            fwd = forward(parity) if with_forward else iter(())
            bwd = backward(1 - parity) if with_backward else iter(())
            next(fwd, None)
            if with_last:
                last(parity)
            for _ in range(FFN_BACKWARD_LAG):
                next(fwd, None)
            alive = True
            while alive:
                alive = next(bwd, True) is None
                alive = (next(fwd, True) is None) or alive

        @pl.when(i == 0)
        def _():
            for r in (loss_ref, dg3_ref, dg4_ref, *slots):
                r[...] = jnp.zeros_like(r)

        @pl.when(i < nt)
        def _():
            emit(i % 2, True, True, True)

        @pl.when(i == nt)
        def _():
            emit(nt % 2, False, True, True)

        @pl.when(i == nt + 1)
        def _():
            emit((nt + 1) % 2, False, False, True)

    cur = lambda i: (jnp.minimum(i, nt - 1), 0)
    prev = lambda i: (jnp.clip(i - 1, 0, nt - 1), 0)
    prev2 = lambda i: (jnp.clip(i - 2, 0, nt - 1), 0)
    return pl.pallas_call(
        body, name="ffn_forward_backward", grid=(nt + 2,),
        in_specs=[pl.BlockSpec((tm, d), cur), pl.BlockSpec((tm, d), prev2), pl.BlockSpec((tm, d), cur), _const(g3.shape),
                  _resident(w_gate.shape), _resident(w_up.shape), _resident(w_down.shape), _const(g4.shape)],
        out_specs=[pl.BlockSpec((tm, d), cur), pl.BlockSpec((tm, ff), cur), pl.BlockSpec((tm, d), cur), pl.BlockSpec((tm, ff), prev),
                   pl.BlockSpec((tm, ff), prev), pl.BlockSpec((tm, d), prev2), _const((8, 128)), _const(g3.shape), _const(g4.shape)],
        out_shape=[jax.ShapeDtypeStruct((t, d), BF16), jax.ShapeDtypeStruct((t, ff), BF16), jax.ShapeDtypeStruct((t, d), BF16),
                   jax.ShapeDtypeStruct((t, ff), BF16), jax.ShapeDtypeStruct((t, ff), BF16), jax.ShapeDtypeStruct((t, d), F32),
                   jax.ShapeDtypeStruct((8, 128), F32), jax.ShapeDtypeStruct(g3.shape, F32), jax.ShapeDtypeStruct(g4.shape, F32)],
        scratch_shapes=[pltpu.VMEM((2, tm, ff), BF16)] * 2 + [pltpu.VMEM((2, tm, d), BF16)] + [pltpu.VMEM((2, tm, d), F32)] * 2,
        compiler_params=_params("arbitrary"),
    )(h1, h1, target, g3, w_gate, w_up, w_down, g4)


def _ffn_weight_grads(f, dd, dgate, dup, act):
    t, d = f.shape
    ff = dgate.shape[1]
    tm = min(TM_WGRAD, t)
    nt = t // tm
    fc = ff // FF_CHUNKS

    def body(f_ref, dd_ref, dgate_ref, dup_ref, act_ref, dwg_ref, dwu_ref, dwd_ref, acc_g, acc_u, acc_d, stage, sem):
        c, i = pl.program_id(0), pl.program_id(1)

        @pl.when(i == 0)
        def _():
            acc_g[...] = jnp.zeros_like(acc_g)
            acc_u[...] = jnp.zeros_like(acc_u)
            acc_d[...] = jnp.zeros_like(acc_d)

        fv = f_ref[...]
        acc_g[...] += _dot_tn(fv, dgate_ref[...])
        acc_u[...] += _dot_tn(fv, dup_ref[...])
        acc_d[...] += _dot_tn(act_ref[...], dd_ref[...])

        @pl.when(i == nt - 1)
        def _():
            rows = pl.ds(pl.multiple_of(c * fc, 16), fc)
            copies = []
            for k, (acc, out, transposed) in enumerate(((acc_d, dwd_ref, False), (acc_g, dwg_ref, True), (acc_u, dwu_ref, True))):
                if k >= 2:
                    copies[k - 2].wait()
                stage[k % 2] = (acc[...].T if transposed else acc[...]).astype(BF16)
                copies.append(pltpu.make_async_copy(stage.at[k % 2], out.at[rows, :], sem.at[k % 2]))
                copies[k].start()
            copies[-2].wait()
            copies[-1].wait()

    row = lambda c, i: (i, 0)
    col = lambda c, i: (i, c)
    return pl.pallas_call(
        body, name="ffn_weight_grads", grid=(FF_CHUNKS, nt),
        in_specs=[pl.BlockSpec((tm, d), row), pl.BlockSpec((tm, d), row), pl.BlockSpec((tm, fc), col), pl.BlockSpec((tm, fc), col),
                  pl.BlockSpec((tm, fc), col)],
        out_specs=[ANY, ANY, ANY],
        out_shape=[jax.ShapeDtypeStruct((ff, d), BF16)] * 3,
        scratch_shapes=[pltpu.VMEM((d, fc), F32), pltpu.VMEM((d, fc), F32), pltpu.VMEM((fc, d), F32), pltpu.VMEM((2, fc, d), BF16),
                        pltpu.SemaphoreType.DMA((2,))],
        compiler_params=_params("arbitrary", "arbitrary"),
    )(f, dd, dgate, dup, act)


def _adamw(w, g, m, v):
    m = ADAM_B1 * m + (1.0 - ADAM_B1) * g
    v = ADAM_B2 * v + (1.0 - ADAM_B2) * (g * g)
    m_hat = m / (1.0 - ADAM_B1 ** ADAM_STEP)
    v_hat = v / (1.0 - ADAM_B2 ** ADAM_STEP)
    return -ADAM_LR * (m_hat / (jnp.sqrt(v_hat) + ADAM_EPS) + ADAM_WD * w), m, v


def _sum_slabs(ref):
    total = ref[0].astype(F32)
    for i in range(1, N_DEV):
        total = total + ref[i].astype(F32)
    return total


def _adamw_rows(r, c):
    tr = r
    for cand in range(8, r, 8):
        if r % cand == 0 and cand * c <= ADAMW_BLOCK_ELEMS:
            tr = cand
    return r if r * c <= ADAMW_BLOCK_ELEMS else tr


def _reduce_adamw_carrying(parts, ws, ms, vs, to_exchange, whole):
    k, nx = len(ws), len(to_exchange)
    r, c = ws[0].shape
    tr = _adamw_rows(r, c)
    steps = r // tr

    def body(*refs):
        p_refs, w_refs, m_refs, v_refs = (refs[a * k:(a + 1) * k] for a in range(4))
        sent, outs = refs[4 * k:4 * k + nx], refs[4 * k + nx:8 * k + nx]
        landed, sems = refs[8 * k + nx:8 * k + 2 * nx], refs[8 * k + 2 * nx:]
        start, finish = _exchange_ops(sent, landed, whole, *sems)
        pl.when(pl.program_id(0) == 0)(start)
        for a in range(k):
            g = _sum_slabs(p_refs[a])
            outs[4 * a][...] = g
            outs[4 * a + 1][...], outs[4 * a + 2][...], outs[4 * a + 3][...] = _adamw(w_refs[a][...], g, m_refs[a][...], v_refs[a][...])
        pl.when(pl.program_id(0) == steps - 1)(finish)

    blk = pl.BlockSpec((tr, c), lambda i: (i, 0))
    out = pl.pallas_call(
        body, name="adamw_ffn_exchange_rest", grid=(steps,),
        in_specs=[pl.BlockSpec((N_DEV, tr, c), lambda i: (0, i, 0))] * k + [blk] * (3 * k) + [ANY] * nx,
        out_specs=[blk] * (4 * k) + [ANY] * nx,
        out_shape=[jax.ShapeDtypeStruct((r, c), F32)] * (4 * k)
        + [jax.ShapeDtypeStruct((N_DEV, *a.shape) if w else a.shape, a.dtype) for a, w in zip(to_exchange, whole)],
        scratch_shapes=_exchange_sems(nx),
        compiler_params=_params("arbitrary"),
    )(*parts, *ws, *ms, *vs, *to_exchange)
    return [tuple(out[4 * a:4 * a + 4]) for a in range(k)], out[4 * k:]


def _reduce_adamw(parts, w, m, v, name):
    r, c = w.shape
    tr = _adamw_rows(r, c)

    def body(p_ref, w_ref, m_ref, v_ref, g_out, d_out, m_out, v_out):
        g = _sum_slabs(p_ref)
        g_out[...] = g
        d_out[...], m_out[...], v_out[...] = _adamw(w_ref[...], g, m_ref[...], v_ref[...])

    blk = pl.BlockSpec((tr, c), lambda i: (i, 0))
    return pl.pallas_call(
        body, name=name, grid=(r // tr,),
        in_specs=[pl.BlockSpec((N_DEV, tr, c), lambda i: (0, i, 0)), blk, blk, blk],
        out_specs=[blk] * 4, out_shape=[jax.ShapeDtypeStruct((r, c), F32)] * 4,
        compiler_params=_params("arbitrary"),
    )(parts, w, m, v)


def _reduce_adamw_small(parts, ws, ms, vs, loss_parts):
    n = len(parts)

    def body(*refs):
        p_refs, w_refs, m_refs, v_refs = (refs[k * n:(k + 1) * n] for k in range(4))
        outs = refs[4 * n + 1:]
        outs[4 * n][...] = _sum_slabs(refs[4 * n])
        for a in range(n):
            g = _sum_slabs(p_refs[a])
            outs[4 * a][...] = g
            outs[4 * a + 1][...], outs[4 * a + 2][...], outs[4 * a + 3][...] = _adamw(w_refs[a][...], g, m_refs[a][...], v_refs[a][...])

    out = pl.pallas_call(
        body, name="adamw_replicated",
        out_shape=[jax.ShapeDtypeStruct(w.shape, F32) for w in ws for _ in range(4)] + [jax.ShapeDtypeStruct(loss_parts.shape[1:], F32)],
        compiler_params=pltpu.CompilerParams(vmem_limit_bytes=VMEM_LIMIT_BYTES),
    )(*parts, *ws, *ms, *vs, loss_parts)
    return [tuple(out[4 * a:4 * a + 4]) for a in range(n)], out[4 * n]


def kernel(x, meta_tokens, norm_mix_pre, w_in, conv_w, pool_w, pool_scale, w_out, norm_mix_post, norm_ffn_pre, w_gate, w_up, w_down, norm_ffn_post, loss_target, m_meta_tokens, m_norm_mix_pre, m_w_in, m_conv_w, m_pool_w, m_pool_scale, m_w_out, m_norm_mix_post, m_norm_ffn_pre, m_w_gate, m_w_up, m_w_down, m_norm_ffn_post, v_meta_tokens, v_norm_mix_pre, v_w_in, v_conv_w, v_pool_w, v_pool_scale, v_w_out, v_norm_mix_post, v_norm_ffn_pre, v_w_gate, v_w_up, v_w_down, v_norm_ffn_post):
    n_seq, seq, d = x.shape
    x2d = x.reshape(n_seq * seq, d)
    target = loss_target.reshape(n_seq * seq, d)

    t_ = lambda a: jnp.swapaxes(a[0], 0, 1)
    (win_s, wout_s, meta_s, conv_s), ffn_shards = _gather_first_weights(
        [w_in[0], w_out[0], meta_tokens, conv_w[0]], [BF16, BF16, F32, F32], [t_(w_gate), t_(w_up), w_down[0]])
    (win_b,) = _columns_from_slabs([win_s])
    wout_b = wout_s.reshape(d, d)
    meta = jnp.transpose(meta_s, (1, 0, 2)).reshape(N_META, d)
    conv = jnp.transpose(conv_s, (1, 0, 2)).reshape(CONV_WIDTH, -1)
    pw, ps = pool_w[0], pool_scale

    a_meta, z_meta = _meta_forward(meta, norm_mix_pre, win_b)
    (h1, z, m, pooled, mixed), ffn_slabs = _mixer_forward(x2d, z_meta, norm_mix_pre, win_b, conv, pw, ps, wout_b, norm_mix_post, n_seq, ffn_shards)
    wg_b, wu_b, wd_b = (s.reshape(-1, d) for s in ffn_slabs)
    f, act, dd, dgate, dup, dh1, loss_sum, dg3, dg4 = _ffn_forward_backward(h1, target, norm_ffn_pre, wg_b, wu_b, wd_b, norm_ffn_post)
    ffn_grads = _ffn_weight_grads(f, dd, dgate, dup, act)
    (gx, dwin, dwout, dg1, dg2, dconv, dpw, dps, dmeta), ffn_parts = _mixer_backward(
        x2d, dh1, m, z, pooled, mixed, meta, a_meta, z_meta, norm_mix_pre, win_b, conv, pw, ps, wout_b, norm_mix_post, n_seq,
        [g.reshape(N_DEV, -1, d) for g in ffn_grads])

    dmeta_s = jnp.transpose(dmeta.reshape(N_META, N_DEV, -1), (1, 0, 2))
    dconv_s = jnp.transpose(dconv.reshape(CONV_WIDTH, N_DEV, -1), (1, 0, 2))
    ffn_res, last = _reduce_adamw_carrying(
        ffn_parts, [t_(w_gate), t_(w_up), w_down[0]], [t_(m_w_gate), t_(m_w_up), m_w_down[0]], [t_(v_w_gate), t_(v_w_up), v_w_down[0]],
        [dwin, dwout.reshape(N_DEV, -1, d), dmeta_s, dconv_s, dg1, dg2, dg3, dg4, dpw, dps, loss_sum], [False] * 4 + [True] * 7)
    replicated = last[4:10]

    names = ["meta_tokens", "norm_mix_pre", "w_in", "conv_w", "pool_w", "pool_scale", "w_out", "norm_mix_post", "norm_ffn_pre", "w_gate",
             "w_up", "w_down", "norm_ffn_post"]
    res = {"w_gate": tuple(jnp.swapaxes(o, 0, 1)[None] for o in ffn_res[0]),
           "w_up": tuple(jnp.swapaxes(o, 0, 1)[None] for o in ffn_res[1]), "w_down": tuple(o[None] for o in ffn_res[2])}
    for nm, parts, w, m_, v_ in (("w_in", last[0], w_in, m_w_in, v_w_in), ("w_out", last[1], w_out, m_w_out, v_w_out),
                                 ("conv_w", last[3], conv_w, m_conv_w, v_conv_w)):
        res[nm] = tuple(o[None] for o in _reduce_adamw(parts, w[0], m_[0], v_[0], "adamw_" + nm))
    res["meta_tokens"] = tuple(_reduce_adamw(last[2], meta_tokens, m_meta_tokens, v_meta_tokens, "adamw_meta_tokens"))
    small, loss = _reduce_adamw_small(
        replicated, [norm_mix_pre, norm_mix_post, norm_ffn_pre, norm_ffn_post, pool_w[0], pool_scale],
        [m_norm_mix_pre, m_norm_mix_post, m_norm_ffn_pre, m_norm_ffn_post, m_pool_w[0], m_pool_scale],
        [v_norm_mix_pre, v_norm_mix_post, v_norm_ffn_pre, v_norm_ffn_post, v_pool_w[0], v_pool_scale], last[10])
    for nm, r in zip(["norm_mix_pre", "norm_mix_post", "norm_ffn_pre", "norm_ffn_post", "pool_w", "pool_scale"], small):
        res[nm] = tuple(o[None] for o in r) if nm == "pool_w" else r

    return (loss[0, 0], gx.reshape(n_seq, seq, d), *[res[nm][0] for nm in names], *[res[nm][1] for nm in names],
            *[res[nm][2] for nm in names], *[res[nm][3] for nm in names])
```

```python
import functools

import jax
import jax.numpy as jnp
from jax import lax
from jax.experimental import pallas as pl
from jax.experimental.pallas import tpu as pltpu

F32, BF16 = jnp.float32, jnp.bfloat16
RMS_EPS = 1e-6
N_META = 16
CONV_WIDTH = 3
POOL_WINDOWS = (2, 4, 8, 16)
POOL_GROUP = 128
HALO = 16
N_DEV = 8
MESH_AXES = ("x", "y", "c")
MESH = pl.DeviceIdType.MESH
VMEM_LIMIT_BYTES = 56 * 1024 * 1024
ADAMW_BLOCK_ELEMS = 64 * 1024
TM_MIX = 512
TM_FFN = 256
FFN_CHUNK = 512
FFN_BACKWARD_LAG = 2
TM_WGRAD = 512
FF_CHUNKS = 2

ADAM_LR, ADAM_B1, ADAM_B2, ADAM_EPS, ADAM_WD, ADAM_STEP = 0.001, 0.9, 0.999, 1e-08, 0.01, 10


def _dot(a, b):
    return jnp.dot(a, b, preferred_element_type=F32)


def _dot_nt(a, b):
    return lax.dot_general(a, b, (((1,), (1,)), ((), ())), preferred_element_type=F32)


def _dot_tn(a, b):
    return lax.dot_general(a, b, (((0,), (0,)), ((), ())), preferred_element_type=F32)


def _rms_stats(h):
    rstd = lax.rsqrt(jnp.mean(h * h, axis=-1, keepdims=True) + RMS_EPS)
    return h * rstd, rstd


def _rms_bwd(hat, rstd, g, dy):
    gdy = dy * g
    proj = jnp.mean(gdy * hat, axis=-1, keepdims=True)
    return rstd * (gdy - hat * proj), jnp.sum(dy * hat, axis=0, keepdims=True)


def _params(*semantics):
    return pltpu.CompilerParams(dimension_semantics=semantics or None, vmem_limit_bytes=VMEM_LIMIT_BYTES)


def _resident(shape):
    zeros = (0,) * len(shape)
    return pl.BlockSpec(shape, lambda *_: zeros, pipeline_mode=pl.Buffered(1))


def _const(shape):
    zeros = (0,) * len(shape)
    return pl.BlockSpec(shape, lambda *_: zeros)


ANY = pl.BlockSpec(memory_space=pl.ANY)


def _my_place():
    x, y, c = (lax.axis_index(a) for a in MESH_AXES)
    return x, y, c


def _exchange_sems(n):
    return [pltpu.SemaphoreType.DMA((n, N_DEV - 1)), pltpu.SemaphoreType.DMA((n, N_DEV - 1)), pltpu.SemaphoreType.DMA((n,))]


def _gather_ops(srcs, outs, send_sems, recv_sems, local_sems):
    n = len(srcs)
    x, y, c = _my_place()
    me, sibling = (x, y, c), (x, y, 1 - c)
    chips = [(1 - x, y), (x, 1 - y), (1 - x, 1 - y)]

    def slab(px, py, pc):
        return 4 * px + 2 * py + pc

    def copy(a, k, block, to, src=None):
        dst = outs[a].at[slab(*block)]
        return pltpu.make_async_remote_copy(
            src_ref=dst if src is None else src, dst_ref=dst, send_sem=send_sems.at[a, k], recv_sem=recv_sems.at[a, k],
            device_id=to, device_id_type=MESH)

    def mine(a):
        return pltpu.make_async_copy(srcs[a], outs[a].at[slab(*me)], local_sems.at[a])

    def first(a):
        return [copy(a, 0, me, sibling, src=srcs[a])] + [copy(a, 1 + j, me, (*chip, c), src=srcs[a]) for j, chip in enumerate(chips)]

    def passed(a, j):
        return copy(a, 4 + j, (*chips[j], c), sibling)

    def start():
        for a in range(n):
            mine(a).start()
            for cp in first(a):
                cp.start()

    def forward():
        for j, chip in enumerate(chips):
            for a in range(n):
                copy(a, 1 + j, (*chip, c), me).wait_recv()
                passed(a, j).start()

    def finish():
        for a in range(n):
            copy(a, 0, sibling, me).wait_recv()
            for j, chip in enumerate(chips):
                copy(a, 4 + j, (*chip, 1 - c), me).wait_recv()
        for a in range(n):
            for cp in first(a) + [passed(a, j) for j in range(len(chips))]:
                cp.wait_send()
            mine(a).wait()

    return start, forward, finish


def _exchange_ops(ins, outs, whole, send_sems, recv_sems, local_sems):
    n = len(ins)
    x, y, c = _my_place()
    me = 4 * x + 2 * y + c

    def src(a, i):
        return ins[a] if whole[a] else ins[a].at[i]

    def mine(a):
        return pltpu.make_async_copy(src(a, me), outs[a].at[me], local_sems.at[a])

    def send(a, k):
        to = (me + k) % N_DEV
        return pltpu.make_async_remote_copy(
            src_ref=src(a, to), dst_ref=outs[a].at[me], send_sem=send_sems.at[a, k - 1], recv_sem=recv_sems.at[a, k - 1],
            device_id=(to // 4, (to // 2) % 2, to % 2), device_id_type=MESH)

    def landed(a, k):
        frm = (me + N_DEV - k) % N_DEV
        return pltpu.make_async_remote_copy(
            src_ref=src(a, frm), dst_ref=outs[a].at[frm], send_sem=send_sems.at[a, k - 1], recv_sem=recv_sems.at[a, k - 1],
            device_id=(x, y, c), device_id_type=MESH)

    def start():
        for a in range(n):
            mine(a).start()
            for k in range(1, N_DEV):
                send(a, k).start()

    def finish():
        for a in range(n):
            for k in range(1, N_DEV):
                landed(a, k).wait_recv()
        for a in range(n):
            for k in range(1, N_DEV):
                send(a, k).wait_send()
            mine(a).wait()

    return start, finish


def _gather_first_weights(gathered, dtypes, cast_only):
    n, k = len(gathered), len(cast_only)

    def body(*refs):
        ins, casts_in = refs[:n], refs[n:n + k]
        outs, casts_out = refs[n + k:2 * n + k], refs[2 * n + k:2 * n + 2 * k]
        stages = refs[2 * n + 2 * k:3 * n + 2 * k]
        start, forward, finish = _gather_ops(stages, outs, *refs[3 * n + 2 * k:])
        for a in range(n):
            stages[a][...] = ins[a][...].astype(stages[a].dtype)
        start()
        for a in range(k):
            casts_out[a][...] = casts_in[a][...].astype(BF16)
        forward()
        finish()

    vmem = pl.BlockSpec(memory_space=pltpu.VMEM)
    out = pl.pallas_call(
        body, name="gather_first_weights",
        out_shape=[jax.ShapeDtypeStruct((N_DEV, *s.shape), d) for s, d in zip(gathered, dtypes)]
        + [jax.ShapeDtypeStruct(s.shape, BF16) for s in cast_only],
        in_specs=[vmem] * (n + k), out_specs=[ANY] * n + [vmem] * k,
        scratch_shapes=[pltpu.VMEM(s.shape, d) for s, d in zip(gathered, dtypes)] + _exchange_sems(n),
        compiler_params=pltpu.CompilerParams(vmem_limit_bytes=VMEM_LIMIT_BYTES),
    )(*gathered, *cast_only)
    return out[:n], out[n:]


def _exchange(arrays, whole, name):
    n = len(arrays)

    def body(*refs):
        start, finish = _exchange_ops(refs[:n], refs[n:2 * n], whole, *refs[2 * n:])
        start()
        finish()

    return pl.pallas_call(
        body, name=name,
        out_shape=[jax.ShapeDtypeStruct((N_DEV, *a.shape) if w else a.shape, a.dtype) for a, w in zip(arrays, whole)],
        in_specs=[ANY] * n, out_specs=[ANY] * n, scratch_shapes=_exchange_sems(n),
    )(*arrays)


def _columns_from_slabs(slabs):
    def body(*refs):
        k = len(refs) // 2
        for src, dst in zip(refs[:k], refs[k:]):
            n = src.shape[2]
            for i in range(N_DEV):
                dst[:, pl.ds(n * i, n)] = src[i]

    return pl.pallas_call(
        body, name="columns_from_slabs",
        out_shape=[jax.ShapeDtypeStruct((s.shape[1], N_DEV * s.shape[2]), s.dtype) for s in slabs],
        compiler_params=pltpu.CompilerParams(vmem_limit_bytes=VMEM_LIMIT_BYTES),
    )(*slabs)


def _window_sum(x, win, ahead):
    n = x.shape[0]
    span = 1
    while span < win:
        x = x + pltpu.roll(x, n - span if ahead else span, 0)
        span *= 2
    return x


def _conv_branch(z, ext_u, conv_ref, tm):
    c_w = z.shape[1] // 4
    b, c, v = z[:, :c_w], z[:, c_w:2 * c_w], z[:, 2 * c_w:3 * c_w]
    u = c * v
    ext_u[pl.ds(HALO, tm), :] = u
    u1 = ext_u[pl.ds(HALO - 1, tm), :]
    u2 = ext_u[pl.ds(HALO - 2, tm), :]
    yc = conv_ref[pl.ds(2, 1), :] * u + conv_ref[pl.ds(1, 1), :] * u1 + conv_ref[pl.ds(0, 1), :] * u2
    return b, c, v, u, u1, u2, yc


def _pool_branch(p, ext_p, pool_w_ref, tm):
    ext_p[pl.ds(HALO, tm), :] = p
    pooled, mixed = [], []
    for g, win in enumerate(POOL_WINDOWS):
        s = _window_sum(ext_p[:, pl.ds(POOL_GROUP * g, POOL_GROUP)], win, ahead=False)[HALO:HALO + tm, :]
        pooled.append((s * (1.0 / win) - p[:, POOL_GROUP * g:POOL_GROUP * (g + 1)]).astype(BF16))
        mixed.append(_dot(pooled[-1], pool_w_ref[g].astype(BF16)))
    return pooled, mixed


def _meta_forward(meta, g1, w_in):
    def body(meta_ref, g1_ref, w_ref, a_ref, z_ref):
        hat, _ = _rms_stats(meta_ref[...])
        a = (hat * g1_ref[...]).astype(BF16)
        a_ref[...] = a
        z_ref[...] = _dot(a, w_ref[...])

    return pl.pallas_call(
        body, name="meta_forward",
        out_shape=[jax.ShapeDtypeStruct(meta.shape, BF16), jax.ShapeDtypeStruct((N_META, w_in.shape[1]), F32)],
        compiler_params=pltpu.CompilerParams(vmem_limit_bytes=VMEM_LIMIT_BYTES),
    )(meta, g1, w_in)


def _mixer_forward(x2d, z_meta, g1, w_in, conv_w, pool_w, pool_scale, w_out, g2, n_seq, to_gather):
    t, d = x2d.shape
    zw = w_in.shape[1]
    cw = zw // 4
    s = t // n_seq
    tm = min(TM_MIX, s)
    nj = s // tm
    ng = len(to_gather)

    def body(x_ref, zm_ref, g1_ref, win_ref, conv_ref, pw_ref, ps_ref, wout_ref, g2_ref, *rest):
        shards, (h1_ref, z_ref, m_ref, pooled_ref, mixed_ref), slabs = rest[:ng], rest[ng:ng + 5], rest[ng + 5:2 * ng + 5]
        ext_u, ext_p = rest[2 * ng + 5:2 * ng + 7]
        start, forward, finish = _gather_ops(shards, slabs, *rest[2 * ng + 7:])
        pl.when((pl.program_id(0) == 0) & (pl.program_id(1) == 0))(start)

        @pl.when(pl.program_id(1) == 0)
        def _():
            zm = zm_ref[...]
            ext_u[pl.ds(0, HALO), :] = zm[:, cw:2 * cw] * zm[:, 2 * cw:3 * cw]
            ext_p[pl.ds(0, HALO), :] = zm[:, 3 * cw:]

        h0 = x_ref[...]
        hat, _ = _rms_stats(h0)
        z = _dot((hat * g1_ref[...]).astype(BF16), win_ref[...])
        z_ref[...] = z.astype(BF16)
        b, _, _, _, _, _, yc = _conv_branch(z, ext_u, conv_ref, tm)
        pooled, mixed = _pool_branch(z[:, 3 * cw:], ext_p, pw_ref, tm)
        pooled_ref[...] = jnp.concatenate(pooled, axis=1)
        mixed_ref[...] = jnp.concatenate(mixed, axis=1).astype(BF16)
        ps = ps_ref[...]
        y = [b * yc] + [mixed[g] * ps[:, POOL_GROUP * g:POOL_GROUP * (g + 1)] for g in range(len(POOL_WINDOWS))]
        m = _dot(jnp.concatenate(y, axis=1).astype(BF16), wout_ref[...])
        m_ref[...] = m
        m_hat, _ = _rms_stats(m)
        h1_ref[...] = h0 + m_hat * g2_ref[...]
        ext_u[pl.ds(0, HALO), :] = ext_u[pl.ds(tm, HALO), :]
        ext_p[pl.ds(0, HALO), :] = ext_p[pl.ds(tm, HALO), :]

        @pl.when((pl.program_id(0) == n_seq - 1) & (pl.program_id(1) == nj - 1))
        def _():
            forward()
            finish()

    row = lambda b, j: (b * nj + j, 0)
    out = pl.pallas_call(
        body, name="mixer_forward", grid=(n_seq, nj),
        in_specs=[pl.BlockSpec((tm, d), row), _const(z_meta.shape), _const(g1.shape), _resident(w_in.shape), _const(conv_w.shape),
                  _const(pool_w.shape), _const(pool_scale.shape), _resident(w_out.shape), _const(g2.shape)] + [ANY] * ng,
        out_specs=[pl.BlockSpec((tm, d), row), pl.BlockSpec((tm, zw), row), pl.BlockSpec((tm, d), row), pl.BlockSpec((tm, cw), row),
                   pl.BlockSpec((tm, cw), row)] + [ANY] * ng,
        out_shape=[jax.ShapeDtypeStruct((t, d), F32), jax.ShapeDtypeStruct((t, zw), BF16), jax.ShapeDtypeStruct((t, d), F32),
                   jax.ShapeDtypeStruct((t, cw), BF16), jax.ShapeDtypeStruct((t, cw), BF16)]
        + [jax.ShapeDtypeStruct((N_DEV, *a.shape), a.dtype) for a in to_gather],
        scratch_shapes=[pltpu.VMEM((tm + HALO, cw), F32), pltpu.VMEM((tm + HALO, cw), F32)] + _exchange_sems(ng),
        compiler_params=_params("arbitrary", "arbitrary"),
    )(x2d, z_meta, g1, w_in, conv_w, pool_w, pool_scale, w_out, g2, *to_gather)
    return out[:5], out[5:]


def _gather_and_mixer_forward(x2d, mixer_shards, ffn_shards, g1, pool_w, pool_scale, g2, n_seq):
    t, d = x2d.shape
    zs, rs, ms, cs = mixer_shards[0].shape[1], mixer_shards[1].shape[0], mixer_shards[2].shape[1], mixer_shards[3].shape[1]
    zw, cw = N_DEV * zs, N_DEV * cs
    s = t // n_seq
    tm = min(TM_MIX, s)
    nj = s // tm
    n1, n2 = len(mixer_shards), len(ffn_shards)
    dtypes = [BF16, BF16, F32, F32] + [BF16] * n2
    shards = list(mixer_shards) + list(ffn_shards)

    def body(x_ref, *rest):
        shard_refs, (g1_ref, pw_ref, ps_ref, g2_ref), rest = rest[:n1 + n2], rest[n1 + n2:n1 + n2 + 4], rest[n1 + n2 + 4:]
        (h1_ref, z_ref, m_ref, pooled_ref, mixed_ref, win_o, wout_o, meta_o, conv_o, am_o, zm_o), rest = rest[:11], rest[11:]
        slabs, rest = rest[:n1 + n2], rest[n1 + n2:]
        stages, rest = rest[:n1 + n2], rest[n1 + n2:]
        win_v, wout_v, meta_v, conv_v, ext_u, ext_p, sem = rest[:7]
        first = _gather_ops(stages[:n1], slabs[:n1], *rest[7:10])
        later = _gather_ops(stages[n1:], slabs[n1:], *rest[10:13])

        @pl.when((pl.program_id(0) == 0) & (pl.program_id(1) == 0))
        def _():
            for src, dst in zip(shard_refs, stages):
                dst[...] = src[...].astype(dst.dtype)
            first[0]()
            later[0]()
            first[1]()
            first[2]()
            copies = [pltpu.make_async_copy(slabs[0].at[i], win_v.at[:, pl.ds(zs * i, zs)], sem.at[i]) for i in range(N_DEV)]
            copies += [pltpu.make_async_copy(slabs[1].at[i], wout_v.at[pl.ds(rs * i, rs), :], sem.at[N_DEV + i]) for i in range(N_DEV)]
            copies += [pltpu.make_async_copy(slabs[2], meta_v, sem.at[2 * N_DEV]), pltpu.make_async_copy(slabs[3], conv_v, sem.at[2 * N_DEV + 1])]
            for cp in copies:
                cp.start()
            for cp in copies:
                cp.wait()
            copies = [pltpu.make_async_copy(win_v, win_o, sem.at[0]), pltpu.make_async_copy(wout_v, wout_o, sem.at[1])]
            for cp in copies:
                cp.start()
            for i in range(N_DEV):
                meta_o[:, pl.ds(ms * i, ms)] = meta_v[i]
                conv_o[:, pl.ds(cs * i, cs)] = conv_v[i]
            hat, _ = _rms_stats(meta_o[...])
            a = (hat * g1_ref[...]).astype(BF16)
            am_o[...] = a
            zm_o[...] = _dot(a, win_v[...])
            for cp in copies:
                cp.wait()

        @pl.when(pl.program_id(1) == 0)
        def _():
            zm = zm_o[...]
            ext_u[pl.ds(0, HALO), :] = zm[:, cw:2 * cw] * zm[:, 2 * cw:3 * cw]
            ext_p[pl.ds(0, HALO), :] = zm[:, 3 * cw:]

        h0 = x_ref[...]
        hat, _ = _rms_stats(h0)
        z = _dot((hat * g1_ref[...]).astype(BF16), win_v[...])
        z_ref[...] = z.astype(BF16)
        b, _, _, _, _, _, yc = _conv_branch(z, ext_u, conv_o, tm)
        pooled, mixed = _pool_branch(z[:, 3 * cw:], ext_p, pw_ref, tm)
        pooled_ref[...] = jnp.concatenate(pooled, axis=1)
        mixed_ref[...] = jnp.concatenate(mixed, axis=1).astype(BF16)
        ps = ps_ref[...]
        y = [b * yc] + [mixed[g] * ps[:, POOL_GROUP * g:POOL_GROUP * (g + 1)] for g in range(len(POOL_WINDOWS))]
        m = _dot(jnp.concatenate(y, axis=1).astype(BF16), wout_v[...])
        m_ref[...] = m
        m_hat, _ = _rms_stats(m)
        h1_ref[...] = h0 + m_hat * g2_ref[...]
        ext_u[pl.ds(0, HALO), :] = ext_u[pl.ds(tm, HALO), :]
        ext_p[pl.ds(0, HALO), :] = ext_p[pl.ds(tm, HALO), :]

        @pl.when((pl.program_id(0) == n_seq - 1) & (pl.program_id(1) == nj - 1))
        def _():
            later[1]()
            later[2]()

    row = lambda b, j: (b * nj + j, 0)
    vmem = pl.BlockSpec(memory_space=pltpu.VMEM)
    small = [(N_META, d), (CONV_WIDTH, cw), (N_META, d), (N_META, zw)]
    out = pl.pallas_call(
        body, name="gather_and_mixer_forward", grid=(n_seq, nj),
        in_specs=[pl.BlockSpec((tm, d), row)] + [vmem] * (n1 + n2)
        + [_const(g1.shape), _const(pool_w.shape), _const(pool_scale.shape), _const(g2.shape)],
        out_specs=[pl.BlockSpec((tm, d), row), pl.BlockSpec((tm, zw), row), pl.BlockSpec((tm, d), row), pl.BlockSpec((tm, cw), row),
                   pl.BlockSpec((tm, cw), row), ANY, ANY] + [_const(sh) for sh in small] + [ANY] * (n1 + n2),
        out_shape=[jax.ShapeDtypeStruct((t, d), F32), jax.ShapeDtypeStruct((t, zw), BF16), jax.ShapeDtypeStruct((t, d), F32),
                   jax.ShapeDtypeStruct((t, cw), BF16), jax.ShapeDtypeStruct((t, cw), BF16),
                   jax.ShapeDtypeStruct((d, zw), BF16), jax.ShapeDtypeStruct((d, d), BF16),
                   jax.ShapeDtypeStruct(small[0], F32), jax.ShapeDtypeStruct(small[1], F32), jax.ShapeDtypeStruct(small[2], BF16),
                   jax.ShapeDtypeStruct(small[3], F32)]
        + [jax.ShapeDtypeStruct((N_DEV, *a.shape), dt) for a, dt in zip(shards, dtypes)],
        scratch_shapes=[pltpu.VMEM(a.shape, dt) for a, dt in zip(shards, dtypes)]
        + [pltpu.VMEM((d, zw), BF16), pltpu.VMEM((d, d), BF16), pltpu.VMEM((N_DEV, N_META, ms), F32),
           pltpu.VMEM((N_DEV, CONV_WIDTH, cs), F32), pltpu.VMEM((tm + HALO, cw), F32), pltpu.VMEM((tm + HALO, cw), F32),
           pltpu.SemaphoreType.DMA((2 * N_DEV + 2,))] + _exchange_sems(n1) + _exchange_sems(n2),
        compiler_params=_params("arbitrary", "arbitrary"),
    )(x2d, *shards, g1, pool_w, pool_scale, g2)
    return out[:5], out[5:11], out[11 + n1:]


def _mixer_backward(x2d, dh1, m, z, pooled, mixed, meta, a_meta, z_meta, g1, w_in, conv_w, pool_w, pool_scale, w_out, g2, n_seq, to_exchange):
    t, d = x2d.shape
    zw = w_in.shape[1]
    cw = zw // 4
    s = t // n_seq
    tm = min(TM_MIX, s)
    nj = s // tm
    n_groups = len(POOL_WINDOWS)
    zs = zw // N_DEV
    nx = len(to_exchange)

    def body(x_ref, dh1_ref, m_ref, z_ref, zprev_ref, pooled_ref, mixed_ref, meta_ref, am_ref, zm_ref, g1_ref, win_ref, conv_ref, pw_ref, ps_ref, wout_ref,
             g2_ref, *rest):
        sent, rest = rest[:nx], rest[nx:]
        gx_ref, dwin_ref, dwout_ref, dg1_ref, dg2_ref, dconv_ref, dpw_ref, dps_ref, dmeta_ref = rest[:9]
        landed, rest = rest[9:9 + nx], rest[9 + nx:]
        ext_u, ext_dyc, ext_dq, acc_win, acc_wout, stage16, sem = rest[:7]
        start, finish = _exchange_ops(sent, landed, [False] * nx, *rest[7:])
        b_id, j = pl.program_id(0), pl.program_id(1)
        jr = nj - 1 - j
        pl.when((b_id == 0) & (j == 0))(start)

        @pl.when((b_id == 0) & (j == 0))
        def _():
            acc_win[...] = jnp.zeros_like(acc_win)
            acc_wout[...] = jnp.zeros_like(acc_wout)
            for r in (dg1_ref, dg2_ref, dconv_ref, dpw_ref, dps_ref, dmeta_ref):
                r[...] = jnp.zeros_like(r)

        @pl.when(j == 0)
        def _():
            ext_dyc[pl.ds(tm, HALO), :] = jnp.zeros((HALO, cw), F32)
            ext_dq[pl.ds(tm, HALO), :] = jnp.zeros((HALO, cw), F32)

        zm = zm_ref[...]
        halo = jnp.where(jr == 0, zm, zprev_ref[...].astype(F32))
        ext_u[pl.ds(0, HALO), :] = halo[:, cw:2 * cw] * halo[:, 2 * cw:3 * cw]

        h0 = x_ref[...]
        hat0, rstd0 = _rms_stats(h0)
        g1 = g1_ref[...]
        a = (hat0 * g1).astype(BF16)
        b, c, v, u, u1, u2, yc = _conv_branch(z_ref[...].astype(F32), ext_u, conv_ref, tm)
        mixed = [mixed_ref[:, pl.ds(POOL_GROUP * g, POOL_GROUP)].astype(F32) for g in range(n_groups)]
        ps = ps_ref[...]
        y = [b * yc] + [mixed[g] * ps[:, POOL_GROUP * g:POOL_GROUP * (g + 1)] for g in range(n_groups)]
        ycat = jnp.concatenate(y, axis=1).astype(BF16)

        dh1v = dh1_ref[...]
        m_hat, m_rstd = _rms_stats(m_ref[...])
        dm, dg2 = _rms_bwd(m_hat, m_rstd, g2_ref[...], dh1v)
        dg2_ref[...] += dg2
        dm = dm.astype(BF16)
        acc_wout[...] += _dot_tn(ycat, dm)
        dycat = _dot_nt(dm, wout_ref[...])

        dyconv = dycat[:, :cw]
        db = dyconv * yc
        dyc = dyconv * b
        ext_dyc[pl.ds(0, tm), :] = dyc
        du = (conv_ref[pl.ds(2, 1), :] * dyc + conv_ref[pl.ds(1, 1), :] * ext_dyc[pl.ds(1, tm), :]
              + conv_ref[pl.ds(0, 1), :] * ext_dyc[pl.ds(2, tm), :])
        dconv_ref[pl.ds(2, 1), :] += jnp.sum(dyc * u, axis=0, keepdims=True)
        dconv_ref[pl.ds(1, 1), :] += jnp.sum(dyc * u1, axis=0, keepdims=True)
        dconv_ref[pl.ds(0, 1), :] += jnp.sum(dyc * u2, axis=0, keepdims=True)

        dp = []
        for g, win in enumerate(POOL_WINDOWS):
            lanes = pl.ds(POOL_GROUP * g, POOL_GROUP)
            dypool = dycat[:, cw + POOL_GROUP * g:cw + POOL_GROUP * (g + 1)]
            dps_ref[:, lanes] += jnp.sum(dypool * mixed[g], axis=0, keepdims=True)
            dmixed = (dypool * ps[:, POOL_GROUP * g:POOL_GROUP * (g + 1)]).astype(BF16)
            dpw_ref[g] += _dot_tn(pooled_ref[:, lanes], dmixed)
            dq = _dot_nt(dmixed, pw_ref[g].astype(BF16))
            ext_dq[pl.ds(0, tm), lanes] = dq
            acc = _window_sum(ext_dq[:, lanes], win, ahead=True)[0:tm, :]
            dp.append(acc * (1.0 / win) - dq)

        dz = jnp.concatenate([db, du * v, du * c] + dp, axis=1).astype(BF16)
        acc_win[...] += _dot_tn(a, dz)
        dh0, dg1 = _rms_bwd(hat0, rstd0, g1, _dot_nt(dz, win_ref[...]))
        dg1_ref[...] += dg1
        gx_ref[...] = dh1v + dh0

        ext_dyc[pl.ds(tm, HALO), :] = ext_dyc[pl.ds(0, HALO), :]
        ext_dq[pl.ds(tm, HALO), :] = ext_dq[pl.ds(0, HALO), :]

        @pl.when(jr == 0)
        def _():
            ext_dyc[pl.ds(tm - HALO, HALO), :] = jnp.zeros((HALO, cw), F32)
            ext_dq[pl.ds(tm - HALO, HALO), :] = jnp.zeros((HALO, cw), F32)
            du_m = (conv_ref[pl.ds(1, 1), :] * ext_dyc[pl.ds(tm - HALO + 1, HALO), :]
                    + conv_ref[pl.ds(0, 1), :] * ext_dyc[pl.ds(tm - HALO + 2, HALO), :])
            dp_m = []
            for g, win in enumerate(POOL_WINDOWS):
                lanes = pl.ds(POOL_GROUP * g, POOL_GROUP)
                acc = ext_dq[pl.ds(tm - HALO + 1, HALO), lanes]
                for k in range(2, win):
                    acc = acc + ext_dq[pl.ds(tm - HALO + k, HALO), lanes]
                dp_m.append(acc * (1.0 / win))
            dz_m = jnp.concatenate([jnp.zeros((HALO, cw), F32), du_m * zm[:, 2 * cw:3 * cw], du_m * zm[:, cw:2 * cw]] + dp_m,
                                   axis=1).astype(BF16)
            acc_win[...] += _dot_tn(am_ref[...], dz_m)
            hat_m, rstd_m = _rms_stats(meta_ref[...])
            dmeta, dg1_m = _rms_bwd(hat_m, rstd_m, g1, _dot_nt(dz_m, win_ref[...]))
            dg1_ref[...] += dg1_m
            dmeta_ref[...] += dmeta

        @pl.when((b_id == n_seq - 1) & (j == nj - 1))
        def _():
            pieces = [(acc_win, zs * i, dwin_ref.at[i]) for i in range(N_DEV)]
            pieces += [(acc_wout, zs * i, dwout_ref.at[:, pl.ds(zs * i, zs)]) for i in range(d // zs)]
            copies = []
            for k, (acc, col, dst) in enumerate(pieces):
                if k >= 2:
                    copies[k - 2].wait()
                stage16[k % 2] = acc[:, pl.ds(col, zs)].astype(BF16)
                copies.append(pltpu.make_async_copy(stage16.at[k % 2], dst, sem.at[k % 2]))
                copies[k].start()
            copies[-2].wait()
            copies[-1].wait()
            finish()

    row = lambda b, j: (b * nj + nj - 1 - j, 0)
    prev = lambda b, j: (jnp.maximum((b * s + (nj - 1 - j) * tm) // HALO - 1, 0), 0)
    small = [g1.shape, g2.shape, conv_w.shape, pool_w.shape, pool_scale.shape, meta.shape]
    out = pl.pallas_call(
        body, name="mixer_backward", grid=(n_seq, nj),
        in_specs=[pl.BlockSpec((tm, d), row), pl.BlockSpec((tm, d), row), pl.BlockSpec((tm, d), row), pl.BlockSpec((tm, zw), row),
                  pl.BlockSpec((HALO, zw), prev), pl.BlockSpec((tm, cw), row), pl.BlockSpec((tm, cw), row), _const(meta.shape), _const(a_meta.shape), _const(z_meta.shape), _const(g1.shape),
                  _resident(w_in.shape), _const(conv_w.shape), _const(pool_w.shape), _const(pool_scale.shape), _resident(w_out.shape),
                  _const(g2.shape)] + [ANY] * nx,
        out_specs=[pl.BlockSpec((tm, d), row), ANY, ANY] + [_const(sh) for sh in small] + [ANY] * nx,
        out_shape=[jax.ShapeDtypeStruct((t, d), F32), jax.ShapeDtypeStruct((N_DEV, d, zs), BF16),
                   jax.ShapeDtypeStruct(w_out.shape, BF16)] + [jax.ShapeDtypeStruct(sh, F32) for sh in small]
        + [jax.ShapeDtypeStruct(a.shape, a.dtype) for a in to_exchange],
        scratch_shapes=[pltpu.VMEM((tm + HALO, cw), F32)] * 3
        + [pltpu.VMEM(w_in.shape, F32), pltpu.VMEM(w_out.shape, F32), pltpu.VMEM((2, d, zs), BF16),
           pltpu.SemaphoreType.DMA((2,))] + _exchange_sems(nx),
        compiler_params=_params("arbitrary", "arbitrary"),
    )(x2d, dh1, m, z, z, pooled, mixed, meta, a_meta, z_meta, g1, w_in, conv_w, pool_w, pool_scale, w_out, g2, *to_exchange)
    return out[:9], out[9:]


def _ffn_forward_backward(h1, target, g3, w_gate, w_up, w_down, g4):
    t, d = h1.shape
    ff = w_gate.shape[0]
    tm = min(TM_FFN, t)
    nt = t // tm
    chunks = [(s, min(FFN_CHUNK, ff - s)) for s in range(0, ff, FFN_CHUNK)]

    def body(h1_ref, h1pp_ref, tgt_ref, g3_ref, wg_ref, wu_ref, wd_ref, g4_ref,
             f_ref, act_ref, dd_ref, dgate_ref, dup_ref, dh1_ref, loss_ref, dg3_ref, dg4_ref, *slots):
        gate_s, up_s, dd_s, dh2_s, df_s = slots
        i = pl.program_id(0)

        def forward(slot):
            h1v = h1_ref[...]
            hat, _ = _rms_stats(h1v)
            f = (hat * g3_ref[...]).astype(BF16)
            f_ref[...] = f
            s, n = chunks[0]
            gate, up = _dot_nt(f_ref[...], wg_ref[pl.ds(s, n), :]), _dot_nt(f_ref[...], wu_ref[pl.ds(s, n), :])
            yield
            down = None
            for k, (s, n) in enumerate(chunks):
                gate_s.at[slot][:, pl.ds(s, n)] = gate.astype(BF16)
                up_s.at[slot][:, pl.ds(s, n)] = up.astype(BF16)
                act = (gate * jax.nn.sigmoid(gate) * up).astype(BF16)
                act_ref[:, pl.ds(s, n)] = act
                if k + 1 < len(chunks):
                    s1, n1 = chunks[k + 1]
                    gate, up = _dot_nt(f_ref[...], wg_ref[pl.ds(s1, n1), :]), _dot_nt(f_ref[...], wu_ref[pl.ds(s1, n1), :])
                yield
                part = _dot(act_ref[:, pl.ds(s, n)], wd_ref[pl.ds(s, n), :])
                down = part if down is None else down + part
                yield
            d_hat, d_rstd = _rms_stats(down)
            g4 = g4_ref[...]
            err = h1v + d_hat * g4 - tgt_ref[...]
            loss_ref[...] += jnp.sum(err * err) * (0.5 / d)
            dh2 = err * (1.0 / d)
            dh2_s.at[slot][...] = dh2
            dd, dg4 = _rms_bwd(d_hat, d_rstd, g4, dh2)
            dg4_ref[...] += dg4
            dd = dd.astype(BF16)
            dd_ref[...] = dd
            dd_s.at[slot][...] = dd

        def backward(slot):
            s, n = chunks[0]
            dact = _dot_nt(dd_s.at[slot][...], wd_ref[pl.ds(s, n), :])
            yield
            df = None
            for k, (s, n) in enumerate(chunks):
                gate = gate_s.at[slot][:, pl.ds(s, n)].astype(F32)
                up = up_s.at[slot][:, pl.ds(s, n)].astype(F32)
                sig = jax.nn.sigmoid(gate)
                dup = (dact * (gate * sig)).astype(BF16)
                dgate = (dact * up * (sig * (1.0 + gate * (1.0 - sig)))).astype(BF16)
                dup_ref[:, pl.ds(s, n)] = dup
                dgate_ref[:, pl.ds(s, n)] = dgate
                if k + 1 < len(chunks):
                    s1, n1 = chunks[k + 1]
                    dact = _dot_nt(dd_s.at[slot][...], wd_ref[pl.ds(s1, n1), :])
                yield
                part = _dot(dgate_ref[:, pl.ds(s, n)], wg_ref[pl.ds(s, n), :]) + _dot(dup_ref[:, pl.ds(s, n)], wu_ref[pl.ds(s, n), :])
                df = part if df is None else df + part
                yield
            df_s.at[slot][...] = df

        def last(slot):
            hat, rstd = _rms_stats(h1pp_ref[...])
            dh1, dg3 = _rms_bwd(hat, rstd, g3_ref[...], df_s.at[slot][...])
            dg3_ref[...] += dg3
            dh1_ref[...] = dh2_s.at[slot][...] + dh1

        def emit(parity, with_forward, with_backward, with_last):
            fwd = forward(parity) if with_forward else iter(())
            bwd = backward(1 - parity) if with_backward else iter(())
            next(fwd, None)
            if with_last:
                last(parity)
            for _ in range(FFN_BACKWARD_LAG):
                next(fwd, None)
            alive = True
            while alive:
                alive = next(bwd, True) is None
                alive = (next(fwd, True) is None) or alive

        @pl.when(i == 0)
        def _():
            for r in (loss_ref, dg3_ref, dg4_ref, *slots):
                r[...] = jnp.zeros_like(r)

        @pl.when(i < nt)
        def _():
            emit(i % 2, True, True, True)

        @pl.when(i == nt)
        def _():
            emit(nt % 2, False, True, True)

        @pl.when(i == nt + 1)
        def _():
            emit((nt + 1) % 2, False, False, True)

    cur = lambda i: (jnp.minimum(i, nt - 1), 0)
    prev = lambda i: (jnp.clip(i - 1, 0, nt - 1), 0)
    prev2 = lambda i: (jnp.clip(i - 2, 0, nt - 1), 0)
    return pl.pallas_call(
        body, name="ffn_forward_backward", grid=(nt + 2,),
        in_specs=[pl.BlockSpec((tm, d), cur), pl.BlockSpec((tm, d), prev2), pl.BlockSpec((tm, d), cur), _const(g3.shape),
                  _resident(w_gate.shape), _resident(w_up.shape), _resident(w_down.shape), _const(g4.shape)],
        out_specs=[pl.BlockSpec((tm, d), cur), pl.BlockSpec((tm, ff), cur), pl.BlockSpec((tm, d), cur), pl.BlockSpec((tm, ff), prev),
                   pl.BlockSpec((tm, ff), prev), pl.BlockSpec((tm, d), prev2), _const((8, 128)), _const(g3.shape), _const(g4.shape)],
        out_shape=[jax.ShapeDtypeStruct((t, d), BF16), jax.ShapeDtypeStruct((t, ff), BF16), jax.ShapeDtypeStruct((t, d), BF16),
                   jax.ShapeDtypeStruct((t, ff), BF16), jax.ShapeDtypeStruct((t, ff), BF16), jax.ShapeDtypeStruct((t, d), F32),
                   jax.ShapeDtypeStruct((8, 128), F32), jax.ShapeDtypeStruct(g3.shape, F32), jax.ShapeDtypeStruct(g4.shape, F32)],
        scratch_shapes=[pltpu.VMEM((2, tm, ff), BF16)] * 2 + [pltpu.VMEM((2, tm, d), BF16)] + [pltpu.VMEM((2, tm, d), F32)] * 2,
        compiler_params=_params("arbitrary"),
    )(h1, h1, target, g3, w_gate, w_up, w_down, g4)


def _ffn_weight_grads(f, dd, dgate, dup, act):
    t, d = f.shape
    ff = dgate.shape[1]
    tm = min(TM_WGRAD, t)
    nt = t // tm
    fc = ff // FF_CHUNKS

    def body(f_ref, dd_ref, dgate_ref, dup_ref, act_ref, dwg_ref, dwu_ref, dwd_ref, acc_g, acc_u, acc_d, stage, sem):
        c, i = pl.program_id(0), pl.program_id(1)

        @pl.when(i == 0)
        def _():
            acc_g[...] = jnp.zeros_like(acc_g)
            acc_u[...] = jnp.zeros_like(acc_u)
            acc_d[...] = jnp.zeros_like(acc_d)

        fv = f_ref[...]
        acc_g[...] += _dot_tn(fv, dgate_ref[...])
        acc_u[...] += _dot_tn(fv, dup_ref[...])
        acc_d[...] += _dot_tn(act_ref[...], dd_ref[...])

        @pl.when(i == nt - 1)
        def _():
            rows = pl.ds(pl.multiple_of(c * fc, 16), fc)
            copies = []
            for k, (acc, out, transposed) in enumerate(((acc_d, dwd_ref, False), (acc_g, dwg_ref, True), (acc_u, dwu_ref, True))):
                if k >= 2:
                    copies[k - 2].wait()
                stage[k % 2] = (acc[...].T if transposed else acc[...]).astype(BF16)
                copies.append(pltpu.make_async_copy(stage.at[k % 2], out.at[rows, :], sem.at[k % 2]))
                copies[k].start()
            copies[-2].wait()
            copies[-1].wait()

    row = lambda c, i: (i, 0)
    col = lambda c, i: (i, c)
    return pl.pallas_call(
        body, name="ffn_weight_grads", grid=(FF_CHUNKS, nt),
        in_specs=[pl.BlockSpec((tm, d), row), pl.BlockSpec((tm, d), row), pl.BlockSpec((tm, fc), col), pl.BlockSpec((tm, fc), col),
                  pl.BlockSpec((tm, fc), col)],
        out_specs=[ANY, ANY, ANY],
        out_shape=[jax.ShapeDtypeStruct((ff, d), BF16)] * 3,
        scratch_shapes=[pltpu.VMEM((d, fc), F32), pltpu.VMEM((d, fc), F32), pltpu.VMEM((fc, d), F32), pltpu.VMEM((2, fc, d), BF16),
                        pltpu.SemaphoreType.DMA((2,))],
        compiler_params=_params("arbitrary", "arbitrary"),
    )(f, dd, dgate, dup, act)


def _adamw(w, g, m, v):
    m = ADAM_B1 * m + (1.0 - ADAM_B1) * g
    v = ADAM_B2 * v + (1.0 - ADAM_B2) * (g * g)
    m_hat = m / (1.0 - ADAM_B1 ** ADAM_STEP)
    v_hat = v / (1.0 - ADAM_B2 ** ADAM_STEP)
    return -ADAM_LR * (m_hat / (jnp.sqrt(v_hat) + ADAM_EPS) + ADAM_WD * w), m, v


def _sum_slabs(ref):
    total = ref[0].astype(F32)
    for i in range(1, N_DEV):
        total = total + ref[i].astype(F32)
    return total


def _adamw_rows(r, c):
    tr = r
    for cand in range(8, r, 8):
        if r % cand == 0 and cand * c <= ADAMW_BLOCK_ELEMS:
            tr = cand
    return r if r * c <= ADAMW_BLOCK_ELEMS else tr


def _reduce_adamw_carrying(parts, ws, ms, vs, to_exchange, whole):
    k, nx = len(ws), len(to_exchange)
    r, c = ws[0].shape
    tr = _adamw_rows(r, c)
    steps = r // tr

    def body(*refs):
        p_refs, w_refs, m_refs, v_refs = (refs[a * k:(a + 1) * k] for a in range(4))
        sent, outs = refs[4 * k:4 * k + nx], refs[4 * k + nx:8 * k + nx]
        landed, sems = refs[8 * k + nx:8 * k + 2 * nx], refs[8 * k + 2 * nx:]
        start, finish = _exchange_ops(sent, landed, whole, *sems)
        pl.when(pl.program_id(0) == 0)(start)
        for a in range(k):
            g = _sum_slabs(p_refs[a])
            outs[4 * a][...] = g
            outs[4 * a + 1][...], outs[4 * a + 2][...], outs[4 * a + 3][...] = _adamw(w_refs[a][...], g, m_refs[a][...], v_refs[a][...])
        pl.when(pl.program_id(0) == steps - 1)(finish)

    blk = pl.BlockSpec((tr, c), lambda i: (i, 0))
    out = pl.pallas_call(
        body, name="adamw_ffn_exchange_rest", grid=(steps,),
        in_specs=[pl.BlockSpec((N_DEV, tr, c), lambda i: (0, i, 0))] * k + [blk] * (3 * k) + [ANY] * nx,
        out_specs=[blk] * (4 * k) + [ANY] * nx,
        out_shape=[jax.ShapeDtypeStruct((r, c), F32)] * (4 * k)
        + [jax.ShapeDtypeStruct((N_DEV, *a.shape) if w else a.shape, a.dtype) for a, w in zip(to_exchange, whole)],
        scratch_shapes=_exchange_sems(nx),
        compiler_params=_params("arbitrary"),
    )(*parts, *ws, *ms, *vs, *to_exchange)
    return [tuple(out[4 * a:4 * a + 4]) for a in range(k)], out[4 * k:]


def _reduce_adamw(parts, w, m, v, name):
    r, c = w.shape
    tr = _adamw_rows(r, c)

    def body(p_ref, w_ref, m_ref, v_ref, g_out, d_out, m_out, v_out):
        g = _sum_slabs(p_ref)
        g_out[...] = g
        d_out[...], m_out[...], v_out[...] = _adamw(w_ref[...], g, m_ref[...], v_ref[...])

    blk = pl.BlockSpec((tr, c), lambda i: (i, 0))
    return pl.pallas_call(
        body, name=name, grid=(r // tr,),
        in_specs=[pl.BlockSpec((N_DEV, tr, c), lambda i: (0, i, 0)), blk, blk, blk],
        out_specs=[blk] * 4, out_shape=[jax.ShapeDtypeStruct((r, c), F32)] * 4,
        compiler_params=_params("arbitrary"),
    )(parts, w, m, v)


def _reduce_adamw_small(parts, ws, ms, vs, loss_parts):
    n = len(parts)

    def body(*refs):
        p_refs, w_refs, m_refs, v_refs = (refs[k * n:(k + 1) * n] for k in range(4))
        outs = refs[4 * n + 1:]
        outs[4 * n][...] = _sum_slabs(refs[4 * n])
        for a in range(n):
            g = _sum_slabs(p_refs[a])
            outs[4 * a][...] = g
            outs[4 * a + 1][...], outs[4 * a + 2][...], outs[4 * a + 3][...] = _adamw(w_refs[a][...], g, m_refs[a][...], v_refs[a][...])

    out = pl.pallas_call(
        body, name="adamw_replicated",
        out_shape=[jax.ShapeDtypeStruct(w.shape, F32) for w in ws for _ in range(4)] + [jax.ShapeDtypeStruct(loss_parts.shape[1:], F32)],
        compiler_params=pltpu.CompilerParams(vmem_limit_bytes=VMEM_LIMIT_BYTES),
    )(*parts, *ws, *ms, *vs, loss_parts)
    return [tuple(out[4 * a:4 * a + 4]) for a in range(n)], out[4 * n]


def kernel(x, meta_tokens, norm_mix_pre, w_in, conv_w, pool_w, pool_scale, w_out, norm_mix_post, norm_ffn_pre, w_gate, w_up, w_down, norm_ffn_post, loss_target, m_meta_tokens, m_norm_mix_pre, m_w_in, m_conv_w, m_pool_w, m_pool_scale, m_w_out, m_norm_mix_post, m_norm_ffn_pre, m_w_gate, m_w_up, m_w_down, m_norm_ffn_post, v_meta_tokens, v_norm_mix_pre, v_w_in, v_conv_w, v_pool_w, v_pool_scale, v_w_out, v_norm_mix_post, v_norm_ffn_pre, v_w_gate, v_w_up, v_w_down, v_norm_ffn_post):
    n_seq, seq, d = x.shape
    x2d = x.reshape(n_seq * seq, d)
    target = loss_target.reshape(n_seq * seq, d)

    t_ = lambda a: jnp.swapaxes(a[0], 0, 1)
    pw, ps = pool_w[0], pool_scale

    (h1, z, m, pooled, mixed), (win_b, wout_b, meta, conv, a_meta, z_meta), ffn_slabs = _gather_and_mixer_forward(
        x2d, [w_in[0], w_out[0], meta_tokens, conv_w[0]], [t_(w_gate), t_(w_up), w_down[0]], norm_mix_pre, pw, ps, norm_mix_post, n_seq)
    wg_b, wu_b, wd_b = (s.reshape(-1, d) for s in ffn_slabs)
    f, act, dd, dgate, dup, dh1, loss_sum, dg3, dg4 = _ffn_forward_backward(h1, target, norm_ffn_pre, wg_b, wu_b, wd_b, norm_ffn_post)
    ffn_grads = _ffn_weight_grads(f, dd, dgate, dup, act)
    (gx, dwin, dwout, dg1, dg2, dconv, dpw, dps, dmeta), ffn_parts = _mixer_backward(
        x2d, dh1, m, z, pooled, mixed, meta, a_meta, z_meta, norm_mix_pre, win_b, conv, pw, ps, wout_b, norm_mix_post, n_seq,
        [g.reshape(N_DEV, -1, d) for g in ffn_grads])

    dmeta_s = jnp.transpose(dmeta.reshape(N_META, N_DEV, -1), (1, 0, 2))
    dconv_s = jnp.transpose(dconv.reshape(CONV_WIDTH, N_DEV, -1), (1, 0, 2))
    ffn_res, last = _reduce_adamw_carrying(
        ffn_parts, [t_(w_gate), t_(w_up), w_down[0]], [t_(m_w_gate), t_(m_w_up), m_w_down[0]], [t_(v_w_gate), t_(v_w_up), v_w_down[0]],
        [dwin, dwout.reshape(N_DEV, -1, d), dmeta_s, dconv_s, dg1, dg2, dg3, dg4, dpw, dps, loss_sum], [False] * 4 + [True] * 7)
    replicated = last[4:10]

    names = ["meta_tokens", "norm_mix_pre", "w_in", "conv_w", "pool_w", "pool_scale", "w_out", "norm_mix_post", "norm_ffn_pre", "w_gate",
             "w_up", "w_down", "norm_ffn_post"]
    res = {"w_gate": tuple(jnp.swapaxes(o, 0, 1)[None] for o in ffn_res[0]),
           "w_up": tuple(jnp.swapaxes(o, 0, 1)[None] for o in ffn_res[1]), "w_down": tuple(o[None] for o in ffn_res[2])}
    for nm, parts, w, m_, v_ in (("w_in", last[0], w_in, m_w_in, v_w_in), ("w_out", last[1], w_out, m_w_out, v_w_out),
                                 ("conv_w", last[3], conv_w, m_conv_w, v_conv_w)):
        res[nm] = tuple(o[None] for o in _reduce_adamw(parts, w[0], m_[0], v_[0], "adamw_" + nm))
    res["meta_tokens"] = tuple(_reduce_adamw(last[2], meta_tokens, m_meta_tokens, v_meta_tokens, "adamw_meta_tokens"))
    small, loss = _reduce_adamw_small(
        replicated, [norm_mix_pre, norm_mix_post, norm_ffn_pre, norm_ffn_post, pool_w[0], pool_scale],
        [m_norm_mix_pre, m_norm_mix_post, m_norm_ffn_pre, m_norm_ffn_post, m_pool_w[0], m_pool_scale],
        [v_norm_mix_pre, v_norm_mix_post, v_norm_ffn_pre, v_norm_ffn_post, v_pool_w[0], v_pool_scale], last[10])
    for nm, r in zip(["norm_mix_pre", "norm_mix_post", "norm_ffn_pre", "norm_ffn_post", "pool_w", "pool_scale"], small):
        res[nm] = tuple(o[None] for o in r) if nm == "pool_w" else r

    return (loss[0, 0], gx.reshape(n_seq, seq, d), *[res[nm][0] for nm in names], *[res[nm][1] for nm in names],
            *[res[nm][2] for nm in names], *[res[nm][3] for nm in names])
```

```python
import functools

import jax
import jax.numpy as jnp
from jax import lax
from jax.experimental import pallas as pl
from jax.experimental.pallas import tpu as pltpu

F32, BF16 = jnp.float32, jnp.bfloat16
RMS_EPS = 1e-6
N_META = 16
CONV_WIDTH = 3
POOL_WINDOWS = (2, 4, 8, 16)
POOL_GROUP = 128
HALO = 16
N_DEV = 8
MESH_AXES = ("x", "y", "c")
MESH = pl.DeviceIdType.MESH
VMEM_LIMIT_BYTES = 56 * 1024 * 1024
ADAMW_BLOCK_ELEMS = 64 * 1024
TM_MIX = 512
TM_FFN = 256
FFN_CHUNK = 512
FFN_BACKWARD_LAG = 2
TM_WGRAD = 512
FF_CHUNKS = 2

ADAM_LR, ADAM_B1, ADAM_B2, ADAM_EPS, ADAM_WD, ADAM_STEP = 0.001, 0.9, 0.999, 1e-08, 0.01, 10


def _dot(a, b):
    return jnp.dot(a, b, preferred_element_type=F32)


def _dot_nt(a, b):
    return lax.dot_general(a, b, (((1,), (1,)), ((), ())), preferred_element_type=F32)


def _dot_tn(a, b):
    return lax.dot_general(a, b, (((0,), (0,)), ((), ())), preferred_element_type=F32)


def _rms_stats(h):
    rstd = lax.rsqrt(jnp.mean(h * h, axis=-1, keepdims=True) + RMS_EPS)
    return h * rstd, rstd


def _rms_bwd(hat, rstd, g, dy):
    gdy = dy * g
    proj = jnp.mean(gdy * hat, axis=-1, keepdims=True)
    return rstd * (gdy - hat * proj), jnp.sum(dy * hat, axis=0, keepdims=True)


def _params(*semantics):
    return pltpu.CompilerParams(dimension_semantics=semantics or None, vmem_limit_bytes=VMEM_LIMIT_BYTES)


def _resident(shape):
    zeros = (0,) * len(shape)
    return pl.BlockSpec(shape, lambda *_: zeros, pipeline_mode=pl.Buffered(1))


def _const(shape):
    zeros = (0,) * len(shape)
    return pl.BlockSpec(shape, lambda *_: zeros)


ANY = pl.BlockSpec(memory_space=pl.ANY)


def _my_place():
    x, y, c = (lax.axis_index(a) for a in MESH_AXES)
    return x, y, c


def _exchange_sems(n):
    return [pltpu.SemaphoreType.DMA((n, N_DEV - 1)), pltpu.SemaphoreType.DMA((n, N_DEV - 1)), pltpu.SemaphoreType.DMA((n,))]


def _gather_ops(srcs, outs, send_sems, recv_sems, local_sems, core_major=False):
    n = len(srcs)
    x, y, c = _my_place()
    me, sibling = (x, y, c), (x, y, 1 - c)
    chips = [(1 - x, y), (x, 1 - y), (1 - x, 1 - y)]

    def slab(px, py, pc):
        return 4 * pc + 2 * px + py if core_major else 4 * px + 2 * py + pc

    def copy(a, k, block, to, src=None):
        dst = outs[a].at[slab(*block)]
        return pltpu.make_async_remote_copy(
            src_ref=dst if src is None else src, dst_ref=dst, send_sem=send_sems.at[a, k], recv_sem=recv_sems.at[a, k],
            device_id=to, device_id_type=MESH)

    def mine(a):
        return pltpu.make_async_copy(srcs[a], outs[a].at[slab(*me)], local_sems.at[a])

    def first(a):
        return [copy(a, 0, me, sibling, src=srcs[a])] + [copy(a, 1 + j, me, (*chip, c), src=srcs[a]) for j, chip in enumerate(chips)]

    def passed(a, j):
        return copy(a, 4 + j, (*chips[j], c), sibling)

    def start():
        for a in range(n):
            mine(a).start()
            for cp in first(a):
                cp.start()

    def forward():
        for j, chip in enumerate(chips):
            for a in range(n):
                copy(a, 1 + j, (*chip, c), me).wait_recv()
                passed(a, j).start()

    def finish():
        for a in range(n):
            copy(a, 0, sibling, me).wait_recv()
            for j, chip in enumerate(chips):
                copy(a, 4 + j, (*chip, 1 - c), me).wait_recv()
        for a in range(n):
            for cp in first(a) + [passed(a, j) for j in range(len(chips))]:
                cp.wait_send()
            mine(a).wait()

    return start, forward, finish


def _exchange_ops(ins, outs, whole, send_sems, recv_sems, local_sems):
    n = len(ins)
    x, y, c = _my_place()
    me = 4 * x + 2 * y + c

    def src(a, i):
        return ins[a] if whole[a] else ins[a].at[i]

    def mine(a):
        return pltpu.make_async_copy(src(a, me), outs[a].at[me], local_sems.at[a])

    def send(a, k):
        to = (me + k) % N_DEV
        return pltpu.make_async_remote_copy(
            src_ref=src(a, to), dst_ref=outs[a].at[me], send_sem=send_sems.at[a, k - 1], recv_sem=recv_sems.at[a, k - 1],
            device_id=(to // 4, (to // 2) % 2, to % 2), device_id_type=MESH)

    def landed(a, k):
        frm = (me + N_DEV - k) % N_DEV
        return pltpu.make_async_remote_copy(
            src_ref=src(a, frm), dst_ref=outs[a].at[frm], send_sem=send_sems.at[a, k - 1], recv_sem=recv_sems.at[a, k - 1],
            device_id=(x, y, c), device_id_type=MESH)

    def start():
        for a in range(n):
            mine(a).start()
            for k in range(1, N_DEV):
                send(a, k).start()

    def finish():
        for a in range(n):
            for k in range(1, N_DEV):
                landed(a, k).wait_recv()
        for a in range(n):
            for k in range(1, N_DEV):
                send(a, k).wait_send()
            mine(a).wait()

    return start, finish


def _core_exchange_sems(n):
    return [pltpu.SemaphoreType.DMA((n, 4)), pltpu.SemaphoreType.DMA((n, N_DEV)), pltpu.SemaphoreType.DMA((n,))]


def _core_exchange_ops(ins, outs, to_core, send_sems, recv_sems, local_sems):
    n = len(ins)
    x, y, c = _my_place()
    me = 4 * x + 2 * y + c
    others = [(0, 1), (1, 0), (1, 1)]

    def slab(a, p):
        if len(ins[a].shape) == len(outs[a].shape):
            return ins[a].at[p]
        rows = outs[a].shape[1]
        return ins[a].at[pl.ds(pl.multiple_of(p * rows, 16), rows), :]

    def send(a, dx, dy):
        tx, ty = (x + dx) % 2, (y + dy) % 2
        return pltpu.make_async_remote_copy(
            src_ref=slab(a, 4 * to_core + 2 * tx + ty), dst_ref=outs[a].at[me], send_sem=send_sems.at[a, 2 * dx + dy],
            recv_sem=recv_sems.at[a, 2 * (2 * dx + dy) + c], device_id=(tx, ty, to_core), device_id_type=MESH)

    def mine(a):
        return pltpu.make_async_copy(slab(a, 4 * to_core + 2 * x + y), outs[a].at[me], local_sems.at[a])

    def landed(a, dx, dy, sc):
        frm = 4 * ((x + dx) % 2) + 2 * ((y + dy) % 2) + sc
        return pltpu.make_async_remote_copy(
            src_ref=slab(a, 0), dst_ref=outs[a].at[frm], send_sem=send_sems.at[a, 0], recv_sem=recv_sems.at[a, 2 * (2 * dx + dy) + sc],
            device_id=(x, y, c), device_id_type=MESH)

    def start():
        for a in range(n):
            for dx, dy in others:
                send(a, dx, dy).start()
            pl.when(c == to_core)(mine(a).start)
            pl.when(c != to_core)(send(a, 0, 0).start)

    def finish():
        @pl.when(c == to_core)
        def _():
            for a in range(n):
                for dx, dy in [(0, 0)] + others:
                    for sc in (0, 1):
                        if (dx, dy, sc) != (0, 0, to_core):
                            landed(a, dx, dy, sc).wait_recv()
            for a in range(n):
                mine(a).wait()

        @pl.when(c != to_core)
        def _():
            for a in range(n):
                send(a, 0, 0).wait_send()

        for a in range(n):
            for dx, dy in others:
                send(a, dx, dy).wait_send()

    return start, finish


def _gather_first_weights(gathered, dtypes, cast_only):
    n, k = len(gathered), len(cast_only)

    def body(*refs):
        ins, casts_in = refs[:n], refs[n:n + k]
        outs, casts_out = refs[n + k:2 * n + k], refs[2 * n + k:2 * n + 2 * k]
        stages = refs[2 * n + 2 * k:3 * n + 2 * k]
        start, forward, finish = _gather_ops(stages, outs, *refs[3 * n + 2 * k:])
        for a in range(n):
            stages[a][...] = ins[a][...].astype(stages[a].dtype)
        start()
        for a in range(k):
            casts_out[a][...] = casts_in[a][...].astype(BF16)
        forward()
        finish()

    vmem = pl.BlockSpec(memory_space=pltpu.VMEM)
    out = pl.pallas_call(
        body, name="gather_first_weights",
        out_shape=[jax.ShapeDtypeStruct((N_DEV, *s.shape), d) for s, d in zip(gathered, dtypes)]
        + [jax.ShapeDtypeStruct(s.shape, BF16) for s in cast_only],
        in_specs=[vmem] * (n + k), out_specs=[ANY] * n + [vmem] * k,
        scratch_shapes=[pltpu.VMEM(s.shape, d) for s, d in zip(gathered, dtypes)] + _exchange_sems(n),
        compiler_params=pltpu.CompilerParams(vmem_limit_bytes=VMEM_LIMIT_BYTES),
    )(*gathered, *cast_only)
    return out[:n], out[n:]


def _exchange(arrays, whole, name):
    n = len(arrays)

    def body(*refs):
        start, finish = _exchange_ops(refs[:n], refs[n:2 * n], whole, *refs[2 * n:])
        start()
        finish()

    return pl.pallas_call(
        body, name=name,
        out_shape=[jax.ShapeDtypeStruct((N_DEV, *a.shape) if w else a.shape, a.dtype) for a, w in zip(arrays, whole)],
        in_specs=[ANY] * n, out_specs=[ANY] * n, scratch_shapes=_exchange_sems(n),
    )(*arrays)


def _columns_from_slabs(slabs):
    def body(*refs):
        k = len(refs) // 2
        for src, dst in zip(refs[:k], refs[k:]):
            n = src.shape[2]
            for i in range(N_DEV):
                dst[:, pl.ds(n * i, n)] = src[i]

    return pl.pallas_call(
        body, name="columns_from_slabs",
        out_shape=[jax.ShapeDtypeStruct((s.shape[1], N_DEV * s.shape[2]), s.dtype) for s in slabs],
        compiler_params=pltpu.CompilerParams(vmem_limit_bytes=VMEM_LIMIT_BYTES),
    )(*slabs)


def _window_sum(x, win, ahead):
    n = x.shape[0]
    span = 1
    while span < win:
        x = x + pltpu.roll(x, n - span if ahead else span, 0)
        span *= 2
    return x


def _conv_branch(z, ext_u, conv_ref, tm):
    c_w = z.shape[1] // 4
    b, c, v = z[:, :c_w], z[:, c_w:2 * c_w], z[:, 2 * c_w:3 * c_w]
    u = c * v
    ext_u[pl.ds(HALO, tm), :] = u
    u1 = ext_u[pl.ds(HALO - 1, tm), :]
    u2 = ext_u[pl.ds(HALO - 2, tm), :]
    yc = conv_ref[pl.ds(2, 1), :] * u + conv_ref[pl.ds(1, 1), :] * u1 + conv_ref[pl.ds(0, 1), :] * u2
    return b, c, v, u, u1, u2, yc


def _pool_branch(p, ext_p, pool_w_ref, tm):
    ext_p[pl.ds(HALO, tm), :] = p
    pooled, mixed = [], []
    for g, win in enumerate(POOL_WINDOWS):
        s = _window_sum(ext_p[:, pl.ds(POOL_GROUP * g, POOL_GROUP)], win, ahead=False)[HALO:HALO + tm, :]
        pooled.append((s * (1.0 / win) - p[:, POOL_GROUP * g:POOL_GROUP * (g + 1)]).astype(BF16))
        mixed.append(_dot(pooled[-1], pool_w_ref[g].astype(BF16)))
    return pooled, mixed


def _meta_forward(meta, g1, w_in):
    def body(meta_ref, g1_ref, w_ref, a_ref, z_ref):
        hat, _ = _rms_stats(meta_ref[...])
        a = (hat * g1_ref[...]).astype(BF16)
        a_ref[...] = a
        z_ref[...] = _dot(a, w_ref[...])

    return pl.pallas_call(
        body, name="meta_forward",
        out_shape=[jax.ShapeDtypeStruct(meta.shape, BF16), jax.ShapeDtypeStruct((N_META, w_in.shape[1]), F32)],
        compiler_params=pltpu.CompilerParams(vmem_limit_bytes=VMEM_LIMIT_BYTES),
    )(meta, g1, w_in)


def _mixer_forward(x2d, z_meta, g1, w_in, conv_w, pool_w, pool_scale, w_out, g2, n_seq, to_gather):
    t, d = x2d.shape
    zw = w_in.shape[1]
    cw = zw // 4
    s = t // n_seq
    tm = min(TM_MIX, s)
    nj = s // tm
    ng = len(to_gather)

    def body(x_ref, zm_ref, g1_ref, win_ref, conv_ref, pw_ref, ps_ref, wout_ref, g2_ref, *rest):
        shards, (h1_ref, z_ref, m_ref, pooled_ref, mixed_ref), slabs = rest[:ng], rest[ng:ng + 5], rest[ng + 5:2 * ng + 5]
        ext_u, ext_p = rest[2 * ng + 5:2 * ng + 7]
        start, forward, finish = _gather_ops(shards, slabs, *rest[2 * ng + 7:])
        pl.when((pl.program_id(0) == 0) & (pl.program_id(1) == 0))(start)

        @pl.when(pl.program_id(1) == 0)
        def _():
            zm = zm_ref[...]
            ext_u[pl.ds(0, HALO), :] = zm[:, cw:2 * cw] * zm[:, 2 * cw:3 * cw]
            ext_p[pl.ds(0, HALO), :] = zm[:, 3 * cw:]

        h0 = x_ref[...]
        hat, _ = _rms_stats(h0)
        z = _dot((hat * g1_ref[...]).astype(BF16), win_ref[...])
        z_ref[...] = z.astype(BF16)
        b, _, _, _, _, _, yc = _conv_branch(z, ext_u, conv_ref, tm)
        pooled, mixed = _pool_branch(z[:, 3 * cw:], ext_p, pw_ref, tm)
        pooled_ref[...] = jnp.concatenate(pooled, axis=1)
        mixed_ref[...] = jnp.concatenate(mixed, axis=1).astype(BF16)
        ps = ps_ref[...]
        y = [b * yc] + [mixed[g] * ps[:, POOL_GROUP * g:POOL_GROUP * (g + 1)] for g in range(len(POOL_WINDOWS))]
        m = _dot(jnp.concatenate(y, axis=1).astype(BF16), wout_ref[...])
        m_ref[...] = m
        m_hat, _ = _rms_stats(m)
        h1_ref[...] = h0 + m_hat * g2_ref[...]
        ext_u[pl.ds(0, HALO), :] = ext_u[pl.ds(tm, HALO), :]
        ext_p[pl.ds(0, HALO), :] = ext_p[pl.ds(tm, HALO), :]

        @pl.when((pl.program_id(0) == n_seq - 1) & (pl.program_id(1) == nj - 1))
        def _():
            forward()
            finish()

    row = lambda b, j: (b * nj + j, 0)
    out = pl.pallas_call(
        body, name="mixer_forward", grid=(n_seq, nj),
        in_specs=[pl.BlockSpec((tm, d), row), _const(z_meta.shape), _const(g1.shape), _resident(w_in.shape), _const(conv_w.shape),
                  _const(pool_w.shape), _const(pool_scale.shape), _resident(w_out.shape), _const(g2.shape)] + [ANY] * ng,
        out_specs=[pl.BlockSpec((tm, d), row), pl.BlockSpec((tm, zw), row), pl.BlockSpec((tm, d), row), pl.BlockSpec((tm, cw), row),
                   pl.BlockSpec((tm, cw), row)] + [ANY] * ng,
        out_shape=[jax.ShapeDtypeStruct((t, d), F32), jax.ShapeDtypeStruct((t, zw), BF16), jax.ShapeDtypeStruct((t, d), F32),
                   jax.ShapeDtypeStruct((t, cw), BF16), jax.ShapeDtypeStruct((t, cw), BF16)]
        + [jax.ShapeDtypeStruct((N_DEV, *a.shape), a.dtype) for a in to_gather],
        scratch_shapes=[pltpu.VMEM((tm + HALO, cw), F32), pltpu.VMEM((tm + HALO, cw), F32)] + _exchange_sems(ng),
        compiler_params=_params("arbitrary", "arbitrary"),
    )(x2d, z_meta, g1, w_in, conv_w, pool_w, pool_scale, w_out, g2, *to_gather)
    return out[:5], out[5:]


def _gather_and_mixer_forward(x2d, mixer_shards, ffn_shards, g1, pool_w, pool_scale, g2, n_seq):
    t, d = x2d.shape
    zs, rs, ms, cs = mixer_shards[0].shape[1], mixer_shards[1].shape[0], mixer_shards[2].shape[1], mixer_shards[3].shape[1]
    zw, cw = N_DEV * zs, N_DEV * cs
    s = t // n_seq
    tm = min(TM_MIX, s)
    nj = s // tm
    n1, n2 = len(mixer_shards), len(ffn_shards)
    dtypes = [BF16, BF16, F32, F32] + [BF16] * n2
    shards = list(mixer_shards) + list(ffn_shards)

    def body(x_ref, *rest):
        shard_refs, (g1_ref, pw_ref, ps_ref, g2_ref), rest = rest[:n1 + n2], rest[n1 + n2:n1 + n2 + 4], rest[n1 + n2 + 4:]
        (h1_ref, z_ref, m_ref, pooled_ref, mixed_ref, win_o, wout_o, meta_o, conv_o, am_o, zm_o), rest = rest[:11], rest[11:]
        slabs, rest = rest[:n1 + n2], rest[n1 + n2:]
        stages, rest = rest[:n1 + n2], rest[n1 + n2:]
        win_v, wout_v, meta_v, conv_v, ext_u, ext_p, sem = rest[:7]
        first = _gather_ops(stages[:n1], slabs[:n1], *rest[7:10])
        later = _gather_ops(stages[n1:], slabs[n1:], *rest[10:13], core_major=True)

        @pl.when((pl.program_id(0) == 0) & (pl.program_id(1) == 0))
        def _():
            for src, dst in zip(shard_refs, stages):
                dst[...] = src[...].astype(dst.dtype)
            first[0]()
            later[0]()
            first[1]()
            first[2]()
            copies = [pltpu.make_async_copy(slabs[0].at[i], win_v.at[:, pl.ds(zs * i, zs)], sem.at[i]) for i in range(N_DEV)]
            copies += [pltpu.make_async_copy(slabs[1].at[i], wout_v.at[pl.ds(rs * i, rs), :], sem.at[N_DEV + i]) for i in range(N_DEV)]
            copies += [pltpu.make_async_copy(slabs[2], meta_v, sem.at[2 * N_DEV]), pltpu.make_async_copy(slabs[3], conv_v, sem.at[2 * N_DEV + 1])]
            for cp in copies:
                cp.start()
            for cp in copies:
                cp.wait()
            copies = [pltpu.make_async_copy(win_v, win_o, sem.at[0]), pltpu.make_async_copy(wout_v, wout_o, sem.at[1])]
            for cp in copies:
                cp.start()
            for i in range(N_DEV):
                meta_o[:, pl.ds(ms * i, ms)] = meta_v[i]
                conv_o[:, pl.ds(cs * i, cs)] = conv_v[i]
            hat, _ = _rms_stats(meta_o[...])
            a = (hat * g1_ref[...]).astype(BF16)
            am_o[...] = a
            zm_o[...] = _dot(a, win_v[...])
            for cp in copies:
                cp.wait()

        @pl.when(pl.program_id(1) == 0)
        def _():
            zm = zm_o[...]
            ext_u[pl.ds(0, HALO), :] = zm[:, cw:2 * cw] * zm[:, 2 * cw:3 * cw]
            ext_p[pl.ds(0, HALO), :] = zm[:, 3 * cw:]

        h0 = x_ref[...]
        hat, _ = _rms_stats(h0)
        z = _dot((hat * g1_ref[...]).astype(BF16), win_v[...])
        z_ref[...] = z.astype(BF16)
        b, _, _, _, _, _, yc = _conv_branch(z, ext_u, conv_o, tm)
        pooled, mixed = _pool_branch(z[:, 3 * cw:], ext_p, pw_ref, tm)
        pooled_ref[...] = jnp.concatenate(pooled, axis=1)
        mixed_ref[...] = jnp.concatenate(mixed, axis=1).astype(BF16)
        ps = ps_ref[...]
        y = [b * yc] + [mixed[g] * ps[:, POOL_GROUP * g:POOL_GROUP * (g + 1)] for g in range(len(POOL_WINDOWS))]
        m = _dot(jnp.concatenate(y, axis=1).astype(BF16), wout_v[...])
        m_ref[...] = m
        m_hat, _ = _rms_stats(m)
        h1_ref[...] = h0 + m_hat * g2_ref[...]
        ext_u[pl.ds(0, HALO), :] = ext_u[pl.ds(tm, HALO), :]
        ext_p[pl.ds(0, HALO), :] = ext_p[pl.ds(tm, HALO), :]

        @pl.when((pl.program_id(0) == n_seq - 1) & (pl.program_id(1) == nj - 1))
        def _():
            later[1]()
            later[2]()

    row = lambda b, j: (b * nj + j, 0)
    vmem = pl.BlockSpec(memory_space=pltpu.VMEM)
    small = [(N_META, d), (CONV_WIDTH, cw), (N_META, d), (N_META, zw)]
    out = pl.pallas_call(
        body, name="gather_and_mixer_forward", grid=(n_seq, nj),
        in_specs=[pl.BlockSpec((tm, d), row)] + [vmem] * (n1 + n2)
        + [_const(g1.shape), _const(pool_w.shape), _const(pool_scale.shape), _const(g2.shape)],
        out_specs=[pl.BlockSpec((tm, d), row), pl.BlockSpec((tm, zw), row), pl.BlockSpec((tm, d), row), pl.BlockSpec((tm, cw), row),
                   pl.BlockSpec((tm, cw), row), ANY, ANY] + [_const(sh) for sh in small] + [ANY] * (n1 + n2),
        out_shape=[jax.ShapeDtypeStruct((t, d), F32), jax.ShapeDtypeStruct((t, zw), BF16), jax.ShapeDtypeStruct((t, d), F32),
                   jax.ShapeDtypeStruct((t, cw), BF16), jax.ShapeDtypeStruct((t, cw), BF16),
                   jax.ShapeDtypeStruct((d, zw), BF16), jax.ShapeDtypeStruct((d, d), BF16),
                   jax.ShapeDtypeStruct(small[0], F32), jax.ShapeDtypeStruct(small[1], F32), jax.ShapeDtypeStruct(small[2], BF16),
                   jax.ShapeDtypeStruct(small[3], F32)]
        + [jax.ShapeDtypeStruct((N_DEV, *a.shape), dt) for a, dt in zip(shards, dtypes)],
        scratch_shapes=[pltpu.VMEM(a.shape, dt) for a, dt in zip(shards, dtypes)]
        + [pltpu.VMEM((d, zw), BF16), pltpu.VMEM((d, d), BF16), pltpu.VMEM((N_DEV, N_META, ms), F32),
           pltpu.VMEM((N_DEV, CONV_WIDTH, cs), F32), pltpu.VMEM((tm + HALO, cw), F32), pltpu.VMEM((tm + HALO, cw), F32),
           pltpu.SemaphoreType.DMA((2 * N_DEV + 2,))] + _exchange_sems(n1) + _exchange_sems(n2),
        compiler_params=_params("arbitrary", "arbitrary"),
    )(x2d, *shards, g1, pool_w, pool_scale, g2)
    return out[:5], out[5:11], out[11 + n1:]


def _mixer_backward(x2d, dh1, m, z, pooled, mixed, meta, a_meta, z_meta, g1, w_in, conv_w, pool_w, pool_scale, w_out, g2, n_seq,
                    to_exchange, landing):
    t, d = x2d.shape
    zw = w_in.shape[1]
    cw = zw // 4
    s = t // n_seq
    tm = min(TM_MIX, s)
    nj = s // tm
    n_groups = len(POOL_WINDOWS)
    zs = zw // N_DEV
    nx = len(to_exchange)
    n_in = 17

    def body(x_ref, dh1_ref, m_ref, z_ref, zprev_ref, pooled_ref, mixed_ref, meta_ref, am_ref, zm_ref, g1_ref, win_ref, conv_ref, pw_ref, ps_ref, wout_ref,
             g2_ref, *rest):
        sent, rest = rest[:nx], rest[2 * nx:]
        gx_ref, dwin_ref, dwout_ref, dg1_ref, dg2_ref, dconv_ref, dpw_ref, dps_ref, dmeta_ref = rest[:9]
        landed, rest = rest[9:9 + nx], rest[9 + nx:]
        ext_u, ext_dyc, ext_dq, acc_win, acc_wout, stage16, sem = rest[:7]
        start, finish = _core_exchange_ops(sent, landed, 1, *rest[7:])
        b_id, j = pl.program_id(0), pl.program_id(1)
        jr = nj - 1 - j
        pl.when((b_id == 0) & (j == 0))(start)

        @pl.when((b_id == 0) & (j == 0))
        def _():
            acc_win[...] = jnp.zeros_like(acc_win)
            acc_wout[...] = jnp.zeros_like(acc_wout)
            for r in (dg1_ref, dg2_ref, dconv_ref, dpw_ref, dps_ref, dmeta_ref):
                r[...] = jnp.zeros_like(r)

        @pl.when(j == 0)
        def _():
            ext_dyc[pl.ds(tm, HALO), :] = jnp.zeros((HALO, cw), F32)
            ext_dq[pl.ds(tm, HALO), :] = jnp.zeros((HALO, cw), F32)

        zm = zm_ref[...]
        halo = jnp.where(jr == 0, zm, zprev_ref[...].astype(F32))
        ext_u[pl.ds(0, HALO), :] = halo[:, cw:2 * cw] * halo[:, 2 * cw:3 * cw]

        h0 = x_ref[...]
        hat0, rstd0 = _rms_stats(h0)
        g1 = g1_ref[...]
        a = (hat0 * g1).astype(BF16)
        b, c, v, u, u1, u2, yc = _conv_branch(z_ref[...].astype(F32), ext_u, conv_ref, tm)
        mixed = [mixed_ref[:, pl.ds(POOL_GROUP * g, POOL_GROUP)].astype(F32) for g in range(n_groups)]
        ps = ps_ref[...]
        y = [b * yc] + [mixed[g] * ps[:, POOL_GROUP * g:POOL_GROUP * (g + 1)] for g in range(n_groups)]
        ycat = jnp.concatenate(y, axis=1).astype(BF16)

        dh1v = dh1_ref[...]
        m_hat, m_rstd = _rms_stats(m_ref[...])
        dm, dg2 = _rms_bwd(m_hat, m_rstd, g2_ref[...], dh1v)
        dg2_ref[...] += dg2
        dm = dm.astype(BF16)
        acc_wout[...] += _dot_tn(ycat, dm)
        dycat = _dot_nt(dm, wout_ref[...])

        dyconv = dycat[:, :cw]
        db = dyconv * yc
        dyc = dyconv * b
        ext_dyc[pl.ds(0, tm), :] = dyc
        du = (conv_ref[pl.ds(2, 1), :] * dyc + conv_ref[pl.ds(1, 1), :] * ext_dyc[pl.ds(1, tm), :]
              + conv_ref[pl.ds(0, 1), :] * ext_dyc[pl.ds(2, tm), :])
        dconv_ref[pl.ds(2, 1), :] += jnp.sum(dyc * u, axis=0, keepdims=True)
        dconv_ref[pl.ds(1, 1), :] += jnp.sum(dyc * u1, axis=0, keepdims=True)
        dconv_ref[pl.ds(0, 1), :] += jnp.sum(dyc * u2, axis=0, keepdims=True)

        dp = []
        for g, win in enumerate(POOL_WINDOWS):
            lanes = pl.ds(POOL_GROUP * g, POOL_GROUP)
            dypool = dycat[:, cw + POOL_GROUP * g:cw + POOL_GROUP * (g + 1)]
            dps_ref[:, lanes] += jnp.sum(dypool * mixed[g], axis=0, keepdims=True)
            dmixed = (dypool * ps[:, POOL_GROUP * g:POOL_GROUP * (g + 1)]).astype(BF16)
            dpw_ref[g] += _dot_tn(pooled_ref[:, lanes], dmixed)
            dq = _dot_nt(dmixed, pw_ref[g].astype(BF16))
            ext_dq[pl.ds(0, tm), lanes] = dq
            acc = _window_sum(ext_dq[:, lanes], win, ahead=True)[0:tm, :]
            dp.append(acc * (1.0 / win) - dq)

        dz = jnp.concatenate([db, du * v, du * c] + dp, axis=1).astype(BF16)
        acc_win[...] += _dot_tn(a, dz)
        dh0, dg1 = _rms_bwd(hat0, rstd0, g1, _dot_nt(dz, win_ref[...]))
        dg1_ref[...] += dg1
        gx_ref[...] = dh1v + dh0

        ext_dyc[pl.ds(tm, HALO), :] = ext_dyc[pl.ds(0, HALO), :]
        ext_dq[pl.ds(tm, HALO), :] = ext_dq[pl.ds(0, HALO), :]

        @pl.when(jr == 0)
        def _():
            ext_dyc[pl.ds(tm - HALO, HALO), :] = jnp.zeros((HALO, cw), F32)
            ext_dq[pl.ds(tm - HALO, HALO), :] = jnp.zeros((HALO, cw), F32)
            du_m = (conv_ref[pl.ds(1, 1), :] * ext_dyc[pl.ds(tm - HALO + 1, HALO), :]
                    + conv_ref[pl.ds(0, 1), :] * ext_dyc[pl.ds(tm - HALO + 2, HALO), :])
            dp_m = []
            for g, win in enumerate(POOL_WINDOWS):
                lanes = pl.ds(POOL_GROUP * g, POOL_GROUP)
                acc = ext_dq[pl.ds(tm - HALO + 1, HALO), lanes]
                for k in range(2, win):
                    acc = acc + ext_dq[pl.ds(tm - HALO + k, HALO), lanes]
                dp_m.append(acc * (1.0 / win))
            dz_m = jnp.concatenate([jnp.zeros((HALO, cw), F32), du_m * zm[:, 2 * cw:3 * cw], du_m * zm[:, cw:2 * cw]] + dp_m,
                                   axis=1).astype(BF16)
            acc_win[...] += _dot_tn(am_ref[...], dz_m)
            hat_m, rstd_m = _rms_stats(meta_ref[...])
            dmeta, dg1_m = _rms_bwd(hat_m, rstd_m, g1, _dot_nt(dz_m, win_ref[...]))
            dg1_ref[...] += dg1_m
            dmeta_ref[...] += dmeta

        @pl.when((b_id == n_seq - 1) & (j == nj - 1))
        def _():
            pieces = [(acc_win, zs * i, dwin_ref.at[i]) for i in range(N_DEV)]
            pieces += [(acc_wout, zs * i, dwout_ref.at[:, pl.ds(zs * i, zs)]) for i in range(d // zs)]
            copies = []
            for k, (acc, col, dst) in enumerate(pieces):
                if k >= 2:
                    copies[k - 2].wait()
                stage16[k % 2] = acc[:, pl.ds(col, zs)].astype(BF16)
                copies.append(pltpu.make_async_copy(stage16.at[k % 2], dst, sem.at[k % 2]))
                copies[k].start()
            copies[-2].wait()
            copies[-1].wait()
            finish()

    row = lambda b, j: (b * nj + nj - 1 - j, 0)
    prev = lambda b, j: (jnp.maximum((b * s + (nj - 1 - j) * tm) // HALO - 1, 0), 0)
    small = [g1.shape, g2.shape, conv_w.shape, pool_w.shape, pool_scale.shape, meta.shape]
    out = pl.pallas_call(
        body, name="mixer_backward", grid=(n_seq, nj),
        in_specs=[pl.BlockSpec((tm, d), row), pl.BlockSpec((tm, d), row), pl.BlockSpec((tm, d), row), pl.BlockSpec((tm, zw), row),
                  pl.BlockSpec((HALO, zw), prev), pl.BlockSpec((tm, cw), row), pl.BlockSpec((tm, cw), row), _const(meta.shape), _const(a_meta.shape), _const(z_meta.shape), _const(g1.shape),
                  _resident(w_in.shape), _const(conv_w.shape), _const(pool_w.shape), _const(pool_scale.shape), _resident(w_out.shape),
                  _const(g2.shape)] + [ANY] * (2 * nx),
        out_specs=[pl.BlockSpec((tm, d), row), ANY, ANY] + [_const(sh) for sh in small] + [ANY] * nx,
        out_shape=[jax.ShapeDtypeStruct((t, d), F32), jax.ShapeDtypeStruct((N_DEV, d, zs), BF16),
                   jax.ShapeDtypeStruct(w_out.shape, BF16)] + [jax.ShapeDtypeStruct(sh, F32) for sh in small]
        + [jax.ShapeDtypeStruct(a.shape, a.dtype) for a in landing],
        input_output_aliases={n_in + nx + k: 9 + k for k in range(nx)},
        scratch_shapes=[pltpu.VMEM((tm + HALO, cw), F32)] * 3
        + [pltpu.VMEM(w_in.shape, F32), pltpu.VMEM(w_out.shape, F32), pltpu.VMEM((2, d, zs), BF16),
           pltpu.SemaphoreType.DMA((2,))] + _core_exchange_sems(nx),
        compiler_params=_params("arbitrary", "arbitrary"),
    )(x2d, dh1, m, z, z, pooled, mixed, meta, a_meta, z_meta, g1, w_in, conv_w, pool_w, pool_scale, w_out, g2, *to_exchange, *landing)
    return out[:9], out[9:]


def _ffn_forward_backward(h1, target, g3, w_gate, w_up, w_down, g4):
    t, d = h1.shape
    ff = w_gate.shape[0]
    tm = min(TM_FFN, t)
    nt = t // tm
    chunks = [(s, min(FFN_CHUNK, ff - s)) for s in range(0, ff, FFN_CHUNK)]

    def body(h1_ref, h1pp_ref, tgt_ref, g3_ref, wg_ref, wu_ref, wd_ref, g4_ref,
             f_ref, act_ref, dd_ref, dgate_ref, dup_ref, dh1_ref, loss_ref, dg3_ref, dg4_ref, *slots):
        gate_s, up_s, dd_s, dh2_s, df_s = slots
        i = pl.program_id(0)

        def forward(slot):
            h1v = h1_ref[...]
            hat, _ = _rms_stats(h1v)
            f = (hat * g3_ref[...]).astype(BF16)
            f_ref[...] = f
            s, n = chunks[0]
            gate, up = _dot_nt(f_ref[...], wg_ref[pl.ds(s, n), :]), _dot_nt(f_ref[...], wu_ref[pl.ds(s, n), :])
            yield
            down = None
            for k, (s, n) in enumerate(chunks):
                gate_s.at[slot][:, pl.ds(s, n)] = gate.astype(BF16)
                up_s.at[slot][:, pl.ds(s, n)] = up.astype(BF16)
                act = (gate * jax.nn.sigmoid(gate) * up).astype(BF16)
                act_ref[:, pl.ds(s, n)] = act
                if k + 1 < len(chunks):
                    s1, n1 = chunks[k + 1]
                    gate, up = _dot_nt(f_ref[...], wg_ref[pl.ds(s1, n1), :]), _dot_nt(f_ref[...], wu_ref[pl.ds(s1, n1), :])
                yield
                part = _dot(act_ref[:, pl.ds(s, n)], wd_ref[pl.ds(s, n), :])
                down = part if down is None else down + part
                yield
            d_hat, d_rstd = _rms_stats(down)
            g4 = g4_ref[...]
            err = h1v + d_hat * g4 - tgt_ref[...]
            loss_ref[...] += jnp.sum(err * err) * (0.5 / d)
            dh2 = err * (1.0 / d)
            dh2_s.at[slot][...] = dh2
            dd, dg4 = _rms_bwd(d_hat, d_rstd, g4, dh2)
            dg4_ref[...] += dg4
            dd = dd.astype(BF16)
            dd_ref[...] = dd
            dd_s.at[slot][...] = dd

        def backward(slot):
            s, n = chunks[0]
            dact = _dot_nt(dd_s.at[slot][...], wd_ref[pl.ds(s, n), :])
            yield
            df = None
            for k, (s, n) in enumerate(chunks):
                gate = gate_s.at[slot][:, pl.ds(s, n)].astype(F32)
                up = up_s.at[slot][:, pl.ds(s, n)].astype(F32)
                sig = jax.nn.sigmoid(gate)
                dup = (dact * (gate * sig)).astype(BF16)
                dgate = (dact * up * (sig * (1.0 + gate * (1.0 - sig)))).astype(BF16)
                dup_ref[:, pl.ds(s, n)] = dup
                dgate_ref[:, pl.ds(s, n)] = dgate
                if k + 1 < len(chunks):
                    s1, n1 = chunks[k + 1]
                    dact = _dot_nt(dd_s.at[slot][...], wd_ref[pl.ds(s1, n1), :])
                yield
                part = _dot(dgate_ref[:, pl.ds(s, n)], wg_ref[pl.ds(s, n), :]) + _dot(dup_ref[:, pl.ds(s, n)], wu_ref[pl.ds(s, n), :])
                df = part if df is None else df + part
                yield
            df_s.at[slot][...] = df

        def last(slot):
            hat, rstd = _rms_stats(h1pp_ref[...])
            dh1, dg3 = _rms_bwd(hat, rstd, g3_ref[...], df_s.at[slot][...])
            dg3_ref[...] += dg3
            dh1_ref[...] = dh2_s.at[slot][...] + dh1

        def emit(parity, with_forward, with_backward, with_last):
            fwd = forward(parity) if with_forward else iter(())
            bwd = backward(1 - parity) if with_backward else iter(())
            next(fwd, None)
            if with_last:
                last(parity)
            for _ in range(FFN_BACKWARD_LAG):
                next(fwd, None)
            alive = True
            while alive:
                alive = next(bwd, True) is None
                alive = (next(fwd, True) is None) or alive

        @pl.when(i == 0)
        def _():
            for r in (loss_ref, dg3_ref, dg4_ref, *slots):
                r[...] = jnp.zeros_like(r)

        @pl.when(i < nt)
        def _():
            emit(i % 2, True, True, True)

        @pl.when(i == nt)
        def _():
            emit(nt % 2, False, True, True)

        @pl.when(i == nt + 1)
        def _():
            emit((nt + 1) % 2, False, False, True)

    cur = lambda i: (jnp.minimum(i, nt - 1), 0)
    prev = lambda i: (jnp.clip(i - 1, 0, nt - 1), 0)
    prev2 = lambda i: (jnp.clip(i - 2, 0, nt - 1), 0)
    return pl.pallas_call(
        body, name="ffn_forward_backward", grid=(nt + 2,),
        in_specs=[pl.BlockSpec((tm, d), cur), pl.BlockSpec((tm, d), prev2), pl.BlockSpec((tm, d), cur), _const(g3.shape),
                  _resident(w_gate.shape), _resident(w_up.shape), _resident(w_down.shape), _const(g4.shape)],
        out_specs=[pl.BlockSpec((tm, d), cur), pl.BlockSpec((tm, ff), cur), pl.BlockSpec((tm, d), cur), pl.BlockSpec((tm, ff), prev),
                   pl.BlockSpec((tm, ff), prev), pl.BlockSpec((tm, d), prev2), _const((8, 128)), _const(g3.shape), _const(g4.shape)],
        out_shape=[jax.ShapeDtypeStruct((t, d), BF16), jax.ShapeDtypeStruct((t, ff), BF16), jax.ShapeDtypeStruct((t, d), BF16),
                   jax.ShapeDtypeStruct((t, ff), BF16), jax.ShapeDtypeStruct((t, ff), BF16), jax.ShapeDtypeStruct((t, d), F32),
                   jax.ShapeDtypeStruct((8, 128), F32), jax.ShapeDtypeStruct(g3.shape, F32), jax.ShapeDtypeStruct(g4.shape, F32)],
        scratch_shapes=[pltpu.VMEM((2, tm, ff), BF16)] * 2 + [pltpu.VMEM((2, tm, d), BF16)] + [pltpu.VMEM((2, tm, d), F32)] * 2,
        compiler_params=_params("arbitrary"),
    )(h1, h1, target, g3, w_gate, w_up, w_down, g4)


def _ffn_weight_grads(f, dd, dgate, dup, act):
    t, d = f.shape
    ff = dgate.shape[1]
    tm = min(TM_WGRAD, t)
    nt = t // tm
    fc = ff // FF_CHUNKS
    assert FF_CHUNKS == 2

    def body(f_ref, dd_ref, dgate_ref, dup_ref, act_ref, dwg_ref, dwu_ref, dwd_ref, *rest):
        landing, (acc_g, acc_u, acc_d, stage, sem) = rest[:3], rest[3:8]
        start, finish = _core_exchange_ops([dwg_ref, dwu_ref, dwd_ref], landing, 0, *rest[8:])
        c, i = pl.program_id(0), pl.program_id(1)
        pl.when((c == 1) & (i == 0))(start)

        @pl.when(i == 0)
        def _():
            acc_g[...] = jnp.zeros_like(acc_g)
            acc_u[...] = jnp.zeros_like(acc_u)
            acc_d[...] = jnp.zeros_like(acc_d)

        fv = f_ref[...]
        acc_g[...] += _dot_tn(fv, dgate_ref[...])
        acc_u[...] += _dot_tn(fv, dup_ref[...])
        acc_d[...] += _dot_tn(act_ref[...], dd_ref[...])

        @pl.when(i == nt - 1)
        def _():
            rows = pl.ds(pl.multiple_of(c * fc, 16), fc)
            copies = []
            for k, (acc, out, transposed) in enumerate(((acc_d, dwd_ref, False), (acc_g, dwg_ref, True), (acc_u, dwu_ref, True))):
                if k >= 2:
                    copies[k - 2].wait()
                stage[k % 2] = (acc[...].T if transposed else acc[...]).astype(BF16)
                copies.append(pltpu.make_async_copy(stage.at[k % 2], out.at[rows, :], sem.at[k % 2]))
                copies[k].start()
            copies[-2].wait()
            copies[-1].wait()

        pl.when((c == 1) & (i == nt - 1))(finish)

    row = lambda c, i: (i, 0)
    col = lambda c, i: (i, c)
    out = pl.pallas_call(
        body, name="ffn_weight_grads", grid=(FF_CHUNKS, nt),
        in_specs=[pl.BlockSpec((tm, d), row), pl.BlockSpec((tm, d), row), pl.BlockSpec((tm, fc), col), pl.BlockSpec((tm, fc), col),
                  pl.BlockSpec((tm, fc), col)],
        out_specs=[ANY] * 6,
        out_shape=[jax.ShapeDtypeStruct((ff, d), BF16)] * 3 + [jax.ShapeDtypeStruct((N_DEV, ff // N_DEV, d), BF16)] * 3,
        scratch_shapes=[pltpu.VMEM((d, fc), F32), pltpu.VMEM((d, fc), F32), pltpu.VMEM((fc, d), F32), pltpu.VMEM((2, fc, d), BF16),
                        pltpu.SemaphoreType.DMA((2,))] + _core_exchange_sems(3),
        compiler_params=_params("arbitrary", "arbitrary"),
    )(f, dd, dgate, dup, act)
    return out[:3], out[3:]


def _adamw(w, g, m, v):
    m = ADAM_B1 * m + (1.0 - ADAM_B1) * g
    v = ADAM_B2 * v + (1.0 - ADAM_B2) * (g * g)
    m_hat = m / (1.0 - ADAM_B1 ** ADAM_STEP)
    v_hat = v / (1.0 - ADAM_B2 ** ADAM_STEP)
    return -ADAM_LR * (m_hat / (jnp.sqrt(v_hat) + ADAM_EPS) + ADAM_WD * w), m, v


def _sum_slabs(ref):
    total = ref[0].astype(F32)
    for i in range(1, N_DEV):
        total = total + ref[i].astype(F32)
    return total


def _adamw_rows(r, c):
    tr = r
    for cand in range(8, r, 8):
        if r % cand == 0 and cand * c <= ADAMW_BLOCK_ELEMS:
            tr = cand
    return r if r * c <= ADAMW_BLOCK_ELEMS else tr


def _reduce_adamw_carrying(parts, ws, ms, vs, to_exchange, whole):
    k, nx = len(ws), len(to_exchange)
    r, c = ws[0].shape
    tr = _adamw_rows(r, c)
    steps = r // tr

    def body(*refs):
        p_refs, w_refs, m_refs, v_refs = (refs[a * k:(a + 1) * k] for a in range(4))
        sent, outs = refs[4 * k:4 * k + nx], refs[4 * k + nx:8 * k + nx]
        landed, sems = refs[8 * k + nx:8 * k + 2 * nx], refs[8 * k + 2 * nx:]
        start, finish = _exchange_ops(sent, landed, whole, *sems)
        pl.when(pl.program_id(0) == 0)(start)
        for a in range(k):
            g = _sum_slabs(p_refs[a])
            outs[4 * a][...] = g
            outs[4 * a + 1][...], outs[4 * a + 2][...], outs[4 * a + 3][...] = _adamw(w_refs[a][...], g, m_refs[a][...], v_refs[a][...])
        pl.when(pl.program_id(0) == steps - 1)(finish)

    blk = pl.BlockSpec((tr, c), lambda i: (i, 0))
    out = pl.pallas_call(
        body, name="adamw_ffn_exchange_rest", grid=(steps,),
        in_specs=[pl.BlockSpec((N_DEV, tr, c), lambda i: (0, i, 0))] * k + [blk] * (3 * k) + [ANY] * nx,
        out_specs=[blk] * (4 * k) + [ANY] * nx,
        out_shape=[jax.ShapeDtypeStruct((r, c), F32)] * (4 * k)
        + [jax.ShapeDtypeStruct((N_DEV, *a.shape) if w else a.shape, a.dtype) for a, w in zip(to_exchange, whole)],
        scratch_shapes=_exchange_sems(nx),
        compiler_params=_params("arbitrary"),
    )(*parts, *ws, *ms, *vs, *to_exchange)
    return [tuple(out[4 * a:4 * a + 4]) for a in range(k)], out[4 * k:]


def _reduce_adamw(parts, w, m, v, name):
    r, c = w.shape
    tr = _adamw_rows(r, c)

    def body(p_ref, w_ref, m_ref, v_ref, g_out, d_out, m_out, v_out):
        g = _sum_slabs(p_ref)
        g_out[...] = g
        d_out[...], m_out[...], v_out[...] = _adamw(w_ref[...], g, m_ref[...], v_ref[...])

    blk = pl.BlockSpec((tr, c), lambda i: (i, 0))
    return pl.pallas_call(
        body, name=name, grid=(r // tr,),
        in_specs=[pl.BlockSpec((N_DEV, tr, c), lambda i: (0, i, 0)), blk, blk, blk],
        out_specs=[blk] * 4, out_shape=[jax.ShapeDtypeStruct((r, c), F32)] * 4,
        compiler_params=_params("arbitrary"),
    )(parts, w, m, v)


def _reduce_adamw_small(parts, ws, ms, vs, loss_parts):
    n = len(parts)

    def body(*refs):
        p_refs, w_refs, m_refs, v_refs = (refs[k * n:(k + 1) * n] for k in range(4))
        outs = refs[4 * n + 1:]
        outs[4 * n][...] = _sum_slabs(refs[4 * n])
        for a in range(n):
            g = _sum_slabs(p_refs[a])
            outs[4 * a][...] = g
            outs[4 * a + 1][...], outs[4 * a + 2][...], outs[4 * a + 3][...] = _adamw(w_refs[a][...], g, m_refs[a][...], v_refs[a][...])

    out = pl.pallas_call(
        body, name="adamw_replicated",
        out_shape=[jax.ShapeDtypeStruct(w.shape, F32) for w in ws for _ in range(4)] + [jax.ShapeDtypeStruct(loss_parts.shape[1:], F32)],
        compiler_params=pltpu.CompilerParams(vmem_limit_bytes=VMEM_LIMIT_BYTES),
    )(*parts, *ws, *ms, *vs, loss_parts)
    return [tuple(out[4 * a:4 * a + 4]) for a in range(n)], out[4 * n]


def kernel(x, meta_tokens, norm_mix_pre, w_in, conv_w, pool_w, pool_scale, w_out, norm_mix_post, norm_ffn_pre, w_gate, w_up, w_down, norm_ffn_post, loss_target, m_meta_tokens, m_norm_mix_pre, m_w_in, m_conv_w, m_pool_w, m_pool_scale, m_w_out, m_norm_mix_post, m_norm_ffn_pre, m_w_gate, m_w_up, m_w_down, m_norm_ffn_post, v_meta_tokens, v_norm_mix_pre, v_w_in, v_conv_w, v_pool_w, v_pool_scale, v_w_out, v_norm_mix_post, v_norm_ffn_pre, v_w_gate, v_w_up, v_w_down, v_norm_ffn_post):
    n_seq, seq, d = x.shape
    x2d = x.reshape(n_seq * seq, d)
    target = loss_target.reshape(n_seq * seq, d)

    t_ = lambda a: jnp.swapaxes(a[0], 0, 1)
    pw, ps = pool_w[0], pool_scale

    (h1, z, m, pooled, mixed), (win_b, wout_b, meta, conv, a_meta, z_meta), ffn_slabs = _gather_and_mixer_forward(
        x2d, [w_in[0], w_out[0], meta_tokens, conv_w[0]], [t_(w_gate), t_(w_up), w_down[0]], norm_mix_pre, pw, ps, norm_mix_post, n_seq)
    wg_b, wu_b, wd_b = (s.reshape(-1, d) for s in ffn_slabs)
    f, act, dd, dgate, dup, dh1, loss_sum, dg3, dg4 = _ffn_forward_backward(h1, target, norm_ffn_pre, wg_b, wu_b, wd_b, norm_ffn_post)
    ffn_grads, landing = _ffn_weight_grads(f, dd, dgate, dup, act)
    (gx, dwin, dwout, dg1, dg2, dconv, dpw, dps, dmeta), ffn_parts = _mixer_backward(
        x2d, dh1, m, z, pooled, mixed, meta, a_meta, z_meta, norm_mix_pre, win_b, conv, pw, ps, wout_b, norm_mix_post, n_seq,
        ffn_grads, landing)

    dmeta_s = jnp.transpose(dmeta.reshape(N_META, N_DEV, -1), (1, 0, 2))
    dconv_s = jnp.transpose(dconv.reshape(CONV_WIDTH, N_DEV, -1), (1, 0, 2))
    ffn_res, last = _reduce_adamw_carrying(
        ffn_parts, [t_(w_gate), t_(w_up), w_down[0]], [t_(m_w_gate), t_(m_w_up), m_w_down[0]], [t_(v_w_gate), t_(v_w_up), v_w_down[0]],
        [dwin, dwout.reshape(N_DEV, -1, d), dmeta_s, dconv_s, dg1, dg2, dg3, dg4, dpw, dps, loss_sum], [False] * 4 + [True] * 7)
    replicated = last[4:10]

    names = ["meta_tokens", "norm_mix_pre", "w_in", "conv_w", "pool_w", "pool_scale", "w_out", "norm_mix_post", "norm_ffn_pre", "w_gate",
             "w_up", "w_down", "norm_ffn_post"]
    res = {"w_gate": tuple(jnp.swapaxes(o, 0, 1)[None] for o in ffn_res[0]),
           "w_up": tuple(jnp.swapaxes(o, 0, 1)[None] for o in ffn_res[1]), "w_down": tuple(o[None] for o in ffn_res[2])}
    for nm, parts, w, m_, v_ in (("w_in", last[0], w_in, m_w_in, v_w_in), ("w_out", last[1], w_out, m_w_out, v_w_out),
                                 ("conv_w", last[3], conv_w, m_conv_w, v_conv_w)):
        res[nm] = tuple(o[None] for o in _reduce_adamw(parts, w[0], m_[0], v_[0], "adamw_" + nm))
    res["meta_tokens"] = tuple(_reduce_adamw(last[2], meta_tokens, m_meta_tokens, v_meta_tokens, "adamw_meta_tokens"))
    small, loss = _reduce_adamw_small(
        replicated, [norm_mix_pre, norm_mix_post, norm_ffn_pre, norm_ffn_post, pool_w[0], pool_scale],
        [m_norm_mix_pre, m_norm_mix_post, m_norm_ffn_pre, m_norm_ffn_post, m_pool_w[0], m_pool_scale],
        [v_norm_mix_pre, v_norm_mix_post, v_norm_ffn_pre, v_norm_ffn_post, v_pool_w[0], v_pool_scale], last[10])
    for nm, r in zip(["norm_mix_pre", "norm_mix_post", "norm_ffn_pre", "norm_ffn_post", "pool_w", "pool_scale"], small):
        res[nm] = tuple(o[None] for o in r) if nm == "pool_w" else r

    return (loss[0, 0], gx.reshape(n_seq, seq, d), *[res[nm][0] for nm in names], *[res[nm][1] for nm in names],
            *[res[nm][2] for nm in names], *[res[nm][3] for nm in names])
```

```python
import functools

import jax
import jax.numpy as jnp
from jax import lax
from jax.experimental import pallas as pl
from jax.experimental.pallas import tpu as pltpu

F32, BF16 = jnp.float32, jnp.bfloat16
RMS_EPS = 1e-6
N_META = 16
CONV_WIDTH = 3
POOL_WINDOWS = (2, 4, 8, 16)
POOL_GROUP = 128
HALO = 16
N_DEV = 8
MESH_AXES = ("x", "y", "c")
MESH = pl.DeviceIdType.MESH
VMEM_LIMIT_BYTES = 56 * 1024 * 1024
ADAMW_BLOCK_ELEMS = 64 * 1024
TM_MIX = 512
TM_FFN = 256
FFN_CHUNK = 512
FFN_BACKWARD_LAG = 2
TM_WGRAD = 512
FF_CHUNKS = 2

ADAM_LR, ADAM_B1, ADAM_B2, ADAM_EPS, ADAM_WD, ADAM_STEP = 0.001, 0.9, 0.999, 1e-08, 0.01, 10


def _dot(a, b):
    return jnp.dot(a, b, preferred_element_type=F32)


def _dot_nt(a, b):
    return lax.dot_general(a, b, (((1,), (1,)), ((), ())), preferred_element_type=F32)


def _dot_tn(a, b):
    return lax.dot_general(a, b, (((0,), (0,)), ((), ())), preferred_element_type=F32)


def _rms_stats(h):
    rstd = lax.rsqrt(jnp.mean(h * h, axis=-1, keepdims=True) + RMS_EPS)
    return h * rstd, rstd


def _rms_bwd(hat, rstd, g, dy):
    gdy = dy * g
    proj = jnp.mean(gdy * hat, axis=-1, keepdims=True)
    return rstd * (gdy - hat * proj), jnp.sum(dy * hat, axis=0, keepdims=True)


def _params(*semantics):
    return pltpu.CompilerParams(dimension_semantics=semantics or None, vmem_limit_bytes=VMEM_LIMIT_BYTES)


def _resident(shape):
    zeros = (0,) * len(shape)
    return pl.BlockSpec(shape, lambda *_: zeros, pipeline_mode=pl.Buffered(1))


def _const(shape):
    zeros = (0,) * len(shape)
    return pl.BlockSpec(shape, lambda *_: zeros)


ANY = pl.BlockSpec(memory_space=pl.ANY)


def _my_place():
    x, y, c = (lax.axis_index(a) for a in MESH_AXES)
    return x, y, c


def _exchange_sems(n):
    return [pltpu.SemaphoreType.DMA((n, N_DEV - 1)), pltpu.SemaphoreType.DMA((n, N_DEV - 1)), pltpu.SemaphoreType.DMA((n,))]


def _gather_ops(srcs, outs, send_sems, recv_sems, local_sems, core_major=False):
    n = len(srcs)
    x, y, c = _my_place()
    me, sibling = (x, y, c), (x, y, 1 - c)
    chips = [(1 - x, y), (x, 1 - y), (1 - x, 1 - y)]

    def slab(px, py, pc):
        return 4 * pc + 2 * px + py if core_major else 4 * px + 2 * py + pc

    def copy(a, k, block, to, src=None):
        dst = outs[a].at[slab(*block)]
        return pltpu.make_async_remote_copy(
            src_ref=dst if src is None else src, dst_ref=dst, send_sem=send_sems.at[a, k], recv_sem=recv_sems.at[a, k],
            device_id=to, device_id_type=MESH)

    def mine(a):
        return pltpu.make_async_copy(srcs[a], outs[a].at[slab(*me)], local_sems.at[a])

    def first(a):
        return [copy(a, 0, me, sibling, src=srcs[a])] + [copy(a, 1 + j, me, (*chip, c), src=srcs[a]) for j, chip in enumerate(chips)]

    def passed(a, j):
        return copy(a, 4 + j, (*chips[j], c), sibling)

    def start():
        for a in range(n):
            mine(a).start()
            for cp in first(a):
                cp.start()

    def forward():
        for j, chip in enumerate(chips):
            for a in range(n):
                copy(a, 1 + j, (*chip, c), me).wait_recv()
                passed(a, j).start()

    def finish():
        for a in range(n):
            copy(a, 0, sibling, me).wait_recv()
            for j, chip in enumerate(chips):
                copy(a, 4 + j, (*chip, 1 - c), me).wait_recv()
        for a in range(n):
            for cp in first(a) + [passed(a, j) for j in range(len(chips))]:
                cp.wait_send()
            mine(a).wait()

    return start, forward, finish


def _exchange_ops(ins, outs, whole, send_sems, recv_sems, local_sems):
    n = len(ins)
    x, y, c = _my_place()
    me = 4 * x + 2 * y + c

    def src(a, i):
        return ins[a] if whole[a] else ins[a].at[i]

    def mine(a):
        return pltpu.make_async_copy(src(a, me), outs[a].at[me], local_sems.at[a])

    def send(a, k):
        to = (me + k) % N_DEV
        return pltpu.make_async_remote_copy(
            src_ref=src(a, to), dst_ref=outs[a].at[me], send_sem=send_sems.at[a, k - 1], recv_sem=recv_sems.at[a, k - 1],
            device_id=(to // 4, (to // 2) % 2, to % 2), device_id_type=MESH)

    def landed(a, k):
        frm = (me + N_DEV - k) % N_DEV
        return pltpu.make_async_remote_copy(
            src_ref=src(a, frm), dst_ref=outs[a].at[frm], send_sem=send_sems.at[a, k - 1], recv_sem=recv_sems.at[a, k - 1],
            device_id=(x, y, c), device_id_type=MESH)

    def start():
        for a in range(n):
            mine(a).start()
            for k in range(1, N_DEV):
                send(a, k).start()

    def finish():
        for a in range(n):
            for k in range(1, N_DEV):
                landed(a, k).wait_recv()
        for a in range(n):
            for k in range(1, N_DEV):
                send(a, k).wait_send()
            mine(a).wait()

    return start, finish


def _core_exchange_sems(n):
    return [pltpu.SemaphoreType.DMA((n, 4)), pltpu.SemaphoreType.DMA((n, N_DEV)), pltpu.SemaphoreType.DMA((n,))]


def _core_exchange_ops(ins, outs, to_core, send_sems, recv_sems, local_sems):
    n = len(ins)
    x, y, c = _my_place()
    me = 4 * x + 2 * y + c
    others = [(0, 1), (1, 0), (1, 1)]

    def slab(a, p):
        if len(ins[a].shape) == len(outs[a].shape):
            return ins[a].at[p]
        rows = outs[a].shape[1]
        return ins[a].at[pl.ds(pl.multiple_of(p * rows, 16), rows), :]

    def send(a, dx, dy):
        tx, ty = (x + dx) % 2, (y + dy) % 2
        return pltpu.make_async_remote_copy(
            src_ref=slab(a, 4 * to_core + 2 * tx + ty), dst_ref=outs[a].at[me], send_sem=send_sems.at[a, 2 * dx + dy],
            recv_sem=recv_sems.at[a, 2 * (2 * dx + dy) + c], device_id=(tx, ty, to_core), device_id_type=MESH)

    def mine(a):
        return pltpu.make_async_copy(slab(a, 4 * to_core + 2 * x + y), outs[a].at[me], local_sems.at[a])

    def landed(a, dx, dy, sc):
        frm = 4 * ((x + dx) % 2) + 2 * ((y + dy) % 2) + sc
        return pltpu.make_async_remote_copy(
            src_ref=slab(a, 0), dst_ref=outs[a].at[frm], send_sem=send_sems.at[a, 0], recv_sem=recv_sems.at[a, 2 * (2 * dx + dy) + sc],
            device_id=(x, y, c), device_id_type=MESH)

    def start():
        for a in range(n):
            for dx, dy in others:
                send(a, dx, dy).start()
            pl.when(c == to_core)(mine(a).start)
            pl.when(c != to_core)(send(a, 0, 0).start)

    def finish():
        @pl.when(c == to_core)
        def _():
            for a in range(n):
                for dx, dy in [(0, 0)] + others:
                    for sc in (0, 1):
                        if (dx, dy, sc) != (0, 0, to_core):
                            landed(a, dx, dy, sc).wait_recv()
            for a in range(n):
                mine(a).wait()

        @pl.when(c != to_core)
        def _():
            for a in range(n):
                send(a, 0, 0).wait_send()

        for a in range(n):
            for dx, dy in others:
                send(a, dx, dy).wait_send()

    return start, finish


def _gather_first_weights(gathered, dtypes, cast_only):
    n, k = len(gathered), len(cast_only)

    def body(*refs):
        ins, casts_in = refs[:n], refs[n:n + k]
        outs, casts_out = refs[n + k:2 * n + k], refs[2 * n + k:2 * n + 2 * k]
        stages = refs[2 * n + 2 * k:3 * n + 2 * k]
        start, forward, finish = _gather_ops(stages, outs, *refs[3 * n + 2 * k:])
        for a in range(n):
            stages[a][...] = ins[a][...].astype(stages[a].dtype)
        start()
        for a in range(k):
            casts_out[a][...] = casts_in[a][...].astype(BF16)
        forward()
        finish()

    vmem = pl.BlockSpec(memory_space=pltpu.VMEM)
    out = pl.pallas_call(
        body, name="gather_first_weights",
        out_shape=[jax.ShapeDtypeStruct((N_DEV, *s.shape), d) for s, d in zip(gathered, dtypes)]
        + [jax.ShapeDtypeStruct(s.shape, BF16) for s in cast_only],
        in_specs=[vmem] * (n + k), out_specs=[ANY] * n + [vmem] * k,
        scratch_shapes=[pltpu.VMEM(s.shape, d) for s, d in zip(gathered, dtypes)] + _exchange_sems(n),
        compiler_params=pltpu.CompilerParams(vmem_limit_bytes=VMEM_LIMIT_BYTES),
    )(*gathered, *cast_only)
    return out[:n], out[n:]


def _exchange(arrays, whole, name):
    n = len(arrays)

    def body(*refs):
        start, finish = _exchange_ops(refs[:n], refs[n:2 * n], whole, *refs[2 * n:])
        start()
        finish()

    return pl.pallas_call(
        body, name=name,
        out_shape=[jax.ShapeDtypeStruct((N_DEV, *a.shape) if w else a.shape, a.dtype) for a, w in zip(arrays, whole)],
        in_specs=[ANY] * n, out_specs=[ANY] * n, scratch_shapes=_exchange_sems(n),
    )(*arrays)


def _columns_from_slabs(slabs):
    def body(*refs):
        k = len(refs) // 2
        for src, dst in zip(refs[:k], refs[k:]):
            n = src.shape[2]
            for i in range(N_DEV):
                dst[:, pl.ds(n * i, n)] = src[i]

    return pl.pallas_call(
        body, name="columns_from_slabs",
        out_shape=[jax.ShapeDtypeStruct((s.shape[1], N_DEV * s.shape[2]), s.dtype) for s in slabs],
        compiler_params=pltpu.CompilerParams(vmem_limit_bytes=VMEM_LIMIT_BYTES),
    )(*slabs)


def _window_sum(x, win, ahead):
    n = x.shape[0]
    span = 1
    while span < win:
        x = x + pltpu.roll(x, n - span if ahead else span, 0)
        span *= 2
    return x


def _conv_branch(z, ext_u, conv_ref, tm):
    c_w = z.shape[1] // 4
    b, c, v = z[:, :c_w], z[:, c_w:2 * c_w], z[:, 2 * c_w:3 * c_w]
    u = c * v
    ext_u[pl.ds(HALO, tm), :] = u
    u1 = ext_u[pl.ds(HALO - 1, tm), :]
    u2 = ext_u[pl.ds(HALO - 2, tm), :]
    yc = conv_ref[pl.ds(2, 1), :] * u + conv_ref[pl.ds(1, 1), :] * u1 + conv_ref[pl.ds(0, 1), :] * u2
    return b, c, v, u, u1, u2, yc


def _pool_branch(p, ext_p, pool_w_ref, tm):
    ext_p[pl.ds(HALO, tm), :] = p
    pooled, mixed = [], []
    for g, win in enumerate(POOL_WINDOWS):
        s = _window_sum(ext_p[:, pl.ds(POOL_GROUP * g, POOL_GROUP)], win, ahead=False)[HALO:HALO + tm, :]
        pooled.append((s * (1.0 / win) - p[:, POOL_GROUP * g:POOL_GROUP * (g + 1)]).astype(BF16))
        mixed.append(_dot(pooled[-1], pool_w_ref[g].astype(BF16)))
    return pooled, mixed


def _meta_forward(meta, g1, w_in):
    def body(meta_ref, g1_ref, w_ref, a_ref, z_ref):
        hat, _ = _rms_stats(meta_ref[...])
        a = (hat * g1_ref[...]).astype(BF16)
        a_ref[...] = a
        z_ref[...] = _dot(a, w_ref[...])

    return pl.pallas_call(
        body, name="meta_forward",
        out_shape=[jax.ShapeDtypeStruct(meta.shape, BF16), jax.ShapeDtypeStruct((N_META, w_in.shape[1]), F32)],
        compiler_params=pltpu.CompilerParams(vmem_limit_bytes=VMEM_LIMIT_BYTES),
    )(meta, g1, w_in)


def _mixer_forward(x2d, z_meta, g1, w_in, conv_w, pool_w, pool_scale, w_out, g2, n_seq, to_gather):
    t, d = x2d.shape
    zw = w_in.shape[1]
    cw = zw // 4
    s = t // n_seq
    tm = min(TM_MIX, s)
    nj = s // tm
    ng = len(to_gather)

    def body(x_ref, zm_ref, g1_ref, win_ref, conv_ref, pw_ref, ps_ref, wout_ref, g2_ref, *rest):
        shards, (h1_ref, z_ref, m_ref, pooled_ref, mixed_ref), slabs = rest[:ng], rest[ng:ng + 5], rest[ng + 5:2 * ng + 5]
        ext_u, ext_p = rest[2 * ng + 5:2 * ng + 7]
        start, forward, finish = _gather_ops(shards, slabs, *rest[2 * ng + 7:])
        pl.when((pl.program_id(0) == 0) & (pl.program_id(1) == 0))(start)

        @pl.when(pl.program_id(1) == 0)
        def _():
            zm = zm_ref[...]
            ext_u[pl.ds(0, HALO), :] = zm[:, cw:2 * cw] * zm[:, 2 * cw:3 * cw]
            ext_p[pl.ds(0, HALO), :] = zm[:, 3 * cw:]

        h0 = x_ref[...]
        hat, _ = _rms_stats(h0)
        z = _dot((hat * g1_ref[...]).astype(BF16), win_ref[...])
        z_ref[...] = z.astype(BF16)
        b, _, _, _, _, _, yc = _conv_branch(z, ext_u, conv_ref, tm)
        pooled, mixed = _pool_branch(z[:, 3 * cw:], ext_p, pw_ref, tm)
        pooled_ref[...] = jnp.concatenate(pooled, axis=1)
        mixed_ref[...] = jnp.concatenate(mixed, axis=1).astype(BF16)
        ps = ps_ref[...]
        y = [b * yc] + [mixed[g] * ps[:, POOL_GROUP * g:POOL_GROUP * (g + 1)] for g in range(len(POOL_WINDOWS))]
        m = _dot(jnp.concatenate(y, axis=1).astype(BF16), wout_ref[...])
        m_ref[...] = m
        m_hat, _ = _rms_stats(m)
        h1_ref[...] = h0 + m_hat * g2_ref[...]
        ext_u[pl.ds(0, HALO), :] = ext_u[pl.ds(tm, HALO), :]
        ext_p[pl.ds(0, HALO), :] = ext_p[pl.ds(tm, HALO), :]

        @pl.when((pl.program_id(0) == n_seq - 1) & (pl.program_id(1) == nj - 1))
        def _():
            forward()
            finish()

    row = lambda b, j: (b * nj + j, 0)
    out = pl.pallas_call(
        body, name="mixer_forward", grid=(n_seq, nj),
        in_specs=[pl.BlockSpec((tm, d), row), _const(z_meta.shape), _const(g1.shape), _resident(w_in.shape), _const(conv_w.shape),
                  _const(pool_w.shape), _const(pool_scale.shape), _resident(w_out.shape), _const(g2.shape)] + [ANY] * ng,
        out_specs=[pl.BlockSpec((tm, d), row), pl.BlockSpec((tm, zw), row), pl.BlockSpec((tm, d), row), pl.BlockSpec((tm, cw), row),
                   pl.BlockSpec((tm, cw), row)] + [ANY] * ng,
        out_shape=[jax.ShapeDtypeStruct((t, d), F32), jax.ShapeDtypeStruct((t, zw), BF16), jax.ShapeDtypeStruct((t, d), F32),
                   jax.ShapeDtypeStruct((t, cw), BF16), jax.ShapeDtypeStruct((t, cw), BF16)]
        + [jax.ShapeDtypeStruct((N_DEV, *a.shape), a.dtype) for a in to_gather],
        scratch_shapes=[pltpu.VMEM((tm + HALO, cw), F32), pltpu.VMEM((tm + HALO, cw), F32)] + _exchange_sems(ng),
        compiler_params=_params("arbitrary", "arbitrary"),
    )(x2d, z_meta, g1, w_in, conv_w, pool_w, pool_scale, w_out, g2, *to_gather)
    return out[:5], out[5:]


def _gather_and_mixer_forward(x2d, mixer_shards, ffn_shards, g1, pool_w, pool_scale, g2, n_seq):
    t, d = x2d.shape
    zs, rs, ms, cs = mixer_shards[0].shape[1], mixer_shards[1].shape[0], mixer_shards[2].shape[1], mixer_shards[3].shape[1]
    zw, cw = N_DEV * zs, N_DEV * cs
    s = t // n_seq
    tm = min(TM_MIX, s)
    nj = s // tm
    n1, n2 = len(mixer_shards), len(ffn_shards)
    dtypes = [BF16, BF16, F32, F32] + [BF16] * n2
    shards = list(mixer_shards) + list(ffn_shards)

    def body(x_ref, *rest):
        shard_refs, (g1_ref, pw_ref, ps_ref, g2_ref), rest = rest[:n1 + n2], rest[n1 + n2:n1 + n2 + 4], rest[n1 + n2 + 4:]
        (h1_ref, z_ref, m_ref, pooled_ref, mixed_ref, win_o, wout_o, meta_o, conv_o, am_o, zm_o), rest = rest[:11], rest[11:]
        slabs, rest = rest[:n1 + n2], rest[n1 + n2:]
        stages, rest = rest[:n1 + n2], rest[n1 + n2:]
        win_v, wout_v, meta_v, conv_v, ext_u, ext_p, sem = rest[:7]
        first = _gather_ops(stages[:n1], slabs[:n1], *rest[7:10])
        later = _gather_ops(stages[n1:], slabs[n1:], *rest[10:13], core_major=True)

        @pl.when((pl.program_id(0) == 0) & (pl.program_id(1) == 0))
        def _():
            for src, dst in zip(shard_refs, stages):
                dst[...] = src[...].astype(dst.dtype)
            first[0]()
            later[0]()
            first[1]()
            first[2]()
            copies = [pltpu.make_async_copy(slabs[0].at[i], win_v.at[:, pl.ds(zs * i, zs)], sem.at[i]) for i in range(N_DEV)]
            copies += [pltpu.make_async_copy(slabs[1].at[i], wout_v.at[pl.ds(rs * i, rs), :], sem.at[N_DEV + i]) for i in range(N_DEV)]
            copies += [pltpu.make_async_copy(slabs[2], meta_v, sem.at[2 * N_DEV]), pltpu.make_async_copy(slabs[3], conv_v, sem.at[2 * N_DEV + 1])]
            for cp in copies:
                cp.start()
            for cp in copies:
                cp.wait()
            copies = [pltpu.make_async_copy(win_v, win_o, sem.at[0]), pltpu.make_async_copy(wout_v, wout_o, sem.at[1])]
            for cp in copies:
                cp.start()
            for i in range(N_DEV):
                meta_o[:, pl.ds(ms * i, ms)] = meta_v[i]
                conv_o[:, pl.ds(cs * i, cs)] = conv_v[i]
            hat, _ = _rms_stats(meta_o[...])
            a = (hat * g1_ref[...]).astype(BF16)
            am_o[...] = a
            zm_o[...] = _dot(a, win_v[...])
            for cp in copies:
                cp.wait()

        @pl.when(pl.program_id(1) == 0)
        def _():
            zm = zm_o[...]
            ext_u[pl.ds(0, HALO), :] = zm[:, cw:2 * cw] * zm[:, 2 * cw:3 * cw]
            ext_p[pl.ds(0, HALO), :] = zm[:, 3 * cw:]

        h0 = x_ref[...]
        hat, _ = _rms_stats(h0)
        z = _dot((hat * g1_ref[...]).astype(BF16), win_v[...])
        z_ref[...] = z.astype(BF16)
        b, _, _, _, _, _, yc = _conv_branch(z, ext_u, conv_o, tm)
        pooled, mixed = _pool_branch(z[:, 3 * cw:], ext_p, pw_ref, tm)
        pooled_ref[...] = jnp.concatenate(pooled, axis=1)
        mixed_ref[...] = jnp.concatenate(mixed, axis=1).astype(BF16)
        ps = ps_ref[...]
        y = [b * yc] + [mixed[g] * ps[:, POOL_GROUP * g:POOL_GROUP * (g + 1)] for g in range(len(POOL_WINDOWS))]
        m = _dot(jnp.concatenate(y, axis=1).astype(BF16), wout_v[...])
        m_ref[...] = m
        m_hat, _ = _rms_stats(m)
        h1_ref[...] = h0 + m_hat * g2_ref[...]
        ext_u[pl.ds(0, HALO), :] = ext_u[pl.ds(tm, HALO), :]
        ext_p[pl.ds(0, HALO), :] = ext_p[pl.ds(tm, HALO), :]

        @pl.when((pl.program_id(0) == n_seq - 1) & (pl.program_id(1) == nj - 1))
        def _():
            later[1]()
            later[2]()

    row = lambda b, j: (b * nj + j, 0)
    vmem = pl.BlockSpec(memory_space=pltpu.VMEM)
    small = [(N_META, d), (CONV_WIDTH, cw), (N_META, d), (N_META, zw)]
    out = pl.pallas_call(
        body, name="gather_and_mixer_forward", grid=(n_seq, nj),
        in_specs=[pl.BlockSpec((tm, d), row)] + [vmem] * (n1 + n2)
        + [_const(g1.shape), _const(pool_w.shape), _const(pool_scale.shape), _const(g2.shape)],
        out_specs=[pl.BlockSpec((tm, d), row), pl.BlockSpec((tm, zw), row), pl.BlockSpec((tm, d), row), pl.BlockSpec((tm, cw), row),
                   pl.BlockSpec((tm, cw), row), ANY, ANY] + [_const(sh) for sh in small] + [ANY] * (n1 + n2),
        out_shape=[jax.ShapeDtypeStruct((t, d), F32), jax.ShapeDtypeStruct((t, zw), BF16), jax.ShapeDtypeStruct((t, d), F32),
                   jax.ShapeDtypeStruct((t, cw), BF16), jax.ShapeDtypeStruct((t, cw), BF16),
                   jax.ShapeDtypeStruct((d, zw), BF16), jax.ShapeDtypeStruct((d, d), BF16),
                   jax.ShapeDtypeStruct(small[0], F32), jax.ShapeDtypeStruct(small[1], F32), jax.ShapeDtypeStruct(small[2], BF16),
                   jax.ShapeDtypeStruct(small[3], F32)]
        + [jax.ShapeDtypeStruct((N_DEV, *a.shape), dt) for a, dt in zip(shards, dtypes)],
        scratch_shapes=[pltpu.VMEM(a.shape, dt) for a, dt in zip(shards, dtypes)]
        + [pltpu.VMEM((d, zw), BF16), pltpu.VMEM((d, d), BF16), pltpu.VMEM((N_DEV, N_META, ms), F32),
           pltpu.VMEM((N_DEV, CONV_WIDTH, cs), F32), pltpu.VMEM((tm + HALO, cw), F32), pltpu.VMEM((tm + HALO, cw), F32),
           pltpu.SemaphoreType.DMA((2 * N_DEV + 2,))] + _exchange_sems(n1) + _exchange_sems(n2),
        compiler_params=_params("arbitrary", "arbitrary"),
    )(x2d, *shards, g1, pool_w, pool_scale, g2)
    return out[:5], out[5:11], out[11 + n1:]


def _mixer_backward(x2d, dh1, m, z, pooled, mixed, meta, a_meta, z_meta, g1, w_in, conv_w, pool_w, pool_scale, w_out, g2, n_seq,
                    to_exchange, landing):
    t, d = x2d.shape
    zw = w_in.shape[1]
    cw = zw // 4
    s = t // n_seq
    tm = min(TM_MIX, s)
    nj = s // tm
    n_groups = len(POOL_WINDOWS)
    zs = zw // N_DEV
    nx = len(to_exchange)
    n_in = 17
    given = [k for k, a in enumerate(landing) if a is not None]
    fresh = [k for k, a in enumerate(landing) if a is None]

    def body(x_ref, dh1_ref, m_ref, z_ref, zprev_ref, pooled_ref, mixed_ref, meta_ref, am_ref, zm_ref, g1_ref, win_ref, conv_ref, pw_ref, ps_ref, wout_ref,
             g2_ref, *rest):
        sent, rest = rest[:nx], rest[nx + len(given):]
        gx_ref, dwin_ref, dwout_ref, dg1_ref, dg2_ref, dconv_ref, dpw_ref, dps_ref, dmeta_ref = rest[:9]
        landed, rest = rest[9:9 + nx], rest[9 + nx:]
        ext_u, ext_dyc, ext_dq, acc_win, acc_wout, stage16, sem = rest[:7]
        north = _core_exchange_ops(sent, landed, 1, *rest[7:10])
        south = _core_exchange_ops([sent[k] for k in fresh], [landed[k] for k in fresh], 0, *rest[10:13])

        def start():
            north[0]()
            south[0]()

        def finish():
            south[1]()
            north[1]()

        b_id, j = pl.program_id(0), pl.program_id(1)
        jr = nj - 1 - j
        pl.when((b_id == 0) & (j == 0))(start)

        @pl.when((b_id == 0) & (j == 0))
        def _():
            acc_win[...] = jnp.zeros_like(acc_win)
            acc_wout[...] = jnp.zeros_like(acc_wout)
            for r in (dg1_ref, dg2_ref, dconv_ref, dpw_ref, dps_ref, dmeta_ref):
                r[...] = jnp.zeros_like(r)

        @pl.when(j == 0)
        def _():
            ext_dyc[pl.ds(tm, HALO), :] = jnp.zeros((HALO, cw), F32)
            ext_dq[pl.ds(tm, HALO), :] = jnp.zeros((HALO, cw), F32)

        zm = zm_ref[...]
        halo = jnp.where(jr == 0, zm, zprev_ref[...].astype(F32))
        ext_u[pl.ds(0, HALO), :] = halo[:, cw:2 * cw] * halo[:, 2 * cw:3 * cw]

        h0 = x_ref[...]
        hat0, rstd0 = _rms_stats(h0)
        g1 = g1_ref[...]
        a = (hat0 * g1).astype(BF16)
        b, c, v, u, u1, u2, yc = _conv_branch(z_ref[...].astype(F32), ext_u, conv_ref, tm)
        mixed = [mixed_ref[:, pl.ds(POOL_GROUP * g, POOL_GROUP)].astype(F32) for g in range(n_groups)]
        ps = ps_ref[...]
        y = [b * yc] + [mixed[g] * ps[:, POOL_GROUP * g:POOL_GROUP * (g + 1)] for g in range(n_groups)]
        ycat = jnp.concatenate(y, axis=1).astype(BF16)

        dh1v = dh1_ref[...]
        m_hat, m_rstd = _rms_stats(m_ref[...])
        dm, dg2 = _rms_bwd(m_hat, m_rstd, g2_ref[...], dh1v)
        dg2_ref[...] += dg2
        dm = dm.astype(BF16)
        acc_wout[...] += _dot_tn(ycat, dm)
        dycat = _dot_nt(dm, wout_ref[...])

        dyconv = dycat[:, :cw]
        db = dyconv * yc
        dyc = dyconv * b
        ext_dyc[pl.ds(0, tm), :] = dyc
        du = (conv_ref[pl.ds(2, 1), :] * dyc + conv_ref[pl.ds(1, 1), :] * ext_dyc[pl.ds(1, tm), :]
              + conv_ref[pl.ds(0, 1), :] * ext_dyc[pl.ds(2, tm), :])
        dconv_ref[pl.ds(2, 1), :] += jnp.sum(dyc * u, axis=0, keepdims=True)
        dconv_ref[pl.ds(1, 1), :] += jnp.sum(dyc * u1, axis=0, keepdims=True)
        dconv_ref[pl.ds(0, 1), :] += jnp.sum(dyc * u2, axis=0, keepdims=True)

        dp = []
        for g, win in enumerate(POOL_WINDOWS):
            lanes = pl.ds(POOL_GROUP * g, POOL_GROUP)
            dypool = dycat[:, cw + POOL_GROUP * g:cw + POOL_GROUP * (g + 1)]
            dps_ref[:, lanes] += jnp.sum(dypool * mixed[g], axis=0, keepdims=True)
            dmixed = (dypool * ps[:, POOL_GROUP * g:POOL_GROUP * (g + 1)]).astype(BF16)
            dpw_ref[g] += _dot_tn(pooled_ref[:, lanes], dmixed)
            dq = _dot_nt(dmixed, pw_ref[g].astype(BF16))
            ext_dq[pl.ds(0, tm), lanes] = dq
            acc = _window_sum(ext_dq[:, lanes], win, ahead=True)[0:tm, :]
            dp.append(acc * (1.0 / win) - dq)

        dz = jnp.concatenate([db, du * v, du * c] + dp, axis=1).astype(BF16)
        acc_win[...] += _dot_tn(a, dz)
        dh0, dg1 = _rms_bwd(hat0, rstd0, g1, _dot_nt(dz, win_ref[...]))
        dg1_ref[...] += dg1
        gx_ref[...] = dh1v + dh0

        ext_dyc[pl.ds(tm, HALO), :] = ext_dyc[pl.ds(0, HALO), :]
        ext_dq[pl.ds(tm, HALO), :] = ext_dq[pl.ds(0, HALO), :]

        @pl.when(jr == 0)
        def _():
            ext_dyc[pl.ds(tm - HALO, HALO), :] = jnp.zeros((HALO, cw), F32)
            ext_dq[pl.ds(tm - HALO, HALO), :] = jnp.zeros((HALO, cw), F32)
            du_m = (conv_ref[pl.ds(1, 1), :] * ext_dyc[pl.ds(tm - HALO + 1, HALO), :]
                    + conv_ref[pl.ds(0, 1), :] * ext_dyc[pl.ds(tm - HALO + 2, HALO), :])
            dp_m = []
            for g, win in enumerate(POOL_WINDOWS):
                lanes = pl.ds(POOL_GROUP * g, POOL_GROUP)
                acc = ext_dq[pl.ds(tm - HALO + 1, HALO), lanes]
                for k in range(2, win):
                    acc = acc + ext_dq[pl.ds(tm - HALO + k, HALO), lanes]
                dp_m.append(acc * (1.0 / win))
            dz_m = jnp.concatenate([jnp.zeros((HALO, cw), F32), du_m * zm[:, 2 * cw:3 * cw], du_m * zm[:, cw:2 * cw]] + dp_m,
                                   axis=1).astype(BF16)
            acc_win[...] += _dot_tn(am_ref[...], dz_m)
            hat_m, rstd_m = _rms_stats(meta_ref[...])
            dmeta, dg1_m = _rms_bwd(hat_m, rstd_m, g1, _dot_nt(dz_m, win_ref[...]))
            dg1_ref[...] += dg1_m
            dmeta_ref[...] += dmeta

        @pl.when((b_id == n_seq - 1) & (j == nj - 1))
        def _():
            pieces = [(acc_win, zs * i, dwin_ref.at[i]) for i in range(N_DEV)]
            pieces += [(acc_wout, zs * i, dwout_ref.at[:, pl.ds(zs * i, zs)]) for i in range(d // zs)]
            copies = []
            for k, (acc, col, dst) in enumerate(pieces):
                if k >= 2:
                    copies[k - 2].wait()
                stage16[k % 2] = acc[:, pl.ds(col, zs)].astype(BF16)
                copies.append(pltpu.make_async_copy(stage16.at[k % 2], dst, sem.at[k % 2]))
                copies[k].start()
            copies[-2].wait()
            copies[-1].wait()
            finish()

    row = lambda b, j: (b * nj + nj - 1 - j, 0)
    prev = lambda b, j: (jnp.maximum((b * s + (nj - 1 - j) * tm) // HALO - 1, 0), 0)
    small = [g1.shape, g2.shape, conv_w.shape, pool_w.shape, pool_scale.shape, meta.shape]
    out = pl.pallas_call(
        body, name="mixer_backward", grid=(n_seq, nj),
        in_specs=[pl.BlockSpec((tm, d), row), pl.BlockSpec((tm, d), row), pl.BlockSpec((tm, d), row), pl.BlockSpec((tm, zw), row),
                  pl.BlockSpec((HALO, zw), prev), pl.BlockSpec((tm, cw), row), pl.BlockSpec((tm, cw), row), _const(meta.shape), _const(a_meta.shape), _const(z_meta.shape), _const(g1.shape),
                  _resident(w_in.shape), _const(conv_w.shape), _const(pool_w.shape), _const(pool_scale.shape), _resident(w_out.shape),
                  _const(g2.shape)] + [ANY] * (nx + len(given)),
        out_specs=[pl.BlockSpec((tm, d), row), ANY, ANY] + [_const(sh) for sh in small] + [ANY] * nx,
        out_shape=[jax.ShapeDtypeStruct((t, d), F32), jax.ShapeDtypeStruct((N_DEV, d, zs), BF16),
                   jax.ShapeDtypeStruct(w_out.shape, BF16)] + [jax.ShapeDtypeStruct(sh, F32) for sh in small]
        + [jax.ShapeDtypeStruct((N_DEV, a.shape[0] // N_DEV, a.shape[1]), a.dtype) for a in to_exchange],
        input_output_aliases={n_in + nx + at: 9 + k for at, k in enumerate(given)},
        scratch_shapes=[pltpu.VMEM((tm + HALO, cw), F32)] * 3
        + [pltpu.VMEM(w_in.shape, F32), pltpu.VMEM(w_out.shape, F32), pltpu.VMEM((2, d, zs), BF16),
           pltpu.SemaphoreType.DMA((2,))] + _core_exchange_sems(nx) + _core_exchange_sems(len(fresh)),
        compiler_params=_params("arbitrary", "arbitrary"),
    )(x2d, dh1, m, z, z, pooled, mixed, meta, a_meta, z_meta, g1, w_in, conv_w, pool_w, pool_scale, w_out, g2, *to_exchange, *[landing[k] for k in given])
    return out[:9], out[9:]


def _ffn_forward_backward(h1, target, g3, w_gate, w_up, w_down, g4):
    t, d = h1.shape
    ff = w_gate.shape[0]
    tm = min(TM_FFN, t)
    nt = t // tm
    chunks = [(s, min(FFN_CHUNK, ff - s)) for s in range(0, ff, FFN_CHUNK)]

    def body(h1_ref, h1pp_ref, tgt_ref, g3_ref, wg_ref, wu_ref, wd_ref, g4_ref,
             f_ref, act_ref, dd_ref, dgate_ref, dup_ref, dh1_ref, loss_ref, dg3_ref, dg4_ref, *slots):
        gate_s, up_s, dd_s, dh2_s, df_s = slots
        i = pl.program_id(0)

        def forward(slot):
            h1v = h1_ref[...]
            hat, _ = _rms_stats(h1v)
            f = (hat * g3_ref[...]).astype(BF16)
            f_ref[...] = f
            s, n = chunks[0]
            gate, up = _dot_nt(f_ref[...], wg_ref[pl.ds(s, n), :]), _dot_nt(f_ref[...], wu_ref[pl.ds(s, n), :])
            yield
            down = None
            for k, (s, n) in enumerate(chunks):
                gate_s.at[slot][:, pl.ds(s, n)] = gate.astype(BF16)
                up_s.at[slot][:, pl.ds(s, n)] = up.astype(BF16)
                act = (gate * jax.nn.sigmoid(gate) * up).astype(BF16)
                act_ref[:, pl.ds(s, n)] = act
                if k + 1 < len(chunks):
                    s1, n1 = chunks[k + 1]
                    gate, up = _dot_nt(f_ref[...], wg_ref[pl.ds(s1, n1), :]), _dot_nt(f_ref[...], wu_ref[pl.ds(s1, n1), :])
                yield
                part = _dot(act_ref[:, pl.ds(s, n)], wd_ref[pl.ds(s, n), :])
                down = part if down is None else down + part
                yield
            d_hat, d_rstd = _rms_stats(down)
            g4 = g4_ref[...]
            err = h1v + d_hat * g4 - tgt_ref[...]
            loss_ref[...] += jnp.sum(err * err) * (0.5 / d)
            dh2 = err * (1.0 / d)
            dh2_s.at[slot][...] = dh2
            dd, dg4 = _rms_bwd(d_hat, d_rstd, g4, dh2)
            dg4_ref[...] += dg4
            dd = dd.astype(BF16)
            dd_ref[...] = dd
            dd_s.at[slot][...] = dd

        def backward(slot):
            s, n = chunks[0]
            dact = _dot_nt(dd_s.at[slot][...], wd_ref[pl.ds(s, n), :])
            yield
            df = None
            for k, (s, n) in enumerate(chunks):
                gate = gate_s.at[slot][:, pl.ds(s, n)].astype(F32)
                up = up_s.at[slot][:, pl.ds(s, n)].astype(F32)
                sig = jax.nn.sigmoid(gate)
                dup = (dact * (gate * sig)).astype(BF16)
                dgate = (dact * up * (sig * (1.0 + gate * (1.0 - sig)))).astype(BF16)
                dup_ref[:, pl.ds(s, n)] = dup
                dgate_ref[:, pl.ds(s, n)] = dgate
                if k + 1 < len(chunks):
                    s1, n1 = chunks[k + 1]
                    dact = _dot_nt(dd_s.at[slot][...], wd_ref[pl.ds(s1, n1), :])
                yield
                part = _dot(dgate_ref[:, pl.ds(s, n)], wg_ref[pl.ds(s, n), :]) + _dot(dup_ref[:, pl.ds(s, n)], wu_ref[pl.ds(s, n), :])
                df = part if df is None else df + part
                yield
            df_s.at[slot][...] = df

        def last(slot):
            hat, rstd = _rms_stats(h1pp_ref[...])
            dh1, dg3 = _rms_bwd(hat, rstd, g3_ref[...], df_s.at[slot][...])
            dg3_ref[...] += dg3
            dh1_ref[...] = dh2_s.at[slot][...] + dh1

        def emit(parity, with_forward, with_backward, with_last):
            fwd = forward(parity) if with_forward else iter(())
            bwd = backward(1 - parity) if with_backward else iter(())
            next(fwd, None)
            if with_last:
                last(parity)
            for _ in range(FFN_BACKWARD_LAG):
                next(fwd, None)
            alive = True
            while alive:
                alive = next(bwd, True) is None
                alive = (next(fwd, True) is None) or alive

        @pl.when(i == 0)
        def _():
            for r in (loss_ref, dg3_ref, dg4_ref, *slots):
                r[...] = jnp.zeros_like(r)

        @pl.when(i < nt)
        def _():
            emit(i % 2, True, True, True)

        @pl.when(i == nt)
        def _():
            emit(nt % 2, False, True, True)

        @pl.when(i == nt + 1)
        def _():
            emit((nt + 1) % 2, False, False, True)

    cur = lambda i: (jnp.minimum(i, nt - 1), 0)
    prev = lambda i: (jnp.clip(i - 1, 0, nt - 1), 0)
    prev2 = lambda i: (jnp.clip(i - 2, 0, nt - 1), 0)
    return pl.pallas_call(
        body, name="ffn_forward_backward", grid=(nt + 2,),
        in_specs=[pl.BlockSpec((tm, d), cur), pl.BlockSpec((tm, d), prev2), pl.BlockSpec((tm, d), cur), _const(g3.shape),
                  _resident(w_gate.shape), _resident(w_up.shape), _resident(w_down.shape), _const(g4.shape)],
        out_specs=[pl.BlockSpec((tm, d), cur), pl.BlockSpec((tm, ff), cur), pl.BlockSpec((tm, d), cur), pl.BlockSpec((tm, ff), prev),
                   pl.BlockSpec((tm, ff), prev), pl.BlockSpec((tm, d), prev2), _const((8, 128)), _const(g3.shape), _const(g4.shape)],
        out_shape=[jax.ShapeDtypeStruct((t, d), BF16), jax.ShapeDtypeStruct((t, ff), BF16), jax.ShapeDtypeStruct((t, d), BF16),
                   jax.ShapeDtypeStruct((t, ff), BF16), jax.ShapeDtypeStruct((t, ff), BF16), jax.ShapeDtypeStruct((t, d), F32),
                   jax.ShapeDtypeStruct((8, 128), F32), jax.ShapeDtypeStruct(g3.shape, F32), jax.ShapeDtypeStruct(g4.shape, F32)],
        scratch_shapes=[pltpu.VMEM((2, tm, ff), BF16)] * 2 + [pltpu.VMEM((2, tm, d), BF16)] + [pltpu.VMEM((2, tm, d), F32)] * 2,
        compiler_params=_params("arbitrary"),
    )(h1, h1, target, g3, w_gate, w_up, w_down, g4)


def _ffn_weight_grads(f, dd, dgate, dup, act):
    t, d = f.shape
    ff = dgate.shape[1]
    tm = min(TM_WGRAD, t)
    nt = t // tm
    fc = ff // FF_CHUNKS
    assert FF_CHUNKS == 2

    def body(f_ref, dd_ref, dgate_ref, dup_ref, act_ref, dwg_ref, dwu_ref, dwd_ref, *rest):
        landing, (acc_g, acc_u, acc_d, stage, sem) = rest[:2], rest[2:7]
        start, finish = _core_exchange_ops([dwg_ref, dwd_ref], landing, 0, *rest[7:])
        c, i = pl.program_id(0), pl.program_id(1)
        pl.when((c == 1) & (i == 0))(start)

        @pl.when(i == 0)
        def _():
            acc_g[...] = jnp.zeros_like(acc_g)
            acc_u[...] = jnp.zeros_like(acc_u)
            acc_d[...] = jnp.zeros_like(acc_d)

        fv = f_ref[...]
        acc_g[...] += _dot_tn(fv, dgate_ref[...])
        acc_u[...] += _dot_tn(fv, dup_ref[...])
        acc_d[...] += _dot_tn(act_ref[...], dd_ref[...])

        @pl.when(i == nt - 1)
        def _():
            rows = pl.ds(pl.multiple_of(c * fc, 16), fc)
            copies = []
            for k, (acc, out, transposed) in enumerate(((acc_d, dwd_ref, False), (acc_g, dwg_ref, True), (acc_u, dwu_ref, True))):
                if k >= 2:
                    copies[k - 2].wait()
                stage[k % 2] = (acc[...].T if transposed else acc[...]).astype(BF16)
                copies.append(pltpu.make_async_copy(stage.at[k % 2], out.at[rows, :], sem.at[k % 2]))
                copies[k].start()
            copies[-2].wait()
            copies[-1].wait()

        pl.when((c == 1) & (i == nt - 1))(finish)

    row = lambda c, i: (i, 0)
    col = lambda c, i: (i, c)
    out = pl.pallas_call(
        body, name="ffn_weight_grads", grid=(FF_CHUNKS, nt),
        in_specs=[pl.BlockSpec((tm, d), row), pl.BlockSpec((tm, d), row), pl.BlockSpec((tm, fc), col), pl.BlockSpec((tm, fc), col),
                  pl.BlockSpec((tm, fc), col)],
        out_specs=[ANY] * 5,
        out_shape=[jax.ShapeDtypeStruct((ff, d), BF16)] * 3 + [jax.ShapeDtypeStruct((N_DEV, ff // N_DEV, d), BF16)] * 2,
        scratch_shapes=[pltpu.VMEM((d, fc), F32), pltpu.VMEM((d, fc), F32), pltpu.VMEM((fc, d), F32), pltpu.VMEM((2, fc, d), BF16),
                        pltpu.SemaphoreType.DMA((2,))] + _core_exchange_sems(2),
        compiler_params=_params("arbitrary", "arbitrary"),
    )(f, dd, dgate, dup, act)
    return out[:3], [out[3], None, out[4]]


def _adamw(w, g, m, v):
    m = ADAM_B1 * m + (1.0 - ADAM_B1) * g
    v = ADAM_B2 * v + (1.0 - ADAM_B2) * (g * g)
    m_hat = m / (1.0 - ADAM_B1 ** ADAM_STEP)
    v_hat = v / (1.0 - ADAM_B2 ** ADAM_STEP)
    return -ADAM_LR * (m_hat / (jnp.sqrt(v_hat) + ADAM_EPS) + ADAM_WD * w), m, v


def _sum_slabs(ref):
    total = ref[0].astype(F32)
    for i in range(1, N_DEV):
        total = total + ref[i].astype(F32)
    return total


def _adamw_rows(r, c):
    tr = r
    for cand in range(8, r, 8):
        if r % cand == 0 and cand * c <= ADAMW_BLOCK_ELEMS:
            tr = cand
    return r if r * c <= ADAMW_BLOCK_ELEMS else tr


def _reduce_adamw_carrying(parts, ws, ms, vs, to_exchange, whole):
    k, nx = len(ws), len(to_exchange)
    r, c = ws[0].shape
    tr = _adamw_rows(r, c)
    steps = r // tr

    def body(*refs):
        p_refs, w_refs, m_refs, v_refs = (refs[a * k:(a + 1) * k] for a in range(4))
        sent, outs = refs[4 * k:4 * k + nx], refs[4 * k + nx:8 * k + nx]
        landed, sems = refs[8 * k + nx:8 * k + 2 * nx], refs[8 * k + 2 * nx:]
        start, finish = _exchange_ops(sent, landed, whole, *sems)
        pl.when(pl.program_id(0) == 0)(start)
        for a in range(k):
            g = _sum_slabs(p_refs[a])
            outs[4 * a][...] = g
            outs[4 * a + 1][...], outs[4 * a + 2][...], outs[4 * a + 3][...] = _adamw(w_refs[a][...], g, m_refs[a][...], v_refs[a][...])
        pl.when(pl.program_id(0) == steps - 1)(finish)

    blk = pl.BlockSpec((tr, c), lambda i: (i, 0))
    out = pl.pallas_call(
        body, name="adamw_ffn_exchange_rest", grid=(steps,),
        in_specs=[pl.BlockSpec((N_DEV, tr, c), lambda i: (0, i, 0))] * k + [blk] * (3 * k) + [ANY] * nx,
        out_specs=[blk] * (4 * k) + [ANY] * nx,
        out_shape=[jax.ShapeDtypeStruct((r, c), F32)] * (4 * k)
        + [jax.ShapeDtypeStruct((N_DEV, *a.shape) if w else a.shape, a.dtype) for a, w in zip(to_exchange, whole)],
        scratch_shapes=_exchange_sems(nx),
        compiler_params=_params("arbitrary"),
    )(*parts, *ws, *ms, *vs, *to_exchange)
    return [tuple(out[4 * a:4 * a + 4]) for a in range(k)], out[4 * k:]


def _reduce_adamw(parts, w, m, v, name):
    r, c = w.shape
    tr = _adamw_rows(r, c)

    def body(p_ref, w_ref, m_ref, v_ref, g_out, d_out, m_out, v_out):
        g = _sum_slabs(p_ref)
        g_out[...] = g
        d_out[...], m_out[...], v_out[...] = _adamw(w_ref[...], g, m_ref[...], v_ref[...])

    blk = pl.BlockSpec((tr, c), lambda i: (i, 0))
    return pl.pallas_call(
        body, name=name, grid=(r // tr,),
        in_specs=[pl.BlockSpec((N_DEV, tr, c), lambda i: (0, i, 0)), blk, blk, blk],
        out_specs=[blk] * 4, out_shape=[jax.ShapeDtypeStruct((r, c), F32)] * 4,
        compiler_params=_params("arbitrary"),
    )(parts, w, m, v)


def _reduce_adamw_small(parts, ws, ms, vs, loss_parts):
    n = len(parts)

    def body(*refs):
        p_refs, w_refs, m_refs, v_refs = (refs[k * n:(k + 1) * n] for k in range(4))
        outs = refs[4 * n + 1:]
        outs[4 * n][...] = _sum_slabs(refs[4 * n])
        for a in range(n):
            g = _sum_slabs(p_refs[a])
            outs[4 * a][...] = g
            outs[4 * a + 1][...], outs[4 * a + 2][...], outs[4 * a + 3][...] = _adamw(w_refs[a][...], g, m_refs[a][...], v_refs[a][...])

    out = pl.pallas_call(
        body, name="adamw_replicated",
        out_shape=[jax.ShapeDtypeStruct(w.shape, F32) for w in ws for _ in range(4)] + [jax.ShapeDtypeStruct(loss_parts.shape[1:], F32)],
        compiler_params=pltpu.CompilerParams(vmem_limit_bytes=VMEM_LIMIT_BYTES),
    )(*parts, *ws, *ms, *vs, loss_parts)
    return [tuple(out[4 * a:4 * a + 4]) for a in range(n)], out[4 * n]


def kernel(x, meta_tokens, norm_mix_pre, w_in, conv_w, pool_w, pool_scale, w_out, norm_mix_post, norm_ffn_pre, w_gate, w_up, w_down, norm_ffn_post, loss_target, m_meta_tokens, m_norm_mix_pre, m_w_in, m_conv_w, m_pool_w, m_pool_scale, m_w_out, m_norm_mix_post, m_norm_ffn_pre, m_w_gate, m_w_up, m_w_down, m_norm_ffn_post, v_meta_tokens, v_norm_mix_pre, v_w_in, v_conv_w, v_pool_w, v_pool_scale, v_w_out, v_norm_mix_post, v_norm_ffn_pre, v_w_gate, v_w_up, v_w_down, v_norm_ffn_post):
    n_seq, seq, d = x.shape
    x2d = x.reshape(n_seq * seq, d)
    target = loss_target.reshape(n_seq * seq, d)

    t_ = lambda a: jnp.swapaxes(a[0], 0, 1)
    pw, ps = pool_w[0], pool_scale

    (h1, z, m, pooled, mixed), (win_b, wout_b, meta, conv, a_meta, z_meta), ffn_slabs = _gather_and_mixer_forward(
        x2d, [w_in[0], w_out[0], meta_tokens, conv_w[0]], [t_(w_gate), t_(w_up), w_down[0]], norm_mix_pre, pw, ps, norm_mix_post, n_seq)
    wg_b, wu_b, wd_b = (s.reshape(-1, d) for s in ffn_slabs)
    f, act, dd, dgate, dup, dh1, loss_sum, dg3, dg4 = _ffn_forward_backward(h1, target, norm_ffn_pre, wg_b, wu_b, wd_b, norm_ffn_post)
    ffn_grads, landing = _ffn_weight_grads(f, dd, dgate, dup, act)
    (gx, dwin, dwout, dg1, dg2, dconv, dpw, dps, dmeta), ffn_parts = _mixer_backward(
        x2d, dh1, m, z, pooled, mixed, meta, a_meta, z_meta, norm_mix_pre, win_b, conv, pw, ps, wout_b, norm_mix_post, n_seq,
        ffn_grads, landing)

    dmeta_s = jnp.transpose(dmeta.reshape(N_META, N_DEV, -1), (1, 0, 2))
    dconv_s = jnp.transpose(dconv.reshape(CONV_WIDTH, N_DEV, -1), (1, 0, 2))
    ffn_res, last = _reduce_adamw_carrying(
        ffn_parts, [t_(w_gate), t_(w_up), w_down[0]], [t_(m_w_gate), t_(m_w_up), m_w_down[0]], [t_(v_w_gate), t_(v_w_up), v_w_down[0]],
        [dwin, dwout.reshape(N_DEV, -1, d), dmeta_s, dconv_s, dg1, dg2, dg3, dg4, dpw, dps, loss_sum], [False] * 4 + [True] * 7)
    replicated = last[4:10]

    names = ["meta_tokens", "norm_mix_pre", "w_in", "conv_w", "pool_w", "pool_scale", "w_out", "norm_mix_post", "norm_ffn_pre", "w_gate",
             "w_up", "w_down", "norm_ffn_post"]
    res = {"w_gate": tuple(jnp.swapaxes(o, 0, 1)[None] for o in ffn_res[0]),
           "w_up": tuple(jnp.swapaxes(o, 0, 1)[None] for o in ffn_res[1]), "w_down": tuple(o[None] for o in ffn_res[2])}
    for nm, parts, w, m_, v_ in (("w_in", last[0], w_in, m_w_in, v_w_in), ("w_out", last[1], w_out, m_w_out, v_w_out),
                                 ("conv_w", last[3], conv_w, m_conv_w, v_conv_w)):
        res[nm] = tuple(o[None] for o in _reduce_adamw(parts, w[0], m_[0], v_[0], "adamw_" + nm))
    res["meta_tokens"] = tuple(_reduce_adamw(last[2], meta_tokens, m_meta_tokens, v_meta_tokens, "adamw_meta_tokens"))
    small, loss = _reduce_adamw_small(
        replicated, [norm_mix_pre, norm_mix_post, norm_ffn_pre, norm_ffn_post, pool_w[0], pool_scale],
        [m_norm_mix_pre, m_norm_mix_post, m_norm_ffn_pre, m_norm_ffn_post, m_pool_w[0], m_pool_scale],
        [v_norm_mix_pre, v_norm_mix_post, v_norm_ffn_pre, v_norm_ffn_post, v_pool_w[0], v_pool_scale], last[10])
    for nm, r in zip(["norm_mix_pre", "norm_mix_post", "norm_ffn_pre", "norm_ffn_post", "pool_w", "pool_scale"], small):
        res[nm] = tuple(o[None] for o in r) if nm == "pool_w" else r

    return (loss[0, 0], gx.reshape(n_seq, seq, d), *[res[nm][0] for nm in names], *[res[nm][1] for nm in names],
            *[res[nm][2] for nm in names], *[res[nm][3] for nm in names])
```

```python
import functools

import jax
import jax.numpy as jnp
from jax import lax
from jax.experimental import pallas as pl
from jax.experimental.pallas import tpu as pltpu

F32, BF16 = jnp.float32, jnp.bfloat16
RMS_EPS = 1e-6
N_META = 16
CONV_WIDTH = 3
POOL_WINDOWS = (2, 4, 8, 16)
POOL_GROUP = 128
HALO = 16
N_DEV = 8
MESH_AXES = ("x", "y", "c")
MESH = pl.DeviceIdType.MESH
VMEM_LIMIT_BYTES = 56 * 1024 * 1024
ADAMW_BLOCK_ELEMS = 64 * 1024
TM_MIX = 512
TM_FFN = 256
FFN_CHUNK = 512
FFN_BACKWARD_LAG = 2
TM_WGRAD = 512
FF_CHUNKS = 2

ADAM_LR, ADAM_B1, ADAM_B2, ADAM_EPS, ADAM_WD, ADAM_STEP = 0.001, 0.9, 0.999, 1e-08, 0.01, 10


def _dot(a, b):
    return jnp.dot(a, b, preferred_element_type=F32)


def _dot_nt(a, b):
    return lax.dot_general(a, b, (((1,), (1,)), ((), ())), preferred_element_type=F32)


def _dot_tn(a, b):
    return lax.dot_general(a, b, (((0,), (0,)), ((), ())), preferred_element_type=F32)


def _rms_stats(h):
    rstd = lax.rsqrt(jnp.mean(h * h, axis=-1, keepdims=True) + RMS_EPS)
    return h * rstd, rstd


def _rms_bwd(hat, rstd, g, dy):
    gdy = dy * g
    proj = jnp.mean(gdy * hat, axis=-1, keepdims=True)
    return rstd * (gdy - hat * proj), jnp.sum(dy * hat, axis=0, keepdims=True)


def _params(*semantics):
    return pltpu.CompilerParams(dimension_semantics=semantics or None, vmem_limit_bytes=VMEM_LIMIT_BYTES)


def _resident(shape):
    zeros = (0,) * len(shape)
    return pl.BlockSpec(shape, lambda *_: zeros, pipeline_mode=pl.Buffered(1))


def _const(shape):
    zeros = (0,) * len(shape)
    return pl.BlockSpec(shape, lambda *_: zeros)


ANY = pl.BlockSpec(memory_space=pl.ANY)


def _my_place():
    x, y, c = (lax.axis_index(a) for a in MESH_AXES)
    return x, y, c


def _exchange_sems(n):
    return [pltpu.SemaphoreType.DMA((n, N_DEV - 1)), pltpu.SemaphoreType.DMA((n, N_DEV - 1)), pltpu.SemaphoreType.DMA((n,))]


def _gather_ops(srcs, outs, send_sems, recv_sems, local_sems, core_major=False):
    n = len(srcs)
    x, y, c = _my_place()
    me, sibling = (x, y, c), (x, y, 1 - c)
    chips = [(1 - x, y), (x, 1 - y), (1 - x, 1 - y)]

    def slab(px, py, pc):
        return 4 * pc + 2 * px + py if core_major else 4 * px + 2 * py + pc

    def copy(a, k, block, to, src=None):
        dst = outs[a].at[slab(*block)]
        return pltpu.make_async_remote_copy(
            src_ref=dst if src is None else src, dst_ref=dst, send_sem=send_sems.at[a, k], recv_sem=recv_sems.at[a, k],
            device_id=to, device_id_type=MESH)

    def mine(a):
        return pltpu.make_async_copy(srcs[a], outs[a].at[slab(*me)], local_sems.at[a])

    def first(a):
        return [copy(a, 0, me, sibling, src=srcs[a])] + [copy(a, 1 + j, me, (*chip, c), src=srcs[a]) for j, chip in enumerate(chips)]

    def passed(a, j):
        return copy(a, 4 + j, (*chips[j], c), sibling)

    def start():
        for a in range(n):
            mine(a).start()
            for cp in first(a):
                cp.start()

    def forward():
        for j, chip in enumerate(chips):
            for a in range(n):
                copy(a, 1 + j, (*chip, c), me).wait_recv()
                passed(a, j).start()

    def finish():
        for a in range(n):
            copy(a, 0, sibling, me).wait_recv()
            for j, chip in enumerate(chips):
                copy(a, 4 + j, (*chip, 1 - c), me).wait_recv()
        for a in range(n):
            for cp in first(a) + [passed(a, j) for j in range(len(chips))]:
                cp.wait_send()
            mine(a).wait()

    return start, forward, finish


def _exchange_ops(ins, outs, whole, send_sems, recv_sems, local_sems):
    n = len(ins)
    x, y, c = _my_place()
    me = 4 * x + 2 * y + c

    def src(a, i):
        return ins[a] if whole[a] else ins[a].at[i]

    def mine(a):
        return pltpu.make_async_copy(src(a, me), outs[a].at[me], local_sems.at[a])

    def send(a, k):
        to = (me + k) % N_DEV
        return pltpu.make_async_remote_copy(
            src_ref=src(a, to), dst_ref=outs[a].at[me], send_sem=send_sems.at[a, k - 1], recv_sem=recv_sems.at[a, k - 1],
            device_id=(to // 4, (to // 2) % 2, to % 2), device_id_type=MESH)

    def landed(a, k):
        frm = (me + N_DEV - k) % N_DEV
        return pltpu.make_async_remote_copy(
            src_ref=src(a, frm), dst_ref=outs[a].at[frm], send_sem=send_sems.at[a, k - 1], recv_sem=recv_sems.at[a, k - 1],
            device_id=(x, y, c), device_id_type=MESH)

    def start():
        for a in range(n):
            mine(a).start()
            for k in range(1, N_DEV):
                send(a, k).start()

    def finish():
        for a in range(n):
            for k in range(1, N_DEV):
                landed(a, k).wait_recv()
        for a in range(n):
            for k in range(1, N_DEV):
                send(a, k).wait_send()
            mine(a).wait()

    return start, finish


def _core_exchange_sems(n):
    return [pltpu.SemaphoreType.DMA((n, 4)), pltpu.SemaphoreType.DMA((n, N_DEV)), pltpu.SemaphoreType.DMA((n,))]


def _core_exchange_ops(ins, outs, to_core, send_sems, recv_sems, local_sems):
    n = len(ins)
    x, y, c = _my_place()
    me = 4 * x + 2 * y + c
    others = [(0, 1), (1, 0), (1, 1)]

    def slab(a, p):
        if len(ins[a].shape) == len(outs[a].shape):
            return ins[a].at[p]
        rows = outs[a].shape[1]
        return ins[a].at[pl.ds(pl.multiple_of(p * rows, 16), rows), :]

    def send(a, dx, dy):
        tx, ty = (x + dx) % 2, (y + dy) % 2
        return pltpu.make_async_remote_copy(
            src_ref=slab(a, 4 * to_core + 2 * tx + ty), dst_ref=outs[a].at[me], send_sem=send_sems.at[a, 2 * dx + dy],
            recv_sem=recv_sems.at[a, 2 * (2 * dx + dy) + c], device_id=(tx, ty, to_core), device_id_type=MESH)

    def mine(a):
        return pltpu.make_async_copy(slab(a, 4 * to_core + 2 * x + y), outs[a].at[me], local_sems.at[a])

    def landed(a, dx, dy, sc):
        frm = 4 * ((x + dx) % 2) + 2 * ((y + dy) % 2) + sc
        return pltpu.make_async_remote_copy(
            src_ref=slab(a, 0), dst_ref=outs[a].at[frm], send_sem=send_sems.at[a, 0], recv_sem=recv_sems.at[a, 2 * (2 * dx + dy) + sc],
            device_id=(x, y, c), device_id_type=MESH)

    def start():
        for a in range(n):
            for dx, dy in others:
                send(a, dx, dy).start()
            pl.when(c == to_core)(mine(a).start)
            pl.when(c != to_core)(send(a, 0, 0).start)

    def finish():
        @pl.when(c == to_core)
        def _():
            for a in range(n):
                for dx, dy in [(0, 0)] + others:
                    for sc in (0, 1):
                        if (dx, dy, sc) != (0, 0, to_core):
                            landed(a, dx, dy, sc).wait_recv()
            for a in range(n):
                mine(a).wait()

        @pl.when(c != to_core)
        def _():
            for a in range(n):
                send(a, 0, 0).wait_send()

        for a in range(n):
            for dx, dy in others:
                send(a, dx, dy).wait_send()

    return start, finish


N_CHIP = 4


def _pair_then_chip_sems(n):
    return [pltpu.SemaphoreType.DMA((n, N_CHIP)) for _ in range(6)] + [pltpu.SemaphoreType.DMA((n,))]


def _pair_then_chip_ops(ins, pairs, outs, mine_v, pair_v, sum_v, pair_send, pair_recv, chip_send, chip_recv, load_a, load_b, own_sem):
    n = len(ins)
    x, y, c = _my_place()
    chip = 2 * x + y
    chips = [(0, 0), (0, 1), (1, 0), (1, 1)]
    others = [(0, 1), (1, 0), (1, 1)]

    def to_sibling(a, j):
        px, py = chips[j]
        return pltpu.make_async_remote_copy(
            src_ref=ins[a].at[4 * px + 2 * py + 1 - c], dst_ref=pairs[a].at[j], send_sem=pair_send.at[a, j], recv_sem=pair_recv.at[a, j],
            device_id=(x, y, 1 - c), device_id_type=MESH)

    def spread(a, dx, dy):
        tx, ty = (x + dx) % 2, (y + dy) % 2
        return pltpu.make_async_remote_copy(
            src_ref=sum_v[a].at[2 * tx + ty], dst_ref=outs[a].at[chip], send_sem=chip_send.at[a, 2 * dx + dy],
            recv_sem=chip_recv.at[a, 2 * dx + dy], device_id=(tx, ty, c), device_id_type=MESH)

    def landed(a, dx, dy):
        frm = 2 * ((x + dx) % 2) + (y + dy) % 2
        return pltpu.make_async_remote_copy(
            src_ref=sum_v[a].at[0], dst_ref=outs[a].at[frm], send_sem=chip_send.at[a, 0], recv_sem=chip_recv.at[a, 2 * dx + dy],
            device_id=(x, y, c), device_id_type=MESH)

    def own(a):
        return pltpu.make_async_copy(sum_v[a].at[chip], outs[a].at[chip], own_sem.at[a])

    def pair():
        loads = []
        for a in range(n):
            for j, (px, py) in enumerate(chips):
                to_sibling(a, j).start()
                loads.append(pltpu.make_async_copy(ins[a].at[4 * px + 2 * py + c], mine_v[a].at[j], load_a.at[a, j]))
                loads[-1].start()
        for a in range(n):
            for j in range(N_CHIP):
                to_sibling(a, j).wait_recv()
                loads.append(pltpu.make_async_copy(pairs[a].at[j], pair_v[a].at[j], load_b.at[a, j]))
                loads[-1].start()
        for cp in loads:
            cp.wait()
        for a in range(n):
            sum_v[a][...] = (mine_v[a][...].astype(F32) + pair_v[a][...].astype(F32)).astype(sum_v[a].dtype)

    def start():
        pair()
        for a in range(n):
            own(a).start()
            for dx, dy in others:
                spread(a, dx, dy).start()

    def finish():
        for a in range(n):
            for dx, dy in others:
                landed(a, dx, dy).wait_recv()
        for a in range(n):
            for dx, dy in others:
                spread(a, dx, dy).wait_send()
            for j in range(N_CHIP):
                to_sibling(a, j).wait_send()
            own(a).wait()

    return start, finish


def _gather_first_weights(gathered, dtypes, cast_only):
    n, k = len(gathered), len(cast_only)

    def body(*refs):
        ins, casts_in = refs[:n], refs[n:n + k]
        outs, casts_out = refs[n + k:2 * n + k], refs[2 * n + k:2 * n + 2 * k]
        stages = refs[2 * n + 2 * k:3 * n + 2 * k]
        start, forward, finish = _gather_ops(stages, outs, *refs[3 * n + 2 * k:])
        for a in range(n):
            stages[a][...] = ins[a][...].astype(stages[a].dtype)
        start()
        for a in range(k):
            casts_out[a][...] = casts_in[a][...].astype(BF16)
        forward()
        finish()

    vmem = pl.BlockSpec(memory_space=pltpu.VMEM)
    out = pl.pallas_call(
        body, name="gather_first_weights",
        out_shape=[jax.ShapeDtypeStruct((N_DEV, *s.shape), d) for s, d in zip(gathered, dtypes)]
        + [jax.ShapeDtypeStruct(s.shape, BF16) for s in cast_only],
        in_specs=[vmem] * (n + k), out_specs=[ANY] * n + [vmem] * k,
        scratch_shapes=[pltpu.VMEM(s.shape, d) for s, d in zip(gathered, dtypes)] + _exchange_sems(n),
        compiler_params=pltpu.CompilerParams(vmem_limit_bytes=VMEM_LIMIT_BYTES),
    )(*gathered, *cast_only)
    return out[:n], out[n:]


def _exchange(arrays, whole, name):
    n = len(arrays)

    def body(*refs):
        start, finish = _exchange_ops(refs[:n], refs[n:2 * n], whole, *refs[2 * n:])
        start()
        finish()

    return pl.pallas_call(
        body, name=name,
        out_shape=[jax.ShapeDtypeStruct((N_DEV, *a.shape) if w else a.shape, a.dtype) for a, w in zip(arrays, whole)],
        in_specs=[ANY] * n, out_specs=[ANY] * n, scratch_shapes=_exchange_sems(n),
    )(*arrays)


def _columns_from_slabs(slabs):
    def body(*refs):
        k = len(refs) // 2
        for src, dst in zip(refs[:k], refs[k:]):
            n = src.shape[2]
            for i in range(N_DEV):
                dst[:, pl.ds(n * i, n)] = src[i]

    return pl.pallas_call(
        body, name="columns_from_slabs",
        out_shape=[jax.ShapeDtypeStruct((s.shape[1], N_DEV * s.shape[2]), s.dtype) for s in slabs],
        compiler_params=pltpu.CompilerParams(vmem_limit_bytes=VMEM_LIMIT_BYTES),
    )(*slabs)


def _window_sum(x, win, ahead):
    n = x.shape[0]
    span = 1
    while span < win:
        x = x + pltpu.roll(x, n - span if ahead else span, 0)
        span *= 2
    return x


def _conv_branch(z, ext_u, conv_ref, tm):
    c_w = z.shape[1] // 4
    b, c, v = z[:, :c_w], z[:, c_w:2 * c_w], z[:, 2 * c_w:3 * c_w]
    u = c * v
    ext_u[pl.ds(HALO, tm), :] = u
    u1 = ext_u[pl.ds(HALO - 1, tm), :]
    u2 = ext_u[pl.ds(HALO - 2, tm), :]
    yc = conv_ref[pl.ds(2, 1), :] * u + conv_ref[pl.ds(1, 1), :] * u1 + conv_ref[pl.ds(0, 1), :] * u2
    return b, c, v, u, u1, u2, yc


def _pool_branch(p, ext_p, pool_w_ref, tm):
    ext_p[pl.ds(HALO, tm), :] = p
    pooled, mixed = [], []
    for g, win in enumerate(POOL_WINDOWS):
        s = _window_sum(ext_p[:, pl.ds(POOL_GROUP * g, POOL_GROUP)], win, ahead=False)[HALO:HALO + tm, :]
        pooled.append((s * (1.0 / win) - p[:, POOL_GROUP * g:POOL_GROUP * (g + 1)]).astype(BF16))
        mixed.append(_dot(pooled[-1], pool_w_ref[g].astype(BF16)))
    return pooled, mixed


def _meta_forward(meta, g1, w_in):
    def body(meta_ref, g1_ref, w_ref, a_ref, z_ref):
        hat, _ = _rms_stats(meta_ref[...])
        a = (hat * g1_ref[...]).astype(BF16)
        a_ref[...] = a
        z_ref[...] = _dot(a, w_ref[...])

    return pl.pallas_call(
        body, name="meta_forward",
        out_shape=[jax.ShapeDtypeStruct(meta.shape, BF16), jax.ShapeDtypeStruct((N_META, w_in.shape[1]), F32)],
        compiler_params=pltpu.CompilerParams(vmem_limit_bytes=VMEM_LIMIT_BYTES),
    )(meta, g1, w_in)


def _mixer_forward(x2d, z_meta, g1, w_in, conv_w, pool_w, pool_scale, w_out, g2, n_seq, to_gather):
    t, d = x2d.shape
    zw = w_in.shape[1]
    cw = zw // 4
    s = t // n_seq
    tm = min(TM_MIX, s)
    nj = s // tm
    ng = len(to_gather)

    def body(x_ref, zm_ref, g1_ref, win_ref, conv_ref, pw_ref, ps_ref, wout_ref, g2_ref, *rest):
        shards, (h1_ref, z_ref, m_ref, pooled_ref, mixed_ref), slabs = rest[:ng], rest[ng:ng + 5], rest[ng + 5:2 * ng + 5]
        ext_u, ext_p = rest[2 * ng + 5:2 * ng + 7]
        start, forward, finish = _gather_ops(shards, slabs, *rest[2 * ng + 7:])
        pl.when((pl.program_id(0) == 0) & (pl.program_id(1) == 0))(start)

        @pl.when(pl.program_id(1) == 0)
        def _():
            zm = zm_ref[...]
            ext_u[pl.ds(0, HALO), :] = zm[:, cw:2 * cw] * zm[:, 2 * cw:3 * cw]
            ext_p[pl.ds(0, HALO), :] = zm[:, 3 * cw:]

        h0 = x_ref[...]
        hat, _ = _rms_stats(h0)
        z = _dot((hat * g1_ref[...]).astype(BF16), win_ref[...])
        z_ref[...] = z.astype(BF16)
        b, _, _, _, _, _, yc = _conv_branch(z, ext_u, conv_ref, tm)
        pooled, mixed = _pool_branch(z[:, 3 * cw:], ext_p, pw_ref, tm)
        pooled_ref[...] = jnp.concatenate(pooled, axis=1)
        mixed_ref[...] = jnp.concatenate(mixed, axis=1).astype(BF16)
        ps = ps_ref[...]
        y = [b * yc] + [mixed[g] * ps[:, POOL_GROUP * g:POOL_GROUP * (g + 1)] for g in range(len(POOL_WINDOWS))]
        m = _dot(jnp.concatenate(y, axis=1).astype(BF16), wout_ref[...])
        m_ref[...] = m
        m_hat, _ = _rms_stats(m)
        h1_ref[...] = h0 + m_hat * g2_ref[...]
        ext_u[pl.ds(0, HALO), :] = ext_u[pl.ds(tm, HALO), :]
        ext_p[pl.ds(0, HALO), :] = ext_p[pl.ds(tm, HALO), :]

        @pl.when((pl.program_id(0) == n_seq - 1) & (pl.program_id(1) == nj - 1))
        def _():
            forward()
            finish()

    row = lambda b, j: (b * nj + j, 0)
    out = pl.pallas_call(
        body, name="mixer_forward", grid=(n_seq, nj),
        in_specs=[pl.BlockSpec((tm, d), row), _const(z_meta.shape), _const(g1.shape), _resident(w_in.shape), _const(conv_w.shape),
                  _const(pool_w.shape), _const(pool_scale.shape), _resident(w_out.shape), _const(g2.shape)] + [ANY] * ng,
        out_specs=[pl.BlockSpec((tm, d), row), pl.BlockSpec((tm, zw), row), pl.BlockSpec((tm, d), row), pl.BlockSpec((tm, cw), row),
                   pl.BlockSpec((tm, cw), row)] + [ANY] * ng,
        out_shape=[jax.ShapeDtypeStruct((t, d), F32), jax.ShapeDtypeStruct((t, zw), BF16), jax.ShapeDtypeStruct((t, d), F32),
                   jax.ShapeDtypeStruct((t, cw), BF16), jax.ShapeDtypeStruct((t, cw), BF16)]
        + [jax.ShapeDtypeStruct((N_DEV, *a.shape), a.dtype) for a in to_gather],
        scratch_shapes=[pltpu.VMEM((tm + HALO, cw), F32), pltpu.VMEM((tm + HALO, cw), F32)] + _exchange_sems(ng),
        compiler_params=_params("arbitrary", "arbitrary"),
    )(x2d, z_meta, g1, w_in, conv_w, pool_w, pool_scale, w_out, g2, *to_gather)
    return out[:5], out[5:]


def _gather_and_mixer_forward(x2d, mixer_shards, ffn_shards, g1, pool_w, pool_scale, g2, n_seq):
    t, d = x2d.shape
    zs, rs, ms, cs = mixer_shards[0].shape[1], mixer_shards[1].shape[0], mixer_shards[2].shape[1], mixer_shards[3].shape[1]
    zw, cw = N_DEV * zs, N_DEV * cs
    s = t // n_seq
    tm = min(TM_MIX, s)
    nj = s // tm
    n1, n2 = len(mixer_shards), len(ffn_shards)
    dtypes = [BF16, BF16, F32, F32] + [BF16] * n2
    shards = list(mixer_shards) + list(ffn_shards)

    def body(x_ref, *rest):
        shard_refs, (g1_ref, pw_ref, ps_ref, g2_ref), rest = rest[:n1 + n2], rest[n1 + n2:n1 + n2 + 4], rest[n1 + n2 + 4:]
        (h1_ref, z_ref, m_ref, pooled_ref, mixed_ref, win_o, wout_o, meta_o, conv_o, am_o, zm_o), rest = rest[:11], rest[11:]
        slabs, rest = rest[:n1 + n2], rest[n1 + n2:]
        stages, rest = rest[:n1 + n2], rest[n1 + n2:]
        win_v, wout_v, meta_v, conv_v, ext_u, ext_p, sem = rest[:7]
        first = _gather_ops(stages[:n1], slabs[:n1], *rest[7:10])
        later = _gather_ops(stages[n1:], slabs[n1:], *rest[10:13], core_major=True)

        @pl.when((pl.program_id(0) == 0) & (pl.program_id(1) == 0))
        def _():
            for src, dst in zip(shard_refs, stages):
                dst[...] = src[...].astype(dst.dtype)
            first[0]()
            later[0]()
            first[1]()
            first[2]()
            copies = [pltpu.make_async_copy(slabs[0].at[i], win_v.at[:, pl.ds(zs * i, zs)], sem.at[i]) for i in range(N_DEV)]
            copies += [pltpu.make_async_copy(slabs[1].at[i], wout_v.at[pl.ds(rs * i, rs), :], sem.at[N_DEV + i]) for i in range(N_DEV)]
            copies += [pltpu.make_async_copy(slabs[2], meta_v, sem.at[2 * N_DEV]), pltpu.make_async_copy(slabs[3], conv_v, sem.at[2 * N_DEV + 1])]
            for cp in copies:
                cp.start()
            for cp in copies:
                cp.wait()
            copies = [pltpu.make_async_copy(win_v, win_o, sem.at[0]), pltpu.make_async_copy(wout_v, wout_o, sem.at[1])]
            for cp in copies:
                cp.start()
            for i in range(N_DEV):
                meta_o[:, pl.ds(ms * i, ms)] = meta_v[i]
                conv_o[:, pl.ds(cs * i, cs)] = conv_v[i]
            hat, _ = _rms_stats(meta_o[...])
            a = (hat * g1_ref[...]).astype(BF16)
            am_o[...] = a
            zm_o[...] = _dot(a, win_v[...])
            for cp in copies:
                cp.wait()

        @pl.when(pl.program_id(1) == 0)
        def _():
            zm = zm_o[...]
            ext_u[pl.ds(0, HALO), :] = zm[:, cw:2 * cw] * zm[:, 2 * cw:3 * cw]
            ext_p[pl.ds(0, HALO), :] = zm[:, 3 * cw:]

        h0 = x_ref[...]
        hat, _ = _rms_stats(h0)
        z = _dot((hat * g1_ref[...]).astype(BF16), win_v[...])
        z_ref[...] = z.astype(BF16)
        b, _, _, _, _, _, yc = _conv_branch(z, ext_u, conv_o, tm)
        pooled, mixed = _pool_branch(z[:, 3 * cw:], ext_p, pw_ref, tm)
        pooled_ref[...] = jnp.concatenate(pooled, axis=1)
        mixed_ref[...] = jnp.concatenate(mixed, axis=1).astype(BF16)
        ps = ps_ref[...]
        y = [b * yc] + [mixed[g] * ps[:, POOL_GROUP * g:POOL_GROUP * (g + 1)] for g in range(len(POOL_WINDOWS))]
        m = _dot(jnp.concatenate(y, axis=1).astype(BF16), wout_v[...])
        m_ref[...] = m
        m_hat, _ = _rms_stats(m)
        h1_ref[...] = h0 + m_hat * g2_ref[...]
        ext_u[pl.ds(0, HALO), :] = ext_u[pl.ds(tm, HALO), :]
        ext_p[pl.ds(0, HALO), :] = ext_p[pl.ds(tm, HALO), :]

        @pl.when((pl.program_id(0) == n_seq - 1) & (pl.program_id(1) == nj - 1))
        def _():
            later[1]()
            later[2]()

    row = lambda b, j: (b * nj + j, 0)
    vmem = pl.BlockSpec(memory_space=pltpu.VMEM)
    small = [(N_META, d), (CONV_WIDTH, cw), (N_META, d), (N_META, zw)]
    out = pl.pallas_call(
        body, name="gather_and_mixer_forward", grid=(n_seq, nj),
        in_specs=[pl.BlockSpec((tm, d), row)] + [vmem] * (n1 + n2)
        + [_const(g1.shape), _const(pool_w.shape), _const(pool_scale.shape), _const(g2.shape)],
        out_specs=[pl.BlockSpec((tm, d), row), pl.BlockSpec((tm, zw), row), pl.BlockSpec((tm, d), row), pl.BlockSpec((tm, cw), row),
                   pl.BlockSpec((tm, cw), row), ANY, ANY] + [_const(sh) for sh in small] + [ANY] * (n1 + n2),
        out_shape=[jax.ShapeDtypeStruct((t, d), F32), jax.ShapeDtypeStruct((t, zw), BF16), jax.ShapeDtypeStruct((t, d), F32),
                   jax.ShapeDtypeStruct((t, cw), BF16), jax.ShapeDtypeStruct((t, cw), BF16),
                   jax.ShapeDtypeStruct((d, zw), BF16), jax.ShapeDtypeStruct((d, d), BF16),
                   jax.ShapeDtypeStruct(small[0], F32), jax.ShapeDtypeStruct(small[1], F32), jax.ShapeDtypeStruct(small[2], BF16),
                   jax.ShapeDtypeStruct(small[3], F32)]
        + [jax.ShapeDtypeStruct((N_DEV, *a.shape), dt) for a, dt in zip(shards, dtypes)],
        scratch_shapes=[pltpu.VMEM(a.shape, dt) for a, dt in zip(shards, dtypes)]
        + [pltpu.VMEM((d, zw), BF16), pltpu.VMEM((d, d), BF16), pltpu.VMEM((N_DEV, N_META, ms), F32),
           pltpu.VMEM((N_DEV, CONV_WIDTH, cs), F32), pltpu.VMEM((tm + HALO, cw), F32), pltpu.VMEM((tm + HALO, cw), F32),
           pltpu.SemaphoreType.DMA((2 * N_DEV + 2,))] + _exchange_sems(n1) + _exchange_sems(n2),
        compiler_params=_params("arbitrary", "arbitrary"),
    )(x2d, *shards, g1, pool_w, pool_scale, g2)
    return out[:5], out[5:11], out[11 + n1:]


def _mixer_backward(x2d, dh1, m, z, pooled, mixed, meta, a_meta, z_meta, g1, w_in, conv_w, pool_w, pool_scale, w_out, g2, n_seq,
                    to_exchange, landing):
    t, d = x2d.shape
    zw = w_in.shape[1]
    cw = zw // 4
    s = t // n_seq
    tm = min(TM_MIX, s)
    nj = s // tm
    n_groups = len(POOL_WINDOWS)
    zs = zw // N_DEV
    nx = len(to_exchange)
    n_in = 17
    given = [k for k, a in enumerate(landing) if a is not None]
    fresh = [k for k, a in enumerate(landing) if a is None]

    def body(x_ref, dh1_ref, m_ref, z_ref, zprev_ref, pooled_ref, mixed_ref, meta_ref, am_ref, zm_ref, g1_ref, win_ref, conv_ref, pw_ref, ps_ref, wout_ref,
             g2_ref, *rest):
        sent, rest = rest[:nx], rest[nx + len(given):]
        gx_ref, dwin_ref, dwout_ref, dg1_ref, dg2_ref, dconv_ref, dpw_ref, dps_ref, dmeta_ref = rest[:9]
        landed, rest = rest[9:9 + nx], rest[9 + nx:]
        ext_u, ext_dyc, ext_dq, acc_win, acc_wout, stage16, sem = rest[:7]
        north = _core_exchange_ops(sent, landed, 1, *rest[7:10])
        south = _core_exchange_ops([sent[k] for k in fresh], [landed[k] for k in fresh], 0, *rest[10:13])

        def start():
            north[0]()
            south[0]()

        def finish():
            south[1]()
            north[1]()

        b_id, j = pl.program_id(0), pl.program_id(1)
        jr = nj - 1 - j
        pl.when((b_id == 0) & (j == 0))(start)

        @pl.when((b_id == 0) & (j == 0))
        def _():
            acc_win[...] = jnp.zeros_like(acc_win)
            acc_wout[...] = jnp.zeros_like(acc_wout)
            for r in (dg1_ref, dg2_ref, dconv_ref, dpw_ref, dps_ref, dmeta_ref):
                r[...] = jnp.zeros_like(r)

        @pl.when(j == 0)
        def _():
            ext_dyc[pl.ds(tm, HALO), :] = jnp.zeros((HALO, cw), F32)
            ext_dq[pl.ds(tm, HALO), :] = jnp.zeros((HALO, cw), F32)

        zm = zm_ref[...]
        halo = jnp.where(jr == 0, zm, zprev_ref[...].astype(F32))
        ext_u[pl.ds(0, HALO), :] = halo[:, cw:2 * cw] * halo[:, 2 * cw:3 * cw]

        h0 = x_ref[...]
        hat0, rstd0 = _rms_stats(h0)
        g1 = g1_ref[...]
        a = (hat0 * g1).astype(BF16)
        b, c, v, u, u1, u2, yc = _conv_branch(z_ref[...].astype(F32), ext_u, conv_ref, tm)
        mixed = [mixed_ref[:, pl.ds(POOL_GROUP * g, POOL_GROUP)].astype(F32) for g in range(n_groups)]
        ps = ps_ref[...]
        y = [b * yc] + [mixed[g] * ps[:, POOL_GROUP * g:POOL_GROUP * (g + 1)] for g in range(n_groups)]
        ycat = jnp.concatenate(y, axis=1).astype(BF16)

        dh1v = dh1_ref[...]
        m_hat, m_rstd = _rms_stats(m_ref[...])
        dm, dg2 = _rms_bwd(m_hat, m_rstd, g2_ref[...], dh1v)
        dg2_ref[...] += dg2
        dm = dm.astype(BF16)
        acc_wout[...] += _dot_tn(ycat, dm)
        dycat = _dot_nt(dm, wout_ref[...])

        dyconv = dycat[:, :cw]
        db = dyconv * yc
        dyc = dyconv * b
        ext_dyc[pl.ds(0, tm), :] = dyc
        du = (conv_ref[pl.ds(2, 1), :] * dyc + conv_ref[pl.ds(1, 1), :] * ext_dyc[pl.ds(1, tm), :]
              + conv_ref[pl.ds(0, 1), :] * ext_dyc[pl.ds(2, tm), :])
        dconv_ref[pl.ds(2, 1), :] += jnp.sum(dyc * u, axis=0, keepdims=True)
        dconv_ref[pl.ds(1, 1), :] += jnp.sum(dyc * u1, axis=0, keepdims=True)
        dconv_ref[pl.ds(0, 1), :] += jnp.sum(dyc * u2, axis=0, keepdims=True)

        dp = []
        for g, win in enumerate(POOL_WINDOWS):
            lanes = pl.ds(POOL_GROUP * g, POOL_GROUP)
            dypool = dycat[:, cw + POOL_GROUP * g:cw + POOL_GROUP * (g + 1)]
            dps_ref[:, lanes] += jnp.sum(dypool * mixed[g], axis=0, keepdims=True)
            dmixed = (dypool * ps[:, POOL_GROUP * g:POOL_GROUP * (g + 1)]).astype(BF16)
            dpw_ref[g] += _dot_tn(pooled_ref[:, lanes], dmixed)
            dq = _dot_nt(dmixed, pw_ref[g].astype(BF16))
            ext_dq[pl.ds(0, tm), lanes] = dq
            acc = _window_sum(ext_dq[:, lanes], win, ahead=True)[0:tm, :]
            dp.append(acc * (1.0 / win) - dq)

        dz = jnp.concatenate([db, du * v, du * c] + dp, axis=1).astype(BF16)
        acc_win[...] += _dot_tn(a, dz)
        dh0, dg1 = _rms_bwd(hat0, rstd0, g1, _dot_nt(dz, win_ref[...]))
        dg1_ref[...] += dg1
        gx_ref[...] = dh1v + dh0

        ext_dyc[pl.ds(tm, HALO), :] = ext_dyc[pl.ds(0, HALO), :]
        ext_dq[pl.ds(tm, HALO), :] = ext_dq[pl.ds(0, HALO), :]

        @pl.when(jr == 0)
        def _():
            ext_dyc[pl.ds(tm - HALO, HALO), :] = jnp.zeros((HALO, cw), F32)
            ext_dq[pl.ds(tm - HALO, HALO), :] = jnp.zeros((HALO, cw), F32)
            du_m = (conv_ref[pl.ds(1, 1), :] * ext_dyc[pl.ds(tm - HALO + 1, HALO), :]
                    + conv_ref[pl.ds(0, 1), :] * ext_dyc[pl.ds(tm - HALO + 2, HALO), :])
            dp_m = []
            for g, win in enumerate(POOL_WINDOWS):
                lanes = pl.ds(POOL_GROUP * g, POOL_GROUP)
                acc = ext_dq[pl.ds(tm - HALO + 1, HALO), lanes]
                for k in range(2, win):
                    acc = acc + ext_dq[pl.ds(tm - HALO + k, HALO), lanes]
                dp_m.append(acc * (1.0 / win))
            dz_m = jnp.concatenate([jnp.zeros((HALO, cw), F32), du_m * zm[:, 2 * cw:3 * cw], du_m * zm[:, cw:2 * cw]] + dp_m,
                                   axis=1).astype(BF16)
            acc_win[...] += _dot_tn(am_ref[...], dz_m)
            hat_m, rstd_m = _rms_stats(meta_ref[...])
            dmeta, dg1_m = _rms_bwd(hat_m, rstd_m, g1, _dot_nt(dz_m, win_ref[...]))
            dg1_ref[...] += dg1_m
            dmeta_ref[...] += dmeta

        @pl.when((b_id == n_seq - 1) & (j == nj - 1))
        def _():
            pieces = [(acc_win, zs * i, dwin_ref.at[i]) for i in range(N_DEV)]
            pieces += [(acc_wout, zs * i, dwout_ref.at[:, pl.ds(zs * i, zs)]) for i in range(d // zs)]
            copies = []
            for k, (acc, col, dst) in enumerate(pieces):
                if k >= 2:
                    copies[k - 2].wait()
                stage16[k % 2] = acc[:, pl.ds(col, zs)].astype(BF16)
                copies.append(pltpu.make_async_copy(stage16.at[k % 2], dst, sem.at[k % 2]))
                copies[k].start()
            copies[-2].wait()
            copies[-1].wait()
            finish()

    row = lambda b, j: (b * nj + nj - 1 - j, 0)
    prev = lambda b, j: (jnp.maximum((b * s + (nj - 1 - j) * tm) // HALO - 1, 0), 0)
    small = [g1.shape, g2.shape, conv_w.shape, pool_w.shape, pool_scale.shape, meta.shape]
    out = pl.pallas_call(
        body, name="mixer_backward", grid=(n_seq, nj),
        in_specs=[pl.BlockSpec((tm, d), row), pl.BlockSpec((tm, d), row), pl.BlockSpec((tm, d), row), pl.BlockSpec((tm, zw), row),
                  pl.BlockSpec((HALO, zw), prev), pl.BlockSpec((tm, cw), row), pl.BlockSpec((tm, cw), row), _const(meta.shape), _const(a_meta.shape), _const(z_meta.shape), _const(g1.shape),
                  _resident(w_in.shape), _const(conv_w.shape), _const(pool_w.shape), _const(pool_scale.shape), _resident(w_out.shape),
                  _const(g2.shape)] + [ANY] * (nx + len(given)),
        out_specs=[pl.BlockSpec((tm, d), row), ANY, ANY] + [_const(sh) for sh in small] + [ANY] * nx,
        out_shape=[jax.ShapeDtypeStruct((t, d), F32), jax.ShapeDtypeStruct((N_DEV, d, zs), BF16),
                   jax.ShapeDtypeStruct(w_out.shape, BF16)] + [jax.ShapeDtypeStruct(sh, F32) for sh in small]
        + [jax.ShapeDtypeStruct((N_DEV, a.shape[0] // N_DEV, a.shape[1]), a.dtype) for a in to_exchange],
        input_output_aliases={n_in + nx + at: 9 + k for at, k in enumerate(given)},
        scratch_shapes=[pltpu.VMEM((tm + HALO, cw), F32)] * 3
        + [pltpu.VMEM(w_in.shape, F32), pltpu.VMEM(w_out.shape, F32), pltpu.VMEM((2, d, zs), BF16),
           pltpu.SemaphoreType.DMA((2,))] + _core_exchange_sems(nx) + _core_exchange_sems(len(fresh)),
        compiler_params=_params("arbitrary", "arbitrary"),
    )(x2d, dh1, m, z, z, pooled, mixed, meta, a_meta, z_meta, g1, w_in, conv_w, pool_w, pool_scale, w_out, g2, *to_exchange, *[landing[k] for k in given])
    return out[:9], out[9:]


def _ffn_forward_backward(h1, target, g3, w_gate, w_up, w_down, g4):
    t, d = h1.shape
    ff = w_gate.shape[0]
    tm = min(TM_FFN, t)
    nt = t // tm
    chunks = [(s, min(FFN_CHUNK, ff - s)) for s in range(0, ff, FFN_CHUNK)]

    def body(h1_ref, h1pp_ref, tgt_ref, g3_ref, wg_ref, wu_ref, wd_ref, g4_ref,
             f_ref, act_ref, dd_ref, dgate_ref, dup_ref, dh1_ref, loss_ref, dg3_ref, dg4_ref, *slots):
        gate_s, up_s, dd_s, dh2_s, df_s = slots
        i = pl.program_id(0)

        def forward(slot):
            h1v = h1_ref[...]
            hat, _ = _rms_stats(h1v)
            f = (hat * g3_ref[...]).astype(BF16)
            f_ref[...] = f
            s, n = chunks[0]
            gate, up = _dot_nt(f_ref[...], wg_ref[pl.ds(s, n), :]), _dot_nt(f_ref[...], wu_ref[pl.ds(s, n), :])
            yield
            down = None
            for k, (s, n) in enumerate(chunks):
                gate_s.at[slot][:, pl.ds(s, n)] = gate.astype(BF16)
                up_s.at[slot][:, pl.ds(s, n)] = up.astype(BF16)
                act = (gate * jax.nn.sigmoid(gate) * up).astype(BF16)
                act_ref[:, pl.ds(s, n)] = act
                if k + 1 < len(chunks):
                    s1, n1 = chunks[k + 1]
                    gate, up = _dot_nt(f_ref[...], wg_ref[pl.ds(s1, n1), :]), _dot_nt(f_ref[...], wu_ref[pl.ds(s1, n1), :])
                yield
                part = _dot(act_ref[:, pl.ds(s, n)], wd_ref[pl.ds(s, n), :])
                down = part if down is None else down + part
                yield
            d_hat, d_rstd = _rms_stats(down)
            g4 = g4_ref[...]
            err = h1v + d_hat * g4 - tgt_ref[...]
            loss_ref[...] += jnp.sum(err * err) * (0.5 / d)
            dh2 = err * (1.0 / d)
            dh2_s.at[slot][...] = dh2
            dd, dg4 = _rms_bwd(d_hat, d_rstd, g4, dh2)
            dg4_ref[...] += dg4
            dd = dd.astype(BF16)
            dd_ref[...] = dd
            dd_s.at[slot][...] = dd

        def backward(slot):
            s, n = chunks[0]
            dact = _dot_nt(dd_s.at[slot][...], wd_ref[pl.ds(s, n), :])
            yield
            df = None
            for k, (s, n) in enumerate(chunks):
                gate = gate_s.at[slot][:, pl.ds(s, n)].astype(F32)
                up = up_s.at[slot][:, pl.ds(s, n)].astype(F32)
                sig = jax.nn.sigmoid(gate)
                dup = (dact * (gate * sig)).astype(BF16)
                dgate = (dact * up * (sig * (1.0 + gate * (1.0 - sig)))).astype(BF16)
                dup_ref[:, pl.ds(s, n)] = dup
                dgate_ref[:, pl.ds(s, n)] = dgate
                if k + 1 < len(chunks):
                    s1, n1 = chunks[k + 1]
                    dact = _dot_nt(dd_s.at[slot][...], wd_ref[pl.ds(s1, n1), :])
                yield
                part = _dot(dgate_ref[:, pl.ds(s, n)], wg_ref[pl.ds(s, n), :]) + _dot(dup_ref[:, pl.ds(s, n)], wu_ref[pl.ds(s, n), :])
                df = part if df is None else df + part
                yield
            df_s.at[slot][...] = df

        def last(slot):
            hat, rstd = _rms_stats(h1pp_ref[...])
            dh1, dg3 = _rms_bwd(hat, rstd, g3_ref[...], df_s.at[slot][...])
            dg3_ref[...] += dg3
            dh1_ref[...] = dh2_s.at[slot][...] + dh1

        def emit(parity, with_forward, with_backward, with_last):
            fwd = forward(parity) if with_forward else iter(())
            bwd = backward(1 - parity) if with_backward else iter(())
            next(fwd, None)
            if with_last:
                last(parity)
            for _ in range(FFN_BACKWARD_LAG):
                next(fwd, None)
            alive = True
            while alive:
                alive = next(bwd, True) is None
                alive = (next(fwd, True) is None) or alive

        @pl.when(i == 0)
        def _():
            for r in (loss_ref, dg3_ref, dg4_ref, *slots):
                r[...] = jnp.zeros_like(r)

        @pl.when(i < nt)
        def _():
            emit(i % 2, True, True, True)

        @pl.when(i == nt)
        def _():
            emit(nt % 2, False, True, True)

        @pl.when(i == nt + 1)
        def _():
            emit((nt + 1) % 2, False, False, True)

    cur = lambda i: (jnp.minimum(i, nt - 1), 0)
    prev = lambda i: (jnp.clip(i - 1, 0, nt - 1), 0)
    prev2 = lambda i: (jnp.clip(i - 2, 0, nt - 1), 0)
    return pl.pallas_call(
        body, name="ffn_forward_backward", grid=(nt + 2,),
        in_specs=[pl.BlockSpec((tm, d), cur), pl.BlockSpec((tm, d), prev2), pl.BlockSpec((tm, d), cur), _const(g3.shape),
                  _resident(w_gate.shape), _resident(w_up.shape), _resident(w_down.shape), _const(g4.shape)],
        out_specs=[pl.BlockSpec((tm, d), cur), pl.BlockSpec((tm, ff), cur), pl.BlockSpec((tm, d), cur), pl.BlockSpec((tm, ff), prev),
                   pl.BlockSpec((tm, ff), prev), pl.BlockSpec((tm, d), prev2), _const((8, 128)), _const(g3.shape), _const(g4.shape)],
        out_shape=[jax.ShapeDtypeStruct((t, d), BF16), jax.ShapeDtypeStruct((t, ff), BF16), jax.ShapeDtypeStruct((t, d), BF16),
                   jax.ShapeDtypeStruct((t, ff), BF16), jax.ShapeDtypeStruct((t, ff), BF16), jax.ShapeDtypeStruct((t, d), F32),
                   jax.ShapeDtypeStruct((8, 128), F32), jax.ShapeDtypeStruct(g3.shape, F32), jax.ShapeDtypeStruct(g4.shape, F32)],
        scratch_shapes=[pltpu.VMEM((2, tm, ff), BF16)] * 2 + [pltpu.VMEM((2, tm, d), BF16)] + [pltpu.VMEM((2, tm, d), F32)] * 2,
        compiler_params=_params("arbitrary"),
    )(h1, h1, target, g3, w_gate, w_up, w_down, g4)


def _ffn_weight_grads(f, dd, dgate, dup, act):
    t, d = f.shape
    ff = dgate.shape[1]
    tm = min(TM_WGRAD, t)
    nt = t // tm
    fc = ff // FF_CHUNKS
    assert FF_CHUNKS == 2

    def body(f_ref, dd_ref, dgate_ref, dup_ref, act_ref, dwg_ref, dwu_ref, dwd_ref, *rest):
        landing, (acc_g, acc_u, acc_d, stage, sem) = rest[:2], rest[2:7]
        start, finish = _core_exchange_ops([dwg_ref, dwd_ref], landing, 0, *rest[7:])
        c, i = pl.program_id(0), pl.program_id(1)
        pl.when((c == 1) & (i == 0))(start)

        @pl.when(i == 0)
        def _():
            acc_g[...] = jnp.zeros_like(acc_g)
            acc_u[...] = jnp.zeros_like(acc_u)
            acc_d[...] = jnp.zeros_like(acc_d)

        fv = f_ref[...]
        acc_g[...] += _dot_tn(fv, dgate_ref[...])
        acc_u[...] += _dot_tn(fv, dup_ref[...])
        acc_d[...] += _dot_tn(act_ref[...], dd_ref[...])

        @pl.when(i == nt - 1)
        def _():
            rows = pl.ds(pl.multiple_of(c * fc, 16), fc)
            copies = []
            for k, (acc, out, transposed) in enumerate(((acc_d, dwd_ref, False), (acc_g, dwg_ref, True), (acc_u, dwu_ref, True))):
                if k >= 2:
                    copies[k - 2].wait()
                stage[k % 2] = (acc[...].T if transposed else acc[...]).astype(BF16)
                copies.append(pltpu.make_async_copy(stage.at[k % 2], out.at[rows, :], sem.at[k % 2]))
                copies[k].start()
            copies[-2].wait()
            copies[-1].wait()

        pl.when((c == 1) & (i == nt - 1))(finish)

    row = lambda c, i: (i, 0)
    col = lambda c, i: (i, c)
    out = pl.pallas_call(
        body, name="ffn_weight_grads", grid=(FF_CHUNKS, nt),
        in_specs=[pl.BlockSpec((tm, d), row), pl.BlockSpec((tm, d), row), pl.BlockSpec((tm, fc), col), pl.BlockSpec((tm, fc), col),
                  pl.BlockSpec((tm, fc), col)],
        out_specs=[ANY] * 5,
        out_shape=[jax.ShapeDtypeStruct((ff, d), BF16)] * 3 + [jax.ShapeDtypeStruct((N_DEV, ff // N_DEV, d), BF16)] * 2,
        scratch_shapes=[pltpu.VMEM((d, fc), F32), pltpu.VMEM((d, fc), F32), pltpu.VMEM((fc, d), F32), pltpu.VMEM((2, fc, d), BF16),
                        pltpu.SemaphoreType.DMA((2,))] + _core_exchange_sems(2),
        compiler_params=_params("arbitrary", "arbitrary"),
    )(f, dd, dgate, dup, act)
    return out[:3], [out[3], None, out[4]]


def _adamw(w, g, m, v):
    m = ADAM_B1 * m + (1.0 - ADAM_B1) * g
    v = ADAM_B2 * v + (1.0 - ADAM_B2) * (g * g)
    m_hat = m / (1.0 - ADAM_B1 ** ADAM_STEP)
    v_hat = v / (1.0 - ADAM_B2 ** ADAM_STEP)
    return -ADAM_LR * (m_hat / (jnp.sqrt(v_hat) + ADAM_EPS) + ADAM_WD * w), m, v


def _sum_slabs(ref):
    total = ref[0].astype(F32)
    for i in range(1, ref.shape[0]):
        total = total + ref[i].astype(F32)
    return total


def _adamw_rows(r, c):
    tr = r
    for cand in range(8, r, 8):
        if r % cand == 0 and cand * c <= ADAMW_BLOCK_ELEMS:
            tr = cand
    return r if r * c <= ADAMW_BLOCK_ELEMS else tr


def _reduce_adamw_carrying(parts, ws, ms, vs, to_reduce, to_exchange, whole):
    k, nr, nx = len(ws), len(to_reduce), len(to_exchange)
    r, c = ws[0].shape
    tr = _adamw_rows(r, c)
    steps = r // tr
    chip_slabs = [jax.ShapeDtypeStruct((N_CHIP, *a.shape[1:]), a.dtype) for a in to_reduce]

    def body(*refs):
        p_refs, w_refs, m_refs, v_refs = (refs[a * k:(a + 1) * k] for a in range(4))
        refs = refs[4 * k:]
        reduced_in, sent, refs = refs[:nr], refs[nr:nr + nx], refs[nr + nx:]
        outs, pairs, sums, landed, refs = refs[:4 * k], refs[4 * k:4 * k + nr], refs[4 * k + nr:4 * k + 2 * nr], \
            refs[4 * k + 2 * nr:4 * k + 2 * nr + nx], refs[4 * k + 2 * nr + nx:]
        mine_v, pair_v, sum_v, refs = refs[:nr], refs[nr:2 * nr], refs[2 * nr:3 * nr], refs[3 * nr:]
        reduce_ops = _pair_then_chip_ops(reduced_in, pairs, sums, mine_v, pair_v, sum_v, *refs[:7])
        direct_ops = _exchange_ops(sent, landed, whole, *refs[7:])

        @pl.when(pl.program_id(0) == 0)
        def _():
            direct_ops[0]()
            reduce_ops[0]()

        for a in range(k):
            g = _sum_slabs(p_refs[a])
            outs[4 * a][...] = g
            outs[4 * a + 1][...], outs[4 * a + 2][...], outs[4 * a + 3][...] = _adamw(w_refs[a][...], g, m_refs[a][...], v_refs[a][...])

        @pl.when(pl.program_id(0) == steps - 1)
        def _():
            reduce_ops[1]()
            direct_ops[1]()

    blk = pl.BlockSpec((tr, c), lambda i: (i, 0))
    out = pl.pallas_call(
        body, name="adamw_ffn_exchange_rest", grid=(steps,),
        in_specs=[pl.BlockSpec((N_DEV, tr, c), lambda i: (0, i, 0))] * k + [blk] * (3 * k) + [ANY] * (nr + nx),
        out_specs=[blk] * (4 * k) + [ANY] * (2 * nr + nx),
        out_shape=[jax.ShapeDtypeStruct((r, c), F32)] * (4 * k) + chip_slabs + chip_slabs
        + [jax.ShapeDtypeStruct((N_DEV, *a.shape) if w else a.shape, a.dtype) for a, w in zip(to_exchange, whole)],
        scratch_shapes=[pltpu.VMEM(a.shape, a.dtype) for a in chip_slabs] * 3 + _pair_then_chip_sems(nr) + _exchange_sems(nx),
        compiler_params=_params("arbitrary"),
    )(*parts, *ws, *ms, *vs, *to_reduce, *to_exchange)
    return [tuple(out[4 * a:4 * a + 4]) for a in range(k)], out[4 * k + nr:4 * k + 2 * nr], out[4 * k + 2 * nr:]


def _reduce_adamw(parts, w, m, v, name):
    r, c = w.shape
    tr = _adamw_rows(r, c)

    def body(p_ref, w_ref, m_ref, v_ref, g_out, d_out, m_out, v_out):
        g = _sum_slabs(p_ref)
        g_out[...] = g
        d_out[...], m_out[...], v_out[...] = _adamw(w_ref[...], g, m_ref[...], v_ref[...])

    blk = pl.BlockSpec((tr, c), lambda i: (i, 0))
    return pl.pallas_call(
        body, name=name, grid=(r // tr,),
        in_specs=[pl.BlockSpec((parts.shape[0], tr, c), lambda i: (0, i, 0)), blk, blk, blk],
        out_specs=[blk] * 4, out_shape=[jax.ShapeDtypeStruct((r, c), F32)] * 4,
        compiler_params=_params("arbitrary"),
    )(parts, w, m, v)


def _reduce_adamw_small(parts, ws, ms, vs, loss_parts):
    n = len(parts)

    def body(*refs):
        p_refs, w_refs, m_refs, v_refs = (refs[k * n:(k + 1) * n] for k in range(4))
        outs = refs[4 * n + 1:]
        outs[4 * n][...] = _sum_slabs(refs[4 * n])
        for a in range(n):
            g = _sum_slabs(p_refs[a])
            outs[4 * a][...] = g
            outs[4 * a + 1][...], outs[4 * a + 2][...], outs[4 * a + 3][...] = _adamw(w_refs[a][...], g, m_refs[a][...], v_refs[a][...])

    out = pl.pallas_call(
        body, name="adamw_replicated",
        out_shape=[jax.ShapeDtypeStruct(w.shape, F32) for w in ws for _ in range(4)] + [jax.ShapeDtypeStruct(loss_parts.shape[1:], F32)],
        compiler_params=pltpu.CompilerParams(vmem_limit_bytes=VMEM_LIMIT_BYTES),
    )(*parts, *ws, *ms, *vs, loss_parts)
    return [tuple(out[4 * a:4 * a + 4]) for a in range(n)], out[4 * n]


def kernel(x, meta_tokens, norm_mix_pre, w_in, conv_w, pool_w, pool_scale, w_out, norm_mix_post, norm_ffn_pre, w_gate, w_up, w_down, norm_ffn_post, loss_target, m_meta_tokens, m_norm_mix_pre, m_w_in, m_conv_w, m_pool_w, m_pool_scale, m_w_out, m_norm_mix_post, m_norm_ffn_pre, m_w_gate, m_w_up, m_w_down, m_norm_ffn_post, v_meta_tokens, v_norm_mix_pre, v_w_in, v_conv_w, v_pool_w, v_pool_scale, v_w_out, v_norm_mix_post, v_norm_ffn_pre, v_w_gate, v_w_up, v_w_down, v_norm_ffn_post):
    n_seq, seq, d = x.shape
    x2d = x.reshape(n_seq * seq, d)
    target = loss_target.reshape(n_seq * seq, d)

    t_ = lambda a: jnp.swapaxes(a[0], 0, 1)
    pw, ps = pool_w[0], pool_scale

    (h1, z, m, pooled, mixed), (win_b, wout_b, meta, conv, a_meta, z_meta), ffn_slabs = _gather_and_mixer_forward(
        x2d, [w_in[0], w_out[0], meta_tokens, conv_w[0]], [t_(w_gate), t_(w_up), w_down[0]], norm_mix_pre, pw, ps, norm_mix_post, n_seq)
    wg_b, wu_b, wd_b = (s.reshape(-1, d) for s in ffn_slabs)
    f, act, dd, dgate, dup, dh1, loss_sum, dg3, dg4 = _ffn_forward_backward(h1, target, norm_ffn_pre, wg_b, wu_b, wd_b, norm_ffn_post)
    ffn_grads, landing = _ffn_weight_grads(f, dd, dgate, dup, act)
    (gx, dwin, dwout, dg1, dg2, dconv, dpw, dps, dmeta), ffn_parts = _mixer_backward(
        x2d, dh1, m, z, pooled, mixed, meta, a_meta, z_meta, norm_mix_pre, win_b, conv, pw, ps, wout_b, norm_mix_post, n_seq,
        ffn_grads, landing)

    dmeta_s = jnp.transpose(dmeta.reshape(N_META, N_DEV, -1), (1, 0, 2))
    dconv_s = jnp.transpose(dconv.reshape(CONV_WIDTH, N_DEV, -1), (1, 0, 2))
    ffn_res, (win_parts, wout_parts), last = _reduce_adamw_carrying(
        ffn_parts, [t_(w_gate), t_(w_up), w_down[0]], [t_(m_w_gate), t_(m_w_up), m_w_down[0]], [t_(v_w_gate), t_(v_w_up), v_w_down[0]],
        [dwin, dwout.reshape(N_DEV, -1, d)], [dmeta_s, dconv_s, dg1, dg2, dg3, dg4, dpw, dps, loss_sum], [False] * 2 + [True] * 7)
    replicated = last[2:8]

    names = ["meta_tokens", "norm_mix_pre", "w_in", "conv_w", "pool_w", "pool_scale", "w_out", "norm_mix_post", "norm_ffn_pre", "w_gate",
             "w_up", "w_down", "norm_ffn_post"]
    res = {"w_gate": tuple(jnp.swapaxes(o, 0, 1)[None] for o in ffn_res[0]),
           "w_up": tuple(jnp.swapaxes(o, 0, 1)[None] for o in ffn_res[1]), "w_down": tuple(o[None] for o in ffn_res[2])}
    for nm, parts, w, m_, v_ in (("w_in", win_parts, w_in, m_w_in, v_w_in), ("w_out", wout_parts, w_out, m_w_out, v_w_out),
                                 ("conv_w", last[1], conv_w, m_conv_w, v_conv_w)):
        res[nm] = tuple(o[None] for o in _reduce_adamw(parts, w[0], m_[0], v_[0], "adamw_" + nm))
    res["meta_tokens"] = tuple(_reduce_adamw(last[0], meta_tokens, m_meta_tokens, v_meta_tokens, "adamw_meta_tokens"))
    small, loss = _reduce_adamw_small(
        replicated, [norm_mix_pre, norm_mix_post, norm_ffn_pre, norm_ffn_post, pool_w[0], pool_scale],
        [m_norm_mix_pre, m_norm_mix_post, m_norm_ffn_pre, m_norm_ffn_post, m_pool_w[0], m_pool_scale],
        [v_norm_mix_pre, v_norm_mix_post, v_norm_ffn_pre, v_norm_ffn_post, v_pool_w[0], v_pool_scale], last[8])
    for nm, r in zip(["norm_mix_pre", "norm_mix_post", "norm_ffn_pre", "norm_ffn_post", "pool_w", "pool_scale"], small):
        res[nm] = tuple(o[None] for o in r) if nm == "pool_w" else r

    return (loss[0, 0], gx.reshape(n_seq, seq, d), *[res[nm][0] for nm in names], *[res[nm][1] for nm in names],
            *[res[nm][2] for nm in names], *[res[nm][3] for nm in names])
```

```python
import functools

import jax
import jax.numpy as jnp
from jax import lax
from jax.experimental import pallas as pl
from jax.experimental.pallas import tpu as pltpu

F32, BF16 = jnp.float32, jnp.bfloat16
RMS_EPS = 1e-6
N_META = 16
CONV_WIDTH = 3
POOL_WINDOWS = (2, 4, 8, 16)
POOL_GROUP = 128
HALO = 16
N_DEV = 8
MESH_AXES = ("x", "y", "c")
MESH = pl.DeviceIdType.MESH
VMEM_LIMIT_BYTES = 56 * 1024 * 1024
ADAMW_BLOCK_ELEMS = 64 * 1024
TM_MIX = 512
TM_FFN = 256
FFN_CHUNK = 512
FFN_BACKWARD_LAG = 2
TM_WGRAD = 512
FF_CHUNKS = 2

ADAM_LR, ADAM_B1, ADAM_B2, ADAM_EPS, ADAM_WD, ADAM_STEP = 0.001, 0.9, 0.999, 1e-08, 0.01, 10


def _dot(a, b):
    return jnp.dot(a, b, preferred_element_type=F32)


def _dot_nt(a, b):
    return lax.dot_general(a, b, (((1,), (1,)), ((), ())), preferred_element_type=F32)


def _dot_tn(a, b):
    return lax.dot_general(a, b, (((0,), (0,)), ((), ())), preferred_element_type=F32)


def _rms_stats(h):
    rstd = lax.rsqrt(jnp.mean(h * h, axis=-1, keepdims=True) + RMS_EPS)
    return h * rstd, rstd


def _rms_bwd(hat, rstd, g, dy):
    gdy = dy * g
    proj = jnp.mean(gdy * hat, axis=-1, keepdims=True)
    return rstd * (gdy - hat * proj), jnp.sum(dy * hat, axis=0, keepdims=True)


def _params(*semantics):
    return pltpu.CompilerParams(dimension_semantics=semantics or None, vmem_limit_bytes=VMEM_LIMIT_BYTES)


def _resident(shape):
    zeros = (0,) * len(shape)
    return pl.BlockSpec(shape, lambda *_: zeros, pipeline_mode=pl.Buffered(1))


def _const(shape):
    zeros = (0,) * len(shape)
    return pl.BlockSpec(shape, lambda *_: zeros)


ANY = pl.BlockSpec(memory_space=pl.ANY)


def _my_place():
    x, y, c = (lax.axis_index(a) for a in MESH_AXES)
    return x, y, c


def _exchange_sems(n):
    return [pltpu.SemaphoreType.DMA((n, N_DEV - 1)), pltpu.SemaphoreType.DMA((n, N_DEV - 1)), pltpu.SemaphoreType.DMA((n,))]


def _gather_ops(srcs, outs, send_sems, recv_sems, local_sems, core_major=False):
    n = len(srcs)
    x, y, c = _my_place()
    me, sibling = (x, y, c), (x, y, 1 - c)
    chips = [(1 - x, y), (x, 1 - y), (1 - x, 1 - y)]

    def slab(px, py, pc):
        return 4 * pc + 2 * px + py if core_major else 4 * px + 2 * py + pc

    def copy(a, k, block, to, src=None):
        dst = outs[a].at[slab(*block)]
        return pltpu.make_async_remote_copy(
            src_ref=dst if src is None else src, dst_ref=dst, send_sem=send_sems.at[a, k], recv_sem=recv_sems.at[a, k],
            device_id=to, device_id_type=MESH)

    def mine(a):
        return pltpu.make_async_copy(srcs[a], outs[a].at[slab(*me)], local_sems.at[a])

    def first(a):
        return [copy(a, 0, me, sibling, src=srcs[a])] + [copy(a, 1 + j, me, (*chip, c), src=srcs[a]) for j, chip in enumerate(chips)]

    def passed(a, j):
        return copy(a, 4 + j, (*chips[j], c), sibling)

    def start():
        for a in range(n):
            mine(a).start()
            for cp in first(a):
                cp.start()

    def forward():
        for j, chip in enumerate(chips):
            for a in range(n):
                copy(a, 1 + j, (*chip, c), me).wait_recv()
                passed(a, j).start()

    def finish():
        for a in range(n):
            copy(a, 0, sibling, me).wait_recv()
            for j, chip in enumerate(chips):
                copy(a, 4 + j, (*chip, 1 - c), me).wait_recv()
        for a in range(n):
            for cp in first(a) + [passed(a, j) for j in range(len(chips))]:
                cp.wait_send()
            mine(a).wait()

    return start, forward, finish


def _exchange_ops(ins, outs, whole, send_sems, recv_sems, local_sems):
    n = len(ins)
    x, y, c = _my_place()
    me = 4 * x + 2 * y + c

    def src(a, i):
        return ins[a] if whole[a] else ins[a].at[i]

    def mine(a):
        return pltpu.make_async_copy(src(a, me), outs[a].at[me], local_sems.at[a])

    def send(a, k):
        to = (me + k) % N_DEV
        return pltpu.make_async_remote_copy(
            src_ref=src(a, to), dst_ref=outs[a].at[me], send_sem=send_sems.at[a, k - 1], recv_sem=recv_sems.at[a, k - 1],
            device_id=(to // 4, (to // 2) % 2, to % 2), device_id_type=MESH)

    def landed(a, k):
        frm = (me + N_DEV - k) % N_DEV
        return pltpu.make_async_remote_copy(
            src_ref=src(a, frm), dst_ref=outs[a].at[frm], send_sem=send_sems.at[a, k - 1], recv_sem=recv_sems.at[a, k - 1],
            device_id=(x, y, c), device_id_type=MESH)

    def start():
        for a in range(n):
            mine(a).start()
            for k in range(1, N_DEV):
                send(a, k).start()

    def finish():
        for a in range(n):
            for k in range(1, N_DEV):
                landed(a, k).wait_recv()
        for a in range(n):
            for k in range(1, N_DEV):
                send(a, k).wait_send()
            mine(a).wait()

    return start, finish


def _core_exchange_sems(n):
    return [pltpu.SemaphoreType.DMA((n, 4)), pltpu.SemaphoreType.DMA((n, N_DEV)), pltpu.SemaphoreType.DMA((n,))]


def _core_exchange_ops(ins, outs, to_core, send_sems, recv_sems, local_sems):
    n = len(ins)
    x, y, c = _my_place()
    me = 4 * x + 2 * y + c
    others = [(0, 1), (1, 0), (1, 1)]

    def slab(a, p):
        if len(ins[a].shape) == len(outs[a].shape):
            return ins[a].at[p]
        rows = outs[a].shape[1]
        return ins[a].at[pl.ds(pl.multiple_of(p * rows, 16), rows), :]

    def send(a, dx, dy):
        tx, ty = (x + dx) % 2, (y + dy) % 2
        return pltpu.make_async_remote_copy(
            src_ref=slab(a, 4 * to_core + 2 * tx + ty), dst_ref=outs[a].at[me], send_sem=send_sems.at[a, 2 * dx + dy],
            recv_sem=recv_sems.at[a, 2 * (2 * dx + dy) + c], device_id=(tx, ty, to_core), device_id_type=MESH)

    def mine(a):
        return pltpu.make_async_copy(slab(a, 4 * to_core + 2 * x + y), outs[a].at[me], local_sems.at[a])

    def landed(a, dx, dy, sc):
        frm = 4 * ((x + dx) % 2) + 2 * ((y + dy) % 2) + sc
        return pltpu.make_async_remote_copy(
            src_ref=slab(a, 0), dst_ref=outs[a].at[frm], send_sem=send_sems.at[a, 0], recv_sem=recv_sems.at[a, 2 * (2 * dx + dy) + sc],
            device_id=(x, y, c), device_id_type=MESH)

    def start():
        for a in range(n):
            for dx, dy in others:
                send(a, dx, dy).start()
            pl.when(c == to_core)(mine(a).start)
            pl.when(c != to_core)(send(a, 0, 0).start)

    def finish():
        @pl.when(c == to_core)
        def _():
            for a in range(n):
                for dx, dy in [(0, 0)] + others:
                    for sc in (0, 1):
                        if (dx, dy, sc) != (0, 0, to_core):
                            landed(a, dx, dy, sc).wait_recv()
            for a in range(n):
                mine(a).wait()

        @pl.when(c != to_core)
        def _():
            for a in range(n):
                send(a, 0, 0).wait_send()

        for a in range(n):
            for dx, dy in others:
                send(a, dx, dy).wait_send()

    return start, finish


N_CHIP = 4


def _pair_then_chip_sems(n):
    return [pltpu.SemaphoreType.DMA((n, N_CHIP)) for _ in range(6)] + [pltpu.SemaphoreType.DMA((n,))]


def _pair_then_chip_ops(ins, pairs, outs, mine_v, pair_v, sum_v, pair_send, pair_recv, chip_send, chip_recv, load_a, load_b, own_sem):
    n = len(ins)
    x, y, c = _my_place()
    chip = 2 * x + y
    chips = [(0, 0), (0, 1), (1, 0), (1, 1)]
    others = [(0, 1), (1, 0), (1, 1)]

    def to_sibling(a, j):
        px, py = chips[j]
        return pltpu.make_async_remote_copy(
            src_ref=ins[a].at[4 * px + 2 * py + 1 - c], dst_ref=pairs[a].at[j], send_sem=pair_send.at[a, j], recv_sem=pair_recv.at[a, j],
            device_id=(x, y, 1 - c), device_id_type=MESH)

    def spread(a, dx, dy):
        tx, ty = (x + dx) % 2, (y + dy) % 2
        return pltpu.make_async_remote_copy(
            src_ref=sum_v[a].at[2 * tx + ty], dst_ref=outs[a].at[chip], send_sem=chip_send.at[a, 2 * dx + dy],
            recv_sem=chip_recv.at[a, 2 * dx + dy], device_id=(tx, ty, c), device_id_type=MESH)

    def landed(a, dx, dy):
        frm = 2 * ((x + dx) % 2) + (y + dy) % 2
        return pltpu.make_async_remote_copy(
            src_ref=sum_v[a].at[0], dst_ref=outs[a].at[frm], send_sem=chip_send.at[a, 0], recv_sem=chip_recv.at[a, 2 * dx + dy],
            device_id=(x, y, c), device_id_type=MESH)

    def own(a):
        return pltpu.make_async_copy(sum_v[a].at[chip], outs[a].at[chip], own_sem.at[a])

    def pair():
        loads = []
        for a in range(n):
            for j, (px, py) in enumerate(chips):
                to_sibling(a, j).start()
                loads.append(pltpu.make_async_copy(ins[a].at[4 * px + 2 * py + c], mine_v[a].at[j], load_a.at[a, j]))
                loads[-1].start()
        for a in range(n):
            for j in range(N_CHIP):
                to_sibling(a, j).wait_recv()
                loads.append(pltpu.make_async_copy(pairs[a].at[j], pair_v[a].at[j], load_b.at[a, j]))
                loads[-1].start()
        for cp in loads:
            cp.wait()
        for a in range(n):
            sum_v[a][...] = (mine_v[a][...].astype(F32) + pair_v[a][...].astype(F32)).astype(sum_v[a].dtype)

    def start():
        pair()
        for a in range(n):
            own(a).start()
            for dx, dy in others:
                spread(a, dx, dy).start()

    def finish():
        for a in range(n):
            for dx, dy in others:
                landed(a, dx, dy).wait_recv()
        for a in range(n):
            for dx, dy in others:
                spread(a, dx, dy).wait_send()
            for j in range(N_CHIP):
                to_sibling(a, j).wait_send()
            own(a).wait()

    return start, finish


def _gather_first_weights(gathered, dtypes, cast_only):
    n, k = len(gathered), len(cast_only)

    def body(*refs):
        ins, casts_in = refs[:n], refs[n:n + k]
        outs, casts_out = refs[n + k:2 * n + k], refs[2 * n + k:2 * n + 2 * k]
        stages = refs[2 * n + 2 * k:3 * n + 2 * k]
        start, forward, finish = _gather_ops(stages, outs, *refs[3 * n + 2 * k:])
        for a in range(n):
            stages[a][...] = ins[a][...].astype(stages[a].dtype)
        start()
        for a in range(k):
            casts_out[a][...] = casts_in[a][...].astype(BF16)
        forward()
        finish()

    vmem = pl.BlockSpec(memory_space=pltpu.VMEM)
    out = pl.pallas_call(
        body, name="gather_first_weights",
        out_shape=[jax.ShapeDtypeStruct((N_DEV, *s.shape), d) for s, d in zip(gathered, dtypes)]
        + [jax.ShapeDtypeStruct(s.shape, BF16) for s in cast_only],
        in_specs=[vmem] * (n + k), out_specs=[ANY] * n + [vmem] * k,
        scratch_shapes=[pltpu.VMEM(s.shape, d) for s, d in zip(gathered, dtypes)] + _exchange_sems(n),
        compiler_params=pltpu.CompilerParams(vmem_limit_bytes=VMEM_LIMIT_BYTES),
    )(*gathered, *cast_only)
    return out[:n], out[n:]


def _exchange(arrays, whole, name):
    n = len(arrays)

    def body(*refs):
        start, finish = _exchange_ops(refs[:n], refs[n:2 * n], whole, *refs[2 * n:])
        start()
        finish()

    return pl.pallas_call(
        body, name=name,
        out_shape=[jax.ShapeDtypeStruct((N_DEV, *a.shape) if w else a.shape, a.dtype) for a, w in zip(arrays, whole)],
        in_specs=[ANY] * n, out_specs=[ANY] * n, scratch_shapes=_exchange_sems(n),
    )(*arrays)


def _columns_from_slabs(slabs):
    def body(*refs):
        k = len(refs) // 2
        for src, dst in zip(refs[:k], refs[k:]):
            n = src.shape[2]
            for i in range(N_DEV):
                dst[:, pl.ds(n * i, n)] = src[i]

    return pl.pallas_call(
        body, name="columns_from_slabs",
        out_shape=[jax.ShapeDtypeStruct((s.shape[1], N_DEV * s.shape[2]), s.dtype) for s in slabs],
        compiler_params=pltpu.CompilerParams(vmem_limit_bytes=VMEM_LIMIT_BYTES),
    )(*slabs)


def _window_sum(x, win, ahead):
    n = x.shape[0]
    span = 1
    while span < win:
        x = x + pltpu.roll(x, n - span if ahead else span, 0)
        span *= 2
    return x


def _conv_branch(z, ext_u, conv_ref, tm):
    c_w = z.shape[1] // 4
    b, c, v = z[:, :c_w], z[:, c_w:2 * c_w], z[:, 2 * c_w:3 * c_w]
    u = c * v
    ext_u[pl.ds(HALO, tm), :] = u
    u1 = ext_u[pl.ds(HALO - 1, tm), :]
    u2 = ext_u[pl.ds(HALO - 2, tm), :]
    yc = conv_ref[pl.ds(2, 1), :] * u + conv_ref[pl.ds(1, 1), :] * u1 + conv_ref[pl.ds(0, 1), :] * u2
    return b, c, v, u, u1, u2, yc


def _pool_branch(p, ext_p, pool_w_ref, tm):
    ext_p[pl.ds(HALO, tm), :] = p
    pooled, mixed = [], []
    for g, win in enumerate(POOL_WINDOWS):
        s = _window_sum(ext_p[:, pl.ds(POOL_GROUP * g, POOL_GROUP)], win, ahead=False)[HALO:HALO + tm, :]
        pooled.append((s * (1.0 / win) - p[:, POOL_GROUP * g:POOL_GROUP * (g + 1)]).astype(BF16))
        mixed.append(_dot(pooled[-1], pool_w_ref[g].astype(BF16)))
    return pooled, mixed


def _meta_forward(meta, g1, w_in):
    def body(meta_ref, g1_ref, w_ref, a_ref, z_ref):
        hat, _ = _rms_stats(meta_ref[...])
        a = (hat * g1_ref[...]).astype(BF16)
        a_ref[...] = a
        z_ref[...] = _dot(a, w_ref[...])

    return pl.pallas_call(
        body, name="meta_forward",
        out_shape=[jax.ShapeDtypeStruct(meta.shape, BF16), jax.ShapeDtypeStruct((N_META, w_in.shape[1]), F32)],
        compiler_params=pltpu.CompilerParams(vmem_limit_bytes=VMEM_LIMIT_BYTES),
    )(meta, g1, w_in)


def _mixer_forward(x2d, z_meta, g1, w_in, conv_w, pool_w, pool_scale, w_out, g2, n_seq, to_gather):
    t, d = x2d.shape
    zw = w_in.shape[1]
    cw = zw // 4
    s = t // n_seq
    tm = min(TM_MIX, s)
    nj = s // tm
    ng = len(to_gather)

    def body(x_ref, zm_ref, g1_ref, win_ref, conv_ref, pw_ref, ps_ref, wout_ref, g2_ref, *rest):
        shards, (h1_ref, z_ref, m_ref, pooled_ref, mixed_ref), slabs = rest[:ng], rest[ng:ng + 5], rest[ng + 5:2 * ng + 5]
        ext_u, ext_p = rest[2 * ng + 5:2 * ng + 7]
        start, forward, finish = _gather_ops(shards, slabs, *rest[2 * ng + 7:])
        pl.when((pl.program_id(0) == 0) & (pl.program_id(1) == 0))(start)

        @pl.when(pl.program_id(1) == 0)
        def _():
            zm = zm_ref[...]
            ext_u[pl.ds(0, HALO), :] = zm[:, cw:2 * cw] * zm[:, 2 * cw:3 * cw]
            ext_p[pl.ds(0, HALO), :] = zm[:, 3 * cw:]

        h0 = x_ref[...]
        hat, _ = _rms_stats(h0)
        z = _dot((hat * g1_ref[...]).astype(BF16), win_ref[...])
        z_ref[...] = z.astype(BF16)
        b, _, _, _, _, _, yc = _conv_branch(z, ext_u, conv_ref, tm)
        pooled, mixed = _pool_branch(z[:, 3 * cw:], ext_p, pw_ref, tm)
        pooled_ref[...] = jnp.concatenate(pooled, axis=1)
        mixed_ref[...] = jnp.concatenate(mixed, axis=1).astype(BF16)
        ps = ps_ref[...]
        y = [b * yc] + [mixed[g] * ps[:, POOL_GROUP * g:POOL_GROUP * (g + 1)] for g in range(len(POOL_WINDOWS))]
        m = _dot(jnp.concatenate(y, axis=1).astype(BF16), wout_ref[...])
        m_ref[...] = m
        m_hat, _ = _rms_stats(m)
        h1_ref[...] = h0 + m_hat * g2_ref[...]
        ext_u[pl.ds(0, HALO), :] = ext_u[pl.ds(tm, HALO), :]
        ext_p[pl.ds(0, HALO), :] = ext_p[pl.ds(tm, HALO), :]

        @pl.when((pl.program_id(0) == n_seq - 1) & (pl.program_id(1) == nj - 1))
        def _():
            forward()
            finish()

    row = lambda b, j: (b * nj + j, 0)
    out = pl.pallas_call(
        body, name="mixer_forward", grid=(n_seq, nj),
        in_specs=[pl.BlockSpec((tm, d), row), _const(z_meta.shape), _const(g1.shape), _resident(w_in.shape), _const(conv_w.shape),
                  _const(pool_w.shape), _const(pool_scale.shape), _resident(w_out.shape), _const(g2.shape)] + [ANY] * ng,
        out_specs=[pl.BlockSpec((tm, d), row), pl.BlockSpec((tm, zw), row), pl.BlockSpec((tm, d), row), pl.BlockSpec((tm, cw), row),
                   pl.BlockSpec((tm, cw), row)] + [ANY] * ng,
        out_shape=[jax.ShapeDtypeStruct((t, d), F32), jax.ShapeDtypeStruct((t, zw), BF16), jax.ShapeDtypeStruct((t, d), F32),
                   jax.ShapeDtypeStruct((t, cw), BF16), jax.ShapeDtypeStruct((t, cw), BF16)]
        + [jax.ShapeDtypeStruct((N_DEV, *a.shape), a.dtype) for a in to_gather],
        scratch_shapes=[pltpu.VMEM((tm + HALO, cw), F32), pltpu.VMEM((tm + HALO, cw), F32)] + _exchange_sems(ng),
        compiler_params=_params("arbitrary", "arbitrary"),
    )(x2d, z_meta, g1, w_in, conv_w, pool_w, pool_scale, w_out, g2, *to_gather)
    return out[:5], out[5:]


def _gather_and_mixer_forward(x2d, mixer_shards, ffn_shards, g1, pool_w, pool_scale, g2, n_seq):
    t, d = x2d.shape
    zs, rs, ms, cs = mixer_shards[0].shape[1], mixer_shards[1].shape[0], mixer_shards[2].shape[1], mixer_shards[3].shape[1]
    zw, cw = N_DEV * zs, N_DEV * cs
    s = t // n_seq
    tm = min(TM_MIX, s)
    nj = s // tm
    n1, n2 = len(mixer_shards), len(ffn_shards)
    dtypes = [BF16, BF16, F32, F32] + [BF16] * n2
    shards = list(mixer_shards) + list(ffn_shards)

    def body(x_ref, *rest):
        shard_refs, (g1_ref, pw_ref, ps_ref, g2_ref), rest = rest[:n1 + n2], rest[n1 + n2:n1 + n2 + 4], rest[n1 + n2 + 4:]
        (h1_ref, z_ref, m_ref, pooled_ref, mixed_ref, win_o, wout_o, meta_o, conv_o, am_o, zm_o), rest = rest[:11], rest[11:]
        slabs, rest = rest[:n1 + n2], rest[n1 + n2:]
        stages, rest = rest[:n1 + n2], rest[n1 + n2:]
        win_v, wout_v, meta_v, conv_v, ext_u, ext_p, sem = rest[:7]
        first = _gather_ops(stages[:n1], slabs[:n1], *rest[7:10])
        later = _gather_ops(stages[n1:], slabs[n1:], *rest[10:13], core_major=True)

        @pl.when((pl.program_id(0) == 0) & (pl.program_id(1) == 0))
        def _():
            for src, dst in zip(shard_refs, stages):
                dst[...] = src[...].astype(dst.dtype)
            first[0]()
            later[0]()
            first[1]()
            first[2]()
            copies = [pltpu.make_async_copy(slabs[0].at[i], win_v.at[:, pl.ds(zs * i, zs)], sem.at[i]) for i in range(N_DEV)]
            copies += [pltpu.make_async_copy(slabs[1].at[i], wout_v.at[pl.ds(rs * i, rs), :], sem.at[N_DEV + i]) for i in range(N_DEV)]
            copies += [pltpu.make_async_copy(slabs[2], meta_v, sem.at[2 * N_DEV]), pltpu.make_async_copy(slabs[3], conv_v, sem.at[2 * N_DEV + 1])]
            for cp in copies:
                cp.start()
            for cp in copies:
                cp.wait()
            copies = [pltpu.make_async_copy(win_v, win_o, sem.at[0]), pltpu.make_async_copy(wout_v, wout_o, sem.at[1])]
            for cp in copies:
                cp.start()
            for i in range(N_DEV):
                meta_o[:, pl.ds(ms * i, ms)] = meta_v[i]
                conv_o[:, pl.ds(cs * i, cs)] = conv_v[i]
            hat, _ = _rms_stats(meta_o[...])
            a = (hat * g1_ref[...]).astype(BF16)
            am_o[...] = a
            zm_o[...] = _dot(a, win_v[...])
            for cp in copies:
                cp.wait()

        @pl.when(pl.program_id(1) == 0)
        def _():
            zm = zm_o[...]
            ext_u[pl.ds(0, HALO), :] = zm[:, cw:2 * cw] * zm[:, 2 * cw:3 * cw]
            ext_p[pl.ds(0, HALO), :] = zm[:, 3 * cw:]

        h0 = x_ref[...]
        hat, _ = _rms_stats(h0)
        z = _dot((hat * g1_ref[...]).astype(BF16), win_v[...])
        z_ref[...] = z.astype(BF16)
        b, _, _, _, _, _, yc = _conv_branch(z, ext_u, conv_o, tm)
        pooled, mixed = _pool_branch(z[:, 3 * cw:], ext_p, pw_ref, tm)
        pooled_ref[...] = jnp.concatenate(pooled, axis=1)
        mixed_ref[...] = jnp.concatenate(mixed, axis=1).astype(BF16)
        ps = ps_ref[...]
        y = [b * yc] + [mixed[g] * ps[:, POOL_GROUP * g:POOL_GROUP * (g + 1)] for g in range(len(POOL_WINDOWS))]
        m = _dot(jnp.concatenate(y, axis=1).astype(BF16), wout_v[...])
        m_ref[...] = m
        m_hat, _ = _rms_stats(m)
        h1_ref[...] = h0 + m_hat * g2_ref[...]
        ext_u[pl.ds(0, HALO), :] = ext_u[pl.ds(tm, HALO), :]
        ext_p[pl.ds(0, HALO), :] = ext_p[pl.ds(tm, HALO), :]

        @pl.when((pl.program_id(0) == n_seq - 1) & (pl.program_id(1) == nj - 1))
        def _():
            later[1]()
            later[2]()

    row = lambda b, j: (b * nj + j, 0)
    vmem = pl.BlockSpec(memory_space=pltpu.VMEM)
    small = [(N_META, d), (CONV_WIDTH, cw), (N_META, d), (N_META, zw)]
    out = pl.pallas_call(
        body, name="gather_and_mixer_forward", grid=(n_seq, nj),
        in_specs=[pl.BlockSpec((tm, d), row)] + [vmem] * (n1 + n2)
        + [_const(g1.shape), _const(pool_w.shape), _const(pool_scale.shape), _const(g2.shape)],
        out_specs=[pl.BlockSpec((tm, d), row), pl.BlockSpec((tm, zw), row), pl.BlockSpec((tm, d), row), pl.BlockSpec((tm, cw), row),
                   pl.BlockSpec((tm, cw), row), ANY, ANY] + [_const(sh) for sh in small] + [ANY] * (n1 + n2),
        out_shape=[jax.ShapeDtypeStruct((t, d), F32), jax.ShapeDtypeStruct((t, zw), BF16), jax.ShapeDtypeStruct((t, d), F32),
                   jax.ShapeDtypeStruct((t, cw), BF16), jax.ShapeDtypeStruct((t, cw), BF16),
                   jax.ShapeDtypeStruct((d, zw), BF16), jax.ShapeDtypeStruct((d, d), BF16),
                   jax.ShapeDtypeStruct(small[0], F32), jax.ShapeDtypeStruct(small[1], F32), jax.ShapeDtypeStruct(small[2], BF16),
                   jax.ShapeDtypeStruct(small[3], F32)]
        + [jax.ShapeDtypeStruct((N_DEV, *a.shape), dt) for a, dt in zip(shards, dtypes)],
        scratch_shapes=[pltpu.VMEM(a.shape, dt) for a, dt in zip(shards, dtypes)]
        + [pltpu.VMEM((d, zw), BF16), pltpu.VMEM((d, d), BF16), pltpu.VMEM((N_DEV, N_META, ms), F32),
           pltpu.VMEM((N_DEV, CONV_WIDTH, cs), F32), pltpu.VMEM((tm + HALO, cw), F32), pltpu.VMEM((tm + HALO, cw), F32),
           pltpu.SemaphoreType.DMA((2 * N_DEV + 2,))] + _exchange_sems(n1) + _exchange_sems(n2),
        compiler_params=_params("arbitrary", "arbitrary"),
    )(x2d, *shards, g1, pool_w, pool_scale, g2)
    return out[:5], out[5:11], out[11 + n1:]


def _mixer_backward(x2d, dh1, m, z, pooled, mixed, meta, a_meta, z_meta, g1, w_in, conv_w, pool_w, pool_scale, w_out, g2, n_seq,
                    to_exchange, landing):
    t, d = x2d.shape
    zw = w_in.shape[1]
    cw = zw // 4
    s = t // n_seq
    tm = min(TM_MIX, s)
    nj = s // tm
    n_groups = len(POOL_WINDOWS)
    zs = zw // N_DEV
    nx = len(to_exchange)
    n_in = 17
    given = [k for k, a in enumerate(landing) if a is not None]
    fresh = [k for k, a in enumerate(landing) if a is None]

    def body(x_ref, dh1_ref, m_ref, z_ref, zprev_ref, pooled_ref, mixed_ref, meta_ref, am_ref, zm_ref, g1_ref, win_ref, conv_ref, pw_ref, ps_ref, wout_ref,
             g2_ref, *rest):
        sent, rest = rest[:nx], rest[nx + len(given):]
        gx_ref, dwin_ref, dwout_ref, dg1_ref, dg2_ref, dconv_ref, dpw_ref, dps_ref, dmeta_ref = rest[:9]
        landed, rest = rest[9:9 + nx], rest[9 + nx:]
        ext_u, ext_dyc, ext_dq, acc_win, acc_wout, stage16, sem = rest[:7]
        north = _core_exchange_ops(sent, landed, 1, *rest[7:10])
        south = _core_exchange_ops([sent[k] for k in fresh], [landed[k] for k in fresh], 0, *rest[10:13])

        def start():
            north[0]()
            south[0]()

        def finish():
            south[1]()
            north[1]()

        b_id, j = pl.program_id(0), pl.program_id(1)
        jr = nj - 1 - j
        pl.when((b_id == 0) & (j == 0))(start)

        @pl.when((b_id == 0) & (j == 0))
        def _():
            acc_win[...] = jnp.zeros_like(acc_win)
            acc_wout[...] = jnp.zeros_like(acc_wout)
            for r in (dg1_ref, dg2_ref, dconv_ref, dpw_ref, dps_ref, dmeta_ref):
                r[...] = jnp.zeros_like(r)

        @pl.when(j == 0)
        def _():
            ext_dyc[pl.ds(tm, HALO), :] = jnp.zeros((HALO, cw), F32)
            ext_dq[pl.ds(tm, HALO), :] = jnp.zeros((HALO, cw), F32)

        zm = zm_ref[...]
        halo = jnp.where(jr == 0, zm, zprev_ref[...].astype(F32))
        ext_u[pl.ds(0, HALO), :] = halo[:, cw:2 * cw] * halo[:, 2 * cw:3 * cw]

        h0 = x_ref[...]
        hat0, rstd0 = _rms_stats(h0)
        g1 = g1_ref[...]
        a = (hat0 * g1).astype(BF16)
        b, c, v, u, u1, u2, yc = _conv_branch(z_ref[...].astype(F32), ext_u, conv_ref, tm)
        mixed = [mixed_ref[:, pl.ds(POOL_GROUP * g, POOL_GROUP)].astype(F32) for g in range(n_groups)]
        ps = ps_ref[...]
        y = [b * yc] + [mixed[g] * ps[:, POOL_GROUP * g:POOL_GROUP * (g + 1)] for g in range(n_groups)]
        ycat = jnp.concatenate(y, axis=1).astype(BF16)

        dh1v = dh1_ref[...]
        m_hat, m_rstd = _rms_stats(m_ref[...])
        dm, dg2 = _rms_bwd(m_hat, m_rstd, g2_ref[...], dh1v)
        dg2_ref[...] += dg2
        dm = dm.astype(BF16)
        acc_wout[...] += _dot_tn(ycat, dm)
        dycat = _dot_nt(dm, wout_ref[...])

        dyconv = dycat[:, :cw]
        db = dyconv * yc
        dyc = dyconv * b
        ext_dyc[pl.ds(0, tm), :] = dyc
        du = (conv_ref[pl.ds(2, 1), :] * dyc + conv_ref[pl.ds(1, 1), :] * ext_dyc[pl.ds(1, tm), :]
              + conv_ref[pl.ds(0, 1), :] * ext_dyc[pl.ds(2, tm), :])
        dconv_ref[pl.ds(2, 1), :] += jnp.sum(dyc * u, axis=0, keepdims=True)
        dconv_ref[pl.ds(1, 1), :] += jnp.sum(dyc * u1, axis=0, keepdims=True)
        dconv_ref[pl.ds(0, 1), :] += jnp.sum(dyc * u2, axis=0, keepdims=True)

        dp = []
        for g, win in enumerate(POOL_WINDOWS):
            lanes = pl.ds(POOL_GROUP * g, POOL_GROUP)
            dypool = dycat[:, cw + POOL_GROUP * g:cw + POOL_GROUP * (g + 1)]
            dps_ref[:, lanes] += jnp.sum(dypool * mixed[g], axis=0, keepdims=True)
            dmixed = (dypool * ps[:, POOL_GROUP * g:POOL_GROUP * (g + 1)]).astype(BF16)
            dpw_ref[g] += _dot_tn(pooled_ref[:, lanes], dmixed)
            dq = _dot_nt(dmixed, pw_ref[g].astype(BF16))
            ext_dq[pl.ds(0, tm), lanes] = dq
            acc = _window_sum(ext_dq[:, lanes], win, ahead=True)[0:tm, :]
            dp.append(acc * (1.0 / win) - dq)

        dz = jnp.concatenate([db, du * v, du * c] + dp, axis=1).astype(BF16)
        acc_win[...] += _dot_tn(a, dz)
        dh0, dg1 = _rms_bwd(hat0, rstd0, g1, _dot_nt(dz, win_ref[...]))
        dg1_ref[...] += dg1
        gx_ref[...] = dh1v + dh0

        ext_dyc[pl.ds(tm, HALO), :] = ext_dyc[pl.ds(0, HALO), :]
        ext_dq[pl.ds(tm, HALO), :] = ext_dq[pl.ds(0, HALO), :]

        @pl.when(jr == 0)
        def _():
            ext_dyc[pl.ds(tm - HALO, HALO), :] = jnp.zeros((HALO, cw), F32)
            ext_dq[pl.ds(tm - HALO, HALO), :] = jnp.zeros((HALO, cw), F32)
            du_m = (conv_ref[pl.ds(1, 1), :] * ext_dyc[pl.ds(tm - HALO + 1, HALO), :]
                    + conv_ref[pl.ds(0, 1), :] * ext_dyc[pl.ds(tm - HALO + 2, HALO), :])
            dp_m = []
            for g, win in enumerate(POOL_WINDOWS):
                lanes = pl.ds(POOL_GROUP * g, POOL_GROUP)
                acc = ext_dq[pl.ds(tm - HALO + 1, HALO), lanes]
                for k in range(2, win):
                    acc = acc + ext_dq[pl.ds(tm - HALO + k, HALO), lanes]
                dp_m.append(acc * (1.0 / win))
            dz_m = jnp.concatenate([jnp.zeros((HALO, cw), F32), du_m * zm[:, 2 * cw:3 * cw], du_m * zm[:, cw:2 * cw]] + dp_m,
                                   axis=1).astype(BF16)
            acc_win[...] += _dot_tn(am_ref[...], dz_m)
            hat_m, rstd_m = _rms_stats(meta_ref[...])
            dmeta, dg1_m = _rms_bwd(hat_m, rstd_m, g1, _dot_nt(dz_m, win_ref[...]))
            dg1_ref[...] += dg1_m
            dmeta_ref[...] += dmeta

        @pl.when((b_id == n_seq - 1) & (j == nj - 1))
        def _():
            pieces = [(acc_win, zs * i, dwin_ref.at[i]) for i in range(N_DEV)]
            pieces += [(acc_wout, zs * i, dwout_ref.at[:, pl.ds(zs * i, zs)]) for i in range(d // zs)]
            copies = []
            for k, (acc, col, dst) in enumerate(pieces):
                if k >= 2:
                    copies[k - 2].wait()
                stage16[k % 2] = acc[:, pl.ds(col, zs)].astype(BF16)
                copies.append(pltpu.make_async_copy(stage16.at[k % 2], dst, sem.at[k % 2]))
                copies[k].start()
            copies[-2].wait()
            copies[-1].wait()
            finish()

    row = lambda b, j: (b * nj + nj - 1 - j, 0)
    prev = lambda b, j: (jnp.maximum((b * s + (nj - 1 - j) * tm) // HALO - 1, 0), 0)
    small = [g1.shape, g2.shape, conv_w.shape, pool_w.shape, pool_scale.shape, meta.shape]
    out = pl.pallas_call(
        body, name="mixer_backward", grid=(n_seq, nj),
        in_specs=[pl.BlockSpec((tm, d), row), pl.BlockSpec((tm, d), row), pl.BlockSpec((tm, d), row), pl.BlockSpec((tm, zw), row),
                  pl.BlockSpec((HALO, zw), prev), pl.BlockSpec((tm, cw), row), pl.BlockSpec((tm, cw), row), _const(meta.shape), _const(a_meta.shape), _const(z_meta.shape), _const(g1.shape),
                  _resident(w_in.shape), _const(conv_w.shape), _const(pool_w.shape), _const(pool_scale.shape), _resident(w_out.shape),
                  _const(g2.shape)] + [ANY] * (nx + len(given)),
        out_specs=[pl.BlockSpec((tm, d), row), ANY, ANY] + [_const(sh) for sh in small] + [ANY] * nx,
        out_shape=[jax.ShapeDtypeStruct((t, d), F32), jax.ShapeDtypeStruct((N_DEV, d, zs), BF16),
                   jax.ShapeDtypeStruct(w_out.shape, BF16)] + [jax.ShapeDtypeStruct(sh, F32) for sh in small]
        + [jax.ShapeDtypeStruct((N_DEV, a.shape[0] // N_DEV, a.shape[1]), a.dtype) for a in to_exchange],
        input_output_aliases={n_in + nx + at: 9 + k for at, k in enumerate(given)},
        scratch_shapes=[pltpu.VMEM((tm + HALO, cw), F32)] * 3
        + [pltpu.VMEM(w_in.shape, F32), pltpu.VMEM(w_out.shape, F32), pltpu.VMEM((2, d, zs), BF16),
           pltpu.SemaphoreType.DMA((2,))] + _core_exchange_sems(nx) + _core_exchange_sems(len(fresh)),
        compiler_params=_params("arbitrary", "arbitrary"),
    )(x2d, dh1, m, z, z, pooled, mixed, meta, a_meta, z_meta, g1, w_in, conv_w, pool_w, pool_scale, w_out, g2, *to_exchange, *[landing[k] for k in given])
    return out[:9], out[9:]


def _ffn_forward_backward(h1, target, g3, w_gate, w_up, w_down, g4):
    t, d = h1.shape
    ff = w_gate.shape[0]
    tm = min(TM_FFN, t)
    nt = t // tm
    chunks = [(s, min(FFN_CHUNK, ff - s)) for s in range(0, ff, FFN_CHUNK)]

    def body(h1_ref, h1pp_ref, tgt_ref, g3_ref, wg_ref, wu_ref, wd_ref, g4_ref,
             f_ref, act_ref, dd_ref, dgate_ref, dup_ref, dh1_ref, loss_ref, dg3_ref, dg4_ref, *slots):
        gate_s, up_s, dd_s, dh2_s, df_s = slots
        i = pl.program_id(0)

        def forward(slot):
            h1v = h1_ref[...]
            hat, _ = _rms_stats(h1v)
            f = (hat * g3_ref[...]).astype(BF16)
            f_ref[...] = f
            s, n = chunks[0]
            gate, up = _dot_nt(f_ref[...], wg_ref[pl.ds(s, n), :]), _dot_nt(f_ref[...], wu_ref[pl.ds(s, n), :])
            yield
            down = None
            for k, (s, n) in enumerate(chunks):
                gate_s.at[slot][:, pl.ds(s, n)] = gate.astype(BF16)
                up_s.at[slot][:, pl.ds(s, n)] = up.astype(BF16)
                act = (gate * jax.nn.sigmoid(gate) * up).astype(BF16)
                act_ref[:, pl.ds(s, n)] = act
                if k + 1 < len(chunks):
                    s1, n1 = chunks[k + 1]
                    gate, up = _dot_nt(f_ref[...], wg_ref[pl.ds(s1, n1), :]), _dot_nt(f_ref[...], wu_ref[pl.ds(s1, n1), :])
                yield
                part = _dot(act_ref[:, pl.ds(s, n)], wd_ref[pl.ds(s, n), :])
                down = part if down is None else down + part
                yield
            d_hat, d_rstd = _rms_stats(down)
            g4 = g4_ref[...]
            err = h1v + d_hat * g4 - tgt_ref[...]
            loss_ref[...] += jnp.sum(err * err) * (0.5 / d)
            dh2 = err * (1.0 / d)
            dh2_s.at[slot][...] = dh2
            dd, dg4 = _rms_bwd(d_hat, d_rstd, g4, dh2)
            dg4_ref[...] += dg4
            dd = dd.astype(BF16)
            dd_ref[...] = dd
            dd_s.at[slot][...] = dd

        def backward(slot):
            s, n = chunks[0]
            dact = _dot_nt(dd_s.at[slot][...], wd_ref[pl.ds(s, n), :])
            yield
            df = None
            for k, (s, n) in enumerate(chunks):
                gate = gate_s.at[slot][:, pl.ds(s, n)].astype(F32)
                up = up_s.at[slot][:, pl.ds(s, n)].astype(F32)
                sig = jax.nn.sigmoid(gate)
                dup = (dact * (gate * sig)).astype(BF16)
                dgate = (dact * up * (sig * (1.0 + gate * (1.0 - sig)))).astype(BF16)
                dup_ref[:, pl.ds(s, n)] = dup
                dgate_ref[:, pl.ds(s, n)] = dgate
                if k + 1 < len(chunks):
                    s1, n1 = chunks[k + 1]
                    dact = _dot_nt(dd_s.at[slot][...], wd_ref[pl.ds(s1, n1), :])
                yield
                part = _dot(dgate_ref[:, pl.ds(s, n)], wg_ref[pl.ds(s, n), :]) + _dot(dup_ref[:, pl.ds(s, n)], wu_ref[pl.ds(s, n), :])
                df = part if df is None else df + part
                yield
            df_s.at[slot][...] = df

        def last(slot):
            hat, rstd = _rms_stats(h1pp_ref[...])
            dh1, dg3 = _rms_bwd(hat, rstd, g3_ref[...], df_s.at[slot][...])
            dg3_ref[...] += dg3
            dh1_ref[...] = dh2_s.at[slot][...] + dh1

        def emit(parity, with_forward, with_backward, with_last):
            fwd = forward(parity) if with_forward else iter(())
            bwd = backward(1 - parity) if with_backward else iter(())
            next(fwd, None)
            if with_last:
                last(parity)
            for _ in range(FFN_BACKWARD_LAG):
                next(fwd, None)
            alive = True
            while alive:
                alive = next(bwd, True) is None
                alive = (next(fwd, True) is None) or alive

        @pl.when(i == 0)
        def _():
            for r in (loss_ref, dg3_ref, dg4_ref, *slots):
                r[...] = jnp.zeros_like(r)

        @pl.when(i < nt)
        def _():
            emit(i % 2, True, True, True)

        @pl.when(i == nt)
        def _():
            emit(nt % 2, False, True, True)

        @pl.when(i == nt + 1)
        def _():
            emit((nt + 1) % 2, False, False, True)

    cur = lambda i: (jnp.minimum(i, nt - 1), 0)
    prev = lambda i: (jnp.clip(i - 1, 0, nt - 1), 0)
    prev2 = lambda i: (jnp.clip(i - 2, 0, nt - 1), 0)
    return pl.pallas_call(
        body, name="ffn_forward_backward", grid=(nt + 2,),
        in_specs=[pl.BlockSpec((tm, d), cur), pl.BlockSpec((tm, d), prev2), pl.BlockSpec((tm, d), cur), _const(g3.shape),
                  _resident(w_gate.shape), _resident(w_up.shape), _resident(w_down.shape), _const(g4.shape)],
        out_specs=[pl.BlockSpec((tm, d), cur), pl.BlockSpec((tm, ff), cur), pl.BlockSpec((tm, d), cur), pl.BlockSpec((tm, ff), prev),
                   pl.BlockSpec((tm, ff), prev), pl.BlockSpec((tm, d), prev2), _const((8, 128)), _const(g3.shape), _const(g4.shape)],
        out_shape=[jax.ShapeDtypeStruct((t, d), BF16), jax.ShapeDtypeStruct((t, ff), BF16), jax.ShapeDtypeStruct((t, d), BF16),
                   jax.ShapeDtypeStruct((t, ff), BF16), jax.ShapeDtypeStruct((t, ff), BF16), jax.ShapeDtypeStruct((t, d), F32),
                   jax.ShapeDtypeStruct((8, 128), F32), jax.ShapeDtypeStruct(g3.shape, F32), jax.ShapeDtypeStruct(g4.shape, F32)],
        scratch_shapes=[pltpu.VMEM((2, tm, ff), BF16)] * 2 + [pltpu.VMEM((2, tm, d), BF16)] + [pltpu.VMEM((2, tm, d), F32)] * 2,
        compiler_params=_params("arbitrary"),
    )(h1, h1, target, g3, w_gate, w_up, w_down, g4)


def _ffn_weight_grads(f, dd, dgate, dup, act):
    t, d = f.shape
    ff = dgate.shape[1]
    tm = min(TM_WGRAD, t)
    nt = t // tm
    fc = ff // FF_CHUNKS
    assert FF_CHUNKS == 2

    def body(f_ref, dd_ref, dgate_ref, dup_ref, act_ref, dwg_ref, dwu_ref, dwd_ref, *rest):
        landing, (acc_g, acc_u, acc_d, stage, sem) = rest[:2], rest[2:7]
        start, finish = _core_exchange_ops([dwg_ref, dwd_ref], landing, 0, *rest[7:])
        c, i = pl.program_id(0), pl.program_id(1)
        pl.when((c == 1) & (i == 0))(start)

        @pl.when(i == 0)
        def _():
            acc_g[...] = jnp.zeros_like(acc_g)
            acc_u[...] = jnp.zeros_like(acc_u)
            acc_d[...] = jnp.zeros_like(acc_d)

        fv = f_ref[...]
        acc_g[...] += _dot_tn(fv, dgate_ref[...])
        acc_u[...] += _dot_tn(fv, dup_ref[...])
        acc_d[...] += _dot_tn(act_ref[...], dd_ref[...])

        @pl.when(i == nt - 1)
        def _():
            rows = pl.ds(pl.multiple_of(c * fc, 16), fc)
            copies = []
            for k, (acc, out, transposed) in enumerate(((acc_d, dwd_ref, False), (acc_g, dwg_ref, True), (acc_u, dwu_ref, True))):
                if k >= 2:
                    copies[k - 2].wait()
                stage[k % 2] = (acc[...].T if transposed else acc[...]).astype(BF16)
                copies.append(pltpu.make_async_copy(stage.at[k % 2], out.at[rows, :], sem.at[k % 2]))
                copies[k].start()
            copies[-2].wait()
            copies[-1].wait()

        pl.when((c == 1) & (i == nt - 1))(finish)

    row = lambda c, i: (i, 0)
    col = lambda c, i: (i, c)
    out = pl.pallas_call(
        body, name="ffn_weight_grads", grid=(FF_CHUNKS, nt),
        in_specs=[pl.BlockSpec((tm, d), row), pl.BlockSpec((tm, d), row), pl.BlockSpec((tm, fc), col), pl.BlockSpec((tm, fc), col),
                  pl.BlockSpec((tm, fc), col)],
        out_specs=[ANY] * 5,
        out_shape=[jax.ShapeDtypeStruct((ff, d), BF16)] * 3 + [jax.ShapeDtypeStruct((N_DEV, ff // N_DEV, d), BF16)] * 2,
        scratch_shapes=[pltpu.VMEM((d, fc), F32), pltpu.VMEM((d, fc), F32), pltpu.VMEM((fc, d), F32), pltpu.VMEM((2, fc, d), BF16),
                        pltpu.SemaphoreType.DMA((2,))] + _core_exchange_sems(2),
        compiler_params=_params("arbitrary", "arbitrary"),
    )(f, dd, dgate, dup, act)
    return out[:3], [out[3], None, out[4]]


def _adamw(w, g, m, v):
    m = ADAM_B1 * m + (1.0 - ADAM_B1) * g
    v = ADAM_B2 * v + (1.0 - ADAM_B2) * (g * g)
    m_hat = m / (1.0 - ADAM_B1 ** ADAM_STEP)
    v_hat = v / (1.0 - ADAM_B2 ** ADAM_STEP)
    return -ADAM_LR * (m_hat / (jnp.sqrt(v_hat) + ADAM_EPS) + ADAM_WD * w), m, v


def _sum_slabs(ref):
    total = ref[0].astype(F32)
    for i in range(1, ref.shape[0]):
        total = total + ref[i].astype(F32)
    return total


def _adamw_rows(r, c):
    tr = r
    for cand in range(8, r, 8):
        if r % cand == 0 and cand * c <= ADAMW_BLOCK_ELEMS:
            tr = cand
    return r if r * c <= ADAMW_BLOCK_ELEMS else tr


def _reduce_adamw_carrying(parts, ws, ms, vs, to_reduce, to_exchange, whole, name):
    k, nr, nx = len(ws), len(to_reduce), len(to_exchange)
    r, c = ws[0].shape if k else (8, 128)
    tr = _adamw_rows(r, c)
    steps = r // tr
    travels = nr + nx > 0
    chip_slabs = [jax.ShapeDtypeStruct((N_CHIP, *a.shape[1:]), a.dtype) for a in to_reduce]

    def body(*refs):
        p_refs, w_refs, m_refs, v_refs = (refs[a * k:(a + 1) * k] for a in range(4))
        refs = refs[4 * k:]
        reduced_in, sent, refs = refs[:nr], refs[nr:nr + nx], refs[nr + nx:]
        outs, pairs, sums, landed, refs = refs[:4 * k], refs[4 * k:4 * k + nr], refs[4 * k + nr:4 * k + 2 * nr], \
            refs[4 * k + 2 * nr:4 * k + 2 * nr + nx], refs[4 * k + 2 * nr + nx:]
        mine_v, pair_v, sum_v, refs = refs[:nr], refs[nr:2 * nr], refs[2 * nr:3 * nr], refs[3 * nr:]
        if travels:
            reduce_ops = _pair_then_chip_ops(reduced_in, pairs, sums, mine_v, pair_v, sum_v, *refs[:7])
            direct_ops = _exchange_ops(sent, landed, whole, *refs[7:])

            @pl.when(pl.program_id(0) == 0)
            def _():
                direct_ops[0]()
                reduce_ops[0]()

        for a in range(k):
            g = _sum_slabs(p_refs[a])
            outs[4 * a][...] = g
            outs[4 * a + 1][...], outs[4 * a + 2][...], outs[4 * a + 3][...] = _adamw(w_refs[a][...], g, m_refs[a][...], v_refs[a][...])

        if travels:
            @pl.when(pl.program_id(0) == steps - 1)
            def _():
                reduce_ops[1]()
                direct_ops[1]()

    blk = pl.BlockSpec((tr, c), lambda i: (i, 0))
    out = pl.pallas_call(
        body, name=name, grid=(steps,),
        in_specs=[pl.BlockSpec((N_DEV, tr, c), lambda i: (0, i, 0))] * k + [blk] * (3 * k) + [ANY] * (nr + nx),
        out_specs=[blk] * (4 * k) + [ANY] * (2 * nr + nx),
        out_shape=[jax.ShapeDtypeStruct((r, c), F32)] * (4 * k) + chip_slabs + chip_slabs
        + [jax.ShapeDtypeStruct((N_DEV, *a.shape) if w else a.shape, a.dtype) for a, w in zip(to_exchange, whole)],
        scratch_shapes=([pltpu.VMEM(a.shape, a.dtype) for a in chip_slabs] * 3 + _pair_then_chip_sems(nr) + _exchange_sems(nx)
                        if travels else []),
        compiler_params=_params("arbitrary"),
    )(*parts, *ws, *ms, *vs, *to_reduce, *to_exchange)
    return [tuple(out[4 * a:4 * a + 4]) for a in range(k)], out[4 * k + nr:4 * k + 2 * nr], out[4 * k + 2 * nr:]


def _reduce_adamw(parts, w, m, v, name):
    r, c = w.shape
    tr = _adamw_rows(r, c)

    def body(p_ref, w_ref, m_ref, v_ref, g_out, d_out, m_out, v_out):
        g = _sum_slabs(p_ref)
        g_out[...] = g
        d_out[...], m_out[...], v_out[...] = _adamw(w_ref[...], g, m_ref[...], v_ref[...])

    blk = pl.BlockSpec((tr, c), lambda i: (i, 0))
    return pl.pallas_call(
        body, name=name, grid=(r // tr,),
        in_specs=[pl.BlockSpec((parts.shape[0], tr, c), lambda i: (0, i, 0)), blk, blk, blk],
        out_specs=[blk] * 4, out_shape=[jax.ShapeDtypeStruct((r, c), F32)] * 4,
        compiler_params=_params("arbitrary"),
    )(parts, w, m, v)


def _reduce_adamw_small(parts, ws, ms, vs, loss_parts):
    n = len(parts)

    def body(*refs):
        p_refs, w_refs, m_refs, v_refs = (refs[k * n:(k + 1) * n] for k in range(4))
        outs = refs[4 * n + 1:]
        outs[4 * n][...] = _sum_slabs(refs[4 * n])
        for a in range(n):
            g = _sum_slabs(p_refs[a])
            outs[4 * a][...] = g
            outs[4 * a + 1][...], outs[4 * a + 2][...], outs[4 * a + 3][...] = _adamw(w_refs[a][...], g, m_refs[a][...], v_refs[a][...])

    out = pl.pallas_call(
        body, name="adamw_replicated",
        out_shape=[jax.ShapeDtypeStruct(w.shape, F32) for w in ws for _ in range(4)] + [jax.ShapeDtypeStruct(loss_parts.shape[1:], F32)],
        compiler_params=pltpu.CompilerParams(vmem_limit_bytes=VMEM_LIMIT_BYTES),
    )(*parts, *ws, *ms, *vs, loss_parts)
    return [tuple(out[4 * a:4 * a + 4]) for a in range(n)], out[4 * n]


def kernel(x, meta_tokens, norm_mix_pre, w_in, conv_w, pool_w, pool_scale, w_out, norm_mix_post, norm_ffn_pre, w_gate, w_up, w_down, norm_ffn_post, loss_target, m_meta_tokens, m_norm_mix_pre, m_w_in, m_conv_w, m_pool_w, m_pool_scale, m_w_out, m_norm_mix_post, m_norm_ffn_pre, m_w_gate, m_w_up, m_w_down, m_norm_ffn_post, v_meta_tokens, v_norm_mix_pre, v_w_in, v_conv_w, v_pool_w, v_pool_scale, v_w_out, v_norm_mix_post, v_norm_ffn_pre, v_w_gate, v_w_up, v_w_down, v_norm_ffn_post):
    n_seq, seq, d = x.shape
    x2d = x.reshape(n_seq * seq, d)
    target = loss_target.reshape(n_seq * seq, d)

    t_ = lambda a: jnp.swapaxes(a[0], 0, 1)
    pw, ps = pool_w[0], pool_scale

    (h1, z, m, pooled, mixed), (win_b, wout_b, meta, conv, a_meta, z_meta), ffn_slabs = _gather_and_mixer_forward(
        x2d, [w_in[0], w_out[0], meta_tokens, conv_w[0]], [t_(w_gate), t_(w_up), w_down[0]], norm_mix_pre, pw, ps, norm_mix_post, n_seq)
    wg_b, wu_b, wd_b = (s.reshape(-1, d) for s in ffn_slabs)
    f, act, dd, dgate, dup, dh1, loss_sum, dg3, dg4 = _ffn_forward_backward(h1, target, norm_ffn_pre, wg_b, wu_b, wd_b, norm_ffn_post)
    ffn_grads, landing = _ffn_weight_grads(f, dd, dgate, dup, act)
    (gx, dwin, dwout, dg1, dg2, dconv, dpw, dps, dmeta), ffn_parts = _mixer_backward(
        x2d, dh1, m, z, pooled, mixed, meta, a_meta, z_meta, norm_mix_pre, win_b, conv, pw, ps, wout_b, norm_mix_post, n_seq,
        ffn_grads, landing)

    dmeta_s = jnp.transpose(dmeta.reshape(N_META, N_DEV, -1), (1, 0, 2))
    dconv_s = jnp.transpose(dconv.reshape(CONV_WIDTH, N_DEV, -1), (1, 0, 2))
    _, (win_parts, wout_parts), last = _reduce_adamw_carrying(
        [], [], [], [], [dwin, dwout.reshape(N_DEV, -1, d)], [dmeta_s, dconv_s, dg1, dg2, dg3, dg4, dpw, dps, loss_sum],
        [False] * 2 + [True] * 7, "exchange_rest")
    ffn_res, _, _ = _reduce_adamw_carrying(
        ffn_parts, [t_(w_gate), t_(w_up), w_down[0]], [t_(m_w_gate), t_(m_w_up), m_w_down[0]], [t_(v_w_gate), t_(v_w_up), v_w_down[0]],
        [], [], [], "adamw_ffn")
    replicated = last[2:8]

    names = ["meta_tokens", "norm_mix_pre", "w_in", "conv_w", "pool_w", "pool_scale", "w_out", "norm_mix_post", "norm_ffn_pre", "w_gate",
             "w_up", "w_down", "norm_ffn_post"]
    res = {"w_gate": tuple(jnp.swapaxes(o, 0, 1)[None] for o in ffn_res[0]),
           "w_up": tuple(jnp.swapaxes(o, 0, 1)[None] for o in ffn_res[1]), "w_down": tuple(o[None] for o in ffn_res[2])}
    for nm, parts, w, m_, v_ in (("w_in", win_parts, w_in, m_w_in, v_w_in), ("w_out", wout_parts, w_out, m_w_out, v_w_out),
                                 ("conv_w", last[1], conv_w, m_conv_w, v_conv_w)):
        res[nm] = tuple(o[None] for o in _reduce_adamw(parts, w[0], m_[0], v_[0], "adamw_" + nm))
    res["meta_tokens"] = tuple(_reduce_adamw(last[0], meta_tokens, m_meta_tokens, v_meta_tokens, "adamw_meta_tokens"))
    small, loss = _reduce_adamw_small(
        replicated, [norm_mix_pre, norm_mix_post, norm_ffn_pre, norm_ffn_post, pool_w[0], pool_scale],
        [m_norm_mix_pre, m_norm_mix_post, m_norm_ffn_pre, m_norm_ffn_post, m_pool_w[0], m_pool_scale],
        [v_norm_mix_pre, v_norm_mix_post, v_norm_ffn_pre, v_norm_ffn_post, v_pool_w[0], v_pool_scale], last[8])
    for nm, r in zip(["norm_mix_pre", "norm_mix_post", "norm_ffn_pre", "norm_ffn_post", "pool_w", "pool_scale"], small):
        res[nm] = tuple(o[None] for o in r) if nm == "pool_w" else r

    return (loss[0, 0], gx.reshape(n_seq, seq, d), *[res[nm][0] for nm in names], *[res[nm][1] for nm in names],
            *[res[nm][2] for nm in names], *[res[nm][3] for nm in names])
```

```python
import functools

import jax
import jax.numpy as jnp
from jax import lax
from jax.experimental import pallas as pl
from jax.experimental.pallas import tpu as pltpu

F32, BF16 = jnp.float32, jnp.bfloat16
RMS_EPS = 1e-6
N_META = 16
CONV_WIDTH = 3
POOL_WINDOWS = (2, 4, 8, 16)
POOL_GROUP = 128
HALO = 16
N_DEV = 8
MESH_AXES = ("x", "y", "c")
MESH = pl.DeviceIdType.MESH
VMEM_LIMIT_BYTES = 56 * 1024 * 1024
ADAMW_BLOCK_ELEMS = 64 * 1024
TM_MIX = 512
TM_FFN = 256
FFN_CHUNK = 512
FFN_BACKWARD_LAG = 2
TM_WGRAD = 512
FF_CHUNKS = 2

ADAM_LR, ADAM_B1, ADAM_B2, ADAM_EPS, ADAM_WD, ADAM_STEP = 0.001, 0.9, 0.999, 1e-08, 0.01, 10


def _dot(a, b):
    return jnp.dot(a, b, preferred_element_type=F32)


def _dot_nt(a, b):
    return lax.dot_general(a, b, (((1,), (1,)), ((), ())), preferred_element_type=F32)


def _dot_tn(a, b):
    return lax.dot_general(a, b, (((0,), (0,)), ((), ())), preferred_element_type=F32)


def _rms_stats(h):
    rstd = lax.rsqrt(jnp.mean(h * h, axis=-1, keepdims=True) + RMS_EPS)
    return h * rstd, rstd


def _rms_bwd(hat, rstd, g, dy):
    gdy = dy * g
    proj = jnp.mean(gdy * hat, axis=-1, keepdims=True)
    return rstd * (gdy - hat * proj), jnp.sum(dy * hat, axis=0, keepdims=True)


def _params(*semantics):
    return pltpu.CompilerParams(dimension_semantics=semantics or None, vmem_limit_bytes=VMEM_LIMIT_BYTES)


def _resident(shape):
    zeros = (0,) * len(shape)
    return pl.BlockSpec(shape, lambda *_: zeros, pipeline_mode=pl.Buffered(1))


def _const(shape):
    zeros = (0,) * len(shape)
    return pl.BlockSpec(shape, lambda *_: zeros)


ANY = pl.BlockSpec(memory_space=pl.ANY)


def _my_place():
    x, y, c = (lax.axis_index(a) for a in MESH_AXES)
    return x, y, c


def _exchange_sems(n):
    return [pltpu.SemaphoreType.DMA((n, N_DEV - 1)), pltpu.SemaphoreType.DMA((n, N_DEV - 1)), pltpu.SemaphoreType.DMA((n,))]


def _gather_ops(srcs, outs, send_sems, recv_sems, local_sems, core_major=False):
    n = len(srcs)
    x, y, c = _my_place()
    me, sibling = (x, y, c), (x, y, 1 - c)
    chips = [(1 - x, y), (x, 1 - y), (1 - x, 1 - y)]

    def slab(px, py, pc):
        return 4 * pc + 2 * px + py if core_major else 4 * px + 2 * py + pc

    def copy(a, k, block, to, src=None):
        dst = outs[a].at[slab(*block)]
        return pltpu.make_async_remote_copy(
            src_ref=dst if src is None else src, dst_ref=dst, send_sem=send_sems.at[a, k], recv_sem=recv_sems.at[a, k],
            device_id=to, device_id_type=MESH)

    def mine(a):
        return pltpu.make_async_copy(srcs[a], outs[a].at[slab(*me)], local_sems.at[a])

    def first(a):
        return [copy(a, 0, me, sibling, src=srcs[a])] + [copy(a, 1 + j, me, (*chip, c), src=srcs[a]) for j, chip in enumerate(chips)]

    def passed(a, j):
        return copy(a, 4 + j, (*chips[j], c), sibling)

    def start():
        for a in range(n):
            mine(a).start()
            for cp in first(a):
                cp.start()

    def forward():
        for j, chip in enumerate(chips):
            for a in range(n):
                copy(a, 1 + j, (*chip, c), me).wait_recv()
                passed(a, j).start()

    def finish():
        for a in range(n):
            copy(a, 0, sibling, me).wait_recv()
            for j, chip in enumerate(chips):
                copy(a, 4 + j, (*chip, 1 - c), me).wait_recv()
        for a in range(n):
            for cp in first(a) + [passed(a, j) for j in range(len(chips))]:
                cp.wait_send()
            mine(a).wait()

    return start, forward, finish


N_GATHER_COPIES = 10


def _gather_halves_sems(n):
    return [pltpu.SemaphoreType.DMA((n, N_GATHER_COPIES)), pltpu.SemaphoreType.DMA((n, N_GATHER_COPIES)), pltpu.SemaphoreType.DMA((n,))]


def _gather_halves_ops(srcs, outs, send_sems, recv_sems, local_sems, core_major=False):
    n = len(srcs)
    x, y, c = _my_place()
    me, sibling = (x, y, c), (x, y, 1 - c)
    xn, yn, dg = (1 - x, y, c), (x, 1 - y, c), (1 - x, 1 - y, c)

    def slab(px, py, pc):
        return 4 * pc + 2 * px + py if core_major else 4 * px + 2 * py + pc

    def half(ref, h):
        rows = ref.shape[0] // 2
        return ref if h is None else ref.at[pl.ds(h * rows, rows)]

    def copy(a, k, block, h, to, from_src=False):
        dst = half(outs[a].at[slab(*block)], h)
        return pltpu.make_async_remote_copy(
            src_ref=half(srcs[a], h) if from_src else dst, dst_ref=dst, send_sem=send_sems.at[a, k], recv_sem=recv_sems.at[a, k],
            device_id=to, device_id_type=MESH)

    def mine(a):
        return pltpu.make_async_copy(srcs[a], outs[a].at[slab(*me)], local_sems.at[a])

    def own(a):
        return [copy(a, 0, me, None, sibling, True), copy(a, 1, me, 0, xn, True), copy(a, 2, me, 1, xn, True),
                copy(a, 3, me, 1, yn, True), copy(a, 4, me, 0, yn, True)]

    def passed(a, k):
        block, h, to = [(xn, 0, yn), (yn, 1, xn), (xn, None, sibling), (yn, None, sibling), (dg, None, sibling)][k - 5]
        return copy(a, k, block, h, to)

    def start():
        for a in range(n):
            mine(a).start()
            for cp in own(a):
                cp.start()

    def forward():
        for a in range(n):
            copy(a, 1, xn, 0, me).wait_recv()
            passed(a, 5).start()
            copy(a, 3, yn, 1, me).wait_recv()
            passed(a, 6).start()
        for a in range(n):
            copy(a, 2, xn, 1, me).wait_recv()
            passed(a, 7).start()
            copy(a, 4, yn, 0, me).wait_recv()
            passed(a, 8).start()
        for a in range(n):
            copy(a, 5, dg, 0, me).wait_recv()
            copy(a, 6, dg, 1, me).wait_recv()
            passed(a, 9).start()

    def finish():
        sib = 1 - c
        for a in range(n):
            copy(a, 0, sibling, None, me).wait_recv()
            copy(a, 7, (1 - x, y, sib), None, me).wait_recv()
            copy(a, 8, (x, 1 - y, sib), None, me).wait_recv()
            copy(a, 9, (1 - x, 1 - y, sib), None, me).wait_recv()
        for a in range(n):
            for cp in own(a) + [passed(a, k) for k in range(5, N_GATHER_COPIES)]:
                cp.wait_send()
            mine(a).wait()

    return start, forward, finish


def _exchange_ops(ins, outs, whole, send_sems, recv_sems, local_sems):
    n = len(ins)
    x, y, c = _my_place()
    me = 4 * x + 2 * y + c

    def src(a, i):
        return ins[a] if whole[a] else ins[a].at[i]

    def mine(a):
        return pltpu.make_async_copy(src(a, me), outs[a].at[me], local_sems.at[a])

    def send(a, k):
        to = (me + k) % N_DEV
        return pltpu.make_async_remote_copy(
            src_ref=src(a, to), dst_ref=outs[a].at[me], send_sem=send_sems.at[a, k - 1], recv_sem=recv_sems.at[a, k - 1],
            device_id=(to // 4, (to // 2) % 2, to % 2), device_id_type=MESH)

    def landed(a, k):
        frm = (me + N_DEV - k) % N_DEV
        return pltpu.make_async_remote_copy(
            src_ref=src(a, frm), dst_ref=outs[a].at[frm], send_sem=send_sems.at[a, k - 1], recv_sem=recv_sems.at[a, k - 1],
            device_id=(x, y, c), device_id_type=MESH)

    def start():
        for a in range(n):
            mine(a).start()
            for k in range(1, N_DEV):
                send(a, k).start()

    def finish():
        for a in range(n):
            for k in range(1, N_DEV):
                landed(a, k).wait_recv()
        for a in range(n):
            for k in range(1, N_DEV):
                send(a, k).wait_send()
            mine(a).wait()

    return start, finish


def _core_exchange_sems(n):
    return [pltpu.SemaphoreType.DMA((n, 4)), pltpu.SemaphoreType.DMA((n, N_DEV)), pltpu.SemaphoreType.DMA((n,))]


def _core_exchange_ops(ins, outs, to_core, send_sems, recv_sems, local_sems):
    n = len(ins)
    x, y, c = _my_place()
    me = 4 * x + 2 * y + c
    others = [(0, 1), (1, 0), (1, 1)]

    def slab(a, p):
        if len(ins[a].shape) == len(outs[a].shape):
            return ins[a].at[p]
        rows = outs[a].shape[1]
        return ins[a].at[pl.ds(pl.multiple_of(p * rows, 16), rows), :]

    def send(a, dx, dy):
        tx, ty = (x + dx) % 2, (y + dy) % 2
        return pltpu.make_async_remote_copy(
            src_ref=slab(a, 4 * to_core + 2 * tx + ty), dst_ref=outs[a].at[me], send_sem=send_sems.at[a, 2 * dx + dy],
            recv_sem=recv_sems.at[a, 2 * (2 * dx + dy) + c], device_id=(tx, ty, to_core), device_id_type=MESH)

    def mine(a):
        return pltpu.make_async_copy(slab(a, 4 * to_core + 2 * x + y), outs[a].at[me], local_sems.at[a])

    def landed(a, dx, dy, sc):
        frm = 4 * ((x + dx) % 2) + 2 * ((y + dy) % 2) + sc
        return pltpu.make_async_remote_copy(
            src_ref=slab(a, 0), dst_ref=outs[a].at[frm], send_sem=send_sems.at[a, 0], recv_sem=recv_sems.at[a, 2 * (2 * dx + dy) + sc],
            device_id=(x, y, c), device_id_type=MESH)

    def start():
        for a in range(n):
            for dx, dy in others:
                send(a, dx, dy).start()
            pl.when(c == to_core)(mine(a).start)
            pl.when(c != to_core)(send(a, 0, 0).start)

    def finish():
        @pl.when(c == to_core)
        def _():
            for a in range(n):
                for dx, dy in [(0, 0)] + others:
                    for sc in (0, 1):
                        if (dx, dy, sc) != (0, 0, to_core):
                            landed(a, dx, dy, sc).wait_recv()
            for a in range(n):
                mine(a).wait()

        @pl.when(c != to_core)
        def _():
            for a in range(n):
                send(a, 0, 0).wait_send()

        for a in range(n):
            for dx, dy in others:
                send(a, dx, dy).wait_send()

    return start, finish


N_CHIP = 4


def _pair_then_chip_sems(n):
    return [pltpu.SemaphoreType.DMA((n, N_CHIP)) for _ in range(6)] + [pltpu.SemaphoreType.DMA((n,))]


def _pair_then_chip_ops(ins, pairs, outs, mine_v, pair_v, sum_v, pair_send, pair_recv, chip_send, chip_recv, load_a, load_b, own_sem):
    n = len(ins)
    x, y, c = _my_place()
    chip = 2 * x + y
    chips = [(0, 0), (0, 1), (1, 0), (1, 1)]
    others = [(0, 1), (1, 0), (1, 1)]

    def to_sibling(a, j):
        px, py = chips[j]
        return pltpu.make_async_remote_copy(
            src_ref=ins[a].at[4 * px + 2 * py + 1 - c], dst_ref=pairs[a].at[j], send_sem=pair_send.at[a, j], recv_sem=pair_recv.at[a, j],
            device_id=(x, y, 1 - c), device_id_type=MESH)

    def spread(a, dx, dy):
        tx, ty = (x + dx) % 2, (y + dy) % 2
        return pltpu.make_async_remote_copy(
            src_ref=sum_v[a].at[2 * tx + ty], dst_ref=outs[a].at[chip], send_sem=chip_send.at[a, 2 * dx + dy],
            recv_sem=chip_recv.at[a, 2 * dx + dy], device_id=(tx, ty, c), device_id_type=MESH)

    def landed(a, dx, dy):
        frm = 2 * ((x + dx) % 2) + (y + dy) % 2
        return pltpu.make_async_remote_copy(
            src_ref=sum_v[a].at[0], dst_ref=outs[a].at[frm], send_sem=chip_send.at[a, 0], recv_sem=chip_recv.at[a, 2 * dx + dy],
            device_id=(x, y, c), device_id_type=MESH)

    def own(a):
        return pltpu.make_async_copy(sum_v[a].at[chip], outs[a].at[chip], own_sem.at[a])

    def pair():
        loads = []
        for a in range(n):
            for j, (px, py) in enumerate(chips):
                to_sibling(a, j).start()
                loads.append(pltpu.make_async_copy(ins[a].at[4 * px + 2 * py + c], mine_v[a].at[j], load_a.at[a, j]))
                loads[-1].start()
        for a in range(n):
            for j in range(N_CHIP):
                to_sibling(a, j).wait_recv()
                loads.append(pltpu.make_async_copy(pairs[a].at[j], pair_v[a].at[j], load_b.at[a, j]))
                loads[-1].start()
        for cp in loads:
            cp.wait()
        for a in range(n):
            sum_v[a][...] = (mine_v[a][...].astype(F32) + pair_v[a][...].astype(F32)).astype(sum_v[a].dtype)

    def start():
        pair()
        for a in range(n):
            own(a).start()
            for dx, dy in others:
                spread(a, dx, dy).start()

    def finish():
        for a in range(n):
            for dx, dy in others:
                landed(a, dx, dy).wait_recv()
        for a in range(n):
            for dx, dy in others:
                spread(a, dx, dy).wait_send()
            for j in range(N_CHIP):
                to_sibling(a, j).wait_send()
            own(a).wait()

    return start, finish


def _gather_first_weights(gathered, dtypes, cast_only):
    n, k = len(gathered), len(cast_only)

    def body(*refs):
        ins, casts_in = refs[:n], refs[n:n + k]
        outs, casts_out = refs[n + k:2 * n + k], refs[2 * n + k:2 * n + 2 * k]
        stages = refs[2 * n + 2 * k:3 * n + 2 * k]
        start, forward, finish = _gather_ops(stages, outs, *refs[3 * n + 2 * k:])
        for a in range(n):
            stages[a][...] = ins[a][...].astype(stages[a].dtype)
        start()
        for a in range(k):
            casts_out[a][...] = casts_in[a][...].astype(BF16)
        forward()
        finish()

    vmem = pl.BlockSpec(memory_space=pltpu.VMEM)
    out = pl.pallas_call(
        body, name="gather_first_weights",
        out_shape=[jax.ShapeDtypeStruct((N_DEV, *s.shape), d) for s, d in zip(gathered, dtypes)]
        + [jax.ShapeDtypeStruct(s.shape, BF16) for s in cast_only],
        in_specs=[vmem] * (n + k), out_specs=[ANY] * n + [vmem] * k,
        scratch_shapes=[pltpu.VMEM(s.shape, d) for s, d in zip(gathered, dtypes)] + _exchange_sems(n),
        compiler_params=pltpu.CompilerParams(vmem_limit_bytes=VMEM_LIMIT_BYTES),
    )(*gathered, *cast_only)
    return out[:n], out[n:]


def _exchange(arrays, whole, name):
    n = len(arrays)

    def body(*refs):
        start, finish = _exchange_ops(refs[:n], refs[n:2 * n], whole, *refs[2 * n:])
        start()
        finish()

    return pl.pallas_call(
        body, name=name,
        out_shape=[jax.ShapeDtypeStruct((N_DEV, *a.shape) if w else a.shape, a.dtype) for a, w in zip(arrays, whole)],
        in_specs=[ANY] * n, out_specs=[ANY] * n, scratch_shapes=_exchange_sems(n),
    )(*arrays)


def _columns_from_slabs(slabs):
    def body(*refs):
        k = len(refs) // 2
        for src, dst in zip(refs[:k], refs[k:]):
            n = src.shape[2]
            for i in range(N_DEV):
                dst[:, pl.ds(n * i, n)] = src[i]

    return pl.pallas_call(
        body, name="columns_from_slabs",
        out_shape=[jax.ShapeDtypeStruct((s.shape[1], N_DEV * s.shape[2]), s.dtype) for s in slabs],
        compiler_params=pltpu.CompilerParams(vmem_limit_bytes=VMEM_LIMIT_BYTES),
    )(*slabs)


def _window_sum(x, win, ahead):
    n = x.shape[0]
    span = 1
    while span < win:
        x = x + pltpu.roll(x, n - span if ahead else span, 0)
        span *= 2
    return x


def _conv_branch(z, ext_u, conv_ref, tm):
    c_w = z.shape[1] // 4
    b, c, v = z[:, :c_w], z[:, c_w:2 * c_w], z[:, 2 * c_w:3 * c_w]
    u = c * v
    ext_u[pl.ds(HALO, tm), :] = u
    u1 = ext_u[pl.ds(HALO - 1, tm), :]
    u2 = ext_u[pl.ds(HALO - 2, tm), :]
    yc = conv_ref[pl.ds(2, 1), :] * u + conv_ref[pl.ds(1, 1), :] * u1 + conv_ref[pl.ds(0, 1), :] * u2
    return b, c, v, u, u1, u2, yc


def _pool_branch(p, ext_p, pool_w_ref, tm):
    ext_p[pl.ds(HALO, tm), :] = p
    pooled, mixed = [], []
    for g, win in enumerate(POOL_WINDOWS):
        s = _window_sum(ext_p[:, pl.ds(POOL_GROUP * g, POOL_GROUP)], win, ahead=False)[HALO:HALO + tm, :]
        pooled.append((s * (1.0 / win) - p[:, POOL_GROUP * g:POOL_GROUP * (g + 1)]).astype(BF16))
        mixed.append(_dot(pooled[-1], pool_w_ref[g].astype(BF16)))
    return pooled, mixed


def _meta_forward(meta, g1, w_in):
    def body(meta_ref, g1_ref, w_ref, a_ref, z_ref):
        hat, _ = _rms_stats(meta_ref[...])
        a = (hat * g1_ref[...]).astype(BF16)
        a_ref[...] = a
        z_ref[...] = _dot(a, w_ref[...])

    return pl.pallas_call(
        body, name="meta_forward",
        out_shape=[jax.ShapeDtypeStruct(meta.shape, BF16), jax.ShapeDtypeStruct((N_META, w_in.shape[1]), F32)],
        compiler_params=pltpu.CompilerParams(vmem_limit_bytes=VMEM_LIMIT_BYTES),
    )(meta, g1, w_in)


def _mixer_forward(x2d, z_meta, g1, w_in, conv_w, pool_w, pool_scale, w_out, g2, n_seq, to_gather):
    t, d = x2d.shape
    zw = w_in.shape[1]
    cw = zw // 4
    s = t // n_seq
    tm = min(TM_MIX, s)
    nj = s // tm
    ng = len(to_gather)

    def body(x_ref, zm_ref, g1_ref, win_ref, conv_ref, pw_ref, ps_ref, wout_ref, g2_ref, *rest):
        shards, (h1_ref, z_ref, m_ref, pooled_ref, mixed_ref), slabs = rest[:ng], rest[ng:ng + 5], rest[ng + 5:2 * ng + 5]
        ext_u, ext_p = rest[2 * ng + 5:2 * ng + 7]
        start, forward, finish = _gather_ops(shards, slabs, *rest[2 * ng + 7:])
        pl.when((pl.program_id(0) == 0) & (pl.program_id(1) == 0))(start)

        @pl.when(pl.program_id(1) == 0)
        def _():
            zm = zm_ref[...]
            ext_u[pl.ds(0, HALO), :] = zm[:, cw:2 * cw] * zm[:, 2 * cw:3 * cw]
            ext_p[pl.ds(0, HALO), :] = zm[:, 3 * cw:]

        h0 = x_ref[...]
        hat, _ = _rms_stats(h0)
        z = _dot((hat * g1_ref[...]).astype(BF16), win_ref[...])
        z_ref[...] = z.astype(BF16)
        b, _, _, _, _, _, yc = _conv_branch(z, ext_u, conv_ref, tm)
        pooled, mixed = _pool_branch(z[:, 3 * cw:], ext_p, pw_ref, tm)
        pooled_ref[...] = jnp.concatenate(pooled, axis=1)
        mixed_ref[...] = jnp.concatenate(mixed, axis=1).astype(BF16)
        ps = ps_ref[...]
        y = [b * yc] + [mixed[g] * ps[:, POOL_GROUP * g:POOL_GROUP * (g + 1)] for g in range(len(POOL_WINDOWS))]
        m = _dot(jnp.concatenate(y, axis=1).astype(BF16), wout_ref[...])
        m_ref[...] = m
        m_hat, _ = _rms_stats(m)
        h1_ref[...] = h0 + m_hat * g2_ref[...]
        ext_u[pl.ds(0, HALO), :] = ext_u[pl.ds(tm, HALO), :]
        ext_p[pl.ds(0, HALO), :] = ext_p[pl.ds(tm, HALO), :]

        @pl.when((pl.program_id(0) == n_seq - 1) & (pl.program_id(1) == nj - 1))
        def _():
            forward()
            finish()

    row = lambda b, j: (b * nj + j, 0)
    out = pl.pallas_call(
        body, name="mixer_forward", grid=(n_seq, nj),
        in_specs=[pl.BlockSpec((tm, d), row), _const(z_meta.shape), _const(g1.shape), _resident(w_in.shape), _const(conv_w.shape),
                  _const(pool_w.shape), _const(pool_scale.shape), _resident(w_out.shape), _const(g2.shape)] + [ANY] * ng,
        out_specs=[pl.BlockSpec((tm, d), row), pl.BlockSpec((tm, zw), row), pl.BlockSpec((tm, d), row), pl.BlockSpec((tm, cw), row),
                   pl.BlockSpec((tm, cw), row)] + [ANY] * ng,
        out_shape=[jax.ShapeDtypeStruct((t, d), F32), jax.ShapeDtypeStruct((t, zw), BF16), jax.ShapeDtypeStruct((t, d), F32),
                   jax.ShapeDtypeStruct((t, cw), BF16), jax.ShapeDtypeStruct((t, cw), BF16)]
        + [jax.ShapeDtypeStruct((N_DEV, *a.shape), a.dtype) for a in to_gather],
        scratch_shapes=[pltpu.VMEM((tm + HALO, cw), F32), pltpu.VMEM((tm + HALO, cw), F32)] + _exchange_sems(ng),
        compiler_params=_params("arbitrary", "arbitrary"),
    )(x2d, z_meta, g1, w_in, conv_w, pool_w, pool_scale, w_out, g2, *to_gather)
    return out[:5], out[5:]


def _gather_and_mixer_forward(x2d, mixer_shards, ffn_shards, g1, pool_w, pool_scale, g2, n_seq):
    t, d = x2d.shape
    zs, rs, ms, cs = mixer_shards[0].shape[1], mixer_shards[1].shape[0], mixer_shards[2].shape[1], mixer_shards[3].shape[1]
    zw, cw = N_DEV * zs, N_DEV * cs
    s = t // n_seq
    tm = min(TM_MIX, s)
    nj = s // tm
    n1, n2 = len(mixer_shards), len(ffn_shards)
    dtypes = [BF16, BF16, F32, F32] + [BF16] * n2
    shards = list(mixer_shards) + list(ffn_shards)

    def body(x_ref, *rest):
        shard_refs, (g1_ref, pw_ref, ps_ref, g2_ref), rest = rest[:n1 + n2], rest[n1 + n2:n1 + n2 + 4], rest[n1 + n2 + 4:]
        (h1_ref, z_ref, m_ref, pooled_ref, mixed_ref, win_o, wout_o, meta_o, conv_o, am_o, zm_o), rest = rest[:11], rest[11:]
        slabs, rest = rest[:n1 + n2], rest[n1 + n2:]
        stages, rest = rest[:n1 + n2], rest[n1 + n2:]
        win_v, wout_v, meta_v, conv_v, ext_u, ext_p, sem = rest[:7]
        first = _gather_halves_ops(stages[:2], slabs[:2], *rest[7:10])
        small = _gather_ops(stages[2:n1], slabs[2:n1], *rest[10:13])
        later = _gather_halves_ops(stages[n1:], slabs[n1:], *rest[13:16], core_major=True)

        @pl.when((pl.program_id(0) == 0) & (pl.program_id(1) == 0))
        def _():
            for src, dst in zip(shard_refs, stages):
                dst[...] = src[...].astype(dst.dtype)
            first[0]()
            small[0]()
            later[0]()
            first[1]()
            small[1]()
            first[2]()
            small[2]()
            copies = [pltpu.make_async_copy(slabs[0].at[i], win_v.at[:, pl.ds(zs * i, zs)], sem.at[i]) for i in range(N_DEV)]
            copies += [pltpu.make_async_copy(slabs[1].at[i], wout_v.at[pl.ds(rs * i, rs), :], sem.at[N_DEV + i]) for i in range(N_DEV)]
            copies += [pltpu.make_async_copy(slabs[2], meta_v, sem.at[2 * N_DEV]), pltpu.make_async_copy(slabs[3], conv_v, sem.at[2 * N_DEV + 1])]
            for cp in copies:
                cp.start()
            for cp in copies:
                cp.wait()
            copies = [pltpu.make_async_copy(win_v, win_o, sem.at[0]), pltpu.make_async_copy(wout_v, wout_o, sem.at[1])]
            for cp in copies:
                cp.start()
            for i in range(N_DEV):
                meta_o[:, pl.ds(ms * i, ms)] = meta_v[i]
                conv_o[:, pl.ds(cs * i, cs)] = conv_v[i]
            hat, _ = _rms_stats(meta_o[...])
            a = (hat * g1_ref[...]).astype(BF16)
            am_o[...] = a
            zm_o[...] = _dot(a, win_v[...])
            for cp in copies:
                cp.wait()

        @pl.when(pl.program_id(1) == 0)
        def _():
            zm = zm_o[...]
            ext_u[pl.ds(0, HALO), :] = zm[:, cw:2 * cw] * zm[:, 2 * cw:3 * cw]
            ext_p[pl.ds(0, HALO), :] = zm[:, 3 * cw:]

        h0 = x_ref[...]
        hat, _ = _rms_stats(h0)
        z = _dot((hat * g1_ref[...]).astype(BF16), win_v[...])
        z_ref[...] = z.astype(BF16)
        b, _, _, _, _, _, yc = _conv_branch(z, ext_u, conv_o, tm)
        pooled, mixed = _pool_branch(z[:, 3 * cw:], ext_p, pw_ref, tm)
        pooled_ref[...] = jnp.concatenate(pooled, axis=1)
        mixed_ref[...] = jnp.concatenate(mixed, axis=1).astype(BF16)
        ps = ps_ref[...]
        y = [b * yc] + [mixed[g] * ps[:, POOL_GROUP * g:POOL_GROUP * (g + 1)] for g in range(len(POOL_WINDOWS))]
        m = _dot(jnp.concatenate(y, axis=1).astype(BF16), wout_v[...])
        m_ref[...] = m
        m_hat, _ = _rms_stats(m)
        h1_ref[...] = h0 + m_hat * g2_ref[...]
        ext_u[pl.ds(0, HALO), :] = ext_u[pl.ds(tm, HALO), :]
        ext_p[pl.ds(0, HALO), :] = ext_p[pl.ds(tm, HALO), :]

        @pl.when((pl.program_id(0) == n_seq - 1) & (pl.program_id(1) == nj - 1))
        def _():
            later[1]()
            later[2]()

    row = lambda b, j: (b * nj + j, 0)
    vmem = pl.BlockSpec(memory_space=pltpu.VMEM)
    small = [(N_META, d), (CONV_WIDTH, cw), (N_META, d), (N_META, zw)]
    out = pl.pallas_call(
        body, name="gather_and_mixer_forward", grid=(n_seq, nj),
        in_specs=[pl.BlockSpec((tm, d), row)] + [vmem] * (n1 + n2)
        + [_const(g1.shape), _const(pool_w.shape), _const(pool_scale.shape), _const(g2.shape)],
        out_specs=[pl.BlockSpec((tm, d), row), pl.BlockSpec((tm, zw), row), pl.BlockSpec((tm, d), row), pl.BlockSpec((tm, cw), row),
                   pl.BlockSpec((tm, cw), row), ANY, ANY] + [_const(sh) for sh in small] + [ANY] * (n1 + n2),
        out_shape=[jax.ShapeDtypeStruct((t, d), F32), jax.ShapeDtypeStruct((t, zw), BF16), jax.ShapeDtypeStruct((t, d), F32),
                   jax.ShapeDtypeStruct((t, cw), BF16), jax.ShapeDtypeStruct((t, cw), BF16),
                   jax.ShapeDtypeStruct((d, zw), BF16), jax.ShapeDtypeStruct((d, d), BF16),
                   jax.ShapeDtypeStruct(small[0], F32), jax.ShapeDtypeStruct(small[1], F32), jax.ShapeDtypeStruct(small[2], BF16),
                   jax.ShapeDtypeStruct(small[3], F32)]
        + [jax.ShapeDtypeStruct((N_DEV, *a.shape), dt) for a, dt in zip(shards, dtypes)],
        scratch_shapes=[pltpu.VMEM(a.shape, dt) for a, dt in zip(shards, dtypes)]
        + [pltpu.VMEM((d, zw), BF16), pltpu.VMEM((d, d), BF16), pltpu.VMEM((N_DEV, N_META, ms), F32),
           pltpu.VMEM((N_DEV, CONV_WIDTH, cs), F32), pltpu.VMEM((tm + HALO, cw), F32), pltpu.VMEM((tm + HALO, cw), F32),
           pltpu.SemaphoreType.DMA((2 * N_DEV + 2,))] + _gather_halves_sems(2) + _exchange_sems(n1 - 2) + _gather_halves_sems(n2),
        compiler_params=_params("arbitrary", "arbitrary"),
    )(x2d, *shards, g1, pool_w, pool_scale, g2)
    return out[:5], out[5:11], out[11 + n1:]


def _mixer_backward(x2d, dh1, m, z, pooled, mixed, meta, a_meta, z_meta, g1, w_in, conv_w, pool_w, pool_scale, w_out, g2, n_seq,
                    to_exchange, landing):
    t, d = x2d.shape
    zw = w_in.shape[1]
    cw = zw // 4
    s = t // n_seq
    tm = min(TM_MIX, s)
    nj = s // tm
    n_groups = len(POOL_WINDOWS)
    zs = zw // N_DEV
    nx = len(to_exchange)
    n_in = 17
    given = [k for k, a in enumerate(landing) if a is not None]
    fresh = [k for k, a in enumerate(landing) if a is None]

    def body(x_ref, dh1_ref, m_ref, z_ref, zprev_ref, pooled_ref, mixed_ref, meta_ref, am_ref, zm_ref, g1_ref, win_ref, conv_ref, pw_ref, ps_ref, wout_ref,
             g2_ref, *rest):
        sent, rest = rest[:nx], rest[nx + len(given):]
        gx_ref, dwin_ref, dwout_ref, dg1_ref, dg2_ref, dconv_ref, dpw_ref, dps_ref, dmeta_ref = rest[:9]
        landed, rest = rest[9:9 + nx], rest[9 + nx:]
        ext_u, ext_dyc, ext_dq, acc_win, acc_wout, stage16, sem = rest[:7]
        north = _core_exchange_ops(sent, landed, 1, *rest[7:10])
        south = _core_exchange_ops([sent[k] for k in fresh], [landed[k] for k in fresh], 0, *rest[10:13])

        def start():
            north[0]()
            south[0]()

        def finish():
            south[1]()
            north[1]()

        b_id, j = pl.program_id(0), pl.program_id(1)
        jr = nj - 1 - j
        pl.when((b_id == 0) & (j == 0))(start)

        @pl.when((b_id == 0) & (j == 0))
        def _():
            acc_win[...] = jnp.zeros_like(acc_win)
            acc_wout[...] = jnp.zeros_like(acc_wout)
            for r in (dg1_ref, dg2_ref, dconv_ref, dpw_ref, dps_ref, dmeta_ref):
                r[...] = jnp.zeros_like(r)

        @pl.when(j == 0)
        def _():
            ext_dyc[pl.ds(tm, HALO), :] = jnp.zeros((HALO, cw), F32)
            ext_dq[pl.ds(tm, HALO), :] = jnp.zeros((HALO, cw), F32)

        zm = zm_ref[...]
        halo = jnp.where(jr == 0, zm, zprev_ref[...].astype(F32))
        ext_u[pl.ds(0, HALO), :] = halo[:, cw:2 * cw] * halo[:, 2 * cw:3 * cw]

        h0 = x_ref[...]
        hat0, rstd0 = _rms_stats(h0)
        g1 = g1_ref[...]
        a = (hat0 * g1).astype(BF16)
        b, c, v, u, u1, u2, yc = _conv_branch(z_ref[...].astype(F32), ext_u, conv_ref, tm)
        mixed = [mixed_ref[:, pl.ds(POOL_GROUP * g, POOL_GROUP)].astype(F32) for g in range(n_groups)]
        ps = ps_ref[...]
        y = [b * yc] + [mixed[g] * ps[:, POOL_GROUP * g:POOL_GROUP * (g + 1)] for g in range(n_groups)]
        ycat = jnp.concatenate(y, axis=1).astype(BF16)

        dh1v = dh1_ref[...]
        m_hat, m_rstd = _rms_stats(m_ref[...])
        dm, dg2 = _rms_bwd(m_hat, m_rstd, g2_ref[...], dh1v)
        dg2_ref[...] += dg2
        dm = dm.astype(BF16)
        acc_wout[...] += _dot_tn(ycat, dm)
        dycat = _dot_nt(dm, wout_ref[...])

        dyconv = dycat[:, :cw]
        db = dyconv * yc
        dyc = dyconv * b
        ext_dyc[pl.ds(0, tm), :] = dyc
        du = (conv_ref[pl.ds(2, 1), :] * dyc + conv_ref[pl.ds(1, 1), :] * ext_dyc[pl.ds(1, tm), :]
              + conv_ref[pl.ds(0, 1), :] * ext_dyc[pl.ds(2, tm), :])
        dconv_ref[pl.ds(2, 1), :] += jnp.sum(dyc * u, axis=0, keepdims=True)
        dconv_ref[pl.ds(1, 1), :] += jnp.sum(dyc * u1, axis=0, keepdims=True)
        dconv_ref[pl.ds(0, 1), :] += jnp.sum(dyc * u2, axis=0, keepdims=True)

        dp = []
        for g, win in enumerate(POOL_WINDOWS):
            lanes = pl.ds(POOL_GROUP * g, POOL_GROUP)
            dypool = dycat[:, cw + POOL_GROUP * g:cw + POOL_GROUP * (g + 1)]
            dps_ref[:, lanes] += jnp.sum(dypool * mixed[g], axis=0, keepdims=True)
            dmixed = (dypool * ps[:, POOL_GROUP * g:POOL_GROUP * (g + 1)]).astype(BF16)
            dpw_ref[g] += _dot_tn(pooled_ref[:, lanes], dmixed)
            dq = _dot_nt(dmixed, pw_ref[g].astype(BF16))
            ext_dq[pl.ds(0, tm), lanes] = dq
            acc = _window_sum(ext_dq[:, lanes], win, ahead=True)[0:tm, :]
            dp.append(acc * (1.0 / win) - dq)

        dz = jnp.concatenate([db, du * v, du * c] + dp, axis=1).astype(BF16)
        acc_win[...] += _dot_tn(a, dz)
        dh0, dg1 = _rms_bwd(hat0, rstd0, g1, _dot_nt(dz, win_ref[...]))
        dg1_ref[...] += dg1
        gx_ref[...] = dh1v + dh0

        ext_dyc[pl.ds(tm, HALO), :] = ext_dyc[pl.ds(0, HALO), :]
        ext_dq[pl.ds(tm, HALO), :] = ext_dq[pl.ds(0, HALO), :]

        @pl.when(jr == 0)
        def _():
            ext_dyc[pl.ds(tm - HALO, HALO), :] = jnp.zeros((HALO, cw), F32)
            ext_dq[pl.ds(tm - HALO, HALO), :] = jnp.zeros((HALO, cw), F32)
            du_m = (conv_ref[pl.ds(1, 1), :] * ext_dyc[pl.ds(tm - HALO + 1, HALO), :]
                    + conv_ref[pl.ds(0, 1), :] * ext_dyc[pl.ds(tm - HALO + 2, HALO), :])
            dp_m = []
            for g, win in enumerate(POOL_WINDOWS):
                lanes = pl.ds(POOL_GROUP * g, POOL_GROUP)
                acc = ext_dq[pl.ds(tm - HALO + 1, HALO), lanes]
                for k in range(2, win):
                    acc = acc + ext_dq[pl.ds(tm - HALO + k, HALO), lanes]
                dp_m.append(acc * (1.0 / win))
            dz_m = jnp.concatenate([jnp.zeros((HALO, cw), F32), du_m * zm[:, 2 * cw:3 * cw], du_m * zm[:, cw:2 * cw]] + dp_m,
                                   axis=1).astype(BF16)
            acc_win[...] += _dot_tn(am_ref[...], dz_m)
            hat_m, rstd_m = _rms_stats(meta_ref[...])
            dmeta, dg1_m = _rms_bwd(hat_m, rstd_m, g1, _dot_nt(dz_m, win_ref[...]))
            dg1_ref[...] += dg1_m
            dmeta_ref[...] += dmeta

        @pl.when((b_id == n_seq - 1) & (j == nj - 1))
        def _():
            pieces = [(acc_win, zs * i, dwin_ref.at[i]) for i in range(N_DEV)]
            pieces += [(acc_wout, zs * i, dwout_ref.at[:, pl.ds(zs * i, zs)]) for i in range(d // zs)]
            copies = []
            for k, (acc, col, dst) in enumerate(pieces):
                if k >= 2:
                    copies[k - 2].wait()
                stage16[k % 2] = acc[:, pl.ds(col, zs)].astype(BF16)
                copies.append(pltpu.make_async_copy(stage16.at[k % 2], dst, sem.at[k % 2]))
                copies[k].start()
            copies[-2].wait()
            copies[-1].wait()
            finish()

    row = lambda b, j: (b * nj + nj - 1 - j, 0)
    prev = lambda b, j: (jnp.maximum((b * s + (nj - 1 - j) * tm) // HALO - 1, 0), 0)
    small = [g1.shape, g2.shape, conv_w.shape, pool_w.shape, pool_scale.shape, meta.shape]
    out = pl.pallas_call(
        body, name="mixer_backward", grid=(n_seq, nj),
        in_specs=[pl.BlockSpec((tm, d), row), pl.BlockSpec((tm, d), row), pl.BlockSpec((tm, d), row), pl.BlockSpec((tm, zw), row),
                  pl.BlockSpec((HALO, zw), prev), pl.BlockSpec((tm, cw), row), pl.BlockSpec((tm, cw), row), _const(meta.shape), _const(a_meta.shape), _const(z_meta.shape), _const(g1.shape),
                  _resident(w_in.shape), _const(conv_w.shape), _const(pool_w.shape), _const(pool_scale.shape), _resident(w_out.shape),
                  _const(g2.shape)] + [ANY] * (nx + len(given)),
        out_specs=[pl.BlockSpec((tm, d), row), ANY, ANY] + [_const(sh) for sh in small] + [ANY] * nx,
        out_shape=[jax.ShapeDtypeStruct((t, d), F32), jax.ShapeDtypeStruct((N_DEV, d, zs), BF16),
                   jax.ShapeDtypeStruct(w_out.shape, BF16)] + [jax.ShapeDtypeStruct(sh, F32) for sh in small]
        + [jax.ShapeDtypeStruct((N_DEV, a.shape[0] // N_DEV, a.shape[1]), a.dtype) for a in to_exchange],
        input_output_aliases={n_in + nx + at: 9 + k for at, k in enumerate(given)},
        scratch_shapes=[pltpu.VMEM((tm + HALO, cw), F32)] * 3
        + [pltpu.VMEM(w_in.shape, F32), pltpu.VMEM(w_out.shape, F32), pltpu.VMEM((2, d, zs), BF16),
           pltpu.SemaphoreType.DMA((2,))] + _core_exchange_sems(nx) + _core_exchange_sems(len(fresh)),
        compiler_params=_params("arbitrary", "arbitrary"),
    )(x2d, dh1, m, z, z, pooled, mixed, meta, a_meta, z_meta, g1, w_in, conv_w, pool_w, pool_scale, w_out, g2, *to_exchange, *[landing[k] for k in given])
    return out[:9], out[9:]


def _ffn_forward_backward(h1, target, g3, w_gate, w_up, w_down, g4):
    t, d = h1.shape
    ff = w_gate.shape[0]
    tm = min(TM_FFN, t)
    nt = t // tm
    chunks = [(s, min(FFN_CHUNK, ff - s)) for s in range(0, ff, FFN_CHUNK)]

    def body(h1_ref, h1pp_ref, tgt_ref, g3_ref, wg_ref, wu_ref, wd_ref, g4_ref,
             f_ref, act_ref, dd_ref, dgate_ref, dup_ref, dh1_ref, loss_ref, dg3_ref, dg4_ref, *slots):
        gate_s, up_s, dd_s, dh2_s, df_s = slots
        i = pl.program_id(0)

        def forward(slot):
            h1v = h1_ref[...]
            hat, _ = _rms_stats(h1v)
            f = (hat * g3_ref[...]).astype(BF16)
            f_ref[...] = f
            s, n = chunks[0]
            gate, up = _dot_nt(f_ref[...], wg_ref[pl.ds(s, n), :]), _dot_nt(f_ref[...], wu_ref[pl.ds(s, n), :])
            yield
            down = None
            for k, (s, n) in enumerate(chunks):
                gate_s.at[slot][:, pl.ds(s, n)] = gate.astype(BF16)
                up_s.at[slot][:, pl.ds(s, n)] = up.astype(BF16)
                act = (gate * jax.nn.sigmoid(gate) * up).astype(BF16)
                act_ref[:, pl.ds(s, n)] = act
                if k + 1 < len(chunks):
                    s1, n1 = chunks[k + 1]
                    gate, up = _dot_nt(f_ref[...], wg_ref[pl.ds(s1, n1), :]), _dot_nt(f_ref[...], wu_ref[pl.ds(s1, n1), :])
                yield
                part = _dot(act_ref[:, pl.ds(s, n)], wd_ref[pl.ds(s, n), :])
                down = part if down is None else down + part
                yield
            d_hat, d_rstd = _rms_stats(down)
            g4 = g4_ref[...]
            err = h1v + d_hat * g4 - tgt_ref[...]
            loss_ref[...] += jnp.sum(err * err) * (0.5 / d)
            dh2 = err * (1.0 / d)
            dh2_s.at[slot][...] = dh2
            dd, dg4 = _rms_bwd(d_hat, d_rstd, g4, dh2)
            dg4_ref[...] += dg4
            dd = dd.astype(BF16)
            dd_ref[...] = dd
            dd_s.at[slot][...] = dd

        def backward(slot):
            s, n = chunks[0]
            dact = _dot_nt(dd_s.at[slot][...], wd_ref[pl.ds(s, n), :])
            yield
            df = None
            for k, (s, n) in enumerate(chunks):
                gate = gate_s.at[slot][:, pl.ds(s, n)].astype(F32)
                up = up_s.at[slot][:, pl.ds(s, n)].astype(F32)
                sig = jax.nn.sigmoid(gate)
                dup = (dact * (gate * sig)).astype(BF16)
                dgate = (dact * up * (sig * (1.0 + gate * (1.0 - sig)))).astype(BF16)
                dup_ref[:, pl.ds(s, n)] = dup
                dgate_ref[:, pl.ds(s, n)] = dgate
                if k + 1 < len(chunks):
                    s1, n1 = chunks[k + 1]
                    dact = _dot_nt(dd_s.at[slot][...], wd_ref[pl.ds(s1, n1), :])
                yield
                part = _dot(dgate_ref[:, pl.ds(s, n)], wg_ref[pl.ds(s, n), :]) + _dot(dup_ref[:, pl.ds(s, n)], wu_ref[pl.ds(s, n), :])
                df = part if df is None else df + part
                yield
            df_s.at[slot][...] = df

        def last(slot):
            hat, rstd = _rms_stats(h1pp_ref[...])
            dh1, dg3 = _rms_bwd(hat, rstd, g3_ref[...], df_s.at[slot][...])
            dg3_ref[...] += dg3
            dh1_ref[...] = dh2_s.at[slot][...] + dh1

        def emit(parity, with_forward, with_backward, with_last):
            fwd = forward(parity) if with_forward else iter(())
            bwd = backward(1 - parity) if with_backward else iter(())
            next(fwd, None)
            if with_last:
                last(parity)
            for _ in range(FFN_BACKWARD_LAG):
                next(fwd, None)
            alive = True
            while alive:
                alive = next(bwd, True) is None
                alive = (next(fwd, True) is None) or alive

        @pl.when(i == 0)
        def _():
            for r in (loss_ref, dg3_ref, dg4_ref, *slots):
                r[...] = jnp.zeros_like(r)

        @pl.when(i < nt)
        def _():
            emit(i % 2, True, True, True)

        @pl.when(i == nt)
        def _():
            emit(nt % 2, False, True, True)

        @pl.when(i == nt + 1)
        def _():
            emit((nt + 1) % 2, False, False, True)

    cur = lambda i: (jnp.minimum(i, nt - 1), 0)
    prev = lambda i: (jnp.clip(i - 1, 0, nt - 1), 0)
    prev2 = lambda i: (jnp.clip(i - 2, 0, nt - 1), 0)
    return pl.pallas_call(
        body, name="ffn_forward_backward", grid=(nt + 2,),
        in_specs=[pl.BlockSpec((tm, d), cur), pl.BlockSpec((tm, d), prev2), pl.BlockSpec((tm, d), cur), _const(g3.shape),
                  _resident(w_gate.shape), _resident(w_up.shape), _resident(w_down.shape), _const(g4.shape)],
        out_specs=[pl.BlockSpec((tm, d), cur), pl.BlockSpec((tm, ff), cur), pl.BlockSpec((tm, d), cur), pl.BlockSpec((tm, ff), prev),
                   pl.BlockSpec((tm, ff), prev), pl.BlockSpec((tm, d), prev2), _const((8, 128)), _const(g3.shape), _const(g4.shape)],
        out_shape=[jax.ShapeDtypeStruct((t, d), BF16), jax.ShapeDtypeStruct((t, ff), BF16), jax.ShapeDtypeStruct((t, d), BF16),
                   jax.ShapeDtypeStruct((t, ff), BF16), jax.ShapeDtypeStruct((t, ff), BF16), jax.ShapeDtypeStruct((t, d), F32),
                   jax.ShapeDtypeStruct((8, 128), F32), jax.ShapeDtypeStruct(g3.shape, F32), jax.ShapeDtypeStruct(g4.shape, F32)],
        scratch_shapes=[pltpu.VMEM((2, tm, ff), BF16)] * 2 + [pltpu.VMEM((2, tm, d), BF16)] + [pltpu.VMEM((2, tm, d), F32)] * 2,
        compiler_params=_params("arbitrary"),
    )(h1, h1, target, g3, w_gate, w_up, w_down, g4)


def _ffn_weight_grads(f, dd, dgate, dup, act):
    t, d = f.shape
    ff = dgate.shape[1]
    tm = min(TM_WGRAD, t)
    nt = t // tm
    fc = ff // FF_CHUNKS
    assert FF_CHUNKS == 2

    def body(f_ref, dd_ref, dgate_ref, dup_ref, act_ref, dwg_ref, dwu_ref, dwd_ref, *rest):
        landing, (acc_g, acc_u, acc_d, stage, sem) = rest[:2], rest[2:7]
        start, finish = _core_exchange_ops([dwg_ref, dwd_ref], landing, 0, *rest[7:])
        c, i = pl.program_id(0), pl.program_id(1)
        pl.when((c == 1) & (i == 0))(start)

        @pl.when(i == 0)
        def _():
            acc_g[...] = jnp.zeros_like(acc_g)
            acc_u[...] = jnp.zeros_like(acc_u)
            acc_d[...] = jnp.zeros_like(acc_d)

        fv = f_ref[...]
        acc_g[...] += _dot_tn(fv, dgate_ref[...])
        acc_u[...] += _dot_tn(fv, dup_ref[...])
        acc_d[...] += _dot_tn(act_ref[...], dd_ref[...])

        @pl.when(i == nt - 1)
        def _():
            rows = pl.ds(pl.multiple_of(c * fc, 16), fc)
            copies = []
            for k, (acc, out, transposed) in enumerate(((acc_d, dwd_ref, False), (acc_g, dwg_ref, True), (acc_u, dwu_ref, True))):
                if k >= 2:
                    copies[k - 2].wait()
                stage[k % 2] = (acc[...].T if transposed else acc[...]).astype(BF16)
                copies.append(pltpu.make_async_copy(stage.at[k % 2], out.at[rows, :], sem.at[k % 2]))
                copies[k].start()
            copies[-2].wait()
            copies[-1].wait()

        pl.when((c == 1) & (i == nt - 1))(finish)

    row = lambda c, i: (i, 0)
    col = lambda c, i: (i, c)
    out = pl.pallas_call(
        body, name="ffn_weight_grads", grid=(FF_CHUNKS, nt),
        in_specs=[pl.BlockSpec((tm, d), row), pl.BlockSpec((tm, d), row), pl.BlockSpec((tm, fc), col), pl.BlockSpec((tm, fc), col),
                  pl.BlockSpec((tm, fc), col)],
        out_specs=[ANY] * 5,
        out_shape=[jax.ShapeDtypeStruct((ff, d), BF16)] * 3 + [jax.ShapeDtypeStruct((N_DEV, ff // N_DEV, d), BF16)] * 2,
        scratch_shapes=[pltpu.VMEM((d, fc), F32), pltpu.VMEM((d, fc), F32), pltpu.VMEM((fc, d), F32), pltpu.VMEM((2, fc, d), BF16),
                        pltpu.SemaphoreType.DMA((2,))] + _core_exchange_sems(2),
        compiler_params=_params("arbitrary", "arbitrary"),
    )(f, dd, dgate, dup, act)
    return out[:3], [out[3], None, out[4]]


def _adamw(w, g, m, v):
    m = ADAM_B1 * m + (1.0 - ADAM_B1) * g
    v = ADAM_B2 * v + (1.0 - ADAM_B2) * (g * g)
    m_hat = m / (1.0 - ADAM_B1 ** ADAM_STEP)
    v_hat = v / (1.0 - ADAM_B2 ** ADAM_STEP)
    return -ADAM_LR * (m_hat / (jnp.sqrt(v_hat) + ADAM_EPS) + ADAM_WD * w), m, v


def _sum_slabs(ref):
    total = ref[0].astype(F32)
    for i in range(1, ref.shape[0]):
        total = total + ref[i].astype(F32)
    return total


def _adamw_rows(r, c):
    tr = r
    for cand in range(8, r, 8):
        if r % cand == 0 and cand * c <= ADAMW_BLOCK_ELEMS:
            tr = cand
    return r if r * c <= ADAMW_BLOCK_ELEMS else tr


def _reduce_adamw_carrying(parts, ws, ms, vs, to_reduce, to_exchange, whole, name):
    k, nr, nx = len(ws), len(to_reduce), len(to_exchange)
    r, c = ws[0].shape if k else (8, 128)
    tr = _adamw_rows(r, c)
    steps = r // tr
    travels = nr + nx > 0
    chip_slabs = [jax.ShapeDtypeStruct((N_CHIP, *a.shape[1:]), a.dtype) for a in to_reduce]

    def body(*refs):
        p_refs, w_refs, m_refs, v_refs = (refs[a * k:(a + 1) * k] for a in range(4))
        refs = refs[4 * k:]
        reduced_in, sent, refs = refs[:nr], refs[nr:nr + nx], refs[nr + nx:]
        outs, pairs, sums, landed, refs = refs[:4 * k], refs[4 * k:4 * k + nr], refs[4 * k + nr:4 * k + 2 * nr], \
            refs[4 * k + 2 * nr:4 * k + 2 * nr + nx], refs[4 * k + 2 * nr + nx:]
        mine_v, pair_v, sum_v, refs = refs[:nr], refs[nr:2 * nr], refs[2 * nr:3 * nr], refs[3 * nr:]
        if travels:
            reduce_ops = _pair_then_chip_ops(reduced_in, pairs, sums, mine_v, pair_v, sum_v, *refs[:7])
            direct_ops = _exchange_ops(sent, landed, whole, *refs[7:])

            @pl.when(pl.program_id(0) == 0)
            def _():
                direct_ops[0]()
                reduce_ops[0]()

        for a in range(k):
            g = _sum_slabs(p_refs[a])
            outs[4 * a][...] = g
            outs[4 * a + 1][...], outs[4 * a + 2][...], outs[4 * a + 3][...] = _adamw(w_refs[a][...], g, m_refs[a][...], v_refs[a][...])

        if travels:
            @pl.when(pl.program_id(0) == steps - 1)
            def _():
                reduce_ops[1]()
                direct_ops[1]()

    blk = pl.BlockSpec((tr, c), lambda i: (i, 0))
    out = pl.pallas_call(
        body, name=name, grid=(steps,),
        in_specs=[pl.BlockSpec((N_DEV, tr, c), lambda i: (0, i, 0))] * k + [blk] * (3 * k) + [ANY] * (nr + nx),
        out_specs=[blk] * (4 * k) + [ANY] * (2 * nr + nx),
        out_shape=[jax.ShapeDtypeStruct((r, c), F32)] * (4 * k) + chip_slabs + chip_slabs
        + [jax.ShapeDtypeStruct((N_DEV, *a.shape) if w else a.shape, a.dtype) for a, w in zip(to_exchange, whole)],
        scratch_shapes=([pltpu.VMEM(a.shape, a.dtype) for a in chip_slabs] * 3 + _pair_then_chip_sems(nr) + _exchange_sems(nx)
                        if travels else []),
        compiler_params=_params("arbitrary"),
    )(*parts, *ws, *ms, *vs, *to_reduce, *to_exchange)
    return [tuple(out[4 * a:4 * a + 4]) for a in range(k)], out[4 * k + nr:4 * k + 2 * nr], out[4 * k + 2 * nr:]


def _reduce_adamw(parts, w, m, v, name):
    r, c = w.shape
    tr = _adamw_rows(r, c)

    def body(p_ref, w_ref, m_ref, v_ref, g_out, d_out, m_out, v_out):
        g = _sum_slabs(p_ref)
        g_out[...] = g
        d_out[...], m_out[...], v_out[...] = _adamw(w_ref[...], g, m_ref[...], v_ref[...])

    blk = pl.BlockSpec((tr, c), lambda i: (i, 0))
    return pl.pallas_call(
        body, name=name, grid=(r // tr,),
        in_specs=[pl.BlockSpec((parts.shape[0], tr, c), lambda i: (0, i, 0)), blk, blk, blk],
        out_specs=[blk] * 4, out_shape=[jax.ShapeDtypeStruct((r, c), F32)] * 4,
        compiler_params=_params("arbitrary"),
    )(parts, w, m, v)


def _reduce_adamw_small(parts, ws, ms, vs, loss_parts):
    n = len(parts)

    def body(*refs):
        p_refs, w_refs, m_refs, v_refs = (refs[k * n:(k + 1) * n] for k in range(4))
        outs = refs[4 * n + 1:]
        outs[4 * n][...] = _sum_slabs(refs[4 * n])
        for a in range(n):
            g = _sum_slabs(p_refs[a])
            outs[4 * a][...] = g
            outs[4 * a + 1][...], outs[4 * a + 2][...], outs[4 * a + 3][...] = _adamw(w_refs[a][...], g, m_refs[a][...], v_refs[a][...])

    out = pl.pallas_call(
        body, name="adamw_replicated",
        out_shape=[jax.ShapeDtypeStruct(w.shape, F32) for w in ws for _ in range(4)] + [jax.ShapeDtypeStruct(loss_parts.shape[1:], F32)],
        compiler_params=pltpu.CompilerParams(vmem_limit_bytes=VMEM_LIMIT_BYTES),
    )(*parts, *ws, *ms, *vs, loss_parts)
    return [tuple(out[4 * a:4 * a + 4]) for a in range(n)], out[4 * n]


def kernel(x, meta_tokens, norm_mix_pre, w_in, conv_w, pool_w, pool_scale, w_out, norm_mix_post, norm_ffn_pre, w_gate, w_up, w_down, norm_ffn_post, loss_target, m_meta_tokens, m_norm_mix_pre, m_w_in, m_conv_w, m_pool_w, m_pool_scale, m_w_out, m_norm_mix_post, m_norm_ffn_pre, m_w_gate, m_w_up, m_w_down, m_norm_ffn_post, v_meta_tokens, v_norm_mix_pre, v_w_in, v_conv_w, v_pool_w, v_pool_scale, v_w_out, v_norm_mix_post, v_norm_ffn_pre, v_w_gate, v_w_up, v_w_down, v_norm_ffn_post):
    n_seq, seq, d = x.shape
    x2d = x.reshape(n_seq * seq, d)
    target = loss_target.reshape(n_seq * seq, d)

    t_ = lambda a: jnp.swapaxes(a[0], 0, 1)
    pw, ps = pool_w[0], pool_scale

    (h1, z, m, pooled, mixed), (win_b, wout_b, meta, conv, a_meta, z_meta), ffn_slabs = _gather_and_mixer_forward(
        x2d, [w_in[0], w_out[0], meta_tokens, conv_w[0]], [t_(w_gate), t_(w_up), w_down[0]], norm_mix_pre, pw, ps, norm_mix_post, n_seq)
    wg_b, wu_b, wd_b = (s.reshape(-1, d) for s in ffn_slabs)
    f, act, dd, dgate, dup, dh1, loss_sum, dg3, dg4 = _ffn_forward_backward(h1, target, norm_ffn_pre, wg_b, wu_b, wd_b, norm_ffn_post)
    ffn_grads, landing = _ffn_weight_grads(f, dd, dgate, dup, act)
    (gx, dwin, dwout, dg1, dg2, dconv, dpw, dps, dmeta), ffn_parts = _mixer_backward(
        x2d, dh1, m, z, pooled, mixed, meta, a_meta, z_meta, norm_mix_pre, win_b, conv, pw, ps, wout_b, norm_mix_post, n_seq,
        ffn_grads, landing)

    dmeta_s = jnp.transpose(dmeta.reshape(N_META, N_DEV, -1), (1, 0, 2))
    dconv_s = jnp.transpose(dconv.reshape(CONV_WIDTH, N_DEV, -1), (1, 0, 2))
    _, (win_parts, wout_parts), last = _reduce_adamw_carrying(
        [], [], [], [], [dwin, dwout.reshape(N_DEV, -1, d)], [dmeta_s, dconv_s, dg1, dg2, dg3, dg4, dpw, dps, loss_sum],
        [False] * 2 + [True] * 7, "exchange_rest")
    ffn_res, _, _ = _reduce_adamw_carrying(
        ffn_parts, [t_(w_gate), t_(w_up), w_down[0]], [t_(m_w_gate), t_(m_w_up), m_w_down[0]], [t_(v_w_gate), t_(v_w_up), v_w_down[0]],
        [], [], [], "adamw_ffn")
    replicated = last[2:8]

    names = ["meta_tokens", "norm_mix_pre", "w_in", "conv_w", "pool_w", "pool_scale", "w_out", "norm_mix_post", "norm_ffn_pre", "w_gate",
             "w_up", "w_down", "norm_ffn_post"]
    res = {"w_gate": tuple(jnp.swapaxes(o, 0, 1)[None] for o in ffn_res[0]),
           "w_up": tuple(jnp.swapaxes(o, 0, 1)[None] for o in ffn_res[1]), "w_down": tuple(o[None] for o in ffn_res[2])}
    for nm, parts, w, m_, v_ in (("w_in", win_parts, w_in, m_w_in, v_w_in), ("w_out", wout_parts, w_out, m_w_out, v_w_out),
                                 ("conv_w", last[1], conv_w, m_conv_w, v_conv_w)):
        res[nm] = tuple(o[None] for o in _reduce_adamw(parts, w[0], m_[0], v_[0], "adamw_" + nm))
    res["meta_tokens"] = tuple(_reduce_adamw(last[0], meta_tokens, m_meta_tokens, v_meta_tokens, "adamw_meta_tokens"))
    small, loss = _reduce_adamw_small(
        replicated, [norm_mix_pre, norm_mix_post, norm_ffn_pre, norm_ffn_post, pool_w[0], pool_scale],
        [m_norm_mix_pre, m_norm_mix_post, m_norm_ffn_pre, m_norm_ffn_post, m_pool_w[0], m_pool_scale],
        [v_norm_mix_pre, v_norm_mix_post, v_norm_ffn_pre, v_norm_ffn_post, v_pool_w[0], v_pool_scale], last[8])
    for nm, r in zip(["norm_mix_pre", "norm_mix_post", "norm_ffn_pre", "norm_ffn_post", "pool_w", "pool_scale"], small):
        res[nm] = tuple(o[None] for o in r) if nm == "pool_w" else r

    return (loss[0, 0], gx.reshape(n_seq, seq, d), *[res[nm][0] for nm in names], *[res[nm][1] for nm in names],
            *[res[nm][2] for nm in names], *[res[nm][3] for nm in names])
```

```python
import functools

import jax
import jax.numpy as jnp
from jax import lax
from jax.experimental import pallas as pl
from jax.experimental.pallas import tpu as pltpu

F32, BF16 = jnp.float32, jnp.bfloat16
RMS_EPS = 1e-6
N_META = 16
CONV_WIDTH = 3
POOL_WINDOWS = (2, 4, 8, 16)
POOL_GROUP = 128
HALO = 16
N_DEV = 8
MESH_AXES = ("x", "y", "c")
MESH = pl.DeviceIdType.MESH
VMEM_LIMIT_BYTES = 56 * 1024 * 1024
ADAMW_BLOCK_ELEMS = 64 * 1024
TM_MIX = 512
TM_FFN = 256
FFN_CHUNK = 512
FFN_BACKWARD_LAG = 2
TM_WGRAD = 512
FF_CHUNKS = 2

ADAM_LR, ADAM_B1, ADAM_B2, ADAM_EPS, ADAM_WD, ADAM_STEP = 0.001, 0.9, 0.999, 1e-08, 0.01, 10


def _dot(a, b):
    return jnp.dot(a, b, preferred_element_type=F32)


def _dot_nt(a, b):
    return lax.dot_general(a, b, (((1,), (1,)), ((), ())), preferred_element_type=F32)


def _dot_tn(a, b):
    return lax.dot_general(a, b, (((0,), (0,)), ((), ())), preferred_element_type=F32)


def _rms_stats(h):
    rstd = lax.rsqrt(jnp.mean(h * h, axis=-1, keepdims=True) + RMS_EPS)
    return h * rstd, rstd


def _rms_bwd(hat, rstd, g, dy):
    gdy = dy * g
    proj = jnp.mean(gdy * hat, axis=-1, keepdims=True)
    return rstd * (gdy - hat * proj), jnp.sum(dy * hat, axis=0, keepdims=True)


def _params(*semantics):
    return pltpu.CompilerParams(dimension_semantics=semantics or None, vmem_limit_bytes=VMEM_LIMIT_BYTES)


def _resident(shape):
    zeros = (0,) * len(shape)
    return pl.BlockSpec(shape, lambda *_: zeros, pipeline_mode=pl.Buffered(1))


def _const(shape):
    zeros = (0,) * len(shape)
    return pl.BlockSpec(shape, lambda *_: zeros)


ANY = pl.BlockSpec(memory_space=pl.ANY)


def _my_place():
    x, y, c = (lax.axis_index(a) for a in MESH_AXES)
    return x, y, c


def _exchange_sems(n):
    return [pltpu.SemaphoreType.DMA((n, N_DEV - 1)), pltpu.SemaphoreType.DMA((n, N_DEV - 1)), pltpu.SemaphoreType.DMA((n,))]


def _gather_ops(srcs, outs, send_sems, recv_sems, local_sems, core_major=False):
    n = len(srcs)
    x, y, c = _my_place()
    me, sibling = (x, y, c), (x, y, 1 - c)
    chips = [(1 - x, y), (x, 1 - y), (1 - x, 1 - y)]

    def slab(px, py, pc):
        return 4 * pc + 2 * px + py if core_major else 4 * px + 2 * py + pc

    def copy(a, k, block, to, src=None):
        dst = outs[a].at[slab(*block)]
        return pltpu.make_async_remote_copy(
            src_ref=dst if src is None else src, dst_ref=dst, send_sem=send_sems.at[a, k], recv_sem=recv_sems.at[a, k],
            device_id=to, device_id_type=MESH)

    def mine(a):
        return pltpu.make_async_copy(srcs[a], outs[a].at[slab(*me)], local_sems.at[a])

    def first(a):
        return [copy(a, 0, me, sibling, src=srcs[a])] + [copy(a, 1 + j, me, (*chip, c), src=srcs[a]) for j, chip in enumerate(chips)]

    def passed(a, j):
        return copy(a, 4 + j, (*chips[j], c), sibling)

    def start():
        for a in range(n):
            mine(a).start()
            for cp in first(a):
                cp.start()

    def forward():
        for j, chip in enumerate(chips):
            for a in range(n):
                copy(a, 1 + j, (*chip, c), me).wait_recv()
                passed(a, j).start()

    def finish():
        for a in range(n):
            copy(a, 0, sibling, me).wait_recv()
            for j, chip in enumerate(chips):
                copy(a, 4 + j, (*chip, 1 - c), me).wait_recv()
        for a in range(n):
            for cp in first(a) + [passed(a, j) for j in range(len(chips))]:
                cp.wait_send()
            mine(a).wait()

    return start, forward, finish


N_GATHER_COPIES = 10


def _gather_halves_sems(n):
    return [pltpu.SemaphoreType.DMA((n, N_GATHER_COPIES)), pltpu.SemaphoreType.DMA((n, N_GATHER_COPIES)), pltpu.SemaphoreType.DMA((n,))]


def _gather_halves_ops(srcs, outs, send_sems, recv_sems, local_sems, core_major=False):
    n = len(srcs)
    x, y, c = _my_place()
    me, sibling = (x, y, c), (x, y, 1 - c)
    xn, yn, dg = (1 - x, y, c), (x, 1 - y, c), (1 - x, 1 - y, c)

    def slab(px, py, pc):
        return 4 * pc + 2 * px + py if core_major else 4 * px + 2 * py + pc

    def half(ref, h):
        rows = ref.shape[0] // 2
        return ref if h is None else ref.at[pl.ds(h * rows, rows)]

    def copy(a, k, block, h, to, from_src=False):
        dst = half(outs[a].at[slab(*block)], h)
        return pltpu.make_async_remote_copy(
            src_ref=half(srcs[a], h) if from_src else dst, dst_ref=dst, send_sem=send_sems.at[a, k], recv_sem=recv_sems.at[a, k],
            device_id=to, device_id_type=MESH)

    def mine(a):
        return pltpu.make_async_copy(srcs[a], outs[a].at[slab(*me)], local_sems.at[a])

    def own(a):
        return [copy(a, 0, me, None, sibling, True), copy(a, 1, me, 0, xn, True), copy(a, 2, me, 1, xn, True),
                copy(a, 3, me, 1, yn, True), copy(a, 4, me, 0, yn, True)]

    def passed(a, k):
        block, h, to = [(xn, 0, yn), (yn, 1, xn), (xn, None, sibling), (yn, None, sibling), (dg, None, sibling)][k - 5]
        return copy(a, k, block, h, to)

    def start():
        for a in range(n):
            mine(a).start()
            for cp in own(a):
                cp.start()

    def pass_on():
        for a in range(n):
            copy(a, 1, xn, 0, me).wait_recv()
            passed(a, 5).start()
            copy(a, 3, yn, 1, me).wait_recv()
            passed(a, 6).start()

    def forward():
        for a in range(n):
            copy(a, 2, xn, 1, me).wait_recv()
            passed(a, 7).start()
            copy(a, 4, yn, 0, me).wait_recv()
            passed(a, 8).start()
        for a in range(n):
            copy(a, 5, dg, 0, me).wait_recv()
            copy(a, 6, dg, 1, me).wait_recv()
            passed(a, 9).start()

    def finish():
        sib = 1 - c
        for a in range(n):
            copy(a, 0, sibling, None, me).wait_recv()
            copy(a, 7, (1 - x, y, sib), None, me).wait_recv()
            copy(a, 8, (x, 1 - y, sib), None, me).wait_recv()
            copy(a, 9, (1 - x, 1 - y, sib), None, me).wait_recv()
        for a in range(n):
            for cp in own(a) + [passed(a, k) for k in range(5, N_GATHER_COPIES)]:
                cp.wait_send()
            mine(a).wait()

    return start, pass_on, forward, finish


def _exchange_ops(ins, outs, whole, send_sems, recv_sems, local_sems):
    n = len(ins)
    x, y, c = _my_place()
    me = 4 * x + 2 * y + c

    def src(a, i):
        return ins[a] if whole[a] else ins[a].at[i]

    def mine(a):
        return pltpu.make_async_copy(src(a, me), outs[a].at[me], local_sems.at[a])

    def send(a, k):
        to = (me + k) % N_DEV
        return pltpu.make_async_remote_copy(
            src_ref=src(a, to), dst_ref=outs[a].at[me], send_sem=send_sems.at[a, k - 1], recv_sem=recv_sems.at[a, k - 1],
            device_id=(to // 4, (to // 2) % 2, to % 2), device_id_type=MESH)

    def landed(a, k):
        frm = (me + N_DEV - k) % N_DEV
        return pltpu.make_async_remote_copy(
            src_ref=src(a, frm), dst_ref=outs[a].at[frm], send_sem=send_sems.at[a, k - 1], recv_sem=recv_sems.at[a, k - 1],
            device_id=(x, y, c), device_id_type=MESH)

    def start():
        for a in range(n):
            mine(a).start()
            for k in range(1, N_DEV):
                send(a, k).start()

    def finish():
        for a in range(n):
            for k in range(1, N_DEV):
                landed(a, k).wait_recv()
        for a in range(n):
            for k in range(1, N_DEV):
                send(a, k).wait_send()
            mine(a).wait()

    return start, finish


def _core_exchange_sems(n):
    return [pltpu.SemaphoreType.DMA((n, 4)), pltpu.SemaphoreType.DMA((n, N_DEV)), pltpu.SemaphoreType.DMA((n,))]


def _core_exchange_ops(ins, outs, to_core, send_sems, recv_sems, local_sems):
    n = len(ins)
    x, y, c = _my_place()
    me = 4 * x + 2 * y + c
    others = [(0, 1), (1, 0), (1, 1)]

    def slab(a, p):
        if len(ins[a].shape) == len(outs[a].shape):
            return ins[a].at[p]
        rows = outs[a].shape[1]
        return ins[a].at[pl.ds(pl.multiple_of(p * rows, 16), rows), :]

    def send(a, dx, dy):
        tx, ty = (x + dx) % 2, (y + dy) % 2
        return pltpu.make_async_remote_copy(
            src_ref=slab(a, 4 * to_core + 2 * tx + ty), dst_ref=outs[a].at[me], send_sem=send_sems.at[a, 2 * dx + dy],
            recv_sem=recv_sems.at[a, 2 * (2 * dx + dy) + c], device_id=(tx, ty, to_core), device_id_type=MESH)

    def mine(a):
        return pltpu.make_async_copy(slab(a, 4 * to_core + 2 * x + y), outs[a].at[me], local_sems.at[a])

    def landed(a, dx, dy, sc):
        frm = 4 * ((x + dx) % 2) + 2 * ((y + dy) % 2) + sc
        return pltpu.make_async_remote_copy(
            src_ref=slab(a, 0), dst_ref=outs[a].at[frm], send_sem=send_sems.at[a, 0], recv_sem=recv_sems.at[a, 2 * (2 * dx + dy) + sc],
            device_id=(x, y, c), device_id_type=MESH)

    def start():
        for a in range(n):
            for dx, dy in others:
                send(a, dx, dy).start()
            pl.when(c == to_core)(mine(a).start)
            pl.when(c != to_core)(send(a, 0, 0).start)

    def finish():
        @pl.when(c == to_core)
        def _():
            for a in range(n):
                for dx, dy in [(0, 0)] + others:
                    for sc in (0, 1):
                        if (dx, dy, sc) != (0, 0, to_core):
                            landed(a, dx, dy, sc).wait_recv()
            for a in range(n):
                mine(a).wait()

        @pl.when(c != to_core)
        def _():
            for a in range(n):
                send(a, 0, 0).wait_send()

        for a in range(n):
            for dx, dy in others:
                send(a, dx, dy).wait_send()

    return start, finish


N_CHIP = 4


def _pair_then_chip_sems(n):
    return [pltpu.SemaphoreType.DMA((n, N_CHIP)) for _ in range(6)] + [pltpu.SemaphoreType.DMA((n,))]


def _pair_then_chip_ops(ins, pairs, outs, mine_v, pair_v, sum_v, pair_send, pair_recv, chip_send, chip_recv, load_a, load_b, own_sem):
    n = len(ins)
    x, y, c = _my_place()
    chip = 2 * x + y
    chips = [(0, 0), (0, 1), (1, 0), (1, 1)]
    others = [(0, 1), (1, 0), (1, 1)]

    def to_sibling(a, j):
        px, py = chips[j]
        return pltpu.make_async_remote_copy(
            src_ref=ins[a].at[4 * px + 2 * py + 1 - c], dst_ref=pairs[a].at[j], send_sem=pair_send.at[a, j], recv_sem=pair_recv.at[a, j],
            device_id=(x, y, 1 - c), device_id_type=MESH)

    def spread(a, dx, dy):
        tx, ty = (x + dx) % 2, (y + dy) % 2
        return pltpu.make_async_remote_copy(
            src_ref=sum_v[a].at[2 * tx + ty], dst_ref=outs[a].at[chip], send_sem=chip_send.at[a, 2 * dx + dy],
            recv_sem=chip_recv.at[a, 2 * dx + dy], device_id=(tx, ty, c), device_id_type=MESH)

    def landed(a, dx, dy):
        frm = 2 * ((x + dx) % 2) + (y + dy) % 2
        return pltpu.make_async_remote_copy(
            src_ref=sum_v[a].at[0], dst_ref=outs[a].at[frm], send_sem=chip_send.at[a, 0], recv_sem=chip_recv.at[a, 2 * dx + dy],
            device_id=(x, y, c), device_id_type=MESH)

    def own(a):
        return pltpu.make_async_copy(sum_v[a].at[chip], outs[a].at[chip], own_sem.at[a])

    def pair():
        loads = []
        for a in range(n):
            for j, (px, py) in enumerate(chips):
                to_sibling(a, j).start()
                loads.append(pltpu.make_async_copy(ins[a].at[4 * px + 2 * py + c], mine_v[a].at[j], load_a.at[a, j]))
                loads[-1].start()
        for a in range(n):
            for j in range(N_CHIP):
                to_sibling(a, j).wait_recv()
                loads.append(pltpu.make_async_copy(pairs[a].at[j], pair_v[a].at[j], load_b.at[a, j]))
                loads[-1].start()
        for cp in loads:
            cp.wait()
        for a in range(n):
            sum_v[a][...] = (mine_v[a][...].astype(F32) + pair_v[a][...].astype(F32)).astype(sum_v[a].dtype)

    def start():
        pair()
        for a in range(n):
            own(a).start()
            for dx, dy in others:
                spread(a, dx, dy).start()

    def finish():
        for a in range(n):
            for dx, dy in others:
                landed(a, dx, dy).wait_recv()
        for a in range(n):
            for dx, dy in others:
                spread(a, dx, dy).wait_send()
            for j in range(N_CHIP):
                to_sibling(a, j).wait_send()
            own(a).wait()

    return start, finish


def _gather_first_weights(gathered, dtypes, cast_only):
    n, k = len(gathered), len(cast_only)

    def body(*refs):
        ins, casts_in = refs[:n], refs[n:n + k]
        outs, casts_out = refs[n + k:2 * n + k], refs[2 * n + k:2 * n + 2 * k]
        stages = refs[2 * n + 2 * k:3 * n + 2 * k]
        start, forward, finish = _gather_ops(stages, outs, *refs[3 * n + 2 * k:])
        for a in range(n):
            stages[a][...] = ins[a][...].astype(stages[a].dtype)
        start()
        for a in range(k):
            casts_out[a][...] = casts_in[a][...].astype(BF16)
        forward()
        finish()

    vmem = pl.BlockSpec(memory_space=pltpu.VMEM)
    out = pl.pallas_call(
        body, name="gather_first_weights",
        out_shape=[jax.ShapeDtypeStruct((N_DEV, *s.shape), d) for s, d in zip(gathered, dtypes)]
        + [jax.ShapeDtypeStruct(s.shape, BF16) for s in cast_only],
        in_specs=[vmem] * (n + k), out_specs=[ANY] * n + [vmem] * k,
        scratch_shapes=[pltpu.VMEM(s.shape, d) for s, d in zip(gathered, dtypes)] + _exchange_sems(n),
        compiler_params=pltpu.CompilerParams(vmem_limit_bytes=VMEM_LIMIT_BYTES),
    )(*gathered, *cast_only)
    return out[:n], out[n:]


def _exchange(arrays, whole, name):
    n = len(arrays)

    def body(*refs):
        start, finish = _exchange_ops(refs[:n], refs[n:2 * n], whole, *refs[2 * n:])
        start()
        finish()

    return pl.pallas_call(
        body, name=name,
        out_shape=[jax.ShapeDtypeStruct((N_DEV, *a.shape) if w else a.shape, a.dtype) for a, w in zip(arrays, whole)],
        in_specs=[ANY] * n, out_specs=[ANY] * n, scratch_shapes=_exchange_sems(n),
    )(*arrays)


def _columns_from_slabs(slabs):
    def body(*refs):
        k = len(refs) // 2
        for src, dst in zip(refs[:k], refs[k:]):
            n = src.shape[2]
            for i in range(N_DEV):
                dst[:, pl.ds(n * i, n)] = src[i]

    return pl.pallas_call(
        body, name="columns_from_slabs",
        out_shape=[jax.ShapeDtypeStruct((s.shape[1], N_DEV * s.shape[2]), s.dtype) for s in slabs],
        compiler_params=pltpu.CompilerParams(vmem_limit_bytes=VMEM_LIMIT_BYTES),
    )(*slabs)


def _window_sum(x, win, ahead):
    n = x.shape[0]
    span = 1
    while span < win:
        x = x + pltpu.roll(x, n - span if ahead else span, 0)
        span *= 2
    return x


def _conv_branch(z, ext_u, conv_ref, tm):
    c_w = z.shape[1] // 4
    b, c, v = z[:, :c_w], z[:, c_w:2 * c_w], z[:, 2 * c_w:3 * c_w]
    u = c * v
    ext_u[pl.ds(HALO, tm), :] = u
    u1 = ext_u[pl.ds(HALO - 1, tm), :]
    u2 = ext_u[pl.ds(HALO - 2, tm), :]
    yc = conv_ref[pl.ds(2, 1), :] * u + conv_ref[pl.ds(1, 1), :] * u1 + conv_ref[pl.ds(0, 1), :] * u2
    return b, c, v, u, u1, u2, yc


def _pool_branch(p, ext_p, pool_w_ref, tm):
    ext_p[pl.ds(HALO, tm), :] = p
    pooled, mixed = [], []
    for g, win in enumerate(POOL_WINDOWS):
        s = _window_sum(ext_p[:, pl.ds(POOL_GROUP * g, POOL_GROUP)], win, ahead=False)[HALO:HALO + tm, :]
        pooled.append((s * (1.0 / win) - p[:, POOL_GROUP * g:POOL_GROUP * (g + 1)]).astype(BF16))
        mixed.append(_dot(pooled[-1], pool_w_ref[g].astype(BF16)))
    return pooled, mixed


def _meta_forward(meta, g1, w_in):
    def body(meta_ref, g1_ref, w_ref, a_ref, z_ref):
        hat, _ = _rms_stats(meta_ref[...])
        a = (hat * g1_ref[...]).astype(BF16)
        a_ref[...] = a
        z_ref[...] = _dot(a, w_ref[...])

    return pl.pallas_call(
        body, name="meta_forward",
        out_shape=[jax.ShapeDtypeStruct(meta.shape, BF16), jax.ShapeDtypeStruct((N_META, w_in.shape[1]), F32)],
        compiler_params=pltpu.CompilerParams(vmem_limit_bytes=VMEM_LIMIT_BYTES),
    )(meta, g1, w_in)


def _mixer_forward(x2d, z_meta, g1, w_in, conv_w, pool_w, pool_scale, w_out, g2, n_seq, to_gather):
    t, d = x2d.shape
    zw = w_in.shape[1]
    cw = zw // 4
    s = t // n_seq
    tm = min(TM_MIX, s)
    nj = s // tm
    ng = len(to_gather)

    def body(x_ref, zm_ref, g1_ref, win_ref, conv_ref, pw_ref, ps_ref, wout_ref, g2_ref, *rest):
        shards, (h1_ref, z_ref, m_ref, pooled_ref, mixed_ref), slabs = rest[:ng], rest[ng:ng + 5], rest[ng + 5:2 * ng + 5]
        ext_u, ext_p = rest[2 * ng + 5:2 * ng + 7]
        start, forward, finish = _gather_ops(shards, slabs, *rest[2 * ng + 7:])
        pl.when((pl.program_id(0) == 0) & (pl.program_id(1) == 0))(start)

        @pl.when(pl.program_id(1) == 0)
        def _():
            zm = zm_ref[...]
            ext_u[pl.ds(0, HALO), :] = zm[:, cw:2 * cw] * zm[:, 2 * cw:3 * cw]
            ext_p[pl.ds(0, HALO), :] = zm[:, 3 * cw:]

        h0 = x_ref[...]
        hat, _ = _rms_stats(h0)
        z = _dot((hat * g1_ref[...]).astype(BF16), win_ref[...])
        z_ref[...] = z.astype(BF16)
        b, _, _, _, _, _, yc = _conv_branch(z, ext_u, conv_ref, tm)
        pooled, mixed = _pool_branch(z[:, 3 * cw:], ext_p, pw_ref, tm)
        pooled_ref[...] = jnp.concatenate(pooled, axis=1)
        mixed_ref[...] = jnp.concatenate(mixed, axis=1).astype(BF16)
        ps = ps_ref[...]
        y = [b * yc] + [mixed[g] * ps[:, POOL_GROUP * g:POOL_GROUP * (g + 1)] for g in range(len(POOL_WINDOWS))]
        m = _dot(jnp.concatenate(y, axis=1).astype(BF16), wout_ref[...])
        m_ref[...] = m
        m_hat, _ = _rms_stats(m)
        h1_ref[...] = h0 + m_hat * g2_ref[...]
        ext_u[pl.ds(0, HALO), :] = ext_u[pl.ds(tm, HALO), :]
        ext_p[pl.ds(0, HALO), :] = ext_p[pl.ds(tm, HALO), :]

        @pl.when((pl.program_id(0) == n_seq - 1) & (pl.program_id(1) == nj - 1))
        def _():
            forward()
            finish()

    row = lambda b, j: (b * nj + j, 0)
    out = pl.pallas_call(
        body, name="mixer_forward", grid=(n_seq, nj),
        in_specs=[pl.BlockSpec((tm, d), row), _const(z_meta.shape), _const(g1.shape), _resident(w_in.shape), _const(conv_w.shape),
                  _const(pool_w.shape), _const(pool_scale.shape), _resident(w_out.shape), _const(g2.shape)] + [ANY] * ng,
        out_specs=[pl.BlockSpec((tm, d), row), pl.BlockSpec((tm, zw), row), pl.BlockSpec((tm, d), row), pl.BlockSpec((tm, cw), row),
                   pl.BlockSpec((tm, cw), row)] + [ANY] * ng,
        out_shape=[jax.ShapeDtypeStruct((t, d), F32), jax.ShapeDtypeStruct((t, zw), BF16), jax.ShapeDtypeStruct((t, d), F32),
                   jax.ShapeDtypeStruct((t, cw), BF16), jax.ShapeDtypeStruct((t, cw), BF16)]
        + [jax.ShapeDtypeStruct((N_DEV, *a.shape), a.dtype) for a in to_gather],
        scratch_shapes=[pltpu.VMEM((tm + HALO, cw), F32), pltpu.VMEM((tm + HALO, cw), F32)] + _exchange_sems(ng),
        compiler_params=_params("arbitrary", "arbitrary"),
    )(x2d, z_meta, g1, w_in, conv_w, pool_w, pool_scale, w_out, g2, *to_gather)
    return out[:5], out[5:]


def _gather_and_mixer_forward(x2d, mixer_shards, ffn_shards, g1, pool_w, pool_scale, g2, n_seq):
    t, d = x2d.shape
    zs, rs, ms, cs = mixer_shards[0].shape[1], mixer_shards[1].shape[0], mixer_shards[2].shape[1], mixer_shards[3].shape[1]
    zw, cw = N_DEV * zs, N_DEV * cs
    s = t // n_seq
    tm = min(TM_MIX, s)
    nj = s // tm
    n1, n2 = len(mixer_shards), len(ffn_shards)
    dtypes = [BF16, BF16, F32, F32] + [BF16] * n2
    shards = list(mixer_shards) + list(ffn_shards)

    def body(x_ref, *rest):
        shard_refs, (g1_ref, pw_ref, ps_ref, g2_ref), rest = rest[:n1 + n2], rest[n1 + n2:n1 + n2 + 4], rest[n1 + n2 + 4:]
        (h1_ref, z_ref, m_ref, pooled_ref, mixed_ref, win_o, wout_o, meta_o, conv_o, am_o, zm_o), rest = rest[:11], rest[11:]
        slabs, rest = rest[:n1 + n2], rest[n1 + n2:]
        stages, rest = rest[:n1 + n2], rest[n1 + n2:]
        win_v, wout_v, meta_v, conv_v, ext_u, ext_p, sem = rest[:7]
        first = _gather_halves_ops(stages[:2], slabs[:2], *rest[7:10])
        small = _gather_ops(stages[2:n1], slabs[2:n1], *rest[10:13])
        later = _gather_halves_ops(stages[n1:], slabs[n1:], *rest[13:16], core_major=True)

        @pl.when((pl.program_id(0) == 0) & (pl.program_id(1) == 0))
        def _():
            for src, dst in zip(shard_refs, stages):
                dst[...] = src[...].astype(dst.dtype)
            first[0]()
            small[0]()
            later[0]()
            first[1]()
            first[2]()
            small[1]()
            first[3]()
            small[2]()
            copies = [pltpu.make_async_copy(slabs[0].at[i], win_v.at[:, pl.ds(zs * i, zs)], sem.at[i]) for i in range(N_DEV)]
            copies += [pltpu.make_async_copy(slabs[1].at[i], wout_v.at[pl.ds(rs * i, rs), :], sem.at[N_DEV + i]) for i in range(N_DEV)]
            copies += [pltpu.make_async_copy(slabs[2], meta_v, sem.at[2 * N_DEV]), pltpu.make_async_copy(slabs[3], conv_v, sem.at[2 * N_DEV + 1])]
            for cp in copies:
                cp.start()
            for cp in copies:
                cp.wait()
            copies = [pltpu.make_async_copy(win_v, win_o, sem.at[0]), pltpu.make_async_copy(wout_v, wout_o, sem.at[1])]
            for cp in copies:
                cp.start()
            for i in range(N_DEV):
                meta_o[:, pl.ds(ms * i, ms)] = meta_v[i]
                conv_o[:, pl.ds(cs * i, cs)] = conv_v[i]
            hat, _ = _rms_stats(meta_o[...])
            a = (hat * g1_ref[...]).astype(BF16)
            am_o[...] = a
            zm_o[...] = _dot(a, win_v[...])
            for cp in copies:
                cp.wait()

        pl.when(pl.program_id(0) * nj + pl.program_id(1) == (n_seq * nj) // 2)(later[1])

        @pl.when(pl.program_id(1) == 0)
        def _():
            zm = zm_o[...]
            ext_u[pl.ds(0, HALO), :] = zm[:, cw:2 * cw] * zm[:, 2 * cw:3 * cw]
            ext_p[pl.ds(0, HALO), :] = zm[:, 3 * cw:]

        h0 = x_ref[...]
        hat, _ = _rms_stats(h0)
        z = _dot((hat * g1_ref[...]).astype(BF16), win_v[...])
        z_ref[...] = z.astype(BF16)
        b, _, _, _, _, _, yc = _conv_branch(z, ext_u, conv_o, tm)
        pooled, mixed = _pool_branch(z[:, 3 * cw:], ext_p, pw_ref, tm)
        pooled_ref[...] = jnp.concatenate(pooled, axis=1)
        mixed_ref[...] = jnp.concatenate(mixed, axis=1).astype(BF16)
        ps = ps_ref[...]
        y = [b * yc] + [mixed[g] * ps[:, POOL_GROUP * g:POOL_GROUP * (g + 1)] for g in range(len(POOL_WINDOWS))]
        m = _dot(jnp.concatenate(y, axis=1).astype(BF16), wout_v[...])
        m_ref[...] = m
        m_hat, _ = _rms_stats(m)
        h1_ref[...] = h0 + m_hat * g2_ref[...]
        ext_u[pl.ds(0, HALO), :] = ext_u[pl.ds(tm, HALO), :]
        ext_p[pl.ds(0, HALO), :] = ext_p[pl.ds(tm, HALO), :]

        @pl.when((pl.program_id(0) == n_seq - 1) & (pl.program_id(1) == nj - 1))
        def _():
            later[2]()
            later[3]()

    row = lambda b, j: (b * nj + j, 0)
    vmem = pl.BlockSpec(memory_space=pltpu.VMEM)
    small = [(N_META, d), (CONV_WIDTH, cw), (N_META, d), (N_META, zw)]
    out = pl.pallas_call(
        body, name="gather_and_mixer_forward", grid=(n_seq, nj),
        in_specs=[pl.BlockSpec((tm, d), row)] + [vmem] * (n1 + n2)
        + [_const(g1.shape), _const(pool_w.shape), _const(pool_scale.shape), _const(g2.shape)],
        out_specs=[pl.BlockSpec((tm, d), row), pl.BlockSpec((tm, zw), row), pl.BlockSpec((tm, d), row), pl.BlockSpec((tm, cw), row),
                   pl.BlockSpec((tm, cw), row), ANY, ANY] + [_const(sh) for sh in small] + [ANY] * (n1 + n2),
        out_shape=[jax.ShapeDtypeStruct((t, d), F32), jax.ShapeDtypeStruct((t, zw), BF16), jax.ShapeDtypeStruct((t, d), F32),
                   jax.ShapeDtypeStruct((t, cw), BF16), jax.ShapeDtypeStruct((t, cw), BF16),
                   jax.ShapeDtypeStruct((d, zw), BF16), jax.ShapeDtypeStruct((d, d), BF16),
                   jax.ShapeDtypeStruct(small[0], F32), jax.ShapeDtypeStruct(small[1], F32), jax.ShapeDtypeStruct(small[2], BF16),
                   jax.ShapeDtypeStruct(small[3], F32)]
        + [jax.ShapeDtypeStruct((N_DEV, *a.shape), dt) for a, dt in zip(shards, dtypes)],
        scratch_shapes=[pltpu.VMEM(a.shape, dt) for a, dt in zip(shards, dtypes)]
        + [pltpu.VMEM((d, zw), BF16), pltpu.VMEM((d, d), BF16), pltpu.VMEM((N_DEV, N_META, ms), F32),
           pltpu.VMEM((N_DEV, CONV_WIDTH, cs), F32), pltpu.VMEM((tm + HALO, cw), F32), pltpu.VMEM((tm + HALO, cw), F32),
           pltpu.SemaphoreType.DMA((2 * N_DEV + 2,))] + _gather_halves_sems(2) + _exchange_sems(n1 - 2) + _gather_halves_sems(n2),
        compiler_params=_params("arbitrary", "arbitrary"),
    )(x2d, *shards, g1, pool_w, pool_scale, g2)
    return out[:5], out[5:11], out[11 + n1:]


def _mixer_backward(x2d, dh1, m, z, pooled, mixed, meta, a_meta, z_meta, g1, w_in, conv_w, pool_w, pool_scale, w_out, g2, n_seq,
                    to_exchange, landing):
    t, d = x2d.shape
    zw = w_in.shape[1]
    cw = zw // 4
    s = t // n_seq
    tm = min(TM_MIX, s)
    nj = s // tm
    n_groups = len(POOL_WINDOWS)
    zs = zw // N_DEV
    nx = len(to_exchange)
    n_in = 17
    given = [k for k, a in enumerate(landing) if a is not None]
    fresh = [k for k, a in enumerate(landing) if a is None]

    def body(x_ref, dh1_ref, m_ref, z_ref, zprev_ref, pooled_ref, mixed_ref, meta_ref, am_ref, zm_ref, g1_ref, win_ref, conv_ref, pw_ref, ps_ref, wout_ref,
             g2_ref, *rest):
        sent, rest = rest[:nx], rest[nx + len(given):]
        gx_ref, dwin_ref, dwout_ref, dg1_ref, dg2_ref, dconv_ref, dpw_ref, dps_ref, dmeta_ref = rest[:9]
        landed, rest = rest[9:9 + nx], rest[9 + nx:]
        ext_u, ext_dyc, ext_dq, acc_win, acc_wout, stage16, sem = rest[:7]
        north = _core_exchange_ops(sent, landed, 1, *rest[7:10])
        south = _core_exchange_ops([sent[k] for k in fresh], [landed[k] for k in fresh], 0, *rest[10:13])

        def start():
            north[0]()
            south[0]()

        def finish():
            south[1]()
            north[1]()

        b_id, j = pl.program_id(0), pl.program_id(1)
        jr = nj - 1 - j
        pl.when((b_id == 0) & (j == 0))(start)

        @pl.when((b_id == 0) & (j == 0))
        def _():
            acc_win[...] = jnp.zeros_like(acc_win)
            acc_wout[...] = jnp.zeros_like(acc_wout)
            for r in (dg1_ref, dg2_ref, dconv_ref, dpw_ref, dps_ref, dmeta_ref):
                r[...] = jnp.zeros_like(r)

        @pl.when(j == 0)
        def _():
            ext_dyc[pl.ds(tm, HALO), :] = jnp.zeros((HALO, cw), F32)
            ext_dq[pl.ds(tm, HALO), :] = jnp.zeros((HALO, cw), F32)

        zm = zm_ref[...]
        halo = jnp.where(jr == 0, zm, zprev_ref[...].astype(F32))
        ext_u[pl.ds(0, HALO), :] = halo[:, cw:2 * cw] * halo[:, 2 * cw:3 * cw]

        h0 = x_ref[...]
        hat0, rstd0 = _rms_stats(h0)
        g1 = g1_ref[...]
        a = (hat0 * g1).astype(BF16)
        b, c, v, u, u1, u2, yc = _conv_branch(z_ref[...].astype(F32), ext_u, conv_ref, tm)
        mixed = [mixed_ref[:, pl.ds(POOL_GROUP * g, POOL_GROUP)].astype(F32) for g in range(n_groups)]
        ps = ps_ref[...]
        y = [b * yc] + [mixed[g] * ps[:, POOL_GROUP * g:POOL_GROUP * (g + 1)] for g in range(n_groups)]
        ycat = jnp.concatenate(y, axis=1).astype(BF16)

        dh1v = dh1_ref[...]
        m_hat, m_rstd = _rms_stats(m_ref[...])
        dm, dg2 = _rms_bwd(m_hat, m_rstd, g2_ref[...], dh1v)
        dg2_ref[...] += dg2
        dm = dm.astype(BF16)
        acc_wout[...] += _dot_tn(ycat, dm)
        dycat = _dot_nt(dm, wout_ref[...])

        dyconv = dycat[:, :cw]
        db = dyconv * yc
        dyc = dyconv * b
        ext_dyc[pl.ds(0, tm), :] = dyc
        du = (conv_ref[pl.ds(2, 1), :] * dyc + conv_ref[pl.ds(1, 1), :] * ext_dyc[pl.ds(1, tm), :]
              + conv_ref[pl.ds(0, 1), :] * ext_dyc[pl.ds(2, tm), :])
        dconv_ref[pl.ds(2, 1), :] += jnp.sum(dyc * u, axis=0, keepdims=True)
        dconv_ref[pl.ds(1, 1), :] += jnp.sum(dyc * u1, axis=0, keepdims=True)
        dconv_ref[pl.ds(0, 1), :] += jnp.sum(dyc * u2, axis=0, keepdims=True)

        dp = []
        for g, win in enumerate(POOL_WINDOWS):
            lanes = pl.ds(POOL_GROUP * g, POOL_GROUP)
            dypool = dycat[:, cw + POOL_GROUP * g:cw + POOL_GROUP * (g + 1)]
            dps_ref[:, lanes] += jnp.sum(dypool * mixed[g], axis=0, keepdims=True)
            dmixed = (dypool * ps[:, POOL_GROUP * g:POOL_GROUP * (g + 1)]).astype(BF16)
            dpw_ref[g] += _dot_tn(pooled_ref[:, lanes], dmixed)
            dq = _dot_nt(dmixed, pw_ref[g].astype(BF16))
            ext_dq[pl.ds(0, tm), lanes] = dq
            acc = _window_sum(ext_dq[:, lanes], win, ahead=True)[0:tm, :]
            dp.append(acc * (1.0 / win) - dq)

        dz = jnp.concatenate([db, du * v, du * c] + dp, axis=1).astype(BF16)
        acc_win[...] += _dot_tn(a, dz)
        dh0, dg1 = _rms_bwd(hat0, rstd0, g1, _dot_nt(dz, win_ref[...]))
        dg1_ref[...] += dg1
        gx_ref[...] = dh1v + dh0

        ext_dyc[pl.ds(tm, HALO), :] = ext_dyc[pl.ds(0, HALO), :]
        ext_dq[pl.ds(tm, HALO), :] = ext_dq[pl.ds(0, HALO), :]

        @pl.when(jr == 0)
        def _():
            ext_dyc[pl.ds(tm - HALO, HALO), :] = jnp.zeros((HALO, cw), F32)
            ext_dq[pl.ds(tm - HALO, HALO), :] = jnp.zeros((HALO, cw), F32)
            du_m = (conv_ref[pl.ds(1, 1), :] * ext_dyc[pl.ds(tm - HALO + 1, HALO), :]
                    + conv_ref[pl.ds(0, 1), :] * ext_dyc[pl.ds(tm - HALO + 2, HALO), :])
            dp_m = []
            for g, win in enumerate(POOL_WINDOWS):
                lanes = pl.ds(POOL_GROUP * g, POOL_GROUP)
                acc = ext_dq[pl.ds(tm - HALO + 1, HALO), lanes]
                for k in range(2, win):
                    acc = acc + ext_dq[pl.ds(tm - HALO + k, HALO), lanes]
                dp_m.append(acc * (1.0 / win))
            dz_m = jnp.concatenate([jnp.zeros((HALO, cw), F32), du_m * zm[:, 2 * cw:3 * cw], du_m * zm[:, cw:2 * cw]] + dp_m,
                                   axis=1).astype(BF16)
            acc_win[...] += _dot_tn(am_ref[...], dz_m)
            hat_m, rstd_m = _rms_stats(meta_ref[...])
            dmeta, dg1_m = _rms_bwd(hat_m, rstd_m, g1, _dot_nt(dz_m, win_ref[...]))
            dg1_ref[...] += dg1_m
            dmeta_ref[...] += dmeta

        @pl.when((b_id == n_seq - 1) & (j == nj - 1))
        def _():
            pieces = [(acc_win, zs * i, dwin_ref.at[i]) for i in range(N_DEV)]
            pieces += [(acc_wout, zs * i, dwout_ref.at[:, pl.ds(zs * i, zs)]) for i in range(d // zs)]
            copies = []
            for k, (acc, col, dst) in enumerate(pieces):
                if k >= 2:
                    copies[k - 2].wait()
                stage16[k % 2] = acc[:, pl.ds(col, zs)].astype(BF16)
                copies.append(pltpu.make_async_copy(stage16.at[k % 2], dst, sem.at[k % 2]))
                copies[k].start()
            copies[-2].wait()
            copies[-1].wait()
            finish()

    row = lambda b, j: (b * nj + nj - 1 - j, 0)
    prev = lambda b, j: (jnp.maximum((b * s + (nj - 1 - j) * tm) // HALO - 1, 0), 0)
    small = [g1.shape, g2.shape, conv_w.shape, pool_w.shape, pool_scale.shape, meta.shape]
    out = pl.pallas_call(
        body, name="mixer_backward", grid=(n_seq, nj),
        in_specs=[pl.BlockSpec((tm, d), row), pl.BlockSpec((tm, d), row), pl.BlockSpec((tm, d), row), pl.BlockSpec((tm, zw), row),
                  pl.BlockSpec((HALO, zw), prev), pl.BlockSpec((tm, cw), row), pl.BlockSpec((tm, cw), row), _const(meta.shape), _const(a_meta.shape), _const(z_meta.shape), _const(g1.shape),
                  _resident(w_in.shape), _const(conv_w.shape), _const(pool_w.shape), _const(pool_scale.shape), _resident(w_out.shape),
                  _const(g2.shape)] + [ANY] * (nx + len(given)),
        out_specs=[pl.BlockSpec((tm, d), row), ANY, ANY] + [_const(sh) for sh in small] + [ANY] * nx,
        out_shape=[jax.ShapeDtypeStruct((t, d), F32), jax.ShapeDtypeStruct((N_DEV, d, zs), BF16),
                   jax.ShapeDtypeStruct(w_out.shape, BF16)] + [jax.ShapeDtypeStruct(sh, F32) for sh in small]
        + [jax.ShapeDtypeStruct((N_DEV, a.shape[0] // N_DEV, a.shape[1]), a.dtype) for a in to_exchange],
        input_output_aliases={n_in + nx + at: 9 + k for at, k in enumerate(given)},
        scratch_shapes=[pltpu.VMEM((tm + HALO, cw), F32)] * 3
        + [pltpu.VMEM(w_in.shape, F32), pltpu.VMEM(w_out.shape, F32), pltpu.VMEM((2, d, zs), BF16),
           pltpu.SemaphoreType.DMA((2,))] + _core_exchange_sems(nx) + _core_exchange_sems(len(fresh)),
        compiler_params=_params("arbitrary", "arbitrary"),
    )(x2d, dh1, m, z, z, pooled, mixed, meta, a_meta, z_meta, g1, w_in, conv_w, pool_w, pool_scale, w_out, g2, *to_exchange, *[landing[k] for k in given])
    return out[:9], out[9:]


def _ffn_forward_backward(h1, target, g3, w_gate, w_up, w_down, g4):
    t, d = h1.shape
    ff = w_gate.shape[0]
    tm = min(TM_FFN, t)
    nt = t // tm
    chunks = [(s, min(FFN_CHUNK, ff - s)) for s in range(0, ff, FFN_CHUNK)]

    def body(h1_ref, h1pp_ref, tgt_ref, g3_ref, wg_ref, wu_ref, wd_ref, g4_ref,
             f_ref, act_ref, dd_ref, dgate_ref, dup_ref, dh1_ref, loss_ref, dg3_ref, dg4_ref, *slots):
        gate_s, up_s, dd_s, dh2_s, df_s = slots
        i = pl.program_id(0)

        def forward(slot):
            h1v = h1_ref[...]
            hat, _ = _rms_stats(h1v)
            f = (hat * g3_ref[...]).astype(BF16)
            f_ref[...] = f
            s, n = chunks[0]
            gate, up = _dot_nt(f_ref[...], wg_ref[pl.ds(s, n), :]), _dot_nt(f_ref[...], wu_ref[pl.ds(s, n), :])
            yield
            down = None
            for k, (s, n) in enumerate(chunks):
                gate_s.at[slot][:, pl.ds(s, n)] = gate.astype(BF16)
                up_s.at[slot][:, pl.ds(s, n)] = up.astype(BF16)
                act = (gate * jax.nn.sigmoid(gate) * up).astype(BF16)
                act_ref[:, pl.ds(s, n)] = act
                if k + 1 < len(chunks):
                    s1, n1 = chunks[k + 1]
                    gate, up = _dot_nt(f_ref[...], wg_ref[pl.ds(s1, n1), :]), _dot_nt(f_ref[...], wu_ref[pl.ds(s1, n1), :])
                yield
                part = _dot(act_ref[:, pl.ds(s, n)], wd_ref[pl.ds(s, n), :])
                down = part if down is None else down + part
                yield
            d_hat, d_rstd = _rms_stats(down)
            g4 = g4_ref[...]
            err = h1v + d_hat * g4 - tgt_ref[...]
            loss_ref[...] += jnp.sum(err * err) * (0.5 / d)
            dh2 = err * (1.0 / d)
            dh2_s.at[slot][...] = dh2
            dd, dg4 = _rms_bwd(d_hat, d_rstd, g4, dh2)
            dg4_ref[...] += dg4
            dd = dd.astype(BF16)
            dd_ref[...] = dd
            dd_s.at[slot][...] = dd

        def backward(slot):
            s, n = chunks[0]
            dact = _dot_nt(dd_s.at[slot][...], wd_ref[pl.ds(s, n), :])
            yield
            df = None
            for k, (s, n) in enumerate(chunks):
                gate = gate_s.at[slot][:, pl.ds(s, n)].astype(F32)
                up = up_s.at[slot][:, pl.ds(s, n)].astype(F32)
                sig = jax.nn.sigmoid(gate)
                dup = (dact * (gate * sig)).astype(BF16)
                dgate = (dact * up * (sig * (1.0 + gate * (1.0 - sig)))).astype(BF16)
                dup_ref[:, pl.ds(s, n)] = dup
                dgate_ref[:, pl.ds(s, n)] = dgate
                if k + 1 < len(chunks):
                    s1, n1 = chunks[k + 1]
                    dact = _dot_nt(dd_s.at[slot][...], wd_ref[pl.ds(s1, n1), :])
                yield
                part = _dot(dgate_ref[:, pl.ds(s, n)], wg_ref[pl.ds(s, n), :]) + _dot(dup_ref[:, pl.ds(s, n)], wu_ref[pl.ds(s, n), :])
                df = part if df is None else df + part
                yield
            df_s.at[slot][...] = df

        def last(slot):
            hat, rstd = _rms_stats(h1pp_ref[...])
            dh1, dg3 = _rms_bwd(hat, rstd, g3_ref[...], df_s.at[slot][...])
            dg3_ref[...] += dg3
            dh1_ref[...] = dh2_s.at[slot][...] + dh1

        def emit(parity, with_forward, with_backward, with_last):
            fwd = forward(parity) if with_forward else iter(())
            bwd = backward(1 - parity) if with_backward else iter(())
            next(fwd, None)
            if with_last:
                last(parity)
            for _ in range(FFN_BACKWARD_LAG):
                next(fwd, None)
            alive = True
            while alive:
                alive = next(bwd, True) is None
                alive = (next(fwd, True) is None) or alive

        @pl.when(i == 0)
        def _():
            for r in (loss_ref, dg3_ref, dg4_ref, *slots):
                r[...] = jnp.zeros_like(r)

        @pl.when(i < nt)
        def _():
            emit(i % 2, True, True, True)

        @pl.when(i == nt)
        def _():
            emit(nt % 2, False, True, True)

        @pl.when(i == nt + 1)
        def _():
            emit((nt + 1) % 2, False, False, True)

    cur = lambda i: (jnp.minimum(i, nt - 1), 0)
    prev = lambda i: (jnp.clip(i - 1, 0, nt - 1), 0)
    prev2 = lambda i: (jnp.clip(i - 2, 0, nt - 1), 0)
    return pl.pallas_call(
        body, name="ffn_forward_backward", grid=(nt + 2,),
        in_specs=[pl.BlockSpec((tm, d), cur), pl.BlockSpec((tm, d), prev2), pl.BlockSpec((tm, d), cur), _const(g3.shape),
                  _resident(w_gate.shape), _resident(w_up.shape), _resident(w_down.shape), _const(g4.shape)],
        out_specs=[pl.BlockSpec((tm, d), cur), pl.BlockSpec((tm, ff), cur), pl.BlockSpec((tm, d), cur), pl.BlockSpec((tm, ff), prev),
                   pl.BlockSpec((tm, ff), prev), pl.BlockSpec((tm, d), prev2), _const((8, 128)), _const(g3.shape), _const(g4.shape)],
        out_shape=[jax.ShapeDtypeStruct((t, d), BF16), jax.ShapeDtypeStruct((t, ff), BF16), jax.ShapeDtypeStruct((t, d), BF16),
                   jax.ShapeDtypeStruct((t, ff), BF16), jax.ShapeDtypeStruct((t, ff), BF16), jax.ShapeDtypeStruct((t, d), F32),
                   jax.ShapeDtypeStruct((8, 128), F32), jax.ShapeDtypeStruct(g3.shape, F32), jax.ShapeDtypeStruct(g4.shape, F32)],
        scratch_shapes=[pltpu.VMEM((2, tm, ff), BF16)] * 2 + [pltpu.VMEM((2, tm, d), BF16)] + [pltpu.VMEM((2, tm, d), F32)] * 2,
        compiler_params=_params("arbitrary"),
    )(h1, h1, target, g3, w_gate, w_up, w_down, g4)


def _ffn_weight_grads(f, dd, dgate, dup, act):
    t, d = f.shape
    ff = dgate.shape[1]
    tm = min(TM_WGRAD, t)
    nt = t // tm
    fc = ff // FF_CHUNKS
    assert FF_CHUNKS == 2

    def body(f_ref, dd_ref, dgate_ref, dup_ref, act_ref, dwg_ref, dwu_ref, dwd_ref, *rest):
        landing, (acc_g, acc_u, acc_d, stage, sem) = rest[:2], rest[2:7]
        start, finish = _core_exchange_ops([dwg_ref, dwd_ref], landing, 0, *rest[7:])
        c, i = pl.program_id(0), pl.program_id(1)
        pl.when((c == 1) & (i == 0))(start)

        @pl.when(i == 0)
        def _():
            acc_g[...] = jnp.zeros_like(acc_g)
            acc_u[...] = jnp.zeros_like(acc_u)
            acc_d[...] = jnp.zeros_like(acc_d)

        fv = f_ref[...]
        acc_g[...] += _dot_tn(fv, dgate_ref[...])
        acc_u[...] += _dot_tn(fv, dup_ref[...])
        acc_d[...] += _dot_tn(act_ref[...], dd_ref[...])

        @pl.when(i == nt - 1)
        def _():
            rows = pl.ds(pl.multiple_of(c * fc, 16), fc)
            copies = []
            for k, (acc, out, transposed) in enumerate(((acc_d, dwd_ref, False), (acc_g, dwg_ref, True), (acc_u, dwu_ref, True))):
                if k >= 2:
                    copies[k - 2].wait()
                stage[k % 2] = (acc[...].T if transposed else acc[...]).astype(BF16)
                copies.append(pltpu.make_async_copy(stage.at[k % 2], out.at[rows, :], sem.at[k % 2]))
                copies[k].start()
            copies[-2].wait()
            copies[-1].wait()

        pl.when((c == 1) & (i == nt - 1))(finish)

    row = lambda c, i: (i, 0)
    col = lambda c, i: (i, c)
    out = pl.pallas_call(
        body, name="ffn_weight_grads", grid=(FF_CHUNKS, nt),
        in_specs=[pl.BlockSpec((tm, d), row), pl.BlockSpec((tm, d), row), pl.BlockSpec((tm, fc), col), pl.BlockSpec((tm, fc), col),
                  pl.BlockSpec((tm, fc), col)],
        out_specs=[ANY] * 5,
        out_shape=[jax.ShapeDtypeStruct((ff, d), BF16)] * 3 + [jax.ShapeDtypeStruct((N_DEV, ff // N_DEV, d), BF16)] * 2,
        scratch_shapes=[pltpu.VMEM((d, fc), F32), pltpu.VMEM((d, fc), F32), pltpu.VMEM((fc, d), F32), pltpu.VMEM((2, fc, d), BF16),
                        pltpu.SemaphoreType.DMA((2,))] + _core_exchange_sems(2),
        compiler_params=_params("arbitrary", "arbitrary"),
    )(f, dd, dgate, dup, act)
    return out[:3], [out[3], None, out[4]]


def _adamw(w, g, m, v):
    m = ADAM_B1 * m + (1.0 - ADAM_B1) * g
    v = ADAM_B2 * v + (1.0 - ADAM_B2) * (g * g)
    m_hat = m / (1.0 - ADAM_B1 ** ADAM_STEP)
    v_hat = v / (1.0 - ADAM_B2 ** ADAM_STEP)
    return -ADAM_LR * (m_hat / (jnp.sqrt(v_hat) + ADAM_EPS) + ADAM_WD * w), m, v


def _sum_slabs(ref):
    total = ref[0].astype(F32)
    for i in range(1, ref.shape[0]):
        total = total + ref[i].astype(F32)
    return total


def _adamw_rows(r, c):
    tr = r
    for cand in range(8, r, 8):
        if r % cand == 0 and cand * c <= ADAMW_BLOCK_ELEMS:
            tr = cand
    return r if r * c <= ADAMW_BLOCK_ELEMS else tr


def _reduce_adamw_carrying(parts, ws, ms, vs, to_reduce, to_exchange, whole, name):
    k, nr, nx = len(ws), len(to_reduce), len(to_exchange)
    r, c = ws[0].shape if k else (8, 128)
    tr = _adamw_rows(r, c)
    steps = r // tr
    travels = nr + nx > 0
    chip_slabs = [jax.ShapeDtypeStruct((N_CHIP, *a.shape[1:]), a.dtype) for a in to_reduce]

    def body(*refs):
        p_refs, w_refs, m_refs, v_refs = (refs[a * k:(a + 1) * k] for a in range(4))
        refs = refs[4 * k:]
        reduced_in, sent, refs = refs[:nr], refs[nr:nr + nx], refs[nr + nx:]
        outs, pairs, sums, landed, refs = refs[:4 * k], refs[4 * k:4 * k + nr], refs[4 * k + nr:4 * k + 2 * nr], \
            refs[4 * k + 2 * nr:4 * k + 2 * nr + nx], refs[4 * k + 2 * nr + nx:]
        mine_v, pair_v, sum_v, refs = refs[:nr], refs[nr:2 * nr], refs[2 * nr:3 * nr], refs[3 * nr:]
        if travels:
            reduce_ops = _pair_then_chip_ops(reduced_in, pairs, sums, mine_v, pair_v, sum_v, *refs[:7])
            direct_ops = _exchange_ops(sent, landed, whole, *refs[7:])

            @pl.when(pl.program_id(0) == 0)
            def _():
                direct_ops[0]()
                reduce_ops[0]()

        for a in range(k):
            g = _sum_slabs(p_refs[a])
            outs[4 * a][...] = g
            outs[4 * a + 1][...], outs[4 * a + 2][...], outs[4 * a + 3][...] = _adamw(w_refs[a][...], g, m_refs[a][...], v_refs[a][...])

        if travels:
            @pl.when(pl.program_id(0) == steps - 1)
            def _():
                reduce_ops[1]()
                direct_ops[1]()

    blk = pl.BlockSpec((tr, c), lambda i: (i, 0))
    out = pl.pallas_call(
        body, name=name, grid=(steps,),
        in_specs=[pl.BlockSpec((N_DEV, tr, c), lambda i: (0, i, 0))] * k + [blk] * (3 * k) + [ANY] * (nr + nx),
        out_specs=[blk] * (4 * k) + [ANY] * (2 * nr + nx),
        out_shape=[jax.ShapeDtypeStruct((r, c), F32)] * (4 * k) + chip_slabs + chip_slabs
        + [jax.ShapeDtypeStruct((N_DEV, *a.shape) if w else a.shape, a.dtype) for a, w in zip(to_exchange, whole)],
        scratch_shapes=([pltpu.VMEM(a.shape, a.dtype) for a in chip_slabs] * 3 + _pair_then_chip_sems(nr) + _exchange_sems(nx)
                        if travels else []),
        compiler_params=_params("arbitrary"),
    )(*parts, *ws, *ms, *vs, *to_reduce, *to_exchange)
    return [tuple(out[4 * a:4 * a + 4]) for a in range(k)], out[4 * k + nr:4 * k + 2 * nr], out[4 * k + 2 * nr:]


def _reduce_adamw(parts, w, m, v, name):
    r, c = w.shape
    tr = _adamw_rows(r, c)

    def body(p_ref, w_ref, m_ref, v_ref, g_out, d_out, m_out, v_out):
        g = _sum_slabs(p_ref)
        g_out[...] = g
        d_out[...], m_out[...], v_out[...] = _adamw(w_ref[...], g, m_ref[...], v_ref[...])

    blk = pl.BlockSpec((tr, c), lambda i: (i, 0))
    return pl.pallas_call(
        body, name=name, grid=(r // tr,),
        in_specs=[pl.BlockSpec((parts.shape[0], tr, c), lambda i: (0, i, 0)), blk, blk, blk],
        out_specs=[blk] * 4, out_shape=[jax.ShapeDtypeStruct((r, c), F32)] * 4,
        compiler_params=_params("arbitrary"),
    )(parts, w, m, v)


def _reduce_adamw_small(parts, ws, ms, vs, loss_parts):
    n = len(parts)

    def body(*refs):
        p_refs, w_refs, m_refs, v_refs = (refs[k * n:(k + 1) * n] for k in range(4))
        outs = refs[4 * n + 1:]
        outs[4 * n][...] = _sum_slabs(refs[4 * n])
        for a in range(n):
            g = _sum_slabs(p_refs[a])
            outs[4 * a][...] = g
            outs[4 * a + 1][...], outs[4 * a + 2][...], outs[4 * a + 3][...] = _adamw(w_refs[a][...], g, m_refs[a][...], v_refs[a][...])

    out = pl.pallas_call(
        body, name="adamw_replicated",
        out_shape=[jax.ShapeDtypeStruct(w.shape, F32) for w in ws for _ in range(4)] + [jax.ShapeDtypeStruct(loss_parts.shape[1:], F32)],
        compiler_params=pltpu.CompilerParams(vmem_limit_bytes=VMEM_LIMIT_BYTES),
    )(*parts, *ws, *ms, *vs, loss_parts)
    return [tuple(out[4 * a:4 * a + 4]) for a in range(n)], out[4 * n]


def kernel(x, meta_tokens, norm_mix_pre, w_in, conv_w, pool_w, pool_scale, w_out, norm_mix_post, norm_ffn_pre, w_gate, w_up, w_down, norm_ffn_post, loss_target, m_meta_tokens, m_norm_mix_pre, m_w_in, m_conv_w, m_pool_w, m_pool_scale, m_w_out, m_norm_mix_post, m_norm_ffn_pre, m_w_gate, m_w_up, m_w_down, m_norm_ffn_post, v_meta_tokens, v_norm_mix_pre, v_w_in, v_conv_w, v_pool_w, v_pool_scale, v_w_out, v_norm_mix_post, v_norm_ffn_pre, v_w_gate, v_w_up, v_w_down, v_norm_ffn_post):
    n_seq, seq, d = x.shape
    x2d = x.reshape(n_seq * seq, d)
    target = loss_target.reshape(n_seq * seq, d)

    t_ = lambda a: jnp.swapaxes(a[0], 0, 1)
    pw, ps = pool_w[0], pool_scale

    (h1, z, m, pooled, mixed), (win_b, wout_b, meta, conv, a_meta, z_meta), ffn_slabs = _gather_and_mixer_forward(
        x2d, [w_in[0], w_out[0], meta_tokens, conv_w[0]], [t_(w_gate), t_(w_up), w_down[0]], norm_mix_pre, pw, ps, norm_mix_post, n_seq)
    wg_b, wu_b, wd_b = (s.reshape(-1, d) for s in ffn_slabs)
    f, act, dd, dgate, dup, dh1, loss_sum, dg3, dg4 = _ffn_forward_backward(h1, target, norm_ffn_pre, wg_b, wu_b, wd_b, norm_ffn_post)
    ffn_grads, landing = _ffn_weight_grads(f, dd, dgate, dup, act)
    (gx, dwin, dwout, dg1, dg2, dconv, dpw, dps, dmeta), ffn_parts = _mixer_backward(
        x2d, dh1, m, z, pooled, mixed, meta, a_meta, z_meta, norm_mix_pre, win_b, conv, pw, ps, wout_b, norm_mix_post, n_seq,
        ffn_grads, landing)

    dmeta_s = jnp.transpose(dmeta.reshape(N_META, N_DEV, -1), (1, 0, 2))
    dconv_s = jnp.transpose(dconv.reshape(CONV_WIDTH, N_DEV, -1), (1, 0, 2))
    _, (win_parts, wout_parts), last = _reduce_adamw_carrying(
        [], [], [], [], [dwin, dwout.reshape(N_DEV, -1, d)], [dmeta_s, dconv_s, dg1, dg2, dg3, dg4, dpw, dps, loss_sum],
        [False] * 2 + [True] * 7, "exchange_rest")
    ffn_res, _, _ = _reduce_adamw_carrying(
        ffn_parts, [t_(w_gate), t_(w_up), w_down[0]], [t_(m_w_gate), t_(m_w_up), m_w_down[0]], [t_(v_w_gate), t_(v_w_up), v_w_down[0]],
        [], [], [], "adamw_ffn")
    replicated = last[2:8]

    names = ["meta_tokens", "norm_mix_pre", "w_in", "conv_w", "pool_w", "pool_scale", "w_out", "norm_mix_post", "norm_ffn_pre", "w_gate",
             "w_up", "w_down", "norm_ffn_post"]
    res = {"w_gate": tuple(jnp.swapaxes(o, 0, 1)[None] for o in ffn_res[0]),
           "w_up": tuple(jnp.swapaxes(o, 0, 1)[None] for o in ffn_res[1]), "w_down": tuple(o[None] for o in ffn_res[2])}
    for nm, parts, w, m_, v_ in (("w_in", win_parts, w_in, m_w_in, v_w_in), ("w_out", wout_parts, w_out, m_w_out, v_w_out),
                                 ("conv_w", last[1], conv_w, m_conv_w, v_conv_w)):
        res[nm] = tuple(o[None] for o in _reduce_adamw(parts, w[0], m_[0], v_[0], "adamw_" + nm))
    res["meta_tokens"] = tuple(_reduce_adamw(last[0], meta_tokens, m_meta_tokens, v_meta_tokens, "adamw_meta_tokens"))
    small, loss = _reduce_adamw_small(
        replicated, [norm_mix_pre, norm_mix_post, norm_ffn_pre, norm_ffn_post, pool_w[0], pool_scale],
        [m_norm_mix_pre, m_norm_mix_post, m_norm_ffn_pre, m_norm_ffn_post, m_pool_w[0], m_pool_scale],
        [v_norm_mix_pre, v_norm_mix_post, v_norm_ffn_pre, v_norm_ffn_post, v_pool_w[0], v_pool_scale], last[8])
    for nm, r in zip(["norm_mix_pre", "norm_mix_post", "norm_ffn_pre", "norm_ffn_post", "pool_w", "pool_scale"], small):
        res[nm] = tuple(o[None] for o in r) if nm == "pool_w" else r

    return (loss[0, 0], gx.reshape(n_seq, seq, d), *[res[nm][0] for nm in names], *[res[nm][1] for nm in names],
            *[res[nm][2] for nm in names], *[res[nm][3] for nm in names])
```

```python
import jax
import jax.numpy as jnp
from jax import lax
from jax.experimental import pallas as pl
from jax.experimental.pallas import tpu as pltpu

F32, BF16 = jnp.float32, jnp.bfloat16
RMS_EPS = 1e-6
N_META = 16
CONV_WIDTH = 3
POOL_WINDOWS = (2, 4, 8, 16)
POOL_GROUP = 128
HALO = 16
N_DEV = 8
MESH_AXES = ("x", "y", "c")
MESH = pl.DeviceIdType.MESH
VMEM_LIMIT_BYTES = 56 * 1024 * 1024
ADAMW_BLOCK_ELEMS = 64 * 1024
TM_MIX = 512
TM_FFN = 256
FFN_CHUNK = 512
FFN_BACKWARD_LAG = 2
TM_WGRAD = 512
FF_CHUNKS = 2

ADAM_LR, ADAM_B1, ADAM_B2, ADAM_EPS, ADAM_WD, ADAM_STEP = 0.001, 0.9, 0.999, 1e-08, 0.01, 10


def _dot(a, b):
    return jnp.dot(a, b, preferred_element_type=F32)


def _dot_nt(a, b):
    return lax.dot_general(a, b, (((1,), (1,)), ((), ())), preferred_element_type=F32)


def _dot_tn(a, b):
    return lax.dot_general(a, b, (((0,), (0,)), ((), ())), preferred_element_type=F32)


def _rms_stats(h):
    rstd = lax.rsqrt(jnp.mean(h * h, axis=-1, keepdims=True) + RMS_EPS)
    return h * rstd, rstd


def _rms_bwd(hat, rstd, g, dy):
    gdy = dy * g
    proj = jnp.mean(gdy * hat, axis=-1, keepdims=True)
    return rstd * (gdy - hat * proj), jnp.sum(dy * hat, axis=0, keepdims=True)


def _params(*semantics):
    return pltpu.CompilerParams(dimension_semantics=semantics or None, vmem_limit_bytes=VMEM_LIMIT_BYTES)


def _resident(shape):
    zeros = (0,) * len(shape)
    return pl.BlockSpec(shape, lambda *_: zeros, pipeline_mode=pl.Buffered(1))


def _const(shape):
    zeros = (0,) * len(shape)
    return pl.BlockSpec(shape, lambda *_: zeros)


ANY = pl.BlockSpec(memory_space=pl.ANY)


def _my_place():
    x, y, c = (lax.axis_index(a) for a in MESH_AXES)
    return x, y, c


def _exchange_sems(n):
    return [pltpu.SemaphoreType.DMA((n, N_DEV - 1)), pltpu.SemaphoreType.DMA((n, N_DEV - 1)), pltpu.SemaphoreType.DMA((n,))]


def _gather_ops(srcs, outs, send_sems, recv_sems, local_sems, core_major=False):
    n = len(srcs)
    x, y, c = _my_place()
    me, sibling = (x, y, c), (x, y, 1 - c)
    chips = [(1 - x, y), (x, 1 - y), (1 - x, 1 - y)]

    def slab(px, py, pc):
        return 4 * pc + 2 * px + py if core_major else 4 * px + 2 * py + pc

    def copy(a, k, block, to, src=None):
        dst = outs[a].at[slab(*block)]
        return pltpu.make_async_remote_copy(
            src_ref=dst if src is None else src, dst_ref=dst, send_sem=send_sems.at[a, k], recv_sem=recv_sems.at[a, k],
            device_id=to, device_id_type=MESH)

    def mine(a):
        return pltpu.make_async_copy(srcs[a], outs[a].at[slab(*me)], local_sems.at[a])

    def first(a):
        return [copy(a, 0, me, sibling, src=srcs[a])] + [copy(a, 1 + j, me, (*chip, c), src=srcs[a]) for j, chip in enumerate(chips)]

    def passed(a, j):
        return copy(a, 4 + j, (*chips[j], c), sibling)

    def start():
        for a in range(n):
            mine(a).start()
            for cp in first(a):
                cp.start()

    def forward():
        for j, chip in enumerate(chips):
            for a in range(n):
                copy(a, 1 + j, (*chip, c), me).wait_recv()
                passed(a, j).start()

    def finish():
        for a in range(n):
            copy(a, 0, sibling, me).wait_recv()
            for j, chip in enumerate(chips):
                copy(a, 4 + j, (*chip, 1 - c), me).wait_recv()
        for a in range(n):
            for cp in first(a) + [passed(a, j) for j in range(len(chips))]:
                cp.wait_send()
            mine(a).wait()

    return start, forward, finish


def _exchange_ops(ins, outs, whole, send_sems, recv_sems, local_sems):
    n = len(ins)
    x, y, c = _my_place()
    me = 4 * x + 2 * y + c

    def src(a, i):
        return ins[a] if whole[a] else ins[a].at[i]

    def mine(a):
        return pltpu.make_async_copy(src(a, me), outs[a].at[me], local_sems.at[a])

    def send(a, k):
        to = (me + k) % N_DEV
        return pltpu.make_async_remote_copy(
            src_ref=src(a, to), dst_ref=outs[a].at[me], send_sem=send_sems.at[a, k - 1], recv_sem=recv_sems.at[a, k - 1],
            device_id=(to // 4, (to // 2) % 2, to % 2), device_id_type=MESH)

    def landed(a, k):
        frm = (me + N_DEV - k) % N_DEV
        return pltpu.make_async_remote_copy(
            src_ref=src(a, frm), dst_ref=outs[a].at[frm], send_sem=send_sems.at[a, k - 1], recv_sem=recv_sems.at[a, k - 1],
            device_id=(x, y, c), device_id_type=MESH)

    def start():
        for a in range(n):
            mine(a).start()
            for k in range(1, N_DEV):
                send(a, k).start()

    def finish():
        for a in range(n):
            for k in range(1, N_DEV):
                landed(a, k).wait_recv()
        for a in range(n):
            for k in range(1, N_DEV):
                send(a, k).wait_send()
            mine(a).wait()

    return start, finish


def _core_exchange_sems(n):
    return [pltpu.SemaphoreType.DMA((n, 4)), pltpu.SemaphoreType.DMA((n, N_DEV)), pltpu.SemaphoreType.DMA((n,))]


def _core_exchange_ops(ins, outs, to_core, send_sems, recv_sems, local_sems):
    n = len(ins)
    x, y, c = _my_place()
    me = 4 * x + 2 * y + c
    others = [(0, 1), (1, 0), (1, 1)]

    def slab(a, p):
        if len(ins[a].shape) == len(outs[a].shape):
            return ins[a].at[p]
        rows = outs[a].shape[1]
        return ins[a].at[pl.ds(pl.multiple_of(p * rows, 16), rows), :]

    def send(a, dx, dy):
        tx, ty = (x + dx) % 2, (y + dy) % 2
        return pltpu.make_async_remote_copy(
            src_ref=slab(a, 4 * to_core + 2 * tx + ty), dst_ref=outs[a].at[me], send_sem=send_sems.at[a, 2 * dx + dy],
            recv_sem=recv_sems.at[a, 2 * (2 * dx + dy) + c], device_id=(tx, ty, to_core), device_id_type=MESH)

    def mine(a):
        return pltpu.make_async_copy(slab(a, 4 * to_core + 2 * x + y), outs[a].at[me], local_sems.at[a])

    def landed(a, dx, dy, sc):
        frm = 4 * ((x + dx) % 2) + 2 * ((y + dy) % 2) + sc
        return pltpu.make_async_remote_copy(
            src_ref=slab(a, 0), dst_ref=outs[a].at[frm], send_sem=send_sems.at[a, 0], recv_sem=recv_sems.at[a, 2 * (2 * dx + dy) + sc],
            device_id=(x, y, c), device_id_type=MESH)

    def start():
        for a in range(n):
            for dx, dy in others:
                send(a, dx, dy).start()
            pl.when(c == to_core)(mine(a).start)
            pl.when(c != to_core)(send(a, 0, 0).start)

    def finish():
        @pl.when(c == to_core)
        def _():
            for a in range(n):
                for dx, dy in [(0, 0)] + others:
                    for sc in (0, 1):
                        if (dx, dy, sc) != (0, 0, to_core):
                            landed(a, dx, dy, sc).wait_recv()
            for a in range(n):
                mine(a).wait()

        @pl.when(c != to_core)
        def _():
            for a in range(n):
                send(a, 0, 0).wait_send()

        for a in range(n):
            for dx, dy in others:
                send(a, dx, dy).wait_send()

    return start, finish


N_CHIP = 4


def _pair_then_chip_sems(n):
    return [pltpu.SemaphoreType.DMA((n, N_CHIP)) for _ in range(6)] + [pltpu.SemaphoreType.DMA((n,))]


def _pair_then_chip_ops(ins, pairs, outs, mine_v, pair_v, sum_v, pair_send, pair_recv, chip_send, chip_recv, load_a, load_b, own_sem):
    n = len(ins)
    x, y, c = _my_place()
    chip = 2 * x + y
    chips = [(0, 0), (0, 1), (1, 0), (1, 1)]
    others = [(0, 1), (1, 0), (1, 1)]

    def to_sibling(a, j):
        px, py = chips[j]
        return pltpu.make_async_remote_copy(
            src_ref=ins[a].at[4 * px + 2 * py + 1 - c], dst_ref=pairs[a].at[j], send_sem=pair_send.at[a, j], recv_sem=pair_recv.at[a, j],
            device_id=(x, y, 1 - c), device_id_type=MESH)

    def spread(a, dx, dy):
        tx, ty = (x + dx) % 2, (y + dy) % 2
        return pltpu.make_async_remote_copy(
            src_ref=sum_v[a].at[2 * tx + ty], dst_ref=outs[a].at[chip], send_sem=chip_send.at[a, 2 * dx + dy],
            recv_sem=chip_recv.at[a, 2 * dx + dy], device_id=(tx, ty, c), device_id_type=MESH)

    def landed(a, dx, dy):
        frm = 2 * ((x + dx) % 2) + (y + dy) % 2
        return pltpu.make_async_remote_copy(
            src_ref=sum_v[a].at[0], dst_ref=outs[a].at[frm], send_sem=chip_send.at[a, 0], recv_sem=chip_recv.at[a, 2 * dx + dy],
            device_id=(x, y, c), device_id_type=MESH)

    def own(a):
        return pltpu.make_async_copy(sum_v[a].at[chip], outs[a].at[chip], own_sem.at[a])

    def pair():
        loads = []
        for a in range(n):
            for j, (px, py) in enumerate(chips):
                to_sibling(a, j).start()
                loads.append(pltpu.make_async_copy(ins[a].at[4 * px + 2 * py + c], mine_v[a].at[j], load_a.at[a, j]))
                loads[-1].start()
        for a in range(n):
            for j in range(N_CHIP):
                to_sibling(a, j).wait_recv()
                loads.append(pltpu.make_async_copy(pairs[a].at[j], pair_v[a].at[j], load_b.at[a, j]))
                loads[-1].start()
        for cp in loads:
            cp.wait()
        for a in range(n):
            sum_v[a][...] = (mine_v[a][...].astype(F32) + pair_v[a][...].astype(F32)).astype(sum_v[a].dtype)

    def start():
        pair()
        for a in range(n):
            own(a).start()
            for dx, dy in others:
                spread(a, dx, dy).start()

    def finish():
        for a in range(n):
            for dx, dy in others:
                landed(a, dx, dy).wait_recv()
        for a in range(n):
            for dx, dy in others:
                spread(a, dx, dy).wait_send()
            for j in range(N_CHIP):
                to_sibling(a, j).wait_send()
            own(a).wait()

    return start, finish


def _window_sum(x, win, ahead):
    n = x.shape[0]
    span = 1
    while span < win:
        x = x + pltpu.roll(x, n - span if ahead else span, 0)
        span *= 2
    return x


def _conv_branch(z, ext_u, conv_ref, tm):
    c_w = z.shape[1] // 4
    b, c, v = z[:, :c_w], z[:, c_w:2 * c_w], z[:, 2 * c_w:3 * c_w]
    u = c * v
    ext_u[pl.ds(HALO, tm), :] = u
    u1 = ext_u[pl.ds(HALO - 1, tm), :]
    u2 = ext_u[pl.ds(HALO - 2, tm), :]
    yc = conv_ref[pl.ds(2, 1), :] * u + conv_ref[pl.ds(1, 1), :] * u1 + conv_ref[pl.ds(0, 1), :] * u2
    return b, c, v, u, u1, u2, yc


def _pool_branch(p, ext_p, pool_w_ref, tm):
    ext_p[pl.ds(HALO, tm), :] = p
    pooled, mixed = [], []
    for g, win in enumerate(POOL_WINDOWS):
        s = _window_sum(ext_p[:, pl.ds(POOL_GROUP * g, POOL_GROUP)], win, ahead=False)[HALO:HALO + tm, :]
        pooled.append((s * (1.0 / win) - p[:, POOL_GROUP * g:POOL_GROUP * (g + 1)]).astype(BF16))
        mixed.append(_dot(pooled[-1], pool_w_ref[g].astype(BF16)))
    return pooled, mixed


def _gather_and_mixer_forward(x2d, mixer_shards, ffn_shards, g1, pool_w, pool_scale, g2, n_seq):
    t, d = x2d.shape
    zs, rs, ms, cs = mixer_shards[0].shape[1], mixer_shards[1].shape[0], mixer_shards[2].shape[1], mixer_shards[3].shape[1]
    zw, cw = N_DEV * zs, N_DEV * cs
    s = t // n_seq
    tm = min(TM_MIX, s)
    nj = s // tm
    n1, n2 = len(mixer_shards), len(ffn_shards)
    dtypes = [BF16, BF16, F32, F32] + [BF16] * n2
    shards = list(mixer_shards) + list(ffn_shards)

    def body(x_ref, *rest):
        shard_refs, (g1_ref, pw_ref, ps_ref, g2_ref), rest = rest[:n1 + n2], rest[n1 + n2:n1 + n2 + 4], rest[n1 + n2 + 4:]
        (h1_ref, z_ref, m_ref, pooled_ref, mixed_ref, win_o, wout_o, meta_o, conv_o, am_o, zm_o), rest = rest[:11], rest[11:]
        slabs, rest = rest[:n1 + n2], rest[n1 + n2:]
        stages, rest = rest[:n1 + n2], rest[n1 + n2:]
        win_v, wout_v, meta_v, conv_v, ext_u, ext_p, sem = rest[:7]
        first = _gather_ops(stages[:n1], slabs[:n1], *rest[7:10])
        later = _gather_ops(stages[n1:], slabs[n1:], *rest[10:13], core_major=True)

        @pl.when((pl.program_id(0) == 0) & (pl.program_id(1) == 0))
        def _():
            for src, dst in zip(shard_refs, stages):
                dst[...] = src[...].astype(dst.dtype)
            first[0]()
            later[0]()
            first[1]()
            first[2]()
            copies = [pltpu.make_async_copy(slabs[0].at[i], win_v.at[:, pl.ds(zs * i, zs)], sem.at[i]) for i in range(N_DEV)]
            copies += [pltpu.make_async_copy(slabs[1].at[i], wout_v.at[pl.ds(rs * i, rs), :], sem.at[N_DEV + i]) for i in range(N_DEV)]
            copies += [pltpu.make_async_copy(slabs[2], meta_v, sem.at[2 * N_DEV]), pltpu.make_async_copy(slabs[3], conv_v, sem.at[2 * N_DEV + 1])]
            for cp in copies:
                cp.start()
            for cp in copies:
                cp.wait()
            copies = [pltpu.make_async_copy(win_v, win_o, sem.at[0]), pltpu.make_async_copy(wout_v, wout_o, sem.at[1])]
            for cp in copies:
                cp.start()
            for i in range(N_DEV):
                meta_o[:, pl.ds(ms * i, ms)] = meta_v[i]
                conv_o[:, pl.ds(cs * i, cs)] = conv_v[i]
            hat, _ = _rms_stats(meta_o[...])
            a = (hat * g1_ref[...]).astype(BF16)
            am_o[...] = a
            zm_o[...] = _dot(a, win_v[...])
            for cp in copies:
                cp.wait()

        @pl.when(pl.program_id(1) == 0)
        def _():
            zm = zm_o[...]
            ext_u[pl.ds(0, HALO), :] = zm[:, cw:2 * cw] * zm[:, 2 * cw:3 * cw]
            ext_p[pl.ds(0, HALO), :] = zm[:, 3 * cw:]

        h0 = x_ref[...]
        hat, _ = _rms_stats(h0)
        z = _dot((hat * g1_ref[...]).astype(BF16), win_v[...])
        z_ref[...] = z.astype(BF16)
        b, _, _, _, _, _, yc = _conv_branch(z, ext_u, conv_o, tm)
        pooled, mixed = _pool_branch(z[:, 3 * cw:], ext_p, pw_ref, tm)
        pooled_ref[...] = jnp.concatenate(pooled, axis=1)
        mixed_ref[...] = jnp.concatenate(mixed, axis=1).astype(BF16)
        ps = ps_ref[...]
        y = [b * yc] + [mixed[g] * ps[:, POOL_GROUP * g:POOL_GROUP * (g + 1)] for g in range(len(POOL_WINDOWS))]
        m = _dot(jnp.concatenate(y, axis=1).astype(BF16), wout_v[...])
        m_ref[...] = m
        m_hat, _ = _rms_stats(m)
        h1_ref[...] = h0 + m_hat * g2_ref[...]
        ext_u[pl.ds(0, HALO), :] = ext_u[pl.ds(tm, HALO), :]
        ext_p[pl.ds(0, HALO), :] = ext_p[pl.ds(tm, HALO), :]

        @pl.when((pl.program_id(0) == n_seq - 1) & (pl.program_id(1) == nj - 1))
        def _():
            later[1]()
            later[2]()

    row = lambda b, j: (b * nj + j, 0)
    vmem = pl.BlockSpec(memory_space=pltpu.VMEM)
    small = [(N_META, d), (CONV_WIDTH, cw), (N_META, d), (N_META, zw)]
    out = pl.pallas_call(
        body, name="gather_and_mixer_forward", grid=(n_seq, nj),
        in_specs=[pl.BlockSpec((tm, d), row)] + [vmem] * (n1 + n2)
        + [_const(g1.shape), _const(pool_w.shape), _const(pool_scale.shape), _const(g2.shape)],
        out_specs=[pl.BlockSpec((tm, d), row), pl.BlockSpec((tm, zw), row), pl.BlockSpec((tm, d), row), pl.BlockSpec((tm, cw), row),
                   pl.BlockSpec((tm, cw), row), ANY, ANY] + [_const(sh) for sh in small] + [ANY] * (n1 + n2),
        out_shape=[jax.ShapeDtypeStruct((t, d), F32), jax.ShapeDtypeStruct((t, zw), BF16), jax.ShapeDtypeStruct((t, d), F32),
                   jax.ShapeDtypeStruct((t, cw), BF16), jax.ShapeDtypeStruct((t, cw), BF16),
                   jax.ShapeDtypeStruct((d, zw), BF16), jax.ShapeDtypeStruct((d, d), BF16),
                   jax.ShapeDtypeStruct(small[0], F32), jax.ShapeDtypeStruct(small[1], F32), jax.ShapeDtypeStruct(small[2], BF16),
                   jax.ShapeDtypeStruct(small[3], F32)]
        + [jax.ShapeDtypeStruct((N_DEV, *a.shape), dt) for a, dt in zip(shards, dtypes)],
        scratch_shapes=[pltpu.VMEM(a.shape, dt) for a, dt in zip(shards, dtypes)]
        + [pltpu.VMEM((d, zw), BF16), pltpu.VMEM((d, d), BF16), pltpu.VMEM((N_DEV, N_META, ms), F32),
           pltpu.VMEM((N_DEV, CONV_WIDTH, cs), F32), pltpu.VMEM((tm + HALO, cw), F32), pltpu.VMEM((tm + HALO, cw), F32),
           pltpu.SemaphoreType.DMA((2 * N_DEV + 2,))] + _exchange_sems(n1) + _exchange_sems(n2),
        compiler_params=_params("arbitrary", "arbitrary"),
    )(x2d, *shards, g1, pool_w, pool_scale, g2)
    return out[:5], out[5:11], out[11 + n1:]


def _mixer_backward(x2d, dh1, m, z, pooled, mixed, meta, a_meta, z_meta, g1, w_in, conv_w, pool_w, pool_scale, w_out, g2, n_seq,
                    to_exchange, landing):
    t, d = x2d.shape
    zw = w_in.shape[1]
    cw = zw // 4
    s = t // n_seq
    tm = min(TM_MIX, s)
    nj = s // tm
    n_groups = len(POOL_WINDOWS)
    zs = zw // N_DEV
    nx = len(to_exchange)
    n_in = 17
    given = [k for k, a in enumerate(landing) if a is not None]
    fresh = [k for k, a in enumerate(landing) if a is None]

    def body(x_ref, dh1_ref, m_ref, z_ref, zprev_ref, pooled_ref, mixed_ref, meta_ref, am_ref, zm_ref, g1_ref, win_ref, conv_ref, pw_ref, ps_ref, wout_ref,
             g2_ref, *rest):
        sent, rest = rest[:nx], rest[nx + len(given):]
        gx_ref, dwin_ref, dwout_ref, dg1_ref, dg2_ref, dconv_ref, dpw_ref, dps_ref, dmeta_ref = rest[:9]
        landed, rest = rest[9:9 + nx], rest[9 + nx:]
        ext_u, ext_dyc, ext_dq, acc_win, acc_wout, stage16, sem = rest[:7]
        north = _core_exchange_ops(sent, landed, 1, *rest[7:10])
        south = _core_exchange_ops([sent[k] for k in fresh], [landed[k] for k in fresh], 0, *rest[10:13])

        def start():
            north[0]()
            south[0]()

        def finish():
            south[1]()
            north[1]()

        b_id, j = pl.program_id(0), pl.program_id(1)
        jr = nj - 1 - j
        pl.when((b_id == 0) & (j == 0))(start)

        @pl.when((b_id == 0) & (j == 0))
        def _():
            acc_win[...] = jnp.zeros_like(acc_win)
            acc_wout[...] = jnp.zeros_like(acc_wout)
            for r in (dg1_ref, dg2_ref, dconv_ref, dpw_ref, dps_ref, dmeta_ref):
                r[...] = jnp.zeros_like(r)

        @pl.when(j == 0)
        def _():
            ext_dyc[pl.ds(tm, HALO), :] = jnp.zeros((HALO, cw), F32)
            ext_dq[pl.ds(tm, HALO), :] = jnp.zeros((HALO, cw), F32)

        zm = zm_ref[...]
        halo = jnp.where(jr == 0, zm, zprev_ref[...].astype(F32))
        ext_u[pl.ds(0, HALO), :] = halo[:, cw:2 * cw] * halo[:, 2 * cw:3 * cw]

        dh1v = dh1_ref[...]
        m_hat, m_rstd = _rms_stats(m_ref[...])
        dm, dg2 = _rms_bwd(m_hat, m_rstd, g2_ref[...], dh1v)
        dg2_ref[...] += dg2
        dm = dm.astype(BF16)
        dycat = _dot_nt(dm, wout_ref[...])

        b, c, v, u, u1, u2, yc = _conv_branch(z_ref[...].astype(F32), ext_u, conv_ref, tm)
        mixed = [mixed_ref[:, pl.ds(POOL_GROUP * g, POOL_GROUP)].astype(F32) for g in range(n_groups)]
        ps = ps_ref[...]
        y = [b * yc] + [mixed[g] * ps[:, POOL_GROUP * g:POOL_GROUP * (g + 1)] for g in range(n_groups)]
        ycat = jnp.concatenate(y, axis=1).astype(BF16)
        acc_wout[...] += _dot_tn(ycat, dm)

        dyconv = dycat[:, :cw]
        db = dyconv * yc
        dyc = dyconv * b
        ext_dyc[pl.ds(0, tm), :] = dyc
        du = (conv_ref[pl.ds(2, 1), :] * dyc + conv_ref[pl.ds(1, 1), :] * ext_dyc[pl.ds(1, tm), :]
              + conv_ref[pl.ds(0, 1), :] * ext_dyc[pl.ds(2, tm), :])
        dconv_ref[pl.ds(2, 1), :] += jnp.sum(dyc * u, axis=0, keepdims=True)
        dconv_ref[pl.ds(1, 1), :] += jnp.sum(dyc * u1, axis=0, keepdims=True)
        dconv_ref[pl.ds(0, 1), :] += jnp.sum(dyc * u2, axis=0, keepdims=True)

        dp = []
        for g, win in enumerate(POOL_WINDOWS):
            lanes = pl.ds(POOL_GROUP * g, POOL_GROUP)
            dypool = dycat[:, cw + POOL_GROUP * g:cw + POOL_GROUP * (g + 1)]
            dps_ref[:, lanes] += jnp.sum(dypool * mixed[g], axis=0, keepdims=True)
            dmixed = (dypool * ps[:, POOL_GROUP * g:POOL_GROUP * (g + 1)]).astype(BF16)
            dq = _dot_nt(dmixed, pw_ref[g].astype(BF16))
            dpw_ref[g] += _dot_tn(pooled_ref[:, lanes], dmixed)
            ext_dq[pl.ds(0, tm), lanes] = dq
            acc = _window_sum(ext_dq[:, lanes], win, ahead=True)[0:tm, :]
            dp.append(acc * (1.0 / win) - dq)

        dz = jnp.concatenate([db, du * v, du * c] + dp, axis=1).astype(BF16)
        da = _dot_nt(dz, win_ref[...])
        h0 = x_ref[...]
        hat0, rstd0 = _rms_stats(h0)
        g1 = g1_ref[...]
        acc_win[...] += _dot_tn((hat0 * g1).astype(BF16), dz)
        dh0, dg1 = _rms_bwd(hat0, rstd0, g1, da)
        dg1_ref[...] += dg1
        gx_ref[...] = dh1v + dh0

        ext_dyc[pl.ds(tm, HALO), :] = ext_dyc[pl.ds(0, HALO), :]
        ext_dq[pl.ds(tm, HALO), :] = ext_dq[pl.ds(0, HALO), :]

        @pl.when(jr == 0)
        def _():
            ext_dyc[pl.ds(tm - HALO, HALO), :] = jnp.zeros((HALO, cw), F32)
            ext_dq[pl.ds(tm - HALO, HALO), :] = jnp.zeros((HALO, cw), F32)
            du_m = (conv_ref[pl.ds(1, 1), :] * ext_dyc[pl.ds(tm - HALO + 1, HALO), :]
                    + conv_ref[pl.ds(0, 1), :] * ext_dyc[pl.ds(tm - HALO + 2, HALO), :])
            dp_m = []
            for g, win in enumerate(POOL_WINDOWS):
                lanes = pl.ds(POOL_GROUP * g, POOL_GROUP)
                acc = ext_dq[pl.ds(tm - HALO + 1, HALO), lanes]
                for k in range(2, win):
                    acc = acc + ext_dq[pl.ds(tm - HALO + k, HALO), lanes]
                dp_m.append(acc * (1.0 / win))
            dz_m = jnp.concatenate([jnp.zeros((HALO, cw), F32), du_m * zm[:, 2 * cw:3 * cw], du_m * zm[:, cw:2 * cw]] + dp_m,
                                   axis=1).astype(BF16)
            acc_win[...] += _dot_tn(am_ref[...], dz_m)
            hat_m, rstd_m = _rms_stats(meta_ref[...])
            dmeta, dg1_m = _rms_bwd(hat_m, rstd_m, g1, _dot_nt(dz_m, win_ref[...]))
            dg1_ref[...] += dg1_m
            dmeta_ref[...] += dmeta

        @pl.when((b_id == n_seq - 1) & (j == nj - 1))
        def _():
            pieces = [(acc_win, zs * i, dwin_ref.at[i]) for i in range(N_DEV)]
            pieces += [(acc_wout, zs * i, dwout_ref.at[:, pl.ds(zs * i, zs)]) for i in range(d // zs)]
            copies = []
            for k, (acc, col, dst) in enumerate(pieces):
                if k >= 2:
                    copies[k - 2].wait()
                stage16[k % 2] = acc[:, pl.ds(col, zs)].astype(BF16)
                copies.append(pltpu.make_async_copy(stage16.at[k % 2], dst, sem.at[k % 2]))
                copies[k].start()
            copies[-2].wait()
            copies[-1].wait()
            finish()

    row = lambda b, j: (b * nj + nj - 1 - j, 0)
    prev = lambda b, j: (jnp.maximum((b * s + (nj - 1 - j) * tm) // HALO - 1, 0), 0)
    small = [g1.shape, g2.shape, conv_w.shape, pool_w.shape, pool_scale.shape, meta.shape]
    out = pl.pallas_call(
        body, name="mixer_backward", grid=(n_seq, nj),
        in_specs=[pl.BlockSpec((tm, d), row), pl.BlockSpec((tm, d), row), pl.BlockSpec((tm, d), row), pl.BlockSpec((tm, zw), row),
                  pl.BlockSpec((HALO, zw), prev), pl.BlockSpec((tm, cw), row), pl.BlockSpec((tm, cw), row), _const(meta.shape), _const(a_meta.shape), _const(z_meta.shape), _const(g1.shape),
                  _resident(w_in.shape), _const(conv_w.shape), _const(pool_w.shape), _const(pool_scale.shape), _resident(w_out.shape),
                  _const(g2.shape)] + [ANY] * (nx + len(given)),
        out_specs=[pl.BlockSpec((tm, d), row), ANY, ANY] + [_const(sh) for sh in small] + [ANY] * nx,
        out_shape=[jax.ShapeDtypeStruct((t, d), F32), jax.ShapeDtypeStruct((N_DEV, d, zs), BF16),
                   jax.ShapeDtypeStruct(w_out.shape, BF16)] + [jax.ShapeDtypeStruct(sh, F32) for sh in small]
        + [jax.ShapeDtypeStruct((N_DEV, a.shape[0] // N_DEV, a.shape[1]), a.dtype) for a in to_exchange],
        input_output_aliases={n_in + nx + at: 9 + k for at, k in enumerate(given)},
        scratch_shapes=[pltpu.VMEM((tm + HALO, cw), F32)] * 3
        + [pltpu.VMEM(w_in.shape, F32), pltpu.VMEM(w_out.shape, F32), pltpu.VMEM((2, d, zs), BF16),
           pltpu.SemaphoreType.DMA((2,))] + _core_exchange_sems(nx) + _core_exchange_sems(len(fresh)),
        compiler_params=_params("arbitrary", "arbitrary"),
    )(x2d, dh1, m, z, z, pooled, mixed, meta, a_meta, z_meta, g1, w_in, conv_w, pool_w, pool_scale, w_out, g2, *to_exchange, *[landing[k] for k in given])
    return out[:9], out[9:]


def _ffn_forward_backward(h1, target, g3, w_gate, w_up, w_down, g4):
    t, d = h1.shape
    ff = w_gate.shape[0]
    tm = min(TM_FFN, t)
    nt = t // tm
    chunks = [(s, min(FFN_CHUNK, ff - s)) for s in range(0, ff, FFN_CHUNK)]

    def body(h1_ref, h1pp_ref, tgt_ref, g3_ref, wg_ref, wu_ref, wd_ref, g4_ref,
             f_ref, act_ref, dd_ref, dgate_ref, dup_ref, dh1_ref, loss_ref, dg3_ref, dg4_ref, *slots):
        gate_s, up_s, dd_s, dh2_s, df_s = slots
        i = pl.program_id(0)

        def forward(slot):
            h1v = h1_ref[...]
            hat, _ = _rms_stats(h1v)
            f = (hat * g3_ref[...]).astype(BF16)
            f_ref[...] = f
            s, n = chunks[0]
            gate, up = _dot_nt(f_ref[...], wg_ref[pl.ds(s, n), :]), _dot_nt(f_ref[...], wu_ref[pl.ds(s, n), :])
            yield
            down = None
            for k, (s, n) in enumerate(chunks):
                gate_s.at[slot][:, pl.ds(s, n)] = gate.astype(BF16)
                up_s.at[slot][:, pl.ds(s, n)] = up.astype(BF16)
                act = (gate * jax.nn.sigmoid(gate) * up).astype(BF16)
                act_ref[:, pl.ds(s, n)] = act
                if k + 1 < len(chunks):
                    s1, n1 = chunks[k + 1]
                    gate, up = _dot_nt(f_ref[...], wg_ref[pl.ds(s1, n1), :]), _dot_nt(f_ref[...], wu_ref[pl.ds(s1, n1), :])
                yield
                part = _dot(act_ref[:, pl.ds(s, n)], wd_ref[pl.ds(s, n), :])
                down = part if down is None else down + part
                yield
            d_hat, d_rstd = _rms_stats(down)
            g4 = g4_ref[...]
            err = h1v + d_hat * g4 - tgt_ref[...]
            loss_ref[...] += jnp.sum(err * err) * (0.5 / d)
            dh2 = err * (1.0 / d)
            dh2_s.at[slot][...] = dh2
            dd, dg4 = _rms_bwd(d_hat, d_rstd, g4, dh2)
            dg4_ref[...] += dg4
            dd = dd.astype(BF16)
            dd_ref[...] = dd
            dd_s.at[slot][...] = dd

        def backward(slot):
            s, n = chunks[0]
            dact = _dot_nt(dd_s.at[slot][...], wd_ref[pl.ds(s, n), :])
            yield
            df = None
            for k, (s, n) in enumerate(chunks):
                gate = gate_s.at[slot][:, pl.ds(s, n)].astype(F32)
                up = up_s.at[slot][:, pl.ds(s, n)].astype(F32)
                sig = jax.nn.sigmoid(gate)
                dup = (dact * (gate * sig)).astype(BF16)
                dgate = (dact * up * (sig * (1.0 + gate * (1.0 - sig)))).astype(BF16)
                dup_ref[:, pl.ds(s, n)] = dup
                dgate_ref[:, pl.ds(s, n)] = dgate
                if k + 1 < len(chunks):
                    s1, n1 = chunks[k + 1]
                    dact = _dot_nt(dd_s.at[slot][...], wd_ref[pl.ds(s1, n1), :])
                yield
                part = _dot(dgate_ref[:, pl.ds(s, n)], wg_ref[pl.ds(s, n), :]) + _dot(dup_ref[:, pl.ds(s, n)], wu_ref[pl.ds(s, n), :])
                df = part if df is None else df + part
                yield
            df_s.at[slot][...] = df

        def last(slot):
            hat, rstd = _rms_stats(h1pp_ref[...])
            dh1, dg3 = _rms_bwd(hat, rstd, g3_ref[...], df_s.at[slot][...])
            dg3_ref[...] += dg3
            dh1_ref[...] = dh2_s.at[slot][...] + dh1

        def emit(parity, with_forward, with_backward, with_last):
            fwd = forward(parity) if with_forward else iter(())
            bwd = backward(1 - parity) if with_backward else iter(())
            next(fwd, None)
            if with_last:
                last(parity)
            for _ in range(FFN_BACKWARD_LAG):
                next(fwd, None)
            alive = True
            while alive:
                alive = next(bwd, True) is None
                alive = (next(fwd, True) is None) or alive

        @pl.when(i == 0)
        def _():
            for r in (loss_ref, dg3_ref, dg4_ref, *slots):
                r[...] = jnp.zeros_like(r)

        @pl.when(i < nt)
        def _():
            emit(i % 2, True, True, True)

        @pl.when(i == nt)
        def _():
            emit(nt % 2, False, True, True)

        @pl.when(i == nt + 1)
        def _():
            emit((nt + 1) % 2, False, False, True)

    cur = lambda i: (jnp.minimum(i, nt - 1), 0)
    prev = lambda i: (jnp.clip(i - 1, 0, nt - 1), 0)
    prev2 = lambda i: (jnp.clip(i - 2, 0, nt - 1), 0)
    return pl.pallas_call(
        body, name="ffn_forward_backward", grid=(nt + 2,),
        in_specs=[pl.BlockSpec((tm, d), cur), pl.BlockSpec((tm, d), prev2), pl.BlockSpec((tm, d), cur), _const(g3.shape),
                  _resident(w_gate.shape), _resident(w_up.shape), _resident(w_down.shape), _const(g4.shape)],
        out_specs=[pl.BlockSpec((tm, d), cur), pl.BlockSpec((tm, ff), cur), pl.BlockSpec((tm, d), cur), pl.BlockSpec((tm, ff), prev),
                   pl.BlockSpec((tm, ff), prev), pl.BlockSpec((tm, d), prev2), _const((8, 128)), _const(g3.shape), _const(g4.shape)],
        out_shape=[jax.ShapeDtypeStruct((t, d), BF16), jax.ShapeDtypeStruct((t, ff), BF16), jax.ShapeDtypeStruct((t, d), BF16),
                   jax.ShapeDtypeStruct((t, ff), BF16), jax.ShapeDtypeStruct((t, ff), BF16), jax.ShapeDtypeStruct((t, d), F32),
                   jax.ShapeDtypeStruct((8, 128), F32), jax.ShapeDtypeStruct(g3.shape, F32), jax.ShapeDtypeStruct(g4.shape, F32)],
        scratch_shapes=[pltpu.VMEM((2, tm, ff), BF16)] * 2 + [pltpu.VMEM((2, tm, d), BF16)] + [pltpu.VMEM((2, tm, d), F32)] * 2,
        compiler_params=_params("arbitrary"),
    )(h1, h1, target, g3, w_gate, w_up, w_down, g4)


def _ffn_weight_grads(f, dd, dgate, dup, act):
    t, d = f.shape
    ff = dgate.shape[1]
    tm = min(TM_WGRAD, t)
    nt = t // tm
    fc = ff // FF_CHUNKS
    assert FF_CHUNKS == 2

    def body(f_ref, dd_ref, dgate_ref, dup_ref, act_ref, dwg_ref, dwu_ref, dwd_ref, *rest):
        landing, (acc_g, acc_u, acc_d, stage, sem) = rest[:2], rest[2:7]
        start, finish = _core_exchange_ops([dwg_ref, dwd_ref], landing, 0, *rest[7:])
        c, i = pl.program_id(0), pl.program_id(1)
        pl.when((c == 1) & (i == 0))(start)

        @pl.when(i == 0)
        def _():
            acc_g[...] = jnp.zeros_like(acc_g)
            acc_u[...] = jnp.zeros_like(acc_u)
            acc_d[...] = jnp.zeros_like(acc_d)

        fv = f_ref[...]
        acc_g[...] += _dot_tn(fv, dgate_ref[...])
        acc_u[...] += _dot_tn(fv, dup_ref[...])
        acc_d[...] += _dot_tn(act_ref[...], dd_ref[...])

        @pl.when(i == nt - 1)
        def _():
            rows = pl.ds(pl.multiple_of(c * fc, 16), fc)
            copies = []
            for k, (acc, out, transposed) in enumerate(((acc_d, dwd_ref, False), (acc_g, dwg_ref, True), (acc_u, dwu_ref, True))):
                if k >= 2:
                    copies[k - 2].wait()
                stage[k % 2] = (acc[...].T if transposed else acc[...]).astype(BF16)
                copies.append(pltpu.make_async_copy(stage.at[k % 2], out.at[rows, :], sem.at[k % 2]))
                copies[k].start()
            copies[-2].wait()
            copies[-1].wait()

        pl.when((c == 1) & (i == nt - 1))(finish)

    row = lambda c, i: (i, 0)
    col = lambda c, i: (i, c)
    out = pl.pallas_call(
        body, name="ffn_weight_grads", grid=(FF_CHUNKS, nt),
        in_specs=[pl.BlockSpec((tm, d), row), pl.BlockSpec((tm, d), row), pl.BlockSpec((tm, fc), col), pl.BlockSpec((tm, fc), col),
                  pl.BlockSpec((tm, fc), col)],
        out_specs=[ANY] * 5,
        out_shape=[jax.ShapeDtypeStruct((ff, d), BF16)] * 3 + [jax.ShapeDtypeStruct((N_DEV, ff // N_DEV, d), BF16)] * 2,
        scratch_shapes=[pltpu.VMEM((d, fc), F32), pltpu.VMEM((d, fc), F32), pltpu.VMEM((fc, d), F32), pltpu.VMEM((2, fc, d), BF16),
                        pltpu.SemaphoreType.DMA((2,))] + _core_exchange_sems(2),
        compiler_params=_params("arbitrary", "arbitrary"),
    )(f, dd, dgate, dup, act)
    return out[:3], [out[3], None, out[4]]


def _adamw(w, g, m, v):
    m = ADAM_B1 * m + (1.0 - ADAM_B1) * g
    v = ADAM_B2 * v + (1.0 - ADAM_B2) * (g * g)
    m_hat = m / (1.0 - ADAM_B1 ** ADAM_STEP)
    v_hat = v / (1.0 - ADAM_B2 ** ADAM_STEP)
    return -ADAM_LR * (m_hat / (jnp.sqrt(v_hat) + ADAM_EPS) + ADAM_WD * w), m, v


def _sum_slabs(ref):
    total = ref[0].astype(F32)
    for i in range(1, ref.shape[0]):
        total = total + ref[i].astype(F32)
    return total


def _adamw_rows(r, c):
    tr = r
    for cand in range(8, r, 8):
        if r % cand == 0 and cand * c <= ADAMW_BLOCK_ELEMS:
            tr = cand
    return r if r * c <= ADAMW_BLOCK_ELEMS else tr


def _reduce_adamw_carrying(parts, ws, ms, vs, to_reduce, to_exchange, whole, name):
    k, nr, nx = len(ws), len(to_reduce), len(to_exchange)
    r, c = ws[0].shape if k else (8, 128)
    tr = _adamw_rows(r, c)
    steps = r // tr
    travels = nr + nx > 0
    chip_slabs = [jax.ShapeDtypeStruct((N_CHIP, *a.shape[1:]), a.dtype) for a in to_reduce]

    def body(*refs):
        p_refs, w_refs, m_refs, v_refs = (refs[a * k:(a + 1) * k] for a in range(4))
        refs = refs[4 * k:]
        reduced_in, sent, refs = refs[:nr], refs[nr:nr + nx], refs[nr + nx:]
        outs, pairs, sums, landed, refs = refs[:4 * k], refs[4 * k:4 * k + nr], refs[4 * k + nr:4 * k + 2 * nr], \
            refs[4 * k + 2 * nr:4 * k + 2 * nr + nx], refs[4 * k + 2 * nr + nx:]
        mine_v, pair_v, sum_v, refs = refs[:nr], refs[nr:2 * nr], refs[2 * nr:3 * nr], refs[3 * nr:]
        if travels:
            reduce_ops = _pair_then_chip_ops(reduced_in, pairs, sums, mine_v, pair_v, sum_v, *refs[:7])
            direct_ops = _exchange_ops(sent, landed, whole, *refs[7:])

            @pl.when(pl.program_id(0) == 0)
            def _():
                direct_ops[0]()
                reduce_ops[0]()

        for a in range(k):
            g = _sum_slabs(p_refs[a])
            outs[4 * a][...] = g
            outs[4 * a + 1][...], outs[4 * a + 2][...], outs[4 * a + 3][...] = _adamw(w_refs[a][...], g, m_refs[a][...], v_refs[a][...])

        if travels:
            @pl.when(pl.program_id(0) == steps - 1)
            def _():
                reduce_ops[1]()
                direct_ops[1]()

    blk = pl.BlockSpec((tr, c), lambda i: (i, 0))
    out = pl.pallas_call(
        body, name=name, grid=(steps,),
        in_specs=[pl.BlockSpec((N_DEV, tr, c), lambda i: (0, i, 0))] * k + [blk] * (3 * k) + [ANY] * (nr + nx),
        out_specs=[blk] * (4 * k) + [ANY] * (2 * nr + nx),
        out_shape=[jax.ShapeDtypeStruct((r, c), F32)] * (4 * k) + chip_slabs + chip_slabs
        + [jax.ShapeDtypeStruct((N_DEV, *a.shape) if w else a.shape, a.dtype) for a, w in zip(to_exchange, whole)],
        scratch_shapes=([pltpu.VMEM(a.shape, a.dtype) for a in chip_slabs] * 3 + _pair_then_chip_sems(nr) + _exchange_sems(nx)
                        if travels else []),
        compiler_params=_params("arbitrary"),
    )(*parts, *ws, *ms, *vs, *to_reduce, *to_exchange)
    return [tuple(out[4 * a:4 * a + 4]) for a in range(k)], out[4 * k + nr:4 * k + 2 * nr], out[4 * k + 2 * nr:]


def _reduce_adamw(parts, w, m, v, name):
    r, c = w.shape
    tr = _adamw_rows(r, c)

    def body(p_ref, w_ref, m_ref, v_ref, g_out, d_out, m_out, v_out):
        g = _sum_slabs(p_ref)
        g_out[...] = g
        d_out[...], m_out[...], v_out[...] = _adamw(w_ref[...], g, m_ref[...], v_ref[...])

    blk = pl.BlockSpec((tr, c), lambda i: (i, 0))
    return pl.pallas_call(
        body, name=name, grid=(r // tr,),
        in_specs=[pl.BlockSpec((parts.shape[0], tr, c), lambda i: (0, i, 0)), blk, blk, blk],
        out_specs=[blk] * 4, out_shape=[jax.ShapeDtypeStruct((r, c), F32)] * 4,
        compiler_params=_params("arbitrary"),
    )(parts, w, m, v)


def _reduce_adamw_small(parts, ws, ms, vs, loss_parts):
    n = len(parts)

    def body(*refs):
        p_refs, w_refs, m_refs, v_refs = (refs[k * n:(k + 1) * n] for k in range(4))
        outs = refs[4 * n + 1:]
        outs[4 * n][...] = _sum_slabs(refs[4 * n])
        for a in range(n):
            g = _sum_slabs(p_refs[a])
            outs[4 * a][...] = g
            outs[4 * a + 1][...], outs[4 * a + 2][...], outs[4 * a + 3][...] = _adamw(w_refs[a][...], g, m_refs[a][...], v_refs[a][...])

    out = pl.pallas_call(
        body, name="adamw_replicated",
        out_shape=[jax.ShapeDtypeStruct(w.shape, F32) for w in ws for _ in range(4)] + [jax.ShapeDtypeStruct(loss_parts.shape[1:], F32)],
        compiler_params=pltpu.CompilerParams(vmem_limit_bytes=VMEM_LIMIT_BYTES),
    )(*parts, *ws, *ms, *vs, loss_parts)
    return [tuple(out[4 * a:4 * a + 4]) for a in range(n)], out[4 * n]


def kernel(x, meta_tokens, norm_mix_pre, w_in, conv_w, pool_w, pool_scale, w_out, norm_mix_post, norm_ffn_pre, w_gate, w_up, w_down, norm_ffn_post, loss_target, m_meta_tokens, m_norm_mix_pre, m_w_in, m_conv_w, m_pool_w, m_pool_scale, m_w_out, m_norm_mix_post, m_norm_ffn_pre, m_w_gate, m_w_up, m_w_down, m_norm_ffn_post, v_meta_tokens, v_norm_mix_pre, v_w_in, v_conv_w, v_pool_w, v_pool_scale, v_w_out, v_norm_mix_post, v_norm_ffn_pre, v_w_gate, v_w_up, v_w_down, v_norm_ffn_post):
    n_seq, seq, d = x.shape
    x2d = x.reshape(n_seq * seq, d)
    target = loss_target.reshape(n_seq * seq, d)

    t_ = lambda a: jnp.swapaxes(a[0], 0, 1)
    pw, ps = pool_w[0], pool_scale

    (h1, z, m, pooled, mixed), (win_b, wout_b, meta, conv, a_meta, z_meta), ffn_slabs = _gather_and_mixer_forward(
        x2d, [w_in[0], w_out[0], meta_tokens, conv_w[0]], [t_(w_gate), t_(w_up), w_down[0]], norm_mix_pre, pw, ps, norm_mix_post, n_seq)
    wg_b, wu_b, wd_b = (s.reshape(-1, d) for s in ffn_slabs)
    f, act, dd, dgate, dup, dh1, loss_sum, dg3, dg4 = _ffn_forward_backward(h1, target, norm_ffn_pre, wg_b, wu_b, wd_b, norm_ffn_post)
    ffn_grads, landing = _ffn_weight_grads(f, dd, dgate, dup, act)
    (gx, dwin, dwout, dg1, dg2, dconv, dpw, dps, dmeta), ffn_parts = _mixer_backward(
        x2d, dh1, m, z, pooled, mixed, meta, a_meta, z_meta, norm_mix_pre, win_b, conv, pw, ps, wout_b, norm_mix_post, n_seq,
        ffn_grads, landing)

    dmeta_s = jnp.transpose(dmeta.reshape(N_META, N_DEV, -1), (1, 0, 2))
    dconv_s = jnp.transpose(dconv.reshape(CONV_WIDTH, N_DEV, -1), (1, 0, 2))
    _, (win_parts, wout_parts), last = _reduce_adamw_carrying(
        [], [], [], [], [dwin, dwout.reshape(N_DEV, -1, d)], [dmeta_s, dconv_s, dg1, dg2, dg3, dg4, dpw, dps, loss_sum],
        [False] * 2 + [True] * 7, "exchange_rest")
    ffn_res, _, _ = _reduce_adamw_carrying(
        ffn_parts, [t_(w_gate), t_(w_up), w_down[0]], [t_(m_w_gate), t_(m_w_up), m_w_down[0]], [t_(v_w_gate), t_(v_w_up), v_w_down[0]],
        [], [], [], "adamw_ffn")
    replicated = last[2:8]

    names = ["meta_tokens", "norm_mix_pre", "w_in", "conv_w", "pool_w", "pool_scale", "w_out", "norm_mix_post", "norm_ffn_pre", "w_gate",
             "w_up", "w_down", "norm_ffn_post"]
    res = {"w_gate": tuple(jnp.swapaxes(o, 0, 1)[None] for o in ffn_res[0]),
           "w_up": tuple(jnp.swapaxes(o, 0, 1)[None] for o in ffn_res[1]), "w_down": tuple(o[None] for o in ffn_res[2])}
    for nm, parts, w, m_, v_ in (("w_in", win_parts, w_in, m_w_in, v_w_in), ("w_out", wout_parts, w_out, m_w_out, v_w_out),
                                 ("conv_w", last[1], conv_w, m_conv_w, v_conv_w)):
        res[nm] = tuple(o[None] for o in _reduce_adamw(parts, w[0], m_[0], v_[0], "adamw_" + nm))
    res["meta_tokens"] = tuple(_reduce_adamw(last[0], meta_tokens, m_meta_tokens, v_meta_tokens, "adamw_meta_tokens"))
    small, loss = _reduce_adamw_small(
        replicated, [norm_mix_pre, norm_mix_post, norm_ffn_pre, norm_ffn_post, pool_w[0], pool_scale],
        [m_norm_mix_pre, m_norm_mix_post, m_norm_ffn_pre, m_norm_ffn_post, m_pool_w[0], m_pool_scale],
        [v_norm_mix_pre, v_norm_mix_post, v_norm_ffn_pre, v_norm_ffn_post, v_pool_w[0], v_pool_scale], last[8])
    for nm, r in zip(["norm_mix_pre", "norm_mix_post", "norm_ffn_pre", "norm_ffn_post", "pool_w", "pool_scale"], small):
        res[nm] = tuple(o[None] for o in r) if nm == "pool_w" else r

    return (loss[0, 0], gx.reshape(n_seq, seq, d), *[res[nm][0] for nm in names], *[res[nm][1] for nm in names],
            *[res[nm][2] for nm in names], *[res[nm][3] for nm in names])
```

```python
import jax
import jax.numpy as jnp
from jax import lax
from jax.experimental import pallas as pl
from jax.experimental.pallas import tpu as pltpu

F32, BF16 = jnp.float32, jnp.bfloat16
RMS_EPS = 1e-6
N_META = 16
CONV_WIDTH = 3
POOL_WINDOWS = (2, 4, 8, 16)
POOL_GROUP = 128
HALO = 16
N_DEV = 8
MESH_AXES = ("x", "y", "c")
MESH = pl.DeviceIdType.MESH
VMEM_LIMIT_BYTES = 56 * 1024 * 1024
ADAMW_BLOCK_ELEMS = 64 * 1024
TM_MIX = 512
TM_FFN = 256
FFN_CHUNK = 512
FFN_BACKWARD_LAG = 2
TM_WGRAD = 512
FF_CHUNKS = 2

ADAM_LR, ADAM_B1, ADAM_B2, ADAM_EPS, ADAM_WD, ADAM_STEP = 0.001, 0.9, 0.999, 1e-08, 0.01, 10


def _dot(a, b):
    return jnp.dot(a, b, preferred_element_type=F32)


def _dot_nt(a, b):
    return lax.dot_general(a, b, (((1,), (1,)), ((), ())), preferred_element_type=F32)


def _dot_tn(a, b):
    return lax.dot_general(a, b, (((0,), (0,)), ((), ())), preferred_element_type=F32)


def _rms_stats(h):
    rstd = lax.rsqrt(jnp.mean(h * h, axis=-1, keepdims=True) + RMS_EPS)
    return h * rstd, rstd


def _rms_bwd(hat, rstd, g, dy):
    gdy = dy * g
    proj = jnp.mean(gdy * hat, axis=-1, keepdims=True)
    return rstd * (gdy - hat * proj), jnp.sum(dy * hat, axis=0, keepdims=True)


def _params(*semantics):
    return pltpu.CompilerParams(dimension_semantics=semantics or None, vmem_limit_bytes=VMEM_LIMIT_BYTES)


def _resident(shape):
    zeros = (0,) * len(shape)
    return pl.BlockSpec(shape, lambda *_: zeros, pipeline_mode=pl.Buffered(1))


def _const(shape):
    zeros = (0,) * len(shape)
    return pl.BlockSpec(shape, lambda *_: zeros)


ANY = pl.BlockSpec(memory_space=pl.ANY)


def _my_place():
    x, y, c = (lax.axis_index(a) for a in MESH_AXES)
    return x, y, c


def _exchange_sems(n):
    return [pltpu.SemaphoreType.DMA((n, N_DEV - 1)), pltpu.SemaphoreType.DMA((n, N_DEV - 1)), pltpu.SemaphoreType.DMA((n,))]


def _gather_ops(srcs, outs, send_sems, recv_sems, local_sems, core_major=False):
    n = len(srcs)
    x, y, c = _my_place()
    me, sibling = (x, y, c), (x, y, 1 - c)
    chips = [(1 - x, y), (x, 1 - y), (1 - x, 1 - y)]

    def slab(px, py, pc):
        return 4 * pc + 2 * px + py if core_major else 4 * px + 2 * py + pc

    def copy(a, k, block, to, src=None):
        dst = outs[a].at[slab(*block)]
        return pltpu.make_async_remote_copy(
            src_ref=dst if src is None else src, dst_ref=dst, send_sem=send_sems.at[a, k], recv_sem=recv_sems.at[a, k],
            device_id=to, device_id_type=MESH)

    def mine(a):
        return pltpu.make_async_copy(srcs[a], outs[a].at[slab(*me)], local_sems.at[a])

    def first(a):
        return [copy(a, 0, me, sibling, src=srcs[a])] + [copy(a, 1 + j, me, (*chip, c), src=srcs[a]) for j, chip in enumerate(chips)]

    def passed(a, j):
        return copy(a, 4 + j, (*chips[j], c), sibling)

    def start():
        for a in range(n):
            mine(a).start()
            for cp in first(a):
                cp.start()

    def forward():
        for j, chip in enumerate(chips):
            for a in range(n):
                copy(a, 1 + j, (*chip, c), me).wait_recv()
                passed(a, j).start()

    def finish():
        for a in range(n):
            copy(a, 0, sibling, me).wait_recv()
            for j, chip in enumerate(chips):
                copy(a, 4 + j, (*chip, 1 - c), me).wait_recv()
        for a in range(n):
            for cp in first(a) + [passed(a, j) for j in range(len(chips))]:
                cp.wait_send()
            mine(a).wait()

    return start, forward, finish


def _exchange_ops(ins, outs, whole, send_sems, recv_sems, local_sems):
    n = len(ins)
    x, y, c = _my_place()
    me = 4 * x + 2 * y + c

    def src(a, i):
        return ins[a] if whole[a] else ins[a].at[i]

    def mine(a):
        return pltpu.make_async_copy(src(a, me), outs[a].at[me], local_sems.at[a])

    def send(a, k):
        to = (me + k) % N_DEV
        return pltpu.make_async_remote_copy(
            src_ref=src(a, to), dst_ref=outs[a].at[me], send_sem=send_sems.at[a, k - 1], recv_sem=recv_sems.at[a, k - 1],
            device_id=(to // 4, (to // 2) % 2, to % 2), device_id_type=MESH)

    def landed(a, k):
        frm = (me + N_DEV - k) % N_DEV
        return pltpu.make_async_remote_copy(
            src_ref=src(a, frm), dst_ref=outs[a].at[frm], send_sem=send_sems.at[a, k - 1], recv_sem=recv_sems.at[a, k - 1],
            device_id=(x, y, c), device_id_type=MESH)

    def start():
        for a in range(n):
            mine(a).start()
            for k in range(1, N_DEV):
                send(a, k).start()

    def finish():
        for a in range(n):
            for k in range(1, N_DEV):
                landed(a, k).wait_recv()
        for a in range(n):
            for k in range(1, N_DEV):
                send(a, k).wait_send()
            mine(a).wait()

    return start, finish


def _core_exchange_sems(n):
    return [pltpu.SemaphoreType.DMA((n, 4)), pltpu.SemaphoreType.DMA((n, N_DEV)), pltpu.SemaphoreType.DMA((n,))]


def _core_exchange_ops(ins, outs, to_core, send_sems, recv_sems, local_sems):
    n = len(ins)
    x, y, c = _my_place()
    me = 4 * x + 2 * y + c
    others = [(0, 1), (1, 0), (1, 1)]

    def slab(a, p):
        if len(ins[a].shape) == len(outs[a].shape):
            return ins[a].at[p]
        rows = outs[a].shape[1]
        return ins[a].at[pl.ds(pl.multiple_of(p * rows, 16), rows), :]

    def send(a, dx, dy):
        tx, ty = (x + dx) % 2, (y + dy) % 2
        return pltpu.make_async_remote_copy(
            src_ref=slab(a, 4 * to_core + 2 * tx + ty), dst_ref=outs[a].at[me], send_sem=send_sems.at[a, 2 * dx + dy],
            recv_sem=recv_sems.at[a, 2 * (2 * dx + dy) + c], device_id=(tx, ty, to_core), device_id_type=MESH)

    def mine(a):
        return pltpu.make_async_copy(slab(a, 4 * to_core + 2 * x + y), outs[a].at[me], local_sems.at[a])

    def landed(a, dx, dy, sc):
        frm = 4 * ((x + dx) % 2) + 2 * ((y + dy) % 2) + sc
        return pltpu.make_async_remote_copy(
            src_ref=slab(a, 0), dst_ref=outs[a].at[frm], send_sem=send_sems.at[a, 0], recv_sem=recv_sems.at[a, 2 * (2 * dx + dy) + sc],
            device_id=(x, y, c), device_id_type=MESH)

    def start():
        for a in range(n):
            for dx, dy in others:
                send(a, dx, dy).start()
            pl.when(c == to_core)(mine(a).start)
            pl.when(c != to_core)(send(a, 0, 0).start)

    def finish():
        @pl.when(c == to_core)
        def _():
            for a in range(n):
                for dx, dy in [(0, 0)] + others:
                    for sc in (0, 1):
                        if (dx, dy, sc) != (0, 0, to_core):
                            landed(a, dx, dy, sc).wait_recv()
            for a in range(n):
                mine(a).wait()

        @pl.when(c != to_core)
        def _():
            for a in range(n):
                send(a, 0, 0).wait_send()

        for a in range(n):
            for dx, dy in others:
                send(a, dx, dy).wait_send()

    return start, finish


N_CHIP = 4


def _pair_then_chip_sems(n):
    return [pltpu.SemaphoreType.DMA((n, N_CHIP)) for _ in range(6)] + [pltpu.SemaphoreType.DMA((n,))]


def _pair_then_chip_ops(ins, pairs, outs, mine_v, pair_v, sum_v, pair_send, pair_recv, chip_send, chip_recv, load_a, load_b, own_sem):
    n = len(ins)
    x, y, c = _my_place()
    chip = 2 * x + y
    chips = [(0, 0), (0, 1), (1, 0), (1, 1)]
    others = [(0, 1), (1, 0), (1, 1)]

    def to_sibling(a, j):
        px, py = chips[j]
        return pltpu.make_async_remote_copy(
            src_ref=ins[a].at[4 * px + 2 * py + 1 - c], dst_ref=pairs[a].at[j], send_sem=pair_send.at[a, j], recv_sem=pair_recv.at[a, j],
            device_id=(x, y, 1 - c), device_id_type=MESH)

    def spread(a, dx, dy):
        tx, ty = (x + dx) % 2, (y + dy) % 2
        return pltpu.make_async_remote_copy(
            src_ref=sum_v[a].at[2 * tx + ty], dst_ref=outs[a].at[chip], send_sem=chip_send.at[a, 2 * dx + dy],
            recv_sem=chip_recv.at[a, 2 * dx + dy], device_id=(tx, ty, c), device_id_type=MESH)

    def landed(a, dx, dy):
        frm = 2 * ((x + dx) % 2) + (y + dy) % 2
        return pltpu.make_async_remote_copy(
            src_ref=sum_v[a].at[0], dst_ref=outs[a].at[frm], send_sem=chip_send.at[a, 0], recv_sem=chip_recv.at[a, 2 * dx + dy],
            device_id=(x, y, c), device_id_type=MESH)

    def own(a):
        return pltpu.make_async_copy(sum_v[a].at[chip], outs[a].at[chip], own_sem.at[a])

    def pair():
        loads = []
        for a in range(n):
            for j, (px, py) in enumerate(chips):
                to_sibling(a, j).start()
                loads.append(pltpu.make_async_copy(ins[a].at[4 * px + 2 * py + c], mine_v[a].at[j], load_a.at[a, j]))
                loads[-1].start()
        for a in range(n):
            for j in range(N_CHIP):
                to_sibling(a, j).wait_recv()
                loads.append(pltpu.make_async_copy(pairs[a].at[j], pair_v[a].at[j], load_b.at[a, j]))
                loads[-1].start()
        for cp in loads:
            cp.wait()
        for a in range(n):
            sum_v[a][...] = (mine_v[a][...].astype(F32) + pair_v[a][...].astype(F32)).astype(sum_v[a].dtype)

    def start():
        pair()
        for a in range(n):
            own(a).start()
            for dx, dy in others:
                spread(a, dx, dy).start()

    def finish():
        for a in range(n):
            for dx, dy in others:
                landed(a, dx, dy).wait_recv()
        for a in range(n):
            for dx, dy in others:
                spread(a, dx, dy).wait_send()
            for j in range(N_CHIP):
                to_sibling(a, j).wait_send()
            own(a).wait()

    return start, finish


def _window_sum(x, win, ahead):
    n = x.shape[0]
    span = 1
    while span < win:
        x = x + pltpu.roll(x, n - span if ahead else span, 0)
        span *= 2
    return x


def _conv_branch(z, ext_u, conv_ref, tm):
    c_w = z.shape[1] // 4
    b, c, v = z[:, :c_w], z[:, c_w:2 * c_w], z[:, 2 * c_w:3 * c_w]
    u = c * v
    ext_u[pl.ds(HALO, tm), :] = u
    u1 = ext_u[pl.ds(HALO - 1, tm), :]
    u2 = ext_u[pl.ds(HALO - 2, tm), :]
    yc = conv_ref[pl.ds(2, 1), :] * u + conv_ref[pl.ds(1, 1), :] * u1 + conv_ref[pl.ds(0, 1), :] * u2
    return b, c, v, u, u1, u2, yc


def _pool_branch(p, ext_p, pool_w_ref, tm):
    ext_p[pl.ds(HALO, tm), :] = p
    pooled, mixed = [], []
    for g, win in enumerate(POOL_WINDOWS):
        s = _window_sum(ext_p[:, pl.ds(POOL_GROUP * g, POOL_GROUP)], win, ahead=False)[HALO:HALO + tm, :]
        pooled.append((s * (1.0 / win) - p[:, POOL_GROUP * g:POOL_GROUP * (g + 1)]).astype(BF16))
        mixed.append(_dot(pooled[-1], pool_w_ref[g].astype(BF16)))
    return pooled, mixed


def _gather_and_mixer_forward(x2d, mixer_shards, ffn_shards, g1, pool_w, pool_scale, g2, n_seq):
    t, d = x2d.shape
    zs, rs, ms, cs = mixer_shards[0].shape[1], mixer_shards[1].shape[0], mixer_shards[2].shape[1], mixer_shards[3].shape[1]
    zw, cw = N_DEV * zs, N_DEV * cs
    s = t // n_seq
    tm = min(TM_MIX, s)
    nj = s // tm
    n1, n2 = len(mixer_shards), len(ffn_shards)
    dtypes = [BF16, BF16, F32, F32] + [BF16] * n2
    shards = list(mixer_shards) + list(ffn_shards)

    def body(x_ref, *rest):
        shard_refs, (g1_ref, pw_ref, ps_ref, g2_ref), rest = rest[:n1 + n2], rest[n1 + n2:n1 + n2 + 4], rest[n1 + n2 + 4:]
        (h1_ref, z_ref, m_ref, pooled_ref, mixed_ref, win_o, wout_o, meta_o, conv_o, am_o, zm_o), rest = rest[:11], rest[11:]
        slabs, rest = rest[:n1 + n2], rest[n1 + n2:]
        stages, rest = rest[:n1 + n2], rest[n1 + n2:]
        win_v, wout_v, meta_v, conv_v, ext_u, ext_p, sem = rest[:7]
        first = _gather_ops(stages[:n1], slabs[:n1], *rest[7:10])
        later = _gather_ops(stages[n1:], slabs[n1:], *rest[10:13], core_major=True)

        @pl.when((pl.program_id(0) == 0) & (pl.program_id(1) == 0))
        def _():
            for src, dst in zip(shard_refs, stages):
                dst[...] = src[...].astype(dst.dtype)
            first[0]()
            later[0]()
            first[1]()
            first[2]()
            copies = [pltpu.make_async_copy(slabs[0].at[i], win_v.at[:, pl.ds(zs * i, zs)], sem.at[i]) for i in range(N_DEV)]
            copies += [pltpu.make_async_copy(slabs[1].at[i], wout_v.at[pl.ds(rs * i, rs), :], sem.at[N_DEV + i]) for i in range(N_DEV)]
            copies += [pltpu.make_async_copy(slabs[2], meta_v, sem.at[2 * N_DEV]), pltpu.make_async_copy(slabs[3], conv_v, sem.at[2 * N_DEV + 1])]
            for cp in copies:
                cp.start()
            for cp in copies:
                cp.wait()
            copies = [pltpu.make_async_copy(win_v, win_o, sem.at[0]), pltpu.make_async_copy(wout_v, wout_o, sem.at[1])]
            for cp in copies:
                cp.start()
            for i in range(N_DEV):
                meta_o[:, pl.ds(ms * i, ms)] = meta_v[i]
                conv_o[:, pl.ds(cs * i, cs)] = conv_v[i]
            hat, _ = _rms_stats(meta_o[...])
            a = (hat * g1_ref[...]).astype(BF16)
            am_o[...] = a
            zm_o[...] = _dot(a, win_v[...])
            for cp in copies:
                cp.wait()

        @pl.when(pl.program_id(1) == 0)
        def _():
            zm = zm_o[...]
            ext_u[pl.ds(0, HALO), :] = zm[:, cw:2 * cw] * zm[:, 2 * cw:3 * cw]
            ext_p[pl.ds(0, HALO), :] = zm[:, 3 * cw:]

        h0 = x_ref[...]
        hat, _ = _rms_stats(h0)
        z = _dot((hat * g1_ref[...]).astype(BF16), win_v[...])
        z_ref[...] = z.astype(BF16)
        b, _, _, _, _, _, yc = _conv_branch(z, ext_u, conv_o, tm)
        pooled, mixed = _pool_branch(z[:, 3 * cw:], ext_p, pw_ref, tm)
        pooled_ref[...] = jnp.concatenate(pooled, axis=1)
        mixed_ref[...] = jnp.concatenate(mixed, axis=1).astype(BF16)
        ps = ps_ref[...]
        y = [b * yc] + [mixed[g] * ps[:, POOL_GROUP * g:POOL_GROUP * (g + 1)] for g in range(len(POOL_WINDOWS))]
        m = _dot(jnp.concatenate(y, axis=1).astype(BF16), wout_v[...])
        m_ref[...] = m
        m_hat, _ = _rms_stats(m)
        h1_ref[...] = h0 + m_hat * g2_ref[...]
        ext_u[pl.ds(0, HALO), :] = ext_u[pl.ds(tm, HALO), :]
        ext_p[pl.ds(0, HALO), :] = ext_p[pl.ds(tm, HALO), :]

        @pl.when((pl.program_id(0) == n_seq - 1) & (pl.program_id(1) == nj - 1))
        def _():
            later[1]()
            later[2]()

    row = lambda b, j: (b * nj + j, 0)
    vmem = pl.BlockSpec(memory_space=pltpu.VMEM)
    small = [(N_META, d), (CONV_WIDTH, cw), (N_META, d), (N_META, zw)]
    out = pl.pallas_call(
        body, name="gather_and_mixer_forward", grid=(n_seq, nj),
        in_specs=[pl.BlockSpec((tm, d), row)] + [vmem] * (n1 + n2)
        + [_const(g1.shape), _const(pool_w.shape), _const(pool_scale.shape), _const(g2.shape)],
        out_specs=[pl.BlockSpec((tm, d), row), pl.BlockSpec((tm, zw), row), pl.BlockSpec((tm, d), row), pl.BlockSpec((tm, cw), row),
                   pl.BlockSpec((tm, cw), row), ANY, ANY] + [_const(sh) for sh in small] + [ANY] * (n1 + n2),
        out_shape=[jax.ShapeDtypeStruct((t, d), F32), jax.ShapeDtypeStruct((t, zw), BF16), jax.ShapeDtypeStruct((t, d), F32),
                   jax.ShapeDtypeStruct((t, cw), BF16), jax.ShapeDtypeStruct((t, cw), BF16),
                   jax.ShapeDtypeStruct((d, zw), BF16), jax.ShapeDtypeStruct((d, d), BF16),
                   jax.ShapeDtypeStruct(small[0], F32), jax.ShapeDtypeStruct(small[1], F32), jax.ShapeDtypeStruct(small[2], BF16),
                   jax.ShapeDtypeStruct(small[3], F32)]
        + [jax.ShapeDtypeStruct((N_DEV, *a.shape), dt) for a, dt in zip(shards, dtypes)],
        scratch_shapes=[pltpu.VMEM(a.shape, dt) for a, dt in zip(shards, dtypes)]
        + [pltpu.VMEM((d, zw), BF16), pltpu.VMEM((d, d), BF16), pltpu.VMEM((N_DEV, N_META, ms), F32),
           pltpu.VMEM((N_DEV, CONV_WIDTH, cs), F32), pltpu.VMEM((tm + HALO, cw), F32), pltpu.VMEM((tm + HALO, cw), F32),
           pltpu.SemaphoreType.DMA((2 * N_DEV + 2,))] + _exchange_sems(n1) + _exchange_sems(n2),
        compiler_params=_params("arbitrary", "arbitrary"),
    )(x2d, *shards, g1, pool_w, pool_scale, g2)
    return out[:5], out[5:11], out[11 + n1:]


def _mixer_backward(x2d, dh1, m, z, pooled, mixed, meta, a_meta, z_meta, g1, w_in, conv_w, pool_w, pool_scale, w_out, g2, n_seq,
                    to_exchange, landing):
    t, d = x2d.shape
    zw = w_in.shape[1]
    cw = zw // 4
    s = t // n_seq
    tm = min(TM_MIX, s)
    nj = s // tm
    n_groups = len(POOL_WINDOWS)
    zs = zw // N_DEV
    nx = len(to_exchange)
    n_in = 17
    given = [k for k, a in enumerate(landing) if a is not None]
    fresh = [k for k, a in enumerate(landing) if a is None]

    def body(x_ref, dh1_ref, m_ref, z_ref, zprev_ref, pooled_ref, mixed_ref, meta_ref, am_ref, zm_ref, g1_ref, win_ref, conv_ref, pw_ref, ps_ref, wout_ref,
             g2_ref, *rest):
        sent, rest = rest[:nx], rest[nx + len(given):]
        gx_ref, dwin_ref, dwout_ref, dg1_ref, dg2_ref, dconv_ref, dpw_ref, dps_ref, dmeta_ref = rest[:9]
        landed, rest = rest[9:9 + nx], rest[9 + nx:]
        ext_u, ext_dyc, ext_dq, acc_win, acc_wout, dz_meta, stage16, sem = rest[:8]
        north = _core_exchange_ops(sent, landed, 1, *rest[8:11])
        south = _core_exchange_ops([sent[k] for k in fresh], [landed[k] for k in fresh], 0, *rest[11:14])

        def start():
            north[0]()
            south[0]()

        def finish():
            south[1]()
            north[1]()

        b_id, j = pl.program_id(0), pl.program_id(1)
        jr = nj - 1 - j
        pl.when((b_id == 0) & (j == 0))(start)

        @pl.when((b_id == 0) & (j == 0))
        def _():
            acc_win[...] = jnp.zeros_like(acc_win)
            acc_wout[...] = jnp.zeros_like(acc_wout)
            dz_meta[...] = jnp.zeros_like(dz_meta)
            for r in (dg1_ref, dg2_ref, dconv_ref, dpw_ref, dps_ref, dmeta_ref):
                r[...] = jnp.zeros_like(r)

        @pl.when(j == 0)
        def _():
            ext_dyc[pl.ds(tm, HALO), :] = jnp.zeros((HALO, cw), F32)
            ext_dq[pl.ds(tm, HALO), :] = jnp.zeros((HALO, cw), F32)

        zm = zm_ref[...]
        halo = jnp.where(jr == 0, zm, zprev_ref[...].astype(F32))
        ext_u[pl.ds(0, HALO), :] = halo[:, cw:2 * cw] * halo[:, 2 * cw:3 * cw]

        dh1v = dh1_ref[...]
        m_hat, m_rstd = _rms_stats(m_ref[...])
        dm, dg2 = _rms_bwd(m_hat, m_rstd, g2_ref[...], dh1v)
        dg2_ref[...] += dg2
        dm = dm.astype(BF16)
        dycat = _dot_nt(dm, wout_ref[...])

        b, c, v, u, u1, u2, yc = _conv_branch(z_ref[...].astype(F32), ext_u, conv_ref, tm)
        mixed = [mixed_ref[:, pl.ds(POOL_GROUP * g, POOL_GROUP)].astype(F32) for g in range(n_groups)]
        ps = ps_ref[...]
        y = [b * yc] + [mixed[g] * ps[:, POOL_GROUP * g:POOL_GROUP * (g + 1)] for g in range(n_groups)]
        ycat = jnp.concatenate(y, axis=1).astype(BF16)
        acc_wout[...] += _dot_tn(ycat, dm)

        dyconv = dycat[:, :cw]
        db = dyconv * yc
        dyc = dyconv * b
        ext_dyc[pl.ds(0, tm), :] = dyc
        du = (conv_ref[pl.ds(2, 1), :] * dyc + conv_ref[pl.ds(1, 1), :] * ext_dyc[pl.ds(1, tm), :]
              + conv_ref[pl.ds(0, 1), :] * ext_dyc[pl.ds(2, tm), :])
        dconv_ref[pl.ds(2, 1), :] += jnp.sum(dyc * u, axis=0, keepdims=True)
        dconv_ref[pl.ds(1, 1), :] += jnp.sum(dyc * u1, axis=0, keepdims=True)
        dconv_ref[pl.ds(0, 1), :] += jnp.sum(dyc * u2, axis=0, keepdims=True)

        dp = []
        for g, win in enumerate(POOL_WINDOWS):
            lanes = pl.ds(POOL_GROUP * g, POOL_GROUP)
            dypool = dycat[:, cw + POOL_GROUP * g:cw + POOL_GROUP * (g + 1)]
            dps_ref[:, lanes] += jnp.sum(dypool * mixed[g], axis=0, keepdims=True)
            dmixed = (dypool * ps[:, POOL_GROUP * g:POOL_GROUP * (g + 1)]).astype(BF16)
            dq = _dot_nt(dmixed, pw_ref[g].astype(BF16))
            dpw_ref[g] += _dot_tn(pooled_ref[:, lanes], dmixed)
            ext_dq[pl.ds(0, tm), lanes] = dq
            acc = _window_sum(ext_dq[:, lanes], win, ahead=True)[0:tm, :]
            dp.append(acc * (1.0 / win) - dq)

        dz = jnp.concatenate([db, du * v, du * c] + dp, axis=1).astype(BF16)
        da = _dot_nt(dz, win_ref[...])
        h0 = x_ref[...]
        hat0, rstd0 = _rms_stats(h0)
        g1 = g1_ref[...]
        acc_win[...] += _dot_tn((hat0 * g1).astype(BF16), dz)
        dh0, dg1 = _rms_bwd(hat0, rstd0, g1, da)
        dg1_ref[...] += dg1
        gx_ref[...] = dh1v + dh0

        ext_dyc[pl.ds(tm, HALO), :] = ext_dyc[pl.ds(0, HALO), :]
        ext_dq[pl.ds(tm, HALO), :] = ext_dq[pl.ds(0, HALO), :]

        @pl.when(jr == 0)
        def _():
            ext_dyc[pl.ds(tm - HALO, HALO), :] = jnp.zeros((HALO, cw), F32)
            ext_dq[pl.ds(tm - HALO, HALO), :] = jnp.zeros((HALO, cw), F32)
            du_m = (conv_ref[pl.ds(1, 1), :] * ext_dyc[pl.ds(tm - HALO + 1, HALO), :]
                    + conv_ref[pl.ds(0, 1), :] * ext_dyc[pl.ds(tm - HALO + 2, HALO), :])
            dp_m = []
            for g, win in enumerate(POOL_WINDOWS):
                lanes = pl.ds(POOL_GROUP * g, POOL_GROUP)
                acc = ext_dq[pl.ds(tm - HALO + 1, HALO), lanes]
                for k in range(2, win):
                    acc = acc + ext_dq[pl.ds(tm - HALO + k, HALO), lanes]
                dp_m.append(acc * (1.0 / win))
            dz_meta[...] += jnp.concatenate(
                [jnp.zeros((HALO, cw), F32), du_m * zm[:, 2 * cw:3 * cw], du_m * zm[:, cw:2 * cw]] + dp_m, axis=1)

        @pl.when((b_id == n_seq - 1) & (j == nj - 1))
        def _():
            dz_m = dz_meta[...].astype(BF16)
            acc_win[...] += _dot_tn(am_ref[...], dz_m)
            hat_m, rstd_m = _rms_stats(meta_ref[...])
            dmeta, dg1_m = _rms_bwd(hat_m, rstd_m, g1, _dot_nt(dz_m, win_ref[...]))
            dg1_ref[...] += dg1_m
            dmeta_ref[...] = dmeta
            pieces = [(acc_win, zs * i, dwin_ref.at[i]) for i in range(N_DEV)]
            pieces += [(acc_wout, zs * i, dwout_ref.at[:, pl.ds(zs * i, zs)]) for i in range(d // zs)]
            copies = []
            for k, (acc, col, dst) in enumerate(pieces):
                if k >= 2:
                    copies[k - 2].wait()
                stage16[k % 2] = acc[:, pl.ds(col, zs)].astype(BF16)
                copies.append(pltpu.make_async_copy(stage16.at[k % 2], dst, sem.at[k % 2]))
                copies[k].start()
            copies[-2].wait()
            copies[-1].wait()
            finish()

    row = lambda b, j: (b * nj + nj - 1 - j, 0)
    prev = lambda b, j: (jnp.maximum((b * s + (nj - 1 - j) * tm) // HALO - 1, 0), 0)
    small = [g1.shape, g2.shape, conv_w.shape, pool_w.shape, pool_scale.shape, meta.shape]
    out = pl.pallas_call(
        body, name="mixer_backward", grid=(n_seq, nj),
        in_specs=[pl.BlockSpec((tm, d), row), pl.BlockSpec((tm, d), row), pl.BlockSpec((tm, d), row), pl.BlockSpec((tm, zw), row),
                  pl.BlockSpec((HALO, zw), prev), pl.BlockSpec((tm, cw), row), pl.BlockSpec((tm, cw), row), _const(meta.shape), _const(a_meta.shape), _const(z_meta.shape), _const(g1.shape),
                  _resident(w_in.shape), _const(conv_w.shape), _const(pool_w.shape), _const(pool_scale.shape), _resident(w_out.shape),
                  _const(g2.shape)] + [ANY] * (nx + len(given)),
        out_specs=[pl.BlockSpec((tm, d), row), ANY, ANY] + [_const(sh) for sh in small] + [ANY] * nx,
        out_shape=[jax.ShapeDtypeStruct((t, d), F32), jax.ShapeDtypeStruct((N_DEV, d, zs), BF16),
                   jax.ShapeDtypeStruct(w_out.shape, BF16)] + [jax.ShapeDtypeStruct(sh, F32) for sh in small]
        + [jax.ShapeDtypeStruct((N_DEV, a.shape[0] // N_DEV, a.shape[1]), a.dtype) for a in to_exchange],
        input_output_aliases={n_in + nx + at: 9 + k for at, k in enumerate(given)},
        scratch_shapes=[pltpu.VMEM((tm + HALO, cw), F32)] * 3
        + [pltpu.VMEM(w_in.shape, F32), pltpu.VMEM(w_out.shape, F32), pltpu.VMEM((HALO, zw), F32), pltpu.VMEM((2, d, zs), BF16),
           pltpu.SemaphoreType.DMA((2,))] + _core_exchange_sems(nx) + _core_exchange_sems(len(fresh)),
        compiler_params=_params("arbitrary", "arbitrary"),
    )(x2d, dh1, m, z, z, pooled, mixed, meta, a_meta, z_meta, g1, w_in, conv_w, pool_w, pool_scale, w_out, g2, *to_exchange, *[landing[k] for k in given])
    return out[:9], out[9:]


def _ffn_forward_backward(h1, target, g3, w_gate, w_up, w_down, g4):
    t, d = h1.shape
    ff = w_gate.shape[0]
    tm = min(TM_FFN, t)
    nt = t // tm
    chunks = [(s, min(FFN_CHUNK, ff - s)) for s in range(0, ff, FFN_CHUNK)]

    def body(h1_ref, h1pp_ref, tgt_ref, g3_ref, wg_ref, wu_ref, wd_ref, g4_ref,
             f_ref, act_ref, dd_ref, dgate_ref, dup_ref, dh1_ref, loss_ref, dg3_ref, dg4_ref, *slots):
        gate_s, up_s, dd_s, dh2_s, df_s = slots
        i = pl.program_id(0)

        def forward(slot):
            h1v = h1_ref[...]
            hat, _ = _rms_stats(h1v)
            f = (hat * g3_ref[...]).astype(BF16)
            f_ref[...] = f
            s, n = chunks[0]
            gate, up = _dot_nt(f_ref[...], wg_ref[pl.ds(s, n), :]), _dot_nt(f_ref[...], wu_ref[pl.ds(s, n), :])
            yield
            down = None
            for k, (s, n) in enumerate(chunks):
                gate_s.at[slot][:, pl.ds(s, n)] = gate.astype(BF16)
                up_s.at[slot][:, pl.ds(s, n)] = up.astype(BF16)
                act = (gate * jax.nn.sigmoid(gate) * up).astype(BF16)
                act_ref[:, pl.ds(s, n)] = act
                if k + 1 < len(chunks):
                    s1, n1 = chunks[k + 1]
                    gate, up = _dot_nt(f_ref[...], wg_ref[pl.ds(s1, n1), :]), _dot_nt(f_ref[...], wu_ref[pl.ds(s1, n1), :])
                yield
                part = _dot(act_ref[:, pl.ds(s, n)], wd_ref[pl.ds(s, n), :])
                down = part if down is None else down + part
                yield
            d_hat, d_rstd = _rms_stats(down)
            g4 = g4_ref[...]
            err = h1v + d_hat * g4 - tgt_ref[...]
            loss_ref[...] += jnp.sum(err * err) * (0.5 / d)
            dh2 = err * (1.0 / d)
            dh2_s.at[slot][...] = dh2
            dd, dg4 = _rms_bwd(d_hat, d_rstd, g4, dh2)
            dg4_ref[...] += dg4
            dd = dd.astype(BF16)
            dd_ref[...] = dd
            dd_s.at[slot][...] = dd

        def backward(slot):
            s, n = chunks[0]
            dact = _dot_nt(dd_s.at[slot][...], wd_ref[pl.ds(s, n), :])
            yield
            df = None
            for k, (s, n) in enumerate(chunks):
                gate = gate_s.at[slot][:, pl.ds(s, n)].astype(F32)
                up = up_s.at[slot][:, pl.ds(s, n)].astype(F32)
                sig = jax.nn.sigmoid(gate)
                dup = (dact * (gate * sig)).astype(BF16)
                dgate = (dact * up * (sig * (1.0 + gate * (1.0 - sig)))).astype(BF16)
                dup_ref[:, pl.ds(s, n)] = dup
                dgate_ref[:, pl.ds(s, n)] = dgate
                if k + 1 < len(chunks):
                    s1, n1 = chunks[k + 1]
                    dact = _dot_nt(dd_s.at[slot][...], wd_ref[pl.ds(s1, n1), :])
                yield
                part = _dot(dgate_ref[:, pl.ds(s, n)], wg_ref[pl.ds(s, n), :]) + _dot(dup_ref[:, pl.ds(s, n)], wu_ref[pl.ds(s, n), :])
                df = part if df is None else df + part
                yield
            df_s.at[slot][...] = df

        def last(slot):
            hat, rstd = _rms_stats(h1pp_ref[...])
            dh1, dg3 = _rms_bwd(hat, rstd, g3_ref[...], df_s.at[slot][...])
            dg3_ref[...] += dg3
            dh1_ref[...] = dh2_s.at[slot][...] + dh1

        def emit(parity, with_forward, with_backward, with_last):
            fwd = forward(parity) if with_forward else iter(())
            bwd = backward(1 - parity) if with_backward else iter(())
            next(fwd, None)
            if with_last:
                last(parity)
            for _ in range(FFN_BACKWARD_LAG):
                next(fwd, None)
            alive = True
            while alive:
                alive = next(bwd, True) is None
                alive = (next(fwd, True) is None) or alive

        @pl.when(i == 0)
        def _():
            for r in (loss_ref, dg3_ref, dg4_ref, *slots):
                r[...] = jnp.zeros_like(r)
            emit(0, True, False, False)

        @pl.when((i > 0) & (i < nt))
        def _():
            emit(i % 2, True, True, True)

        @pl.when(i == nt)
        def _():
            emit(nt % 2, False, True, True)

        @pl.when(i == nt + 1)
        def _():
            emit((nt + 1) % 2, False, False, True)

    cur = lambda i: (jnp.minimum(i, nt - 1), 0)
    prev = lambda i: (jnp.clip(i - 1, 0, nt - 1), 0)
    prev2 = lambda i: (jnp.clip(i - 2, 0, nt - 1), 0)
    return pl.pallas_call(
        body, name="ffn_forward_backward", grid=(nt + 2,),
        in_specs=[pl.BlockSpec((tm, d), cur), pl.BlockSpec((tm, d), prev2), pl.BlockSpec((tm, d), cur), _const(g3.shape),
                  _resident(w_gate.shape), _resident(w_up.shape), _resident(w_down.shape), _const(g4.shape)],
        out_specs=[pl.BlockSpec((tm, d), cur), pl.BlockSpec((tm, ff), cur), pl.BlockSpec((tm, d), cur), pl.BlockSpec((tm, ff), prev),
                   pl.BlockSpec((tm, ff), prev), pl.BlockSpec((tm, d), prev2), _const((8, 128)), _const(g3.shape), _const(g4.shape)],
        out_shape=[jax.ShapeDtypeStruct((t, d), BF16), jax.ShapeDtypeStruct((t, ff), BF16), jax.ShapeDtypeStruct((t, d), BF16),
                   jax.ShapeDtypeStruct((t, ff), BF16), jax.ShapeDtypeStruct((t, ff), BF16), jax.ShapeDtypeStruct((t, d), F32),
                   jax.ShapeDtypeStruct((8, 128), F32), jax.ShapeDtypeStruct(g3.shape, F32), jax.ShapeDtypeStruct(g4.shape, F32)],
        scratch_shapes=[pltpu.VMEM((2, tm, ff), BF16)] * 2 + [pltpu.VMEM((2, tm, d), BF16)] + [pltpu.VMEM((2, tm, d), F32)] * 2,
        compiler_params=_params("arbitrary"),
    )(h1, h1, target, g3, w_gate, w_up, w_down, g4)


def _ffn_weight_grads(f, dd, dgate, dup, act):
    t, d = f.shape
    ff = dgate.shape[1]
    tm = min(TM_WGRAD, t)
    nt = t // tm
    fc = ff // FF_CHUNKS
    assert FF_CHUNKS == 2

    def body(f_ref, dd_ref, dgate_ref, dup_ref, act_ref, dwg_ref, dwu_ref, dwd_ref, *rest):
        landing, (acc_g, acc_u, acc_d, stage, sem) = rest[:2], rest[2:7]
        start, finish = _core_exchange_ops([dwg_ref, dwd_ref], landing, 0, *rest[7:])
        c, i = pl.program_id(0), pl.program_id(1)
        pl.when((c == 1) & (i == 0))(start)

        @pl.when(i == 0)
        def _():
            acc_g[...] = jnp.zeros_like(acc_g)
            acc_u[...] = jnp.zeros_like(acc_u)
            acc_d[...] = jnp.zeros_like(acc_d)

        fv = f_ref[...]
        acc_g[...] += _dot_tn(fv, dgate_ref[...])
        acc_u[...] += _dot_tn(fv, dup_ref[...])
        acc_d[...] += _dot_tn(act_ref[...], dd_ref[...])

        @pl.when(i == nt - 1)
        def _():
            rows = pl.ds(pl.multiple_of(c * fc, 16), fc)
            copies = []
            for k, (acc, out, transposed) in enumerate(((acc_d, dwd_ref, False), (acc_g, dwg_ref, True), (acc_u, dwu_ref, True))):
                if k >= 2:
                    copies[k - 2].wait()
                stage[k % 2] = (acc[...].T if transposed else acc[...]).astype(BF16)
                copies.append(pltpu.make_async_copy(stage.at[k % 2], out.at[rows, :], sem.at[k % 2]))
                copies[k].start()
            copies[-2].wait()
            copies[-1].wait()

        pl.when((c == 1) & (i == nt - 1))(finish)

    row = lambda c, i: (i, 0)
    col = lambda c, i: (i, c)
    out = pl.pallas_call(
        body, name="ffn_weight_grads", grid=(FF_CHUNKS, nt),
        in_specs=[pl.BlockSpec((tm, d), row), pl.BlockSpec((tm, d), row), pl.BlockSpec((tm, fc), col), pl.BlockSpec((tm, fc), col),
                  pl.BlockSpec((tm, fc), col)],
        out_specs=[ANY] * 5,
        out_shape=[jax.ShapeDtypeStruct((ff, d), BF16)] * 3 + [jax.ShapeDtypeStruct((N_DEV, ff // N_DEV, d), BF16)] * 2,
        scratch_shapes=[pltpu.VMEM((d, fc), F32), pltpu.VMEM((d, fc), F32), pltpu.VMEM((fc, d), F32), pltpu.VMEM((2, fc, d), BF16),
                        pltpu.SemaphoreType.DMA((2,))] + _core_exchange_sems(2),
        compiler_params=_params("arbitrary", "arbitrary"),
    )(f, dd, dgate, dup, act)
    return out[:3], [out[3], None, out[4]]


def _adamw(w, g, m, v):
    m = ADAM_B1 * m + (1.0 - ADAM_B1) * g
    v = ADAM_B2 * v + (1.0 - ADAM_B2) * (g * g)
    m_hat = m / (1.0 - ADAM_B1 ** ADAM_STEP)
    v_hat = v / (1.0 - ADAM_B2 ** ADAM_STEP)
    return -ADAM_LR * (m_hat / (jnp.sqrt(v_hat) + ADAM_EPS) + ADAM_WD * w), m, v


def _sum_slabs(ref):
    total = ref[0].astype(F32)
    for i in range(1, ref.shape[0]):
        total = total + ref[i].astype(F32)
    return total


def _adamw_rows(r, c):
    tr = r
    for cand in range(8, r, 8):
        if r % cand == 0 and cand * c <= ADAMW_BLOCK_ELEMS:
            tr = cand
    return r if r * c <= ADAMW_BLOCK_ELEMS else tr


def _reduce_adamw_carrying(parts, ws, ms, vs, to_reduce, to_exchange, whole, name):
    k, nr, nx = len(ws), len(to_reduce), len(to_exchange)
    r, c = ws[0].shape if k else (8, 128)
    tr = _adamw_rows(r, c)
    steps = r // tr
    travels = nr + nx > 0
    chip_slabs = [jax.ShapeDtypeStruct((N_CHIP, *a.shape[1:]), a.dtype) for a in to_reduce]

    def body(*refs):
        p_refs, w_refs, m_refs, v_refs = (refs[a * k:(a + 1) * k] for a in range(4))
        refs = refs[4 * k:]
        reduced_in, sent, refs = refs[:nr], refs[nr:nr + nx], refs[nr + nx:]
        outs, pairs, sums, landed, refs = refs[:4 * k], refs[4 * k:4 * k + nr], refs[4 * k + nr:4 * k + 2 * nr], \
            refs[4 * k + 2 * nr:4 * k + 2 * nr + nx], refs[4 * k + 2 * nr + nx:]
        mine_v, pair_v, sum_v, refs = refs[:nr], refs[nr:2 * nr], refs[2 * nr:3 * nr], refs[3 * nr:]
        if travels:
            reduce_ops = _pair_then_chip_ops(reduced_in, pairs, sums, mine_v, pair_v, sum_v, *refs[:7])
            direct_ops = _exchange_ops(sent, landed, whole, *refs[7:])

            @pl.when(pl.program_id(0) == 0)
            def _():
                direct_ops[0]()
                reduce_ops[0]()

        for a in range(k):
            g = _sum_slabs(p_refs[a])
            outs[4 * a][...] = g
            outs[4 * a + 1][...], outs[4 * a + 2][...], outs[4 * a + 3][...] = _adamw(w_refs[a][...], g, m_refs[a][...], v_refs[a][...])

        if travels:
            @pl.when(pl.program_id(0) == steps - 1)
            def _():
                reduce_ops[1]()
                direct_ops[1]()

    blk = pl.BlockSpec((tr, c), lambda i: (i, 0))
    out = pl.pallas_call(
        body, name=name, grid=(steps,),
        in_specs=[pl.BlockSpec((N_DEV, tr, c), lambda i: (0, i, 0))] * k + [blk] * (3 * k) + [ANY] * (nr + nx),
        out_specs=[blk] * (4 * k) + [ANY] * (2 * nr + nx),
        out_shape=[jax.ShapeDtypeStruct((r, c), F32)] * (4 * k) + chip_slabs + chip_slabs
        + [jax.ShapeDtypeStruct((N_DEV, *a.shape) if w else a.shape, a.dtype) for a, w in zip(to_exchange, whole)],
        scratch_shapes=([pltpu.VMEM(a.shape, a.dtype) for a in chip_slabs] * 3 + _pair_then_chip_sems(nr) + _exchange_sems(nx)
                        if travels else []),
        compiler_params=_params("arbitrary"),
    )(*parts, *ws, *ms, *vs, *to_reduce, *to_exchange)
    return [tuple(out[4 * a:4 * a + 4]) for a in range(k)], out[4 * k + nr:4 * k + 2 * nr], out[4 * k + 2 * nr:]


def _reduce_adamw(parts, w, m, v, name):
    r, c = w.shape
    tr = _adamw_rows(r, c)

    def body(p_ref, w_ref, m_ref, v_ref, g_out, d_out, m_out, v_out):
        g = _sum_slabs(p_ref)
        g_out[...] = g
        d_out[...], m_out[...], v_out[...] = _adamw(w_ref[...], g, m_ref[...], v_ref[...])

    blk = pl.BlockSpec((tr, c), lambda i: (i, 0))
    return pl.pallas_call(
        body, name=name, grid=(r // tr,),
        in_specs=[pl.BlockSpec((parts.shape[0], tr, c), lambda i: (0, i, 0)), blk, blk, blk],
        out_specs=[blk] * 4, out_shape=[jax.ShapeDtypeStruct((r, c), F32)] * 4,
        compiler_params=_params("arbitrary"),
    )(parts, w, m, v)


def _reduce_adamw_small(parts, ws, ms, vs, loss_parts):
    n = len(parts)

    def body(*refs):
        p_refs, w_refs, m_refs, v_refs = (refs[k * n:(k + 1) * n] for k in range(4))
        outs = refs[4 * n + 1:]
        outs[4 * n][...] = _sum_slabs(refs[4 * n])
        for a in range(n):
            g = _sum_slabs(p_refs[a])
            outs[4 * a][...] = g
            outs[4 * a + 1][...], outs[4 * a + 2][...], outs[4 * a + 3][...] = _adamw(w_refs[a][...], g, m_refs[a][...], v_refs[a][...])

    out = pl.pallas_call(
        body, name="adamw_replicated",
        out_shape=[jax.ShapeDtypeStruct(w.shape, F32) for w in ws for _ in range(4)] + [jax.ShapeDtypeStruct(loss_parts.shape[1:], F32)],
        compiler_params=pltpu.CompilerParams(vmem_limit_bytes=VMEM_LIMIT_BYTES),
    )(*parts, *ws, *ms, *vs, loss_parts)
    return [tuple(out[4 * a:4 * a + 4]) for a in range(n)], out[4 * n]


def kernel(x, meta_tokens, norm_mix_pre, w_in, conv_w, pool_w, pool_scale, w_out, norm_mix_post, norm_ffn_pre, w_gate, w_up, w_down, norm_ffn_post, loss_target, m_meta_tokens, m_norm_mix_pre, m_w_in, m_conv_w, m_pool_w, m_pool_scale, m_w_out, m_norm_mix_post, m_norm_ffn_pre, m_w_gate, m_w_up, m_w_down, m_norm_ffn_post, v_meta_tokens, v_norm_mix_pre, v_w_in, v_conv_w, v_pool_w, v_pool_scale, v_w_out, v_norm_mix_post, v_norm_ffn_pre, v_w_gate, v_w_up, v_w_down, v_norm_ffn_post):
    n_seq, seq, d = x.shape
    x2d = x.reshape(n_seq * seq, d)
    target = loss_target.reshape(n_seq * seq, d)

    t_ = lambda a: jnp.swapaxes(a[0], 0, 1)
    pw, ps = pool_w[0], pool_scale

    (h1, z, m, pooled, mixed), (win_b, wout_b, meta, conv, a_meta, z_meta), ffn_slabs = _gather_and_mixer_forward(
        x2d, [w_in[0], w_out[0], meta_tokens, conv_w[0]], [t_(w_gate), t_(w_up), w_down[0]], norm_mix_pre, pw, ps, norm_mix_post, n_seq)
    wg_b, wu_b, wd_b = (s.reshape(-1, d) for s in ffn_slabs)
    f, act, dd, dgate, dup, dh1, loss_sum, dg3, dg4 = _ffn_forward_backward(h1, target, norm_ffn_pre, wg_b, wu_b, wd_b, norm_ffn_post)
    ffn_grads, landing = _ffn_weight_grads(f, dd, dgate, dup, act)
    (gx, dwin, dwout, dg1, dg2, dconv, dpw, dps, dmeta), ffn_parts = _mixer_backward(
        x2d, dh1, m, z, pooled, mixed, meta, a_meta, z_meta, norm_mix_pre, win_b, conv, pw, ps, wout_b, norm_mix_post, n_seq,
        ffn_grads, landing)

    dmeta_s = jnp.transpose(dmeta.reshape(N_META, N_DEV, -1), (1, 0, 2))
    dconv_s = jnp.transpose(dconv.reshape(CONV_WIDTH, N_DEV, -1), (1, 0, 2))
    _, (win_parts, wout_parts), last = _reduce_adamw_carrying(
        [], [], [], [], [dwin, dwout.reshape(N_DEV, -1, d)], [dmeta_s, dconv_s, dg1, dg2, dg3, dg4, dpw, dps, loss_sum],
        [False] * 2 + [True] * 7, "exchange_rest")
    ffn_res, _, _ = _reduce_adamw_carrying(
        ffn_parts, [t_(w_gate), t_(w_up), w_down[0]], [t_(m_w_gate), t_(m_w_up), m_w_down[0]], [t_(v_w_gate), t_(v_w_up), v_w_down[0]],
        [], [], [], "adamw_ffn")
    replicated = last[2:8]

    names = ["meta_tokens", "norm_mix_pre", "w_in", "conv_w", "pool_w", "pool_scale", "w_out", "norm_mix_post", "norm_ffn_pre", "w_gate",
             "w_up", "w_down", "norm_ffn_post"]
    res = {"w_gate": tuple(jnp.swapaxes(o, 0, 1)[None] for o in ffn_res[0]),
           "w_up": tuple(jnp.swapaxes(o, 0, 1)[None] for o in ffn_res[1]), "w_down": tuple(o[None] for o in ffn_res[2])}
    for nm, parts, w, m_, v_ in (("w_in", win_parts, w_in, m_w_in, v_w_in), ("w_out", wout_parts, w_out, m_w_out, v_w_out),
                                 ("conv_w", last[1], conv_w, m_conv_w, v_conv_w)):
        res[nm] = tuple(o[None] for o in _reduce_adamw(parts, w[0], m_[0], v_[0], "adamw_" + nm))
    res["meta_tokens"] = tuple(_reduce_adamw(last[0], meta_tokens, m_meta_tokens, v_meta_tokens, "adamw_meta_tokens"))
    small, loss = _reduce_adamw_small(
        replicated, [norm_mix_pre, norm_mix_post, norm_ffn_pre, norm_ffn_post, pool_w[0], pool_scale],
        [m_norm_mix_pre, m_norm_mix_post, m_norm_ffn_pre, m_norm_ffn_post, m_pool_w[0], m_pool_scale],
        [v_norm_mix_pre, v_norm_mix_post, v_norm_ffn_pre, v_norm_ffn_post, v_pool_w[0], v_pool_scale], last[8])
    for nm, r in zip(["norm_mix_pre", "norm_mix_post", "norm_ffn_pre", "norm_ffn_post", "pool_w", "pool_scale"], small):
        res[nm] = tuple(o[None] for o in r) if nm == "pool_w" else r

    return (loss[0, 0], gx.reshape(n_seq, seq, d), *[res[nm][0] for nm in names], *[res[nm][1] for nm in names],
            *[res[nm][2] for nm in names], *[res[nm][3] for nm in names])
```

```python
import jax
import jax.numpy as jnp
from jax import lax
from jax.experimental import pallas as pl
from jax.experimental.pallas import tpu as pltpu

F32, BF16 = jnp.float32, jnp.bfloat16
RMS_EPS = 1e-6
N_META = 16
CONV_WIDTH = 3
POOL_WINDOWS = (2, 4, 8, 16)
POOL_GROUP = 128
HALO = 16
N_DEV = 8
MESH_AXES = ("x", "y", "c")
MESH = pl.DeviceIdType.MESH
VMEM_LIMIT_BYTES = 56 * 1024 * 1024
ADAMW_BLOCK_ELEMS = 64 * 1024
TM_MIX = 512
TM_FFN = 256
FFN_CHUNK = 512
FFN_BACKWARD_LAG = 3
TM_WGRAD = 512
FF_CHUNKS = 2

ADAM_LR, ADAM_B1, ADAM_B2, ADAM_EPS, ADAM_WD, ADAM_STEP = 0.001, 0.9, 0.999, 1e-08, 0.01, 10


def _dot(a, b):
    return jnp.dot(a, b, preferred_element_type=F32)


def _dot_nt(a, b):
    return lax.dot_general(a, b, (((1,), (1,)), ((), ())), preferred_element_type=F32)


def _dot_tn(a, b):
    return lax.dot_general(a, b, (((0,), (0,)), ((), ())), preferred_element_type=F32)


def _rms_stats(h):
    rstd = lax.rsqrt(jnp.mean(h * h, axis=-1, keepdims=True) + RMS_EPS)
    return h * rstd, rstd


def _rms_bwd(hat, rstd, g, dy):
    gdy = dy * g
    proj = jnp.mean(gdy * hat, axis=-1, keepdims=True)
    return rstd * (gdy - hat * proj), jnp.sum(dy * hat, axis=0, keepdims=True)


def _params(*semantics):
    return pltpu.CompilerParams(dimension_semantics=semantics or None, vmem_limit_bytes=VMEM_LIMIT_BYTES)


def _resident(shape):
    zeros = (0,) * len(shape)
    return pl.BlockSpec(shape, lambda *_: zeros, pipeline_mode=pl.Buffered(1))


def _const(shape):
    zeros = (0,) * len(shape)
    return pl.BlockSpec(shape, lambda *_: zeros)


ANY = pl.BlockSpec(memory_space=pl.ANY)


def _my_place():
    x, y, c = (lax.axis_index(a) for a in MESH_AXES)
    return x, y, c


def _exchange_sems(n):
    return [pltpu.SemaphoreType.DMA((n, N_DEV - 1)), pltpu.SemaphoreType.DMA((n, N_DEV - 1)), pltpu.SemaphoreType.DMA((n,))]


def _gather_ops(srcs, outs, send_sems, recv_sems, local_sems, core_major=False):
    n = len(srcs)
    x, y, c = _my_place()
    me, sibling = (x, y, c), (x, y, 1 - c)
    chips = [(1 - x, y), (x, 1 - y), (1 - x, 1 - y)]

    def slab(px, py, pc):
        return 4 * pc + 2 * px + py if core_major else 4 * px + 2 * py + pc

    def copy(a, k, block, to, src=None):
        dst = outs[a].at[slab(*block)]
        return pltpu.make_async_remote_copy(
            src_ref=dst if src is None else src, dst_ref=dst, send_sem=send_sems.at[a, k], recv_sem=recv_sems.at[a, k],
            device_id=to, device_id_type=MESH)

    def mine(a):
        return pltpu.make_async_copy(srcs[a], outs[a].at[slab(*me)], local_sems.at[a])

    def first(a):
        return [copy(a, 0, me, sibling, src=srcs[a])] + [copy(a, 1 + j, me, (*chip, c), src=srcs[a]) for j, chip in enumerate(chips)]

    def passed(a, j):
        return copy(a, 4 + j, (*chips[j], c), sibling)

    def start():
        for a in range(n):
            mine(a).start()
            for cp in first(a):
                cp.start()

    def forward():
        for j, chip in enumerate(chips):
            for a in range(n):
                copy(a, 1 + j, (*chip, c), me).wait_recv()
                passed(a, j).start()

    def finish():
        for a in range(n):
            copy(a, 0, sibling, me).wait_recv()
            for j, chip in enumerate(chips):
                copy(a, 4 + j, (*chip, 1 - c), me).wait_recv()
        for a in range(n):
            for cp in first(a) + [passed(a, j) for j in range(len(chips))]:
                cp.wait_send()
            mine(a).wait()

    return start, forward, finish


def _exchange_ops(ins, outs, whole, send_sems, recv_sems, local_sems):
    n = len(ins)
    x, y, c = _my_place()
    me = 4 * x + 2 * y + c

    def src(a, i):
        return ins[a] if whole[a] else ins[a].at[i]

    def mine(a):
        return pltpu.make_async_copy(src(a, me), outs[a].at[me], local_sems.at[a])

    def send(a, k):
        to = (me + k) % N_DEV
        return pltpu.make_async_remote_copy(
            src_ref=src(a, to), dst_ref=outs[a].at[me], send_sem=send_sems.at[a, k - 1], recv_sem=recv_sems.at[a, k - 1],
            device_id=(to // 4, (to // 2) % 2, to % 2), device_id_type=MESH)

    def landed(a, k):
        frm = (me + N_DEV - k) % N_DEV
        return pltpu.make_async_remote_copy(
            src_ref=src(a, frm), dst_ref=outs[a].at[frm], send_sem=send_sems.at[a, k - 1], recv_sem=recv_sems.at[a, k - 1],
            device_id=(x, y, c), device_id_type=MESH)

    def start():
        for a in range(n):
            mine(a).start()
            for k in range(1, N_DEV):
                send(a, k).start()

    def finish():
        for a in range(n):
            for k in range(1, N_DEV):
                landed(a, k).wait_recv()
        for a in range(n):
            for k in range(1, N_DEV):
                send(a, k).wait_send()
            mine(a).wait()

    return start, finish


def _core_exchange_sems(n):
    return [pltpu.SemaphoreType.DMA((n, 4)), pltpu.SemaphoreType.DMA((n, N_DEV)), pltpu.SemaphoreType.DMA((n,))]


def _core_exchange_ops(ins, outs, to_core, send_sems, recv_sems, local_sems):
    n = len(ins)
    x, y, c = _my_place()
    me = 4 * x + 2 * y + c
    others = [(0, 1), (1, 0), (1, 1)]

    def slab(a, p):
        if len(ins[a].shape) == len(outs[a].shape):
            return ins[a].at[p]
        rows = outs[a].shape[1]
        return ins[a].at[pl.ds(pl.multiple_of(p * rows, 16), rows), :]

    def send(a, dx, dy):
        tx, ty = (x + dx) % 2, (y + dy) % 2
        return pltpu.make_async_remote_copy(
            src_ref=slab(a, 4 * to_core + 2 * tx + ty), dst_ref=outs[a].at[me], send_sem=send_sems.at[a, 2 * dx + dy],
            recv_sem=recv_sems.at[a, 2 * (2 * dx + dy) + c], device_id=(tx, ty, to_core), device_id_type=MESH)

    def mine(a):
        return pltpu.make_async_copy(slab(a, 4 * to_core + 2 * x + y), outs[a].at[me], local_sems.at[a])

    def landed(a, dx, dy, sc):
        frm = 4 * ((x + dx) % 2) + 2 * ((y + dy) % 2) + sc
        return pltpu.make_async_remote_copy(
            src_ref=slab(a, 0), dst_ref=outs[a].at[frm], send_sem=send_sems.at[a, 0], recv_sem=recv_sems.at[a, 2 * (2 * dx + dy) + sc],
            device_id=(x, y, c), device_id_type=MESH)

    def start():
        for a in range(n):
            for dx, dy in others:
                send(a, dx, dy).start()
            pl.when(c == to_core)(mine(a).start)
            pl.when(c != to_core)(send(a, 0, 0).start)

    def finish():
        @pl.when(c == to_core)
        def _():
            for a in range(n):
                for dx, dy in [(0, 0)] + others:
                    for sc in (0, 1):
                        if (dx, dy, sc) != (0, 0, to_core):
                            landed(a, dx, dy, sc).wait_recv()
            for a in range(n):
                mine(a).wait()

        @pl.when(c != to_core)
        def _():
            for a in range(n):
                send(a, 0, 0).wait_send()

        for a in range(n):
            for dx, dy in others:
                send(a, dx, dy).wait_send()

    return start, finish


N_CHIP = 4


def _pair_then_chip_sems(n):
    return [pltpu.SemaphoreType.DMA((n, N_CHIP)) for _ in range(6)] + [pltpu.SemaphoreType.DMA((n,))]


def _pair_then_chip_ops(ins, pairs, outs, mine_v, pair_v, sum_v, pair_send, pair_recv, chip_send, chip_recv, load_a, load_b, own_sem):
    n = len(ins)
    x, y, c = _my_place()
    chip = 2 * x + y
    chips = [(0, 0), (0, 1), (1, 0), (1, 1)]
    others = [(0, 1), (1, 0), (1, 1)]

    def to_sibling(a, j):
        px, py = chips[j]
        return pltpu.make_async_remote_copy(
            src_ref=ins[a].at[4 * px + 2 * py + 1 - c], dst_ref=pairs[a].at[j], send_sem=pair_send.at[a, j], recv_sem=pair_recv.at[a, j],
            device_id=(x, y, 1 - c), device_id_type=MESH)

    def spread(a, dx, dy):
        tx, ty = (x + dx) % 2, (y + dy) % 2
        return pltpu.make_async_remote_copy(
            src_ref=sum_v[a].at[2 * tx + ty], dst_ref=outs[a].at[chip], send_sem=chip_send.at[a, 2 * dx + dy],
            recv_sem=chip_recv.at[a, 2 * dx + dy], device_id=(tx, ty, c), device_id_type=MESH)

    def landed(a, dx, dy):
        frm = 2 * ((x + dx) % 2) + (y + dy) % 2
        return pltpu.make_async_remote_copy(
            src_ref=sum_v[a].at[0], dst_ref=outs[a].at[frm], send_sem=chip_send.at[a, 0], recv_sem=chip_recv.at[a, 2 * dx + dy],
            device_id=(x, y, c), device_id_type=MESH)

    def own(a):
        return pltpu.make_async_copy(sum_v[a].at[chip], outs[a].at[chip], own_sem.at[a])

    def pair():
        loads = []
        for a in range(n):
            for j, (px, py) in enumerate(chips):
                to_sibling(a, j).start()
                loads.append(pltpu.make_async_copy(ins[a].at[4 * px + 2 * py + c], mine_v[a].at[j], load_a.at[a, j]))
                loads[-1].start()
        for a in range(n):
            for j in range(N_CHIP):
                to_sibling(a, j).wait_recv()
                loads.append(pltpu.make_async_copy(pairs[a].at[j], pair_v[a].at[j], load_b.at[a, j]))
                loads[-1].start()
        for cp in loads:
            cp.wait()
        for a in range(n):
            sum_v[a][...] = (mine_v[a][...].astype(F32) + pair_v[a][...].astype(F32)).astype(sum_v[a].dtype)

    def start():
        pair()
        for a in range(n):
            own(a).start()
            for dx, dy in others:
                spread(a, dx, dy).start()

    def finish():
        for a in range(n):
            for dx, dy in others:
                landed(a, dx, dy).wait_recv()
        for a in range(n):
            for dx, dy in others:
                spread(a, dx, dy).wait_send()
            for j in range(N_CHIP):
                to_sibling(a, j).wait_send()
            own(a).wait()

    return start, finish


def _window_sum(x, win, ahead):
    n = x.shape[0]
    span = 1
    while span < win:
        x = x + pltpu.roll(x, n - span if ahead else span, 0)
        span *= 2
    return x


def _conv_branch(z, ext_u, conv_ref, tm):
    c_w = z.shape[1] // 4
    b, c, v = z[:, :c_w], z[:, c_w:2 * c_w], z[:, 2 * c_w:3 * c_w]
    u = c * v
    ext_u[pl.ds(HALO, tm), :] = u
    u1 = ext_u[pl.ds(HALO - 1, tm), :]
    u2 = ext_u[pl.ds(HALO - 2, tm), :]
    yc = conv_ref[pl.ds(2, 1), :] * u + conv_ref[pl.ds(1, 1), :] * u1 + conv_ref[pl.ds(0, 1), :] * u2
    return b, c, v, u, u1, u2, yc


def _pool_branch(p, ext_p, pool_w_ref, tm):
    ext_p[pl.ds(HALO, tm), :] = p
    pooled, mixed = [], []
    for g, win in enumerate(POOL_WINDOWS):
        s = _window_sum(ext_p[:, pl.ds(POOL_GROUP * g, POOL_GROUP)], win, ahead=False)[HALO:HALO + tm, :]
        pooled.append((s * (1.0 / win) - p[:, POOL_GROUP * g:POOL_GROUP * (g + 1)]).astype(BF16))
        mixed.append(_dot(pooled[-1], pool_w_ref[g].astype(BF16)))
    return pooled, mixed


def _gather_and_mixer_forward(x2d, mixer_shards, ffn_shards, g1, pool_w, pool_scale, g2, n_seq):
    t, d = x2d.shape
    zs, rs, ms, cs = mixer_shards[0].shape[1], mixer_shards[1].shape[0], mixer_shards[2].shape[1], mixer_shards[3].shape[1]
    zw, cw = N_DEV * zs, N_DEV * cs
    s = t // n_seq
    tm = min(TM_MIX, s)
    nj = s // tm
    n1, n2 = len(mixer_shards), len(ffn_shards)
    dtypes = [BF16, BF16, F32, F32] + [BF16] * n2
    shards = list(mixer_shards) + list(ffn_shards)

    def body(x_ref, *rest):
        shard_refs, (g1_ref, pw_ref, ps_ref, g2_ref), rest = rest[:n1 + n2], rest[n1 + n2:n1 + n2 + 4], rest[n1 + n2 + 4:]
        (h1_ref, z_ref, m_ref, pooled_ref, mixed_ref, win_o, wout_o, meta_o, conv_o, am_o, zm_o), rest = rest[:11], rest[11:]
        slabs, rest = rest[:n1 + n2], rest[n1 + n2:]
        stages, rest = rest[:n1 + n2], rest[n1 + n2:]
        win_v, wout_v, meta_v, conv_v, ext_u, ext_p, sem = rest[:7]
        first = _gather_ops(stages[:n1], slabs[:n1], *rest[7:10])
        later = _gather_ops(stages[n1:], slabs[n1:], *rest[10:13], core_major=True)

        @pl.when((pl.program_id(0) == 0) & (pl.program_id(1) == 0))
        def _():
            for src, dst in zip(shard_refs, stages):
                dst[...] = src[...].astype(dst.dtype)
            first[0]()
            later[0]()
            first[1]()
            first[2]()
            copies = [pltpu.make_async_copy(slabs[0].at[i], win_v.at[:, pl.ds(zs * i, zs)], sem.at[i]) for i in range(N_DEV)]
            copies += [pltpu.make_async_copy(slabs[1].at[i], wout_v.at[pl.ds(rs * i, rs), :], sem.at[N_DEV + i]) for i in range(N_DEV)]
            copies += [pltpu.make_async_copy(slabs[2], meta_v, sem.at[2 * N_DEV]), pltpu.make_async_copy(slabs[3], conv_v, sem.at[2 * N_DEV + 1])]
            for cp in copies:
                cp.start()
            for cp in copies:
                cp.wait()
            copies = [pltpu.make_async_copy(win_v, win_o, sem.at[0]), pltpu.make_async_copy(wout_v, wout_o, sem.at[1])]
            for cp in copies:
                cp.start()
            for i in range(N_DEV):
                meta_o[:, pl.ds(ms * i, ms)] = meta_v[i]
                conv_o[:, pl.ds(cs * i, cs)] = conv_v[i]
            hat, _ = _rms_stats(meta_o[...])
            a = (hat * g1_ref[...]).astype(BF16)
            am_o[...] = a
            zm_o[...] = _dot(a, win_v[...])
            for cp in copies:
                cp.wait()

        @pl.when(pl.program_id(1) == 0)
        def _():
            zm = zm_o[...]
            ext_u[pl.ds(0, HALO), :] = zm[:, cw:2 * cw] * zm[:, 2 * cw:3 * cw]
            ext_p[pl.ds(0, HALO), :] = zm[:, 3 * cw:]

        h0 = x_ref[...]
        hat, _ = _rms_stats(h0)
        z = _dot((hat * g1_ref[...]).astype(BF16), win_v[...])
        z_ref[...] = z.astype(BF16)
        b, _, _, _, _, _, yc = _conv_branch(z, ext_u, conv_o, tm)
        pooled, mixed = _pool_branch(z[:, 3 * cw:], ext_p, pw_ref, tm)
        pooled_ref[...] = jnp.concatenate(pooled, axis=1)
        mixed_ref[...] = jnp.concatenate(mixed, axis=1).astype(BF16)
        ps = ps_ref[...]
        y = [b * yc] + [mixed[g] * ps[:, POOL_GROUP * g:POOL_GROUP * (g + 1)] for g in range(len(POOL_WINDOWS))]
        m = _dot(jnp.concatenate(y, axis=1).astype(BF16), wout_v[...])
        m_ref[...] = m
        m_hat, _ = _rms_stats(m)
        h1_ref[...] = h0 + m_hat * g2_ref[...]
        ext_u[pl.ds(0, HALO), :] = ext_u[pl.ds(tm, HALO), :]
        ext_p[pl.ds(0, HALO), :] = ext_p[pl.ds(tm, HALO), :]

        @pl.when((pl.program_id(0) == n_seq - 1) & (pl.program_id(1) == nj - 1))
        def _():
            later[1]()
            later[2]()

    row = lambda b, j: (b * nj + j, 0)
    vmem = pl.BlockSpec(memory_space=pltpu.VMEM)
    small = [(N_META, d), (CONV_WIDTH, cw), (N_META, d), (N_META, zw)]
    out = pl.pallas_call(
        body, name="gather_and_mixer_forward", grid=(n_seq, nj),
        in_specs=[pl.BlockSpec((tm, d), row)] + [vmem] * (n1 + n2)
        + [_const(g1.shape), _const(pool_w.shape), _const(pool_scale.shape), _const(g2.shape)],
        out_specs=[pl.BlockSpec((tm, d), row), pl.BlockSpec((tm, zw), row), pl.BlockSpec((tm, d), row), pl.BlockSpec((tm, cw), row),
                   pl.BlockSpec((tm, cw), row), ANY, ANY] + [_const(sh) for sh in small] + [ANY] * (n1 + n2),
        out_shape=[jax.ShapeDtypeStruct((t, d), F32), jax.ShapeDtypeStruct((t, zw), BF16), jax.ShapeDtypeStruct((t, d), F32),
                   jax.ShapeDtypeStruct((t, cw), BF16), jax.ShapeDtypeStruct((t, cw), BF16),
                   jax.ShapeDtypeStruct((d, zw), BF16), jax.ShapeDtypeStruct((d, d), BF16),
                   jax.ShapeDtypeStruct(small[0], F32), jax.ShapeDtypeStruct(small[1], F32), jax.ShapeDtypeStruct(small[2], BF16),
                   jax.ShapeDtypeStruct(small[3], F32)]
        + [jax.ShapeDtypeStruct((N_DEV, *a.shape), dt) for a, dt in zip(shards, dtypes)],
        scratch_shapes=[pltpu.VMEM(a.shape, dt) for a, dt in zip(shards, dtypes)]
        + [pltpu.VMEM((d, zw), BF16), pltpu.VMEM((d, d), BF16), pltpu.VMEM((N_DEV, N_META, ms), F32),
           pltpu.VMEM((N_DEV, CONV_WIDTH, cs), F32), pltpu.VMEM((tm + HALO, cw), F32), pltpu.VMEM((tm + HALO, cw), F32),
           pltpu.SemaphoreType.DMA((2 * N_DEV + 2,))] + _exchange_sems(n1) + _exchange_sems(n2),
        compiler_params=_params("arbitrary", "arbitrary"),
    )(x2d, *shards, g1, pool_w, pool_scale, g2)
    return out[:5], out[5:11], out[11 + n1:]


def _mixer_backward(x2d, dh1, m, z, pooled, mixed, meta, a_meta, z_meta, g1, w_in, conv_w, pool_w, pool_scale, w_out, g2, n_seq,
                    to_exchange, landing):
    t, d = x2d.shape
    zw = w_in.shape[1]
    cw = zw // 4
    s = t // n_seq
    tm = min(TM_MIX, s)
    nj = s // tm
    n_groups = len(POOL_WINDOWS)
    zs = zw // N_DEV
    nx = len(to_exchange)
    n_in = 17
    given = [k for k, a in enumerate(landing) if a is not None]
    fresh = [k for k, a in enumerate(landing) if a is None]

    def body(x_ref, dh1_ref, m_ref, z_ref, zprev_ref, pooled_ref, mixed_ref, meta_ref, am_ref, zm_ref, g1_ref, win_ref, conv_ref, pw_ref, ps_ref, wout_ref,
             g2_ref, *rest):
        sent, rest = rest[:nx], rest[nx + len(given):]
        gx_ref, dwin_ref, dwout_ref, dg1_ref, dg2_ref, dconv_ref, dpw_ref, dps_ref, dmeta_ref = rest[:9]
        landed, rest = rest[9:9 + nx], rest[9 + nx:]
        ext_u, ext_dyc, ext_dq, acc_win, acc_wout, dz_meta, stage16, sem = rest[:8]
        north = _core_exchange_ops(sent, landed, 1, *rest[8:11])
        south = _core_exchange_ops([sent[k] for k in fresh], [landed[k] for k in fresh], 0, *rest[11:14])

        def start():
            north[0]()
            south[0]()

        def finish():
            south[1]()
            north[1]()

        b_id, j = pl.program_id(0), pl.program_id(1)
        jr = nj - 1 - j
        pl.when((b_id == 0) & (j == 0))(start)

        @pl.when((b_id == 0) & (j == 0))
        def _():
            acc_win[...] = jnp.zeros_like(acc_win)
            acc_wout[...] = jnp.zeros_like(acc_wout)
            dz_meta[...] = jnp.zeros_like(dz_meta)
            for r in (dg1_ref, dg2_ref, dconv_ref, dpw_ref, dps_ref, dmeta_ref):
                r[...] = jnp.zeros_like(r)

        @pl.when(j == 0)
        def _():
            ext_dyc[pl.ds(tm, HALO), :] = jnp.zeros((HALO, cw), F32)
            ext_dq[pl.ds(tm, HALO), :] = jnp.zeros((HALO, cw), F32)

        zm = zm_ref[...]
        halo = jnp.where(jr == 0, zm, zprev_ref[...].astype(F32))
        ext_u[pl.ds(0, HALO), :] = halo[:, cw:2 * cw] * halo[:, 2 * cw:3 * cw]

        dh1v = dh1_ref[...]
        m_hat, m_rstd = _rms_stats(m_ref[...])
        dm, dg2 = _rms_bwd(m_hat, m_rstd, g2_ref[...], dh1v)
        dg2_ref[...] += dg2
        dm = dm.astype(BF16)
        dycat = _dot_nt(dm, wout_ref[...])

        b, c, v, u, u1, u2, yc = _conv_branch(z_ref[...].astype(F32), ext_u, conv_ref, tm)
        mixed = [mixed_ref[:, pl.ds(POOL_GROUP * g, POOL_GROUP)].astype(F32) for g in range(n_groups)]
        ps = ps_ref[...]
        y = [b * yc] + [mixed[g] * ps[:, POOL_GROUP * g:POOL_GROUP * (g + 1)] for g in range(n_groups)]
        ycat = jnp.concatenate(y, axis=1).astype(BF16)
        acc_wout[...] += _dot_tn(ycat, dm)

        dyconv = dycat[:, :cw]
        db = dyconv * yc
        dyc = dyconv * b
        ext_dyc[pl.ds(0, tm), :] = dyc
        du = (conv_ref[pl.ds(2, 1), :] * dyc + conv_ref[pl.ds(1, 1), :] * ext_dyc[pl.ds(1, tm), :]
              + conv_ref[pl.ds(0, 1), :] * ext_dyc[pl.ds(2, tm), :])
        dconv_ref[pl.ds(2, 1), :] += jnp.sum(dyc * u, axis=0, keepdims=True)
        dconv_ref[pl.ds(1, 1), :] += jnp.sum(dyc * u1, axis=0, keepdims=True)
        dconv_ref[pl.ds(0, 1), :] += jnp.sum(dyc * u2, axis=0, keepdims=True)

        dp = []
        for g, win in enumerate(POOL_WINDOWS):
            lanes = pl.ds(POOL_GROUP * g, POOL_GROUP)
            dypool = dycat[:, cw + POOL_GROUP * g:cw + POOL_GROUP * (g + 1)]
            dps_ref[:, lanes] += jnp.sum(dypool * mixed[g], axis=0, keepdims=True)
            dmixed = (dypool * ps[:, POOL_GROUP * g:POOL_GROUP * (g + 1)]).astype(BF16)
            dq = _dot_nt(dmixed, pw_ref[g].astype(BF16))
            dpw_ref[g] += _dot_tn(pooled_ref[:, lanes], dmixed)
            ext_dq[pl.ds(0, tm), lanes] = dq
            acc = _window_sum(ext_dq[:, lanes], win, ahead=True)[0:tm, :]
            dp.append(acc * (1.0 / win) - dq)

        dz = jnp.concatenate([db, du * v, du * c] + dp, axis=1).astype(BF16)
        da = _dot_nt(dz, win_ref[...])
        h0 = x_ref[...]
        hat0, rstd0 = _rms_stats(h0)
        g1 = g1_ref[...]
        acc_win[...] += _dot_tn((hat0 * g1).astype(BF16), dz)
        dh0, dg1 = _rms_bwd(hat0, rstd0, g1, da)
        dg1_ref[...] += dg1
        gx_ref[...] = dh1v + dh0

        ext_dyc[pl.ds(tm, HALO), :] = ext_dyc[pl.ds(0, HALO), :]
        ext_dq[pl.ds(tm, HALO), :] = ext_dq[pl.ds(0, HALO), :]

        @pl.when(jr == 0)
        def _():
            ext_dyc[pl.ds(tm - HALO, HALO), :] = jnp.zeros((HALO, cw), F32)
            ext_dq[pl.ds(tm - HALO, HALO), :] = jnp.zeros((HALO, cw), F32)
            du_m = (conv_ref[pl.ds(1, 1), :] * ext_dyc[pl.ds(tm - HALO + 1, HALO), :]
                    + conv_ref[pl.ds(0, 1), :] * ext_dyc[pl.ds(tm - HALO + 2, HALO), :])
            dp_m = []
            for g, win in enumerate(POOL_WINDOWS):
                lanes = pl.ds(POOL_GROUP * g, POOL_GROUP)
                acc = ext_dq[pl.ds(tm - HALO + 1, HALO), lanes]
                for k in range(2, win):
                    acc = acc + ext_dq[pl.ds(tm - HALO + k, HALO), lanes]
                dp_m.append(acc * (1.0 / win))
            dz_meta[...] += jnp.concatenate(
                [jnp.zeros((HALO, cw), F32), du_m * zm[:, 2 * cw:3 * cw], du_m * zm[:, cw:2 * cw]] + dp_m, axis=1)

        @pl.when((b_id == n_seq - 1) & (j == nj - 1))
        def _():
            dz_m = dz_meta[...].astype(BF16)
            acc_win[...] += _dot_tn(am_ref[...], dz_m)
            hat_m, rstd_m = _rms_stats(meta_ref[...])
            dmeta, dg1_m = _rms_bwd(hat_m, rstd_m, g1, _dot_nt(dz_m, win_ref[...]))
            dg1_ref[...] += dg1_m
            dmeta_ref[...] = dmeta
            pieces = [(acc_win, zs * i, dwin_ref.at[i]) for i in range(N_DEV)]
            pieces += [(acc_wout, zs * i, dwout_ref.at[:, pl.ds(zs * i, zs)]) for i in range(d // zs)]
            copies = []
            for k, (acc, col, dst) in enumerate(pieces):
                if k >= 2:
                    copies[k - 2].wait()
                stage16[k % 2] = acc[:, pl.ds(col, zs)].astype(BF16)
                copies.append(pltpu.make_async_copy(stage16.at[k % 2], dst, sem.at[k % 2]))
                copies[k].start()
            copies[-2].wait()
            copies[-1].wait()
            finish()

    row = lambda b, j: (b * nj + nj - 1 - j, 0)
    prev = lambda b, j: (jnp.maximum((b * s + (nj - 1 - j) * tm) // HALO - 1, 0), 0)
    small = [g1.shape, g2.shape, conv_w.shape, pool_w.shape, pool_scale.shape, meta.shape]
    out = pl.pallas_call(
        body, name="mixer_backward", grid=(n_seq, nj),
        in_specs=[pl.BlockSpec((tm, d), row), pl.BlockSpec((tm, d), row), pl.BlockSpec((tm, d), row), pl.BlockSpec((tm, zw), row),
                  pl.BlockSpec((HALO, zw), prev), pl.BlockSpec((tm, cw), row), pl.BlockSpec((tm, cw), row), _const(meta.shape), _const(a_meta.shape), _const(z_meta.shape), _const(g1.shape),
                  _resident(w_in.shape), _const(conv_w.shape), _const(pool_w.shape), _const(pool_scale.shape), _resident(w_out.shape),
                  _const(g2.shape)] + [ANY] * (nx + len(given)),
        out_specs=[pl.BlockSpec((tm, d), row), ANY, ANY] + [_const(sh) for sh in small] + [ANY] * nx,
        out_shape=[jax.ShapeDtypeStruct((t, d), F32), jax.ShapeDtypeStruct((N_DEV, d, zs), BF16),
                   jax.ShapeDtypeStruct(w_out.shape, BF16)] + [jax.ShapeDtypeStruct(sh, F32) for sh in small]
        + [jax.ShapeDtypeStruct((N_DEV, a.shape[0] // N_DEV, a.shape[1]), a.dtype) for a in to_exchange],
        input_output_aliases={n_in + nx + at: 9 + k for at, k in enumerate(given)},
        scratch_shapes=[pltpu.VMEM((tm + HALO, cw), F32)] * 3
        + [pltpu.VMEM(w_in.shape, F32), pltpu.VMEM(w_out.shape, F32), pltpu.VMEM((HALO, zw), F32), pltpu.VMEM((2, d, zs), BF16),
           pltpu.SemaphoreType.DMA((2,))] + _core_exchange_sems(nx) + _core_exchange_sems(len(fresh)),
        compiler_params=_params("arbitrary", "arbitrary"),
    )(x2d, dh1, m, z, z, pooled, mixed, meta, a_meta, z_meta, g1, w_in, conv_w, pool_w, pool_scale, w_out, g2, *to_exchange, *[landing[k] for k in given])
    return out[:9], out[9:]


def _ffn_forward_backward(h1, target, g3, w_gate, w_up, w_down, g4):
    t, d = h1.shape
    ff = w_gate.shape[0]
    tm = min(TM_FFN, t)
    nt = t // tm
    chunks = [(s, min(FFN_CHUNK, ff - s)) for s in range(0, ff, FFN_CHUNK)]

    def body(h1_ref, h1pp_ref, tgt_ref, g3_ref, wg_ref, wu_ref, wd_ref, g4_ref,
             f_ref, act_ref, dd_ref, dgate_ref, dup_ref, dh1_ref, loss_ref, dg3_ref, dg4_ref, *slots):
        gate_s, up_s, dd_s, dh2_s, df_s = slots
        i = pl.program_id(0)

        def forward(slot):
            h1v = h1_ref[...]
            hat, _ = _rms_stats(h1v)
            f = (hat * g3_ref[...]).astype(BF16)
            f_ref[...] = f
            s, n = chunks[0]
            gate, up = _dot_nt(f_ref[...], wg_ref[pl.ds(s, n), :]), _dot_nt(f_ref[...], wu_ref[pl.ds(s, n), :])
            yield
            down = None
            for k, (s, n) in enumerate(chunks):
                gate_s.at[slot][:, pl.ds(s, n)] = gate.astype(BF16)
                up_s.at[slot][:, pl.ds(s, n)] = up.astype(BF16)
                act = (gate * jax.nn.sigmoid(gate) * up).astype(BF16)
                act_ref[:, pl.ds(s, n)] = act
                if k + 1 < len(chunks):
                    s1, n1 = chunks[k + 1]
                    gate, up = _dot_nt(f_ref[...], wg_ref[pl.ds(s1, n1), :]), _dot_nt(f_ref[...], wu_ref[pl.ds(s1, n1), :])
                yield
                part = _dot(act_ref[:, pl.ds(s, n)], wd_ref[pl.ds(s, n), :])
                down = part if down is None else down + part
                yield
            d_hat, d_rstd = _rms_stats(down)
            g4 = g4_ref[...]
            err = h1v + d_hat * g4 - tgt_ref[...]
            loss_ref[...] += jnp.sum(err * err) * (0.5 / d)
            dh2 = err * (1.0 / d)
            dh2_s.at[slot][...] = dh2
            dd, dg4 = _rms_bwd(d_hat, d_rstd, g4, dh2)
            dg4_ref[...] += dg4
            dd = dd.astype(BF16)
            dd_ref[...] = dd
            dd_s.at[slot][...] = dd

        def backward(slot):
            s, n = chunks[0]
            dact = _dot_nt(dd_s.at[slot][...], wd_ref[pl.ds(s, n), :])
            yield
            df = None
            for k, (s, n) in enumerate(chunks):
                gate = gate_s.at[slot][:, pl.ds(s, n)].astype(F32)
                up = up_s.at[slot][:, pl.ds(s, n)].astype(F32)
                sig = jax.nn.sigmoid(gate)
                dup = (dact * (gate * sig)).astype(BF16)
                dgate = (dact * up * (sig * (1.0 + gate * (1.0 - sig)))).astype(BF16)
                dup_ref[:, pl.ds(s, n)] = dup
                dgate_ref[:, pl.ds(s, n)] = dgate
                if k + 1 < len(chunks):
                    s1, n1 = chunks[k + 1]
                    dact = _dot_nt(dd_s.at[slot][...], wd_ref[pl.ds(s1, n1), :])
                yield
                part = _dot(dgate_ref[:, pl.ds(s, n)], wg_ref[pl.ds(s, n), :]) + _dot(dup_ref[:, pl.ds(s, n)], wu_ref[pl.ds(s, n), :])
                df = part if df is None else df + part
                yield
            df_s.at[slot][...] = df

        def last(slot):
            hat, rstd = _rms_stats(h1pp_ref[...])
            dh1, dg3 = _rms_bwd(hat, rstd, g3_ref[...], df_s.at[slot][...])
            dg3_ref[...] += dg3
            dh1_ref[...] = dh2_s.at[slot][...] + dh1

        def emit(parity, with_forward, with_backward, with_last):
            fwd = forward(parity) if with_forward else iter(())
            bwd = backward(1 - parity) if with_backward else iter(())
            next(fwd, None)
            if with_last:
                last(parity)
            for _ in range(FFN_BACKWARD_LAG):
                next(fwd, None)
            alive = True
            while alive:
                alive = next(bwd, True) is None
                alive = (next(fwd, True) is None) or alive

        @pl.when(i == 0)
        def _():
            for r in (loss_ref, dg3_ref, dg4_ref, *slots):
                r[...] = jnp.zeros_like(r)

        @pl.when(i < nt)
        def _():
            emit(i % 2, True, True, True)

        @pl.when(i == nt)
        def _():
            emit(nt % 2, False, True, True)

        @pl.when(i == nt + 1)
        def _():
            emit((nt + 1) % 2, False, False, True)

    cur = lambda i: (jnp.minimum(i, nt - 1), 0)
    prev = lambda i: (jnp.clip(i - 1, 0, nt - 1), 0)
    prev2 = lambda i: (jnp.clip(i - 2, 0, nt - 1), 0)
    return pl.pallas_call(
        body, name="ffn_forward_backward", grid=(nt + 2,),
        in_specs=[pl.BlockSpec((tm, d), cur), pl.BlockSpec((tm, d), prev2), pl.BlockSpec((tm, d), cur), _const(g3.shape),
                  _resident(w_gate.shape), _resident(w_up.shape), _resident(w_down.shape), _const(g4.shape)],
        out_specs=[pl.BlockSpec((tm, d), cur), pl.BlockSpec((tm, ff), cur), pl.BlockSpec((tm, d), cur), pl.BlockSpec((tm, ff), prev),
                   pl.BlockSpec((tm, ff), prev), pl.BlockSpec((tm, d), prev2), _const((8, 128)), _const(g3.shape), _const(g4.shape)],
        out_shape=[jax.ShapeDtypeStruct((t, d), BF16), jax.ShapeDtypeStruct((t, ff), BF16), jax.ShapeDtypeStruct((t, d), BF16),
                   jax.ShapeDtypeStruct((t, ff), BF16), jax.ShapeDtypeStruct((t, ff), BF16), jax.ShapeDtypeStruct((t, d), F32),
                   jax.ShapeDtypeStruct((8, 128), F32), jax.ShapeDtypeStruct(g3.shape, F32), jax.ShapeDtypeStruct(g4.shape, F32)],
        scratch_shapes=[pltpu.VMEM((2, tm, ff), BF16)] * 2 + [pltpu.VMEM((2, tm, d), BF16)] + [pltpu.VMEM((2, tm, d), F32)] * 2,
        compiler_params=_params("arbitrary"),
    )(h1, h1, target, g3, w_gate, w_up, w_down, g4)


def _ffn_weight_grads(f, dd, dgate, dup, act):
    t, d = f.shape
    ff = dgate.shape[1]
    tm = min(TM_WGRAD, t)
    nt = t // tm
    fc = ff // FF_CHUNKS
    assert FF_CHUNKS == 2

    def body(f_ref, dd_ref, dgate_ref, dup_ref, act_ref, dwg_ref, dwu_ref, dwd_ref, *rest):
        landing, (acc_g, acc_u, acc_d, stage, sem) = rest[:2], rest[2:7]
        start, finish = _core_exchange_ops([dwg_ref, dwd_ref], landing, 0, *rest[7:])
        c, i = pl.program_id(0), pl.program_id(1)
        pl.when((c == 1) & (i == 0))(start)

        @pl.when(i == 0)
        def _():
            acc_g[...] = jnp.zeros_like(acc_g)
            acc_u[...] = jnp.zeros_like(acc_u)
            acc_d[...] = jnp.zeros_like(acc_d)

        fv = f_ref[...]
        acc_g[...] += _dot_tn(fv, dgate_ref[...])
        acc_u[...] += _dot_tn(fv, dup_ref[...])
        acc_d[...] += _dot_tn(act_ref[...], dd_ref[...])

        @pl.when(i == nt - 1)
        def _():
            rows = pl.ds(pl.multiple_of(c * fc, 16), fc)
            copies = []
            for k, (acc, out, transposed) in enumerate(((acc_d, dwd_ref, False), (acc_g, dwg_ref, True), (acc_u, dwu_ref, True))):
                if k >= 2:
                    copies[k - 2].wait()
                stage[k % 2] = (acc[...].T if transposed else acc[...]).astype(BF16)
                copies.append(pltpu.make_async_copy(stage.at[k % 2], out.at[rows, :], sem.at[k % 2]))
                copies[k].start()
            copies[-2].wait()
            copies[-1].wait()

        pl.when((c == 1) & (i == nt - 1))(finish)

    row = lambda c, i: (i, 0)
    col = lambda c, i: (i, c)
    out = pl.pallas_call(
        body, name="ffn_weight_grads", grid=(FF_CHUNKS, nt),
        in_specs=[pl.BlockSpec((tm, d), row), pl.BlockSpec((tm, d), row), pl.BlockSpec((tm, fc), col), pl.BlockSpec((tm, fc), col),
                  pl.BlockSpec((tm, fc), col)],
        out_specs=[ANY] * 5,
        out_shape=[jax.ShapeDtypeStruct((ff, d), BF16)] * 3 + [jax.ShapeDtypeStruct((N_DEV, ff // N_DEV, d), BF16)] * 2,
        scratch_shapes=[pltpu.VMEM((d, fc), F32), pltpu.VMEM((d, fc), F32), pltpu.VMEM((fc, d), F32), pltpu.VMEM((2, fc, d), BF16),
                        pltpu.SemaphoreType.DMA((2,))] + _core_exchange_sems(2),
        compiler_params=_params("arbitrary", "arbitrary"),
    )(f, dd, dgate, dup, act)
    return out[:3], [out[3], None, out[4]]


def _adamw(w, g, m, v):
    m = ADAM_B1 * m + (1.0 - ADAM_B1) * g
    v = ADAM_B2 * v + (1.0 - ADAM_B2) * (g * g)
    m_hat = m / (1.0 - ADAM_B1 ** ADAM_STEP)
    v_hat = v / (1.0 - ADAM_B2 ** ADAM_STEP)
    return -ADAM_LR * (m_hat / (jnp.sqrt(v_hat) + ADAM_EPS) + ADAM_WD * w), m, v


def _sum_slabs(ref):
    total = ref[0].astype(F32)
    for i in range(1, ref.shape[0]):
        total = total + ref[i].astype(F32)
    return total


def _adamw_rows(r, c):
    tr = r
    for cand in range(8, r, 8):
        if r % cand == 0 and cand * c <= ADAMW_BLOCK_ELEMS:
            tr = cand
    return r if r * c <= ADAMW_BLOCK_ELEMS else tr


def _reduce_adamw_carrying(parts, ws, ms, vs, to_reduce, to_exchange, whole, name):
    k, nr, nx = len(ws), len(to_reduce), len(to_exchange)
    r, c = ws[0].shape if k else (8, 128)
    tr = _adamw_rows(r, c)
    steps = r // tr
    travels = nr + nx > 0
    chip_slabs = [jax.ShapeDtypeStruct((N_CHIP, *a.shape[1:]), a.dtype) for a in to_reduce]

    def body(*refs):
        p_refs, w_refs, m_refs, v_refs = (refs[a * k:(a + 1) * k] for a in range(4))
        refs = refs[4 * k:]
        reduced_in, sent, refs = refs[:nr], refs[nr:nr + nx], refs[nr + nx:]
        outs, pairs, sums, landed, refs = refs[:4 * k], refs[4 * k:4 * k + nr], refs[4 * k + nr:4 * k + 2 * nr], \
            refs[4 * k + 2 * nr:4 * k + 2 * nr + nx], refs[4 * k + 2 * nr + nx:]
        mine_v, pair_v, sum_v, refs = refs[:nr], refs[nr:2 * nr], refs[2 * nr:3 * nr], refs[3 * nr:]
        if travels:
            reduce_ops = _pair_then_chip_ops(reduced_in, pairs, sums, mine_v, pair_v, sum_v, *refs[:7])
            direct_ops = _exchange_ops(sent, landed, whole, *refs[7:])

            @pl.when(pl.program_id(0) == 0)
            def _():
                direct_ops[0]()
                reduce_ops[0]()

        for a in range(k):
            g = _sum_slabs(p_refs[a])
            outs[4 * a][...] = g
            outs[4 * a + 1][...], outs[4 * a + 2][...], outs[4 * a + 3][...] = _adamw(w_refs[a][...], g, m_refs[a][...], v_refs[a][...])

        if travels:
            @pl.when(pl.program_id(0) == steps - 1)
            def _():
                reduce_ops[1]()
                direct_ops[1]()

    blk = pl.BlockSpec((tr, c), lambda i: (i, 0))
    out = pl.pallas_call(
        body, name=name, grid=(steps,),
        in_specs=[pl.BlockSpec((N_DEV, tr, c), lambda i: (0, i, 0))] * k + [blk] * (3 * k) + [ANY] * (nr + nx),
        out_specs=[blk] * (4 * k) + [ANY] * (2 * nr + nx),
        out_shape=[jax.ShapeDtypeStruct((r, c), F32)] * (4 * k) + chip_slabs + chip_slabs
        + [jax.ShapeDtypeStruct((N_DEV, *a.shape) if w else a.shape, a.dtype) for a, w in zip(to_exchange, whole)],
        scratch_shapes=([pltpu.VMEM(a.shape, a.dtype) for a in chip_slabs] * 3 + _pair_then_chip_sems(nr) + _exchange_sems(nx)
                        if travels else []),
        compiler_params=_params("arbitrary"),
    )(*parts, *ws, *ms, *vs, *to_reduce, *to_exchange)
    return [tuple(out[4 * a:4 * a + 4]) for a in range(k)], out[4 * k + nr:4 * k + 2 * nr], out[4 * k + 2 * nr:]


def _reduce_adamw(parts, w, m, v, name):
    r, c = w.shape
    tr = _adamw_rows(r, c)

    def body(p_ref, w_ref, m_ref, v_ref, g_out, d_out, m_out, v_out):
        g = _sum_slabs(p_ref)
        g_out[...] = g
        d_out[...], m_out[...], v_out[...] = _adamw(w_ref[...], g, m_ref[...], v_ref[...])

    blk = pl.BlockSpec((tr, c), lambda i: (i, 0))
    return pl.pallas_call(
        body, name=name, grid=(r // tr,),
        in_specs=[pl.BlockSpec((parts.shape[0], tr, c), lambda i: (0, i, 0)), blk, blk, blk],
        out_specs=[blk] * 4, out_shape=[jax.ShapeDtypeStruct((r, c), F32)] * 4,
        compiler_params=_params("arbitrary"),
    )(parts, w, m, v)


def _reduce_adamw_small(parts, ws, ms, vs, loss_parts):
    n = len(parts)

    def body(*refs):
        p_refs, w_refs, m_refs, v_refs = (refs[k * n:(k + 1) * n] for k in range(4))
        outs = refs[4 * n + 1:]
        outs[4 * n][...] = _sum_slabs(refs[4 * n])
        for a in range(n):
            g = _sum_slabs(p_refs[a])
            outs[4 * a][...] = g
            outs[4 * a + 1][...], outs[4 * a + 2][...], outs[4 * a + 3][...] = _adamw(w_refs[a][...], g, m_refs[a][...], v_refs[a][...])

    out = pl.pallas_call(
        body, name="adamw_replicated",
        out_shape=[jax.ShapeDtypeStruct(w.shape, F32) for w in ws for _ in range(4)] + [jax.ShapeDtypeStruct(loss_parts.shape[1:], F32)],
        compiler_params=pltpu.CompilerParams(vmem_limit_bytes=VMEM_LIMIT_BYTES),
    )(*parts, *ws, *ms, *vs, loss_parts)
    return [tuple(out[4 * a:4 * a + 4]) for a in range(n)], out[4 * n]


def kernel(x, meta_tokens, norm_mix_pre, w_in, conv_w, pool_w, pool_scale, w_out, norm_mix_post, norm_ffn_pre, w_gate, w_up, w_down, norm_ffn_post, loss_target, m_meta_tokens, m_norm_mix_pre, m_w_in, m_conv_w, m_pool_w, m_pool_scale, m_w_out, m_norm_mix_post, m_norm_ffn_pre, m_w_gate, m_w_up, m_w_down, m_norm_ffn_post, v_meta_tokens, v_norm_mix_pre, v_w_in, v_conv_w, v_pool_w, v_pool_scale, v_w_out, v_norm_mix_post, v_norm_ffn_pre, v_w_gate, v_w_up, v_w_down, v_norm_ffn_post):
    n_seq, seq, d = x.shape
    x2d = x.reshape(n_seq * seq, d)
    target = loss_target.reshape(n_seq * seq, d)

    t_ = lambda a: jnp.swapaxes(a[0], 0, 1)
    pw, ps = pool_w[0], pool_scale

    (h1, z, m, pooled, mixed), (win_b, wout_b, meta, conv, a_meta, z_meta), ffn_slabs = _gather_and_mixer_forward(
        x2d, [w_in[0], w_out[0], meta_tokens, conv_w[0]], [t_(w_gate), t_(w_up), w_down[0]], norm_mix_pre, pw, ps, norm_mix_post, n_seq)
    wg_b, wu_b, wd_b = (s.reshape(-1, d) for s in ffn_slabs)
    f, act, dd, dgate, dup, dh1, loss_sum, dg3, dg4 = _ffn_forward_backward(h1, target, norm_ffn_pre, wg_b, wu_b, wd_b, norm_ffn_post)
    ffn_grads, landing = _ffn_weight_grads(f, dd, dgate, dup, act)
    (gx, dwin, dwout, dg1, dg2, dconv, dpw, dps, dmeta), ffn_parts = _mixer_backward(
        x2d, dh1, m, z, pooled, mixed, meta, a_meta, z_meta, norm_mix_pre, win_b, conv, pw, ps, wout_b, norm_mix_post, n_seq,
        ffn_grads, landing)

    dmeta_s = jnp.transpose(dmeta.reshape(N_META, N_DEV, -1), (1, 0, 2))
    dconv_s = jnp.transpose(dconv.reshape(CONV_WIDTH, N_DEV, -1), (1, 0, 2))
    _, (win_parts, wout_parts), last = _reduce_adamw_carrying(
        [], [], [], [], [dwin, dwout.reshape(N_DEV, -1, d)], [dmeta_s, dconv_s, dg1, dg2, dg3, dg4, dpw, dps, loss_sum],
        [False] * 2 + [True] * 7, "exchange_rest")
    ffn_res, _, _ = _reduce_adamw_carrying(
        ffn_parts, [t_(w_gate), t_(w_up), w_down[0]], [t_(m_w_gate), t_(m_w_up), m_w_down[0]], [t_(v_w_gate), t_(v_w_up), v_w_down[0]],
        [], [], [], "adamw_ffn")
    replicated = last[2:8]

    names = ["meta_tokens", "norm_mix_pre", "w_in", "conv_w", "pool_w", "pool_scale", "w_out", "norm_mix_post", "norm_ffn_pre", "w_gate",
             "w_up", "w_down", "norm_ffn_post"]
    res = {"w_gate": tuple(jnp.swapaxes(o, 0, 1)[None] for o in ffn_res[0]),
           "w_up": tuple(jnp.swapaxes(o, 0, 1)[None] for o in ffn_res[1]), "w_down": tuple(o[None] for o in ffn_res[2])}
    for nm, parts, w, m_, v_ in (("w_in", win_parts, w_in, m_w_in, v_w_in), ("w_out", wout_parts, w_out, m_w_out, v_w_out),
                                 ("conv_w", last[1], conv_w, m_conv_w, v_conv_w)):
        res[nm] = tuple(o[None] for o in _reduce_adamw(parts, w[0], m_[0], v_[0], "adamw_" + nm))
    res["meta_tokens"] = tuple(_reduce_adamw(last[0], meta_tokens, m_meta_tokens, v_meta_tokens, "adamw_meta_tokens"))
    small, loss = _reduce_adamw_small(
        replicated, [norm_mix_pre, norm_mix_post, norm_ffn_pre, norm_ffn_post, pool_w[0], pool_scale],
        [m_norm_mix_pre, m_norm_mix_post, m_norm_ffn_pre, m_norm_ffn_post, m_pool_w[0], m_pool_scale],
        [v_norm_mix_pre, v_norm_mix_post, v_norm_ffn_pre, v_norm_ffn_post, v_pool_w[0], v_pool_scale], last[8])
    for nm, r in zip(["norm_mix_pre", "norm_mix_post", "norm_ffn_pre", "norm_ffn_post", "pool_w", "pool_scale"], small):
        res[nm] = tuple(o[None] for o in r) if nm == "pool_w" else r

    return (loss[0, 0], gx.reshape(n_seq, seq, d), *[res[nm][0] for nm in names], *[res[nm][1] for nm in names],
            *[res[nm][2] for nm in names], *[res[nm][3] for nm in names])
```

```python
import jax
import jax.numpy as jnp
from jax import lax
from jax.experimental import pallas as pl
from jax.experimental.pallas import tpu as pltpu

F32, BF16 = jnp.float32, jnp.bfloat16
RMS_EPS = 1e-6
N_META = 16
CONV_WIDTH = 3
POOL_WINDOWS = (2, 4, 8, 16)
POOL_GROUP = 128
HALO = 16
N_DEV = 8
MESH_AXES = ("x", "y", "c")
MESH = pl.DeviceIdType.MESH
VMEM_LIMIT_BYTES = 56 * 1024 * 1024
ADAMW_BLOCK_ELEMS = 64 * 1024
TM_MIX = 512
TM_FFN = 256
FFN_CHUNK = 512
FFN_BACKWARD_LAG = 3
TM_WGRAD = 512
FF_CHUNKS = 2

ADAM_LR, ADAM_B1, ADAM_B2, ADAM_EPS, ADAM_WD, ADAM_STEP = 0.001, 0.9, 0.999, 1e-08, 0.01, 10


def _dot(a, b):
    return jnp.dot(a, b, preferred_element_type=F32)


def _dot_nt(a, b):
    return lax.dot_general(a, b, (((1,), (1,)), ((), ())), preferred_element_type=F32)


def _dot_tn(a, b):
    return lax.dot_general(a, b, (((0,), (0,)), ((), ())), preferred_element_type=F32)


def _rms_stats(h):
    rstd = lax.rsqrt(jnp.mean(h * h, axis=-1, keepdims=True) + RMS_EPS)
    return h * rstd, rstd


def _rms_bwd(hat, rstd, g, dy):
    gdy = dy * g
    proj = jnp.mean(gdy * hat, axis=-1, keepdims=True)
    return rstd * (gdy - hat * proj), jnp.sum(dy * hat, axis=0, keepdims=True)


def _params(*semantics):
    return pltpu.CompilerParams(dimension_semantics=semantics or None, vmem_limit_bytes=VMEM_LIMIT_BYTES)


def _resident(shape):
    zeros = (0,) * len(shape)
    return pl.BlockSpec(shape, lambda *_: zeros, pipeline_mode=pl.Buffered(1))


def _const(shape):
    zeros = (0,) * len(shape)
    return pl.BlockSpec(shape, lambda *_: zeros)


ANY = pl.BlockSpec(memory_space=pl.ANY)


def _my_place():
    x, y, c = (lax.axis_index(a) for a in MESH_AXES)
    return x, y, c


def _exchange_sems(n):
    return [pltpu.SemaphoreType.DMA((n, N_DEV - 1)), pltpu.SemaphoreType.DMA((n, N_DEV - 1)), pltpu.SemaphoreType.DMA((n,))]


def _gather_ops(srcs, outs, send_sems, recv_sems, local_sems, core_major=False):
    n = len(srcs)
    x, y, c = _my_place()
    me, sibling = (x, y, c), (x, y, 1 - c)
    chips = [(1 - x, y), (x, 1 - y), (1 - x, 1 - y)]

    def slab(px, py, pc):
        return 4 * pc + 2 * px + py if core_major else 4 * px + 2 * py + pc

    def copy(a, k, block, to, src=None):
        dst = outs[a].at[slab(*block)]
        return pltpu.make_async_remote_copy(
            src_ref=dst if src is None else src, dst_ref=dst, send_sem=send_sems.at[a, k], recv_sem=recv_sems.at[a, k],
            device_id=to, device_id_type=MESH)

    def mine(a):
        return pltpu.make_async_copy(srcs[a], outs[a].at[slab(*me)], local_sems.at[a])

    def first(a):
        return [copy(a, 0, me, sibling, src=srcs[a])] + [copy(a, 1 + j, me, (*chip, c), src=srcs[a]) for j, chip in enumerate(chips)]

    def passed(a, j):
        return copy(a, 4 + j, (*chips[j], c), sibling)

    def start():
        for a in range(n):
            mine(a).start()
            for cp in first(a):
                cp.start()

    def forward():
        for j, chip in enumerate(chips):
            for a in range(n):
                copy(a, 1 + j, (*chip, c), me).wait_recv()
                passed(a, j).start()

    def finish():
        for a in range(n):
            copy(a, 0, sibling, me).wait_recv()
            for j, chip in enumerate(chips):
                copy(a, 4 + j, (*chip, 1 - c), me).wait_recv()
        for a in range(n):
            for cp in first(a) + [passed(a, j) for j in range(len(chips))]:
                cp.wait_send()
            mine(a).wait()

    return start, forward, finish


def _exchange_ops(ins, outs, whole, send_sems, recv_sems, local_sems):
    n = len(ins)
    x, y, c = _my_place()
    me = 4 * x + 2 * y + c

    def src(a, i):
        return ins[a] if whole[a] else ins[a].at[i]

    def mine(a):
        return pltpu.make_async_copy(src(a, me), outs[a].at[me], local_sems.at[a])

    def send(a, k):
        to = (me + k) % N_DEV
        return pltpu.make_async_remote_copy(
            src_ref=src(a, to), dst_ref=outs[a].at[me], send_sem=send_sems.at[a, k - 1], recv_sem=recv_sems.at[a, k - 1],
            device_id=(to // 4, (to // 2) % 2, to % 2), device_id_type=MESH)

    def landed(a, k):
        frm = (me + N_DEV - k) % N_DEV
        return pltpu.make_async_remote_copy(
            src_ref=src(a, frm), dst_ref=outs[a].at[frm], send_sem=send_sems.at[a, k - 1], recv_sem=recv_sems.at[a, k - 1],
            device_id=(x, y, c), device_id_type=MESH)

    def start():
        for a in range(n):
            mine(a).start()
            for k in range(1, N_DEV):
                send(a, k).start()

    def finish():
        for a in range(n):
            for k in range(1, N_DEV):
                landed(a, k).wait_recv()
        for a in range(n):
            for k in range(1, N_DEV):
                send(a, k).wait_send()
            mine(a).wait()

    return start, finish


def _core_exchange_sems(n):
    return [pltpu.SemaphoreType.DMA((n, 4)), pltpu.SemaphoreType.DMA((n, N_DEV)), pltpu.SemaphoreType.DMA((n,))]


def _core_exchange_ops(ins, outs, to_core, send_sems, recv_sems, local_sems):
    n = len(ins)
    x, y, c = _my_place()
    me = 4 * x + 2 * y + c
    others = [(0, 1), (1, 0), (1, 1)]

    def slab(a, p):
        if len(ins[a].shape) == len(outs[a].shape):
            return ins[a].at[p]
        rows = outs[a].shape[1]
        return ins[a].at[pl.ds(pl.multiple_of(p * rows, 16), rows), :]

    def send(a, dx, dy):
        tx, ty = (x + dx) % 2, (y + dy) % 2
        return pltpu.make_async_remote_copy(
            src_ref=slab(a, 4 * to_core + 2 * tx + ty), dst_ref=outs[a].at[me], send_sem=send_sems.at[a, 2 * dx + dy],
            recv_sem=recv_sems.at[a, 2 * (2 * dx + dy) + c], device_id=(tx, ty, to_core), device_id_type=MESH)

    def mine(a):
        return pltpu.make_async_copy(slab(a, 4 * to_core + 2 * x + y), outs[a].at[me], local_sems.at[a])

    def landed(a, dx, dy, sc):
        frm = 4 * ((x + dx) % 2) + 2 * ((y + dy) % 2) + sc
        return pltpu.make_async_remote_copy(
            src_ref=slab(a, 0), dst_ref=outs[a].at[frm], send_sem=send_sems.at[a, 0], recv_sem=recv_sems.at[a, 2 * (2 * dx + dy) + sc],
            device_id=(x, y, c), device_id_type=MESH)

    def start():
        for a in range(n):
            for dx, dy in others:
                send(a, dx, dy).start()
            pl.when(c == to_core)(mine(a).start)
            pl.when(c != to_core)(send(a, 0, 0).start)

    def finish():
        @pl.when(c == to_core)
        def _():
            for a in range(n):
                for dx, dy in [(0, 0)] + others:
                    for sc in (0, 1):
                        if (dx, dy, sc) != (0, 0, to_core):
                            landed(a, dx, dy, sc).wait_recv()
            for a in range(n):
                mine(a).wait()

        @pl.when(c != to_core)
        def _():
            for a in range(n):
                send(a, 0, 0).wait_send()

        for a in range(n):
            for dx, dy in others:
                send(a, dx, dy).wait_send()

    return start, finish


N_CHIP = 4


def _pair_then_chip_sems(n):
    return [pltpu.SemaphoreType.DMA((n, N_CHIP)) for _ in range(6)] + [pltpu.SemaphoreType.DMA((n,))]


def _pair_then_chip_ops(ins, pairs, outs, mine_v, pair_v, sum_v, pair_send, pair_recv, chip_send, chip_recv, load_a, load_b, own_sem):
    n = len(ins)
    x, y, c = _my_place()
    chip = 2 * x + y
    chips = [(0, 0), (0, 1), (1, 0), (1, 1)]
    others = [(0, 1), (1, 0), (1, 1)]

    def to_sibling(a, j):
        px, py = chips[j]
        return pltpu.make_async_remote_copy(
            src_ref=ins[a].at[4 * px + 2 * py + 1 - c], dst_ref=pairs[a].at[j], send_sem=pair_send.at[a, j], recv_sem=pair_recv.at[a, j],
            device_id=(x, y, 1 - c), device_id_type=MESH)

    def spread(a, dx, dy):
        tx, ty = (x + dx) % 2, (y + dy) % 2
        return pltpu.make_async_remote_copy(
            src_ref=sum_v[a].at[2 * tx + ty], dst_ref=outs[a].at[chip], send_sem=chip_send.at[a, 2 * dx + dy],
            recv_sem=chip_recv.at[a, 2 * dx + dy], device_id=(tx, ty, c), device_id_type=MESH)

    def landed(a, dx, dy):
        frm = 2 * ((x + dx) % 2) + (y + dy) % 2
        return pltpu.make_async_remote_copy(
            src_ref=sum_v[a].at[0], dst_ref=outs[a].at[frm], send_sem=chip_send.at[a, 0], recv_sem=chip_recv.at[a, 2 * dx + dy],
            device_id=(x, y, c), device_id_type=MESH)

    def own(a):
        return pltpu.make_async_copy(sum_v[a].at[chip], outs[a].at[chip], own_sem.at[a])

    def pair():
        loads = []
        for a in range(n):
            for j, (px, py) in enumerate(chips):
                to_sibling(a, j).start()
                loads.append(pltpu.make_async_copy(ins[a].at[4 * px + 2 * py + c], mine_v[a].at[j], load_a.at[a, j]))
                loads[-1].start()
        for a in range(n):
            for j in range(N_CHIP):
                to_sibling(a, j).wait_recv()
                loads.append(pltpu.make_async_copy(pairs[a].at[j], pair_v[a].at[j], load_b.at[a, j]))
                loads[-1].start()
        for cp in loads:
            cp.wait()
        for a in range(n):
            sum_v[a][...] = (mine_v[a][...].astype(F32) + pair_v[a][...].astype(F32)).astype(sum_v[a].dtype)

    def start():
        pair()
        for a in range(n):
            own(a).start()
            for dx, dy in others:
                spread(a, dx, dy).start()

    def finish():
        for a in range(n):
            for dx, dy in others:
                landed(a, dx, dy).wait_recv()
        for a in range(n):
            for dx, dy in others:
                spread(a, dx, dy).wait_send()
            for j in range(N_CHIP):
                to_sibling(a, j).wait_send()
            own(a).wait()

    return start, finish


def _window_sum(x, win, ahead):
    n = x.shape[0]
    span = 1
    while span < win:
        x = x + pltpu.roll(x, n - span if ahead else span, 0)
        span *= 2
    return x


def _conv_branch(z, ext_u, conv_ref, tm):
    c_w = z.shape[1] // 4
    b, c, v = z[:, :c_w], z[:, c_w:2 * c_w], z[:, 2 * c_w:3 * c_w]
    u = c * v
    ext_u[pl.ds(HALO, tm), :] = u
    u1 = ext_u[pl.ds(HALO - 1, tm), :]
    u2 = ext_u[pl.ds(HALO - 2, tm), :]
    yc = conv_ref[pl.ds(2, 1), :] * u + conv_ref[pl.ds(1, 1), :] * u1 + conv_ref[pl.ds(0, 1), :] * u2
    return b, c, v, u, u1, u2, yc


def _pool_branch(p, ext_p, pool_w_ref, tm):
    ext_p[pl.ds(HALO, tm), :] = p
    pooled, mixed = [], []
    for g, win in enumerate(POOL_WINDOWS):
        s = _window_sum(ext_p[:, pl.ds(POOL_GROUP * g, POOL_GROUP)], win, ahead=False)[HALO:HALO + tm, :]
        pooled.append((s * (1.0 / win) - p[:, POOL_GROUP * g:POOL_GROUP * (g + 1)]).astype(BF16))
        mixed.append(_dot(pooled[-1], pool_w_ref[g].astype(BF16)))
    return pooled, mixed


def _gather_and_mixer_forward(x2d, mixer_shards, ffn_shards, g1, pool_w, pool_scale, g2, n_seq):
    t, d = x2d.shape
    zs, rs, ms, cs = mixer_shards[0].shape[1], mixer_shards[1].shape[0], mixer_shards[2].shape[1], mixer_shards[3].shape[1]
    zw, cw = N_DEV * zs, N_DEV * cs
    s = t // n_seq
    tm = min(TM_MIX, s)
    nj = s // tm
    n1, n2 = len(mixer_shards), len(ffn_shards)
    dtypes = [BF16, BF16, F32, F32] + [BF16] * n2
    shards = list(mixer_shards) + list(ffn_shards)

    def body(x_ref, *rest):
        shard_refs, (g1_ref, pw_ref, ps_ref, g2_ref), rest = rest[:n1 + n2], rest[n1 + n2:n1 + n2 + 4], rest[n1 + n2 + 4:]
        (h1_ref, z_ref, m_ref, pooled_ref, mixed_ref, win_o, wout_o, meta_o, conv_o, am_o, zm_o), rest = rest[:11], rest[11:]
        slabs, rest = rest[:n1 + n2], rest[n1 + n2:]
        stages, rest = rest[:n1 + n2], rest[n1 + n2:]
        win_v, wout_v, meta_v, conv_v, ext_u, ext_p, sem = rest[:7]
        first = _gather_ops(stages[:n1], slabs[:n1], *rest[7:10])
        later = _gather_ops(stages[n1:], slabs[n1:], *rest[10:13], core_major=True)

        @pl.when((pl.program_id(0) == 0) & (pl.program_id(1) == 0))
        def _():
            for src, dst in zip(shard_refs, stages):
                dst[...] = src[...].astype(dst.dtype)
            first[0]()
            later[0]()
            first[1]()
            first[2]()
            copies = [pltpu.make_async_copy(slabs[0].at[i], win_v.at[:, pl.ds(zs * i, zs)], sem.at[i]) for i in range(N_DEV)]
            copies += [pltpu.make_async_copy(slabs[1].at[i], wout_v.at[pl.ds(rs * i, rs), :], sem.at[N_DEV + i]) for i in range(N_DEV)]
            copies += [pltpu.make_async_copy(slabs[2], meta_v, sem.at[2 * N_DEV]), pltpu.make_async_copy(slabs[3], conv_v, sem.at[2 * N_DEV + 1])]
            for cp in copies:
                cp.start()
            for cp in copies:
                cp.wait()
            copies = [pltpu.make_async_copy(win_v, win_o, sem.at[0]), pltpu.make_async_copy(wout_v, wout_o, sem.at[1])]
            for cp in copies:
                cp.start()
            for i in range(N_DEV):
                meta_o[:, pl.ds(ms * i, ms)] = meta_v[i]
                conv_o[:, pl.ds(cs * i, cs)] = conv_v[i]
            hat, _ = _rms_stats(meta_o[...])
            a = (hat * g1_ref[...]).astype(BF16)
            am_o[...] = a
            zm_o[...] = _dot(a, win_v[...])
            for cp in copies:
                cp.wait()

        @pl.when(pl.program_id(1) == 0)
        def _():
            zm = zm_o[...]
            ext_u[pl.ds(0, HALO), :] = zm[:, cw:2 * cw] * zm[:, 2 * cw:3 * cw]
            ext_p[pl.ds(0, HALO), :] = zm[:, 3 * cw:]

        h0 = x_ref[...]
        hat, _ = _rms_stats(h0)
        z = _dot((hat * g1_ref[...]).astype(BF16), win_v[...])
        z_ref[...] = z.astype(BF16)
        b, _, _, _, _, _, yc = _conv_branch(z, ext_u, conv_o, tm)
        pooled, mixed = _pool_branch(z[:, 3 * cw:], ext_p, pw_ref, tm)
        pooled_ref[...] = jnp.concatenate(pooled, axis=1)
        mixed_ref[...] = jnp.concatenate(mixed, axis=1).astype(BF16)
        ps = ps_ref[...]
        y = [b * yc] + [mixed[g] * ps[:, POOL_GROUP * g:POOL_GROUP * (g + 1)] for g in range(len(POOL_WINDOWS))]
        m = _dot(jnp.concatenate(y, axis=1).astype(BF16), wout_v[...])
        m_ref[...] = m
        m_hat, _ = _rms_stats(m)
        h1_ref[...] = h0 + m_hat * g2_ref[...]
        ext_u[pl.ds(0, HALO), :] = ext_u[pl.ds(tm, HALO), :]
        ext_p[pl.ds(0, HALO), :] = ext_p[pl.ds(tm, HALO), :]

        @pl.when((pl.program_id(0) == n_seq - 1) & (pl.program_id(1) == nj - 1))
        def _():
            later[1]()
            later[2]()

    row = lambda b, j: (b * nj + j, 0)
    vmem = pl.BlockSpec(memory_space=pltpu.VMEM)
    small = [(N_META, d), (CONV_WIDTH, cw), (N_META, d), (N_META, zw)]
    out = pl.pallas_call(
        body, name="gather_and_mixer_forward", grid=(n_seq, nj),
        in_specs=[pl.BlockSpec((tm, d), row)] + [vmem] * (n1 + n2)
        + [_const(g1.shape), _const(pool_w.shape), _const(pool_scale.shape), _const(g2.shape)],
        out_specs=[pl.BlockSpec((tm, d), row), pl.BlockSpec((tm, zw), row), pl.BlockSpec((tm, d), row), pl.BlockSpec((tm, cw), row),
                   pl.BlockSpec((tm, cw), row), ANY, ANY] + [_const(sh) for sh in small] + [ANY] * (n1 + n2),
        out_shape=[jax.ShapeDtypeStruct((t, d), F32), jax.ShapeDtypeStruct((t, zw), BF16), jax.ShapeDtypeStruct((t, d), F32),
                   jax.ShapeDtypeStruct((t, cw), BF16), jax.ShapeDtypeStruct((t, cw), BF16),
                   jax.ShapeDtypeStruct((d, zw), BF16), jax.ShapeDtypeStruct((d, d), BF16),
                   jax.ShapeDtypeStruct(small[0], F32), jax.ShapeDtypeStruct(small[1], F32), jax.ShapeDtypeStruct(small[2], BF16),
                   jax.ShapeDtypeStruct(small[3], F32)]
        + [jax.ShapeDtypeStruct((N_DEV, *a.shape), dt) for a, dt in zip(shards, dtypes)],
        scratch_shapes=[pltpu.VMEM(a.shape, dt) for a, dt in zip(shards, dtypes)]
        + [pltpu.VMEM((d, zw), BF16), pltpu.VMEM((d, d), BF16), pltpu.VMEM((N_DEV, N_META, ms), F32),
           pltpu.VMEM((N_DEV, CONV_WIDTH, cs), F32), pltpu.VMEM((tm + HALO, cw), F32), pltpu.VMEM((tm + HALO, cw), F32),
           pltpu.SemaphoreType.DMA((2 * N_DEV + 2,))] + _exchange_sems(n1) + _exchange_sems(n2),
        compiler_params=_params("arbitrary", "arbitrary"),
    )(x2d, *shards, g1, pool_w, pool_scale, g2)
    return out[:5], out[5:11], out[11 + n1:]


def _mixer_backward(x2d, dh1, m, z, pooled, mixed, meta, a_meta, z_meta, g1, w_in, conv_w, pool_w, pool_scale, w_out, g2, n_seq,
                    to_exchange, landing):
    t, d = x2d.shape
    zw = w_in.shape[1]
    cw = zw // 4
    s = t // n_seq
    tm = min(TM_MIX, s)
    nj = s // tm
    n_groups = len(POOL_WINDOWS)
    zs = zw // N_DEV
    nx = len(to_exchange)
    n_in = 17
    given = [k for k, a in enumerate(landing) if a is not None]
    fresh = [k for k, a in enumerate(landing) if a is None]

    def body(x_ref, dh1_ref, m_ref, z_ref, zprev_ref, pooled_ref, mixed_ref, meta_ref, am_ref, zm_ref, g1_ref, win_ref, conv_ref, pw_ref, ps_ref, wout_ref,
             g2_ref, *rest):
        sent, rest = rest[:nx], rest[nx + len(given):]
        gx_ref, dwin_ref, dwout_ref, dg1_ref, dg2_ref, dconv_ref, dpw_ref, dps_ref, dmeta_ref = rest[:9]
        landed, rest = rest[9:9 + nx], rest[9 + nx:]
        ext_u, ext_dyc, ext_dq, acc_win, acc_wout, dz_meta, stage16, sem = rest[:8]
        north = _core_exchange_ops(sent, landed, 1, *rest[8:11])
        south = _core_exchange_ops([sent[k] for k in fresh], [landed[k] for k in fresh], 0, *rest[11:14])

        def start():
            north[0]()
            south[0]()

        def finish():
            south[1]()
            north[1]()

        b_id, j = pl.program_id(0), pl.program_id(1)
        jr = nj - 1 - j
        pl.when((b_id == 0) & (j == 0))(start)

        @pl.when((b_id == 0) & (j == 0))
        def _():
            acc_win[...] = jnp.zeros_like(acc_win)
            acc_wout[...] = jnp.zeros_like(acc_wout)
            dz_meta[...] = jnp.zeros_like(dz_meta)
            for r in (dg1_ref, dg2_ref, dconv_ref, dpw_ref, dps_ref, dmeta_ref):
                r[...] = jnp.zeros_like(r)

        @pl.when(j == 0)
        def _():
            ext_dyc[pl.ds(tm, HALO), :] = jnp.zeros((HALO, cw), F32)
            ext_dq[pl.ds(tm, HALO), :] = jnp.zeros((HALO, cw), F32)

        zm = zm_ref[...]
        halo = jnp.where(jr == 0, zm, zprev_ref[...].astype(F32))
        ext_u[pl.ds(0, HALO), :] = halo[:, cw:2 * cw] * halo[:, 2 * cw:3 * cw]

        dh1v = dh1_ref[...]
        m_hat, m_rstd = _rms_stats(m_ref[...])
        dm, dg2 = _rms_bwd(m_hat, m_rstd, g2_ref[...], dh1v)
        dg2_ref[...] += dg2
        dm = dm.astype(BF16)
        dycat = _dot_nt(dm, wout_ref[...])

        b, c, v, u, u1, u2, yc = _conv_branch(z_ref[...].astype(F32), ext_u, conv_ref, tm)
        mixed = [mixed_ref[:, pl.ds(POOL_GROUP * g, POOL_GROUP)].astype(F32) for g in range(n_groups)]
        ps = ps_ref[...]
        y = [b * yc] + [mixed[g] * ps[:, POOL_GROUP * g:POOL_GROUP * (g + 1)] for g in range(n_groups)]
        ycat = jnp.concatenate(y, axis=1).astype(BF16)
        acc_wout[...] += _dot_tn(ycat, dm)

        dyconv = dycat[:, :cw]
        db = dyconv * yc
        dyc = dyconv * b
        ext_dyc[pl.ds(0, tm), :] = dyc
        du = (conv_ref[pl.ds(2, 1), :] * dyc + conv_ref[pl.ds(1, 1), :] * ext_dyc[pl.ds(1, tm), :]
              + conv_ref[pl.ds(0, 1), :] * ext_dyc[pl.ds(2, tm), :])
        dconv_ref[pl.ds(2, 1), :] += jnp.sum(dyc * u, axis=0, keepdims=True)
        dconv_ref[pl.ds(1, 1), :] += jnp.sum(dyc * u1, axis=0, keepdims=True)
        dconv_ref[pl.ds(0, 1), :] += jnp.sum(dyc * u2, axis=0, keepdims=True)

        dp = []
        for g, win in enumerate(POOL_WINDOWS):
            lanes = pl.ds(POOL_GROUP * g, POOL_GROUP)
            dypool = dycat[:, cw + POOL_GROUP * g:cw + POOL_GROUP * (g + 1)]
            dps_ref[:, lanes] += jnp.sum(dypool * mixed[g], axis=0, keepdims=True)
            dmixed = (dypool * ps[:, POOL_GROUP * g:POOL_GROUP * (g + 1)]).astype(BF16)
            dq = _dot_nt(dmixed, pw_ref[g].astype(BF16))
            dpw_ref[g] += _dot_tn(pooled_ref[:, lanes], dmixed)
            ext_dq[pl.ds(0, tm), lanes] = dq
            acc = _window_sum(ext_dq[:, lanes], win, ahead=True)[0:tm, :]
            dp.append(acc * (1.0 / win) - dq)

        dz = jnp.concatenate([db, du * v, du * c] + dp, axis=1).astype(BF16)
        da = _dot_nt(dz, win_ref[...])
        h0 = x_ref[...]
        hat0, rstd0 = _rms_stats(h0)
        g1 = g1_ref[...]
        acc_win[...] += _dot_tn((hat0 * g1).astype(BF16), dz)
        dh0, dg1 = _rms_bwd(hat0, rstd0, g1, da)
        dg1_ref[...] += dg1
        gx_ref[...] = dh1v + dh0

        ext_dyc[pl.ds(tm, HALO), :] = ext_dyc[pl.ds(0, HALO), :]
        ext_dq[pl.ds(tm, HALO), :] = ext_dq[pl.ds(0, HALO), :]

        @pl.when(jr == 0)
        def _():
            ext_dyc[pl.ds(tm - HALO, HALO), :] = jnp.zeros((HALO, cw), F32)
            ext_dq[pl.ds(tm - HALO, HALO), :] = jnp.zeros((HALO, cw), F32)
            du_m = (conv_ref[pl.ds(1, 1), :] * ext_dyc[pl.ds(tm - HALO + 1, HALO), :]
                    + conv_ref[pl.ds(0, 1), :] * ext_dyc[pl.ds(tm - HALO + 2, HALO), :])
            dp_m = []
            for g, win in enumerate(POOL_WINDOWS):
                lanes = pl.ds(POOL_GROUP * g, POOL_GROUP)
                acc = ext_dq[pl.ds(tm - HALO + 1, HALO), lanes]
                for k in range(2, win):
                    acc = acc + ext_dq[pl.ds(tm - HALO + k, HALO), lanes]
                dp_m.append(acc * (1.0 / win))
            dz_meta[...] += jnp.concatenate(
                [jnp.zeros((HALO, cw), F32), du_m * zm[:, 2 * cw:3 * cw], du_m * zm[:, cw:2 * cw]] + dp_m, axis=1)

        @pl.when((b_id == n_seq - 1) & (j == nj - 1))
        def _():
            dz_m = dz_meta[...].astype(BF16)
            acc_win[...] += _dot_tn(am_ref[...], dz_m)
            hat_m, rstd_m = _rms_stats(meta_ref[...])
            dmeta, dg1_m = _rms_bwd(hat_m, rstd_m, g1, _dot_nt(dz_m, win_ref[...]))
            dg1_ref[...] += dg1_m
            dmeta_ref[...] = dmeta
            pieces = [(acc_win, zs * i, dwin_ref.at[i]) for i in range(N_DEV)]
            pieces += [(acc_wout, zs * i, dwout_ref.at[:, pl.ds(zs * i, zs)]) for i in range(d // zs)]
            copies = []
            for k, (acc, col, dst) in enumerate(pieces):
                if k >= 2:
                    copies[k - 2].wait()
                stage16[k % 2] = acc[:, pl.ds(col, zs)].astype(BF16)
                copies.append(pltpu.make_async_copy(stage16.at[k % 2], dst, sem.at[k % 2]))
                copies[k].start()
            copies[-2].wait()
            copies[-1].wait()
            finish()

    row = lambda b, j: (b * nj + nj - 1 - j, 0)
    prev = lambda b, j: (jnp.maximum((b * s + (nj - 1 - j) * tm) // HALO - 1, 0), 0)
    small = [g1.shape, g2.shape, conv_w.shape, pool_w.shape, pool_scale.shape, meta.shape]
    out = pl.pallas_call(
        body, name="mixer_backward", grid=(n_seq, nj),
        in_specs=[pl.BlockSpec((tm, d), row), pl.BlockSpec((tm, d), row), pl.BlockSpec((tm, d), row), pl.BlockSpec((tm, zw), row),
                  pl.BlockSpec((HALO, zw), prev), pl.BlockSpec((tm, cw), row), pl.BlockSpec((tm, cw), row), _const(meta.shape), _const(a_meta.shape), _const(z_meta.shape), _const(g1.shape),
                  _resident(w_in.shape), _const(conv_w.shape), _const(pool_w.shape), _const(pool_scale.shape), _resident(w_out.shape),
                  _const(g2.shape)] + [ANY] * (nx + len(given)),
        out_specs=[pl.BlockSpec((tm, d), row), ANY, ANY] + [_const(sh) for sh in small] + [ANY] * nx,
        out_shape=[jax.ShapeDtypeStruct((t, d), F32), jax.ShapeDtypeStruct((N_DEV, d, zs), BF16),
                   jax.ShapeDtypeStruct(w_out.shape, BF16)] + [jax.ShapeDtypeStruct(sh, F32) for sh in small]
        + [jax.ShapeDtypeStruct((N_DEV, a.shape[0] // N_DEV, a.shape[1]), a.dtype) for a in to_exchange],
        input_output_aliases={n_in + nx + at: 9 + k for at, k in enumerate(given)},
        scratch_shapes=[pltpu.VMEM((tm + HALO, cw), F32)] * 3
        + [pltpu.VMEM(w_in.shape, F32), pltpu.VMEM(w_out.shape, F32), pltpu.VMEM((HALO, zw), F32), pltpu.VMEM((2, d, zs), BF16),
           pltpu.SemaphoreType.DMA((2,))] + _core_exchange_sems(nx) + _core_exchange_sems(len(fresh)),
        compiler_params=_params("arbitrary", "arbitrary"),
    )(x2d, dh1, m, z, z, pooled, mixed, meta, a_meta, z_meta, g1, w_in, conv_w, pool_w, pool_scale, w_out, g2, *to_exchange, *[landing[k] for k in given])
    return out[:9], out[9:]


def _ffn_forward_backward(h1, target, g3, w_gate, w_up, w_down, g4):
    t, d = h1.shape
    ff = w_gate.shape[0]
    tm = min(TM_FFN, t)
    nt = t // tm
    chunks = [(s, min(FFN_CHUNK, ff - s)) for s in range(0, ff, FFN_CHUNK)]

    def body(h1_ref, h1pp_ref, tgt_ref, g3_ref, wg_ref, wu_ref, wd_ref, g4_ref,
             f_ref, act_ref, dd_ref, dgate_ref, dup_ref, dh1_ref, loss_ref, dg3_ref, dg4_ref, *slots):
        gate_s, up_s, dd_s, dh2_s, df_s = slots
        i = pl.program_id(0)

        def forward(slot):
            h1v = h1_ref[...]
            hat, _ = _rms_stats(h1v)
            f = (hat * g3_ref[...]).astype(BF16)
            f_ref[...] = f
            s, n = chunks[0]
            gate, up = _dot_nt(f_ref[...], wg_ref[pl.ds(s, n), :]), _dot_nt(f_ref[...], wu_ref[pl.ds(s, n), :])
            yield
            down = None
            for k, (s, n) in enumerate(chunks):
                gate_s.at[slot][:, pl.ds(s, n)] = gate.astype(BF16)
                up_s.at[slot][:, pl.ds(s, n)] = up.astype(BF16)
                act = (gate * jax.nn.sigmoid(gate) * up).astype(BF16)
                act_ref[:, pl.ds(s, n)] = act
                if k + 1 < len(chunks):
                    s1, n1 = chunks[k + 1]
                    gate, up = _dot_nt(f_ref[...], wg_ref[pl.ds(s1, n1), :]), _dot_nt(f_ref[...], wu_ref[pl.ds(s1, n1), :])
                yield
                part = _dot(act_ref[:, pl.ds(s, n)], wd_ref[pl.ds(s, n), :])
                down = part if down is None else down + part
                yield
            d_hat, d_rstd = _rms_stats(down)
            g4 = g4_ref[...]
            err = h1v + d_hat * g4 - tgt_ref[...]
            loss_ref[...] += jnp.sum(err * err) * (0.5 / d)
            dh2 = err * (1.0 / d)
            dh2_s.at[slot][...] = dh2
            dd, dg4 = _rms_bwd(d_hat, d_rstd, g4, dh2)
            dg4_ref[...] += dg4
            dd = dd.astype(BF16)
            dd_ref[...] = dd
            dd_s.at[slot][...] = dd

        def backward(slot):
            s, n = chunks[0]
            dact = _dot_nt(dd_s.at[slot][...], wd_ref[pl.ds(s, n), :])
            yield
            df = None
            for k, (s, n) in enumerate(chunks):
                gate = gate_s.at[slot][:, pl.ds(s, n)].astype(F32)
                up = up_s.at[slot][:, pl.ds(s, n)].astype(F32)
                sig = jax.nn.sigmoid(gate)
                dup = (dact * (gate * sig)).astype(BF16)
                dgate = (dact * up * (sig * (1.0 + gate * (1.0 - sig)))).astype(BF16)
                dup_ref[:, pl.ds(s, n)] = dup
                dgate_ref[:, pl.ds(s, n)] = dgate
                if k + 1 < len(chunks):
                    s1, n1 = chunks[k + 1]
                    dact = _dot_nt(dd_s.at[slot][...], wd_ref[pl.ds(s1, n1), :])
                yield
                part = _dot(dgate_ref[:, pl.ds(s, n)], wg_ref[pl.ds(s, n), :]) + _dot(dup_ref[:, pl.ds(s, n)], wu_ref[pl.ds(s, n), :])
                df = part if df is None else df + part
                yield
            df_s.at[slot][...] = df

        def last(slot):
            hat, rstd = _rms_stats(h1pp_ref[...])
            dh1, dg3 = _rms_bwd(hat, rstd, g3_ref[...], df_s.at[slot][...])
            dg3_ref[...] += dg3
            dh1_ref[...] = dh2_s.at[slot][...] + dh1

        def emit(parity, with_forward, with_backward, with_last):
            fwd = forward(parity) if with_forward else iter(())
            bwd = backward(1 - parity) if with_backward else iter(())
            next(fwd, None)
            if with_last:
                last(parity)
            for _ in range(FFN_BACKWARD_LAG):
                next(fwd, None)
            alive = True
            while alive:
                alive = next(bwd, True) is None
                alive = (next(fwd, True) is None) or alive

        @pl.when(i == 0)
        def _():
            for r in (loss_ref, dg3_ref, dg4_ref, *slots):
                r[...] = jnp.zeros_like(r)

        @pl.when(i < nt)
        def _():
            emit(i % 2, True, True, True)

        @pl.when(i == nt)
        def _():
            emit(nt % 2, False, True, True)

        @pl.when(i == nt + 1)
        def _():
            emit((nt + 1) % 2, False, False, True)

    cur = lambda i: (jnp.minimum(i, nt - 1), 0)
    prev = lambda i: (jnp.clip(i - 1, 0, nt - 1), 0)
    prev2 = lambda i: (jnp.clip(i - 2, 0, nt - 1), 0)
    return pl.pallas_call(
        body, name="ffn_forward_backward", grid=(nt + 2,),
        in_specs=[pl.BlockSpec((tm, d), cur), pl.BlockSpec((tm, d), prev2), pl.BlockSpec((tm, d), cur), _const(g3.shape),
                  _resident(w_gate.shape), _resident(w_up.shape), _resident(w_down.shape), _const(g4.shape)],
        out_specs=[pl.BlockSpec((tm, d), cur), pl.BlockSpec((tm, ff), cur), pl.BlockSpec((tm, d), cur), pl.BlockSpec((tm, ff), prev),
                   pl.BlockSpec((tm, ff), prev), pl.BlockSpec((tm, d), prev2), _const((8, 128)), _const(g3.shape), _const(g4.shape)],
        out_shape=[jax.ShapeDtypeStruct((t, d), BF16), jax.ShapeDtypeStruct((t, ff), BF16), jax.ShapeDtypeStruct((t, d), BF16),
                   jax.ShapeDtypeStruct((t, ff), BF16), jax.ShapeDtypeStruct((t, ff), BF16), jax.ShapeDtypeStruct((t, d), F32),
                   jax.ShapeDtypeStruct((8, 128), F32), jax.ShapeDtypeStruct(g3.shape, F32), jax.ShapeDtypeStruct(g4.shape, F32)],
        scratch_shapes=[pltpu.VMEM((2, tm, ff), BF16)] * 2 + [pltpu.VMEM((2, tm, d), BF16)] + [pltpu.VMEM((2, tm, d), F32)] * 2,
        compiler_params=_params("arbitrary"),
    )(h1, h1, target, g3, w_gate, w_up, w_down, g4)


def _ffn_weight_grads(f, dd, dgate, dup, act):
    t, d = f.shape
    ff = dgate.shape[1]
    tm = min(TM_WGRAD, t)
    nt = t // tm
    fc = ff // FF_CHUNKS
    assert FF_CHUNKS == 2

    def body(f_ref, dd_ref, dgate_ref, dup_ref, act_ref, dwg_ref, dwu_ref, dwd_ref, *rest):
        landing, (acc_g, acc_u, acc_d, stage, sem) = rest[:2], rest[2:7]
        start, finish = _core_exchange_ops([dwg_ref, dwd_ref], landing, 0, *rest[7:])
        c, i = pl.program_id(0), pl.program_id(1)
        pl.when((c == 1) & (i == 0))(start)

        @pl.when(i == 0)
        def _():
            acc_g[...] = jnp.zeros_like(acc_g)
            acc_u[...] = jnp.zeros_like(acc_u)
            acc_d[...] = jnp.zeros_like(acc_d)

        fv = f_ref[...]
        acc_g[...] += _dot_tn(fv, dgate_ref[...])
        acc_u[...] += _dot_tn(fv, dup_ref[...])
        acc_d[...] += _dot_tn(act_ref[...], dd_ref[...])

        @pl.when(i == nt - 1)
        def _():
            rows = pl.ds(pl.multiple_of(c * fc, 16), fc)
            copies = []
            for k, (acc, out, transposed) in enumerate(((acc_d, dwd_ref, False), (acc_g, dwg_ref, True), (acc_u, dwu_ref, True))):
                if k >= 2:
                    copies[k - 2].wait()
                stage[k % 2] = (acc[...].T if transposed else acc[...]).astype(BF16)
                copies.append(pltpu.make_async_copy(stage.at[k % 2], out.at[rows, :], sem.at[k % 2]))
                copies[k].start()
            copies[-2].wait()
            copies[-1].wait()

        pl.when((c == 1) & (i == nt - 1))(finish)

    row = lambda c, i: (i, 0)
    col = lambda c, i: (i, c)
    out = pl.pallas_call(
        body, name="ffn_weight_grads", grid=(FF_CHUNKS, nt),
        in_specs=[pl.BlockSpec((tm, d), row), pl.BlockSpec((tm, d), row), pl.BlockSpec((tm, fc), col), pl.BlockSpec((tm, fc), col),
                  pl.BlockSpec((tm, fc), col)],
        out_specs=[ANY] * 5,
        out_shape=[jax.ShapeDtypeStruct((ff, d), BF16)] * 3 + [jax.ShapeDtypeStruct((N_DEV, ff // N_DEV, d), BF16)] * 2,
        scratch_shapes=[pltpu.VMEM((d, fc), F32), pltpu.VMEM((d, fc), F32), pltpu.VMEM((fc, d), F32), pltpu.VMEM((2, fc, d), BF16),
                        pltpu.SemaphoreType.DMA((2,))] + _core_exchange_sems(2),
        compiler_params=_params("arbitrary", "arbitrary"),
    )(f, dd, dgate, dup, act)
    return out[:3], [out[3], None, out[4]]


def _adamw(w, g, m, v):
    m = ADAM_B1 * m + (1.0 - ADAM_B1) * g
    v = ADAM_B2 * v + (1.0 - ADAM_B2) * (g * g)
    m_hat = m / (1.0 - ADAM_B1 ** ADAM_STEP)
    v_hat = v / (1.0 - ADAM_B2 ** ADAM_STEP)
    return -ADAM_LR * (m_hat / (jnp.sqrt(v_hat) + ADAM_EPS) + ADAM_WD * w), m, v


def _sum_slabs(ref):
    total = ref[0].astype(F32)
    for i in range(1, ref.shape[0]):
        total = total + ref[i].astype(F32)
    return total


def _adamw_rows(r, c):
    tr = r
    for cand in range(8, r, 8):
        if r % cand == 0 and cand * c <= ADAMW_BLOCK_ELEMS:
            tr = cand
    return r if r * c <= ADAMW_BLOCK_ELEMS else tr


def _reduce_adamw_carrying(parts, ws, ms, vs, to_reduce, to_exchange, whole, name):
    k, nr, nx = len(ws), len(to_reduce), len(to_exchange)
    r, c = ws[0].shape if k else (8, 128)
    tr = _adamw_rows(r, c)
    steps = r // tr
    travels = nr + nx > 0
    nd = list(whole).count(False)
    assert list(whole) == [False] * nd + [True] * (nx - nd)
    chip_slabs = [jax.ShapeDtypeStruct((N_CHIP, *a.shape[1:]), a.dtype) for a in to_reduce]

    def body(*refs):
        p_refs, w_refs, m_refs, v_refs = (refs[a * k:(a + 1) * k] for a in range(4))
        refs = refs[4 * k:]
        reduced_in, sent, refs = refs[:nr], refs[nr:nr + nx], refs[nr + nx:]
        outs, pairs, sums, landed, refs = refs[:4 * k], refs[4 * k:4 * k + nr], refs[4 * k + nr:4 * k + 2 * nr], \
            refs[4 * k + 2 * nr:4 * k + 2 * nr + nx], refs[4 * k + 2 * nr + nx:]
        mine_v, pair_v, sum_v, refs = refs[:nr], refs[nr:2 * nr], refs[2 * nr:3 * nr], refs[3 * nr:]
        if travels:
            reduce_ops = _pair_then_chip_ops(reduced_in, pairs, sums, mine_v, pair_v, sum_v, *refs[:7])
            direct_ops = _exchange_ops(sent[:nd], landed[:nd], [False] * nd, *refs[7:10])
            gather_ops = _gather_ops(sent[nd:], landed[nd:], *refs[10:13])

            @pl.when(pl.program_id(0) == 0)
            def _():
                direct_ops[0]()
                gather_ops[0]()
                reduce_ops[0]()

        for a in range(k):
            g = _sum_slabs(p_refs[a])
            outs[4 * a][...] = g
            outs[4 * a + 1][...], outs[4 * a + 2][...], outs[4 * a + 3][...] = _adamw(w_refs[a][...], g, m_refs[a][...], v_refs[a][...])

        if travels:
            @pl.when(pl.program_id(0) == steps - 1)
            def _():
                gather_ops[1]()
                reduce_ops[1]()
                gather_ops[2]()
                direct_ops[1]()

    blk = pl.BlockSpec((tr, c), lambda i: (i, 0))
    out = pl.pallas_call(
        body, name=name, grid=(steps,),
        in_specs=[pl.BlockSpec((N_DEV, tr, c), lambda i: (0, i, 0))] * k + [blk] * (3 * k) + [ANY] * (nr + nx),
        out_specs=[blk] * (4 * k) + [ANY] * (2 * nr + nx),
        out_shape=[jax.ShapeDtypeStruct((r, c), F32)] * (4 * k) + chip_slabs + chip_slabs
        + [jax.ShapeDtypeStruct((N_DEV, *a.shape) if w else a.shape, a.dtype) for a, w in zip(to_exchange, whole)],
        scratch_shapes=([pltpu.VMEM(a.shape, a.dtype) for a in chip_slabs] * 3 + _pair_then_chip_sems(nr) + _exchange_sems(nd)
                        + _exchange_sems(nx - nd) if travels else []),
        compiler_params=_params("arbitrary"),
    )(*parts, *ws, *ms, *vs, *to_reduce, *to_exchange)
    return [tuple(out[4 * a:4 * a + 4]) for a in range(k)], out[4 * k + nr:4 * k + 2 * nr], out[4 * k + 2 * nr:]


def _reduce_adamw(parts, w, m, v, name):
    r, c = w.shape
    tr = _adamw_rows(r, c)

    def body(p_ref, w_ref, m_ref, v_ref, g_out, d_out, m_out, v_out):
        g = _sum_slabs(p_ref)
        g_out[...] = g
        d_out[...], m_out[...], v_out[...] = _adamw(w_ref[...], g, m_ref[...], v_ref[...])

    blk = pl.BlockSpec((tr, c), lambda i: (i, 0))
    return pl.pallas_call(
        body, name=name, grid=(r // tr,),
        in_specs=[pl.BlockSpec((parts.shape[0], tr, c), lambda i: (0, i, 0)), blk, blk, blk],
        out_specs=[blk] * 4, out_shape=[jax.ShapeDtypeStruct((r, c), F32)] * 4,
        compiler_params=_params("arbitrary"),
    )(parts, w, m, v)


def _reduce_adamw_small(parts, ws, ms, vs, loss_parts):
    n = len(parts)

    def body(*refs):
        p_refs, w_refs, m_refs, v_refs = (refs[k * n:(k + 1) * n] for k in range(4))
        outs = refs[4 * n + 1:]
        outs[4 * n][...] = _sum_slabs(refs[4 * n])
        for a in range(n):
            g = _sum_slabs(p_refs[a])
            outs[4 * a][...] = g
            outs[4 * a + 1][...], outs[4 * a + 2][...], outs[4 * a + 3][...] = _adamw(w_refs[a][...], g, m_refs[a][...], v_refs[a][...])

    out = pl.pallas_call(
        body, name="adamw_replicated",
        out_shape=[jax.ShapeDtypeStruct(w.shape, F32) for w in ws for _ in range(4)] + [jax.ShapeDtypeStruct(loss_parts.shape[1:], F32)],
        compiler_params=pltpu.CompilerParams(vmem_limit_bytes=VMEM_LIMIT_BYTES),
    )(*parts, *ws, *ms, *vs, loss_parts)
    return [tuple(out[4 * a:4 * a + 4]) for a in range(n)], out[4 * n]


def kernel(x, meta_tokens, norm_mix_pre, w_in, conv_w, pool_w, pool_scale, w_out, norm_mix_post, norm_ffn_pre, w_gate, w_up, w_down, norm_ffn_post, loss_target, m_meta_tokens, m_norm_mix_pre, m_w_in, m_conv_w, m_pool_w, m_pool_scale, m_w_out, m_norm_mix_post, m_norm_ffn_pre, m_w_gate, m_w_up, m_w_down, m_norm_ffn_post, v_meta_tokens, v_norm_mix_pre, v_w_in, v_conv_w, v_pool_w, v_pool_scale, v_w_out, v_norm_mix_post, v_norm_ffn_pre, v_w_gate, v_w_up, v_w_down, v_norm_ffn_post):
    n_seq, seq, d = x.shape
    x2d = x.reshape(n_seq * seq, d)
    target = loss_target.reshape(n_seq * seq, d)

    t_ = lambda a: jnp.swapaxes(a[0], 0, 1)
    pw, ps = pool_w[0], pool_scale

    (h1, z, m, pooled, mixed), (win_b, wout_b, meta, conv, a_meta, z_meta), ffn_slabs = _gather_and_mixer_forward(
        x2d, [w_in[0], w_out[0], meta_tokens, conv_w[0]], [t_(w_gate), t_(w_up), w_down[0]], norm_mix_pre, pw, ps, norm_mix_post, n_seq)
    wg_b, wu_b, wd_b = (s.reshape(-1, d) for s in ffn_slabs)
    f, act, dd, dgate, dup, dh1, loss_sum, dg3, dg4 = _ffn_forward_backward(h1, target, norm_ffn_pre, wg_b, wu_b, wd_b, norm_ffn_post)
    ffn_grads, landing = _ffn_weight_grads(f, dd, dgate, dup, act)
    (gx, dwin, dwout, dg1, dg2, dconv, dpw, dps, dmeta), ffn_parts = _mixer_backward(
        x2d, dh1, m, z, pooled, mixed, meta, a_meta, z_meta, norm_mix_pre, win_b, conv, pw, ps, wout_b, norm_mix_post, n_seq,
        ffn_grads, landing)

    dmeta_s = jnp.transpose(dmeta.reshape(N_META, N_DEV, -1), (1, 0, 2))
    dconv_s = jnp.transpose(dconv.reshape(CONV_WIDTH, N_DEV, -1), (1, 0, 2))
    _, (win_parts, wout_parts), last = _reduce_adamw_carrying(
        [], [], [], [], [dwin, dwout.reshape(N_DEV, -1, d)], [dmeta_s, dconv_s, dg1, dg2, dg3, dg4, dpw.astype(BF16), dps, loss_sum],
        [False] * 2 + [True] * 7, "exchange_rest")
    ffn_res, _, _ = _reduce_adamw_carrying(
        ffn_parts, [t_(w_gate), t_(w_up), w_down[0]], [t_(m_w_gate), t_(m_w_up), m_w_down[0]], [t_(v_w_gate), t_(v_w_up), v_w_down[0]],
        [], [], [], "adamw_ffn")
    replicated = last[2:8]

    names = ["meta_tokens", "norm_mix_pre", "w_in", "conv_w", "pool_w", "pool_scale", "w_out", "norm_mix_post", "norm_ffn_pre", "w_gate",
             "w_up", "w_down", "norm_ffn_post"]
    res = {"w_gate": tuple(jnp.swapaxes(o, 0, 1)[None] for o in ffn_res[0]),
           "w_up": tuple(jnp.swapaxes(o, 0, 1)[None] for o in ffn_res[1]), "w_down": tuple(o[None] for o in ffn_res[2])}
    for nm, parts, w, m_, v_ in (("w_in", win_parts, w_in, m_w_in, v_w_in), ("w_out", wout_parts, w_out, m_w_out, v_w_out),
                                 ("conv_w", last[1], conv_w, m_conv_w, v_conv_w)):
        res[nm] = tuple(o[None] for o in _reduce_adamw(parts, w[0], m_[0], v_[0], "adamw_" + nm))
    res["meta_tokens"] = tuple(_reduce_adamw(last[0], meta_tokens, m_meta_tokens, v_meta_tokens, "adamw_meta_tokens"))
    small, loss = _reduce_adamw_small(
        replicated, [norm_mix_pre, norm_mix_post, norm_ffn_pre, norm_ffn_post, pool_w[0], pool_scale],
        [m_norm_mix_pre, m_norm_mix_post, m_norm_ffn_pre, m_norm_ffn_post, m_pool_w[0], m_pool_scale],
        [v_norm_mix_pre, v_norm_mix_post, v_norm_ffn_pre, v_norm_ffn_post, v_pool_w[0], v_pool_scale], last[8])
    for nm, r in zip(["norm_mix_pre", "norm_mix_post", "norm_ffn_pre", "norm_ffn_post", "pool_w", "pool_scale"], small):
        res[nm] = tuple(o[None] for o in r) if nm == "pool_w" else r

    return (loss[0, 0], gx.reshape(n_seq, seq, d), *[res[nm][0] for nm in names], *[res[nm][1] for nm in names],
            *[res[nm][2] for nm in names], *[res[nm][3] for nm in names])
```

```python
import jax
import jax.numpy as jnp
from jax import lax
from jax.experimental import pallas as pl
from jax.experimental.pallas import tpu as pltpu

F32, BF16 = jnp.float32, jnp.bfloat16
RMS_EPS = 1e-6
N_META = 16
CONV_WIDTH = 3
POOL_WINDOWS = (2, 4, 8, 16)
POOL_GROUP = 128
HALO = 16
N_DEV = 8
MESH_AXES = ("x", "y", "c")
MESH = pl.DeviceIdType.MESH
VMEM_LIMIT_BYTES = 56 * 1024 * 1024
ADAMW_BLOCK_ELEMS = 64 * 1024
TM_MIX = 512
TM_FFN = 256
FFN_CHUNK = 512
FFN_BACKWARD_LAG = 3
TM_WGRAD = 1024
FF_CHUNKS = 2

ADAM_LR, ADAM_B1, ADAM_B2, ADAM_EPS, ADAM_WD, ADAM_STEP = 0.001, 0.9, 0.999, 1e-08, 0.01, 10


def _dot(a, b):
    return jnp.dot(a, b, preferred_element_type=F32)


def _dot_nt(a, b):
    return lax.dot_general(a, b, (((1,), (1,)), ((), ())), preferred_element_type=F32)


def _dot_tn(a, b):
    return lax.dot_general(a, b, (((0,), (0,)), ((), ())), preferred_element_type=F32)


def _rms_stats(h):
    rstd = lax.rsqrt(jnp.mean(h * h, axis=-1, keepdims=True) + RMS_EPS)
    return h * rstd, rstd


def _rms_bwd(hat, rstd, g, dy):
    gdy = dy * g
    proj = jnp.mean(gdy * hat, axis=-1, keepdims=True)
    return rstd * (gdy - hat * proj), jnp.sum(dy * hat, axis=0, keepdims=True)


def _params(*semantics):
    return pltpu.CompilerParams(dimension_semantics=semantics or None, vmem_limit_bytes=VMEM_LIMIT_BYTES)


def _resident(shape):
    zeros = (0,) * len(shape)
    return pl.BlockSpec(shape, lambda *_: zeros, pipeline_mode=pl.Buffered(1))


def _const(shape):
    zeros = (0,) * len(shape)
    return pl.BlockSpec(shape, lambda *_: zeros)


ANY = pl.BlockSpec(memory_space=pl.ANY)


def _my_place():
    x, y, c = (lax.axis_index(a) for a in MESH_AXES)
    return x, y, c


def _exchange_sems(n):
    return [pltpu.SemaphoreType.DMA((n, N_DEV - 1)), pltpu.SemaphoreType.DMA((n, N_DEV - 1)), pltpu.SemaphoreType.DMA((n,))]


def _gather_ops(srcs, outs, send_sems, recv_sems, local_sems, core_major=False):
    n = len(srcs)
    x, y, c = _my_place()
    me, sibling = (x, y, c), (x, y, 1 - c)
    chips = [(1 - x, y), (x, 1 - y), (1 - x, 1 - y)]

    def slab(px, py, pc):
        return 4 * pc + 2 * px + py if core_major else 4 * px + 2 * py + pc

    def copy(a, k, block, to, src=None):
        dst = outs[a].at[slab(*block)]
        return pltpu.make_async_remote_copy(
            src_ref=dst if src is None else src, dst_ref=dst, send_sem=send_sems.at[a, k], recv_sem=recv_sems.at[a, k],
            device_id=to, device_id_type=MESH)

    def mine(a):
        return pltpu.make_async_copy(srcs[a], outs[a].at[slab(*me)], local_sems.at[a])

    def first(a):
        return [copy(a, 0, me, sibling, src=srcs[a])] + [copy(a, 1 + j, me, (*chip, c), src=srcs[a]) for j, chip in enumerate(chips)]

    def passed(a, j):
        return copy(a, 4 + j, (*chips[j], c), sibling)

    def start():
        for a in range(n):
            mine(a).start()
            for cp in first(a):
                cp.start()

    def forward(which=(0, 1, 2)):
        for j in which:
            for a in range(n):
                copy(a, 1 + j, (*chips[j], c), me).wait_recv()
                passed(a, j).start()

    def finish():
        for a in range(n):
            copy(a, 0, sibling, me).wait_recv()
            for j, chip in enumerate(chips):
                copy(a, 4 + j, (*chip, 1 - c), me).wait_recv()
        for a in range(n):
            for cp in first(a) + [passed(a, j) for j in range(len(chips))]:
                cp.wait_send()
            mine(a).wait()

    return start, forward, finish


def _exchange_ops(ins, outs, whole, send_sems, recv_sems, local_sems):
    n = len(ins)
    x, y, c = _my_place()
    me = 4 * x + 2 * y + c

    def src(a, i):
        return ins[a] if whole[a] else ins[a].at[i]

    def mine(a):
        return pltpu.make_async_copy(src(a, me), outs[a].at[me], local_sems.at[a])

    def send(a, k):
        to = (me + k) % N_DEV
        return pltpu.make_async_remote_copy(
            src_ref=src(a, to), dst_ref=outs[a].at[me], send_sem=send_sems.at[a, k - 1], recv_sem=recv_sems.at[a, k - 1],
            device_id=(to // 4, (to // 2) % 2, to % 2), device_id_type=MESH)

    def landed(a, k):
        frm = (me + N_DEV - k) % N_DEV
        return pltpu.make_async_remote_copy(
            src_ref=src(a, frm), dst_ref=outs[a].at[frm], send_sem=send_sems.at[a, k - 1], recv_sem=recv_sems.at[a, k - 1],
            device_id=(x, y, c), device_id_type=MESH)

    def start():
        for a in range(n):
            mine(a).start()
            for k in range(1, N_DEV):
                send(a, k).start()

    def finish():
        for a in range(n):
            for k in range(1, N_DEV):
                landed(a, k).wait_recv()
        for a in range(n):
            for k in range(1, N_DEV):
                send(a, k).wait_send()
            mine(a).wait()

    return start, finish


def _core_exchange_sems(n):
    return [pltpu.SemaphoreType.DMA((n, 4)), pltpu.SemaphoreType.DMA((n, N_DEV)), pltpu.SemaphoreType.DMA((n,))]


def _core_exchange_ops(ins, outs, to_core, send_sems, recv_sems, local_sems):
    n = len(ins)
    x, y, c = _my_place()
    me = 4 * x + 2 * y + c
    others = [(0, 1), (1, 0), (1, 1)]

    def slab(a, p):
        if len(ins[a].shape) == len(outs[a].shape):
            return ins[a].at[p]
        rows = outs[a].shape[1]
        return ins[a].at[pl.ds(pl.multiple_of(p * rows, 16), rows), :]

    def send(a, dx, dy):
        tx, ty = (x + dx) % 2, (y + dy) % 2
        return pltpu.make_async_remote_copy(
            src_ref=slab(a, 4 * to_core + 2 * tx + ty), dst_ref=outs[a].at[me], send_sem=send_sems.at[a, 2 * dx + dy],
            recv_sem=recv_sems.at[a, 2 * (2 * dx + dy) + c], device_id=(tx, ty, to_core), device_id_type=MESH)

    def mine(a):
        return pltpu.make_async_copy(slab(a, 4 * to_core + 2 * x + y), outs[a].at[me], local_sems.at[a])

    def landed(a, dx, dy, sc):
        frm = 4 * ((x + dx) % 2) + 2 * ((y + dy) % 2) + sc
        return pltpu.make_async_remote_copy(
            src_ref=slab(a, 0), dst_ref=outs[a].at[frm], send_sem=send_sems.at[a, 0], recv_sem=recv_sems.at[a, 2 * (2 * dx + dy) + sc],
            device_id=(x, y, c), device_id_type=MESH)

    def start():
        for a in range(n):
            for dx, dy in others:
                send(a, dx, dy).start()
            pl.when(c == to_core)(mine(a).start)
            pl.when(c != to_core)(send(a, 0, 0).start)

    def finish():
        @pl.when(c == to_core)
        def _():
            for a in range(n):
                for dx, dy in [(0, 0)] + others:
                    for sc in (0, 1):
                        if (dx, dy, sc) != (0, 0, to_core):
                            landed(a, dx, dy, sc).wait_recv()
            for a in range(n):
                mine(a).wait()

        @pl.when(c != to_core)
        def _():
            for a in range(n):
                send(a, 0, 0).wait_send()

        for a in range(n):
            for dx, dy in others:
                send(a, dx, dy).wait_send()

    return start, finish


N_CHIP = 4


def _pair_then_chip_sems(n):
    return [pltpu.SemaphoreType.DMA((n, N_CHIP)) for _ in range(6)] + [pltpu.SemaphoreType.DMA((n,))]


def _pair_then_chip_ops(ins, pairs, outs, mine_v, pair_v, sum_v, pair_send, pair_recv, chip_send, chip_recv, load_a, load_b, own_sem):
    n = len(ins)
    x, y, c = _my_place()
    chip = 2 * x + y
    chips = [(0, 0), (0, 1), (1, 0), (1, 1)]
    others = [(0, 1), (1, 0), (1, 1)]

    def to_sibling(a, j):
        px, py = chips[j]
        return pltpu.make_async_remote_copy(
            src_ref=ins[a].at[4 * px + 2 * py + 1 - c], dst_ref=pairs[a].at[j], send_sem=pair_send.at[a, j], recv_sem=pair_recv.at[a, j],
            device_id=(x, y, 1 - c), device_id_type=MESH)

    def spread(a, dx, dy):
        tx, ty = (x + dx) % 2, (y + dy) % 2
        return pltpu.make_async_remote_copy(
            src_ref=sum_v[a].at[2 * tx + ty], dst_ref=outs[a].at[chip], send_sem=chip_send.at[a, 2 * dx + dy],
            recv_sem=chip_recv.at[a, 2 * dx + dy], device_id=(tx, ty, c), device_id_type=MESH)

    def landed(a, dx, dy):
        frm = 2 * ((x + dx) % 2) + (y + dy) % 2
        return pltpu.make_async_remote_copy(
            src_ref=sum_v[a].at[0], dst_ref=outs[a].at[frm], send_sem=chip_send.at[a, 0], recv_sem=chip_recv.at[a, 2 * dx + dy],
            device_id=(x, y, c), device_id_type=MESH)

    def own(a):
        return pltpu.make_async_copy(sum_v[a].at[chip], outs[a].at[chip], own_sem.at[a])

    def pair():
        loads = []
        for a in range(n):
            for j, (px, py) in enumerate(chips):
                to_sibling(a, j).start()
                loads.append(pltpu.make_async_copy(ins[a].at[4 * px + 2 * py + c], mine_v[a].at[j], load_a.at[a, j]))
                loads[-1].start()
        for a in range(n):
            for j in range(N_CHIP):
                to_sibling(a, j).wait_recv()
                loads.append(pltpu.make_async_copy(pairs[a].at[j], pair_v[a].at[j], load_b.at[a, j]))
                loads[-1].start()
        for cp in loads:
            cp.wait()
        for a in range(n):
            sum_v[a][...] = (mine_v[a][...].astype(F32) + pair_v[a][...].astype(F32)).astype(sum_v[a].dtype)

    def start():
        pair()
        for a in range(n):
            own(a).start()
            for dx, dy in others:
                spread(a, dx, dy).start()

    def finish():
        for a in range(n):
            for dx, dy in others:
                landed(a, dx, dy).wait_recv()
        for a in range(n):
            for dx, dy in others:
                spread(a, dx, dy).wait_send()
            for j in range(N_CHIP):
                to_sibling(a, j).wait_send()
            own(a).wait()

    return start, finish


def _window_sum(x, win, ahead):
    n = x.shape[0]
    span = 1
    while span < win:
        x = x + pltpu.roll(x, n - span if ahead else span, 0)
        span *= 2
    return x


def _conv_branch(z, ext_u, conv_ref, tm):
    c_w = z.shape[1] // 4
    b, c, v = z[:, :c_w], z[:, c_w:2 * c_w], z[:, 2 * c_w:3 * c_w]
    u = c * v
    ext_u[pl.ds(HALO, tm), :] = u
    u1 = ext_u[pl.ds(HALO - 1, tm), :]
    u2 = ext_u[pl.ds(HALO - 2, tm), :]
    yc = conv_ref[pl.ds(2, 1), :] * u + conv_ref[pl.ds(1, 1), :] * u1 + conv_ref[pl.ds(0, 1), :] * u2
    return b, c, v, u, u1, u2, yc


def _pool_branch(p, ext_p, pool_w_ref, tm):
    ext_p[pl.ds(HALO, tm), :] = p
    pooled, mixed = [], []
    for g, win in enumerate(POOL_WINDOWS):
        s = _window_sum(ext_p[:, pl.ds(POOL_GROUP * g, POOL_GROUP)], win, ahead=False)[HALO:HALO + tm, :]
        pooled.append((s * (1.0 / win) - p[:, POOL_GROUP * g:POOL_GROUP * (g + 1)]).astype(BF16))
        mixed.append(_dot(pooled[-1], pool_w_ref[g].astype(BF16)))
    return pooled, mixed


def _gather_and_mixer_forward(x2d, mixer_shards, ffn_shards, g1, pool_w, pool_scale, g2, n_seq):
    t, d = x2d.shape
    zs, rs, ms, cs = mixer_shards[0].shape[1], mixer_shards[1].shape[0], mixer_shards[2].shape[1], mixer_shards[3].shape[1]
    zw, cw = N_DEV * zs, N_DEV * cs
    s = t // n_seq
    tm = min(TM_MIX, s)
    nj = s // tm
    n1, n2 = len(mixer_shards), len(ffn_shards)
    dtypes = [BF16, BF16, F32, F32] + [BF16] * n2
    shards = list(mixer_shards) + list(ffn_shards)

    def body(x_ref, *rest):
        shard_refs, (g1_ref, pw_ref, ps_ref, g2_ref), rest = rest[:n1 + n2], rest[n1 + n2:n1 + n2 + 4], rest[n1 + n2 + 4:]
        (h1_ref, z_ref, m_ref, pooled_ref, mixed_ref, win_o, wout_o, meta_o, conv_o, am_o, zm_o), rest = rest[:11], rest[11:]
        slabs, rest = rest[:n1 + n2], rest[n1 + n2:]
        stages, rest = rest[:n1 + n2], rest[n1 + n2:]
        win_v, wout_v, meta_v, conv_v, ext_u, ext_p, sem = rest[:7]
        first = _gather_ops(stages[:n1], slabs[:n1], *rest[7:10])
        later = _gather_ops(stages[n1:], slabs[n1:], *rest[10:13], core_major=True)

        @pl.when((pl.program_id(0) == 0) & (pl.program_id(1) == 0))
        def _():
            for src, dst in zip(shard_refs, stages):
                dst[...] = src[...].astype(dst.dtype)
            first[0]()
            later[0]()
            first[1]()
            first[2]()
            copies = [pltpu.make_async_copy(slabs[0].at[i], win_v.at[:, pl.ds(zs * i, zs)], sem.at[i]) for i in range(N_DEV)]
            copies += [pltpu.make_async_copy(slabs[1].at[i], wout_v.at[pl.ds(rs * i, rs), :], sem.at[N_DEV + i]) for i in range(N_DEV)]
            copies += [pltpu.make_async_copy(slabs[2], meta_v, sem.at[2 * N_DEV]), pltpu.make_async_copy(slabs[3], conv_v, sem.at[2 * N_DEV + 1])]
            for cp in copies:
                cp.start()
            for cp in copies:
                cp.wait()
            copies = [pltpu.make_async_copy(win_v, win_o, sem.at[0]), pltpu.make_async_copy(wout_v, wout_o, sem.at[1])]
            for cp in copies:
                cp.start()
            for i in range(N_DEV):
                meta_o[:, pl.ds(ms * i, ms)] = meta_v[i]
                conv_o[:, pl.ds(cs * i, cs)] = conv_v[i]
            hat, _ = _rms_stats(meta_o[...])
            a = (hat * g1_ref[...]).astype(BF16)
            am_o[...] = a
            zm_o[...] = _dot(a, win_v[...])
            for cp in copies:
                cp.wait()

        @pl.when(pl.program_id(1) == 0)
        def _():
            zm = zm_o[...]
            ext_u[pl.ds(0, HALO), :] = zm[:, cw:2 * cw] * zm[:, 2 * cw:3 * cw]
            ext_p[pl.ds(0, HALO), :] = zm[:, 3 * cw:]

        h0 = x_ref[...]
        hat, _ = _rms_stats(h0)
        z = _dot((hat * g1_ref[...]).astype(BF16), win_v[...])
        z_ref[...] = z.astype(BF16)
        b, _, _, _, _, _, yc = _conv_branch(z, ext_u, conv_o, tm)
        pooled, mixed = _pool_branch(z[:, 3 * cw:], ext_p, pw_ref, tm)
        pooled_ref[...] = jnp.concatenate(pooled, axis=1)
        mixed_ref[...] = jnp.concatenate(mixed, axis=1).astype(BF16)
        ps = ps_ref[...]
        y = [b * yc] + [mixed[g] * ps[:, POOL_GROUP * g:POOL_GROUP * (g + 1)] for g in range(len(POOL_WINDOWS))]
        m = _dot(jnp.concatenate(y, axis=1).astype(BF16), wout_v[...])
        m_ref[...] = m
        m_hat, _ = _rms_stats(m)
        h1_ref[...] = h0 + m_hat * g2_ref[...]
        ext_u[pl.ds(0, HALO), :] = ext_u[pl.ds(tm, HALO), :]
        ext_p[pl.ds(0, HALO), :] = ext_p[pl.ds(tm, HALO), :]

        @pl.when(pl.program_id(0) * nj + pl.program_id(1) == (3 * n_seq * nj) // 4)
        def _():
            later[1]((0, 1))

        @pl.when((pl.program_id(0) == n_seq - 1) & (pl.program_id(1) == nj - 1))
        def _():
            later[1]((2,))
            later[2]()

    row = lambda b, j: (b * nj + j, 0)
    vmem = pl.BlockSpec(memory_space=pltpu.VMEM)
    small = [(N_META, d), (CONV_WIDTH, cw), (N_META, d), (N_META, zw)]
    out = pl.pallas_call(
        body, name="gather_and_mixer_forward", grid=(n_seq, nj),
        in_specs=[pl.BlockSpec((tm, d), row)] + [vmem] * (n1 + n2)
        + [_const(g1.shape), _const(pool_w.shape), _const(pool_scale.shape), _const(g2.shape)],
        out_specs=[pl.BlockSpec((tm, d), row), pl.BlockSpec((tm, zw), row), pl.BlockSpec((tm, d), row), pl.BlockSpec((tm, cw), row),
                   pl.BlockSpec((tm, cw), row), ANY, ANY] + [_const(sh) for sh in small] + [ANY] * (n1 + n2),
        out_shape=[jax.ShapeDtypeStruct((t, d), F32), jax.ShapeDtypeStruct((t, zw), BF16), jax.ShapeDtypeStruct((t, d), F32),
                   jax.ShapeDtypeStruct((t, cw), BF16), jax.ShapeDtypeStruct((t, cw), BF16),
                   jax.ShapeDtypeStruct((d, zw), BF16), jax.ShapeDtypeStruct((d, d), BF16),
                   jax.ShapeDtypeStruct(small[0], F32), jax.ShapeDtypeStruct(small[1], F32), jax.ShapeDtypeStruct(small[2], BF16),
                   jax.ShapeDtypeStruct(small[3], F32)]
        + [jax.ShapeDtypeStruct((N_DEV, *a.shape), dt) for a, dt in zip(shards, dtypes)],
        scratch_shapes=[pltpu.VMEM(a.shape, dt) for a, dt in zip(shards, dtypes)]
        + [pltpu.VMEM((d, zw), BF16), pltpu.VMEM((d, d), BF16), pltpu.VMEM((N_DEV, N_META, ms), F32),
           pltpu.VMEM((N_DEV, CONV_WIDTH, cs), F32), pltpu.VMEM((tm + HALO, cw), F32), pltpu.VMEM((tm + HALO, cw), F32),
           pltpu.SemaphoreType.DMA((2 * N_DEV + 2,))] + _exchange_sems(n1) + _exchange_sems(n2),
        compiler_params=_params("arbitrary", "arbitrary"),
    )(x2d, *shards, g1, pool_w, pool_scale, g2)
    return out[:5], out[5:11], out[11 + n1:]


def _mixer_backward(x2d, dh1, m, z, pooled, mixed, meta, a_meta, z_meta, g1, w_in, conv_w, pool_w, pool_scale, w_out, g2, n_seq,
                    to_exchange, landing):
    t, d = x2d.shape
    zw = w_in.shape[1]
    cw = zw // 4
    s = t // n_seq
    tm = min(TM_MIX, s)
    nj = s // tm
    n_groups = len(POOL_WINDOWS)
    zs = zw // N_DEV
    nx = len(to_exchange)
    n_in = 17
    given = [k for k, a in enumerate(landing) if a is not None]
    fresh = [k for k, a in enumerate(landing) if a is None]

    def body(x_ref, dh1_ref, m_ref, z_ref, zprev_ref, pooled_ref, mixed_ref, meta_ref, am_ref, zm_ref, g1_ref, win_ref, conv_ref, pw_ref, ps_ref, wout_ref,
             g2_ref, *rest):
        sent, rest = rest[:nx], rest[nx + len(given):]
        gx_ref, dwin_ref, dwout_ref, dg1_ref, dg2_ref, dconv_ref, dpw_ref, dps_ref, dmeta_ref = rest[:9]
        landed, rest = rest[9:9 + nx], rest[9 + nx:]
        ext_u, ext_dyc, ext_dq, acc_win, acc_wout, dz_meta, stage16, sem = rest[:8]
        north = _core_exchange_ops(sent, landed, 1, *rest[8:11])
        south = _core_exchange_ops([sent[k] for k in fresh], [landed[k] for k in fresh], 0, *rest[11:14])

        def start():
            north[0]()
            south[0]()

        def finish():
            south[1]()
            north[1]()

        b_id, j = pl.program_id(0), pl.program_id(1)
        jr = nj - 1 - j
        pl.when((b_id == 0) & (j == 0))(start)

        @pl.when((b_id == 0) & (j == 0))
        def _():
            acc_win[...] = jnp.zeros_like(acc_win)
            acc_wout[...] = jnp.zeros_like(acc_wout)
            dz_meta[...] = jnp.zeros_like(dz_meta)
            for r in (dg1_ref, dg2_ref, dconv_ref, dpw_ref, dps_ref, dmeta_ref):
                r[...] = jnp.zeros_like(r)

        @pl.when(j == 0)
        def _():
            ext_dyc[pl.ds(tm, HALO), :] = jnp.zeros((HALO, cw), F32)
            ext_dq[pl.ds(tm, HALO), :] = jnp.zeros((HALO, cw), F32)

        zm = zm_ref[...]
        halo = jnp.where(jr == 0, zm, zprev_ref[...].astype(F32))
        ext_u[pl.ds(0, HALO), :] = halo[:, cw:2 * cw] * halo[:, 2 * cw:3 * cw]

        dh1v = dh1_ref[...]
        m_hat, m_rstd = _rms_stats(m_ref[...])
        dm, dg2 = _rms_bwd(m_hat, m_rstd, g2_ref[...], dh1v)
        dg2_ref[...] += dg2
        dm = dm.astype(BF16)
        dycat = _dot_nt(dm, wout_ref[...])

        b, c, v, u, u1, u2, yc = _conv_branch(z_ref[...].astype(F32), ext_u, conv_ref, tm)
        mixed = [mixed_ref[:, pl.ds(POOL_GROUP * g, POOL_GROUP)].astype(F32) for g in range(n_groups)]
        ps = ps_ref[...]
        y = [b * yc] + [mixed[g] * ps[:, POOL_GROUP * g:POOL_GROUP * (g + 1)] for g in range(n_groups)]
        ycat = jnp.concatenate(y, axis=1).astype(BF16)
        acc_wout[...] += _dot_tn(ycat, dm)

        dyconv = dycat[:, :cw]
        db = dyconv * yc
        dyc = dyconv * b
        ext_dyc[pl.ds(0, tm), :] = dyc
        du = (conv_ref[pl.ds(2, 1), :] * dyc + conv_ref[pl.ds(1, 1), :] * ext_dyc[pl.ds(1, tm), :]
              + conv_ref[pl.ds(0, 1), :] * ext_dyc[pl.ds(2, tm), :])
        dconv_ref[pl.ds(2, 1), :] += jnp.sum(dyc * u, axis=0, keepdims=True)
        dconv_ref[pl.ds(1, 1), :] += jnp.sum(dyc * u1, axis=0, keepdims=True)
        dconv_ref[pl.ds(0, 1), :] += jnp.sum(dyc * u2, axis=0, keepdims=True)

        dp = []
        for g, win in enumerate(POOL_WINDOWS):
            lanes = pl.ds(POOL_GROUP * g, POOL_GROUP)
            dypool = dycat[:, cw + POOL_GROUP * g:cw + POOL_GROUP * (g + 1)]
            dps_ref[:, lanes] += jnp.sum(dypool * mixed[g], axis=0, keepdims=True)
            dmixed = (dypool * ps[:, POOL_GROUP * g:POOL_GROUP * (g + 1)]).astype(BF16)
            dq = _dot_nt(dmixed, pw_ref[g].astype(BF16))
            dpw_ref[g] += _dot_tn(pooled_ref[:, lanes], dmixed)
            ext_dq[pl.ds(0, tm), lanes] = dq
            acc = _window_sum(ext_dq[:, lanes], win, ahead=True)[0:tm, :]
            dp.append(acc * (1.0 / win) - dq)

        dz = jnp.concatenate([db, du * v, du * c] + dp, axis=1).astype(BF16)
        da = _dot_nt(dz, win_ref[...])
        h0 = x_ref[...]
        hat0, rstd0 = _rms_stats(h0)
        g1 = g1_ref[...]
        acc_win[...] += _dot_tn((hat0 * g1).astype(BF16), dz)
        dh0, dg1 = _rms_bwd(hat0, rstd0, g1, da)
        dg1_ref[...] += dg1
        gx_ref[...] = dh1v + dh0

        ext_dyc[pl.ds(tm, HALO), :] = ext_dyc[pl.ds(0, HALO), :]
        ext_dq[pl.ds(tm, HALO), :] = ext_dq[pl.ds(0, HALO), :]

        @pl.when(jr == 0)
        def _():
            ext_dyc[pl.ds(tm - HALO, HALO), :] = jnp.zeros((HALO, cw), F32)
            ext_dq[pl.ds(tm - HALO, HALO), :] = jnp.zeros((HALO, cw), F32)
            du_m = (conv_ref[pl.ds(1, 1), :] * ext_dyc[pl.ds(tm - HALO + 1, HALO), :]
                    + conv_ref[pl.ds(0, 1), :] * ext_dyc[pl.ds(tm - HALO + 2, HALO), :])
            dp_m = []
            for g, win in enumerate(POOL_WINDOWS):
                lanes = pl.ds(POOL_GROUP * g, POOL_GROUP)
                acc = ext_dq[pl.ds(tm - HALO + 1, HALO), lanes]
                for k in range(2, win):
                    acc = acc + ext_dq[pl.ds(tm - HALO + k, HALO), lanes]
                dp_m.append(acc * (1.0 / win))
            dz_meta[...] += jnp.concatenate(
                [jnp.zeros((HALO, cw), F32), du_m * zm[:, 2 * cw:3 * cw], du_m * zm[:, cw:2 * cw]] + dp_m, axis=1)

        @pl.when((b_id == n_seq - 1) & (j == nj - 1))
        def _():
            dz_m = dz_meta[...].astype(BF16)
            acc_win[...] += _dot_tn(am_ref[...], dz_m)
            hat_m, rstd_m = _rms_stats(meta_ref[...])
            dmeta, dg1_m = _rms_bwd(hat_m, rstd_m, g1, _dot_nt(dz_m, win_ref[...]))
            dg1_ref[...] += dg1_m
            dmeta_ref[...] = dmeta
            pieces = [(acc_win, zs * i, dwin_ref.at[i]) for i in range(N_DEV)]
            pieces += [(acc_wout, zs * i, dwout_ref.at[:, pl.ds(zs * i, zs)]) for i in range(d // zs)]
            copies = []
            for k, (acc, col, dst) in enumerate(pieces):
                if k >= 2:
                    copies[k - 2].wait()
                stage16[k % 2] = acc[:, pl.ds(col, zs)].astype(BF16)
                copies.append(pltpu.make_async_copy(stage16.at[k % 2], dst, sem.at[k % 2]))
                copies[k].start()
            copies[-2].wait()
            copies[-1].wait()
            finish()

    row = lambda b, j: (b * nj + nj - 1 - j, 0)
    prev = lambda b, j: (jnp.maximum((b * s + (nj - 1 - j) * tm) // HALO - 1, 0), 0)
    small = [g1.shape, g2.shape, conv_w.shape, pool_w.shape, pool_scale.shape, meta.shape]
    out = pl.pallas_call(
        body, name="mixer_backward", grid=(n_seq, nj),
        in_specs=[pl.BlockSpec((tm, d), row), pl.BlockSpec((tm, d), row), pl.BlockSpec((tm, d), row), pl.BlockSpec((tm, zw), row),
                  pl.BlockSpec((HALO, zw), prev), pl.BlockSpec((tm, cw), row), pl.BlockSpec((tm, cw), row), _const(meta.shape), _const(a_meta.shape), _const(z_meta.shape), _const(g1.shape),
                  _resident(w_in.shape), _const(conv_w.shape), _const(pool_w.shape), _const(pool_scale.shape), _resident(w_out.shape),
                  _const(g2.shape)] + [ANY] * (nx + len(given)),
        out_specs=[pl.BlockSpec((tm, d), row), ANY, ANY] + [_const(sh) for sh in small] + [ANY] * nx,
        out_shape=[jax.ShapeDtypeStruct((t, d), F32), jax.ShapeDtypeStruct((N_DEV, d, zs), BF16),
                   jax.ShapeDtypeStruct(w_out.shape, BF16)] + [jax.ShapeDtypeStruct(sh, F32) for sh in small]
        + [jax.ShapeDtypeStruct((N_DEV, a.shape[0] // N_DEV, a.shape[1]), a.dtype) for a in to_exchange],
        input_output_aliases={n_in + nx + at: 9 + k for at, k in enumerate(given)},
        scratch_shapes=[pltpu.VMEM((tm + HALO, cw), F32)] * 3
        + [pltpu.VMEM(w_in.shape, F32), pltpu.VMEM(w_out.shape, F32), pltpu.VMEM((HALO, zw), F32), pltpu.VMEM((2, d, zs), BF16),
           pltpu.SemaphoreType.DMA((2,))] + _core_exchange_sems(nx) + _core_exchange_sems(len(fresh)),
        compiler_params=_params("arbitrary", "arbitrary"),
    )(x2d, dh1, m, z, z, pooled, mixed, meta, a_meta, z_meta, g1, w_in, conv_w, pool_w, pool_scale, w_out, g2, *to_exchange, *[landing[k] for k in given])
    return out[:9], out[9:]


def _ffn_forward_backward(h1, target, g3, w_gate, w_up, w_down, g4):
    t, d = h1.shape
    ff = w_gate.shape[0]
    tm = min(TM_FFN, t)
    nt = t // tm
    chunks = [(s, min(FFN_CHUNK, ff - s)) for s in range(0, ff, FFN_CHUNK)]

    def body(h1_ref, h1pp_ref, tgt_ref, g3_ref, wg_ref, wu_ref, wd_ref, g4_ref,
             f_ref, act_ref, dd_ref, dgate_ref, dup_ref, dh1_ref, loss_ref, dg3_ref, dg4_ref, *slots):
        gate_s, up_s, dd_s, dh2_s, df_s = slots
        i = pl.program_id(0)

        def forward(slot):
            h1v = h1_ref[...]
            hat, _ = _rms_stats(h1v)
            f = (hat * g3_ref[...]).astype(BF16)
            f_ref[...] = f
            s, n = chunks[0]
            gate, up = _dot_nt(f_ref[...], wg_ref[pl.ds(s, n), :]), _dot_nt(f_ref[...], wu_ref[pl.ds(s, n), :])
            yield
            down = None
            for k, (s, n) in enumerate(chunks):
                gate_s.at[slot][:, pl.ds(s, n)] = gate.astype(BF16)
                up_s.at[slot][:, pl.ds(s, n)] = up.astype(BF16)
                act = (gate * jax.nn.sigmoid(gate) * up).astype(BF16)
                act_ref[:, pl.ds(s, n)] = act
                if k + 1 < len(chunks):
                    s1, n1 = chunks[k + 1]
                    gate, up = _dot_nt(f_ref[...], wg_ref[pl.ds(s1, n1), :]), _dot_nt(f_ref[...], wu_ref[pl.ds(s1, n1), :])
                yield
                part = _dot(act_ref[:, pl.ds(s, n)], wd_ref[pl.ds(s, n), :])
                down = part if down is None else down + part
                yield
            d_hat, d_rstd = _rms_stats(down)
            g4 = g4_ref[...]
            err = h1v + d_hat * g4 - tgt_ref[...]
            loss_ref[...] += jnp.sum(err * err) * (0.5 / d)
            dh2 = err * (1.0 / d)
            dh2_s.at[slot][...] = dh2
            dd, dg4 = _rms_bwd(d_hat, d_rstd, g4, dh2)
            dg4_ref[...] += dg4
            dd = dd.astype(BF16)
            dd_ref[...] = dd
            dd_s.at[slot][...] = dd

        def backward(slot):
            s, n = chunks[0]
            dact = _dot_nt(dd_s.at[slot][...], wd_ref[pl.ds(s, n), :])
            yield
            df = None
            for k, (s, n) in enumerate(chunks):
                gate = gate_s.at[slot][:, pl.ds(s, n)].astype(F32)
                up = up_s.at[slot][:, pl.ds(s, n)].astype(F32)
                sig = jax.nn.sigmoid(gate)
                dup = (dact * (gate * sig)).astype(BF16)
                dgate = (dact * up * (sig * (1.0 + gate * (1.0 - sig)))).astype(BF16)
                dup_ref[:, pl.ds(s, n)] = dup
                dgate_ref[:, pl.ds(s, n)] = dgate
                if k + 1 < len(chunks):
                    s1, n1 = chunks[k + 1]
                    dact = _dot_nt(dd_s.at[slot][...], wd_ref[pl.ds(s1, n1), :])
                yield
                part = _dot(dgate_ref[:, pl.ds(s, n)], wg_ref[pl.ds(s, n), :]) + _dot(dup_ref[:, pl.ds(s, n)], wu_ref[pl.ds(s, n), :])
                df = part if df is None else df + part
                yield
            df_s.at[slot][...] = df

        def last(slot):
            hat, rstd = _rms_stats(h1pp_ref[...])
            dh1, dg3 = _rms_bwd(hat, rstd, g3_ref[...], df_s.at[slot][...])
            dg3_ref[...] += dg3
            dh1_ref[...] = dh2_s.at[slot][...] + dh1

        def emit(parity, with_forward, with_backward, with_last):
            fwd = forward(parity) if with_forward else iter(())
            bwd = backward(1 - parity) if with_backward else iter(())
            next(fwd, None)
            if with_last:
                last(parity)
            for _ in range(FFN_BACKWARD_LAG):
                next(fwd, None)
            alive = True
            while alive:
                alive = next(bwd, True) is None
                alive = (next(fwd, True) is None) or alive

        @pl.when(i == 0)
        def _():
            for r in (loss_ref, dg3_ref, dg4_ref, *slots):
                r[...] = jnp.zeros_like(r)

        @pl.when(i < nt)
        def _():
            emit(i % 2, True, True, True)

        @pl.when(i == nt)
        def _():
            emit(nt % 2, False, True, True)

        @pl.when(i == nt + 1)
        def _():
            emit((nt + 1) % 2, False, False, True)

    cur = lambda i: (jnp.minimum(i, nt - 1), 0)
    prev = lambda i: (jnp.clip(i - 1, 0, nt - 1), 0)
    prev2 = lambda i: (jnp.clip(i - 2, 0, nt - 1), 0)
    return pl.pallas_call(
        body, name="ffn_forward_backward", grid=(nt + 2,),
        in_specs=[pl.BlockSpec((tm, d), cur), pl.BlockSpec((tm, d), prev2), pl.BlockSpec((tm, d), cur), _const(g3.shape),
                  _resident(w_gate.shape), _resident(w_up.shape), _resident(w_down.shape), _const(g4.shape)],
        out_specs=[pl.BlockSpec((tm, d), cur), pl.BlockSpec((tm, ff), cur), pl.BlockSpec((tm, d), cur), pl.BlockSpec((tm, ff), prev),
                   pl.BlockSpec((tm, ff), prev), pl.BlockSpec((tm, d), prev2), _const((8, 128)), _const(g3.shape), _const(g4.shape)],
        out_shape=[jax.ShapeDtypeStruct((t, d), BF16), jax.ShapeDtypeStruct((t, ff), BF16), jax.ShapeDtypeStruct((t, d), BF16),
                   jax.ShapeDtypeStruct((t, ff), BF16), jax.ShapeDtypeStruct((t, ff), BF16), jax.ShapeDtypeStruct((t, d), F32),
                   jax.ShapeDtypeStruct((8, 128), F32), jax.ShapeDtypeStruct(g3.shape, F32), jax.ShapeDtypeStruct(g4.shape, F32)],
        scratch_shapes=[pltpu.VMEM((2, tm, ff), BF16)] * 2 + [pltpu.VMEM((2, tm, d), BF16)] + [pltpu.VMEM((2, tm, d), F32)] * 2,
        compiler_params=_params("arbitrary"),
    )(h1, h1, target, g3, w_gate, w_up, w_down, g4)


def _ffn_weight_grads(f, dd, dgate, dup, act):
    t, d = f.shape
    ff = dgate.shape[1]
    tm = min(TM_WGRAD, t)
    nt = t // tm
    fc = ff // FF_CHUNKS
    assert FF_CHUNKS == 2

    def body(f_ref, dd_ref, dgate_ref, dup_ref, act_ref, dwg_ref, dwu_ref, dwd_ref, *rest):
        landing, (acc_g, acc_u, acc_d, stage, sem) = rest[:2], rest[2:7]
        start, finish = _core_exchange_ops([dwg_ref, dwd_ref], landing, 0, *rest[7:])
        c, i = pl.program_id(0), pl.program_id(1)
        pl.when((c == 1) & (i == 0))(start)

        @pl.when(i == 0)
        def _():
            acc_g[...] = jnp.zeros_like(acc_g)
            acc_u[...] = jnp.zeros_like(acc_u)
            acc_d[...] = jnp.zeros_like(acc_d)

        fv = f_ref[...]
        acc_g[...] += _dot_tn(fv, dgate_ref[...])
        acc_u[...] += _dot_tn(fv, dup_ref[...])
        acc_d[...] += _dot_tn(act_ref[...], dd_ref[...])

        @pl.when(i == nt - 1)
        def _():
            rows = pl.ds(pl.multiple_of(c * fc, 16), fc)
            copies = []
            for k, (acc, out, transposed) in enumerate(((acc_d, dwd_ref, False), (acc_g, dwg_ref, True), (acc_u, dwu_ref, True))):
                if k >= 2:
                    copies[k - 2].wait()
                stage[k % 2] = (acc[...].T if transposed else acc[...]).astype(BF16)
                copies.append(pltpu.make_async_copy(stage.at[k % 2], out.at[rows, :], sem.at[k % 2]))
                copies[k].start()
            copies[-2].wait()
            copies[-1].wait()

        pl.when((c == 1) & (i == nt - 1))(finish)

    row = lambda c, i: (i, 0)
    col = lambda c, i: (i, c)
    out = pl.pallas_call(
        body, name="ffn_weight_grads", grid=(FF_CHUNKS, nt),
        in_specs=[pl.BlockSpec((tm, d), row), pl.BlockSpec((tm, d), row), pl.BlockSpec((tm, fc), col), pl.BlockSpec((tm, fc), col),
                  pl.BlockSpec((tm, fc), col)],
        out_specs=[ANY] * 5,
        out_shape=[jax.ShapeDtypeStruct((ff, d), BF16)] * 3 + [jax.ShapeDtypeStruct((N_DEV, ff // N_DEV, d), BF16)] * 2,
        scratch_shapes=[pltpu.VMEM((d, fc), F32), pltpu.VMEM((d, fc), F32), pltpu.VMEM((fc, d), F32), pltpu.VMEM((2, fc, d), BF16),
                        pltpu.SemaphoreType.DMA((2,))] + _core_exchange_sems(2),
        compiler_params=_params("arbitrary", "arbitrary"),
    )(f, dd, dgate, dup, act)
    return out[:3], [out[3], None, out[4]]


def _adamw(w, g, m, v):
    m = ADAM_B1 * m + (1.0 - ADAM_B1) * g
    v = ADAM_B2 * v + (1.0 - ADAM_B2) * (g * g)
    m_hat = m / (1.0 - ADAM_B1 ** ADAM_STEP)
    v_hat = v / (1.0 - ADAM_B2 ** ADAM_STEP)
    return -ADAM_LR * (m_hat / (jnp.sqrt(v_hat) + ADAM_EPS) + ADAM_WD * w), m, v


def _sum_slabs(ref):
    total = ref[0].astype(F32)
    for i in range(1, ref.shape[0]):
        total = total + ref[i].astype(F32)
    return total


def _adamw_rows(r, c):
    tr = r
    for cand in range(8, r, 8):
        if r % cand == 0 and cand * c <= ADAMW_BLOCK_ELEMS:
            tr = cand
    return r if r * c <= ADAMW_BLOCK_ELEMS else tr


def _reduce_adamw_carrying(parts, ws, ms, vs, to_reduce, to_exchange, whole, name):
    k, nr, nx = len(ws), len(to_reduce), len(to_exchange)
    r, c = ws[0].shape if k else (8, 128)
    tr = _adamw_rows(r, c)
    steps = r // tr
    travels = nr + nx > 0
    nd = list(whole).count(False)
    assert list(whole) == [False] * nd + [True] * (nx - nd)
    chip_slabs = [jax.ShapeDtypeStruct((N_CHIP, *a.shape[1:]), a.dtype) for a in to_reduce]

    def body(*refs):
        p_refs, w_refs, m_refs, v_refs = (refs[a * k:(a + 1) * k] for a in range(4))
        refs = refs[4 * k:]
        reduced_in, sent, refs = refs[:nr], refs[nr:nr + nx], refs[nr + nx:]
        outs, pairs, sums, landed, refs = refs[:4 * k], refs[4 * k:4 * k + nr], refs[4 * k + nr:4 * k + 2 * nr], \
            refs[4 * k + 2 * nr:4 * k + 2 * nr + nx], refs[4 * k + 2 * nr + nx:]
        mine_v, pair_v, sum_v, refs = refs[:nr], refs[nr:2 * nr], refs[2 * nr:3 * nr], refs[3 * nr:]
        if travels:
            reduce_ops = _pair_then_chip_ops(reduced_in, pairs, sums, mine_v, pair_v, sum_v, *refs[:7])
            direct_ops = _exchange_ops(sent[:nd], landed[:nd], [False] * nd, *refs[7:10])
            gather_ops = _gather_ops(sent[nd:], landed[nd:], *refs[10:13])

            @pl.when(pl.program_id(0) == 0)
            def _():
                direct_ops[0]()
                gather_ops[0]()
                reduce_ops[0]()

        for a in range(k):
            g = _sum_slabs(p_refs[a])
            outs[4 * a][...] = g
            outs[4 * a + 1][...], outs[4 * a + 2][...], outs[4 * a + 3][...] = _adamw(w_refs[a][...], g, m_refs[a][...], v_refs[a][...])

        if travels:
            @pl.when(pl.program_id(0) == steps - 1)
            def _():
                gather_ops[1]()
                reduce_ops[1]()
                gather_ops[2]()
                direct_ops[1]()

    blk = pl.BlockSpec((tr, c), lambda i: (i, 0))
    out = pl.pallas_call(
        body, name=name, grid=(steps,),
        in_specs=[pl.BlockSpec((N_DEV, tr, c), lambda i: (0, i, 0))] * k + [blk] * (3 * k) + [ANY] * (nr + nx),
        out_specs=[blk] * (4 * k) + [ANY] * (2 * nr + nx),
        out_shape=[jax.ShapeDtypeStruct((r, c), F32)] * (4 * k) + chip_slabs + chip_slabs
        + [jax.ShapeDtypeStruct((N_DEV, *a.shape) if w else a.shape, a.dtype) for a, w in zip(to_exchange, whole)],
        scratch_shapes=([pltpu.VMEM(a.shape, a.dtype) for a in chip_slabs] * 3 + _pair_then_chip_sems(nr) + _exchange_sems(nd)
                        + _exchange_sems(nx - nd) if travels else []),
        compiler_params=_params("arbitrary"),
    )(*parts, *ws, *ms, *vs, *to_reduce, *to_exchange)
    return [tuple(out[4 * a:4 * a + 4]) for a in range(k)], out[4 * k + nr:4 * k + 2 * nr], out[4 * k + 2 * nr:]


def _reduce_adamw(parts, w, m, v, name):
    r, c = w.shape
    tr = _adamw_rows(r, c)

    def body(p_ref, w_ref, m_ref, v_ref, g_out, d_out, m_out, v_out):
        g = _sum_slabs(p_ref)
        g_out[...] = g
        d_out[...], m_out[...], v_out[...] = _adamw(w_ref[...], g, m_ref[...], v_ref[...])

    blk = pl.BlockSpec((tr, c), lambda i: (i, 0))
    return pl.pallas_call(
        body, name=name, grid=(r // tr,),
        in_specs=[pl.BlockSpec((parts.shape[0], tr, c), lambda i: (0, i, 0)), blk, blk, blk],
        out_specs=[blk] * 4, out_shape=[jax.ShapeDtypeStruct((r, c), F32)] * 4,
        compiler_params=_params("arbitrary"),
    )(parts, w, m, v)


def _reduce_adamw_small(parts, ws, ms, vs, loss_parts):
    n = len(parts)

    def body(*refs):
        p_refs, w_refs, m_refs, v_refs = (refs[k * n:(k + 1) * n] for k in range(4))
        outs = refs[4 * n + 1:]
        outs[4 * n][...] = _sum_slabs(refs[4 * n])
        for a in range(n):
            g = _sum_slabs(p_refs[a])
            outs[4 * a][...] = g
            outs[4 * a + 1][...], outs[4 * a + 2][...], outs[4 * a + 3][...] = _adamw(w_refs[a][...], g, m_refs[a][...], v_refs[a][...])

    out = pl.pallas_call(
        body, name="adamw_replicated",
        out_shape=[jax.ShapeDtypeStruct(w.shape, F32) for w in ws for _ in range(4)] + [jax.ShapeDtypeStruct(loss_parts.shape[1:], F32)],
        compiler_params=pltpu.CompilerParams(vmem_limit_bytes=VMEM_LIMIT_BYTES),
    )(*parts, *ws, *ms, *vs, loss_parts)
    return [tuple(out[4 * a:4 * a + 4]) for a in range(n)], out[4 * n]


def kernel(x, meta_tokens, norm_mix_pre, w_in, conv_w, pool_w, pool_scale, w_out, norm_mix_post, norm_ffn_pre, w_gate, w_up, w_down, norm_ffn_post, loss_target, m_meta_tokens, m_norm_mix_pre, m_w_in, m_conv_w, m_pool_w, m_pool_scale, m_w_out, m_norm_mix_post, m_norm_ffn_pre, m_w_gate, m_w_up, m_w_down, m_norm_ffn_post, v_meta_tokens, v_norm_mix_pre, v_w_in, v_conv_w, v_pool_w, v_pool_scale, v_w_out, v_norm_mix_post, v_norm_ffn_pre, v_w_gate, v_w_up, v_w_down, v_norm_ffn_post):
    n_seq, seq, d = x.shape
    x2d = x.reshape(n_seq * seq, d)
    target = loss_target.reshape(n_seq * seq, d)

    t_ = lambda a: jnp.swapaxes(a[0], 0, 1)
    pw, ps = pool_w[0], pool_scale

    (h1, z, m, pooled, mixed), (win_b, wout_b, meta, conv, a_meta, z_meta), ffn_slabs = _gather_and_mixer_forward(
        x2d, [w_in[0], w_out[0], meta_tokens, conv_w[0]], [t_(w_gate), t_(w_up), w_down[0]], norm_mix_pre, pw, ps, norm_mix_post, n_seq)
    wg_b, wu_b, wd_b = (s.reshape(-1, d) for s in ffn_slabs)
    f, act, dd, dgate, dup, dh1, loss_sum, dg3, dg4 = _ffn_forward_backward(h1, target, norm_ffn_pre, wg_b, wu_b, wd_b, norm_ffn_post)
    ffn_grads, landing = _ffn_weight_grads(f, dd, dgate, dup, act)
    (gx, dwin, dwout, dg1, dg2, dconv, dpw, dps, dmeta), ffn_parts = _mixer_backward(
        x2d, dh1, m, z, pooled, mixed, meta, a_meta, z_meta, norm_mix_pre, win_b, conv, pw, ps, wout_b, norm_mix_post, n_seq,
        ffn_grads, landing)

    dmeta_s = jnp.transpose(dmeta.reshape(N_META, N_DEV, -1), (1, 0, 2))
    dconv_s = jnp.transpose(dconv.reshape(CONV_WIDTH, N_DEV, -1), (1, 0, 2))
    _, (win_parts, wout_parts), last = _reduce_adamw_carrying(
        [], [], [], [], [dwin, dwout.reshape(N_DEV, -1, d)], [dmeta_s, dconv_s, dg1, dg2, dg3, dg4, dpw.astype(BF16), dps, loss_sum],
        [False] * 2 + [True] * 7, "exchange_rest")
    ffn_res, _, _ = _reduce_adamw_carrying(
        ffn_parts, [t_(w_gate), t_(w_up), w_down[0]], [t_(m_w_gate), t_(m_w_up), m_w_down[0]], [t_(v_w_gate), t_(v_w_up), v_w_down[0]],
        [], [], [], "adamw_ffn")
    replicated = last[2:8]

    names = ["meta_tokens", "norm_mix_pre", "w_in", "conv_w", "pool_w", "pool_scale", "w_out", "norm_mix_post", "norm_ffn_pre", "w_gate",
             "w_up", "w_down", "norm_ffn_post"]
    res = {"w_gate": tuple(jnp.swapaxes(o, 0, 1)[None] for o in ffn_res[0]),
           "w_up": tuple(jnp.swapaxes(o, 0, 1)[None] for o in ffn_res[1]), "w_down": tuple(o[None] for o in ffn_res[2])}
    for nm, parts, w, m_, v_ in (("w_in", win_parts, w_in, m_w_in, v_w_in), ("w_out", wout_parts, w_out, m_w_out, v_w_out),
                                 ("conv_w", last[1], conv_w, m_conv_w, v_conv_w)):
        res[nm] = tuple(o[None] for o in _reduce_adamw(parts, w[0], m_[0], v_[0], "adamw_" + nm))
    res["meta_tokens"] = tuple(_reduce_adamw(last[0], meta_tokens, m_meta_tokens, v_meta_tokens, "adamw_meta_tokens"))
    small, loss = _reduce_adamw_small(
        replicated, [norm_mix_pre, norm_mix_post, norm_ffn_pre, norm_ffn_post, pool_w[0], pool_scale],
        [m_norm_mix_pre, m_norm_mix_post, m_norm_ffn_pre, m_norm_ffn_post, m_pool_w[0], m_pool_scale],
        [v_norm_mix_pre, v_norm_mix_post, v_norm_ffn_pre, v_norm_ffn_post, v_pool_w[0], v_pool_scale], last[8])
    for nm, r in zip(["norm_mix_pre", "norm_mix_post", "norm_ffn_pre", "norm_ffn_post", "pool_w", "pool_scale"], small):
        res[nm] = tuple(o[None] for o in r) if nm == "pool_w" else r

    return (loss[0, 0], gx.reshape(n_seq, seq, d), *[res[nm][0] for nm in names], *[res[nm][1] for nm in names],
            *[res[nm][2] for nm in names], *[res[nm][3] for nm in names])
```

```python
import jax
import jax.numpy as jnp
from jax import lax
from jax.experimental import pallas as pl
from jax.experimental.pallas import tpu as pltpu

F32, BF16 = jnp.float32, jnp.bfloat16
RMS_EPS = 1e-6
N_META = 16
CONV_WIDTH = 3
POOL_WINDOWS = (2, 4, 8, 16)
POOL_GROUP = 128
HALO = 16
N_DEV = 8
MESH_AXES = ("x", "y", "c")
MESH = pl.DeviceIdType.MESH
VMEM_LIMIT_BYTES = 56 * 1024 * 1024
ADAMW_BLOCK_ELEMS = 64 * 1024
TM_MIX = 512
TM_FFN = 256
FFN_CHUNK = 512
FFN_BACKWARD_LAG = 3
TM_WGRAD = 512
FF_CHUNKS = 2

ADAM_LR, ADAM_B1, ADAM_B2, ADAM_EPS, ADAM_WD, ADAM_STEP = 0.001, 0.9, 0.999, 1e-08, 0.01, 10


def _dot(a, b):
    return jnp.dot(a, b, preferred_element_type=F32)


def _dot_nt(a, b):
    return lax.dot_general(a, b, (((1,), (1,)), ((), ())), preferred_element_type=F32)


def _dot_tn(a, b):
    return lax.dot_general(a, b, (((0,), (0,)), ((), ())), preferred_element_type=F32)


def _rms_stats(h):
    rstd = lax.rsqrt(jnp.mean(h * h, axis=-1, keepdims=True) + RMS_EPS)
    return h * rstd, rstd


def _rms_bwd(hat, rstd, g, dy):
    gdy = dy * g
    proj = jnp.mean(gdy * hat, axis=-1, keepdims=True)
    return rstd * (gdy - hat * proj), jnp.sum(dy * hat, axis=0, keepdims=True)


def _params(*semantics):
    return pltpu.CompilerParams(dimension_semantics=semantics or None, vmem_limit_bytes=VMEM_LIMIT_BYTES)


def _resident(shape):
    zeros = (0,) * len(shape)
    return pl.BlockSpec(shape, lambda *_: zeros, pipeline_mode=pl.Buffered(1))


def _const(shape):
    zeros = (0,) * len(shape)
    return pl.BlockSpec(shape, lambda *_: zeros)


ANY = pl.BlockSpec(memory_space=pl.ANY)


def _my_place():
    x, y, c = (lax.axis_index(a) for a in MESH_AXES)
    return x, y, c


def _exchange_sems(n):
    return [pltpu.SemaphoreType.DMA((n, N_DEV - 1)), pltpu.SemaphoreType.DMA((n, N_DEV - 1)), pltpu.SemaphoreType.DMA((n,))]


def _gather_ops(srcs, outs, send_sems, recv_sems, local_sems, core_major=False):
    n = len(srcs)
    x, y, c = _my_place()
    me, sibling = (x, y, c), (x, y, 1 - c)
    chips = [(1 - x, y), (x, 1 - y), (1 - x, 1 - y)]

    def slab(px, py, pc):
        return 4 * pc + 2 * px + py if core_major else 4 * px + 2 * py + pc

    def copy(a, k, block, to, src=None):
        dst = outs[a].at[slab(*block)]
        return pltpu.make_async_remote_copy(
            src_ref=dst if src is None else src, dst_ref=dst, send_sem=send_sems.at[a, k], recv_sem=recv_sems.at[a, k],
            device_id=to, device_id_type=MESH)

    def mine(a):
        return pltpu.make_async_copy(srcs[a], outs[a].at[slab(*me)], local_sems.at[a])

    def first(a):
        return [copy(a, 0, me, sibling, src=srcs[a])] + [copy(a, 1 + j, me, (*chip, c), src=srcs[a]) for j, chip in enumerate(chips)]

    def passed(a, j):
        return copy(a, 4 + j, (*chips[j], c), sibling)

    def start():
        for a in range(n):
            mine(a).start()
            for cp in first(a):
                cp.start()

    def forward():
        for j, chip in enumerate(chips):
            for a in range(n):
                copy(a, 1 + j, (*chip, c), me).wait_recv()
                passed(a, j).start()

    def finish():
        for a in range(n):
            copy(a, 0, sibling, me).wait_recv()
            for j, chip in enumerate(chips):
                copy(a, 4 + j, (*chip, 1 - c), me).wait_recv()
        for a in range(n):
            for cp in first(a) + [passed(a, j) for j in range(len(chips))]:
                cp.wait_send()
            mine(a).wait()

    return start, forward, finish


def _exchange_ops(ins, outs, whole, send_sems, recv_sems, local_sems):
    n = len(ins)
    x, y, c = _my_place()
    me = 4 * x + 2 * y + c

    def src(a, i):
        return ins[a] if whole[a] else ins[a].at[i]

    def mine(a):
        return pltpu.make_async_copy(src(a, me), outs[a].at[me], local_sems.at[a])

    def send(a, k):
        to = (me + k) % N_DEV
        return pltpu.make_async_remote_copy(
            src_ref=src(a, to), dst_ref=outs[a].at[me], send_sem=send_sems.at[a, k - 1], recv_sem=recv_sems.at[a, k - 1],
            device_id=(to // 4, (to // 2) % 2, to % 2), device_id_type=MESH)

    def landed(a, k):
        frm = (me + N_DEV - k) % N_DEV
        return pltpu.make_async_remote_copy(
            src_ref=src(a, frm), dst_ref=outs[a].at[frm], send_sem=send_sems.at[a, k - 1], recv_sem=recv_sems.at[a, k - 1],
            device_id=(x, y, c), device_id_type=MESH)

    def start():
        for a in range(n):
            mine(a).start()
            for k in range(1, N_DEV):
                send(a, k).start()

    def finish():
        for a in range(n):
            for k in range(1, N_DEV):
                landed(a, k).wait_recv()
        for a in range(n):
            for k in range(1, N_DEV):
                send(a, k).wait_send()
            mine(a).wait()

    return start, finish


def _core_exchange_sems(n):
    return [pltpu.SemaphoreType.DMA((n, 4)), pltpu.SemaphoreType.DMA((n, N_DEV)), pltpu.SemaphoreType.DMA((n,))]


def _core_exchange_ops(ins, outs, to_core, send_sems, recv_sems, local_sems):
    n = len(ins)
    x, y, c = _my_place()
    me = 4 * x + 2 * y + c
    others = [(0, 1), (1, 0), (1, 1)]

    def slab(a, p):
        if len(ins[a].shape) == len(outs[a].shape):
            return ins[a].at[p]
        rows = outs[a].shape[1]
        return ins[a].at[pl.ds(pl.multiple_of(p * rows, 16), rows), :]

    def send(a, dx, dy):
        tx, ty = (x + dx) % 2, (y + dy) % 2
        return pltpu.make_async_remote_copy(
            src_ref=slab(a, 4 * to_core + 2 * tx + ty), dst_ref=outs[a].at[me], send_sem=send_sems.at[a, 2 * dx + dy],
            recv_sem=recv_sems.at[a, 2 * (2 * dx + dy) + c], device_id=(tx, ty, to_core), device_id_type=MESH)

    def mine(a):
        return pltpu.make_async_copy(slab(a, 4 * to_core + 2 * x + y), outs[a].at[me], local_sems.at[a])

    def landed(a, dx, dy, sc):
        frm = 4 * ((x + dx) % 2) + 2 * ((y + dy) % 2) + sc
        return pltpu.make_async_remote_copy(
            src_ref=slab(a, 0), dst_ref=outs[a].at[frm], send_sem=send_sems.at[a, 0], recv_sem=recv_sems.at[a, 2 * (2 * dx + dy) + sc],
            device_id=(x, y, c), device_id_type=MESH)

    def start():
        for a in range(n):
            for dx, dy in others:
                send(a, dx, dy).start()
            pl.when(c == to_core)(mine(a).start)
            pl.when(c != to_core)(send(a, 0, 0).start)

    def finish():
        @pl.when(c == to_core)
        def _():
            for a in range(n):
                for dx, dy in [(0, 0)] + others:
                    for sc in (0, 1):
                        if (dx, dy, sc) != (0, 0, to_core):
                            landed(a, dx, dy, sc).wait_recv()
            for a in range(n):
                mine(a).wait()

        @pl.when(c != to_core)
        def _():
            for a in range(n):
                send(a, 0, 0).wait_send()

        for a in range(n):
            for dx, dy in others:
                send(a, dx, dy).wait_send()

    return start, finish


N_CHIP = 4


def _pair_then_chip_sems(n):
    return [pltpu.SemaphoreType.DMA((n, N_CHIP)) for _ in range(6)] + [pltpu.SemaphoreType.DMA((n,))]


def _pair_then_chip_ops(ins, pairs, outs, mine_v, pair_v, sum_v, pair_send, pair_recv, chip_send, chip_recv, load_a, load_b, own_sem):
    n = len(ins)
    x, y, c = _my_place()
    chip = 2 * x + y
    chips = [(0, 0), (0, 1), (1, 0), (1, 1)]
    others = [(0, 1), (1, 0), (1, 1)]

    def to_sibling(a, j):
        px, py = chips[j]
        return pltpu.make_async_remote_copy(
            src_ref=ins[a].at[4 * px + 2 * py + 1 - c], dst_ref=pairs[a].at[j], send_sem=pair_send.at[a, j], recv_sem=pair_recv.at[a, j],
            device_id=(x, y, 1 - c), device_id_type=MESH)

    def spread(a, dx, dy):
        tx, ty = (x + dx) % 2, (y + dy) % 2
        return pltpu.make_async_remote_copy(
            src_ref=sum_v[a].at[2 * tx + ty], dst_ref=outs[a].at[chip], send_sem=chip_send.at[a, 2 * dx + dy],
            recv_sem=chip_recv.at[a, 2 * dx + dy], device_id=(tx, ty, c), device_id_type=MESH)

    def landed(a, dx, dy):
        frm = 2 * ((x + dx) % 2) + (y + dy) % 2
        return pltpu.make_async_remote_copy(
            src_ref=sum_v[a].at[0], dst_ref=outs[a].at[frm], send_sem=chip_send.at[a, 0], recv_sem=chip_recv.at[a, 2 * dx + dy],
            device_id=(x, y, c), device_id_type=MESH)

    def own(a):
        return pltpu.make_async_copy(sum_v[a].at[chip], outs[a].at[chip], own_sem.at[a])

    def pair():
        loads = []
        for a in range(n):
            for j, (px, py) in enumerate(chips):
                to_sibling(a, j).start()
                loads.append(pltpu.make_async_copy(ins[a].at[4 * px + 2 * py + c], mine_v[a].at[j], load_a.at[a, j]))
                loads[-1].start()
        for a in range(n):
            for j in range(N_CHIP):
                to_sibling(a, j).wait_recv()
                loads.append(pltpu.make_async_copy(pairs[a].at[j], pair_v[a].at[j], load_b.at[a, j]))
                loads[-1].start()
        for cp in loads:
            cp.wait()
        for a in range(n):
            sum_v[a][...] = (mine_v[a][...].astype(F32) + pair_v[a][...].astype(F32)).astype(sum_v[a].dtype)

    def start():
        pair()
        for a in range(n):
            own(a).start()
            for dx, dy in others:
                spread(a, dx, dy).start()

    def finish():
        for a in range(n):
            for dx, dy in others:
                landed(a, dx, dy).wait_recv()
        for a in range(n):
            for dx, dy in others:
                spread(a, dx, dy).wait_send()
            for j in range(N_CHIP):
                to_sibling(a, j).wait_send()
            own(a).wait()

    return start, finish


def _window_sum(x, win, ahead):
    n = x.shape[0]
    span = 1
    while span < win:
        x = x + pltpu.roll(x, n - span if ahead else span, 0)
        span *= 2
    return x


def _conv_branch(z, ext_u, conv_ref, tm):
    c_w = z.shape[1] // 4
    b, c, v = z[:, :c_w], z[:, c_w:2 * c_w], z[:, 2 * c_w:3 * c_w]
    u = c * v
    ext_u[pl.ds(HALO, tm), :] = u
    u1 = ext_u[pl.ds(HALO - 1, tm), :]
    u2 = ext_u[pl.ds(HALO - 2, tm), :]
    yc = conv_ref[pl.ds(2, 1), :] * u + conv_ref[pl.ds(1, 1), :] * u1 + conv_ref[pl.ds(0, 1), :] * u2
    return b, c, v, u, u1, u2, yc


def _pool_branch(p, ext_p, pool_w_ref, tm):
    ext_p[pl.ds(HALO, tm), :] = p
    pooled, mixed = [], []
    for g, win in enumerate(POOL_WINDOWS):
        s = _window_sum(ext_p[:, pl.ds(POOL_GROUP * g, POOL_GROUP)], win, ahead=False)[HALO:HALO + tm, :]
        pooled.append((s * (1.0 / win) - p[:, POOL_GROUP * g:POOL_GROUP * (g + 1)]).astype(BF16))
        mixed.append(_dot(pooled[-1], pool_w_ref[g].astype(BF16)))
    return pooled, mixed


def _gather_and_mixer_forward(x2d, mixer_shards, ffn_shards, g1, pool_w, pool_scale, g2, n_seq):
    t, d = x2d.shape
    zs, rs, ms, cs = mixer_shards[0].shape[1], mixer_shards[1].shape[0], mixer_shards[2].shape[1], mixer_shards[3].shape[1]
    zw, cw = N_DEV * zs, N_DEV * cs
    s = t // n_seq
    tm = min(TM_MIX, s)
    nj = s // tm
    n1, n2 = len(mixer_shards), len(ffn_shards)
    dtypes = [BF16, BF16, F32, F32] + [BF16] * n2
    shards = list(mixer_shards) + list(ffn_shards)

    def body(x_ref, *rest):
        shard_refs, (g1_ref, pw_ref, ps_ref, g2_ref), rest = rest[:n1 + n2], rest[n1 + n2:n1 + n2 + 4], rest[n1 + n2 + 4:]
        (h1_ref, z_ref, m_ref, pooled_ref, mixed_ref, win_o, wout_o, meta_o, conv_o, am_o, zm_o), rest = rest[:11], rest[11:]
        slabs, rest = rest[:n1 + n2], rest[n1 + n2:]
        stages, rest = rest[:n1 + n2], rest[n1 + n2:]
        win_v, wout_v, meta_v, conv_v, ext_u, ext_p, sem = rest[:7]
        first = _gather_ops(stages[:n1], slabs[:n1], *rest[7:10])
        later = _gather_ops(stages[n1:], slabs[n1:], *rest[10:13], core_major=True)

        @pl.when((pl.program_id(0) == 0) & (pl.program_id(1) == 0))
        def _():
            for src, dst in zip(shard_refs, stages):
                dst[...] = src[...].astype(dst.dtype)
            first[0]()
            later[0]()
            first[1]()
            first[2]()
            copies = [pltpu.make_async_copy(slabs[0].at[i], win_v.at[:, pl.ds(zs * i, zs)], sem.at[i]) for i in range(N_DEV)]
            copies += [pltpu.make_async_copy(slabs[1].at[i], wout_v.at[pl.ds(rs * i, rs), :], sem.at[N_DEV + i]) for i in range(N_DEV)]
            copies += [pltpu.make_async_copy(slabs[2], meta_v, sem.at[2 * N_DEV]), pltpu.make_async_copy(slabs[3], conv_v, sem.at[2 * N_DEV + 1])]
            for cp in copies:
                cp.start()
            for cp in copies:
                cp.wait()
            copies = [pltpu.make_async_copy(win_v, win_o, sem.at[0]), pltpu.make_async_copy(wout_v, wout_o, sem.at[1])]
            for cp in copies:
                cp.start()
            for i in range(N_DEV):
                meta_o[:, pl.ds(ms * i, ms)] = meta_v[i]
                conv_o[:, pl.ds(cs * i, cs)] = conv_v[i]
            hat, _ = _rms_stats(meta_o[...])
            a = (hat * g1_ref[...]).astype(BF16)
            am_o[...] = a
            zm_o[...] = _dot(a, win_v[...])
            for cp in copies:
                cp.wait()

        @pl.when(pl.program_id(1) == 0)
        def _():
            zm = zm_o[...]
            ext_u[pl.ds(0, HALO), :] = zm[:, cw:2 * cw] * zm[:, 2 * cw:3 * cw]
            ext_p[pl.ds(0, HALO), :] = zm[:, 3 * cw:]

        h0 = x_ref[...]
        hat, _ = _rms_stats(h0)
        z = _dot((hat * g1_ref[...]).astype(BF16), win_v[...])
        z_ref[...] = z.astype(BF16)
        b, _, _, _, _, _, yc = _conv_branch(z, ext_u, conv_o, tm)
        pooled, mixed = _pool_branch(z[:, 3 * cw:], ext_p, pw_ref, tm)
        pooled_ref[...] = jnp.concatenate(pooled, axis=1)
        mixed_ref[...] = jnp.concatenate(mixed, axis=1).astype(BF16)
        ps = ps_ref[...]
        y = [b * yc] + [mixed[g] * ps[:, POOL_GROUP * g:POOL_GROUP * (g + 1)] for g in range(len(POOL_WINDOWS))]
        m = _dot(jnp.concatenate(y, axis=1).astype(BF16), wout_v[...])
        m_ref[...] = m
        m_hat, _ = _rms_stats(m)
        h1_ref[...] = h0 + m_hat * g2_ref[...]
        ext_u[pl.ds(0, HALO), :] = ext_u[pl.ds(tm, HALO), :]
        ext_p[pl.ds(0, HALO), :] = ext_p[pl.ds(tm, HALO), :]

        @pl.when((pl.program_id(0) == n_seq - 1) & (pl.program_id(1) == nj - 1))
        def _():
            later[1]()
            later[2]()

    row = lambda b, j: (b * nj + j, 0)
    vmem = pl.BlockSpec(memory_space=pltpu.VMEM)
    small = [(N_META, d), (CONV_WIDTH, cw), (N_META, d), (N_META, zw)]
    out = pl.pallas_call(
        body, name="gather_and_mixer_forward", grid=(n_seq, nj),
        in_specs=[pl.BlockSpec((tm, d), row)] + [vmem] * (n1 + n2)
        + [_const(g1.shape), _const(pool_w.shape), _const(pool_scale.shape), _const(g2.shape)],
        out_specs=[pl.BlockSpec((tm, d), row), pl.BlockSpec((tm, zw), row), pl.BlockSpec((tm, d), row), pl.BlockSpec((tm, cw), row),
                   pl.BlockSpec((tm, cw), row), ANY, ANY] + [_const(sh) for sh in small] + [ANY] * (n1 + n2),
        out_shape=[jax.ShapeDtypeStruct((t, d), F32), jax.ShapeDtypeStruct((t, zw), BF16), jax.ShapeDtypeStruct((t, d), F32),
                   jax.ShapeDtypeStruct((t, cw), BF16), jax.ShapeDtypeStruct((t, cw), BF16),
                   jax.ShapeDtypeStruct((d, zw), BF16), jax.ShapeDtypeStruct((d, d), BF16),
                   jax.ShapeDtypeStruct(small[0], F32), jax.ShapeDtypeStruct(small[1], F32), jax.ShapeDtypeStruct(small[2], BF16),
                   jax.ShapeDtypeStruct(small[3], F32)]
        + [jax.ShapeDtypeStruct((N_DEV, *a.shape), dt) for a, dt in zip(shards, dtypes)],
        scratch_shapes=[pltpu.VMEM(a.shape, dt) for a, dt in zip(shards, dtypes)]
        + [pltpu.VMEM((d, zw), BF16), pltpu.VMEM((d, d), BF16), pltpu.VMEM((N_DEV, N_META, ms), F32),
           pltpu.VMEM((N_DEV, CONV_WIDTH, cs), F32), pltpu.VMEM((tm + HALO, cw), F32), pltpu.VMEM((tm + HALO, cw), F32),
           pltpu.SemaphoreType.DMA((2 * N_DEV + 2,))] + _exchange_sems(n1) + _exchange_sems(n2),
        compiler_params=_params("arbitrary", "arbitrary"),
    )(x2d, *shards, g1, pool_w, pool_scale, g2)
    return out[:5], out[5:11], out[11 + n1:]


def _mixer_backward(x2d, dh1, m, z, pooled, mixed, meta, a_meta, z_meta, g1, w_in, conv_w, pool_w, pool_scale, w_out, g2, n_seq,
                    to_exchange, landing):
    t, d = x2d.shape
    zw = w_in.shape[1]
    cw = zw // 4
    s = t // n_seq
    tm = min(TM_MIX, s)
    nj = s // tm
    n_groups = len(POOL_WINDOWS)
    zs = zw // N_DEV
    nx = len(to_exchange)
    n_in = 17
    given = [k for k, a in enumerate(landing) if a is not None]
    fresh = [k for k, a in enumerate(landing) if a is None]

    def body(x_ref, dh1_ref, m_ref, z_ref, zprev_ref, pooled_ref, mixed_ref, meta_ref, am_ref, zm_ref, g1_ref, win_ref, conv_ref, pw_ref, ps_ref, wout_ref,
             g2_ref, *rest):
        sent, rest = rest[:nx], rest[nx + len(given):]
        gx_ref, dwin_ref, dwout_ref, dg1_ref, dg2_ref, dconv_ref, dpw_ref, dps_ref, dmeta_ref = rest[:9]
        landed, rest = rest[9:9 + nx], rest[9 + nx:]
        ext_u, ext_dyc, ext_dq, acc_win, acc_wout, dz_meta, stage16, sem = rest[:8]
        north = _core_exchange_ops(sent, landed, 1, *rest[8:11])
        south = _core_exchange_ops([sent[k] for k in fresh], [landed[k] for k in fresh], 0, *rest[11:14])

        def start():
            north[0]()
            south[0]()

        def finish():
            south[1]()
            north[1]()

        b_id, j = pl.program_id(0), pl.program_id(1)
        jr = nj - 1 - j
        pl.when((b_id == 0) & (j == 0))(start)

        @pl.when((b_id == 0) & (j == 0))
        def _():
            acc_win[...] = jnp.zeros_like(acc_win)
            acc_wout[...] = jnp.zeros_like(acc_wout)
            dz_meta[...] = jnp.zeros_like(dz_meta)
            for r in (dg1_ref, dg2_ref, dconv_ref, dpw_ref, dps_ref, dmeta_ref):
                r[...] = jnp.zeros_like(r)

        @pl.when(j == 0)
        def _():
            ext_dyc[pl.ds(tm, HALO), :] = jnp.zeros((HALO, cw), F32)
            ext_dq[pl.ds(tm, HALO), :] = jnp.zeros((HALO, cw), F32)

        zm = zm_ref[...]
        halo = jnp.where(jr == 0, zm, zprev_ref[...].astype(F32))
        ext_u[pl.ds(0, HALO), :] = halo[:, cw:2 * cw] * halo[:, 2 * cw:3 * cw]

        dh1v = dh1_ref[...]
        m_hat, m_rstd = _rms_stats(m_ref[...])
        dm, dg2 = _rms_bwd(m_hat, m_rstd, g2_ref[...], dh1v)
        dg2_ref[...] += dg2
        dm = dm.astype(BF16)
        dycat = _dot_nt(dm, wout_ref[...])

        b, c, v, u, u1, u2, yc = _conv_branch(z_ref[...].astype(F32), ext_u, conv_ref, tm)
        mixed = [mixed_ref[:, pl.ds(POOL_GROUP * g, POOL_GROUP)].astype(F32) for g in range(n_groups)]
        ps = ps_ref[...]
        y = [b * yc] + [mixed[g] * ps[:, POOL_GROUP * g:POOL_GROUP * (g + 1)] for g in range(n_groups)]
        ycat = jnp.concatenate(y, axis=1).astype(BF16)
        acc_wout[...] += _dot_tn(ycat, dm)

        dyconv = dycat[:, :cw]
        db = dyconv * yc
        dyc = dyconv * b
        ext_dyc[pl.ds(0, tm), :] = dyc
        du = (conv_ref[pl.ds(2, 1), :] * dyc + conv_ref[pl.ds(1, 1), :] * ext_dyc[pl.ds(1, tm), :]
              + conv_ref[pl.ds(0, 1), :] * ext_dyc[pl.ds(2, tm), :])
        dconv_ref[pl.ds(2, 1), :] += jnp.sum(dyc * u, axis=0, keepdims=True)
        dconv_ref[pl.ds(1, 1), :] += jnp.sum(dyc * u1, axis=0, keepdims=True)
        dconv_ref[pl.ds(0, 1), :] += jnp.sum(dyc * u2, axis=0, keepdims=True)

        dp = []
        for g, win in enumerate(POOL_WINDOWS):
            lanes = pl.ds(POOL_GROUP * g, POOL_GROUP)
            dypool = dycat[:, cw + POOL_GROUP * g:cw + POOL_GROUP * (g + 1)]
            dps_ref[:, lanes] += jnp.sum(dypool * mixed[g], axis=0, keepdims=True)
            dmixed = (dypool * ps[:, POOL_GROUP * g:POOL_GROUP * (g + 1)]).astype(BF16)
            dq = _dot_nt(dmixed, pw_ref[g].astype(BF16))
            dpw_ref[g] += _dot_tn(pooled_ref[:, lanes], dmixed)
            ext_dq[pl.ds(0, tm), lanes] = dq
            acc = _window_sum(ext_dq[:, lanes], win, ahead=True)[0:tm, :]
            dp.append(acc * (1.0 / win) - dq)

        dz = jnp.concatenate([db, du * v, du * c] + dp, axis=1).astype(BF16)
        da = _dot_nt(dz, win_ref[...])
        h0 = x_ref[...]
        hat0, rstd0 = _rms_stats(h0)
        g1 = g1_ref[...]
        acc_win[...] += _dot_tn((hat0 * g1).astype(BF16), dz)
        dh0, dg1 = _rms_bwd(hat0, rstd0, g1, da)
        dg1_ref[...] += dg1
        gx_ref[...] = dh1v + dh0

        ext_dyc[pl.ds(tm, HALO), :] = ext_dyc[pl.ds(0, HALO), :]
        ext_dq[pl.ds(tm, HALO), :] = ext_dq[pl.ds(0, HALO), :]

        @pl.when(jr == 0)
        def _():
            ext_dyc[pl.ds(tm - HALO, HALO), :] = jnp.zeros((HALO, cw), F32)
            ext_dq[pl.ds(tm - HALO, HALO), :] = jnp.zeros((HALO, cw), F32)
            du_m = (conv_ref[pl.ds(1, 1), :] * ext_dyc[pl.ds(tm - HALO + 1, HALO), :]
                    + conv_ref[pl.ds(0, 1), :] * ext_dyc[pl.ds(tm - HALO + 2, HALO), :])
            dp_m = []
            for g, win in enumerate(POOL_WINDOWS):
                lanes = pl.ds(POOL_GROUP * g, POOL_GROUP)
                acc = ext_dq[pl.ds(tm - HALO + 1, HALO), lanes]
                for k in range(2, win):
                    acc = acc + ext_dq[pl.ds(tm - HALO + k, HALO), lanes]
                dp_m.append(acc * (1.0 / win))
            dz_meta[...] += jnp.concatenate(
                [jnp.zeros((HALO, cw), F32), du_m * zm[:, 2 * cw:3 * cw], du_m * zm[:, cw:2 * cw]] + dp_m, axis=1)

        @pl.when((b_id == n_seq - 1) & (j == nj - 1))
        def _():
            dz_m = dz_meta[...].astype(BF16)
            acc_win[...] += _dot_tn(am_ref[...], dz_m)
            hat_m, rstd_m = _rms_stats(meta_ref[...])
            dmeta, dg1_m = _rms_bwd(hat_m, rstd_m, g1, _dot_nt(dz_m, win_ref[...]))
            dg1_ref[...] += dg1_m
            dmeta_ref[...] = dmeta
            pieces = [(acc_win, zs * i, dwin_ref.at[i]) for i in range(N_DEV)]
            pieces += [(acc_wout, zs * i, dwout_ref.at[:, pl.ds(zs * i, zs)]) for i in range(d // zs)]
            copies = []
            for k, (acc, col, dst) in enumerate(pieces):
                if k >= 2:
                    copies[k - 2].wait()
                stage16[k % 2] = acc[:, pl.ds(col, zs)].astype(BF16)
                copies.append(pltpu.make_async_copy(stage16.at[k % 2], dst, sem.at[k % 2]))
                copies[k].start()
            copies[-2].wait()
            copies[-1].wait()
            finish()

    row = lambda b, j: (b * nj + nj - 1 - j, 0)
    prev = lambda b, j: (jnp.maximum((b * s + (nj - 1 - j) * tm) // HALO - 1, 0), 0)
    small = [g1.shape, g2.shape, conv_w.shape, pool_w.shape, pool_scale.shape, meta.shape]
    out = pl.pallas_call(
        body, name="mixer_backward", grid=(n_seq, nj),
        in_specs=[pl.BlockSpec((tm, d), row), pl.BlockSpec((tm, d), row), pl.BlockSpec((tm, d), row), pl.BlockSpec((tm, zw), row),
                  pl.BlockSpec((HALO, zw), prev), pl.BlockSpec((tm, cw), row), pl.BlockSpec((tm, cw), row), _const(meta.shape), _const(a_meta.shape), _const(z_meta.shape), _const(g1.shape),
                  _resident(w_in.shape), _const(conv_w.shape), _const(pool_w.shape), _const(pool_scale.shape), _resident(w_out.shape),
                  _const(g2.shape)] + [ANY] * (nx + len(given)),
        out_specs=[pl.BlockSpec((tm, d), row), ANY, ANY] + [_const(sh) for sh in small] + [ANY] * nx,
        out_shape=[jax.ShapeDtypeStruct((t, d), F32), jax.ShapeDtypeStruct((N_DEV, d, zs), BF16),
                   jax.ShapeDtypeStruct(w_out.shape, BF16)] + [jax.ShapeDtypeStruct(sh, F32) for sh in small]
        + [jax.ShapeDtypeStruct((N_DEV, a.shape[0] // N_DEV, a.shape[1]), a.dtype) for a in to_exchange],
        input_output_aliases={n_in + nx + at: 9 + k for at, k in enumerate(given)},
        scratch_shapes=[pltpu.VMEM((tm + HALO, cw), F32)] * 3
        + [pltpu.VMEM(w_in.shape, F32), pltpu.VMEM(w_out.shape, F32), pltpu.VMEM((HALO, zw), F32), pltpu.VMEM((2, d, zs), BF16),
           pltpu.SemaphoreType.DMA((2,))] + _core_exchange_sems(nx) + _core_exchange_sems(len(fresh)),
        compiler_params=_params("arbitrary", "arbitrary"),
    )(x2d, dh1, m, z, z, pooled, mixed, meta, a_meta, z_meta, g1, w_in, conv_w, pool_w, pool_scale, w_out, g2, *to_exchange, *[landing[k] for k in given])
    return out[:9], out[9:]


def _ffn_forward_backward(h1, target, g3, w_gate, w_up, w_down, g4):
    t, d = h1.shape
    ff = w_gate.shape[0]
    tm = min(TM_FFN, t)
    nt = t // tm
    chunks = [(s, min(FFN_CHUNK, ff - s)) for s in range(0, ff, FFN_CHUNK)]

    def body(h1_ref, h1pp_ref, tgt_ref, g3_ref, wg_ref, wu_ref, wd_ref, g4_ref,
             f_ref, act_ref, dd_ref, dgate_ref, dup_ref, dh1_ref, loss_ref, dg3_ref, dg4_ref, *slots):
        gate_s, up_s, dd_s, dh2_s, df_s = slots
        i = pl.program_id(0)

        def forward(slot):
            h1v = h1_ref[...]
            hat, _ = _rms_stats(h1v)
            f = (hat * g3_ref[...]).astype(BF16)
            f_ref[...] = f
            s, n = chunks[0]
            gate, up = _dot_nt(f_ref[...], wg_ref[pl.ds(s, n), :]), _dot_nt(f_ref[...], wu_ref[pl.ds(s, n), :])
            yield
            down = None
            for k, (s, n) in enumerate(chunks):
                gate_s.at[slot][:, pl.ds(s, n)] = gate.astype(BF16)
                up_s.at[slot][:, pl.ds(s, n)] = up.astype(BF16)
                act = (gate * jax.nn.sigmoid(gate) * up).astype(BF16)
                act_ref[:, pl.ds(s, n)] = act
                if k + 1 < len(chunks):
                    s1, n1 = chunks[k + 1]
                    gate, up = _dot_nt(f_ref[...], wg_ref[pl.ds(s1, n1), :]), _dot_nt(f_ref[...], wu_ref[pl.ds(s1, n1), :])
                yield
                part = _dot(act_ref[:, pl.ds(s, n)], wd_ref[pl.ds(s, n), :])
                down = part if down is None else down + part
                yield
            d_hat, d_rstd = _rms_stats(down)
            g4 = g4_ref[...]
            err = h1v + d_hat * g4 - tgt_ref[...]
            loss_ref[...] += jnp.sum(err * err) * (0.5 / d)
            dh2 = err * (1.0 / d)
            dh2_s.at[slot][...] = dh2
            dd, dg4 = _rms_bwd(d_hat, d_rstd, g4, dh2)
            dg4_ref[...] += dg4
            dd = dd.astype(BF16)
            dd_ref[...] = dd
            dd_s.at[slot][...] = dd

        def backward(slot):
            s, n = chunks[0]
            dact = _dot_nt(dd_s.at[slot][...], wd_ref[pl.ds(s, n), :])
            yield
            df = None
            for k, (s, n) in enumerate(chunks):
                gate = gate_s.at[slot][:, pl.ds(s, n)].astype(F32)
                up = up_s.at[slot][:, pl.ds(s, n)].astype(F32)
                sig = jax.nn.sigmoid(gate)
                dup = (dact * (gate * sig)).astype(BF16)
                dgate = (dact * up * (sig * (1.0 + gate * (1.0 - sig)))).astype(BF16)
                dup_ref[:, pl.ds(s, n)] = dup
                dgate_ref[:, pl.ds(s, n)] = dgate
                if k + 1 < len(chunks):
                    s1, n1 = chunks[k + 1]
                    dact = _dot_nt(dd_s.at[slot][...], wd_ref[pl.ds(s1, n1), :])
                yield
                part = _dot(dgate_ref[:, pl.ds(s, n)], wg_ref[pl.ds(s, n), :]) + _dot(dup_ref[:, pl.ds(s, n)], wu_ref[pl.ds(s, n), :])
                df = part if df is None else df + part
                yield
            df_s.at[slot][...] = df

        def last(slot):
            hat, rstd = _rms_stats(h1pp_ref[...])
            dh1, dg3 = _rms_bwd(hat, rstd, g3_ref[...], df_s.at[slot][...])
            dg3_ref[...] += dg3
            dh1_ref[...] = dh2_s.at[slot][...] + dh1

        def emit(parity, with_forward, with_backward, with_last):
            fwd = forward(parity) if with_forward else iter(())
            bwd = backward(1 - parity) if with_backward else iter(())
            next(fwd, None)
            if with_last:
                last(parity)
            for _ in range(FFN_BACKWARD_LAG):
                next(fwd, None)
            alive = True
            while alive:
                alive = next(bwd, True) is None
                alive = (next(fwd, True) is None) or alive

        @pl.when(i == 0)
        def _():
            for r in (loss_ref, dg3_ref, dg4_ref, *slots):
                r[...] = jnp.zeros_like(r)

        @pl.when(i < nt)
        def _():
            emit(i % 2, True, True, True)

        @pl.when(i == nt)
        def _():
            emit(nt % 2, False, True, True)

        @pl.when(i == nt + 1)
        def _():
            emit((nt + 1) % 2, False, False, True)

    cur = lambda i: (jnp.minimum(i, nt - 1), 0)
    prev = lambda i: (jnp.clip(i - 1, 0, nt - 1), 0)
    prev2 = lambda i: (jnp.clip(i - 2, 0, nt - 1), 0)
    return pl.pallas_call(
        body, name="ffn_forward_backward", grid=(nt + 2,),
        in_specs=[pl.BlockSpec((tm, d), cur), pl.BlockSpec((tm, d), prev2), pl.BlockSpec((tm, d), cur), _const(g3.shape),
                  _resident(w_gate.shape), _resident(w_up.shape), _resident(w_down.shape), _const(g4.shape)],
        out_specs=[pl.BlockSpec((tm, d), cur), pl.BlockSpec((tm, ff), cur), pl.BlockSpec((tm, d), cur), pl.BlockSpec((tm, ff), prev),
                   pl.BlockSpec((tm, ff), prev), pl.BlockSpec((tm, d), prev2), _const((8, 128)), _const(g3.shape), _const(g4.shape)],
        out_shape=[jax.ShapeDtypeStruct((t, d), BF16), jax.ShapeDtypeStruct((t, ff), BF16), jax.ShapeDtypeStruct((t, d), BF16),
                   jax.ShapeDtypeStruct((t, ff), BF16), jax.ShapeDtypeStruct((t, ff), BF16), jax.ShapeDtypeStruct((t, d), F32),
                   jax.ShapeDtypeStruct((8, 128), F32), jax.ShapeDtypeStruct(g3.shape, F32), jax.ShapeDtypeStruct(g4.shape, F32)],
        scratch_shapes=[pltpu.VMEM((2, tm, ff), BF16)] * 2 + [pltpu.VMEM((2, tm, d), BF16)] + [pltpu.VMEM((2, tm, d), F32)] * 2,
        compiler_params=_params("arbitrary"),
    )(h1, h1, target, g3, w_gate, w_up, w_down, g4)


def _ffn_weight_grads(f, dd, dgate, dup, act):
    t, d = f.shape
    ff = dgate.shape[1]
    tm = min(TM_WGRAD, t)
    nt = t // tm
    fc = ff // FF_CHUNKS
    assert FF_CHUNKS == 2

    def body(f_ref, dd_ref, dgate_ref, dup_ref, act_ref, dwg_ref, dwu_ref, dwd_ref, *rest):
        landing, (acc_g, acc_u, acc_d, stage, sem) = rest[:2], rest[2:7]
        start, finish = _core_exchange_ops([dwg_ref, dwd_ref], landing, 0, *rest[7:])
        c, i = pl.program_id(0), pl.program_id(1)
        pl.when((c == 1) & (i == 0))(start)

        @pl.when(i == 0)
        def _():
            acc_g[...] = jnp.zeros_like(acc_g)
            acc_u[...] = jnp.zeros_like(acc_u)
            acc_d[...] = jnp.zeros_like(acc_d)

        fv = f_ref[...]
        acc_g[...] += _dot_tn(fv, dgate_ref[...])
        acc_u[...] += _dot_tn(fv, dup_ref[...])
        acc_d[...] += _dot_tn(act_ref[...], dd_ref[...])

        @pl.when(i == nt - 1)
        def _():
            rows = pl.ds(pl.multiple_of(c * fc, 16), fc)
            copies = []
            for k, (acc, out, transposed) in enumerate(((acc_d, dwd_ref, False), (acc_g, dwg_ref, True), (acc_u, dwu_ref, True))):
                if k >= 2:
                    copies[k - 2].wait()
                stage[k % 2] = (acc[...].T if transposed else acc[...]).astype(BF16)
                copies.append(pltpu.make_async_copy(stage.at[k % 2], out.at[rows, :], sem.at[k % 2]))
                copies[k].start()
            copies[-2].wait()
            copies[-1].wait()

        pl.when((c == 1) & (i == nt - 1))(finish)

    row = lambda c, i: (i, 0)
    col = lambda c, i: (i, c)
    out = pl.pallas_call(
        body, name="ffn_weight_grads", grid=(FF_CHUNKS, nt),
        in_specs=[pl.BlockSpec((tm, d), row), pl.BlockSpec((tm, d), row), pl.BlockSpec((tm, fc), col), pl.BlockSpec((tm, fc), col),
                  pl.BlockSpec((tm, fc), col)],
        out_specs=[ANY] * 5,
        out_shape=[jax.ShapeDtypeStruct((ff, d), BF16)] * 3 + [jax.ShapeDtypeStruct((N_DEV, ff // N_DEV, d), BF16)] * 2,
        scratch_shapes=[pltpu.VMEM((d, fc), F32), pltpu.VMEM((d, fc), F32), pltpu.VMEM((fc, d), F32), pltpu.VMEM((2, fc, d), BF16),
                        pltpu.SemaphoreType.DMA((2,))] + _core_exchange_sems(2),
        compiler_params=_params("arbitrary", "arbitrary"),
    )(f, dd, dgate, dup, act)
    return out[:3], [out[3], None, out[4]]


def _adamw(w, g, m, v):
    m = ADAM_B1 * m + (1.0 - ADAM_B1) * g
    v = ADAM_B2 * v + (1.0 - ADAM_B2) * (g * g)
    m_hat = m / (1.0 - ADAM_B1 ** ADAM_STEP)
    v_hat = v / (1.0 - ADAM_B2 ** ADAM_STEP)
    return -ADAM_LR * (m_hat / (jnp.sqrt(v_hat) + ADAM_EPS) + ADAM_WD * w), m, v


def _sum_slabs(ref):
    total = ref[0].astype(F32)
    for i in range(1, ref.shape[0]):
        total = total + ref[i].astype(F32)
    return total


def _adamw_rows(r, c):
    tr = r
    for cand in range(8, r, 8):
        if r % cand == 0 and cand * c <= ADAMW_BLOCK_ELEMS:
            tr = cand
    return r if r * c <= ADAMW_BLOCK_ELEMS else tr


def _reduce_adamw_carrying(parts, ws, ms, vs, to_reduce, to_exchange, whole, name):
    k, nr, nx = len(ws), len(to_reduce), len(to_exchange)
    r, c = ws[0].shape if k else (8, 128)
    tr = _adamw_rows(r, c)
    steps = r // tr
    travels = nr + nx > 0
    nd = list(whole).count(False)
    assert list(whole) == [False] * nd + [True] * (nx - nd)
    chip_slabs = [jax.ShapeDtypeStruct((N_CHIP, *a.shape[1:]), a.dtype) for a in to_reduce]

    def body(*refs):
        p_refs, w_refs, m_refs, v_refs = (refs[a * k:(a + 1) * k] for a in range(4))
        refs = refs[4 * k:]
        reduced_in, sent, refs = refs[:nr], refs[nr:nr + nx], refs[nr + nx:]
        outs, pairs, sums, landed, refs = refs[:4 * k], refs[4 * k:4 * k + nr], refs[4 * k + nr:4 * k + 2 * nr], \
            refs[4 * k + 2 * nr:4 * k + 2 * nr + nx], refs[4 * k + 2 * nr + nx:]
        mine_v, pair_v, sum_v, refs = refs[:nr], refs[nr:2 * nr], refs[2 * nr:3 * nr], refs[3 * nr:]
        if travels:
            reduce_ops = _pair_then_chip_ops(reduced_in, pairs, sums, mine_v, pair_v, sum_v, *refs[:7])
            direct_ops = _exchange_ops(sent[:nd], landed[:nd], [False] * nd, *refs[7:10])
            gather_ops = _gather_ops(sent[nd:], landed[nd:], *refs[10:13])

            @pl.when(pl.program_id(0) == 0)
            def _():
                direct_ops[0]()
                gather_ops[0]()
                reduce_ops[0]()

        for a in range(k):
            g = _sum_slabs(p_refs[a])
            outs[4 * a][...] = g
            outs[4 * a + 1][...], outs[4 * a + 2][...], outs[4 * a + 3][...] = _adamw(w_refs[a][...], g, m_refs[a][...], v_refs[a][...])

        if travels:
            @pl.when(pl.program_id(0) == steps - 1)
            def _():
                gather_ops[1]()
                reduce_ops[1]()
                gather_ops[2]()
                direct_ops[1]()

    blk = pl.BlockSpec((tr, c), lambda i: (i, 0))
    out = pl.pallas_call(
        body, name=name, grid=(steps,),
        in_specs=[pl.BlockSpec((N_DEV, tr, c), lambda i: (0, i, 0))] * k + [blk] * (3 * k) + [ANY] * (nr + nx),
        out_specs=[blk] * (4 * k) + [ANY] * (2 * nr + nx),
        out_shape=[jax.ShapeDtypeStruct((r, c), F32)] * (4 * k) + chip_slabs + chip_slabs
        + [jax.ShapeDtypeStruct((N_DEV, *a.shape) if w else a.shape, a.dtype) for a, w in zip(to_exchange, whole)],
        scratch_shapes=([pltpu.VMEM(a.shape, a.dtype) for a in chip_slabs] * 3 + _pair_then_chip_sems(nr) + _exchange_sems(nd)
                        + _exchange_sems(nx - nd) if travels else []),
        compiler_params=_params("arbitrary"),
    )(*parts, *ws, *ms, *vs, *to_reduce, *to_exchange)
    return [tuple(out[4 * a:4 * a + 4]) for a in range(k)], out[4 * k + nr:4 * k + 2 * nr], out[4 * k + 2 * nr:]


def _reduce_adamw_small(parts, ws, ms, vs, loss_parts):
    n = len(parts)

    def body(*refs):
        p_refs, w_refs, m_refs, v_refs = (refs[k * n:(k + 1) * n] for k in range(4))
        outs = refs[4 * n + 1:]
        outs[4 * n][...] = _sum_slabs(refs[4 * n])
        for a in range(n):
            g = _sum_slabs(p_refs[a])
            outs[4 * a][...] = g
            outs[4 * a + 1][...], outs[4 * a + 2][...], outs[4 * a + 3][...] = _adamw(w_refs[a][...], g, m_refs[a][...], v_refs[a][...])

    out = pl.pallas_call(
        body, name="adamw_rest",
        out_shape=[jax.ShapeDtypeStruct(w.shape, F32) for w in ws for _ in range(4)] + [jax.ShapeDtypeStruct(loss_parts.shape[1:], F32)],
        compiler_params=pltpu.CompilerParams(vmem_limit_bytes=VMEM_LIMIT_BYTES),
    )(*parts, *ws, *ms, *vs, loss_parts)
    return [tuple(out[4 * a:4 * a + 4]) for a in range(n)], out[4 * n]


def kernel(x, meta_tokens, norm_mix_pre, w_in, conv_w, pool_w, pool_scale, w_out, norm_mix_post, norm_ffn_pre, w_gate, w_up, w_down, norm_ffn_post, loss_target, m_meta_tokens, m_norm_mix_pre, m_w_in, m_conv_w, m_pool_w, m_pool_scale, m_w_out, m_norm_mix_post, m_norm_ffn_pre, m_w_gate, m_w_up, m_w_down, m_norm_ffn_post, v_meta_tokens, v_norm_mix_pre, v_w_in, v_conv_w, v_pool_w, v_pool_scale, v_w_out, v_norm_mix_post, v_norm_ffn_pre, v_w_gate, v_w_up, v_w_down, v_norm_ffn_post):
    n_seq, seq, d = x.shape
    x2d = x.reshape(n_seq * seq, d)
    target = loss_target.reshape(n_seq * seq, d)

    t_ = lambda a: jnp.swapaxes(a[0], 0, 1)
    pw, ps = pool_w[0], pool_scale

    (h1, z, m, pooled, mixed), (win_b, wout_b, meta, conv, a_meta, z_meta), ffn_slabs = _gather_and_mixer_forward(
        x2d, [w_in[0], w_out[0], meta_tokens, conv_w[0]], [t_(w_gate), t_(w_up), w_down[0]], norm_mix_pre, pw, ps, norm_mix_post, n_seq)
    wg_b, wu_b, wd_b = (s.reshape(-1, d) for s in ffn_slabs)
    f, act, dd, dgate, dup, dh1, loss_sum, dg3, dg4 = _ffn_forward_backward(h1, target, norm_ffn_pre, wg_b, wu_b, wd_b, norm_ffn_post)
    ffn_grads, landing = _ffn_weight_grads(f, dd, dgate, dup, act)
    (gx, dwin, dwout, dg1, dg2, dconv, dpw, dps, dmeta), ffn_parts = _mixer_backward(
        x2d, dh1, m, z, pooled, mixed, meta, a_meta, z_meta, norm_mix_pre, win_b, conv, pw, ps, wout_b, norm_mix_post, n_seq,
        ffn_grads, landing)

    dmeta_s = jnp.transpose(dmeta.reshape(N_META, N_DEV, -1), (1, 0, 2))
    dconv_s = jnp.transpose(dconv.reshape(CONV_WIDTH, N_DEV, -1), (1, 0, 2))
    _, (win_parts, wout_parts), last = _reduce_adamw_carrying(
        [], [], [], [], [dwin, dwout.reshape(N_DEV, -1, d)], [dmeta_s, dconv_s, dg1, dg2, dg3, dg4, dpw.astype(BF16), dps, loss_sum],
        [False] * 2 + [True] * 7, "exchange_rest")
    ffn_res, _, _ = _reduce_adamw_carrying(
        ffn_parts, [t_(w_gate), t_(w_up), w_down[0]], [t_(m_w_gate), t_(m_w_up), m_w_down[0]], [t_(v_w_gate), t_(v_w_up), v_w_down[0]],
        [], [], [], "adamw_ffn")
    replicated = last[2:8]

    names = ["meta_tokens", "norm_mix_pre", "w_in", "conv_w", "pool_w", "pool_scale", "w_out", "norm_mix_post", "norm_ffn_pre", "w_gate",
             "w_up", "w_down", "norm_ffn_post"]
    res = {"w_gate": tuple(jnp.swapaxes(o, 0, 1)[None] for o in ffn_res[0]),
           "w_up": tuple(jnp.swapaxes(o, 0, 1)[None] for o in ffn_res[1]), "w_down": tuple(o[None] for o in ffn_res[2])}
    rest_names = ["w_in", "w_out", "meta_tokens", "conv_w", "norm_mix_pre", "norm_mix_post", "norm_ffn_pre", "norm_ffn_post", "pool_w",
                  "pool_scale"]
    rest_res, loss = _reduce_adamw_small(
        [win_parts, wout_parts, last[0], last[1], *replicated],
        [w_in[0], w_out[0], meta_tokens, conv_w[0], norm_mix_pre, norm_mix_post, norm_ffn_pre, norm_ffn_post, pool_w[0], pool_scale],
        [m_w_in[0], m_w_out[0], m_meta_tokens, m_conv_w[0], m_norm_mix_pre, m_norm_mix_post, m_norm_ffn_pre, m_norm_ffn_post,
         m_pool_w[0], m_pool_scale],
        [v_w_in[0], v_w_out[0], v_meta_tokens, v_conv_w[0], v_norm_mix_pre, v_norm_mix_post, v_norm_ffn_pre, v_norm_ffn_post,
         v_pool_w[0], v_pool_scale], last[8])
    for nm, r in zip(rest_names, rest_res):
        res[nm] = tuple(o[None] for o in r) if nm in ("w_in", "w_out", "conv_w", "pool_w") else r

    return (loss[0, 0], gx.reshape(n_seq, seq, d), *[res[nm][0] for nm in names], *[res[nm][1] for nm in names],
            *[res[nm][2] for nm in names], *[res[nm][3] for nm in names])
```

```python
import jax
import jax.numpy as jnp
from jax import lax
from jax.experimental import pallas as pl
from jax.experimental.pallas import tpu as pltpu

F32, BF16 = jnp.float32, jnp.bfloat16
RMS_EPS = 1e-6
N_META = 16
CONV_WIDTH = 3
POOL_WINDOWS = (2, 4, 8, 16)
POOL_GROUP = 128
HALO = 16
N_DEV = 8
MESH_AXES = ("x", "y", "c")
MESH = pl.DeviceIdType.MESH
VMEM_LIMIT_BYTES = 56 * 1024 * 1024
ADAMW_BLOCK_ELEMS = 64 * 1024
TM_MIX = 512
TM_FFN = 256
FFN_CHUNK = 512
FFN_BACKWARD_LAG = 3
TM_WGRAD = 512
FF_CHUNKS = 2

ADAM_LR, ADAM_B1, ADAM_B2, ADAM_EPS, ADAM_WD, ADAM_STEP = 0.001, 0.9, 0.999, 1e-08, 0.01, 10


def _dot(a, b):
    return jnp.dot(a, b, preferred_element_type=F32)


def _dot_nt(a, b):
    return lax.dot_general(a, b, (((1,), (1,)), ((), ())), preferred_element_type=F32)


def _dot_tn(a, b):
    return lax.dot_general(a, b, (((0,), (0,)), ((), ())), preferred_element_type=F32)


def _rms_stats(h):
    rstd = lax.rsqrt(jnp.mean(h * h, axis=-1, keepdims=True) + RMS_EPS)
    return h * rstd, rstd


def _rms_bwd(hat, rstd, g, dy):
    gdy = dy * g
    proj = jnp.mean(gdy * hat, axis=-1, keepdims=True)
    return rstd * (gdy - hat * proj), jnp.sum(dy * hat, axis=0, keepdims=True)


def _params(*semantics):
    return pltpu.CompilerParams(dimension_semantics=semantics or None, vmem_limit_bytes=VMEM_LIMIT_BYTES)


def _resident(shape):
    zeros = (0,) * len(shape)
    return pl.BlockSpec(shape, lambda *_: zeros, pipeline_mode=pl.Buffered(1))


def _const(shape):
    zeros = (0,) * len(shape)
    return pl.BlockSpec(shape, lambda *_: zeros)


ANY = pl.BlockSpec(memory_space=pl.ANY)


def _my_place():
    x, y, c = (lax.axis_index(a) for a in MESH_AXES)
    return x, y, c


def _exchange_sems(n):
    return [pltpu.SemaphoreType.DMA((n, N_DEV - 1)), pltpu.SemaphoreType.DMA((n, N_DEV - 1)), pltpu.SemaphoreType.DMA((n,))]


def _gather_ops(srcs, outs, send_sems, recv_sems, local_sems, core_major=False):
    n = len(srcs)
    x, y, c = _my_place()
    me, sibling = (x, y, c), (x, y, 1 - c)
    chips = [(1 - x, y), (x, 1 - y), (1 - x, 1 - y)]

    def slab(px, py, pc):
        return 4 * pc + 2 * px + py if core_major else 4 * px + 2 * py + pc

    def copy(a, k, block, to, src=None):
        dst = outs[a].at[slab(*block)]
        return pltpu.make_async_remote_copy(
            src_ref=dst if src is None else src, dst_ref=dst, send_sem=send_sems.at[a, k], recv_sem=recv_sems.at[a, k],
            device_id=to, device_id_type=MESH)

    def mine(a):
        return pltpu.make_async_copy(srcs[a], outs[a].at[slab(*me)], local_sems.at[a])

    def first(a):
        return [copy(a, 0, me, sibling, src=srcs[a])] + [copy(a, 1 + j, me, (*chip, c), src=srcs[a]) for j, chip in enumerate(chips)]

    def passed(a, j):
        return copy(a, 4 + j, (*chips[j], c), sibling)

    def start():
        for a in range(n):
            mine(a).start()
            for cp in first(a):
                cp.start()

    def forward():
        for j, chip in enumerate(chips):
            for a in range(n):
                copy(a, 1 + j, (*chip, c), me).wait_recv()
                passed(a, j).start()

    def finish():
        for a in range(n):
            copy(a, 0, sibling, me).wait_recv()
            for j, chip in enumerate(chips):
                copy(a, 4 + j, (*chip, 1 - c), me).wait_recv()
        for a in range(n):
            for cp in first(a) + [passed(a, j) for j in range(len(chips))]:
                cp.wait_send()
            mine(a).wait()

    return start, forward, finish


def _exchange_ops(ins, outs, whole, send_sems, recv_sems, local_sems):
    n = len(ins)
    x, y, c = _my_place()
    me = 4 * x + 2 * y + c

    def src(a, i):
        return ins[a] if whole[a] else ins[a].at[i]

    def mine(a):
        return pltpu.make_async_copy(src(a, me), outs[a].at[me], local_sems.at[a])

    def send(a, k):
        to = (me + k) % N_DEV
        return pltpu.make_async_remote_copy(
            src_ref=src(a, to), dst_ref=outs[a].at[me], send_sem=send_sems.at[a, k - 1], recv_sem=recv_sems.at[a, k - 1],
            device_id=(to // 4, (to // 2) % 2, to % 2), device_id_type=MESH)

    def landed(a, k):
        frm = (me + N_DEV - k) % N_DEV
        return pltpu.make_async_remote_copy(
            src_ref=src(a, frm), dst_ref=outs[a].at[frm], send_sem=send_sems.at[a, k - 1], recv_sem=recv_sems.at[a, k - 1],
            device_id=(x, y, c), device_id_type=MESH)

    def start():
        for a in range(n):
            mine(a).start()
            for k in range(1, N_DEV):
                send(a, k).start()

    def finish():
        for a in range(n):
            for k in range(1, N_DEV):
                landed(a, k).wait_recv()
        for a in range(n):
            for k in range(1, N_DEV):
                send(a, k).wait_send()
            mine(a).wait()

    return start, finish


def _core_exchange_sems(n):
    return [pltpu.SemaphoreType.DMA((n, 4)), pltpu.SemaphoreType.DMA((n, N_DEV)), pltpu.SemaphoreType.DMA((n,))]


def _core_exchange_ops(ins, outs, to_core, send_sems, recv_sems, local_sems):
    n = len(ins)
    x, y, c = _my_place()
    me = 4 * x + 2 * y + c
    others = [(0, 1), (1, 0), (1, 1)]

    def slab(a, p):
        if len(ins[a].shape) == len(outs[a].shape):
            return ins[a].at[p]
        rows = outs[a].shape[1]
        return ins[a].at[pl.ds(pl.multiple_of(p * rows, 16), rows), :]

    def send(a, dx, dy):
        tx, ty = (x + dx) % 2, (y + dy) % 2
        return pltpu.make_async_remote_copy(
            src_ref=slab(a, 4 * to_core + 2 * tx + ty), dst_ref=outs[a].at[me], send_sem=send_sems.at[a, 2 * dx + dy],
            recv_sem=recv_sems.at[a, 2 * (2 * dx + dy) + c], device_id=(tx, ty, to_core), device_id_type=MESH)

    def mine(a):
        return pltpu.make_async_copy(slab(a, 4 * to_core + 2 * x + y), outs[a].at[me], local_sems.at[a])

    def landed(a, dx, dy, sc):
        frm = 4 * ((x + dx) % 2) + 2 * ((y + dy) % 2) + sc
        return pltpu.make_async_remote_copy(
            src_ref=slab(a, 0), dst_ref=outs[a].at[frm], send_sem=send_sems.at[a, 0], recv_sem=recv_sems.at[a, 2 * (2 * dx + dy) + sc],
            device_id=(x, y, c), device_id_type=MESH)

    def start():
        for a in range(n):
            for dx, dy in others:
                send(a, dx, dy).start()
            pl.when(c == to_core)(mine(a).start)
            pl.when(c != to_core)(send(a, 0, 0).start)

    def finish():
        @pl.when(c == to_core)
        def _():
            for a in range(n):
                for dx, dy in [(0, 0)] + others:
                    for sc in (0, 1):
                        if (dx, dy, sc) != (0, 0, to_core):
                            landed(a, dx, dy, sc).wait_recv()
            for a in range(n):
                mine(a).wait()

        @pl.when(c != to_core)
        def _():
            for a in range(n):
                send(a, 0, 0).wait_send()

        for a in range(n):
            for dx, dy in others:
                send(a, dx, dy).wait_send()

    return start, finish


N_CHIP = 4


def _pair_then_chip_sems(n):
    return [pltpu.SemaphoreType.DMA((n, N_CHIP)) for _ in range(6)] + [pltpu.SemaphoreType.DMA((n,))]


def _pair_then_chip_ops(ins, pairs, outs, mine_v, pair_v, sum_v, pair_send, pair_recv, chip_send, chip_recv, load_a, load_b, own_sem):
    n = len(ins)
    x, y, c = _my_place()
    chip = 2 * x + y
    chips = [(0, 0), (0, 1), (1, 0), (1, 1)]
    others = [(0, 1), (1, 0), (1, 1)]

    def to_sibling(a, j):
        px, py = chips[j]
        return pltpu.make_async_remote_copy(
            src_ref=ins[a].at[4 * px + 2 * py + 1 - c], dst_ref=pairs[a].at[j], send_sem=pair_send.at[a, j], recv_sem=pair_recv.at[a, j],
            device_id=(x, y, 1 - c), device_id_type=MESH)

    def spread(a, dx, dy):
        tx, ty = (x + dx) % 2, (y + dy) % 2
        return pltpu.make_async_remote_copy(
            src_ref=sum_v[a].at[2 * tx + ty], dst_ref=outs[a].at[chip], send_sem=chip_send.at[a, 2 * dx + dy],
            recv_sem=chip_recv.at[a, 2 * dx + dy], device_id=(tx, ty, c), device_id_type=MESH)

    def landed(a, dx, dy):
        frm = 2 * ((x + dx) % 2) + (y + dy) % 2
        return pltpu.make_async_remote_copy(
            src_ref=sum_v[a].at[0], dst_ref=outs[a].at[frm], send_sem=chip_send.at[a, 0], recv_sem=chip_recv.at[a, 2 * dx + dy],
            device_id=(x, y, c), device_id_type=MESH)

    def own(a):
        return pltpu.make_async_copy(sum_v[a].at[chip], outs[a].at[chip], own_sem.at[a])

    def pair():
        loads = []
        for a in range(n):
            for j, (px, py) in enumerate(chips):
                to_sibling(a, j).start()
                loads.append(pltpu.make_async_copy(ins[a].at[4 * px + 2 * py + c], mine_v[a].at[j], load_a.at[a, j]))
                loads[-1].start()
        for a in range(n):
            for j in range(N_CHIP):
                to_sibling(a, j).wait_recv()
                loads.append(pltpu.make_async_copy(pairs[a].at[j], pair_v[a].at[j], load_b.at[a, j]))
                loads[-1].start()
        for cp in loads:
            cp.wait()
        for a in range(n):
            sum_v[a][...] = (mine_v[a][...].astype(F32) + pair_v[a][...].astype(F32)).astype(sum_v[a].dtype)

    def start():
        pair()
        for a in range(n):
            own(a).start()
            for dx, dy in others:
                spread(a, dx, dy).start()

    def finish():
        for a in range(n):
            for dx, dy in others:
                landed(a, dx, dy).wait_recv()
        for a in range(n):
            for dx, dy in others:
                spread(a, dx, dy).wait_send()
            for j in range(N_CHIP):
                to_sibling(a, j).wait_send()
            own(a).wait()

    return start, finish


def _window_sum(x, win, ahead):
    n = x.shape[0]
    span = 1
    while span < win:
        x = x + pltpu.roll(x, n - span if ahead else span, 0)
        span *= 2
    return x


def _conv_branch(z, ext_u, conv_ref, tm):
    c_w = z.shape[1] // 4
    b, c, v = z[:, :c_w], z[:, c_w:2 * c_w], z[:, 2 * c_w:3 * c_w]
    u = c * v
    ext_u[pl.ds(HALO, tm), :] = u
    u1 = ext_u[pl.ds(HALO - 1, tm), :]
    u2 = ext_u[pl.ds(HALO - 2, tm), :]
    yc = conv_ref[pl.ds(2, 1), :] * u + conv_ref[pl.ds(1, 1), :] * u1 + conv_ref[pl.ds(0, 1), :] * u2
    return b, c, v, u, u1, u2, yc


def _pool_branch(p, ext_p, pool_w_ref, tm):
    ext_p[pl.ds(HALO, tm), :] = p
    pooled, mixed = [], []
    for g, win in enumerate(POOL_WINDOWS):
        s = _window_sum(ext_p[:, pl.ds(POOL_GROUP * g, POOL_GROUP)], win, ahead=False)[HALO:HALO + tm, :]
        pooled.append((s * (1.0 / win) - p[:, POOL_GROUP * g:POOL_GROUP * (g + 1)]).astype(BF16))
        mixed.append(_dot(pooled[-1], pool_w_ref[g].astype(BF16)))
    return pooled, mixed


def _gather_and_mixer_forward(x2d, mixer_shards, ffn_shards, g1, pool_w, pool_scale, g2, n_seq):
    t, d = x2d.shape
    zs, rs, ms, cs = mixer_shards[0].shape[1], mixer_shards[1].shape[0], mixer_shards[2].shape[1], mixer_shards[3].shape[1]
    zw, cw = N_DEV * zs, N_DEV * cs
    s = t // n_seq
    tm = min(TM_MIX, s)
    nj = s // tm
    n1, n2 = len(mixer_shards), len(ffn_shards)
    dtypes = [BF16, BF16, F32, F32] + [BF16] * n2
    shards = list(mixer_shards) + list(ffn_shards)

    def body(x_ref, *rest):
        shard_refs, (g1_ref, pw_ref, ps_ref, g2_ref), rest = rest[:n1 + n2], rest[n1 + n2:n1 + n2 + 4], rest[n1 + n2 + 4:]
        (h1_ref, z_ref, m_ref, pooled_ref, mixed_ref, win_o, wout_o, meta_o, conv_o, am_o, zm_o), rest = rest[:11], rest[11:]
        slabs, rest = rest[:n1 + n2], rest[n1 + n2:]
        stages, rest = rest[:n1 + n2], rest[n1 + n2:]
        win_v, wout_v, meta_v, conv_v, ext_u, ext_p, sem = rest[:7]
        first = _gather_ops(stages[:n1], slabs[:n1], *rest[7:10])
        later = _gather_ops(stages[n1:], slabs[n1:], *rest[10:13], core_major=True)

        @pl.when((pl.program_id(0) == 0) & (pl.program_id(1) == 0))
        def _():
            for src, dst in zip(shard_refs, stages):
                dst[...] = src[...].astype(dst.dtype)
            first[0]()
            later[0]()
            first[1]()
            first[2]()
            copies = [pltpu.make_async_copy(slabs[0].at[i], win_v.at[:, pl.ds(zs * i, zs)], sem.at[i]) for i in range(N_DEV)]
            copies += [pltpu.make_async_copy(slabs[1].at[i], wout_v.at[pl.ds(rs * i, rs), :], sem.at[N_DEV + i]) for i in range(N_DEV)]
            copies += [pltpu.make_async_copy(slabs[2], meta_v, sem.at[2 * N_DEV]), pltpu.make_async_copy(slabs[3], conv_v, sem.at[2 * N_DEV + 1])]
            for cp in copies:
                cp.start()
            for cp in copies:
                cp.wait()
            copies = [pltpu.make_async_copy(win_v, win_o, sem.at[0]), pltpu.make_async_copy(wout_v, wout_o, sem.at[1])]
            for cp in copies:
                cp.start()
            for i in range(N_DEV):
                meta_o[:, pl.ds(ms * i, ms)] = meta_v[i]
                conv_o[:, pl.ds(cs * i, cs)] = conv_v[i]
            hat, _ = _rms_stats(meta_o[...])
            a = (hat * g1_ref[...]).astype(BF16)
            am_o[...] = a
            zm_o[...] = _dot(a, win_v[...])
            for cp in copies:
                cp.wait()

        @pl.when(pl.program_id(1) == 0)
        def _():
            zm = zm_o[...]
            ext_u[pl.ds(0, HALO), :] = zm[:, cw:2 * cw] * zm[:, 2 * cw:3 * cw]
            ext_p[pl.ds(0, HALO), :] = zm[:, 3 * cw:]

        h0 = x_ref[...]
        hat, _ = _rms_stats(h0)
        z = _dot((hat * g1_ref[...]).astype(BF16), win_v[...])
        z_ref[...] = z.astype(BF16)
        b, _, _, _, _, _, yc = _conv_branch(z, ext_u, conv_o, tm)
        pooled, mixed = _pool_branch(z[:, 3 * cw:], ext_p, pw_ref, tm)
        pooled_ref[...] = jnp.concatenate(pooled, axis=1)
        mixed_ref[...] = jnp.concatenate(mixed, axis=1).astype(BF16)
        ps = ps_ref[...]
        y = [b * yc] + [mixed[g] * ps[:, POOL_GROUP * g:POOL_GROUP * (g + 1)] for g in range(len(POOL_WINDOWS))]
        m = _dot(jnp.concatenate(y, axis=1).astype(BF16), wout_v[...])
        m_ref[...] = m
        m_hat, _ = _rms_stats(m)
        h1_ref[...] = h0 + m_hat * g2_ref[...]
        ext_u[pl.ds(0, HALO), :] = ext_u[pl.ds(tm, HALO), :]
        ext_p[pl.ds(0, HALO), :] = ext_p[pl.ds(tm, HALO), :]

        @pl.when((pl.program_id(0) == n_seq - 1) & (pl.program_id(1) == nj - 1))
        def _():
            later[1]()
            later[2]()

    row = lambda b, j: (b * nj + j, 0)
    vmem = pl.BlockSpec(memory_space=pltpu.VMEM)
    small = [(N_META, d), (CONV_WIDTH, cw), (N_META, d), (N_META, zw)]
    out = pl.pallas_call(
        body, name="gather_and_mixer_forward", grid=(n_seq, nj),
        in_specs=[pl.BlockSpec((tm, d), row)] + [vmem] * (n1 + n2)
        + [_const(g1.shape), _const(pool_w.shape), _const(pool_scale.shape), _const(g2.shape)],
        out_specs=[pl.BlockSpec((tm, d), row), pl.BlockSpec((tm, zw), row), pl.BlockSpec((tm, d), row), pl.BlockSpec((tm, cw), row),
                   pl.BlockSpec((tm, cw), row), ANY, ANY] + [_const(sh) for sh in small] + [ANY] * (n1 + n2),
        out_shape=[jax.ShapeDtypeStruct((t, d), F32), jax.ShapeDtypeStruct((t, zw), BF16), jax.ShapeDtypeStruct((t, d), F32),
                   jax.ShapeDtypeStruct((t, cw), BF16), jax.ShapeDtypeStruct((t, cw), BF16),
                   jax.ShapeDtypeStruct((d, zw), BF16), jax.ShapeDtypeStruct((d, d), BF16),
                   jax.ShapeDtypeStruct(small[0], F32), jax.ShapeDtypeStruct(small[1], F32), jax.ShapeDtypeStruct(small[2], BF16),
                   jax.ShapeDtypeStruct(small[3], F32)]
        + [jax.ShapeDtypeStruct((N_DEV, *a.shape), dt) for a, dt in zip(shards, dtypes)],
        scratch_shapes=[pltpu.VMEM(a.shape, dt) for a, dt in zip(shards, dtypes)]
        + [pltpu.VMEM((d, zw), BF16), pltpu.VMEM((d, d), BF16), pltpu.VMEM((N_DEV, N_META, ms), F32),
           pltpu.VMEM((N_DEV, CONV_WIDTH, cs), F32), pltpu.VMEM((tm + HALO, cw), F32), pltpu.VMEM((tm + HALO, cw), F32),
           pltpu.SemaphoreType.DMA((2 * N_DEV + 2,))] + _exchange_sems(n1) + _exchange_sems(n2),
        compiler_params=_params("arbitrary", "arbitrary"),
    )(x2d, *shards, g1, pool_w, pool_scale, g2)
    return out[:5], out[5:11], out[11 + n1:]


def _mixer_backward(x2d, dh1, m, z, pooled, mixed, meta, a_meta, z_meta, g1, w_in, conv_w, pool_w, pool_scale, w_out, g2, n_seq,
                    to_exchange, landing):
    t, d = x2d.shape
    zw = w_in.shape[1]
    cw = zw // 4
    s = t // n_seq
    tm = min(TM_MIX, s)
    nj = s // tm
    n_groups = len(POOL_WINDOWS)
    zs = zw // N_DEV
    nx = len(to_exchange)
    n_in = 17
    given = [k for k, a in enumerate(landing) if a is not None]
    fresh = [k for k, a in enumerate(landing) if a is None]

    def body(x_ref, dh1_ref, m_ref, z_ref, zprev_ref, pooled_ref, mixed_ref, meta_ref, am_ref, zm_ref, g1_ref, win_ref, conv_ref, pw_ref, ps_ref, wout_ref,
             g2_ref, *rest):
        sent, rest = rest[:nx], rest[nx + len(given):]
        gx_ref, dwin_ref, dwout_ref, dg1_ref, dg2_ref, dconv_ref, dpw_ref, dps_ref, dmeta_ref = rest[:9]
        landed, rest = rest[9:9 + nx], rest[9 + nx:]
        ext_u, ext_dyc, ext_dq, acc_win, acc_wout, dz_meta, stage16, sem = rest[:8]
        north = _core_exchange_ops(sent, landed, 1, *rest[8:11])
        south = _core_exchange_ops([sent[k] for k in fresh], [landed[k] for k in fresh], 0, *rest[11:14])

        def start():
            north[0]()
            south[0]()

        def finish():
            south[1]()
            north[1]()

        b_id, j = pl.program_id(0), pl.program_id(1)
        jr = nj - 1 - j
        pl.when((b_id == 0) & (j == 0))(start)

        @pl.when((b_id == 0) & (j == 0))
        def _():
            acc_win[...] = jnp.zeros_like(acc_win)
            acc_wout[...] = jnp.zeros_like(acc_wout)
            dz_meta[...] = jnp.zeros_like(dz_meta)
            for r in (dg1_ref, dg2_ref, dconv_ref, dpw_ref, dps_ref, dmeta_ref):
                r[...] = jnp.zeros_like(r)

        @pl.when(j == 0)
        def _():
            ext_dyc[pl.ds(tm, HALO), :] = jnp.zeros((HALO, cw), F32)
            ext_dq[pl.ds(tm, HALO), :] = jnp.zeros((HALO, cw), F32)

        zm = zm_ref[...]
        halo = jnp.where(jr == 0, zm, zprev_ref[...].astype(F32))
        ext_u[pl.ds(0, HALO), :] = halo[:, cw:2 * cw] * halo[:, 2 * cw:3 * cw]

        dh1v = dh1_ref[...]
        m_hat, m_rstd = _rms_stats(m_ref[...])
        dm, dg2 = _rms_bwd(m_hat, m_rstd, g2_ref[...], dh1v)
        dg2_ref[...] += dg2
        dm = dm.astype(BF16)
        dycat = _dot_nt(dm, wout_ref[...])

        b, c, v, u, u1, u2, yc = _conv_branch(z_ref[...].astype(F32), ext_u, conv_ref, tm)
        mixed = [mixed_ref[:, pl.ds(POOL_GROUP * g, POOL_GROUP)].astype(F32) for g in range(n_groups)]
        ps = ps_ref[...]
        y = [b * yc] + [mixed[g] * ps[:, POOL_GROUP * g:POOL_GROUP * (g + 1)] for g in range(n_groups)]
        ycat = jnp.concatenate(y, axis=1).astype(BF16)
        acc_wout[...] += _dot_tn(ycat, dm)

        dyconv = dycat[:, :cw]
        db = dyconv * yc
        dyc = dyconv * b
        ext_dyc[pl.ds(0, tm), :] = dyc
        du = (conv_ref[pl.ds(2, 1), :] * dyc + conv_ref[pl.ds(1, 1), :] * ext_dyc[pl.ds(1, tm), :]
              + conv_ref[pl.ds(0, 1), :] * ext_dyc[pl.ds(2, tm), :])
        dconv_ref[pl.ds(2, 1), :] += jnp.sum(dyc * u, axis=0, keepdims=True)
        dconv_ref[pl.ds(1, 1), :] += jnp.sum(dyc * u1, axis=0, keepdims=True)
        dconv_ref[pl.ds(0, 1), :] += jnp.sum(dyc * u2, axis=0, keepdims=True)

        dp = []
        for g, win in enumerate(POOL_WINDOWS):
            lanes = pl.ds(POOL_GROUP * g, POOL_GROUP)
            dypool = dycat[:, cw + POOL_GROUP * g:cw + POOL_GROUP * (g + 1)]
            dps_ref[:, lanes] += jnp.sum(dypool * mixed[g], axis=0, keepdims=True)
            dmixed = (dypool * ps[:, POOL_GROUP * g:POOL_GROUP * (g + 1)]).astype(BF16)
            dq = _dot_nt(dmixed, pw_ref[g].astype(BF16))
            dpw_ref[g] += _dot_tn(pooled_ref[:, lanes], dmixed)
            ext_dq[pl.ds(0, tm), lanes] = dq
            acc = _window_sum(ext_dq[:, lanes], win, ahead=True)[0:tm, :]
            dp.append(acc * (1.0 / win) - dq)

        dz = jnp.concatenate([db, du * v, du * c] + dp, axis=1).astype(BF16)
        da = _dot_nt(dz, win_ref[...])
        h0 = x_ref[...]
        hat0, rstd0 = _rms_stats(h0)
        g1 = g1_ref[...]
        acc_win[...] += _dot_tn((hat0 * g1).astype(BF16), dz)
        dh0, dg1 = _rms_bwd(hat0, rstd0, g1, da)
        dg1_ref[...] += dg1
        gx_ref[...] = dh1v + dh0

        ext_dyc[pl.ds(tm, HALO), :] = ext_dyc[pl.ds(0, HALO), :]
        ext_dq[pl.ds(tm, HALO), :] = ext_dq[pl.ds(0, HALO), :]

        @pl.when(jr == 0)
        def _():
            ext_dyc[pl.ds(tm - HALO, HALO), :] = jnp.zeros((HALO, cw), F32)
            ext_dq[pl.ds(tm - HALO, HALO), :] = jnp.zeros((HALO, cw), F32)
            du_m = (conv_ref[pl.ds(1, 1), :] * ext_dyc[pl.ds(tm - HALO + 1, HALO), :]
                    + conv_ref[pl.ds(0, 1), :] * ext_dyc[pl.ds(tm - HALO + 2, HALO), :])
            dp_m = []
            for g, win in enumerate(POOL_WINDOWS):
                lanes = pl.ds(POOL_GROUP * g, POOL_GROUP)
                acc = ext_dq[pl.ds(tm - HALO + 1, HALO), lanes]
                for k in range(2, win):
                    acc = acc + ext_dq[pl.ds(tm - HALO + k, HALO), lanes]
                dp_m.append(acc * (1.0 / win))
            dz_meta[...] += jnp.concatenate(
                [jnp.zeros((HALO, cw), F32), du_m * zm[:, 2 * cw:3 * cw], du_m * zm[:, cw:2 * cw]] + dp_m, axis=1)

        @pl.when((b_id == n_seq - 1) & (j == nj - 1))
        def _():
            dz_m = dz_meta[...].astype(BF16)
            acc_win[...] += _dot_tn(am_ref[...], dz_m)
            hat_m, rstd_m = _rms_stats(meta_ref[...])
            dmeta, dg1_m = _rms_bwd(hat_m, rstd_m, g1, _dot_nt(dz_m, win_ref[...]))
            dg1_ref[...] += dg1_m
            dmeta_ref[...] = dmeta
            pieces = [(acc_win, zs * i, dwin_ref.at[i]) for i in range(N_DEV)]
            pieces += [(acc_wout, zs * i, dwout_ref.at[:, pl.ds(zs * i, zs)]) for i in range(d // zs)]
            copies = []
            for k, (acc, col, dst) in enumerate(pieces):
                if k >= 2:
                    copies[k - 2].wait()
                stage16[k % 2] = acc[:, pl.ds(col, zs)].astype(BF16)
                copies.append(pltpu.make_async_copy(stage16.at[k % 2], dst, sem.at[k % 2]))
                copies[k].start()
            copies[-2].wait()
            copies[-1].wait()
            finish()

    row = lambda b, j: (b * nj + nj - 1 - j, 0)
    prev = lambda b, j: (jnp.maximum((b * s + (nj - 1 - j) * tm) // HALO - 1, 0), 0)
    small = [g1.shape, g2.shape, conv_w.shape, pool_w.shape, pool_scale.shape, meta.shape]
    out = pl.pallas_call(
        body, name="mixer_backward", grid=(n_seq, nj),
        in_specs=[pl.BlockSpec((tm, d), row), pl.BlockSpec((tm, d), row), pl.BlockSpec((tm, d), row), pl.BlockSpec((tm, zw), row),
                  pl.BlockSpec((HALO, zw), prev), pl.BlockSpec((tm, cw), row), pl.BlockSpec((tm, cw), row), _const(meta.shape), _const(a_meta.shape), _const(z_meta.shape), _const(g1.shape),
                  _resident(w_in.shape), _const(conv_w.shape), _const(pool_w.shape), _const(pool_scale.shape), _resident(w_out.shape),
                  _const(g2.shape)] + [ANY] * (nx + len(given)),
        out_specs=[pl.BlockSpec((tm, d), row), ANY, ANY] + [_const(sh) for sh in small] + [ANY] * nx,
        out_shape=[jax.ShapeDtypeStruct((t, d), F32), jax.ShapeDtypeStruct((N_DEV, d, zs), BF16),
                   jax.ShapeDtypeStruct(w_out.shape, BF16)] + [jax.ShapeDtypeStruct(sh, F32) for sh in small]
        + [jax.ShapeDtypeStruct((N_DEV, a.shape[0] // N_DEV, a.shape[1]), a.dtype) for a in to_exchange],
        input_output_aliases={n_in + nx + at: 9 + k for at, k in enumerate(given)},
        scratch_shapes=[pltpu.VMEM((tm + HALO, cw), F32)] * 3
        + [pltpu.VMEM(w_in.shape, F32), pltpu.VMEM(w_out.shape, F32), pltpu.VMEM((HALO, zw), F32), pltpu.VMEM((2, d, zs), BF16),
           pltpu.SemaphoreType.DMA((2,))] + _core_exchange_sems(nx) + _core_exchange_sems(len(fresh)),
        compiler_params=_params("arbitrary", "arbitrary"),
    )(x2d, dh1, m, z, z, pooled, mixed, meta, a_meta, z_meta, g1, w_in, conv_w, pool_w, pool_scale, w_out, g2, *to_exchange, *[landing[k] for k in given])
    return out[:9], out[9:]


def _ffn_forward_backward(h1, target, g3, w_gate, w_up, w_down, g4):
    t, d = h1.shape
    ff = w_gate.shape[0]
    tm = min(TM_FFN, t)
    nt = t // tm
    chunks = [(s, min(FFN_CHUNK, ff - s)) for s in range(0, ff, FFN_CHUNK)]

    def body(h1_ref, h1pp_ref, tgt_ref, g3_ref, wg_hbm, wu_hbm, wd_ref, g4_ref,
             f_ref, act_ref, dd_ref, dgate_ref, dup_ref, dh1_ref, loss_ref, dg3_ref, dg4_ref, *slots):
        gate_s, up_s, dd_s, dh2_s, df_s, wgu, wsem = slots
        i = pl.program_id(0)

        def gu(s, n):
            return wgu.at[pl.ds(2 * s, 2 * n), :]

        @pl.when(i == 0)
        def _():
            copies = []
            for k, (s, n) in enumerate(chunks):
                copies += [pltpu.make_async_copy(wg_hbm.at[pl.ds(s, n), :], wgu.at[pl.ds(2 * s, n), :], wsem.at[2 * k]),
                           pltpu.make_async_copy(wu_hbm.at[pl.ds(s, n), :], wgu.at[pl.ds(2 * s + n, n), :], wsem.at[2 * k + 1])]
            for cp in copies:
                cp.start()
            for cp in copies:
                cp.wait()

        def forward(slot):
            h1v = h1_ref[...]
            hat, _ = _rms_stats(h1v)
            f = (hat * g3_ref[...]).astype(BF16)
            f_ref[...] = f
            s, n = chunks[0]
            both = _dot_nt(f_ref[...], gu(s, n)[...])
            yield
            down = None
            for k, (s, n) in enumerate(chunks):
                gate, up = both[:, :n], both[:, n:]
                gate_s.at[slot][:, pl.ds(s, n)] = gate.astype(BF16)
                up_s.at[slot][:, pl.ds(s, n)] = up.astype(BF16)
                act = (gate * jax.nn.sigmoid(gate) * up).astype(BF16)
                act_ref[:, pl.ds(s, n)] = act
                if k + 1 < len(chunks):
                    s1, n1 = chunks[k + 1]
                    both = _dot_nt(f_ref[...], gu(s1, n1)[...])
                yield
                part = _dot(act_ref[:, pl.ds(s, n)], wd_ref[pl.ds(s, n), :])
                down = part if down is None else down + part
                yield
            d_hat, d_rstd = _rms_stats(down)
            g4 = g4_ref[...]
            err = h1v + d_hat * g4 - tgt_ref[...]
            loss_ref[...] += jnp.sum(err * err) * (0.5 / d)
            dh2 = err * (1.0 / d)
            dh2_s.at[slot][...] = dh2
            dd, dg4 = _rms_bwd(d_hat, d_rstd, g4, dh2)
            dg4_ref[...] += dg4
            dd = dd.astype(BF16)
            dd_ref[...] = dd
            dd_s.at[slot][...] = dd

        def backward(slot):
            s, n = chunks[0]
            dact = _dot_nt(dd_s.at[slot][...], wd_ref[pl.ds(s, n), :])
            yield
            df = None
            for k, (s, n) in enumerate(chunks):
                gate = gate_s.at[slot][:, pl.ds(s, n)].astype(F32)
                up = up_s.at[slot][:, pl.ds(s, n)].astype(F32)
                sig = jax.nn.sigmoid(gate)
                dup = (dact * (gate * sig)).astype(BF16)
                dgate = (dact * up * (sig * (1.0 + gate * (1.0 - sig)))).astype(BF16)
                dup_ref[:, pl.ds(s, n)] = dup
                dgate_ref[:, pl.ds(s, n)] = dgate
                if k + 1 < len(chunks):
                    s1, n1 = chunks[k + 1]
                    dact = _dot_nt(dd_s.at[slot][...], wd_ref[pl.ds(s1, n1), :])
                yield
                part = _dot(jnp.concatenate([dgate_ref[:, pl.ds(s, n)], dup_ref[:, pl.ds(s, n)]], axis=1), gu(s, n)[...])
                df = part if df is None else df + part
                yield
            df_s.at[slot][...] = df

        def last(slot):
            hat, rstd = _rms_stats(h1pp_ref[...])
            dh1, dg3 = _rms_bwd(hat, rstd, g3_ref[...], df_s.at[slot][...])
            dg3_ref[...] += dg3
            dh1_ref[...] = dh2_s.at[slot][...] + dh1

        def emit(parity, with_forward, with_backward, with_last):
            fwd = forward(parity) if with_forward else iter(())
            bwd = backward(1 - parity) if with_backward else iter(())
            next(fwd, None)
            if with_last:
                last(parity)
            for _ in range(FFN_BACKWARD_LAG):
                next(fwd, None)
            alive = True
            while alive:
                alive = next(bwd, True) is None
                alive = (next(fwd, True) is None) or alive

        @pl.when(i == 0)
        def _():
            for r in (loss_ref, dg3_ref, dg4_ref, gate_s, up_s, dd_s, dh2_s, df_s):
                r[...] = jnp.zeros_like(r)

        @pl.when(i < nt)
        def _():
            emit(i % 2, True, True, True)

        @pl.when(i == nt)
        def _():
            emit(nt % 2, False, True, True)

        @pl.when(i == nt + 1)
        def _():
            emit((nt + 1) % 2, False, False, True)

    cur = lambda i: (jnp.minimum(i, nt - 1), 0)
    prev = lambda i: (jnp.clip(i - 1, 0, nt - 1), 0)
    prev2 = lambda i: (jnp.clip(i - 2, 0, nt - 1), 0)
    return pl.pallas_call(
        body, name="ffn_forward_backward", grid=(nt + 2,),
        in_specs=[pl.BlockSpec((tm, d), cur), pl.BlockSpec((tm, d), prev2), pl.BlockSpec((tm, d), cur), _const(g3.shape),
                  ANY, ANY, _resident(w_down.shape), _const(g4.shape)],
        out_specs=[pl.BlockSpec((tm, d), cur), pl.BlockSpec((tm, ff), cur), pl.BlockSpec((tm, d), cur), pl.BlockSpec((tm, ff), prev),
                   pl.BlockSpec((tm, ff), prev), pl.BlockSpec((tm, d), prev2), _const((8, 128)), _const(g3.shape), _const(g4.shape)],
        out_shape=[jax.ShapeDtypeStruct((t, d), BF16), jax.ShapeDtypeStruct((t, ff), BF16), jax.ShapeDtypeStruct((t, d), BF16),
                   jax.ShapeDtypeStruct((t, ff), BF16), jax.ShapeDtypeStruct((t, ff), BF16), jax.ShapeDtypeStruct((t, d), F32),
                   jax.ShapeDtypeStruct((8, 128), F32), jax.ShapeDtypeStruct(g3.shape, F32), jax.ShapeDtypeStruct(g4.shape, F32)],
        scratch_shapes=[pltpu.VMEM((2, tm, ff), BF16)] * 2 + [pltpu.VMEM((2, tm, d), BF16)] + [pltpu.VMEM((2, tm, d), F32)] * 2
        + [pltpu.VMEM((2 * ff, d), BF16), pltpu.SemaphoreType.DMA((2 * len(chunks),))],
        compiler_params=_params("arbitrary"),
    )(h1, h1, target, g3, w_gate, w_up, w_down, g4)


def _ffn_weight_grads(f, dd, dgate, dup, act):
    t, d = f.shape
    ff = dgate.shape[1]
    tm = min(TM_WGRAD, t)
    nt = t // tm
    fc = ff // FF_CHUNKS
    assert FF_CHUNKS == 2

    def body(f_ref, dd_ref, dgate_ref, dup_ref, act_ref, dwg_ref, dwu_ref, dwd_ref, *rest):
        landing, (acc_g, acc_u, acc_d, stage, sem) = rest[:2], rest[2:7]
        start, finish = _core_exchange_ops([dwg_ref, dwd_ref], landing, 0, *rest[7:])
        c, i = pl.program_id(0), pl.program_id(1)
        pl.when((c == 1) & (i == 0))(start)

        @pl.when(i == 0)
        def _():
            acc_g[...] = jnp.zeros_like(acc_g)
            acc_u[...] = jnp.zeros_like(acc_u)
            acc_d[...] = jnp.zeros_like(acc_d)

        fv = f_ref[...]
        acc_g[...] += _dot_tn(fv, dgate_ref[...])
        acc_u[...] += _dot_tn(fv, dup_ref[...])
        acc_d[...] += _dot_tn(act_ref[...], dd_ref[...])

        @pl.when(i == nt - 1)
        def _():
            rows = pl.ds(pl.multiple_of(c * fc, 16), fc)
            copies = []
            for k, (acc, out, transposed) in enumerate(((acc_d, dwd_ref, False), (acc_g, dwg_ref, True), (acc_u, dwu_ref, True))):
                if k >= 2:
                    copies[k - 2].wait()
                stage[k % 2] = (acc[...].T if transposed else acc[...]).astype(BF16)
                copies.append(pltpu.make_async_copy(stage.at[k % 2], out.at[rows, :], sem.at[k % 2]))
                copies[k].start()
            copies[-2].wait()
            copies[-1].wait()

        pl.when((c == 1) & (i == nt - 1))(finish)

    row = lambda c, i: (i, 0)
    col = lambda c, i: (i, c)
    out = pl.pallas_call(
        body, name="ffn_weight_grads", grid=(FF_CHUNKS, nt),
        in_specs=[pl.BlockSpec((tm, d), row), pl.BlockSpec((tm, d), row), pl.BlockSpec((tm, fc), col), pl.BlockSpec((tm, fc), col),
                  pl.BlockSpec((tm, fc), col)],
        out_specs=[ANY] * 5,
        out_shape=[jax.ShapeDtypeStruct((ff, d), BF16)] * 3 + [jax.ShapeDtypeStruct((N_DEV, ff // N_DEV, d), BF16)] * 2,
        scratch_shapes=[pltpu.VMEM((d, fc), F32), pltpu.VMEM((d, fc), F32), pltpu.VMEM((fc, d), F32), pltpu.VMEM((2, fc, d), BF16),
                        pltpu.SemaphoreType.DMA((2,))] + _core_exchange_sems(2),
        compiler_params=_params("arbitrary", "arbitrary"),
    )(f, dd, dgate, dup, act)
    return out[:3], [out[3], None, out[4]]


def _adamw(w, g, m, v):
    m = ADAM_B1 * m + (1.0 - ADAM_B1) * g
    v = ADAM_B2 * v + (1.0 - ADAM_B2) * (g * g)
    m_hat = m / (1.0 - ADAM_B1 ** ADAM_STEP)
    v_hat = v / (1.0 - ADAM_B2 ** ADAM_STEP)
    return -ADAM_LR * (m_hat / (jnp.sqrt(v_hat) + ADAM_EPS) + ADAM_WD * w), m, v


def _sum_slabs(ref):
    total = ref[0].astype(F32)
    for i in range(1, ref.shape[0]):
        total = total + ref[i].astype(F32)
    return total


def _adamw_rows(r, c):
    tr = r
    for cand in range(8, r, 8):
        if r % cand == 0 and cand * c <= ADAMW_BLOCK_ELEMS:
            tr = cand
    return r if r * c <= ADAMW_BLOCK_ELEMS else tr


def _reduce_adamw_carrying(parts, ws, ms, vs, to_reduce, to_exchange, whole, name):
    k, nr, nx = len(ws), len(to_reduce), len(to_exchange)
    r, c = ws[0].shape if k else (8, 128)
    tr = _adamw_rows(r, c)
    steps = r // tr
    travels = nr + nx > 0
    nd = list(whole).count(False)
    assert list(whole) == [False] * nd + [True] * (nx - nd)
    chip_slabs = [jax.ShapeDtypeStruct((N_CHIP, *a.shape[1:]), a.dtype) for a in to_reduce]

    def body(*refs):
        p_refs, w_refs, m_refs, v_refs = (refs[a * k:(a + 1) * k] for a in range(4))
        refs = refs[4 * k:]
        reduced_in, sent, refs = refs[:nr], refs[nr:nr + nx], refs[nr + nx:]
        outs, pairs, sums, landed, refs = refs[:4 * k], refs[4 * k:4 * k + nr], refs[4 * k + nr:4 * k + 2 * nr], \
            refs[4 * k + 2 * nr:4 * k + 2 * nr + nx], refs[4 * k + 2 * nr + nx:]
        mine_v, pair_v, sum_v, refs = refs[:nr], refs[nr:2 * nr], refs[2 * nr:3 * nr], refs[3 * nr:]
        if travels:
            reduce_ops = _pair_then_chip_ops(reduced_in, pairs, sums, mine_v, pair_v, sum_v, *refs[:7])
            direct_ops = _exchange_ops(sent[:nd], landed[:nd], [False] * nd, *refs[7:10])
            gather_ops = _gather_ops(sent[nd:], landed[nd:], *refs[10:13])

            @pl.when(pl.program_id(0) == 0)
            def _():
                direct_ops[0]()
                gather_ops[0]()
                reduce_ops[0]()

        for a in range(k):
            g = _sum_slabs(p_refs[a])
            outs[4 * a][...] = g
            outs[4 * a + 1][...], outs[4 * a + 2][...], outs[4 * a + 3][...] = _adamw(w_refs[a][...], g, m_refs[a][...], v_refs[a][...])

        if travels:
            @pl.when(pl.program_id(0) == steps - 1)
            def _():
                gather_ops[1]()
                reduce_ops[1]()
                gather_ops[2]()
                direct_ops[1]()

    blk = pl.BlockSpec((tr, c), lambda i: (i, 0))
    out = pl.pallas_call(
        body, name=name, grid=(steps,),
        in_specs=[pl.BlockSpec((N_DEV, tr, c), lambda i: (0, i, 0))] * k + [blk] * (3 * k) + [ANY] * (nr + nx),
        out_specs=[blk] * (4 * k) + [ANY] * (2 * nr + nx),
        out_shape=[jax.ShapeDtypeStruct((r, c), F32)] * (4 * k) + chip_slabs + chip_slabs
        + [jax.ShapeDtypeStruct((N_DEV, *a.shape) if w else a.shape, a.dtype) for a, w in zip(to_exchange, whole)],
        scratch_shapes=([pltpu.VMEM(a.shape, a.dtype) for a in chip_slabs] * 3 + _pair_then_chip_sems(nr) + _exchange_sems(nd)
                        + _exchange_sems(nx - nd) if travels else []),
        compiler_params=_params("arbitrary"),
    )(*parts, *ws, *ms, *vs, *to_reduce, *to_exchange)
    return [tuple(out[4 * a:4 * a + 4]) for a in range(k)], out[4 * k + nr:4 * k + 2 * nr], out[4 * k + 2 * nr:]


def _reduce_adamw_small(parts, ws, ms, vs, loss_parts):
    n = len(parts)

    def body(*refs):
        p_refs, w_refs, m_refs, v_refs = (refs[k * n:(k + 1) * n] for k in range(4))
        outs = refs[4 * n + 1:]
        outs[4 * n][...] = _sum_slabs(refs[4 * n])
        for a in range(n):
            g = _sum_slabs(p_refs[a])
            outs[4 * a][...] = g
            outs[4 * a + 1][...], outs[4 * a + 2][...], outs[4 * a + 3][...] = _adamw(w_refs[a][...], g, m_refs[a][...], v_refs[a][...])

    out = pl.pallas_call(
        body, name="adamw_rest",
        out_shape=[jax.ShapeDtypeStruct(w.shape, F32) for w in ws for _ in range(4)] + [jax.ShapeDtypeStruct(loss_parts.shape[1:], F32)],
        compiler_params=pltpu.CompilerParams(vmem_limit_bytes=VMEM_LIMIT_BYTES),
    )(*parts, *ws, *ms, *vs, loss_parts)
    return [tuple(out[4 * a:4 * a + 4]) for a in range(n)], out[4 * n]


def kernel(x, meta_tokens, norm_mix_pre, w_in, conv_w, pool_w, pool_scale, w_out, norm_mix_post, norm_ffn_pre, w_gate, w_up, w_down, norm_ffn_post, loss_target, m_meta_tokens, m_norm_mix_pre, m_w_in, m_conv_w, m_pool_w, m_pool_scale, m_w_out, m_norm_mix_post, m_norm_ffn_pre, m_w_gate, m_w_up, m_w_down, m_norm_ffn_post, v_meta_tokens, v_norm_mix_pre, v_w_in, v_conv_w, v_pool_w, v_pool_scale, v_w_out, v_norm_mix_post, v_norm_ffn_pre, v_w_gate, v_w_up, v_w_down, v_norm_ffn_post):
    n_seq, seq, d = x.shape
    x2d = x.reshape(n_seq * seq, d)
    target = loss_target.reshape(n_seq * seq, d)

    t_ = lambda a: jnp.swapaxes(a[0], 0, 1)
    pw, ps = pool_w[0], pool_scale

    (h1, z, m, pooled, mixed), (win_b, wout_b, meta, conv, a_meta, z_meta), ffn_slabs = _gather_and_mixer_forward(
        x2d, [w_in[0], w_out[0], meta_tokens, conv_w[0]], [t_(w_gate), t_(w_up), w_down[0]], norm_mix_pre, pw, ps, norm_mix_post, n_seq)
    wg_b, wu_b, wd_b = (s.reshape(-1, d) for s in ffn_slabs)
    f, act, dd, dgate, dup, dh1, loss_sum, dg3, dg4 = _ffn_forward_backward(h1, target, norm_ffn_pre, wg_b, wu_b, wd_b, norm_ffn_post)
    ffn_grads, landing = _ffn_weight_grads(f, dd, dgate, dup, act)
    (gx, dwin, dwout, dg1, dg2, dconv, dpw, dps, dmeta), ffn_parts = _mixer_backward(
        x2d, dh1, m, z, pooled, mixed, meta, a_meta, z_meta, norm_mix_pre, win_b, conv, pw, ps, wout_b, norm_mix_post, n_seq,
        ffn_grads, landing)

    dmeta_s = jnp.transpose(dmeta.reshape(N_META, N_DEV, -1), (1, 0, 2))
    dconv_s = jnp.transpose(dconv.reshape(CONV_WIDTH, N_DEV, -1), (1, 0, 2))
    _, (win_parts, wout_parts), last = _reduce_adamw_carrying(
        [], [], [], [], [dwin, dwout.reshape(N_DEV, -1, d)], [dmeta_s, dconv_s, dg1, dg2, dg3, dg4, dpw.astype(BF16), dps, loss_sum],
        [False] * 2 + [True] * 7, "exchange_rest")
    ffn_res, _, _ = _reduce_adamw_carrying(
        ffn_parts, [t_(w_gate), t_(w_up), w_down[0]], [t_(m_w_gate), t_(m_w_up), m_w_down[0]], [t_(v_w_gate), t_(v_w_up), v_w_down[0]],
        [], [], [], "adamw_ffn")
    replicated = last[2:8]

    names = ["meta_tokens", "norm_mix_pre", "w_in", "conv_w", "pool_w", "pool_scale", "w_out", "norm_mix_post", "norm_ffn_pre", "w_gate",
             "w_up", "w_down", "norm_ffn_post"]
    res = {"w_gate": tuple(jnp.swapaxes(o, 0, 1)[None] for o in ffn_res[0]),
           "w_up": tuple(jnp.swapaxes(o, 0, 1)[None] for o in ffn_res[1]), "w_down": tuple(o[None] for o in ffn_res[2])}
    rest_names = ["w_in", "w_out", "meta_tokens", "conv_w", "norm_mix_pre", "norm_mix_post", "norm_ffn_pre", "norm_ffn_post", "pool_w",
                  "pool_scale"]
    rest_res, loss = _reduce_adamw_small(
        [win_parts, wout_parts, last[0], last[1], *replicated],
        [w_in[0], w_out[0], meta_tokens, conv_w[0], norm_mix_pre, norm_mix_post, norm_ffn_pre, norm_ffn_post, pool_w[0], pool_scale],
        [m_w_in[0], m_w_out[0], m_meta_tokens, m_conv_w[0], m_norm_mix_pre, m_norm_mix_post, m_norm_ffn_pre, m_norm_ffn_post,
         m_pool_w[0], m_pool_scale],
        [v_w_in[0], v_w_out[0], v_meta_tokens, v_conv_w[0], v_norm_mix_pre, v_norm_mix_post, v_norm_ffn_pre, v_norm_ffn_post,
         v_pool_w[0], v_pool_scale], last[8])
    for nm, r in zip(rest_names, rest_res):
        res[nm] = tuple(o[None] for o in r) if nm in ("w_in", "w_out", "conv_w", "pool_w") else r

    return (loss[0, 0], gx.reshape(n_seq, seq, d), *[res[nm][0] for nm in names], *[res[nm][1] for nm in names],
            *[res[nm][2] for nm in names], *[res[nm][3] for nm in names])
```

```python
import jax
import jax.numpy as jnp
from jax import lax
from jax.experimental import pallas as pl
from jax.experimental.pallas import tpu as pltpu

F32, BF16 = jnp.float32, jnp.bfloat16
RMS_EPS = 1e-6
N_META = 16
CONV_WIDTH = 3
POOL_WINDOWS = (2, 4, 8, 16)
POOL_GROUP = 128
HALO = 16
N_DEV = 8
MESH_AXES = ("x", "y", "c")
MESH = pl.DeviceIdType.MESH
VMEM_LIMIT_BYTES = 56 * 1024 * 1024
ADAMW_BLOCK_ELEMS = 64 * 1024
TM_MIX = 512
TM_FFN = 256
FFN_CHUNK = 512
FFN_BACKWARD_LAG = 3
TM_WGRAD = 512
FF_CHUNKS = 2

ADAM_LR, ADAM_B1, ADAM_B2, ADAM_EPS, ADAM_WD, ADAM_STEP = 0.001, 0.9, 0.999, 1e-08, 0.01, 10


def _dot(a, b):
    return jnp.dot(a, b, preferred_element_type=F32)


def _dot_nt(a, b):
    return lax.dot_general(a, b, (((1,), (1,)), ((), ())), preferred_element_type=F32)


def _dot_tn(a, b):
    return lax.dot_general(a, b, (((0,), (0,)), ((), ())), preferred_element_type=F32)


def _rms_stats(h):
    rstd = lax.rsqrt(jnp.mean(h * h, axis=-1, keepdims=True) + RMS_EPS)
    return h * rstd, rstd


def _rms_bwd(hat, rstd, g, dy):
    gdy = dy * g
    proj = jnp.mean(gdy * hat, axis=-1, keepdims=True)
    return rstd * (gdy - hat * proj), jnp.sum(dy * hat, axis=0, keepdims=True)


def _params(*semantics):
    return pltpu.CompilerParams(dimension_semantics=semantics or None, vmem_limit_bytes=VMEM_LIMIT_BYTES)


def _resident(shape):
    zeros = (0,) * len(shape)
    return pl.BlockSpec(shape, lambda *_: zeros, pipeline_mode=pl.Buffered(1))


def _const(shape):
    zeros = (0,) * len(shape)
    return pl.BlockSpec(shape, lambda *_: zeros)


ANY = pl.BlockSpec(memory_space=pl.ANY)


def _my_place():
    x, y, c = (lax.axis_index(a) for a in MESH_AXES)
    return x, y, c


def _exchange_sems(n):
    return [pltpu.SemaphoreType.DMA((n, N_DEV - 1)), pltpu.SemaphoreType.DMA((n, N_DEV - 1)), pltpu.SemaphoreType.DMA((n,))]


def _gather_ops(srcs, outs, send_sems, recv_sems, local_sems, core_major=False):
    n = len(srcs)
    x, y, c = _my_place()
    me, sibling = (x, y, c), (x, y, 1 - c)
    chips = [(1 - x, y), (x, 1 - y), (1 - x, 1 - y)]

    def slab(px, py, pc):
        return 4 * pc + 2 * px + py if core_major else 4 * px + 2 * py + pc

    def copy(a, k, block, to, src=None):
        dst = outs[a].at[slab(*block)]
        return pltpu.make_async_remote_copy(
            src_ref=dst if src is None else src, dst_ref=dst, send_sem=send_sems.at[a, k], recv_sem=recv_sems.at[a, k],
            device_id=to, device_id_type=MESH)

    def mine(a):
        return pltpu.make_async_copy(srcs[a], outs[a].at[slab(*me)], local_sems.at[a])

    def first(a):
        return [copy(a, 0, me, sibling, src=srcs[a])] + [copy(a, 1 + j, me, (*chip, c), src=srcs[a]) for j, chip in enumerate(chips)]

    def passed(a, j):
        return copy(a, 4 + j, (*chips[j], c), sibling)

    def start():
        for a in range(n):
            mine(a).start()
            for cp in first(a):
                cp.start()

    def forward():
        for j, chip in enumerate(chips):
            for a in range(n):
                copy(a, 1 + j, (*chip, c), me).wait_recv()
                passed(a, j).start()

    def finish():
        for a in range(n):
            copy(a, 0, sibling, me).wait_recv()
            for j, chip in enumerate(chips):
                copy(a, 4 + j, (*chip, 1 - c), me).wait_recv()
        for a in range(n):
            for cp in first(a) + [passed(a, j) for j in range(len(chips))]:
                cp.wait_send()
            mine(a).wait()

    return start, forward, finish


def _exchange_ops(ins, outs, whole, send_sems, recv_sems, local_sems):
    n = len(ins)
    x, y, c = _my_place()
    me = 4 * x + 2 * y + c

    def src(a, i):
        return ins[a] if whole[a] else ins[a].at[i]

    def mine(a):
        return pltpu.make_async_copy(src(a, me), outs[a].at[me], local_sems.at[a])

    def send(a, k):
        to = (me + k) % N_DEV
        return pltpu.make_async_remote_copy(
            src_ref=src(a, to), dst_ref=outs[a].at[me], send_sem=send_sems.at[a, k - 1], recv_sem=recv_sems.at[a, k - 1],
            device_id=(to // 4, (to // 2) % 2, to % 2), device_id_type=MESH)

    def landed(a, k):
        frm = (me + N_DEV - k) % N_DEV
        return pltpu.make_async_remote_copy(
            src_ref=src(a, frm), dst_ref=outs[a].at[frm], send_sem=send_sems.at[a, k - 1], recv_sem=recv_sems.at[a, k - 1],
            device_id=(x, y, c), device_id_type=MESH)

    def start():
        for a in range(n):
            mine(a).start()
            for k in range(1, N_DEV):
                send(a, k).start()

    def finish():
        for a in range(n):
            for k in range(1, N_DEV):
                landed(a, k).wait_recv()
        for a in range(n):
            for k in range(1, N_DEV):
                send(a, k).wait_send()
            mine(a).wait()

    return start, finish


def _core_exchange_sems(n):
    return [pltpu.SemaphoreType.DMA((n, 4)), pltpu.SemaphoreType.DMA((n, N_DEV)), pltpu.SemaphoreType.DMA((n,))]


def _core_exchange_ops(ins, outs, to_core, send_sems, recv_sems, local_sems):
    n = len(ins)
    x, y, c = _my_place()
    me = 4 * x + 2 * y + c
    others = [(0, 1), (1, 0), (1, 1)]

    def slab(a, p):
        if len(ins[a].shape) == len(outs[a].shape):
            return ins[a].at[p]
        rows = outs[a].shape[1]
        return ins[a].at[pl.ds(pl.multiple_of(p * rows, 16), rows), :]

    def send(a, dx, dy):
        tx, ty = (x + dx) % 2, (y + dy) % 2
        return pltpu.make_async_remote_copy(
            src_ref=slab(a, 4 * to_core + 2 * tx + ty), dst_ref=outs[a].at[me], send_sem=send_sems.at[a, 2 * dx + dy],
            recv_sem=recv_sems.at[a, 2 * (2 * dx + dy) + c], device_id=(tx, ty, to_core), device_id_type=MESH)

    def mine(a):
        return pltpu.make_async_copy(slab(a, 4 * to_core + 2 * x + y), outs[a].at[me], local_sems.at[a])

    def landed(a, dx, dy, sc):
        frm = 4 * ((x + dx) % 2) + 2 * ((y + dy) % 2) + sc
        return pltpu.make_async_remote_copy(
            src_ref=slab(a, 0), dst_ref=outs[a].at[frm], send_sem=send_sems.at[a, 0], recv_sem=recv_sems.at[a, 2 * (2 * dx + dy) + sc],
            device_id=(x, y, c), device_id_type=MESH)

    def start():
        for a in range(n):
            for dx, dy in others:
                send(a, dx, dy).start()
            pl.when(c == to_core)(mine(a).start)
            pl.when(c != to_core)(send(a, 0, 0).start)

    def finish():
        @pl.when(c == to_core)
        def _():
            for a in range(n):
                for dx, dy in [(0, 0)] + others:
                    for sc in (0, 1):
                        if (dx, dy, sc) != (0, 0, to_core):
                            landed(a, dx, dy, sc).wait_recv()
            for a in range(n):
                mine(a).wait()

        @pl.when(c != to_core)
        def _():
            for a in range(n):
                send(a, 0, 0).wait_send()

        for a in range(n):
            for dx, dy in others:
                send(a, dx, dy).wait_send()

    return start, finish


N_CHIP = 4


def _pair_then_chip_sems(n):
    return [pltpu.SemaphoreType.DMA((n, N_CHIP)) for _ in range(6)] + [pltpu.SemaphoreType.DMA((n,))]


def _pair_then_chip_ops(ins, pairs, outs, mine_v, pair_v, sum_v, pair_send, pair_recv, chip_send, chip_recv, load_a, load_b, own_sem):
    n = len(ins)
    x, y, c = _my_place()
    chip = 2 * x + y
    chips = [(0, 0), (0, 1), (1, 0), (1, 1)]
    others = [(0, 1), (1, 0), (1, 1)]

    def to_sibling(a, j):
        px, py = chips[j]
        return pltpu.make_async_remote_copy(
            src_ref=ins[a].at[4 * px + 2 * py + 1 - c], dst_ref=pairs[a].at[j], send_sem=pair_send.at[a, j], recv_sem=pair_recv.at[a, j],
            device_id=(x, y, 1 - c), device_id_type=MESH)

    def spread(a, dx, dy):
        tx, ty = (x + dx) % 2, (y + dy) % 2
        return pltpu.make_async_remote_copy(
            src_ref=sum_v[a].at[2 * tx + ty], dst_ref=outs[a].at[chip], send_sem=chip_send.at[a, 2 * dx + dy],
            recv_sem=chip_recv.at[a, 2 * dx + dy], device_id=(tx, ty, c), device_id_type=MESH)

    def landed(a, dx, dy):
        frm = 2 * ((x + dx) % 2) + (y + dy) % 2
        return pltpu.make_async_remote_copy(
            src_ref=sum_v[a].at[0], dst_ref=outs[a].at[frm], send_sem=chip_send.at[a, 0], recv_sem=chip_recv.at[a, 2 * dx + dy],
            device_id=(x, y, c), device_id_type=MESH)

    def own(a):
        return pltpu.make_async_copy(sum_v[a].at[chip], outs[a].at[chip], own_sem.at[a])

    def pair():
        loads = []
        for a in range(n):
            for j, (px, py) in enumerate(chips):
                to_sibling(a, j).start()
                loads.append(pltpu.make_async_copy(ins[a].at[4 * px + 2 * py + c], mine_v[a].at[j], load_a.at[a, j]))
                loads[-1].start()
        for a in range(n):
            for j in range(N_CHIP):
                to_sibling(a, j).wait_recv()
                loads.append(pltpu.make_async_copy(pairs[a].at[j], pair_v[a].at[j], load_b.at[a, j]))
                loads[-1].start()
        for cp in loads:
            cp.wait()
        for a in range(n):
            sum_v[a][...] = (mine_v[a][...].astype(F32) + pair_v[a][...].astype(F32)).astype(sum_v[a].dtype)

    def start():
        pair()
        for a in range(n):
            own(a).start()
            for dx, dy in others:
                spread(a, dx, dy).start()

    def finish():
        for a in range(n):
            for dx, dy in others:
                landed(a, dx, dy).wait_recv()
        for a in range(n):
            for dx, dy in others:
                spread(a, dx, dy).wait_send()
            for j in range(N_CHIP):
                to_sibling(a, j).wait_send()
            own(a).wait()

    return start, finish


def _window_sum(x, win, ahead):
    n = x.shape[0]
    span = 1
    while span < win:
        x = x + pltpu.roll(x, n - span if ahead else span, 0)
        span *= 2
    return x


def _conv_branch(z, ext_u, conv_ref, tm):
    c_w = z.shape[1] // 4
    b, c, v = z[:, :c_w], z[:, c_w:2 * c_w], z[:, 2 * c_w:3 * c_w]
    u = c * v
    ext_u[pl.ds(HALO, tm), :] = u
    u1 = ext_u[pl.ds(HALO - 1, tm), :]
    u2 = ext_u[pl.ds(HALO - 2, tm), :]
    yc = conv_ref[pl.ds(2, 1), :] * u + conv_ref[pl.ds(1, 1), :] * u1 + conv_ref[pl.ds(0, 1), :] * u2
    return b, c, v, u, u1, u2, yc


def _pool_branch(p, ext_p, pool_w_ref, tm):
    ext_p[pl.ds(HALO, tm), :] = p
    pooled, mixed = [], []
    for g, win in enumerate(POOL_WINDOWS):
        s = _window_sum(ext_p[:, pl.ds(POOL_GROUP * g, POOL_GROUP)], win, ahead=False)[HALO:HALO + tm, :]
        pooled.append((s * (1.0 / win) - p[:, POOL_GROUP * g:POOL_GROUP * (g + 1)]).astype(BF16))
        mixed.append(_dot(pooled[-1], pool_w_ref[g].astype(BF16)))
    return pooled, mixed


def _gather_and_mixer_forward(x2d, mixer_shards, ffn_shards, g1, pool_w, pool_scale, g2, n_seq):
    t, d = x2d.shape
    zs, rs, ms, cs = mixer_shards[0].shape[1], mixer_shards[1].shape[0], mixer_shards[2].shape[1], mixer_shards[3].shape[1]
    zw, cw = N_DEV * zs, N_DEV * cs
    s = t // n_seq
    tm = min(TM_MIX, s)
    nj = s // tm
    n1, n2 = len(mixer_shards), len(ffn_shards)
    dtypes = [BF16, BF16, F32, F32] + [BF16] * n2
    shards = list(mixer_shards) + list(ffn_shards)

    def body(x_ref, *rest):
        shard_refs, (g1_ref, pw_ref, ps_ref, g2_ref), rest = rest[:n1 + n2], rest[n1 + n2:n1 + n2 + 4], rest[n1 + n2 + 4:]
        (h1_ref, z_ref, m_ref, pooled_ref, mixed_ref, win_o, wout_o, meta_o, conv_o, am_o, zm_o), rest = rest[:11], rest[11:]
        slabs, rest = rest[:n1 + n2], rest[n1 + n2:]
        stages, rest = rest[:n1 + n2], rest[n1 + n2:]
        win_v, wout_v, meta_v, conv_v, ext_u, ext_p, sem = rest[:7]
        first = _gather_ops(stages[:n1], slabs[:n1], *rest[7:10])
        later = _gather_ops(stages[n1:], slabs[n1:], *rest[10:13], core_major=True)

        @pl.when((pl.program_id(0) == 0) & (pl.program_id(1) == 0))
        def _():
            for src, dst in zip(shard_refs, stages):
                dst[...] = src[...].astype(dst.dtype)
            first[0]()
            later[0]()
            first[1]()
            first[2]()
            copies = [pltpu.make_async_copy(slabs[0].at[i], win_v.at[:, pl.ds(zs * i, zs)], sem.at[i]) for i in range(N_DEV)]
            copies += [pltpu.make_async_copy(slabs[1].at[i], wout_v.at[pl.ds(rs * i, rs), :], sem.at[N_DEV + i]) for i in range(N_DEV)]
            copies += [pltpu.make_async_copy(slabs[2], meta_v, sem.at[2 * N_DEV]), pltpu.make_async_copy(slabs[3], conv_v, sem.at[2 * N_DEV + 1])]
            for cp in copies:
                cp.start()
            for cp in copies:
                cp.wait()
            copies = [pltpu.make_async_copy(win_v, win_o, sem.at[0]), pltpu.make_async_copy(wout_v, wout_o, sem.at[1])]
            for cp in copies:
                cp.start()
            for i in range(N_DEV):
                meta_o[:, pl.ds(ms * i, ms)] = meta_v[i]
                conv_o[:, pl.ds(cs * i, cs)] = conv_v[i]
            hat, _ = _rms_stats(meta_o[...])
            a = (hat * g1_ref[...]).astype(BF16)
            am_o[...] = a
            zm_o[...] = _dot(a, win_v[...])
            for cp in copies:
                cp.wait()

        @pl.when(pl.program_id(1) == 0)
        def _():
            zm = zm_o[...]
            ext_u[pl.ds(0, HALO), :] = zm[:, cw:2 * cw] * zm[:, 2 * cw:3 * cw]
            ext_p[pl.ds(0, HALO), :] = zm[:, 3 * cw:]

        h0 = x_ref[...]
        hat, _ = _rms_stats(h0)
        z = _dot((hat * g1_ref[...]).astype(BF16), win_v[...])
        z_ref[...] = z.astype(BF16)
        b, _, _, _, _, _, yc = _conv_branch(z, ext_u, conv_o, tm)
        pooled, mixed = _pool_branch(z[:, 3 * cw:], ext_p, pw_ref, tm)
        pooled_ref[...] = jnp.concatenate(pooled, axis=1)
        mixed_ref[...] = jnp.concatenate(mixed, axis=1).astype(BF16)
        ps = ps_ref[...]
        y = [b * yc] + [mixed[g] * ps[:, POOL_GROUP * g:POOL_GROUP * (g + 1)] for g in range(len(POOL_WINDOWS))]
        m = _dot(jnp.concatenate(y, axis=1).astype(BF16), wout_v[...])
        m_ref[...] = m
        m_hat, _ = _rms_stats(m)
        h1_ref[...] = h0 + m_hat * g2_ref[...]
        ext_u[pl.ds(0, HALO), :] = ext_u[pl.ds(tm, HALO), :]
        ext_p[pl.ds(0, HALO), :] = ext_p[pl.ds(tm, HALO), :]

        @pl.when((pl.program_id(0) == n_seq - 1) & (pl.program_id(1) == nj - 1))
        def _():
            later[1]()
            later[2]()

    row = lambda b, j: (b * nj + j, 0)
    vmem = pl.BlockSpec(memory_space=pltpu.VMEM)
    small = [(N_META, d), (CONV_WIDTH, cw), (N_META, d), (N_META, zw)]
    out = pl.pallas_call(
        body, name="gather_and_mixer_forward", grid=(n_seq, nj),
        in_specs=[pl.BlockSpec((tm, d), row)] + [vmem] * (n1 + n2)
        + [_const(g1.shape), _const(pool_w.shape), _const(pool_scale.shape), _const(g2.shape)],
        out_specs=[pl.BlockSpec((tm, d), row), pl.BlockSpec((tm, zw), row), pl.BlockSpec((tm, d), row), pl.BlockSpec((tm, cw), row),
                   pl.BlockSpec((tm, cw), row), ANY, ANY] + [_const(sh) for sh in small] + [ANY] * (n1 + n2),
        out_shape=[jax.ShapeDtypeStruct((t, d), F32), jax.ShapeDtypeStruct((t, zw), BF16), jax.ShapeDtypeStruct((t, d), F32),
                   jax.ShapeDtypeStruct((t, cw), BF16), jax.ShapeDtypeStruct((t, cw), BF16),
                   jax.ShapeDtypeStruct((d, zw), BF16), jax.ShapeDtypeStruct((d, d), BF16),
                   jax.ShapeDtypeStruct(small[0], F32), jax.ShapeDtypeStruct(small[1], F32), jax.ShapeDtypeStruct(small[2], BF16),
                   jax.ShapeDtypeStruct(small[3], F32)]
        + [jax.ShapeDtypeStruct((N_DEV, *a.shape), dt) for a, dt in zip(shards, dtypes)],
        scratch_shapes=[pltpu.VMEM(a.shape, dt) for a, dt in zip(shards, dtypes)]
        + [pltpu.VMEM((d, zw), BF16), pltpu.VMEM((d, d), BF16), pltpu.VMEM((N_DEV, N_META, ms), F32),
           pltpu.VMEM((N_DEV, CONV_WIDTH, cs), F32), pltpu.VMEM((tm + HALO, cw), F32), pltpu.VMEM((tm + HALO, cw), F32),
           pltpu.SemaphoreType.DMA((2 * N_DEV + 2,))] + _exchange_sems(n1) + _exchange_sems(n2),
        compiler_params=_params("arbitrary", "arbitrary"),
    )(x2d, *shards, g1, pool_w, pool_scale, g2)
    return out[:5], out[5:11], out[11 + n1:]


def _mixer_backward(x2d, dh1, m, z, pooled, mixed, meta, a_meta, z_meta, g1, w_in, conv_w, pool_w, pool_scale, w_out, g2, n_seq,
                    to_exchange, landing):
    t, d = x2d.shape
    zw = w_in.shape[1]
    cw = zw // 4
    s = t // n_seq
    tm = min(TM_MIX, s)
    nj = s // tm
    n_groups = len(POOL_WINDOWS)
    zs = zw // N_DEV
    nx = len(to_exchange)
    n_in = 17
    given = [k for k, a in enumerate(landing) if a is not None]
    fresh = [k for k, a in enumerate(landing) if a is None]

    def body(x_ref, dh1_ref, m_ref, z_ref, zprev_ref, pooled_ref, mixed_ref, meta_ref, am_ref, zm_ref, g1_ref, win_ref, conv_ref, pw_ref, ps_ref, wout_ref,
             g2_ref, *rest):
        sent, rest = rest[:nx], rest[nx + len(given):]
        gx_ref, dwin_ref, dwout_ref, dg1_ref, dg2_ref, dconv_ref, dpw_ref, dps_ref, dmeta_ref = rest[:9]
        landed, rest = rest[9:9 + nx], rest[9 + nx:]
        ext_u, ext_dyc, ext_dq, acc_win, acc_wout, dz_meta, stage16, sem = rest[:8]
        north = _core_exchange_ops(sent, landed, 1, *rest[8:11])
        south = _core_exchange_ops([sent[k] for k in fresh], [landed[k] for k in fresh], 0, *rest[11:14])

        def start():
            north[0]()
            south[0]()

        def finish():
            south[1]()
            north[1]()

        b_id, j = pl.program_id(0), pl.program_id(1)
        jr = nj - 1 - j
        pl.when((b_id == 0) & (j == 0))(start)

        @pl.when((b_id == 0) & (j == 0))
        def _():
            acc_win[...] = jnp.zeros_like(acc_win)
            acc_wout[...] = jnp.zeros_like(acc_wout)
            dz_meta[...] = jnp.zeros_like(dz_meta)
            for r in (dg1_ref, dg2_ref, dconv_ref, dpw_ref, dps_ref, dmeta_ref):
                r[...] = jnp.zeros_like(r)

        @pl.when(j == 0)
        def _():
            ext_dyc[pl.ds(tm, HALO), :] = jnp.zeros((HALO, cw), F32)
            ext_dq[pl.ds(tm, HALO), :] = jnp.zeros((HALO, cw), F32)

        zm = zm_ref[...]
        halo = jnp.where(jr == 0, zm, zprev_ref[...].astype(F32))
        ext_u[pl.ds(0, HALO), :] = halo[:, cw:2 * cw] * halo[:, 2 * cw:3 * cw]

        dh1v = dh1_ref[...]
        m_hat, m_rstd = _rms_stats(m_ref[...])
        dm, dg2 = _rms_bwd(m_hat, m_rstd, g2_ref[...], dh1v)
        dg2_ref[...] += dg2
        dm = dm.astype(BF16)
        dycat = _dot_nt(dm, wout_ref[...])

        b, c, v, u, u1, u2, yc = _conv_branch(z_ref[...].astype(F32), ext_u, conv_ref, tm)
        mixed = [mixed_ref[:, pl.ds(POOL_GROUP * g, POOL_GROUP)].astype(F32) for g in range(n_groups)]
        ps = ps_ref[...]
        y = [b * yc] + [mixed[g] * ps[:, POOL_GROUP * g:POOL_GROUP * (g + 1)] for g in range(n_groups)]
        ycat = jnp.concatenate(y, axis=1).astype(BF16)
        acc_wout[...] += _dot_tn(ycat, dm)

        dyconv = dycat[:, :cw]
        db = dyconv * yc
        dyc = dyconv * b
        ext_dyc[pl.ds(0, tm), :] = dyc
        du = (conv_ref[pl.ds(2, 1), :] * dyc + conv_ref[pl.ds(1, 1), :] * ext_dyc[pl.ds(1, tm), :]
              + conv_ref[pl.ds(0, 1), :] * ext_dyc[pl.ds(2, tm), :])
        dconv_ref[pl.ds(2, 1), :] += jnp.sum(dyc * u, axis=0, keepdims=True)
        dconv_ref[pl.ds(1, 1), :] += jnp.sum(dyc * u1, axis=0, keepdims=True)
        dconv_ref[pl.ds(0, 1), :] += jnp.sum(dyc * u2, axis=0, keepdims=True)

        dp = []
        for g, win in enumerate(POOL_WINDOWS):
            lanes = pl.ds(POOL_GROUP * g, POOL_GROUP)
            dypool = dycat[:, cw + POOL_GROUP * g:cw + POOL_GROUP * (g + 1)]
            dps_ref[:, lanes] += jnp.sum(dypool * mixed[g], axis=0, keepdims=True)
            dmixed = (dypool * ps[:, POOL_GROUP * g:POOL_GROUP * (g + 1)]).astype(BF16)
            dq = _dot_nt(dmixed, pw_ref[g].astype(BF16))
            dpw_ref[g] += _dot_tn(pooled_ref[:, lanes], dmixed)
            ext_dq[pl.ds(0, tm), lanes] = dq
            acc = _window_sum(ext_dq[:, lanes], win, ahead=True)[0:tm, :]
            dp.append(acc * (1.0 / win) - dq)

        dz = jnp.concatenate([db, du * v, du * c] + dp, axis=1).astype(BF16)
        da = _dot_nt(dz, win_ref[...])
        h0 = x_ref[...]
        hat0, rstd0 = _rms_stats(h0)
        g1 = g1_ref[...]
        acc_win[...] += _dot_tn((hat0 * g1).astype(BF16), dz)
        dh0, dg1 = _rms_bwd(hat0, rstd0, g1, da)
        dg1_ref[...] += dg1
        gx_ref[...] = dh1v + dh0

        ext_dyc[pl.ds(tm, HALO), :] = ext_dyc[pl.ds(0, HALO), :]
        ext_dq[pl.ds(tm, HALO), :] = ext_dq[pl.ds(0, HALO), :]

        @pl.when(jr == 0)
        def _():
            ext_dyc[pl.ds(tm - HALO, HALO), :] = jnp.zeros((HALO, cw), F32)
            ext_dq[pl.ds(tm - HALO, HALO), :] = jnp.zeros((HALO, cw), F32)
            du_m = (conv_ref[pl.ds(1, 1), :] * ext_dyc[pl.ds(tm - HALO + 1, HALO), :]
                    + conv_ref[pl.ds(0, 1), :] * ext_dyc[pl.ds(tm - HALO + 2, HALO), :])
            dp_m = []
            for g, win in enumerate(POOL_WINDOWS):
                lanes = pl.ds(POOL_GROUP * g, POOL_GROUP)
                acc = ext_dq[pl.ds(tm - HALO + 1, HALO), lanes]
                for k in range(2, win):
                    acc = acc + ext_dq[pl.ds(tm - HALO + k, HALO), lanes]
                dp_m.append(acc * (1.0 / win))
            dz_meta[...] += jnp.concatenate(
                [jnp.zeros((HALO, cw), F32), du_m * zm[:, 2 * cw:3 * cw], du_m * zm[:, cw:2 * cw]] + dp_m, axis=1)

        @pl.when((b_id == n_seq - 1) & (j == nj - 1))
        def _():
            dz_m = dz_meta[...].astype(BF16)
            acc_win[...] += _dot_tn(am_ref[...], dz_m)
            hat_m, rstd_m = _rms_stats(meta_ref[...])
            dmeta, dg1_m = _rms_bwd(hat_m, rstd_m, g1, _dot_nt(dz_m, win_ref[...]))
            dg1_ref[...] += dg1_m
            dmeta_ref[...] = dmeta
            pieces = [(acc_win, zs * i, dwin_ref.at[i]) for i in range(N_DEV)]
            pieces += [(acc_wout, zs * i, dwout_ref.at[:, pl.ds(zs * i, zs)]) for i in range(d // zs)]
            copies = []
            for k, (acc, col, dst) in enumerate(pieces):
                if k >= 2:
                    copies[k - 2].wait()
                stage16[k % 2] = acc[:, pl.ds(col, zs)].astype(BF16)
                copies.append(pltpu.make_async_copy(stage16.at[k % 2], dst, sem.at[k % 2]))
                copies[k].start()
            copies[-2].wait()
            copies[-1].wait()
            finish()

    row = lambda b, j: (b * nj + nj - 1 - j, 0)
    prev = lambda b, j: (jnp.maximum((b * s + (nj - 1 - j) * tm) // HALO - 1, 0), 0)
    small = [g1.shape, g2.shape, conv_w.shape, pool_w.shape, pool_scale.shape, meta.shape]
    out = pl.pallas_call(
        body, name="mixer_backward", grid=(n_seq, nj),
        in_specs=[pl.BlockSpec((tm, d), row), pl.BlockSpec((tm, d), row), pl.BlockSpec((tm, d), row), pl.BlockSpec((tm, zw), row),
                  pl.BlockSpec((HALO, zw), prev), pl.BlockSpec((tm, cw), row), pl.BlockSpec((tm, cw), row), _const(meta.shape), _const(a_meta.shape), _const(z_meta.shape), _const(g1.shape),
                  _resident(w_in.shape), _const(conv_w.shape), _const(pool_w.shape), _const(pool_scale.shape), _resident(w_out.shape),
                  _const(g2.shape)] + [ANY] * (nx + len(given)),
        out_specs=[pl.BlockSpec((tm, d), row), ANY, ANY] + [_const(sh) for sh in small] + [ANY] * nx,
        out_shape=[jax.ShapeDtypeStruct((t, d), F32), jax.ShapeDtypeStruct((N_DEV, d, zs), BF16),
                   jax.ShapeDtypeStruct(w_out.shape, BF16)] + [jax.ShapeDtypeStruct(sh, F32) for sh in small]
        + [jax.ShapeDtypeStruct((N_DEV, a.shape[0] // N_DEV, a.shape[1]), a.dtype) for a in to_exchange],
        input_output_aliases={n_in + nx + at: 9 + k for at, k in enumerate(given)},
        scratch_shapes=[pltpu.VMEM((tm + HALO, cw), F32)] * 3
        + [pltpu.VMEM(w_in.shape, F32), pltpu.VMEM(w_out.shape, F32), pltpu.VMEM((HALO, zw), F32), pltpu.VMEM((2, d, zs), BF16),
           pltpu.SemaphoreType.DMA((2,))] + _core_exchange_sems(nx) + _core_exchange_sems(len(fresh)),
        compiler_params=_params("arbitrary", "arbitrary"),
    )(x2d, dh1, m, z, z, pooled, mixed, meta, a_meta, z_meta, g1, w_in, conv_w, pool_w, pool_scale, w_out, g2, *to_exchange, *[landing[k] for k in given])
    return out[:9], out[9:]


def _ffn_forward_backward(h1, target, g3, w_gate, w_up, w_down, g4):
    t, d = h1.shape
    ff = w_gate.shape[0]
    tm = min(TM_FFN, t)
    nt = t // tm
    chunks = [(s, min(FFN_CHUNK, ff - s)) for s in range(0, ff, FFN_CHUNK)]

    def body(h1_ref, h1pp_ref, tgt_ref, g3_ref, wg_hbm, wu_hbm, wd_ref, g4_ref,
             f_ref, act_ref, dd_ref, dgate_ref, dup_ref, dh1_ref, loss_ref, dg3_ref, dg4_ref, *slots):
        gate_s, up_s, dd_s, dh2_s, df_s, wgu, wsem = slots
        i = pl.program_id(0)

        def gu(s, n):
            return wgu.at[pl.ds(2 * s, 2 * n), :]

        @pl.when(i == 0)
        def _():
            copies = []
            for k, (s, n) in enumerate(chunks):
                copies += [pltpu.make_async_copy(wg_hbm.at[pl.ds(s, n), :], wgu.at[pl.ds(2 * s, n), :], wsem.at[2 * k]),
                           pltpu.make_async_copy(wu_hbm.at[pl.ds(s, n), :], wgu.at[pl.ds(2 * s + n, n), :], wsem.at[2 * k + 1])]
            for k, cp in enumerate(copies):
                cp.start(priority=k % 2)
            for r in (loss_ref, dg3_ref, dg4_ref, gate_s, up_s, dd_s, dh2_s, df_s):
                r[...] = jnp.zeros_like(r)
            for cp in copies:
                cp.wait()

        def forward(slot):
            h1v = h1_ref[...]
            hat, _ = _rms_stats(h1v)
            f = (hat * g3_ref[...]).astype(BF16)
            f_ref[...] = f
            s, n = chunks[0]
            both = _dot_nt(f_ref[...], gu(s, n)[...])
            yield
            down = None
            for k, (s, n) in enumerate(chunks):
                gate, up = both[:, :n], both[:, n:]
                gate_s.at[slot][:, pl.ds(s, n)] = gate.astype(BF16)
                up_s.at[slot][:, pl.ds(s, n)] = up.astype(BF16)
                act = (gate * jax.nn.sigmoid(gate) * up).astype(BF16)
                act_ref[:, pl.ds(s, n)] = act
                if k + 1 < len(chunks):
                    s1, n1 = chunks[k + 1]
                    both = _dot_nt(f_ref[...], gu(s1, n1)[...])
                yield
                part = _dot(act_ref[:, pl.ds(s, n)], wd_ref[pl.ds(s, n), :])
                down = part if down is None else down + part
                yield
            d_hat, d_rstd = _rms_stats(down)
            g4 = g4_ref[...]
            err = h1v + d_hat * g4 - tgt_ref[...]
            loss_ref[...] += jnp.sum(err * err) * (0.5 / d)
            dh2 = err * (1.0 / d)
            dh2_s.at[slot][...] = dh2
            dd, dg4 = _rms_bwd(d_hat, d_rstd, g4, dh2)
            dg4_ref[...] += dg4
            dd = dd.astype(BF16)
            dd_ref[...] = dd
            dd_s.at[slot][...] = dd

        def backward(slot):
            s, n = chunks[0]
            dact = _dot_nt(dd_s.at[slot][...], wd_ref[pl.ds(s, n), :])
            yield
            df = None
            for k, (s, n) in enumerate(chunks):
                gate = gate_s.at[slot][:, pl.ds(s, n)].astype(F32)
                up = up_s.at[slot][:, pl.ds(s, n)].astype(F32)
                sig = jax.nn.sigmoid(gate)
                dup = (dact * (gate * sig)).astype(BF16)
                dgate = (dact * up * (sig * (1.0 + gate * (1.0 - sig)))).astype(BF16)
                dup_ref[:, pl.ds(s, n)] = dup
                dgate_ref[:, pl.ds(s, n)] = dgate
                if k + 1 < len(chunks):
                    s1, n1 = chunks[k + 1]
                    dact = _dot_nt(dd_s.at[slot][...], wd_ref[pl.ds(s1, n1), :])
                yield
                part = _dot(jnp.concatenate([dgate_ref[:, pl.ds(s, n)], dup_ref[:, pl.ds(s, n)]], axis=1), gu(s, n)[...])
                df = part if df is None else df + part
                yield
            df_s.at[slot][...] = df

        def last(slot):
            hat, rstd = _rms_stats(h1pp_ref[...])
            dh1, dg3 = _rms_bwd(hat, rstd, g3_ref[...], df_s.at[slot][...])
            dg3_ref[...] += dg3
            dh1_ref[...] = dh2_s.at[slot][...] + dh1

        def emit(parity, with_forward, with_backward, with_last):
            fwd = forward(parity) if with_forward else iter(())
            bwd = backward(1 - parity) if with_backward else iter(())
            next(fwd, None)
            if with_last:
                last(parity)
            for _ in range(FFN_BACKWARD_LAG):
                next(fwd, None)
            alive = True
            while alive:
                alive = next(bwd, True) is None
                alive = (next(fwd, True) is None) or alive

        @pl.when(i < nt)
        def _():
            emit(i % 2, True, True, True)

        @pl.when(i == nt)
        def _():
            emit(nt % 2, False, True, True)

        @pl.when(i == nt + 1)
        def _():
            emit((nt + 1) % 2, False, False, True)

    cur = lambda i: (jnp.minimum(i, nt - 1), 0)
    prev = lambda i: (jnp.clip(i - 1, 0, nt - 1), 0)
    prev2 = lambda i: (jnp.clip(i - 2, 0, nt - 1), 0)
    return pl.pallas_call(
        body, name="ffn_forward_backward", grid=(nt + 2,),
        in_specs=[pl.BlockSpec((tm, d), cur), pl.BlockSpec((tm, d), prev2), pl.BlockSpec((tm, d), cur), _const(g3.shape),
                  ANY, ANY, _resident(w_down.shape), _const(g4.shape)],
        out_specs=[pl.BlockSpec((tm, d), cur), pl.BlockSpec((tm, ff), cur), pl.BlockSpec((tm, d), cur), pl.BlockSpec((tm, ff), prev),
                   pl.BlockSpec((tm, ff), prev), pl.BlockSpec((tm, d), prev2), _const((8, 128)), _const(g3.shape), _const(g4.shape)],
        out_shape=[jax.ShapeDtypeStruct((t, d), BF16), jax.ShapeDtypeStruct((t, ff), BF16), jax.ShapeDtypeStruct((t, d), BF16),
                   jax.ShapeDtypeStruct((t, ff), BF16), jax.ShapeDtypeStruct((t, ff), BF16), jax.ShapeDtypeStruct((t, d), F32),
                   jax.ShapeDtypeStruct((8, 128), F32), jax.ShapeDtypeStruct(g3.shape, F32), jax.ShapeDtypeStruct(g4.shape, F32)],
        scratch_shapes=[pltpu.VMEM((2, tm, ff), BF16)] * 2 + [pltpu.VMEM((2, tm, d), BF16)] + [pltpu.VMEM((2, tm, d), F32)] * 2
        + [pltpu.VMEM((2 * ff, d), BF16), pltpu.SemaphoreType.DMA((2 * len(chunks),))],
        compiler_params=_params("arbitrary"),
    )(h1, h1, target, g3, w_gate, w_up, w_down, g4)


def _ffn_weight_grads(f, dd, dgate, dup, act):
    t, d = f.shape
    ff = dgate.shape[1]
    tm = min(TM_WGRAD, t)
    nt = t // tm
    fc = ff // FF_CHUNKS
    assert FF_CHUNKS == 2

    def body(f_ref, dd_ref, dgate_ref, dup_ref, act_ref, dwg_ref, dwu_ref, dwd_ref, *rest):
        landing, (acc_g, acc_u, acc_d, stage, sem) = rest[:2], rest[2:7]
        start, finish = _core_exchange_ops([dwg_ref, dwd_ref], landing, 0, *rest[7:])
        c, i = pl.program_id(0), pl.program_id(1)
        pl.when((c == 1) & (i == 0))(start)

        @pl.when(i == 0)
        def _():
            acc_g[...] = jnp.zeros_like(acc_g)
            acc_u[...] = jnp.zeros_like(acc_u)
            acc_d[...] = jnp.zeros_like(acc_d)

        fv = f_ref[...]
        acc_g[...] += _dot_tn(fv, dgate_ref[...])
        acc_u[...] += _dot_tn(fv, dup_ref[...])
        acc_d[...] += _dot_tn(act_ref[...], dd_ref[...])

        @pl.when(i == nt - 1)
        def _():
            rows = pl.ds(pl.multiple_of(c * fc, 16), fc)
            copies = []
            for k, (acc, out, transposed) in enumerate(((acc_d, dwd_ref, False), (acc_g, dwg_ref, True), (acc_u, dwu_ref, True))):
                if k >= 2:
                    copies[k - 2].wait()
                stage[k % 2] = (acc[...].T if transposed else acc[...]).astype(BF16)
                copies.append(pltpu.make_async_copy(stage.at[k % 2], out.at[rows, :], sem.at[k % 2]))
                copies[k].start()
            copies[-2].wait()
            copies[-1].wait()

        pl.when((c == 1) & (i == nt - 1))(finish)

    row = lambda c, i: (i, 0)
    col = lambda c, i: (i, c)
    out = pl.pallas_call(
        body, name="ffn_weight_grads", grid=(FF_CHUNKS, nt),
        in_specs=[pl.BlockSpec((tm, d), row), pl.BlockSpec((tm, d), row), pl.BlockSpec((tm, fc), col), pl.BlockSpec((tm, fc), col),
                  pl.BlockSpec((tm, fc), col)],
        out_specs=[ANY] * 5,
        out_shape=[jax.ShapeDtypeStruct((ff, d), BF16)] * 3 + [jax.ShapeDtypeStruct((N_DEV, ff // N_DEV, d), BF16)] * 2,
        scratch_shapes=[pltpu.VMEM((d, fc), F32), pltpu.VMEM((d, fc), F32), pltpu.VMEM((fc, d), F32), pltpu.VMEM((2, fc, d), BF16),
                        pltpu.SemaphoreType.DMA((2,))] + _core_exchange_sems(2),
        compiler_params=_params("arbitrary", "arbitrary"),
    )(f, dd, dgate, dup, act)
    return out[:3], [out[3], None, out[4]]


def _adamw(w, g, m, v):
    m = ADAM_B1 * m + (1.0 - ADAM_B1) * g
    v = ADAM_B2 * v + (1.0 - ADAM_B2) * (g * g)
    m_hat = m / (1.0 - ADAM_B1 ** ADAM_STEP)
    v_hat = v / (1.0 - ADAM_B2 ** ADAM_STEP)
    return -ADAM_LR * (m_hat / (jnp.sqrt(v_hat) + ADAM_EPS) + ADAM_WD * w), m, v


def _sum_slabs(ref):
    total = ref[0].astype(F32)
    for i in range(1, ref.shape[0]):
        total = total + ref[i].astype(F32)
    return total


def _adamw_rows(r, c):
    tr = r
    for cand in range(8, r, 8):
        if r % cand == 0 and cand * c <= ADAMW_BLOCK_ELEMS:
            tr = cand
    return r if r * c <= ADAMW_BLOCK_ELEMS else tr


def _reduce_adamw_carrying(parts, ws, ms, vs, to_reduce, to_exchange, whole, name):
    k, nr, nx = len(ws), len(to_reduce), len(to_exchange)
    r, c = ws[0].shape if k else (8, 128)
    tr = _adamw_rows(r, c)
    steps = r // tr
    travels = nr + nx > 0
    nd = list(whole).count(False)
    assert list(whole) == [False] * nd + [True] * (nx - nd)
    chip_slabs = [jax.ShapeDtypeStruct((N_CHIP, *a.shape[1:]), a.dtype) for a in to_reduce]

    def body(*refs):
        p_refs, w_refs, m_refs, v_refs = (refs[a * k:(a + 1) * k] for a in range(4))
        refs = refs[4 * k:]
        reduced_in, sent, refs = refs[:nr], refs[nr:nr + nx], refs[nr + nx:]
        outs, pairs, sums, landed, refs = refs[:4 * k], refs[4 * k:4 * k + nr], refs[4 * k + nr:4 * k + 2 * nr], \
            refs[4 * k + 2 * nr:4 * k + 2 * nr + nx], refs[4 * k + 2 * nr + nx:]
        mine_v, pair_v, sum_v, refs = refs[:nr], refs[nr:2 * nr], refs[2 * nr:3 * nr], refs[3 * nr:]
        if travels:
            reduce_ops = _pair_then_chip_ops(reduced_in, pairs, sums, mine_v, pair_v, sum_v, *refs[:7])
            direct_ops = _exchange_ops(sent[:nd], landed[:nd], [False] * nd, *refs[7:10])
            gather_ops = _gather_ops(sent[nd:], landed[nd:], *refs[10:13])

            @pl.when(pl.program_id(0) == 0)
            def _():
                direct_ops[0]()
                gather_ops[0]()
                reduce_ops[0]()

        for a in range(k):
            g = _sum_slabs(p_refs[a])
            outs[4 * a][...] = g
            outs[4 * a + 1][...], outs[4 * a + 2][...], outs[4 * a + 3][...] = _adamw(w_refs[a][...], g, m_refs[a][...], v_refs[a][...])

        if travels:
            @pl.when(pl.program_id(0) == steps - 1)
            def _():
                gather_ops[1]()
                reduce_ops[1]()
                gather_ops[2]()
                direct_ops[1]()

    blk = pl.BlockSpec((tr, c), lambda i: (i, 0))
    out = pl.pallas_call(
        body, name=name, grid=(steps,),
        in_specs=[pl.BlockSpec((N_DEV, tr, c), lambda i: (0, i, 0))] * k + [blk] * (3 * k) + [ANY] * (nr + nx),
        out_specs=[blk] * (4 * k) + [ANY] * (2 * nr + nx),
        out_shape=[jax.ShapeDtypeStruct((r, c), F32)] * (4 * k) + chip_slabs + chip_slabs
        + [jax.ShapeDtypeStruct((N_DEV, *a.shape) if w else a.shape, a.dtype) for a, w in zip(to_exchange, whole)],
        scratch_shapes=([pltpu.VMEM(a.shape, a.dtype) for a in chip_slabs] * 3 + _pair_then_chip_sems(nr) + _exchange_sems(nd)
                        + _exchange_sems(nx - nd) if travels else []),
        compiler_params=_params("arbitrary"),
    )(*parts, *ws, *ms, *vs, *to_reduce, *to_exchange)
    return [tuple(out[4 * a:4 * a + 4]) for a in range(k)], out[4 * k + nr:4 * k + 2 * nr], out[4 * k + 2 * nr:]


def _reduce_adamw_small(parts, ws, ms, vs, loss_parts):
    n = len(parts)

    def body(*refs):
        p_refs, w_refs, m_refs, v_refs = (refs[k * n:(k + 1) * n] for k in range(4))
        outs = refs[4 * n + 1:]
        outs[4 * n][...] = _sum_slabs(refs[4 * n])
        for a in range(n):
            g = _sum_slabs(p_refs[a])
            outs[4 * a][...] = g
            outs[4 * a + 1][...], outs[4 * a + 2][...], outs[4 * a + 3][...] = _adamw(w_refs[a][...], g, m_refs[a][...], v_refs[a][...])

    out = pl.pallas_call(
        body, name="adamw_rest",
        out_shape=[jax.ShapeDtypeStruct(w.shape, F32) for w in ws for _ in range(4)] + [jax.ShapeDtypeStruct(loss_parts.shape[1:], F32)],
        compiler_params=pltpu.CompilerParams(vmem_limit_bytes=VMEM_LIMIT_BYTES),
    )(*parts, *ws, *ms, *vs, loss_parts)
    return [tuple(out[4 * a:4 * a + 4]) for a in range(n)], out[4 * n]


def kernel(x, meta_tokens, norm_mix_pre, w_in, conv_w, pool_w, pool_scale, w_out, norm_mix_post, norm_ffn_pre, w_gate, w_up, w_down, norm_ffn_post, loss_target, m_meta_tokens, m_norm_mix_pre, m_w_in, m_conv_w, m_pool_w, m_pool_scale, m_w_out, m_norm_mix_post, m_norm_ffn_pre, m_w_gate, m_w_up, m_w_down, m_norm_ffn_post, v_meta_tokens, v_norm_mix_pre, v_w_in, v_conv_w, v_pool_w, v_pool_scale, v_w_out, v_norm_mix_post, v_norm_ffn_pre, v_w_gate, v_w_up, v_w_down, v_norm_ffn_post):
    n_seq, seq, d = x.shape
    x2d = x.reshape(n_seq * seq, d)
    target = loss_target.reshape(n_seq * seq, d)

    t_ = lambda a: jnp.swapaxes(a[0], 0, 1)
    pw, ps = pool_w[0], pool_scale

    (h1, z, m, pooled, mixed), (win_b, wout_b, meta, conv, a_meta, z_meta), ffn_slabs = _gather_and_mixer_forward(
        x2d, [w_in[0], w_out[0], meta_tokens, conv_w[0]], [t_(w_gate), t_(w_up), w_down[0]], norm_mix_pre, pw, ps, norm_mix_post, n_seq)
    wg_b, wu_b, wd_b = (s.reshape(-1, d) for s in ffn_slabs)
    f, act, dd, dgate, dup, dh1, loss_sum, dg3, dg4 = _ffn_forward_backward(h1, target, norm_ffn_pre, wg_b, wu_b, wd_b, norm_ffn_post)
    ffn_grads, landing = _ffn_weight_grads(f, dd, dgate, dup, act)
    (gx, dwin, dwout, dg1, dg2, dconv, dpw, dps, dmeta), ffn_parts = _mixer_backward(
        x2d, dh1, m, z, pooled, mixed, meta, a_meta, z_meta, norm_mix_pre, win_b, conv, pw, ps, wout_b, norm_mix_post, n_seq,
        ffn_grads, landing)

    dmeta_s = jnp.transpose(dmeta.reshape(N_META, N_DEV, -1), (1, 0, 2))
    dconv_s = jnp.transpose(dconv.reshape(CONV_WIDTH, N_DEV, -1), (1, 0, 2))
    _, (win_parts, wout_parts), last = _reduce_adamw_carrying(
        [], [], [], [], [dwin, dwout.reshape(N_DEV, -1, d)], [dmeta_s, dconv_s, dg1, dg2, dg3, dg4, dpw.astype(BF16), dps, loss_sum],
        [False] * 2 + [True] * 7, "exchange_rest")
    ffn_res, _, _ = _reduce_adamw_carrying(
        ffn_parts, [t_(w_gate), t_(w_up), w_down[0]], [t_(m_w_gate), t_(m_w_up), m_w_down[0]], [t_(v_w_gate), t_(v_w_up), v_w_down[0]],
        [], [], [], "adamw_ffn")
    replicated = last[2:8]

    names = ["meta_tokens", "norm_mix_pre", "w_in", "conv_w", "pool_w", "pool_scale", "w_out", "norm_mix_post", "norm_ffn_pre", "w_gate",
             "w_up", "w_down", "norm_ffn_post"]
    res = {"w_gate": tuple(jnp.swapaxes(o, 0, 1)[None] for o in ffn_res[0]),
           "w_up": tuple(jnp.swapaxes(o, 0, 1)[None] for o in ffn_res[1]), "w_down": tuple(o[None] for o in ffn_res[2])}
    rest_names = ["w_in", "w_out", "meta_tokens", "conv_w", "norm_mix_pre", "norm_mix_post", "norm_ffn_pre", "norm_ffn_post", "pool_w",
                  "pool_scale"]
    rest_res, loss = _reduce_adamw_small(
        [win_parts, wout_parts, last[0], last[1], *replicated],
        [w_in[0], w_out[0], meta_tokens, conv_w[0], norm_mix_pre, norm_mix_post, norm_ffn_pre, norm_ffn_post, pool_w[0], pool_scale],
        [m_w_in[0], m_w_out[0], m_meta_tokens, m_conv_w[0], m_norm_mix_pre, m_norm_mix_post, m_norm_ffn_pre, m_norm_ffn_post,
         m_pool_w[0], m_pool_scale],
        [v_w_in[0], v_w_out[0], v_meta_tokens, v_conv_w[0], v_norm_mix_pre, v_norm_mix_post, v_norm_ffn_pre, v_norm_ffn_post,
         v_pool_w[0], v_pool_scale], last[8])
    for nm, r in zip(rest_names, rest_res):
        res[nm] = tuple(o[None] for o in r) if nm in ("w_in", "w_out", "conv_w", "pool_w") else r

    return (loss[0, 0], gx.reshape(n_seq, seq, d), *[res[nm][0] for nm in names], *[res[nm][1] for nm in names],
            *[res[nm][2] for nm in names], *[res[nm][3] for nm in names])
```

```python
import jax
import jax.numpy as jnp
from jax import lax
from jax.experimental import pallas as pl
from jax.experimental.pallas import tpu as pltpu

F32, BF16 = jnp.float32, jnp.bfloat16
RMS_EPS = 1e-6
N_META = 16
CONV_WIDTH = 3
POOL_WINDOWS = (2, 4, 8, 16)
POOL_GROUP = 128
HALO = 16
N_DEV = 8
MESH_AXES = ("x", "y", "c")
MESH = pl.DeviceIdType.MESH
VMEM_LIMIT_BYTES = 56 * 1024 * 1024
ADAMW_BLOCK_ELEMS = 96 * 1024
TM_MIX = 512
TM_FFN = 256
FFN_CHUNK = 512
FFN_BACKWARD_LAG = 3
TM_WGRAD = 512
FF_CHUNKS = 2

ADAM_LR, ADAM_B1, ADAM_B2, ADAM_EPS, ADAM_WD, ADAM_STEP = 0.001, 0.9, 0.999, 1e-08, 0.01, 10


def _dot(a, b):
    return jnp.dot(a, b, preferred_element_type=F32)


def _dot_nt(a, b):
    return lax.dot_general(a, b, (((1,), (1,)), ((), ())), preferred_element_type=F32)


def _dot_tn(a, b):
    return lax.dot_general(a, b, (((0,), (0,)), ((), ())), preferred_element_type=F32)


def _rms_stats(h):
    rstd = lax.rsqrt(jnp.mean(h * h, axis=-1, keepdims=True) + RMS_EPS)
    return h * rstd, rstd


def _rms_bwd(hat, rstd, g, dy):
    gdy = dy * g
    proj = jnp.mean(gdy * hat, axis=-1, keepdims=True)
    return rstd * (gdy - hat * proj), jnp.sum(dy * hat, axis=0, keepdims=True)


def _params(*semantics):
    return pltpu.CompilerParams(dimension_semantics=semantics or None, vmem_limit_bytes=VMEM_LIMIT_BYTES)


def _resident(shape):
    zeros = (0,) * len(shape)
    return pl.BlockSpec(shape, lambda *_: zeros, pipeline_mode=pl.Buffered(1))


def _const(shape):
    zeros = (0,) * len(shape)
    return pl.BlockSpec(shape, lambda *_: zeros)


ANY = pl.BlockSpec(memory_space=pl.ANY)


def _my_place():
    x, y, c = (lax.axis_index(a) for a in MESH_AXES)
    return x, y, c


def _exchange_sems(n):
    return [pltpu.SemaphoreType.DMA((n, N_DEV - 1)), pltpu.SemaphoreType.DMA((n, N_DEV - 1)), pltpu.SemaphoreType.DMA((n,))]


def _gather_ops(srcs, outs, send_sems, recv_sems, local_sems, core_major=False):
    n = len(srcs)
    x, y, c = _my_place()
    me, sibling = (x, y, c), (x, y, 1 - c)
    chips = [(1 - x, y), (x, 1 - y), (1 - x, 1 - y)]

    def slab(px, py, pc):
        return 4 * pc + 2 * px + py if core_major else 4 * px + 2 * py + pc

    def copy(a, k, block, to, src=None):
        dst = outs[a].at[slab(*block)]
        return pltpu.make_async_remote_copy(
            src_ref=dst if src is None else src, dst_ref=dst, send_sem=send_sems.at[a, k], recv_sem=recv_sems.at[a, k],
            device_id=to, device_id_type=MESH)

    def mine(a):
        return pltpu.make_async_copy(srcs[a], outs[a].at[slab(*me)], local_sems.at[a])

    def first(a):
        return [copy(a, 0, me, sibling, src=srcs[a])] + [copy(a, 1 + j, me, (*chip, c), src=srcs[a]) for j, chip in enumerate(chips)]

    def passed(a, j):
        return copy(a, 4 + j, (*chips[j], c), sibling)

    def start():
        for a in range(n):
            mine(a).start()
            for cp in first(a):
                cp.start()

    def forward():
        for j, chip in enumerate(chips):
            for a in range(n):
                copy(a, 1 + j, (*chip, c), me).wait_recv()
                passed(a, j).start()

    def finish():
        for a in range(n):
            copy(a, 0, sibling, me).wait_recv()
            for j, chip in enumerate(chips):
                copy(a, 4 + j, (*chip, 1 - c), me).wait_recv()
        for a in range(n):
            for cp in first(a) + [passed(a, j) for j in range(len(chips))]:
                cp.wait_send()
            mine(a).wait()

    return start, forward, finish


def _exchange_ops(ins, outs, whole, send_sems, recv_sems, local_sems):
    n = len(ins)
    x, y, c = _my_place()
    me = 4 * x + 2 * y + c

    def src(a, i):
        return ins[a] if whole[a] else ins[a].at[i]

    def mine(a):
        return pltpu.make_async_copy(src(a, me), outs[a].at[me], local_sems.at[a])

    def send(a, k):
        to = (me + k) % N_DEV
        return pltpu.make_async_remote_copy(
            src_ref=src(a, to), dst_ref=outs[a].at[me], send_sem=send_sems.at[a, k - 1], recv_sem=recv_sems.at[a, k - 1],
            device_id=(to // 4, (to // 2) % 2, to % 2), device_id_type=MESH)

    def landed(a, k):
        frm = (me + N_DEV - k) % N_DEV
        return pltpu.make_async_remote_copy(
            src_ref=src(a, frm), dst_ref=outs[a].at[frm], send_sem=send_sems.at[a, k - 1], recv_sem=recv_sems.at[a, k - 1],
            device_id=(x, y, c), device_id_type=MESH)

    def start():
        for a in range(n):
            mine(a).start()
            for k in range(1, N_DEV):
                send(a, k).start()

    def finish():
        for a in range(n):
            for k in range(1, N_DEV):
                landed(a, k).wait_recv()
        for a in range(n):
            for k in range(1, N_DEV):
                send(a, k).wait_send()
            mine(a).wait()

    return start, finish


def _core_exchange_sems(n):
    return [pltpu.SemaphoreType.DMA((n, 4)), pltpu.SemaphoreType.DMA((n, N_DEV)), pltpu.SemaphoreType.DMA((n,))]


def _core_exchange_ops(ins, outs, to_core, send_sems, recv_sems, local_sems):
    n = len(ins)
    x, y, c = _my_place()
    me = 4 * x + 2 * y + c
    others = [(0, 1), (1, 0), (1, 1)]

    def slab(a, p):
        if len(ins[a].shape) == len(outs[a].shape):
            return ins[a].at[p]
        rows = outs[a].shape[1]
        return ins[a].at[pl.ds(pl.multiple_of(p * rows, 16), rows), :]

    def send(a, dx, dy):
        tx, ty = (x + dx) % 2, (y + dy) % 2
        return pltpu.make_async_remote_copy(
            src_ref=slab(a, 4 * to_core + 2 * tx + ty), dst_ref=outs[a].at[me], send_sem=send_sems.at[a, 2 * dx + dy],
            recv_sem=recv_sems.at[a, 2 * (2 * dx + dy) + c], device_id=(tx, ty, to_core), device_id_type=MESH)

    def mine(a):
        return pltpu.make_async_copy(slab(a, 4 * to_core + 2 * x + y), outs[a].at[me], local_sems.at[a])

    def landed(a, dx, dy, sc):
        frm = 4 * ((x + dx) % 2) + 2 * ((y + dy) % 2) + sc
        return pltpu.make_async_remote_copy(
            src_ref=slab(a, 0), dst_ref=outs[a].at[frm], send_sem=send_sems.at[a, 0], recv_sem=recv_sems.at[a, 2 * (2 * dx + dy) + sc],
            device_id=(x, y, c), device_id_type=MESH)

    def start():
        for a in range(n):
            for dx, dy in others:
                send(a, dx, dy).start()
            pl.when(c == to_core)(mine(a).start)
            pl.when(c != to_core)(send(a, 0, 0).start)

    def finish():
        @pl.when(c == to_core)
        def _():
            for a in range(n):
                for dx, dy in [(0, 0)] + others:
                    for sc in (0, 1):
                        if (dx, dy, sc) != (0, 0, to_core):
                            landed(a, dx, dy, sc).wait_recv()
            for a in range(n):
                mine(a).wait()

        @pl.when(c != to_core)
        def _():
            for a in range(n):
                send(a, 0, 0).wait_send()

        for a in range(n):
            for dx, dy in others:
                send(a, dx, dy).wait_send()

    return start, finish


N_CHIP = 4


def _pair_then_chip_sems(n):
    return [pltpu.SemaphoreType.DMA((n, N_CHIP)) for _ in range(6)] + [pltpu.SemaphoreType.DMA((n,))]


def _pair_then_chip_ops(ins, pairs, outs, mine_v, pair_v, sum_v, pair_send, pair_recv, chip_send, chip_recv, load_a, load_b, own_sem):
    n = len(ins)
    x, y, c = _my_place()
    chip = 2 * x + y
    chips = [(0, 0), (0, 1), (1, 0), (1, 1)]
    others = [(0, 1), (1, 0), (1, 1)]

    def to_sibling(a, j):
        px, py = chips[j]
        return pltpu.make_async_remote_copy(
            src_ref=ins[a].at[4 * px + 2 * py + 1 - c], dst_ref=pairs[a].at[j], send_sem=pair_send.at[a, j], recv_sem=pair_recv.at[a, j],
            device_id=(x, y, 1 - c), device_id_type=MESH)

    def spread(a, dx, dy):
        tx, ty = (x + dx) % 2, (y + dy) % 2
        return pltpu.make_async_remote_copy(
            src_ref=sum_v[a].at[2 * tx + ty], dst_ref=outs[a].at[chip], send_sem=chip_send.at[a, 2 * dx + dy],
            recv_sem=chip_recv.at[a, 2 * dx + dy], device_id=(tx, ty, c), device_id_type=MESH)

    def landed(a, dx, dy):
        frm = 2 * ((x + dx) % 2) + (y + dy) % 2
        return pltpu.make_async_remote_copy(
            src_ref=sum_v[a].at[0], dst_ref=outs[a].at[frm], send_sem=chip_send.at[a, 0], recv_sem=chip_recv.at[a, 2 * dx + dy],
            device_id=(x, y, c), device_id_type=MESH)

    def own(a):
        return pltpu.make_async_copy(sum_v[a].at[chip], outs[a].at[chip], own_sem.at[a])

    def pair():
        loads = []
        for a in range(n):
            for j, (px, py) in enumerate(chips):
                to_sibling(a, j).start()
                loads.append(pltpu.make_async_copy(ins[a].at[4 * px + 2 * py + c], mine_v[a].at[j], load_a.at[a, j]))
                loads[-1].start()
        for a in range(n):
            for j in range(N_CHIP):
                to_sibling(a, j).wait_recv()
                loads.append(pltpu.make_async_copy(pairs[a].at[j], pair_v[a].at[j], load_b.at[a, j]))
                loads[-1].start()
        for cp in loads:
            cp.wait()
        for a in range(n):
            sum_v[a][...] = (mine_v[a][...].astype(F32) + pair_v[a][...].astype(F32)).astype(sum_v[a].dtype)

    def start():
        pair()
        for a in range(n):
            own(a).start()
            for dx, dy in others:
                spread(a, dx, dy).start()

    def finish():
        for a in range(n):
            for dx, dy in others:
                landed(a, dx, dy).wait_recv()
        for a in range(n):
            for dx, dy in others:
                spread(a, dx, dy).wait_send()
            for j in range(N_CHIP):
                to_sibling(a, j).wait_send()
            own(a).wait()

    return start, finish


def _window_sum(x, win, ahead):
    n = x.shape[0]
    span = 1
    while span < win:
        x = x + pltpu.roll(x, n - span if ahead else span, 0)
        span *= 2
    return x


def _conv_branch(z, ext_u, conv_ref, tm):
    c_w = z.shape[1] // 4
    b, c, v = z[:, :c_w], z[:, c_w:2 * c_w], z[:, 2 * c_w:3 * c_w]
    u = c * v
    ext_u[pl.ds(HALO, tm), :] = u
    u1 = ext_u[pl.ds(HALO - 1, tm), :]
    u2 = ext_u[pl.ds(HALO - 2, tm), :]
    yc = conv_ref[pl.ds(2, 1), :] * u + conv_ref[pl.ds(1, 1), :] * u1 + conv_ref[pl.ds(0, 1), :] * u2
    return b, c, v, u, u1, u2, yc


def _pool_branch(p, ext_p, pool_w_ref, tm):
    ext_p[pl.ds(HALO, tm), :] = p
    pooled, mixed = [], []
    for g, win in enumerate(POOL_WINDOWS):
        s = _window_sum(ext_p[:, pl.ds(POOL_GROUP * g, POOL_GROUP)], win, ahead=False)[HALO:HALO + tm, :]
        pooled.append((s * (1.0 / win) - p[:, POOL_GROUP * g:POOL_GROUP * (g + 1)]).astype(BF16))
        mixed.append(_dot(pooled[-1], pool_w_ref[g].astype(BF16)))
    return pooled, mixed


def _gather_and_mixer_forward(x2d, mixer_shards, ffn_shards, g1, pool_w, pool_scale, g2, n_seq):
    t, d = x2d.shape
    zs, rs, ms, cs = mixer_shards[0].shape[1], mixer_shards[1].shape[0], mixer_shards[2].shape[1], mixer_shards[3].shape[1]
    zw, cw = N_DEV * zs, N_DEV * cs
    s = t // n_seq
    tm = min(TM_MIX, s)
    nj = s // tm
    n1, n2 = len(mixer_shards), len(ffn_shards)
    dtypes = [BF16, BF16, F32, F32] + [BF16] * n2
    shards = list(mixer_shards) + list(ffn_shards)

    def body(x_ref, *rest):
        shard_refs, (g1_ref, pw_ref, ps_ref, g2_ref), rest = rest[:n1 + n2], rest[n1 + n2:n1 + n2 + 4], rest[n1 + n2 + 4:]
        (h1_ref, z_ref, m_ref, pooled_ref, mixed_ref, win_o, wout_o, meta_o, conv_o, am_o, zm_o), rest = rest[:11], rest[11:]
        slabs, rest = rest[:n1 + n2], rest[n1 + n2:]
        stages, rest = rest[:n1 + n2], rest[n1 + n2:]
        win_v, wout_v, meta_v, conv_v, ext_u, ext_p, sem = rest[:7]
        first = _gather_ops(stages[:n1], slabs[:n1], *rest[7:10])
        later = _gather_ops(stages[n1:], slabs[n1:], *rest[10:13], core_major=True)

        @pl.when((pl.program_id(0) == 0) & (pl.program_id(1) == 0))
        def _():
            for src, dst in zip(shard_refs, stages):
                dst[...] = src[...].astype(dst.dtype)
            first[0]()
            later[0]()
            first[1]()
            first[2]()
            copies = [pltpu.make_async_copy(slabs[0].at[i], win_v.at[:, pl.ds(zs * i, zs)], sem.at[i]) for i in range(N_DEV)]
            copies += [pltpu.make_async_copy(slabs[1].at[i], wout_v.at[pl.ds(rs * i, rs), :], sem.at[N_DEV + i]) for i in range(N_DEV)]
            copies += [pltpu.make_async_copy(slabs[2], meta_v, sem.at[2 * N_DEV]), pltpu.make_async_copy(slabs[3], conv_v, sem.at[2 * N_DEV + 1])]
            for cp in copies:
                cp.start()
            for cp in copies:
                cp.wait()
            copies = [pltpu.make_async_copy(win_v, win_o, sem.at[0]), pltpu.make_async_copy(wout_v, wout_o, sem.at[1])]
            for cp in copies:
                cp.start()
            for i in range(N_DEV):
                meta_o[:, pl.ds(ms * i, ms)] = meta_v[i]
                conv_o[:, pl.ds(cs * i, cs)] = conv_v[i]
            hat, _ = _rms_stats(meta_o[...])
            a = (hat * g1_ref[...]).astype(BF16)
            am_o[...] = a
            zm_o[...] = _dot(a, win_v[...])
            for cp in copies:
                cp.wait()

        @pl.when(pl.program_id(1) == 0)
        def _():
            zm = zm_o[...]
            ext_u[pl.ds(0, HALO), :] = zm[:, cw:2 * cw] * zm[:, 2 * cw:3 * cw]
            ext_p[pl.ds(0, HALO), :] = zm[:, 3 * cw:]

        h0 = x_ref[...]
        hat, _ = _rms_stats(h0)
        z = _dot((hat * g1_ref[...]).astype(BF16), win_v[...])
        z_ref[...] = z.astype(BF16)
        b, _, _, _, _, _, yc = _conv_branch(z, ext_u, conv_o, tm)
        pooled, mixed = _pool_branch(z[:, 3 * cw:], ext_p, pw_ref, tm)
        pooled_ref[...] = jnp.concatenate(pooled, axis=1)
        mixed_ref[...] = jnp.concatenate(mixed, axis=1).astype(BF16)
        ps = ps_ref[...]
        y = [b * yc] + [mixed[g] * ps[:, POOL_GROUP * g:POOL_GROUP * (g + 1)] for g in range(len(POOL_WINDOWS))]
        m = _dot(jnp.concatenate(y, axis=1).astype(BF16), wout_v[...])
        m_ref[...] = m
        m_hat, _ = _rms_stats(m)
        h1_ref[...] = h0 + m_hat * g2_ref[...]
        ext_u[pl.ds(0, HALO), :] = ext_u[pl.ds(tm, HALO), :]
        ext_p[pl.ds(0, HALO), :] = ext_p[pl.ds(tm, HALO), :]

        @pl.when((pl.program_id(0) == n_seq - 1) & (pl.program_id(1) == nj - 1))
        def _():
            later[1]()
            later[2]()

    row = lambda b, j: (b * nj + j, 0)
    vmem = pl.BlockSpec(memory_space=pltpu.VMEM)
    small = [(N_META, d), (CONV_WIDTH, cw), (N_META, d), (N_META, zw)]
    out = pl.pallas_call(
        body, name="gather_and_mixer_forward", grid=(n_seq, nj),
        in_specs=[pl.BlockSpec((tm, d), row)] + [vmem] * (n1 + n2)
        + [_const(g1.shape), _const(pool_w.shape), _const(pool_scale.shape), _const(g2.shape)],
        out_specs=[pl.BlockSpec((tm, d), row), pl.BlockSpec((tm, zw), row), pl.BlockSpec((tm, d), row), pl.BlockSpec((tm, cw), row),
                   pl.BlockSpec((tm, cw), row), ANY, ANY] + [_const(sh) for sh in small] + [ANY] * (n1 + n2),
        out_shape=[jax.ShapeDtypeStruct((t, d), F32), jax.ShapeDtypeStruct((t, zw), BF16), jax.ShapeDtypeStruct((t, d), F32),
                   jax.ShapeDtypeStruct((t, cw), BF16), jax.ShapeDtypeStruct((t, cw), BF16),
                   jax.ShapeDtypeStruct((d, zw), BF16), jax.ShapeDtypeStruct((d, d), BF16),
                   jax.ShapeDtypeStruct(small[0], F32), jax.ShapeDtypeStruct(small[1], F32), jax.ShapeDtypeStruct(small[2], BF16),
                   jax.ShapeDtypeStruct(small[3], F32)]
        + [jax.ShapeDtypeStruct((N_DEV, *a.shape), dt) for a, dt in zip(shards, dtypes)],
        scratch_shapes=[pltpu.VMEM(a.shape, dt) for a, dt in zip(shards, dtypes)]
        + [pltpu.VMEM((d, zw), BF16), pltpu.VMEM((d, d), BF16), pltpu.VMEM((N_DEV, N_META, ms), F32),
           pltpu.VMEM((N_DEV, CONV_WIDTH, cs), F32), pltpu.VMEM((tm + HALO, cw), F32), pltpu.VMEM((tm + HALO, cw), F32),
           pltpu.SemaphoreType.DMA((2 * N_DEV + 2,))] + _exchange_sems(n1) + _exchange_sems(n2),
        compiler_params=_params("arbitrary", "arbitrary"),
    )(x2d, *shards, g1, pool_w, pool_scale, g2)
    return out[:5], out[5:11], out[11 + n1:]


def _mixer_backward(x2d, dh1, m, z, pooled, mixed, meta, a_meta, z_meta, g1, w_in, conv_w, pool_w, pool_scale, w_out, g2, n_seq,
                    to_exchange, landing):
    t, d = x2d.shape
    zw = w_in.shape[1]
    cw = zw // 4
    s = t // n_seq
    tm = min(TM_MIX, s)
    nj = s // tm
    n_groups = len(POOL_WINDOWS)
    zs = zw // N_DEV
    nx = len(to_exchange)
    n_in = 17
    given = [k for k, a in enumerate(landing) if a is not None]
    fresh = [k for k, a in enumerate(landing) if a is None]

    def body(x_ref, dh1_ref, m_ref, z_ref, zprev_ref, pooled_ref, mixed_ref, meta_ref, am_ref, zm_ref, g1_ref, win_ref, conv_ref, pw_ref, ps_ref, wout_ref,
             g2_ref, *rest):
        sent, rest = rest[:nx], rest[nx + len(given):]
        gx_ref, dwin_ref, dwout_ref, dg1_ref, dg2_ref, dconv_ref, dpw_ref, dps_ref, dmeta_ref = rest[:9]
        landed, rest = rest[9:9 + nx], rest[9 + nx:]
        ext_u, ext_dyc, ext_dq, acc_win, acc_wout, dz_meta, stage16, sem = rest[:8]
        north = _core_exchange_ops(sent, landed, 1, *rest[8:11])
        south = _core_exchange_ops([sent[k] for k in fresh], [landed[k] for k in fresh], 0, *rest[11:14])

        def start():
            north[0]()
            south[0]()

        def finish():
            south[1]()
            north[1]()

        b_id, j = pl.program_id(0), pl.program_id(1)
        jr = nj - 1 - j
        pl.when((b_id == 0) & (j == 0))(start)

        @pl.when((b_id == 0) & (j == 0))
        def _():
            acc_win[...] = jnp.zeros_like(acc_win)
            acc_wout[...] = jnp.zeros_like(acc_wout)
            dz_meta[...] = jnp.zeros_like(dz_meta)
            for r in (dg1_ref, dg2_ref, dconv_ref, dpw_ref, dps_ref, dmeta_ref):
                r[...] = jnp.zeros_like(r)

        @pl.when(j == 0)
        def _():
            ext_dyc[pl.ds(tm, HALO), :] = jnp.zeros((HALO, cw), F32)
            ext_dq[pl.ds(tm, HALO), :] = jnp.zeros((HALO, cw), F32)

        zm = zm_ref[...]
        halo = jnp.where(jr == 0, zm, zprev_ref[...].astype(F32))
        ext_u[pl.ds(0, HALO), :] = halo[:, cw:2 * cw] * halo[:, 2 * cw:3 * cw]

        dh1v = dh1_ref[...]
        m_hat, m_rstd = _rms_stats(m_ref[...])
        dm, dg2 = _rms_bwd(m_hat, m_rstd, g2_ref[...], dh1v)
        dg2_ref[...] += dg2
        dm = dm.astype(BF16)
        dycat = _dot_nt(dm, wout_ref[...])

        b, c, v, u, u1, u2, yc = _conv_branch(z_ref[...].astype(F32), ext_u, conv_ref, tm)
        mixed = [mixed_ref[:, pl.ds(POOL_GROUP * g, POOL_GROUP)].astype(F32) for g in range(n_groups)]
        ps = ps_ref[...]
        y = [b * yc] + [mixed[g] * ps[:, POOL_GROUP * g:POOL_GROUP * (g + 1)] for g in range(n_groups)]
        ycat = jnp.concatenate(y, axis=1).astype(BF16)
        acc_wout[...] += _dot_tn(ycat, dm)

        dyconv = dycat[:, :cw]
        db = dyconv * yc
        dyc = dyconv * b
        ext_dyc[pl.ds(0, tm), :] = dyc
        du = (conv_ref[pl.ds(2, 1), :] * dyc + conv_ref[pl.ds(1, 1), :] * ext_dyc[pl.ds(1, tm), :]
              + conv_ref[pl.ds(0, 1), :] * ext_dyc[pl.ds(2, tm), :])
        dconv_ref[pl.ds(2, 1), :] += jnp.sum(dyc * u, axis=0, keepdims=True)
        dconv_ref[pl.ds(1, 1), :] += jnp.sum(dyc * u1, axis=0, keepdims=True)
        dconv_ref[pl.ds(0, 1), :] += jnp.sum(dyc * u2, axis=0, keepdims=True)

        dp = []
        for g, win in enumerate(POOL_WINDOWS):
            lanes = pl.ds(POOL_GROUP * g, POOL_GROUP)
            dypool = dycat[:, cw + POOL_GROUP * g:cw + POOL_GROUP * (g + 1)]
            dps_ref[:, lanes] += jnp.sum(dypool * mixed[g], axis=0, keepdims=True)
            dmixed = (dypool * ps[:, POOL_GROUP * g:POOL_GROUP * (g + 1)]).astype(BF16)
            dq = _dot_nt(dmixed, pw_ref[g].astype(BF16))
            dpw_ref[g] += _dot_tn(pooled_ref[:, lanes], dmixed)
            ext_dq[pl.ds(0, tm), lanes] = dq
            acc = _window_sum(ext_dq[:, lanes], win, ahead=True)[0:tm, :]
            dp.append(acc * (1.0 / win) - dq)

        dz = jnp.concatenate([db, du * v, du * c] + dp, axis=1).astype(BF16)
        da = _dot_nt(dz, win_ref[...])
        h0 = x_ref[...]
        hat0, rstd0 = _rms_stats(h0)
        g1 = g1_ref[...]
        acc_win[...] += _dot_tn((hat0 * g1).astype(BF16), dz)
        dh0, dg1 = _rms_bwd(hat0, rstd0, g1, da)
        dg1_ref[...] += dg1
        gx_ref[...] = dh1v + dh0

        ext_dyc[pl.ds(tm, HALO), :] = ext_dyc[pl.ds(0, HALO), :]
        ext_dq[pl.ds(tm, HALO), :] = ext_dq[pl.ds(0, HALO), :]

        @pl.when(jr == 0)
        def _():
            ext_dyc[pl.ds(tm - HALO, HALO), :] = jnp.zeros((HALO, cw), F32)
            ext_dq[pl.ds(tm - HALO, HALO), :] = jnp.zeros((HALO, cw), F32)
            du_m = (conv_ref[pl.ds(1, 1), :] * ext_dyc[pl.ds(tm - HALO + 1, HALO), :]
                    + conv_ref[pl.ds(0, 1), :] * ext_dyc[pl.ds(tm - HALO + 2, HALO), :])
            dp_m = []
            for g, win in enumerate(POOL_WINDOWS):
                lanes = pl.ds(POOL_GROUP * g, POOL_GROUP)
                acc = ext_dq[pl.ds(tm - HALO + 1, HALO), lanes]
                for k in range(2, win):
                    acc = acc + ext_dq[pl.ds(tm - HALO + k, HALO), lanes]
                dp_m.append(acc * (1.0 / win))
            dz_meta[...] += jnp.concatenate(
                [jnp.zeros((HALO, cw), F32), du_m * zm[:, 2 * cw:3 * cw], du_m * zm[:, cw:2 * cw]] + dp_m, axis=1)

        @pl.when((b_id == n_seq - 1) & (j == nj - 1))
        def _():
            dz_m = dz_meta[...].astype(BF16)
            acc_win[...] += _dot_tn(am_ref[...], dz_m)
            hat_m, rstd_m = _rms_stats(meta_ref[...])
            dmeta, dg1_m = _rms_bwd(hat_m, rstd_m, g1, _dot_nt(dz_m, win_ref[...]))
            dg1_ref[...] += dg1_m
            dmeta_ref[...] = dmeta
            pieces = [(acc_win, zs * i, dwin_ref.at[i]) for i in range(N_DEV)]
            pieces += [(acc_wout, zs * i, dwout_ref.at[:, pl.ds(zs * i, zs)]) for i in range(d // zs)]
            copies = []
            for k, (acc, col, dst) in enumerate(pieces):
                if k >= 2:
                    copies[k - 2].wait()
                stage16[k % 2] = acc[:, pl.ds(col, zs)].astype(BF16)
                copies.append(pltpu.make_async_copy(stage16.at[k % 2], dst, sem.at[k % 2]))
                copies[k].start()
            copies[-2].wait()
            copies[-1].wait()
            finish()

    row = lambda b, j: (b * nj + nj - 1 - j, 0)
    prev = lambda b, j: (jnp.maximum((b * s + (nj - 1 - j) * tm) // HALO - 1, 0), 0)
    small = [g1.shape, g2.shape, conv_w.shape, pool_w.shape, pool_scale.shape, meta.shape]
    out = pl.pallas_call(
        body, name="mixer_backward", grid=(n_seq, nj),
        in_specs=[pl.BlockSpec((tm, d), row), pl.BlockSpec((tm, d), row), pl.BlockSpec((tm, d), row), pl.BlockSpec((tm, zw), row),
                  pl.BlockSpec((HALO, zw), prev), pl.BlockSpec((tm, cw), row), pl.BlockSpec((tm, cw), row), _const(meta.shape), _const(a_meta.shape), _const(z_meta.shape), _const(g1.shape),
                  _resident(w_in.shape), _const(conv_w.shape), _const(pool_w.shape), _const(pool_scale.shape), _resident(w_out.shape),
                  _const(g2.shape)] + [ANY] * (nx + len(given)),
        out_specs=[pl.BlockSpec((tm, d), row), ANY, ANY] + [_const(sh) for sh in small] + [ANY] * nx,
        out_shape=[jax.ShapeDtypeStruct((t, d), F32), jax.ShapeDtypeStruct((N_DEV, d, zs), BF16),
                   jax.ShapeDtypeStruct(w_out.shape, BF16)] + [jax.ShapeDtypeStruct(sh, F32) for sh in small]
        + [jax.ShapeDtypeStruct((N_DEV, a.shape[0] // N_DEV, a.shape[1]), a.dtype) for a in to_exchange],
        input_output_aliases={n_in + nx + at: 9 + k for at, k in enumerate(given)},
        scratch_shapes=[pltpu.VMEM((tm + HALO, cw), F32)] * 3
        + [pltpu.VMEM(w_in.shape, F32), pltpu.VMEM(w_out.shape, F32), pltpu.VMEM((HALO, zw), F32), pltpu.VMEM((2, d, zs), BF16),
           pltpu.SemaphoreType.DMA((2,))] + _core_exchange_sems(nx) + _core_exchange_sems(len(fresh)),
        compiler_params=_params("arbitrary", "arbitrary"),
    )(x2d, dh1, m, z, z, pooled, mixed, meta, a_meta, z_meta, g1, w_in, conv_w, pool_w, pool_scale, w_out, g2, *to_exchange, *[landing[k] for k in given])
    return out[:9], out[9:]


def _ffn_forward_backward(h1, target, g3, w_gate, w_up, w_down, g4):
    t, d = h1.shape
    ff = w_gate.shape[0]
    tm = min(TM_FFN, t)
    nt = t // tm
    chunks = [(s, min(FFN_CHUNK, ff - s)) for s in range(0, ff, FFN_CHUNK)]

    def body(h1_ref, h1pp_ref, tgt_ref, g3_ref, wg_hbm, wu_hbm, wd_ref, g4_ref,
             f_ref, act_ref, dd_ref, dgate_ref, dup_ref, dh1_ref, loss_ref, dg3_ref, dg4_ref, *slots):
        gate_s, up_s, dd_s, dh2_s, df_s, wgu, wsem = slots
        i = pl.program_id(0)

        def gu(s, n):
            return wgu.at[pl.ds(2 * s, 2 * n), :]

        @pl.when(i == 0)
        def _():
            copies = []
            for k, (s, n) in enumerate(chunks):
                copies += [pltpu.make_async_copy(wg_hbm.at[pl.ds(s, n), :], wgu.at[pl.ds(2 * s, n), :], wsem.at[2 * k]),
                           pltpu.make_async_copy(wu_hbm.at[pl.ds(s, n), :], wgu.at[pl.ds(2 * s + n, n), :], wsem.at[2 * k + 1])]
            for k, cp in enumerate(copies):
                cp.start(priority=k % 2)
            for r in (loss_ref, dg3_ref, dg4_ref, gate_s, up_s, dd_s, dh2_s, df_s):
                r[...] = jnp.zeros_like(r)
            for cp in copies:
                cp.wait()

        def forward(slot):
            h1v = h1_ref[...]
            hat, _ = _rms_stats(h1v)
            f = (hat * g3_ref[...]).astype(BF16)
            f_ref[...] = f
            s, n = chunks[0]
            both = _dot_nt(f_ref[...], gu(s, n)[...])
            yield
            down = None
            for k, (s, n) in enumerate(chunks):
                gate, up = both[:, :n], both[:, n:]
                gate_s.at[slot][:, pl.ds(s, n)] = gate.astype(BF16)
                up_s.at[slot][:, pl.ds(s, n)] = up.astype(BF16)
                act = (gate * jax.nn.sigmoid(gate) * up).astype(BF16)
                act_ref[:, pl.ds(s, n)] = act
                if k + 1 < len(chunks):
                    s1, n1 = chunks[k + 1]
                    both = _dot_nt(f_ref[...], gu(s1, n1)[...])
                yield
                part = _dot(act_ref[:, pl.ds(s, n)], wd_ref[pl.ds(s, n), :])
                down = part if down is None else down + part
                yield
            d_hat, d_rstd = _rms_stats(down)
            g4 = g4_ref[...]
            err = h1v + d_hat * g4 - tgt_ref[...]
            loss_ref[...] += jnp.sum(err * err) * (0.5 / d)
            dh2 = err * (1.0 / d)
            dh2_s.at[slot][...] = dh2
            dd, dg4 = _rms_bwd(d_hat, d_rstd, g4, dh2)
            dg4_ref[...] += dg4
            dd = dd.astype(BF16)
            dd_ref[...] = dd
            dd_s.at[slot][...] = dd

        def backward(slot):
            s, n = chunks[0]
            dact = _dot_nt(dd_s.at[slot][...], wd_ref[pl.ds(s, n), :])
            yield
            df = None
            for k, (s, n) in enumerate(chunks):
                gate = gate_s.at[slot][:, pl.ds(s, n)].astype(F32)
                up = up_s.at[slot][:, pl.ds(s, n)].astype(F32)
                sig = jax.nn.sigmoid(gate)
                dup = (dact * (gate * sig)).astype(BF16)
                dgate = (dact * up * (sig * (1.0 + gate * (1.0 - sig)))).astype(BF16)
                dup_ref[:, pl.ds(s, n)] = dup
                dgate_ref[:, pl.ds(s, n)] = dgate
                if k + 1 < len(chunks):
                    s1, n1 = chunks[k + 1]
                    dact = _dot_nt(dd_s.at[slot][...], wd_ref[pl.ds(s1, n1), :])
                yield
                part = _dot(jnp.concatenate([dgate_ref[:, pl.ds(s, n)], dup_ref[:, pl.ds(s, n)]], axis=1), gu(s, n)[...])
                df = part if df is None else df + part
                yield
            df_s.at[slot][...] = df

        def last(slot):
            hat, rstd = _rms_stats(h1pp_ref[...])
            dh1, dg3 = _rms_bwd(hat, rstd, g3_ref[...], df_s.at[slot][...])
            dg3_ref[...] += dg3
            dh1_ref[...] = dh2_s.at[slot][...] + dh1

        def emit(parity, with_forward, with_backward, with_last):
            fwd = forward(parity) if with_forward else iter(())
            bwd = backward(1 - parity) if with_backward else iter(())
            next(fwd, None)
            if with_last:
                last(parity)
            for _ in range(FFN_BACKWARD_LAG):
                next(fwd, None)
            alive = True
            while alive:
                alive = next(bwd, True) is None
                alive = (next(fwd, True) is None) or alive

        @pl.when(i < nt)
        def _():
            emit(i % 2, True, True, True)

        @pl.when(i == nt)
        def _():
            emit(nt % 2, False, True, True)

        @pl.when(i == nt + 1)
        def _():
            emit((nt + 1) % 2, False, False, True)

    cur = lambda i: (jnp.minimum(i, nt - 1), 0)
    prev = lambda i: (jnp.clip(i - 1, 0, nt - 1), 0)
    prev2 = lambda i: (jnp.clip(i - 2, 0, nt - 1), 0)
    return pl.pallas_call(
        body, name="ffn_forward_backward", grid=(nt + 2,),
        in_specs=[pl.BlockSpec((tm, d), cur), pl.BlockSpec((tm, d), prev2), pl.BlockSpec((tm, d), cur), _const(g3.shape),
                  ANY, ANY, _resident(w_down.shape), _const(g4.shape)],
        out_specs=[pl.BlockSpec((tm, d), cur), pl.BlockSpec((tm, ff), cur), pl.BlockSpec((tm, d), cur), pl.BlockSpec((tm, ff), prev),
                   pl.BlockSpec((tm, ff), prev), pl.BlockSpec((tm, d), prev2), _const((8, 128)), _const(g3.shape), _const(g4.shape)],
        out_shape=[jax.ShapeDtypeStruct((t, d), BF16), jax.ShapeDtypeStruct((t, ff), BF16), jax.ShapeDtypeStruct((t, d), BF16),
                   jax.ShapeDtypeStruct((t, ff), BF16), jax.ShapeDtypeStruct((t, ff), BF16), jax.ShapeDtypeStruct((t, d), F32),
                   jax.ShapeDtypeStruct((8, 128), F32), jax.ShapeDtypeStruct(g3.shape, F32), jax.ShapeDtypeStruct(g4.shape, F32)],
        scratch_shapes=[pltpu.VMEM((2, tm, ff), BF16)] * 2 + [pltpu.VMEM((2, tm, d), BF16)] + [pltpu.VMEM((2, tm, d), F32)] * 2
        + [pltpu.VMEM((2 * ff, d), BF16), pltpu.SemaphoreType.DMA((2 * len(chunks),))],
        compiler_params=_params("arbitrary"),
    )(h1, h1, target, g3, w_gate, w_up, w_down, g4)


def _ffn_weight_grads(f, dd, dgate, dup, act):
    t, d = f.shape
    ff = dgate.shape[1]
    tm = min(TM_WGRAD, t)
    nt = t // tm
    fc = ff // FF_CHUNKS
    assert FF_CHUNKS == 2

    def body(f_ref, dd_ref, dgate_ref, dup_ref, act_ref, dwg_ref, dwu_ref, dwd_ref, *rest):
        landing, (acc_g, acc_u, acc_d, stage, sem) = rest[:2], rest[2:7]
        start, finish = _core_exchange_ops([dwg_ref, dwd_ref], landing, 0, *rest[7:])
        c, i = pl.program_id(0), pl.program_id(1)
        pl.when((c == 1) & (i == 0))(start)

        @pl.when(i == 0)
        def _():
            acc_g[...] = jnp.zeros_like(acc_g)
            acc_u[...] = jnp.zeros_like(acc_u)
            acc_d[...] = jnp.zeros_like(acc_d)

        fv = f_ref[...]
        acc_g[...] += _dot_tn(fv, dgate_ref[...])
        acc_u[...] += _dot_tn(fv, dup_ref[...])
        acc_d[...] += _dot_tn(act_ref[...], dd_ref[...])

        @pl.when(i == nt - 1)
        def _():
            rows = pl.ds(pl.multiple_of(c * fc, 16), fc)
            copies = []
            for k, (acc, out, transposed) in enumerate(((acc_d, dwd_ref, False), (acc_g, dwg_ref, True), (acc_u, dwu_ref, True))):
                if k >= 2:
                    copies[k - 2].wait()
                stage[k % 2] = (acc[...].T if transposed else acc[...]).astype(BF16)
                copies.append(pltpu.make_async_copy(stage.at[k % 2], out.at[rows, :], sem.at[k % 2]))
                copies[k].start()
            copies[-2].wait()
            copies[-1].wait()

        pl.when((c == 1) & (i == nt - 1))(finish)

    row = lambda c, i: (i, 0)
    col = lambda c, i: (i, c)
    out = pl.pallas_call(
        body, name="ffn_weight_grads", grid=(FF_CHUNKS, nt),
        in_specs=[pl.BlockSpec((tm, d), row), pl.BlockSpec((tm, d), row), pl.BlockSpec((tm, fc), col), pl.BlockSpec((tm, fc), col),
                  pl.BlockSpec((tm, fc), col)],
        out_specs=[ANY] * 5,
        out_shape=[jax.ShapeDtypeStruct((ff, d), BF16)] * 3 + [jax.ShapeDtypeStruct((N_DEV, ff // N_DEV, d), BF16)] * 2,
        scratch_shapes=[pltpu.VMEM((d, fc), F32), pltpu.VMEM((d, fc), F32), pltpu.VMEM((fc, d), F32), pltpu.VMEM((2, fc, d), BF16),
                        pltpu.SemaphoreType.DMA((2,))] + _core_exchange_sems(2),
        compiler_params=_params("arbitrary", "arbitrary"),
    )(f, dd, dgate, dup, act)
    return out[:3], [out[3], None, out[4]]


def _adamw(w, g, m, v):
    m = ADAM_B1 * m + (1.0 - ADAM_B1) * g
    v = ADAM_B2 * v + (1.0 - ADAM_B2) * (g * g)
    m_hat = m / (1.0 - ADAM_B1 ** ADAM_STEP)
    v_hat = v / (1.0 - ADAM_B2 ** ADAM_STEP)
    return -ADAM_LR * (m_hat / (jnp.sqrt(v_hat) + ADAM_EPS) + ADAM_WD * w), m, v


def _sum_slabs(ref):
    total = ref[0].astype(F32)
    for i in range(1, ref.shape[0]):
        total = total + ref[i].astype(F32)
    return total


def _adamw_rows(r, c):
    tr = r
    for cand in range(8, r, 8):
        if r % cand == 0 and cand * c <= ADAMW_BLOCK_ELEMS:
            tr = cand
    return r if r * c <= ADAMW_BLOCK_ELEMS else tr


def _reduce_adamw_carrying(parts, ws, ms, vs, to_reduce, to_exchange, whole, name):
    k, nr, nx = len(ws), len(to_reduce), len(to_exchange)
    r, c = ws[0].shape if k else (8, 128)
    tr = _adamw_rows(r, c)
    steps = r // tr
    travels = nr + nx > 0
    nd = list(whole).count(False)
    assert list(whole) == [False] * nd + [True] * (nx - nd)
    chip_slabs = [jax.ShapeDtypeStruct((N_CHIP, *a.shape[1:]), a.dtype) for a in to_reduce]

    def body(*refs):
        p_refs, w_refs, m_refs, v_refs = (refs[a * k:(a + 1) * k] for a in range(4))
        refs = refs[4 * k:]
        reduced_in, sent, refs = refs[:nr], refs[nr:nr + nx], refs[nr + nx:]
        outs, pairs, sums, landed, refs = refs[:4 * k], refs[4 * k:4 * k + nr], refs[4 * k + nr:4 * k + 2 * nr], \
            refs[4 * k + 2 * nr:4 * k + 2 * nr + nx], refs[4 * k + 2 * nr + nx:]
        mine_v, pair_v, sum_v, refs = refs[:nr], refs[nr:2 * nr], refs[2 * nr:3 * nr], refs[3 * nr:]
        if travels:
            reduce_ops = _pair_then_chip_ops(reduced_in, pairs, sums, mine_v, pair_v, sum_v, *refs[:7])
            direct_ops = _exchange_ops(sent[:nd], landed[:nd], [False] * nd, *refs[7:10])
            gather_ops = _gather_ops(sent[nd:], landed[nd:], *refs[10:13])

            @pl.when(pl.program_id(0) == 0)
            def _():
                direct_ops[0]()
                gather_ops[0]()
                reduce_ops[0]()

        for a in range(k):
            g = _sum_slabs(p_refs[a])
            outs[4 * a][...] = g
            outs[4 * a + 1][...], outs[4 * a + 2][...], outs[4 * a + 3][...] = _adamw(w_refs[a][...], g, m_refs[a][...], v_refs[a][...])

        if travels:
            @pl.when(pl.program_id(0) == steps - 1)
            def _():
                gather_ops[1]()
                reduce_ops[1]()
                gather_ops[2]()
                direct_ops[1]()

    blk = pl.BlockSpec((tr, c), lambda i: (i, 0))
    out = pl.pallas_call(
        body, name=name, grid=(steps,),
        in_specs=[pl.BlockSpec((N_DEV, tr, c), lambda i: (0, i, 0))] * k + [blk] * (3 * k) + [ANY] * (nr + nx),
        out_specs=[blk] * (4 * k) + [ANY] * (2 * nr + nx),
        out_shape=[jax.ShapeDtypeStruct((r, c), F32)] * (4 * k) + chip_slabs + chip_slabs
        + [jax.ShapeDtypeStruct((N_DEV, *a.shape) if w else a.shape, a.dtype) for a, w in zip(to_exchange, whole)],
        scratch_shapes=([pltpu.VMEM(a.shape, a.dtype) for a in chip_slabs] * 3 + _pair_then_chip_sems(nr) + _exchange_sems(nd)
                        + _exchange_sems(nx - nd) if travels else []),
        compiler_params=_params("arbitrary"),
    )(*parts, *ws, *ms, *vs, *to_reduce, *to_exchange)
    return [tuple(out[4 * a:4 * a + 4]) for a in range(k)], out[4 * k + nr:4 * k + 2 * nr], out[4 * k + 2 * nr:]


def _reduce_adamw_small(parts, ws, ms, vs, loss_parts):
    n = len(parts)

    def body(*refs):
        p_refs, w_refs, m_refs, v_refs = (refs[k * n:(k + 1) * n] for k in range(4))
        outs = refs[4 * n + 1:]
        outs[4 * n][...] = _sum_slabs(refs[4 * n])
        for a in range(n):
            g = _sum_slabs(p_refs[a])
            outs[4 * a][...] = g
            outs[4 * a + 1][...], outs[4 * a + 2][...], outs[4 * a + 3][...] = _adamw(w_refs[a][...], g, m_refs[a][...], v_refs[a][...])

    out = pl.pallas_call(
        body, name="adamw_rest",
        out_shape=[jax.ShapeDtypeStruct(w.shape, F32) for w in ws for _ in range(4)] + [jax.ShapeDtypeStruct(loss_parts.shape[1:], F32)],
        compiler_params=pltpu.CompilerParams(vmem_limit_bytes=VMEM_LIMIT_BYTES),
    )(*parts, *ws, *ms, *vs, loss_parts)
    return [tuple(out[4 * a:4 * a + 4]) for a in range(n)], out[4 * n]


def kernel(x, meta_tokens, norm_mix_pre, w_in, conv_w, pool_w, pool_scale, w_out, norm_mix_post, norm_ffn_pre, w_gate, w_up, w_down, norm_ffn_post, loss_target, m_meta_tokens, m_norm_mix_pre, m_w_in, m_conv_w, m_pool_w, m_pool_scale, m_w_out, m_norm_mix_post, m_norm_ffn_pre, m_w_gate, m_w_up, m_w_down, m_norm_ffn_post, v_meta_tokens, v_norm_mix_pre, v_w_in, v_conv_w, v_pool_w, v_pool_scale, v_w_out, v_norm_mix_post, v_norm_ffn_pre, v_w_gate, v_w_up, v_w_down, v_norm_ffn_post):
    n_seq, seq, d = x.shape
    x2d = x.reshape(n_seq * seq, d)
    target = loss_target.reshape(n_seq * seq, d)

    t_ = lambda a: jnp.swapaxes(a[0], 0, 1)
    pw, ps = pool_w[0], pool_scale

    (h1, z, m, pooled, mixed), (win_b, wout_b, meta, conv, a_meta, z_meta), ffn_slabs = _gather_and_mixer_forward(
        x2d, [w_in[0], w_out[0], meta_tokens, conv_w[0]], [t_(w_gate), t_(w_up), w_down[0]], norm_mix_pre, pw, ps, norm_mix_post, n_seq)
    wg_b, wu_b, wd_b = (s.reshape(-1, d) for s in ffn_slabs)
    f, act, dd, dgate, dup, dh1, loss_sum, dg3, dg4 = _ffn_forward_backward(h1, target, norm_ffn_pre, wg_b, wu_b, wd_b, norm_ffn_post)
    ffn_grads, landing = _ffn_weight_grads(f, dd, dgate, dup, act)
    (gx, dwin, dwout, dg1, dg2, dconv, dpw, dps, dmeta), ffn_parts = _mixer_backward(
        x2d, dh1, m, z, pooled, mixed, meta, a_meta, z_meta, norm_mix_pre, win_b, conv, pw, ps, wout_b, norm_mix_post, n_seq,
        ffn_grads, landing)

    dmeta_s = jnp.transpose(dmeta.reshape(N_META, N_DEV, -1), (1, 0, 2))
    dconv_s = jnp.transpose(dconv.reshape(CONV_WIDTH, N_DEV, -1), (1, 0, 2))
    _, (win_parts, wout_parts), last = _reduce_adamw_carrying(
        [], [], [], [], [dwin, dwout.reshape(N_DEV, -1, d)], [dmeta_s, dconv_s, dg1, dg2, dg3, dg4, dpw.astype(BF16), dps, loss_sum],
        [False] * 2 + [True] * 7, "exchange_rest")
    ffn_res, _, _ = _reduce_adamw_carrying(
        ffn_parts, [t_(w_gate), t_(w_up), w_down[0]], [t_(m_w_gate), t_(m_w_up), m_w_down[0]], [t_(v_w_gate), t_(v_w_up), v_w_down[0]],
        [], [], [], "adamw_ffn")
    replicated = last[2:8]

    names = ["meta_tokens", "norm_mix_pre", "w_in", "conv_w", "pool_w", "pool_scale", "w_out", "norm_mix_post", "norm_ffn_pre", "w_gate",
             "w_up", "w_down", "norm_ffn_post"]
    res = {"w_gate": tuple(jnp.swapaxes(o, 0, 1)[None] for o in ffn_res[0]),
           "w_up": tuple(jnp.swapaxes(o, 0, 1)[None] for o in ffn_res[1]), "w_down": tuple(o[None] for o in ffn_res[2])}
    rest_names = ["w_in", "w_out", "meta_tokens", "conv_w", "norm_mix_pre", "norm_mix_post", "norm_ffn_pre", "norm_ffn_post", "pool_w",
                  "pool_scale"]
    rest_res, loss = _reduce_adamw_small(
        [win_parts, wout_parts, last[0], last[1], *replicated],
        [w_in[0], w_out[0], meta_tokens, conv_w[0], norm_mix_pre, norm_mix_post, norm_ffn_pre, norm_ffn_post, pool_w[0], pool_scale],
        [m_w_in[0], m_w_out[0], m_meta_tokens, m_conv_w[0], m_norm_mix_pre, m_norm_mix_post, m_norm_ffn_pre, m_norm_ffn_post,
         m_pool_w[0], m_pool_scale],
        [v_w_in[0], v_w_out[0], v_meta_tokens, v_conv_w[0], v_norm_mix_pre, v_norm_mix_post, v_norm_ffn_pre, v_norm_ffn_post,
         v_pool_w[0], v_pool_scale], last[8])
    for nm, r in zip(rest_names, rest_res):
        res[nm] = tuple(o[None] for o in r) if nm in ("w_in", "w_out", "conv_w", "pool_w") else r

    return (loss[0, 0], gx.reshape(n_seq, seq, d), *[res[nm][0] for nm in names], *[res[nm][1] for nm in names],
            *[res[nm][2] for nm in names], *[res[nm][3] for nm in names])
```

```python
import jax
import jax.numpy as jnp
from jax import lax
from jax.experimental import pallas as pl
from jax.experimental.pallas import tpu as pltpu

F32, BF16 = jnp.float32, jnp.bfloat16
RMS_EPS = 1e-6
N_META = 16
CONV_WIDTH = 3
POOL_WINDOWS = (2, 4, 8, 16)
POOL_GROUP = 128
HALO = 16
N_DEV = 8
MESH_AXES = ("x", "y", "c")
MESH = pl.DeviceIdType.MESH
VMEM_LIMIT_BYTES = 56 * 1024 * 1024
ADAMW_BLOCK_ELEMS = 192 * 1024
TM_MIX = 512
TM_FFN = 256
FFN_CHUNK = 512
FFN_BACKWARD_LAG = 3
TM_WGRAD = 512
FF_CHUNKS = 2

ADAM_LR, ADAM_B1, ADAM_B2, ADAM_EPS, ADAM_WD, ADAM_STEP = 0.001, 0.9, 0.999, 1e-08, 0.01, 10


def _dot(a, b):
    return jnp.dot(a, b, preferred_element_type=F32)


def _dot_nt(a, b):
    return lax.dot_general(a, b, (((1,), (1,)), ((), ())), preferred_element_type=F32)


def _dot_tn(a, b):
    return lax.dot_general(a, b, (((0,), (0,)), ((), ())), preferred_element_type=F32)


def _rms_stats(h):
    rstd = lax.rsqrt(jnp.mean(h * h, axis=-1, keepdims=True) + RMS_EPS)
    return h * rstd, rstd


def _rms_bwd(hat, rstd, g, dy):
    gdy = dy * g
    proj = jnp.mean(gdy * hat, axis=-1, keepdims=True)
    return rstd * (gdy - hat * proj), jnp.sum(dy * hat, axis=0, keepdims=True)


def _params(*semantics):
    return pltpu.CompilerParams(dimension_semantics=semantics or None, vmem_limit_bytes=VMEM_LIMIT_BYTES)


def _resident(shape):
    zeros = (0,) * len(shape)
    return pl.BlockSpec(shape, lambda *_: zeros, pipeline_mode=pl.Buffered(1))


def _const(shape):
    zeros = (0,) * len(shape)
    return pl.BlockSpec(shape, lambda *_: zeros)


ANY = pl.BlockSpec(memory_space=pl.ANY)


def _my_place():
    x, y, c = (lax.axis_index(a) for a in MESH_AXES)
    return x, y, c


def _exchange_sems(n):
    return [pltpu.SemaphoreType.DMA((n, N_DEV - 1)), pltpu.SemaphoreType.DMA((n, N_DEV - 1)), pltpu.SemaphoreType.DMA((n,))]


def _gather_ops(srcs, outs, send_sems, recv_sems, local_sems, core_major=False):
    n = len(srcs)
    x, y, c = _my_place()
    me, sibling = (x, y, c), (x, y, 1 - c)
    chips = [(1 - x, y), (x, 1 - y), (1 - x, 1 - y)]

    def slab(px, py, pc):
        return 4 * pc + 2 * px + py if core_major else 4 * px + 2 * py + pc

    def copy(a, k, block, to, src=None):
        dst = outs[a].at[slab(*block)]
        return pltpu.make_async_remote_copy(
            src_ref=dst if src is None else src, dst_ref=dst, send_sem=send_sems.at[a, k], recv_sem=recv_sems.at[a, k],
            device_id=to, device_id_type=MESH)

    def mine(a):
        return pltpu.make_async_copy(srcs[a], outs[a].at[slab(*me)], local_sems.at[a])

    def first(a):
        return [copy(a, 0, me, sibling, src=srcs[a])] + [copy(a, 1 + j, me, (*chip, c), src=srcs[a]) for j, chip in enumerate(chips)]

    def passed(a, j):
        return copy(a, 4 + j, (*chips[j], c), sibling)

    def start():
        for a in range(n):
            mine(a).start()
            for cp in first(a):
                cp.start()

    def forward():
        for j, chip in enumerate(chips):
            for a in range(n):
                copy(a, 1 + j, (*chip, c), me).wait_recv()
                passed(a, j).start()

    def finish():
        for a in range(n):
            copy(a, 0, sibling, me).wait_recv()
            for j, chip in enumerate(chips):
                copy(a, 4 + j, (*chip, 1 - c), me).wait_recv()
        for a in range(n):
            for cp in first(a) + [passed(a, j) for j in range(len(chips))]:
                cp.wait_send()
            mine(a).wait()

    return start, forward, finish


def _exchange_ops(ins, outs, whole, send_sems, recv_sems, local_sems):
    n = len(ins)
    x, y, c = _my_place()
    me = 4 * x + 2 * y + c

    def src(a, i):
        return ins[a] if whole[a] else ins[a].at[i]

    def mine(a):
        return pltpu.make_async_copy(src(a, me), outs[a].at[me], local_sems.at[a])

    def send(a, k):
        to = (me + k) % N_DEV
        return pltpu.make_async_remote_copy(
            src_ref=src(a, to), dst_ref=outs[a].at[me], send_sem=send_sems.at[a, k - 1], recv_sem=recv_sems.at[a, k - 1],
            device_id=(to // 4, (to // 2) % 2, to % 2), device_id_type=MESH)

    def landed(a, k):
        frm = (me + N_DEV - k) % N_DEV
        return pltpu.make_async_remote_copy(
            src_ref=src(a, frm), dst_ref=outs[a].at[frm], send_sem=send_sems.at[a, k - 1], recv_sem=recv_sems.at[a, k - 1],
            device_id=(x, y, c), device_id_type=MESH)

    def start():
        for a in range(n):
            mine(a).start()
            for k in range(1, N_DEV):
                send(a, k).start()

    def finish():
        for a in range(n):
            for k in range(1, N_DEV):
                landed(a, k).wait_recv()
        for a in range(n):
            for k in range(1, N_DEV):
                send(a, k).wait_send()
            mine(a).wait()

    return start, finish


def _core_exchange_sems(n):
    return [pltpu.SemaphoreType.DMA((n, 4)), pltpu.SemaphoreType.DMA((n, N_DEV)), pltpu.SemaphoreType.DMA((n,))]


def _core_exchange_ops(ins, outs, to_core, send_sems, recv_sems, local_sems):
    n = len(ins)
    x, y, c = _my_place()
    me = 4 * x + 2 * y + c
    others = [(0, 1), (1, 0), (1, 1)]

    def slab(a, p):
        if len(ins[a].shape) == len(outs[a].shape):
            return ins[a].at[p]
        rows = outs[a].shape[1]
        return ins[a].at[pl.ds(pl.multiple_of(p * rows, 16), rows), :]

    def send(a, dx, dy):
        tx, ty = (x + dx) % 2, (y + dy) % 2
        return pltpu.make_async_remote_copy(
            src_ref=slab(a, 4 * to_core + 2 * tx + ty), dst_ref=outs[a].at[me], send_sem=send_sems.at[a, 2 * dx + dy],
            recv_sem=recv_sems.at[a, 2 * (2 * dx + dy) + c], device_id=(tx, ty, to_core), device_id_type=MESH)

    def mine(a):
        return pltpu.make_async_copy(slab(a, 4 * to_core + 2 * x + y), outs[a].at[me], local_sems.at[a])

    def landed(a, dx, dy, sc):
        frm = 4 * ((x + dx) % 2) + 2 * ((y + dy) % 2) + sc
        return pltpu.make_async_remote_copy(
            src_ref=slab(a, 0), dst_ref=outs[a].at[frm], send_sem=send_sems.at[a, 0], recv_sem=recv_sems.at[a, 2 * (2 * dx + dy) + sc],
            device_id=(x, y, c), device_id_type=MESH)

    def start():
        for a in range(n):
            for dx, dy in others:
                send(a, dx, dy).start()
            pl.when(c == to_core)(mine(a).start)
            pl.when(c != to_core)(send(a, 0, 0).start)

    def finish():
        @pl.when(c == to_core)
        def _():
            for a in range(n):
                for dx, dy in [(0, 0)] + others:
                    for sc in (0, 1):
                        if (dx, dy, sc) != (0, 0, to_core):
                            landed(a, dx, dy, sc).wait_recv()
            for a in range(n):
                mine(a).wait()

        @pl.when(c != to_core)
        def _():
            for a in range(n):
                send(a, 0, 0).wait_send()

        for a in range(n):
            for dx, dy in others:
                send(a, dx, dy).wait_send()

    return start, finish


N_CHIP = 4


def _pair_then_chip_sems(n):
    return [pltpu.SemaphoreType.DMA((n, N_CHIP)) for _ in range(6)] + [pltpu.SemaphoreType.DMA((n,))]


def _pair_then_chip_ops(ins, pairs, outs, mine_v, pair_v, sum_v, pair_send, pair_recv, chip_send, chip_recv, load_a, load_b, own_sem):
    n = len(ins)
    x, y, c = _my_place()
    chip = 2 * x + y
    chips = [(0, 0), (0, 1), (1, 0), (1, 1)]
    others = [(0, 1), (1, 0), (1, 1)]

    def to_sibling(a, j):
        px, py = chips[j]
        return pltpu.make_async_remote_copy(
            src_ref=ins[a].at[4 * px + 2 * py + 1 - c], dst_ref=pairs[a].at[j], send_sem=pair_send.at[a, j], recv_sem=pair_recv.at[a, j],
            device_id=(x, y, 1 - c), device_id_type=MESH)

    def spread(a, dx, dy):
        tx, ty = (x + dx) % 2, (y + dy) % 2
        return pltpu.make_async_remote_copy(
            src_ref=sum_v[a].at[2 * tx + ty], dst_ref=outs[a].at[chip], send_sem=chip_send.at[a, 2 * dx + dy],
            recv_sem=chip_recv.at[a, 2 * dx + dy], device_id=(tx, ty, c), device_id_type=MESH)

    def landed(a, dx, dy):
        frm = 2 * ((x + dx) % 2) + (y + dy) % 2
        return pltpu.make_async_remote_copy(
            src_ref=sum_v[a].at[0], dst_ref=outs[a].at[frm], send_sem=chip_send.at[a, 0], recv_sem=chip_recv.at[a, 2 * dx + dy],
            device_id=(x, y, c), device_id_type=MESH)

    def own(a):
        return pltpu.make_async_copy(sum_v[a].at[chip], outs[a].at[chip], own_sem.at[a])

    def pair():
        loads = []
        for a in range(n):
            for j, (px, py) in enumerate(chips):
                to_sibling(a, j).start()
                loads.append(pltpu.make_async_copy(ins[a].at[4 * px + 2 * py + c], mine_v[a].at[j], load_a.at[a, j]))
                loads[-1].start()
        for a in range(n):
            for j in range(N_CHIP):
                to_sibling(a, j).wait_recv()
                loads.append(pltpu.make_async_copy(pairs[a].at[j], pair_v[a].at[j], load_b.at[a, j]))
                loads[-1].start()
        for cp in loads:
            cp.wait()
        for a in range(n):
            sum_v[a][...] = (mine_v[a][...].astype(F32) + pair_v[a][...].astype(F32)).astype(sum_v[a].dtype)

    def start():
        pair()
        for a in range(n):
            own(a).start()
            for dx, dy in others:
                spread(a, dx, dy).start()

    def finish():
        for a in range(n):
            for dx, dy in others:
                landed(a, dx, dy).wait_recv()
        for a in range(n):
            for dx, dy in others:
                spread(a, dx, dy).wait_send()
            for j in range(N_CHIP):
                to_sibling(a, j).wait_send()
            own(a).wait()

    return start, finish


def _window_sum(x, win, ahead):
    n = x.shape[0]
    span = 1
    while span < win:
        x = x + pltpu.roll(x, n - span if ahead else span, 0)
        span *= 2
    return x


def _conv_branch(z, ext_u, conv_ref, tm):
    c_w = z.shape[1] // 4
    b, c, v = z[:, :c_w], z[:, c_w:2 * c_w], z[:, 2 * c_w:3 * c_w]
    u = c * v
    ext_u[pl.ds(HALO, tm), :] = u
    u1 = ext_u[pl.ds(HALO - 1, tm), :]
    u2 = ext_u[pl.ds(HALO - 2, tm), :]
    yc = conv_ref[pl.ds(2, 1), :] * u + conv_ref[pl.ds(1, 1), :] * u1 + conv_ref[pl.ds(0, 1), :] * u2
    return b, c, v, u, u1, u2, yc


def _pool_branch(p, ext_p, pool_w_ref, tm):
    ext_p[pl.ds(HALO, tm), :] = p
    pooled, mixed = [], []
    for g, win in enumerate(POOL_WINDOWS):
        s = _window_sum(ext_p[:, pl.ds(POOL_GROUP * g, POOL_GROUP)], win, ahead=False)[HALO:HALO + tm, :]
        pooled.append((s * (1.0 / win) - p[:, POOL_GROUP * g:POOL_GROUP * (g + 1)]).astype(BF16))
        mixed.append(_dot(pooled[-1], pool_w_ref[g].astype(BF16)))
    return pooled, mixed


def _gather_and_mixer_forward(x2d, mixer_shards, ffn_shards, g1, pool_w, pool_scale, g2, n_seq):
    t, d = x2d.shape
    zs, rs, ms, cs = mixer_shards[0].shape[1], mixer_shards[1].shape[0], mixer_shards[2].shape[1], mixer_shards[3].shape[1]
    zw, cw = N_DEV * zs, N_DEV * cs
    s = t // n_seq
    tm = min(TM_MIX, s)
    nj = s // tm
    n1, n2 = len(mixer_shards), len(ffn_shards)
    dtypes = [BF16, BF16, F32, F32] + [BF16] * n2
    shards = list(mixer_shards) + list(ffn_shards)

    def body(x_ref, *rest):
        shard_refs, (g1_ref, pw_ref, ps_ref, g2_ref), rest = rest[:n1 + n2], rest[n1 + n2:n1 + n2 + 4], rest[n1 + n2 + 4:]
        (h1_ref, z_ref, m_ref, pooled_ref, mixed_ref, win_o, wout_o, meta_o, conv_o, am_o, zm_o), rest = rest[:11], rest[11:]
        slabs, rest = rest[:n1 + n2], rest[n1 + n2:]
        stages, rest = rest[:n1 + n2], rest[n1 + n2:]
        win_v, wout_v, meta_v, conv_v, ext_u, ext_p, sem = rest[:7]
        first = _gather_ops(stages[:n1], slabs[:n1], *rest[7:10])
        later = _gather_ops(stages[n1:], slabs[n1:], *rest[10:13], core_major=True)

        @pl.when((pl.program_id(0) == 0) & (pl.program_id(1) == 0))
        def _():
            for src, dst in zip(shard_refs, stages):
                dst[...] = src[...].astype(dst.dtype)
            first[0]()
            later[0]()
            first[1]()
            first[2]()
            copies = [pltpu.make_async_copy(slabs[0].at[i], win_v.at[:, pl.ds(zs * i, zs)], sem.at[i]) for i in range(N_DEV)]
            copies += [pltpu.make_async_copy(slabs[1].at[i], wout_v.at[pl.ds(rs * i, rs), :], sem.at[N_DEV + i]) for i in range(N_DEV)]
            copies += [pltpu.make_async_copy(slabs[2], meta_v, sem.at[2 * N_DEV]), pltpu.make_async_copy(slabs[3], conv_v, sem.at[2 * N_DEV + 1])]
            for cp in copies:
                cp.start()
            for cp in copies:
                cp.wait()
            copies = [pltpu.make_async_copy(win_v, win_o, sem.at[0]), pltpu.make_async_copy(wout_v, wout_o, sem.at[1])]
            for cp in copies:
                cp.start()
            for i in range(N_DEV):
                meta_o[:, pl.ds(ms * i, ms)] = meta_v[i]
                conv_o[:, pl.ds(cs * i, cs)] = conv_v[i]
            hat, _ = _rms_stats(meta_o[...])
            a = (hat * g1_ref[...]).astype(BF16)
            am_o[...] = a
            zm_o[...] = _dot(a, win_v[...])
            for cp in copies:
                cp.wait()

        @pl.when(pl.program_id(1) == 0)
        def _():
            zm = zm_o[...]
            ext_u[pl.ds(0, HALO), :] = zm[:, cw:2 * cw] * zm[:, 2 * cw:3 * cw]
            ext_p[pl.ds(0, HALO), :] = zm[:, 3 * cw:]

        h0 = x_ref[...]
        hat, _ = _rms_stats(h0)
        z = _dot((hat * g1_ref[...]).astype(BF16), win_v[...])
        z_ref[...] = z.astype(BF16)
        b, _, _, _, _, _, yc = _conv_branch(z, ext_u, conv_o, tm)
        pooled, mixed = _pool_branch(z[:, 3 * cw:], ext_p, pw_ref, tm)
        pooled_ref[...] = jnp.concatenate(pooled, axis=1)
        mixed_ref[...] = jnp.concatenate(mixed, axis=1).astype(BF16)
        ps = ps_ref[...]
        y = [b * yc] + [mixed[g] * ps[:, POOL_GROUP * g:POOL_GROUP * (g + 1)] for g in range(len(POOL_WINDOWS))]
        m = _dot(jnp.concatenate(y, axis=1).astype(BF16), wout_v[...])
        m_ref[...] = m
        m_hat, _ = _rms_stats(m)
        h1_ref[...] = h0 + m_hat * g2_ref[...]
        ext_u[pl.ds(0, HALO), :] = ext_u[pl.ds(tm, HALO), :]
        ext_p[pl.ds(0, HALO), :] = ext_p[pl.ds(tm, HALO), :]

        @pl.when((pl.program_id(0) == n_seq - 1) & (pl.program_id(1) == nj - 1))
        def _():
            later[1]()
            later[2]()

    row = lambda b, j: (b * nj + j, 0)
    vmem = pl.BlockSpec(memory_space=pltpu.VMEM)
    small = [(N_META, d), (CONV_WIDTH, cw), (N_META, d), (N_META, zw)]
    out = pl.pallas_call(
        body, name="gather_and_mixer_forward", grid=(n_seq, nj),
        in_specs=[pl.BlockSpec((tm, d), row)] + [vmem] * (n1 + n2)
        + [_const(g1.shape), _const(pool_w.shape), _const(pool_scale.shape), _const(g2.shape)],
        out_specs=[pl.BlockSpec((tm, d), row), pl.BlockSpec((tm, zw), row), pl.BlockSpec((tm, d), row), pl.BlockSpec((tm, cw), row),
                   pl.BlockSpec((tm, cw), row), ANY, ANY] + [_const(sh) for sh in small] + [ANY] * (n1 + n2),
        out_shape=[jax.ShapeDtypeStruct((t, d), F32), jax.ShapeDtypeStruct((t, zw), BF16), jax.ShapeDtypeStruct((t, d), F32),
                   jax.ShapeDtypeStruct((t, cw), BF16), jax.ShapeDtypeStruct((t, cw), BF16),
                   jax.ShapeDtypeStruct((d, zw), BF16), jax.ShapeDtypeStruct((d, d), BF16),
                   jax.ShapeDtypeStruct(small[0], F32), jax.ShapeDtypeStruct(small[1], F32), jax.ShapeDtypeStruct(small[2], BF16),
                   jax.ShapeDtypeStruct(small[3], F32)]
        + [jax.ShapeDtypeStruct((N_DEV, *a.shape), dt) for a, dt in zip(shards, dtypes)],
        scratch_shapes=[pltpu.VMEM(a.shape, dt) for a, dt in zip(shards, dtypes)]
        + [pltpu.VMEM((d, zw), BF16), pltpu.VMEM((d, d), BF16), pltpu.VMEM((N_DEV, N_META, ms), F32),
           pltpu.VMEM((N_DEV, CONV_WIDTH, cs), F32), pltpu.VMEM((tm + HALO, cw), F32), pltpu.VMEM((tm + HALO, cw), F32),
           pltpu.SemaphoreType.DMA((2 * N_DEV + 2,))] + _exchange_sems(n1) + _exchange_sems(n2),
        compiler_params=_params("arbitrary", "arbitrary"),
    )(x2d, *shards, g1, pool_w, pool_scale, g2)
    return out[:5], out[5:11], out[11 + n1:]


def _mixer_backward(x2d, dh1, m, z, pooled, mixed, meta, a_meta, z_meta, g1, w_in, conv_w, pool_w, pool_scale, w_out, g2, n_seq,
                    to_exchange, landing):
    t, d = x2d.shape
    zw = w_in.shape[1]
    cw = zw // 4
    s = t // n_seq
    tm = min(TM_MIX, s)
    nj = s // tm
    n_groups = len(POOL_WINDOWS)
    zs = zw // N_DEV
    nx = len(to_exchange)
    n_in = 17
    given = [k for k, a in enumerate(landing) if a is not None]
    fresh = [k for k, a in enumerate(landing) if a is None]

    def body(x_ref, dh1_ref, m_ref, z_ref, zprev_ref, pooled_ref, mixed_ref, meta_ref, am_ref, zm_ref, g1_ref, win_ref, conv_ref, pw_ref, ps_ref, wout_ref,
             g2_ref, *rest):
        sent, rest = rest[:nx], rest[nx + len(given):]
        gx_ref, dwin_ref, dwout_ref, dg1_ref, dg2_ref, dconv_ref, dpw_ref, dps_ref, dmeta_ref = rest[:9]
        landed, rest = rest[9:9 + nx], rest[9 + nx:]
        ext_u, ext_dyc, ext_dq, acc_win, acc_wout, dz_meta, stage16, sem = rest[:8]
        north = _core_exchange_ops(sent, landed, 1, *rest[8:11])
        south = _core_exchange_ops([sent[k] for k in fresh], [landed[k] for k in fresh], 0, *rest[11:14])

        def start():
            north[0]()
            south[0]()

        def finish():
            south[1]()
            north[1]()

        b_id, j = pl.program_id(0), pl.program_id(1)
        jr = nj - 1 - j
        pl.when((b_id == 0) & (j == 0))(start)

        @pl.when((b_id == 0) & (j == 0))
        def _():
            acc_win[...] = jnp.zeros_like(acc_win)
            acc_wout[...] = jnp.zeros_like(acc_wout)
            dz_meta[...] = jnp.zeros_like(dz_meta)
            for r in (dg1_ref, dg2_ref, dconv_ref, dpw_ref, dps_ref, dmeta_ref):
                r[...] = jnp.zeros_like(r)

        @pl.when(j == 0)
        def _():
            ext_dyc[pl.ds(tm, HALO), :] = jnp.zeros((HALO, cw), F32)
            ext_dq[pl.ds(tm, HALO), :] = jnp.zeros((HALO, cw), F32)

        zm = zm_ref[...]
        halo = jnp.where(jr == 0, zm, zprev_ref[...].astype(F32))
        ext_u[pl.ds(0, HALO), :] = halo[:, cw:2 * cw] * halo[:, 2 * cw:3 * cw]

        dh1v = dh1_ref[...]
        m_hat, m_rstd = _rms_stats(m_ref[...])
        dm, dg2 = _rms_bwd(m_hat, m_rstd, g2_ref[...], dh1v)
        dg2_ref[...] += dg2
        dm = dm.astype(BF16)
        dycat = _dot_nt(dm, wout_ref[...])

        b, c, v, u, u1, u2, yc = _conv_branch(z_ref[...].astype(F32), ext_u, conv_ref, tm)
        mixed = [mixed_ref[:, pl.ds(POOL_GROUP * g, POOL_GROUP)].astype(F32) for g in range(n_groups)]
        ps = ps_ref[...]
        y = [b * yc] + [mixed[g] * ps[:, POOL_GROUP * g:POOL_GROUP * (g + 1)] for g in range(n_groups)]
        ycat = jnp.concatenate(y, axis=1).astype(BF16)
        acc_wout[...] += _dot_tn(ycat, dm)

        dyconv = dycat[:, :cw]
        db = dyconv * yc
        dyc = dyconv * b
        ext_dyc[pl.ds(0, tm), :] = dyc
        du = (conv_ref[pl.ds(2, 1), :] * dyc + conv_ref[pl.ds(1, 1), :] * ext_dyc[pl.ds(1, tm), :]
              + conv_ref[pl.ds(0, 1), :] * ext_dyc[pl.ds(2, tm), :])
        dconv_ref[pl.ds(2, 1), :] += jnp.sum(dyc * u, axis=0, keepdims=True)
        dconv_ref[pl.ds(1, 1), :] += jnp.sum(dyc * u1, axis=0, keepdims=True)
        dconv_ref[pl.ds(0, 1), :] += jnp.sum(dyc * u2, axis=0, keepdims=True)

        dp = []
        for g, win in enumerate(POOL_WINDOWS):
            lanes = pl.ds(POOL_GROUP * g, POOL_GROUP)
            dypool = dycat[:, cw + POOL_GROUP * g:cw + POOL_GROUP * (g + 1)]
            dps_ref[:, lanes] += jnp.sum(dypool * mixed[g], axis=0, keepdims=True)
            dmixed = (dypool * ps[:, POOL_GROUP * g:POOL_GROUP * (g + 1)]).astype(BF16)
            dq = _dot_nt(dmixed, pw_ref[g].astype(BF16))
            dpw_ref[g] += _dot_tn(pooled_ref[:, lanes], dmixed)
            ext_dq[pl.ds(0, tm), lanes] = dq
            acc = _window_sum(ext_dq[:, lanes], win, ahead=True)[0:tm, :]
            dp.append(acc * (1.0 / win) - dq)

        dz = jnp.concatenate([db, du * v, du * c] + dp, axis=1).astype(BF16)
        da = _dot_nt(dz, win_ref[...])
        h0 = x_ref[...]
        hat0, rstd0 = _rms_stats(h0)
        g1 = g1_ref[...]
        acc_win[...] += _dot_tn((hat0 * g1).astype(BF16), dz)
        dh0, dg1 = _rms_bwd(hat0, rstd0, g1, da)
        dg1_ref[...] += dg1
        gx_ref[...] = dh1v + dh0

        ext_dyc[pl.ds(tm, HALO), :] = ext_dyc[pl.ds(0, HALO), :]
        ext_dq[pl.ds(tm, HALO), :] = ext_dq[pl.ds(0, HALO), :]

        @pl.when(jr == 0)
        def _():
            ext_dyc[pl.ds(tm - HALO, HALO), :] = jnp.zeros((HALO, cw), F32)
            ext_dq[pl.ds(tm - HALO, HALO), :] = jnp.zeros((HALO, cw), F32)
            du_m = (conv_ref[pl.ds(1, 1), :] * ext_dyc[pl.ds(tm - HALO + 1, HALO), :]
                    + conv_ref[pl.ds(0, 1), :] * ext_dyc[pl.ds(tm - HALO + 2, HALO), :])
            dp_m = []
            for g, win in enumerate(POOL_WINDOWS):
                lanes = pl.ds(POOL_GROUP * g, POOL_GROUP)
                acc = ext_dq[pl.ds(tm - HALO + 1, HALO), lanes]
                for k in range(2, win):
                    acc = acc + ext_dq[pl.ds(tm - HALO + k, HALO), lanes]
                dp_m.append(acc * (1.0 / win))
            dz_meta[...] += jnp.concatenate(
                [jnp.zeros((HALO, cw), F32), du_m * zm[:, 2 * cw:3 * cw], du_m * zm[:, cw:2 * cw]] + dp_m, axis=1)

        @pl.when((b_id == n_seq - 1) & (j == nj - 1))
        def _():
            dz_m = dz_meta[...].astype(BF16)
            acc_win[...] += _dot_tn(am_ref[...], dz_m)
            hat_m, rstd_m = _rms_stats(meta_ref[...])
            dmeta, dg1_m = _rms_bwd(hat_m, rstd_m, g1, _dot_nt(dz_m, win_ref[...]))
            dg1_ref[...] += dg1_m
            dmeta_ref[...] = dmeta
            pieces = [(acc_win, zs * i, dwin_ref.at[i]) for i in range(N_DEV)]
            pieces += [(acc_wout, zs * i, dwout_ref.at[:, pl.ds(zs * i, zs)]) for i in range(d // zs)]
            copies = []
            for k, (acc, col, dst) in enumerate(pieces):
                if k >= 2:
                    copies[k - 2].wait()
                stage16[k % 2] = acc[:, pl.ds(col, zs)].astype(BF16)
                copies.append(pltpu.make_async_copy(stage16.at[k % 2], dst, sem.at[k % 2]))
                copies[k].start()
            copies[-2].wait()
            copies[-1].wait()
            finish()

    row = lambda b, j: (b * nj + nj - 1 - j, 0)
    prev = lambda b, j: (jnp.maximum((b * s + (nj - 1 - j) * tm) // HALO - 1, 0), 0)
    small = [g1.shape, g2.shape, conv_w.shape, pool_w.shape, pool_scale.shape, meta.shape]
    out = pl.pallas_call(
        body, name="mixer_backward", grid=(n_seq, nj),
        in_specs=[pl.BlockSpec((tm, d), row), pl.BlockSpec((tm, d), row), pl.BlockSpec((tm, d), row), pl.BlockSpec((tm, zw), row),
                  pl.BlockSpec((HALO, zw), prev), pl.BlockSpec((tm, cw), row), pl.BlockSpec((tm, cw), row), _const(meta.shape), _const(a_meta.shape), _const(z_meta.shape), _const(g1.shape),
                  _resident(w_in.shape), _const(conv_w.shape), _const(pool_w.shape), _const(pool_scale.shape), _resident(w_out.shape),
                  _const(g2.shape)] + [ANY] * (nx + len(given)),
        out_specs=[pl.BlockSpec((tm, d), row), ANY, ANY] + [_const(sh) for sh in small] + [ANY] * nx,
        out_shape=[jax.ShapeDtypeStruct((t, d), F32), jax.ShapeDtypeStruct((N_DEV, d, zs), BF16),
                   jax.ShapeDtypeStruct(w_out.shape, BF16)] + [jax.ShapeDtypeStruct(sh, F32) for sh in small]
        + [jax.ShapeDtypeStruct((N_DEV, a.shape[0] // N_DEV, a.shape[1]), a.dtype) for a in to_exchange],
        input_output_aliases={n_in + nx + at: 9 + k for at, k in enumerate(given)},
        scratch_shapes=[pltpu.VMEM((tm + HALO, cw), F32)] * 3
        + [pltpu.VMEM(w_in.shape, F32), pltpu.VMEM(w_out.shape, F32), pltpu.VMEM((HALO, zw), F32), pltpu.VMEM((2, d, zs), BF16),
           pltpu.SemaphoreType.DMA((2,))] + _core_exchange_sems(nx) + _core_exchange_sems(len(fresh)),
        compiler_params=_params("arbitrary", "arbitrary"),
    )(x2d, dh1, m, z, z, pooled, mixed, meta, a_meta, z_meta, g1, w_in, conv_w, pool_w, pool_scale, w_out, g2, *to_exchange, *[landing[k] for k in given])
    return out[:9], out[9:]


def _ffn_forward_backward(h1, target, g3, w_gate, w_up, w_down, g4):
    t, d = h1.shape
    ff = w_gate.shape[0]
    tm = min(TM_FFN, t)
    nt = t // tm
    chunks = [(s, min(FFN_CHUNK, ff - s)) for s in range(0, ff, FFN_CHUNK)]

    def body(h1_ref, h1pp_ref, tgt_ref, g3_ref, wg_hbm, wu_hbm, wd_ref, g4_ref,
             f_ref, act_ref, dd_ref, dgate_ref, dup_ref, dh1_ref, loss_ref, dg3_ref, dg4_ref, *slots):
        gate_s, up_s, dd_s, dh2_s, df_s, wgu, wsem = slots
        i = pl.program_id(0)

        def gu(s, n):
            return wgu.at[pl.ds(2 * s, 2 * n), :]

        @pl.when(i == 0)
        def _():
            copies = []
            for k, (s, n) in enumerate(chunks):
                copies += [pltpu.make_async_copy(wg_hbm.at[pl.ds(s, n), :], wgu.at[pl.ds(2 * s, n), :], wsem.at[2 * k]),
                           pltpu.make_async_copy(wu_hbm.at[pl.ds(s, n), :], wgu.at[pl.ds(2 * s + n, n), :], wsem.at[2 * k + 1])]
            for k, cp in enumerate(copies):
                cp.start(priority=k % 2)
            for r in (loss_ref, dg3_ref, dg4_ref, gate_s, up_s, dd_s, dh2_s, df_s):
                r[...] = jnp.zeros_like(r)
            for cp in copies:
                cp.wait()

        def forward(slot):
            h1v = h1_ref[...]
            hat, _ = _rms_stats(h1v)
            f = (hat * g3_ref[...]).astype(BF16)
            f_ref[...] = f
            s, n = chunks[0]
            both = _dot_nt(f_ref[...], gu(s, n)[...])
            yield
            down = None
            for k, (s, n) in enumerate(chunks):
                gate, up = both[:, :n], both[:, n:]
                gate_s.at[slot][:, pl.ds(s, n)] = gate.astype(BF16)
                up_s.at[slot][:, pl.ds(s, n)] = up.astype(BF16)
                act = (gate * jax.nn.sigmoid(gate) * up).astype(BF16)
                act_ref[:, pl.ds(s, n)] = act
                if k + 1 < len(chunks):
                    s1, n1 = chunks[k + 1]
                    both = _dot_nt(f_ref[...], gu(s1, n1)[...])
                yield
                part = _dot(act_ref[:, pl.ds(s, n)], wd_ref[pl.ds(s, n), :])
                down = part if down is None else down + part
                yield
            d_hat, d_rstd = _rms_stats(down)
            g4 = g4_ref[...]
            err = h1v + d_hat * g4 - tgt_ref[...]
            loss_ref[...] += jnp.sum(err * err) * (0.5 / d)
            dh2 = err * (1.0 / d)
            dh2_s.at[slot][...] = dh2
            dd, dg4 = _rms_bwd(d_hat, d_rstd, g4, dh2)
            dg4_ref[...] += dg4
            dd = dd.astype(BF16)
            dd_ref[...] = dd
            dd_s.at[slot][...] = dd

        def backward(slot):
            s, n = chunks[0]
            dact = _dot_nt(dd_s.at[slot][...], wd_ref[pl.ds(s, n), :])
            yield
            df = None
            for k, (s, n) in enumerate(chunks):
                gate = gate_s.at[slot][:, pl.ds(s, n)].astype(F32)
                up = up_s.at[slot][:, pl.ds(s, n)].astype(F32)
                sig = jax.nn.sigmoid(gate)
                dup = (dact * (gate * sig)).astype(BF16)
                dgate = (dact * up * (sig * (1.0 + gate * (1.0 - sig)))).astype(BF16)
                dup_ref[:, pl.ds(s, n)] = dup
                dgate_ref[:, pl.ds(s, n)] = dgate
                if k + 1 < len(chunks):
                    s1, n1 = chunks[k + 1]
                    dact = _dot_nt(dd_s.at[slot][...], wd_ref[pl.ds(s1, n1), :])
                yield
                part = _dot(jnp.concatenate([dgate_ref[:, pl.ds(s, n)], dup_ref[:, pl.ds(s, n)]], axis=1), gu(s, n)[...])
                df = part if df is None else df + part
                yield
            df_s.at[slot][...] = df

        def last(slot):
            hat, rstd = _rms_stats(h1pp_ref[...])
            dh1, dg3 = _rms_bwd(hat, rstd, g3_ref[...], df_s.at[slot][...])
            dg3_ref[...] += dg3
            dh1_ref[...] = dh2_s.at[slot][...] + dh1

        def emit(parity, with_forward, with_backward, with_last):
            fwd = forward(parity) if with_forward else iter(())
            bwd = backward(1 - parity) if with_backward else iter(())
            next(fwd, None)
            if with_last:
                last(parity)
            for _ in range(FFN_BACKWARD_LAG):
                next(fwd, None)
            alive = True
            while alive:
                alive = next(bwd, True) is None
                alive = (next(fwd, True) is None) or alive

        @pl.when(i < nt)
        def _():
            emit(i % 2, True, True, True)

        @pl.when(i == nt)
        def _():
            emit(nt % 2, False, True, True)

        @pl.when(i == nt + 1)
        def _():
            emit((nt + 1) % 2, False, False, True)

    cur = lambda i: (jnp.minimum(i, nt - 1), 0)
    prev = lambda i: (jnp.clip(i - 1, 0, nt - 1), 0)
    prev2 = lambda i: (jnp.clip(i - 2, 0, nt - 1), 0)
    return pl.pallas_call(
        body, name="ffn_forward_backward", grid=(nt + 2,),
        in_specs=[pl.BlockSpec((tm, d), cur), pl.BlockSpec((tm, d), prev2), pl.BlockSpec((tm, d), cur), _const(g3.shape),
                  ANY, ANY, _resident(w_down.shape), _const(g4.shape)],
        out_specs=[pl.BlockSpec((tm, d), cur), pl.BlockSpec((tm, ff), cur), pl.BlockSpec((tm, d), cur), pl.BlockSpec((tm, ff), prev),
                   pl.BlockSpec((tm, ff), prev), pl.BlockSpec((tm, d), prev2), _const((8, 128)), _const(g3.shape), _const(g4.shape)],
        out_shape=[jax.ShapeDtypeStruct((t, d), BF16), jax.ShapeDtypeStruct((t, ff), BF16), jax.ShapeDtypeStruct((t, d), BF16),
                   jax.ShapeDtypeStruct((t, ff), BF16), jax.ShapeDtypeStruct((t, ff), BF16), jax.ShapeDtypeStruct((t, d), F32),
                   jax.ShapeDtypeStruct((8, 128), F32), jax.ShapeDtypeStruct(g3.shape, F32), jax.ShapeDtypeStruct(g4.shape, F32)],
        scratch_shapes=[pltpu.VMEM((2, tm, ff), BF16)] * 2 + [pltpu.VMEM((2, tm, d), BF16)] + [pltpu.VMEM((2, tm, d), F32)] * 2
        + [pltpu.VMEM((2 * ff, d), BF16), pltpu.SemaphoreType.DMA((2 * len(chunks),))],
        compiler_params=_params("arbitrary"),
    )(h1, h1, target, g3, w_gate, w_up, w_down, g4)


def _ffn_weight_grads(f, dd, dgate, dup, act):
    t, d = f.shape
    ff = dgate.shape[1]
    tm = min(TM_WGRAD, t)
    nt = t // tm
    fc = ff // FF_CHUNKS
    assert FF_CHUNKS == 2

    def body(f_ref, dd_ref, dgate_ref, dup_ref, act_ref, dwg_ref, dwu_ref, dwd_ref, *rest):
        landing, (acc_g, acc_u, acc_d, stage, sem) = rest[:2], rest[2:7]
        start, finish = _core_exchange_ops([dwg_ref, dwd_ref], landing, 0, *rest[7:])
        c, i = pl.program_id(0), pl.program_id(1)
        pl.when((c == 1) & (i == 0))(start)

        @pl.when(i == 0)
        def _():
            acc_g[...] = jnp.zeros_like(acc_g)
            acc_u[...] = jnp.zeros_like(acc_u)
            acc_d[...] = jnp.zeros_like(acc_d)

        fv = f_ref[...]
        acc_g[...] += _dot_tn(fv, dgate_ref[...])
        acc_u[...] += _dot_tn(fv, dup_ref[...])
        acc_d[...] += _dot_tn(act_ref[...], dd_ref[...])

        @pl.when(i == nt - 1)
        def _():
            rows = pl.ds(pl.multiple_of(c * fc, 16), fc)
            copies = []
            for k, (acc, out, transposed) in enumerate(((acc_d, dwd_ref, False), (acc_g, dwg_ref, True), (acc_u, dwu_ref, True))):
                if k >= 2:
                    copies[k - 2].wait()
                stage[k % 2] = (acc[...].T if transposed else acc[...]).astype(BF16)
                copies.append(pltpu.make_async_copy(stage.at[k % 2], out.at[rows, :], sem.at[k % 2]))
                copies[k].start()
            copies[-2].wait()
            copies[-1].wait()

        pl.when((c == 1) & (i == nt - 1))(finish)

    row = lambda c, i: (i, 0)
    col = lambda c, i: (i, c)
    out = pl.pallas_call(
        body, name="ffn_weight_grads", grid=(FF_CHUNKS, nt),
        in_specs=[pl.BlockSpec((tm, d), row), pl.BlockSpec((tm, d), row), pl.BlockSpec((tm, fc), col), pl.BlockSpec((tm, fc), col),
                  pl.BlockSpec((tm, fc), col)],
        out_specs=[ANY] * 5,
        out_shape=[jax.ShapeDtypeStruct((ff, d), BF16)] * 3 + [jax.ShapeDtypeStruct((N_DEV, ff // N_DEV, d), BF16)] * 2,
        scratch_shapes=[pltpu.VMEM((d, fc), F32), pltpu.VMEM((d, fc), F32), pltpu.VMEM((fc, d), F32), pltpu.VMEM((2, fc, d), BF16),
                        pltpu.SemaphoreType.DMA((2,))] + _core_exchange_sems(2),
        compiler_params=_params("arbitrary", "arbitrary"),
    )(f, dd, dgate, dup, act)
    return out[:3], [out[3], None, out[4]]


def _adamw(w, g, m, v):
    m = ADAM_B1 * m + (1.0 - ADAM_B1) * g
    v = ADAM_B2 * v + (1.0 - ADAM_B2) * (g * g)
    m_hat = m / (1.0 - ADAM_B1 ** ADAM_STEP)
    v_hat = v / (1.0 - ADAM_B2 ** ADAM_STEP)
    return -ADAM_LR * (m_hat / (jnp.sqrt(v_hat) + ADAM_EPS) + ADAM_WD * w), m, v


def _sum_slabs(ref):
    total = ref[0].astype(F32)
    for i in range(1, ref.shape[0]):
        total = total + ref[i].astype(F32)
    return total


def _adamw_rows(r, c):
    tr = r
    for cand in range(8, r, 8):
        if r % cand == 0 and cand * c <= ADAMW_BLOCK_ELEMS:
            tr = cand
    return r if r * c <= ADAMW_BLOCK_ELEMS else tr


def _reduce_adamw_carrying(parts, ws, ms, vs, to_reduce, to_exchange, whole, name):
    k, nr, nx = len(ws), len(to_reduce), len(to_exchange)
    r, c = ws[0].shape if k else (8, 128)
    tr = _adamw_rows(r, c)
    steps = r // tr
    travels = nr + nx > 0
    nd = list(whole).count(False)
    assert list(whole) == [False] * nd + [True] * (nx - nd)
    chip_slabs = [jax.ShapeDtypeStruct((N_CHIP, *a.shape[1:]), a.dtype) for a in to_reduce]

    def body(*refs):
        p_refs, w_refs, m_refs, v_refs = (refs[a * k:(a + 1) * k] for a in range(4))
        refs = refs[4 * k:]
        reduced_in, sent, refs = refs[:nr], refs[nr:nr + nx], refs[nr + nx:]
        outs, pairs, sums, landed, refs = refs[:4 * k], refs[4 * k:4 * k + nr], refs[4 * k + nr:4 * k + 2 * nr], \
            refs[4 * k + 2 * nr:4 * k + 2 * nr + nx], refs[4 * k + 2 * nr + nx:]
        mine_v, pair_v, sum_v, refs = refs[:nr], refs[nr:2 * nr], refs[2 * nr:3 * nr], refs[3 * nr:]
        if travels:
            reduce_ops = _pair_then_chip_ops(reduced_in, pairs, sums, mine_v, pair_v, sum_v, *refs[:7])
            direct_ops = _exchange_ops(sent[:nd], landed[:nd], [False] * nd, *refs[7:10])
            gather_ops = _gather_ops(sent[nd:], landed[nd:], *refs[10:13])

            @pl.when(pl.program_id(0) == 0)
            def _():
                direct_ops[0]()
                gather_ops[0]()
                reduce_ops[0]()

        for a in range(k):
            g = _sum_slabs(p_refs[a])
            outs[4 * a][...] = g
            outs[4 * a + 1][...], outs[4 * a + 2][...], outs[4 * a + 3][...] = _adamw(w_refs[a][...], g, m_refs[a][...], v_refs[a][...])

        if travels:
            @pl.when(pl.program_id(0) == steps - 1)
            def _():
                gather_ops[1]()
                reduce_ops[1]()
                gather_ops[2]()
                direct_ops[1]()

    blk = pl.BlockSpec((tr, c), lambda i: (i, 0))
    out = pl.pallas_call(
        body, name=name, grid=(steps,),
        in_specs=[pl.BlockSpec((N_DEV, tr, c), lambda i: (0, i, 0))] * k + [blk] * (3 * k) + [ANY] * (nr + nx),
        out_specs=[blk] * (4 * k) + [ANY] * (2 * nr + nx),
        out_shape=[jax.ShapeDtypeStruct((r, c), F32)] * (4 * k) + chip_slabs + chip_slabs
        + [jax.ShapeDtypeStruct((N_DEV, *a.shape) if w else a.shape, a.dtype) for a, w in zip(to_exchange, whole)],
        scratch_shapes=([pltpu.VMEM(a.shape, a.dtype) for a in chip_slabs] * 3 + _pair_then_chip_sems(nr) + _exchange_sems(nd)
                        + _exchange_sems(nx - nd) if travels else []),
        compiler_params=_params("arbitrary"),
    )(*parts, *ws, *ms, *vs, *to_reduce, *to_exchange)
    return [tuple(out[4 * a:4 * a + 4]) for a in range(k)], out[4 * k + nr:4 * k + 2 * nr], out[4 * k + 2 * nr:]


def _reduce_adamw_small(parts, ws, ms, vs, loss_parts):
    n = len(parts)

    def body(*refs):
        p_refs, w_refs, m_refs, v_refs = (refs[k * n:(k + 1) * n] for k in range(4))
        outs = refs[4 * n + 1:]
        outs[4 * n][...] = _sum_slabs(refs[4 * n])
        for a in range(n):
            g = _sum_slabs(p_refs[a])
            outs[4 * a][...] = g
            outs[4 * a + 1][...], outs[4 * a + 2][...], outs[4 * a + 3][...] = _adamw(w_refs[a][...], g, m_refs[a][...], v_refs[a][...])

    out = pl.pallas_call(
        body, name="adamw_rest",
        out_shape=[jax.ShapeDtypeStruct(w.shape, F32) for w in ws for _ in range(4)] + [jax.ShapeDtypeStruct(loss_parts.shape[1:], F32)],
        compiler_params=pltpu.CompilerParams(vmem_limit_bytes=VMEM_LIMIT_BYTES),
    )(*parts, *ws, *ms, *vs, loss_parts)
    return [tuple(out[4 * a:4 * a + 4]) for a in range(n)], out[4 * n]


def kernel(x, meta_tokens, norm_mix_pre, w_in, conv_w, pool_w, pool_scale, w_out, norm_mix_post, norm_ffn_pre, w_gate, w_up, w_down, norm_ffn_post, loss_target, m_meta_tokens, m_norm_mix_pre, m_w_in, m_conv_w, m_pool_w, m_pool_scale, m_w_out, m_norm_mix_post, m_norm_ffn_pre, m_w_gate, m_w_up, m_w_down, m_norm_ffn_post, v_meta_tokens, v_norm_mix_pre, v_w_in, v_conv_w, v_pool_w, v_pool_scale, v_w_out, v_norm_mix_post, v_norm_ffn_pre, v_w_gate, v_w_up, v_w_down, v_norm_ffn_post):
    n_seq, seq, d = x.shape
    x2d = x.reshape(n_seq * seq, d)
    target = loss_target.reshape(n_seq * seq, d)

    t_ = lambda a: jnp.swapaxes(a[0], 0, 1)
    pw, ps = pool_w[0], pool_scale

    (h1, z, m, pooled, mixed), (win_b, wout_b, meta, conv, a_meta, z_meta), ffn_slabs = _gather_and_mixer_forward(
        x2d, [w_in[0], w_out[0], meta_tokens, conv_w[0]], [t_(w_gate), t_(w_up), w_down[0]], norm_mix_pre, pw, ps, norm_mix_post, n_seq)
    wg_b, wu_b, wd_b = (s.reshape(-1, d) for s in ffn_slabs)
    f, act, dd, dgate, dup, dh1, loss_sum, dg3, dg4 = _ffn_forward_backward(h1, target, norm_ffn_pre, wg_b, wu_b, wd_b, norm_ffn_post)
    ffn_grads, landing = _ffn_weight_grads(f, dd, dgate, dup, act)
    (gx, dwin, dwout, dg1, dg2, dconv, dpw, dps, dmeta), ffn_parts = _mixer_backward(
        x2d, dh1, m, z, pooled, mixed, meta, a_meta, z_meta, norm_mix_pre, win_b, conv, pw, ps, wout_b, norm_mix_post, n_seq,
        ffn_grads, landing)

    dmeta_s = jnp.transpose(dmeta.reshape(N_META, N_DEV, -1), (1, 0, 2))
    dconv_s = jnp.transpose(dconv.reshape(CONV_WIDTH, N_DEV, -1), (1, 0, 2))
    _, (win_parts, wout_parts), last = _reduce_adamw_carrying(
        [], [], [], [], [dwin, dwout.reshape(N_DEV, -1, d)], [dmeta_s, dconv_s, dg1, dg2, dg3, dg4, dpw.astype(BF16), dps, loss_sum],
        [False] * 2 + [True] * 7, "exchange_rest")
    ffn_res, _, _ = _reduce_adamw_carrying(
        ffn_parts, [t_(w_gate), t_(w_up), w_down[0]], [t_(m_w_gate), t_(m_w_up), m_w_down[0]], [t_(v_w_gate), t_(v_w_up), v_w_down[0]],
        [], [], [], "adamw_ffn")
    replicated = last[2:8]

    names = ["meta_tokens", "norm_mix_pre", "w_in", "conv_w", "pool_w", "pool_scale", "w_out", "norm_mix_post", "norm_ffn_pre", "w_gate",
             "w_up", "w_down", "norm_ffn_post"]
    res = {"w_gate": tuple(jnp.swapaxes(o, 0, 1)[None] for o in ffn_res[0]),
           "w_up": tuple(jnp.swapaxes(o, 0, 1)[None] for o in ffn_res[1]), "w_down": tuple(o[None] for o in ffn_res[2])}
    rest_names = ["w_in", "w_out", "meta_tokens", "conv_w", "norm_mix_pre", "norm_mix_post", "norm_ffn_pre", "norm_ffn_post", "pool_w",
                  "pool_scale"]
    rest_res, loss = _reduce_adamw_small(
        [win_parts, wout_parts, last[0], last[1], *replicated],
        [w_in[0], w_out[0], meta_tokens, conv_w[0], norm_mix_pre, norm_mix_post, norm_ffn_pre, norm_ffn_post, pool_w[0], pool_scale],
        [m_w_in[0], m_w_out[0], m_meta_tokens, m_conv_w[0], m_norm_mix_pre, m_norm_mix_post, m_norm_ffn_pre, m_norm_ffn_post,
         m_pool_w[0], m_pool_scale],
        [v_w_in[0], v_w_out[0], v_meta_tokens, v_conv_w[0], v_norm_mix_pre, v_norm_mix_post, v_norm_ffn_pre, v_norm_ffn_post,
         v_pool_w[0], v_pool_scale], last[8])
    for nm, r in zip(rest_names, rest_res):
        res[nm] = tuple(o[None] for o in r) if nm in ("w_in", "w_out", "conv_w", "pool_w") else r

    return (loss[0, 0], gx.reshape(n_seq, seq, d), *[res[nm][0] for nm in names], *[res[nm][1] for nm in names],
            *[res[nm][2] for nm in names], *[res[nm][3] for nm in names])
```
